```python
import math
import jax
import jax.numpy as jnp
from jax import lax
import numpy as np

D_MODEL = 1024
BATCH = 16
SEQ = 2048
DEPTH = 2

N_EVEN = (DEPTH + 1) // 2
N_ODD = DEPTH // 2
SB_HEADS = 8
SB_HEAD_DIM = 64
SB_WIDTH = SB_HEADS * SB_HEAD_DIM
QUERY_BLOCK = 128
POOL_WINDOWS = (2, 4, 8, 16)
POOL_WIDTH = D_MODEL - SB_WIDTH
POOL_GROUP = POOL_WIDTH // len(POOL_WINDOWS)
AB_IN_WIDTH = 3 * SB_WIDTH + POOL_WIDTH
SSM_WIDTH = D_MODEL
SSM_GROUP = 16
SSM_GROUPS = SSM_WIDTH // SSM_GROUP
SSM_STATE = 64
DT_MIN = 1e-3
DT_MAX = 1e-1
MEM_LEN = 256
XA_HEADS = 4
XA_HEAD_DIM = D_MODEL // XA_HEADS
D_FF = 2816
CONV_WIDTH = 3
EPS = 1e-6

kernel_name = "hybrid_stickbreak_pool_s5_block"


def rmsnorm(x, g):
    xf = x.astype(jnp.float32)
    xf = xf * lax.rsqrt(jnp.mean(xf * xf, axis=-1, keepdims=True) + EPS)
    return (xf * g.astype(jnp.float32)).astype(x.dtype)


def stick_breaking_attention(q, k, v):
    seq = q.shape[1]
    scale = q.shape[-1] ** -0.5
    outs = []
    for t0 in range(0, seq, QUERY_BLOCK):
        t1 = t0 + QUERY_BLOCK
        z = jnp.einsum('bqhd,bkhd->bhqk', q[:, t0:t1], k[:, :t1]).astype(jnp.float32) * scale
        causal = jnp.arange(t1)[None, :] < (t0 + jnp.arange(QUERY_BLOCK))[:, None]
        log_beta = jax.nn.log_sigmoid(z)
        log_keep = jnp.where(causal, log_beta - z, 0.0)
        after = lax.cumsum(log_keep, axis=3, reverse=True) - log_keep
        w = jnp.where(causal, jnp.exp(log_beta + after), 0.0)
        outs.append(jnp.einsum('bhqk,bkhd->bqhd', w.astype(v.dtype), v[:, :t1]))
    return jnp.concatenate(outs, axis=1)


def multiscale_pool(u, w_grp, scale):
    bsz, seq, _ = u.shape
    ug = u.astype(jnp.float32).reshape(bsz, seq, len(POOL_WINDOWS), POOL_GROUP)
    cs = jnp.concatenate([jnp.zeros_like(ug[:, :1]), jnp.cumsum(ug, axis=1)], axis=1)
    t = jnp.arange(seq)
    pooled = []
    for g, win in enumerate(POOL_WINDOWS):
        cs_g = cs[:, :, g]
        lo = jnp.maximum(t + 1 - win, 0)
        cnt = jnp.minimum(t + 1, win).astype(jnp.float32)[None, :, None]
        mean = (cs_g[:, 1:] - cs_g[:, lo]) / cnt
        pooled.append(mean - ug[:, :, g])
    p = jnp.stack(pooled, axis=2)
    y = jnp.einsum('bsgc,gcd->bsgd', p, w_grp.astype(jnp.float32)).reshape(bsz, seq, POOL_WIDTH)
    return (y * scale.astype(jnp.float32)).astype(u.dtype)


def _complex_linear_combine(left, right):
    a1r, a1i, b1r, b1i = left
    a2r, a2i, b2r, b2i = right
    ar = a1r * a2r - a1i * a2i
    ai = a1r * a2i + a1i * a2r
    br = a2r * b1r - a2i * b1i + b2r
    bi = a2r * b1i + a2i * b1r + b2i
    return (ar, ai, br, bi)


def s5_ssm(u, lam_re, lam_im, log_dt, b_re, b_im, c_re, c_im, d_skip):
    bsz, seq, _ = u.shape
    f32 = jnp.float32
    uf = u.astype(f32)
    ug = uf.reshape(bsz, seq, SSM_GROUPS, SSM_GROUP)
    lam_re = lam_re.astype(f32)
    lam_im = lam_im.astype(f32)
    dt = jnp.exp(log_dt.astype(f32))[:, None]
    mag = jnp.exp(lam_re * dt)
    ang = lam_im * dt
    lb_re = mag * jnp.cos(ang)
    lb_im = mag * jnp.sin(ang)
    n_re = lb_re - 1.0
    den = lam_re * lam_re + lam_im * lam_im
    coef_re = (n_re * lam_re + lb_im * lam_im) / den
    coef_im = (lb_im * lam_re - n_re * lam_im) / den
    b_re = b_re.astype(f32)
    b_im = b_im.astype(f32)
    bb_re = coef_re[..., None] * b_re - coef_im[..., None] * b_im
    bb_im = coef_re[..., None] * b_im + coef_im[..., None] * b_re
    bu_re = jnp.einsum('bsgc,gpc->bsgp', ug, bb_re)
    bu_im = jnp.einsum('bsgc,gpc->bsgp', ug, bb_im)
    a_re = jnp.broadcast_to(lb_re, (1, seq) + lb_re.shape)
    a_im = jnp.broadcast_to(lb_im, (1, seq) + lb_im.shape)
    _, _, h_re, h_im = lax.associative_scan(
        _complex_linear_combine, (a_re, a_im, bu_re, bu_im), axis=1)
    y = (jnp.einsum('bsgp,gcp->bsgc', h_re, c_re.astype(f32))
         - jnp.einsum('bsgp,gcp->bsgc', h_im, c_im.astype(f32)))
    return y.reshape(bsz, seq, SSM_WIDTH) + d_skip.astype(f32) * uf


def memory_cross_attention(h, mem_n, w_q, w_kv, w_o):
    bsz, seq, _ = h.shape
    m = mem_n.shape[1]
    q = (h @ w_q).reshape(bsz, seq, XA_HEADS, XA_HEAD_DIM)
    k, v = jnp.split(mem_n @ w_kv, 2, axis=-1)
    k = k.reshape(bsz, m, XA_HEADS, XA_HEAD_DIM)
    v = v.reshape(bsz, m, XA_HEADS, XA_HEAD_DIM)
    scores = jnp.einsum('bshd,bmhd->bhsm', q, k).astype(jnp.float32) * (XA_HEAD_DIM ** -0.5)
    p = jax.nn.softmax(scores, axis=-1).astype(v.dtype)
    o = jnp.einsum('bhsm,bmhd->bshd', p, v).reshape(bsz, seq, D_MODEL)
    return o @ w_o


def conv_gated_mlp(h, w_up, conv_w, conv_b, w_down):
    up = h @ w_up
    seq = up.shape[1]
    padded = jnp.pad(up, ((0, 0), (CONV_WIDTH - 1, 0), (0, 0)))
    conv = conv_b
    for i in range(CONV_WIDTH):
        conv = conv + conv_w[i] * padded[:, i:i + seq]
    val, gate = jnp.split(conv, 2, axis=-1)
    return (jax.nn.silu(gate) * val) @ w_down


def _fwd_setup_inputs(seed: int = 0) -> dict:
    key = jax.random.key(seed)
    ks = iter(jax.random.split(key, 40))

    def nrm(shape, fan_in):
        return jax.random.normal(next(ks), shape, jnp.float32) * (fan_in ** -0.5)

    def gain(shape):
        return 1.0 + 0.02 * jax.random.normal(next(ks), shape, jnp.float32)

    n_arange = jnp.arange(SSM_STATE, dtype=jnp.float32)
    return {
        "x": jax.random.normal(next(ks), (BATCH, SEQ, D_MODEL), jnp.float32),
        "mem": jax.random.normal(next(ks), (BATCH, MEM_LEN, D_MODEL), jnp.float32),
        "norm_mix": gain((DEPTH, D_MODEL)),
        "norm_xattn": gain((DEPTH, D_MODEL)),
        "norm_ffn": gain((DEPTH, D_MODEL)),
        "norm_mem": gain((D_MODEL,)),
        "norm_final": gain((D_MODEL,)),
        "ab_w_in": nrm((N_EVEN, D_MODEL, AB_IN_WIDTH), D_MODEL),
        "pool_w": nrm((N_EVEN, len(POOL_WINDOWS), POOL_GROUP, POOL_GROUP), POOL_GROUP),
        "pool_scale": gain((N_EVEN, POOL_WIDTH)),
        "ab_w_out": nrm((N_EVEN, SB_WIDTH + POOL_WIDTH, D_MODEL), SB_WIDTH + POOL_WIDTH),
        "ssm_w_in": nrm((N_ODD, D_MODEL, SSM_WIDTH), D_MODEL),
        "ssm_lam_re": -0.5 + 0.01 * jax.random.normal(next(ks), (N_ODD, SSM_GROUPS, SSM_STATE), jnp.float32),
        "ssm_lam_im": math.pi * n_arange + 0.01 * jax.random.normal(next(ks), (N_ODD, SSM_GROUPS, SSM_STATE), jnp.float32),
        "ssm_log_dt": jax.random.uniform(next(ks), (N_ODD, SSM_GROUPS), jnp.float32,
                                         math.log(DT_MIN), math.log(DT_MAX)),
        "ssm_b_re": nrm((N_ODD, SSM_GROUPS, SSM_STATE, SSM_GROUP), 2 * SSM_GROUP),
        "ssm_b_im": nrm((N_ODD, SSM_GROUPS, SSM_STATE, SSM_GROUP), 2 * SSM_GROUP),
        "ssm_c_re": nrm((N_ODD, SSM_GROUPS, SSM_GROUP, SSM_STATE), SSM_STATE),
        "ssm_c_im": nrm((N_ODD, SSM_GROUPS, SSM_GROUP, SSM_STATE), SSM_STATE),
        "ssm_d": jax.random.normal(next(ks), (N_ODD, SSM_WIDTH), jnp.float32),
        "ssm_w_glu": nrm((N_ODD, SSM_WIDTH, 2 * D_MODEL), SSM_WIDTH),
        "xa_w_q": nrm((DEPTH, D_MODEL, D_MODEL), D_MODEL),
        "xa_w_kv": nrm((DEPTH, D_MODEL, 2 * D_MODEL), D_MODEL),
        "xa_w_o": nrm((DEPTH, D_MODEL, D_MODEL), D_MODEL),
        "ffn_w_up": nrm((DEPTH, D_MODEL, 2 * D_FF), D_MODEL),
        "ffn_conv_w": nrm((DEPTH, CONV_WIDTH, 2 * D_FF), CONV_WIDTH),
        "ffn_conv_b": 0.01 * jax.random.normal(next(ks), (DEPTH, 2 * D_FF), jnp.float32),
        "ffn_w_down": nrm((DEPTH, D_FF, D_MODEL), D_FF),
    }


def _fwd_reference(x, mem, norm_mix, norm_xattn, norm_ffn, norm_mem, norm_final,
              ab_w_in, pool_w, pool_scale, ab_w_out,
              ssm_w_in, ssm_lam_re, ssm_lam_im, ssm_log_dt, ssm_b_re, ssm_b_im,
              ssm_c_re, ssm_c_im, ssm_d, ssm_w_glu,
              xa_w_q, xa_w_kv, xa_w_o,
              ffn_w_up, ffn_conv_w, ffn_conv_b, ffn_w_down):
    bsz, seq, _ = x.shape
    mem_n = rmsnorm(mem, norm_mem)
    for layer in range(DEPTH):
        h = rmsnorm(x, norm_mix[layer])
        if layer % 2 == 0:
            e = layer // 2
            proj = h @ ab_w_in[e]
            q, k, v, u = jnp.split(proj, [SB_WIDTH, 2 * SB_WIDTH, 3 * SB_WIDTH], axis=-1)
            q = q.reshape(bsz, seq, SB_HEADS, SB_HEAD_DIM)
            k = k.reshape(bsz, seq, SB_HEADS, SB_HEAD_DIM)
            v = v.reshape(bsz, seq, SB_HEADS, SB_HEAD_DIM)
            a_out = stick_breaking_attention(q, k, v).reshape(bsz, seq, SB_WIDTH)
            p_out = multiscale_pool(u, pool_w[e], pool_scale[e])
            mix = jnp.concatenate([a_out, p_out], axis=-1) @ ab_w_out[e]
        else:
            o = layer // 2
            u = h @ ssm_w_in[o]
            y = s5_ssm(u, ssm_lam_re[o], ssm_lam_im[o], ssm_log_dt[o], ssm_b_re[o],
                       ssm_b_im[o], ssm_c_re[o], ssm_c_im[o], ssm_d[o])
            glu = jax.nn.gelu(y).astype(x.dtype) @ ssm_w_glu[o]
            val, gate = jnp.split(glu, 2, axis=-1)
            mix = val * jax.nn.sigmoid(gate)
        x = x + mix
        x = x + memory_cross_attention(rmsnorm(x, norm_xattn[layer]), mem_n,
                                       xa_w_q[layer], xa_w_kv[layer], xa_w_o[layer])
        x = x + conv_gated_mlp(rmsnorm(x, norm_ffn[layer]), ffn_w_up[layer],
                               ffn_conv_w[layer], ffn_conv_b[layer], ffn_w_down[layer])
    return rmsnorm(x, norm_final)


import jax as _jax
import jax.numpy as _jnp

TWIN_FORMAT = 'train_step'
FWD_PARAMS = ['x', 'mem', 'norm_mix', 'norm_xattn', 'norm_ffn', 'norm_mem', 'norm_final', 'ab_w_in', 'pool_w', 'pool_scale', 'ab_w_out', 'ssm_w_in', 'ssm_lam_re', 'ssm_lam_im', 'ssm_log_dt', 'ssm_b_re', 'ssm_b_im', 'ssm_c_re', 'ssm_c_im', 'ssm_d', 'ssm_w_glu', 'xa_w_q', 'xa_w_kv', 'xa_w_o', 'ffn_w_up', 'ffn_conv_w', 'ffn_conv_b', 'ffn_w_down']
TWIN_WEIGHTS = ['norm_mix', 'norm_xattn', 'norm_ffn', 'norm_mem', 'norm_final', 'ab_w_in', 'pool_w', 'pool_scale', 'ab_w_out', 'ssm_w_in', 'ssm_lam_re', 'ssm_lam_im', 'ssm_log_dt', 'ssm_b_re', 'ssm_b_im', 'ssm_c_re', 'ssm_c_im', 'ssm_d', 'ssm_w_glu', 'xa_w_q', 'xa_w_kv', 'xa_w_o', 'ffn_w_up', 'ffn_conv_w', 'ffn_conv_b', 'ffn_w_down']
TWIN_DIFF_INPUT = 'x'
TWIN_INPUTS = ['x', 'mem', 'norm_mix', 'norm_xattn', 'norm_ffn', 'norm_mem', 'norm_final', 'ab_w_in', 'pool_w', 'pool_scale', 'ab_w_out', 'ssm_w_in', 'ssm_lam_re', 'ssm_lam_im', 'ssm_log_dt', 'ssm_b_re', 'ssm_b_im', 'ssm_c_re', 'ssm_c_im', 'ssm_d', 'ssm_w_glu', 'xa_w_q', 'xa_w_kv', 'xa_w_o', 'ffn_w_up', 'ffn_conv_w', 'ffn_conv_b', 'ffn_w_down', 'loss_target', 'm_norm_mix', 'm_norm_xattn', 'm_norm_ffn', 'm_norm_mem', 'm_norm_final', 'm_ab_w_in', 'm_pool_w', 'm_pool_scale', 'm_ab_w_out', 'm_ssm_w_in', 'm_ssm_lam_re', 'm_ssm_lam_im', 'm_ssm_log_dt', 'm_ssm_b_re', 'm_ssm_b_im', 'm_ssm_c_re', 'm_ssm_c_im', 'm_ssm_d', 'm_ssm_w_glu', 'm_xa_w_q', 'm_xa_w_kv', 'm_xa_w_o', 'm_ffn_w_up', 'm_ffn_conv_w', 'm_ffn_conv_b', 'm_ffn_w_down', 'v_norm_mix', 'v_norm_xattn', 'v_norm_ffn', 'v_norm_mem', 'v_norm_final', 'v_ab_w_in', 'v_pool_w', 'v_pool_scale', 'v_ab_w_out', 'v_ssm_w_in', 'v_ssm_lam_re', 'v_ssm_lam_im', 'v_ssm_log_dt', 'v_ssm_b_re', 'v_ssm_b_im', 'v_ssm_c_re', 'v_ssm_c_im', 'v_ssm_d', 'v_ssm_w_glu', 'v_xa_w_q', 'v_xa_w_kv', 'v_xa_w_o', 'v_ffn_w_up', 'v_ffn_conv_w', 'v_ffn_conv_b', 'v_ffn_w_down']
TWIN_OUTPUTS = ['loss', 'grad_x', 'grad_norm_mix', 'grad_norm_xattn', 'grad_norm_ffn', 'grad_norm_mem', 'grad_norm_final', 'grad_ab_w_in', 'grad_pool_w', 'grad_pool_scale', 'grad_ab_w_out', 'grad_ssm_w_in', 'grad_ssm_lam_re', 'grad_ssm_lam_im', 'grad_ssm_log_dt', 'grad_ssm_b_re', 'grad_ssm_b_im', 'grad_ssm_c_re', 'grad_ssm_c_im', 'grad_ssm_d', 'grad_ssm_w_glu', 'grad_xa_w_q', 'grad_xa_w_kv', 'grad_xa_w_o', 'grad_ffn_w_up', 'grad_ffn_conv_w', 'grad_ffn_conv_b', 'grad_ffn_w_down', 'delta_norm_mix', 'delta_norm_xattn', 'delta_norm_ffn', 'delta_norm_mem', 'delta_norm_final', 'delta_ab_w_in', 'delta_pool_w', 'delta_pool_scale', 'delta_ab_w_out', 'delta_ssm_w_in', 'delta_ssm_lam_re', 'delta_ssm_lam_im', 'delta_ssm_log_dt', 'delta_ssm_b_re', 'delta_ssm_b_im', 'delta_ssm_c_re', 'delta_ssm_c_im', 'delta_ssm_d', 'delta_ssm_w_glu', 'delta_xa_w_q', 'delta_xa_w_kv', 'delta_xa_w_o', 'delta_ffn_w_up', 'delta_ffn_conv_w', 'delta_ffn_conv_b', 'delta_ffn_w_down', 'new_m_norm_mix', 'new_m_norm_xattn', 'new_m_norm_ffn', 'new_m_norm_mem', 'new_m_norm_final', 'new_m_ab_w_in', 'new_m_pool_w', 'new_m_pool_scale', 'new_m_ab_w_out', 'new_m_ssm_w_in', 'new_m_ssm_lam_re', 'new_m_ssm_lam_im', 'new_m_ssm_log_dt', 'new_m_ssm_b_re', 'new_m_ssm_b_im', 'new_m_ssm_c_re', 'new_m_ssm_c_im', 'new_m_ssm_d', 'new_m_ssm_w_glu', 'new_m_xa_w_q', 'new_m_xa_w_kv', 'new_m_xa_w_o', 'new_m_ffn_w_up', 'new_m_ffn_conv_w', 'new_m_ffn_conv_b', 'new_m_ffn_w_down', 'new_v_norm_mix', 'new_v_norm_xattn', 'new_v_norm_ffn', 'new_v_norm_mem', 'new_v_norm_final', 'new_v_ab_w_in', 'new_v_pool_w', 'new_v_pool_scale', 'new_v_ab_w_out', 'new_v_ssm_w_in', 'new_v_ssm_lam_re', 'new_v_ssm_lam_im', 'new_v_ssm_log_dt', 'new_v_ssm_b_re', 'new_v_ssm_b_im', 'new_v_ssm_c_re', 'new_v_ssm_c_im', 'new_v_ssm_d', 'new_v_ssm_w_glu', 'new_v_xa_w_q', 'new_v_xa_w_kv', 'new_v_xa_w_o', 'new_v_ffn_w_up', 'new_v_ffn_conv_w', 'new_v_ffn_conv_b', 'new_v_ffn_w_down']
TWIN_LEAF_KINDS = {'loss': 'loss', 'grad_x': 'grad_x', 'grad_norm_mix': 'grad_w', 'grad_norm_xattn': 'grad_w', 'grad_norm_ffn': 'grad_w', 'grad_norm_mem': 'grad_w', 'grad_norm_final': 'grad_w', 'grad_ab_w_in': 'grad_w', 'grad_pool_w': 'grad_w', 'grad_pool_scale': 'grad_w', 'grad_ab_w_out': 'grad_w', 'grad_ssm_w_in': 'grad_w', 'grad_ssm_lam_re': 'grad_w', 'grad_ssm_lam_im': 'grad_w', 'grad_ssm_log_dt': 'grad_w', 'grad_ssm_b_re': 'grad_w', 'grad_ssm_b_im': 'grad_w', 'grad_ssm_c_re': 'grad_w', 'grad_ssm_c_im': 'grad_w', 'grad_ssm_d': 'grad_w', 'grad_ssm_w_glu': 'grad_w', 'grad_xa_w_q': 'grad_w', 'grad_xa_w_kv': 'grad_w', 'grad_xa_w_o': 'grad_w', 'grad_ffn_w_up': 'grad_w', 'grad_ffn_conv_w': 'grad_w', 'grad_ffn_conv_b': 'grad_w', 'grad_ffn_w_down': 'grad_w', 'delta_norm_mix': 'delta_w', 'delta_norm_xattn': 'delta_w', 'delta_norm_ffn': 'delta_w', 'delta_norm_mem': 'delta_w', 'delta_norm_final': 'delta_w', 'delta_ab_w_in': 'delta_w', 'delta_pool_w': 'delta_w', 'delta_pool_scale': 'delta_w', 'delta_ab_w_out': 'delta_w', 'delta_ssm_w_in': 'delta_w', 'delta_ssm_lam_re': 'delta_w', 'delta_ssm_lam_im': 'delta_w', 'delta_ssm_log_dt': 'delta_w', 'delta_ssm_b_re': 'delta_w', 'delta_ssm_b_im': 'delta_w', 'delta_ssm_c_re': 'delta_w', 'delta_ssm_c_im': 'delta_w', 'delta_ssm_d': 'delta_w', 'delta_ssm_w_glu': 'delta_w', 'delta_xa_w_q': 'delta_w', 'delta_xa_w_kv': 'delta_w', 'delta_xa_w_o': 'delta_w', 'delta_ffn_w_up': 'delta_w', 'delta_ffn_conv_w': 'delta_w', 'delta_ffn_conv_b': 'delta_w', 'delta_ffn_w_down': 'delta_w', 'new_m_norm_mix': 'new_m', 'new_m_norm_xattn': 'new_m', 'new_m_norm_ffn': 'new_m', 'new_m_norm_mem': 'new_m', 'new_m_norm_final': 'new_m', 'new_m_ab_w_in': 'new_m', 'new_m_pool_w': 'new_m', 'new_m_pool_scale': 'new_m', 'new_m_ab_w_out': 'new_m', 'new_m_ssm_w_in': 'new_m', 'new_m_ssm_lam_re': 'new_m', 'new_m_ssm_lam_im': 'new_m', 'new_m_ssm_log_dt': 'new_m', 'new_m_ssm_b_re': 'new_m', 'new_m_ssm_b_im': 'new_m', 'new_m_ssm_c_re': 'new_m', 'new_m_ssm_c_im': 'new_m', 'new_m_ssm_d': 'new_m', 'new_m_ssm_w_glu': 'new_m', 'new_m_xa_w_q': 'new_m', 'new_m_xa_w_kv': 'new_m', 'new_m_xa_w_o': 'new_m', 'new_m_ffn_w_up': 'new_m', 'new_m_ffn_conv_w': 'new_m', 'new_m_ffn_conv_b': 'new_m', 'new_m_ffn_w_down': 'new_m', 'new_v_norm_mix': 'new_v', 'new_v_norm_xattn': 'new_v', 'new_v_norm_ffn': 'new_v', 'new_v_norm_mem': 'new_v', 'new_v_norm_final': 'new_v', 'new_v_ab_w_in': 'new_v', 'new_v_pool_w': 'new_v', 'new_v_pool_scale': 'new_v', 'new_v_ab_w_out': 'new_v', 'new_v_ssm_w_in': 'new_v', 'new_v_ssm_lam_re': 'new_v', 'new_v_ssm_lam_im': 'new_v', 'new_v_ssm_log_dt': 'new_v', 'new_v_ssm_b_re': 'new_v', 'new_v_ssm_b_im': 'new_v', 'new_v_ssm_c_re': 'new_v', 'new_v_ssm_c_im': 'new_v', 'new_v_ssm_d': 'new_v', 'new_v_ssm_w_glu': 'new_v', 'new_v_xa_w_q': 'new_v', 'new_v_xa_w_kv': 'new_v', 'new_v_xa_w_o': 'new_v', 'new_v_ffn_w_up': 'new_v', 'new_v_ffn_conv_w': 'new_v', 'new_v_ffn_conv_b': 'new_v', 'new_v_ffn_w_down': 'new_v'}


def _forward(args):
    return _fwd_reference(*[args[k] for k in FWD_PARAMS])


def _output_shape():
    out = _jax.eval_shape(lambda: _forward(_fwd_setup_inputs(0)))
    return out.shape, out.dtype

N_MICROBATCH = 1
ADAM_LR = 0.001
ADAM_B1 = 0.9
ADAM_B2 = 0.999
ADAM_EPS = 1e-08
ADAM_WD = 0.01
ADAM_STEP = 10
PER_EXAMPLE_BATCH_AXIS = {'x': 0, 'mem': 0, 'loss_target': 0}
SHARED_INPUTS = []
_WEIGHT_DTYPES = {'norm_mix': _jnp.float32, 'norm_xattn': _jnp.float32, 'norm_ffn': _jnp.float32, 'norm_mem': _jnp.float32, 'norm_final': _jnp.float32, 'ab_w_in': _jnp.float32, 'pool_w': _jnp.float32, 'pool_scale': _jnp.float32, 'ab_w_out': _jnp.float32, 'ssm_w_in': _jnp.float32, 'ssm_lam_re': _jnp.float32, 'ssm_lam_im': _jnp.float32, 'ssm_log_dt': _jnp.float32, 'ssm_b_re': _jnp.float32, 'ssm_b_im': _jnp.float32, 'ssm_c_re': _jnp.float32, 'ssm_c_im': _jnp.float32, 'ssm_d': _jnp.float32, 'ssm_w_glu': _jnp.float32, 'xa_w_q': _jnp.float32, 'xa_w_kv': _jnp.float32, 'xa_w_o': _jnp.float32, 'ffn_w_up': _jnp.float32, 'ffn_conv_w': _jnp.float32, 'ffn_conv_b': _jnp.float32, 'ffn_w_down': _jnp.float32}
MOMENT_SCALE = {'norm_mix': 1.047043e-01, 'norm_xattn': 1.631145e-02, 'norm_ffn': 1.108837e-01, 'norm_mem': 3.476377e-02, 'norm_final': 3.202829e+01, 'ab_w_in': 9.749091e-02, 'pool_w': 1.482098e-01, 'pool_scale': 1.533344e-01, 'ab_w_out': 1.278624e-01, 'ssm_w_in': 4.922716e-02, 'ssm_lam_re': 4.107979e-03, 'ssm_lam_im': 4.559974e-03, 'ssm_log_dt': 3.068303e+00, 'ssm_b_re': 2.376380e-03, 'ssm_b_im': 2.431254e-03, 'ssm_c_re': 3.482770e-03, 'ssm_c_im': 3.504742e-03, 'ssm_d': 5.348340e-02, 'ssm_w_glu': 3.510460e-02, 'xa_w_q': 1.565830e-02, 'xa_w_kv': 1.585419e-02, 'xa_w_o': 1.614060e-02, 'ffn_w_up': 4.667446e-02, 'ffn_conv_w': 4.699019e-02, 'ffn_conv_b': 4.709293e-02, 'ffn_w_down': 7.602837e-02}


def _to_microbatches(a, axis):
    t = _jnp.moveaxis(a, axis, 0)
    t = t.reshape((N_MICROBATCH, t.shape[0] // N_MICROBATCH) + t.shape[1:])
    return _jnp.moveaxis(t, 1, axis + 1)


def setup_inputs(seed: int = 0) -> dict:
    inp = _fwd_setup_inputs(seed)
    key = _jax.random.fold_in(_jax.random.key(seed), 7919)
    shape, _ = _output_shape()
    out = dict(inp)
    out["loss_target"] = _jax.random.normal(_jax.random.fold_in(key, 0), shape, _jnp.float32)
    for i, name in enumerate(TWIN_WEIGHTS):
        w = inp[name].astype(_jnp.float32)
        if MOMENT_SCALE is None:
            s = _jnp.sqrt(_jnp.mean(_jnp.square(w)) + 1e-30)
        else:
            s = MOMENT_SCALE[name]
        km, kv = _jax.random.split(_jax.random.fold_in(key, i + 1))
        out[name] = w
        out["m_" + name] = s * _jax.random.normal(km, w.shape, _jnp.float32)
        out["v_" + name] = (s * s) * _jax.random.uniform(kv, w.shape, _jnp.float32, 0.5, 1.5)
    if N_MICROBATCH > 1:
        for name, axis in PER_EXAMPLE_BATCH_AXIS.items():
            out[name] = _to_microbatches(out[name], axis)
    return {'x': out['x'], 'mem': out['mem'], 'norm_mix': out['norm_mix'], 'norm_xattn': out['norm_xattn'], 'norm_ffn': out['norm_ffn'], 'norm_mem': out['norm_mem'], 'norm_final': out['norm_final'], 'ab_w_in': out['ab_w_in'], 'pool_w': out['pool_w'], 'pool_scale': out['pool_scale'], 'ab_w_out': out['ab_w_out'], 'ssm_w_in': out['ssm_w_in'], 'ssm_lam_re': out['ssm_lam_re'], 'ssm_lam_im': out['ssm_lam_im'], 'ssm_log_dt': out['ssm_log_dt'], 'ssm_b_re': out['ssm_b_re'], 'ssm_b_im': out['ssm_b_im'], 'ssm_c_re': out['ssm_c_re'], 'ssm_c_im': out['ssm_c_im'], 'ssm_d': out['ssm_d'], 'ssm_w_glu': out['ssm_w_glu'], 'xa_w_q': out['xa_w_q'], 'xa_w_kv': out['xa_w_kv'], 'xa_w_o': out['xa_w_o'], 'ffn_w_up': out['ffn_w_up'], 'ffn_conv_w': out['ffn_conv_w'], 'ffn_conv_b': out['ffn_conv_b'], 'ffn_w_down': out['ffn_w_down'], 'loss_target': out['loss_target'], 'm_norm_mix': out['m_norm_mix'], 'm_norm_xattn': out['m_norm_xattn'], 'm_norm_ffn': out['m_norm_ffn'], 'm_norm_mem': out['m_norm_mem'], 'm_norm_final': out['m_norm_final'], 'm_ab_w_in': out['m_ab_w_in'], 'm_pool_w': out['m_pool_w'], 'm_pool_scale': out['m_pool_scale'], 'm_ab_w_out': out['m_ab_w_out'], 'm_ssm_w_in': out['m_ssm_w_in'], 'm_ssm_lam_re': out['m_ssm_lam_re'], 'm_ssm_lam_im': out['m_ssm_lam_im'], 'm_ssm_log_dt': out['m_ssm_log_dt'], 'm_ssm_b_re': out['m_ssm_b_re'], 'm_ssm_b_im': out['m_ssm_b_im'], 'm_ssm_c_re': out['m_ssm_c_re'], 'm_ssm_c_im': out['m_ssm_c_im'], 'm_ssm_d': out['m_ssm_d'], 'm_ssm_w_glu': out['m_ssm_w_glu'], 'm_xa_w_q': out['m_xa_w_q'], 'm_xa_w_kv': out['m_xa_w_kv'], 'm_xa_w_o': out['m_xa_w_o'], 'm_ffn_w_up': out['m_ffn_w_up'], 'm_ffn_conv_w': out['m_ffn_conv_w'], 'm_ffn_conv_b': out['m_ffn_conv_b'], 'm_ffn_w_down': out['m_ffn_w_down'], 'v_norm_mix': out['v_norm_mix'], 'v_norm_xattn': out['v_norm_xattn'], 'v_norm_ffn': out['v_norm_ffn'], 'v_norm_mem': out['v_norm_mem'], 'v_norm_final': out['v_norm_final'], 'v_ab_w_in': out['v_ab_w_in'], 'v_pool_w': out['v_pool_w'], 'v_pool_scale': out['v_pool_scale'], 'v_ab_w_out': out['v_ab_w_out'], 'v_ssm_w_in': out['v_ssm_w_in'], 'v_ssm_lam_re': out['v_ssm_lam_re'], 'v_ssm_lam_im': out['v_ssm_lam_im'], 'v_ssm_log_dt': out['v_ssm_log_dt'], 'v_ssm_b_re': out['v_ssm_b_re'], 'v_ssm_b_im': out['v_ssm_b_im'], 'v_ssm_c_re': out['v_ssm_c_re'], 'v_ssm_c_im': out['v_ssm_c_im'], 'v_ssm_d': out['v_ssm_d'], 'v_ssm_w_glu': out['v_ssm_w_glu'], 'v_xa_w_q': out['v_xa_w_q'], 'v_xa_w_kv': out['v_xa_w_kv'], 'v_xa_w_o': out['v_xa_w_o'], 'v_ffn_w_up': out['v_ffn_w_up'], 'v_ffn_conv_w': out['v_ffn_conv_w'], 'v_ffn_conv_b': out['v_ffn_conv_b'], 'v_ffn_w_down': out['v_ffn_w_down']}


def _loss(weights, diff, rest, loss_target):
    with _jax.named_scope("forward"):
        args = {**rest, TWIN_DIFF_INPUT: diff, **{k: w.astype(_WEIGHT_DTYPES[k]) for k, w in weights.items()}}
        y = _forward(args)
    with _jax.named_scope("loss_head"):
        err = _jnp.square(y.astype(_jnp.float32) - loss_target)
        return 0.5 * _jnp.sum(_jnp.mean(err, axis=-1)) if err.ndim else 0.5 * err


def _adamw(w, g, m, v):
    m = ADAM_B1 * m + (1.0 - ADAM_B1) * g
    v = ADAM_B2 * v + (1.0 - ADAM_B2) * _jnp.square(g)
    m_hat = m / (1.0 - ADAM_B1 ** ADAM_STEP)
    v_hat = v / (1.0 - ADAM_B2 ** ADAM_STEP)
    delta = -ADAM_LR * (m_hat / (_jnp.sqrt(v_hat) + ADAM_EPS) + ADAM_WD * w)
    return delta, m, v


def reference(x, mem, norm_mix, norm_xattn, norm_ffn, norm_mem, norm_final, ab_w_in, pool_w, pool_scale, ab_w_out, ssm_w_in, ssm_lam_re, ssm_lam_im, ssm_log_dt, ssm_b_re, ssm_b_im, ssm_c_re, ssm_c_im, ssm_d, ssm_w_glu, xa_w_q, xa_w_kv, xa_w_o, ffn_w_up, ffn_conv_w, ffn_conv_b, ffn_w_down, loss_target, m_norm_mix, m_norm_xattn, m_norm_ffn, m_norm_mem, m_norm_final, m_ab_w_in, m_pool_w, m_pool_scale, m_ab_w_out, m_ssm_w_in, m_ssm_lam_re, m_ssm_lam_im, m_ssm_log_dt, m_ssm_b_re, m_ssm_b_im, m_ssm_c_re, m_ssm_c_im, m_ssm_d, m_ssm_w_glu, m_xa_w_q, m_xa_w_kv, m_xa_w_o, m_ffn_w_up, m_ffn_conv_w, m_ffn_conv_b, m_ffn_w_down, v_norm_mix, v_norm_xattn, v_norm_ffn, v_norm_mem, v_norm_final, v_ab_w_in, v_pool_w, v_pool_scale, v_ab_w_out, v_ssm_w_in, v_ssm_lam_re, v_ssm_lam_im, v_ssm_log_dt, v_ssm_b_re, v_ssm_b_im, v_ssm_c_re, v_ssm_c_im, v_ssm_d, v_ssm_w_glu, v_xa_w_q, v_xa_w_kv, v_xa_w_o, v_ffn_w_up, v_ffn_conv_w, v_ffn_conv_b, v_ffn_w_down):
    given = dict(x=x, mem=mem, norm_mix=norm_mix, norm_xattn=norm_xattn, norm_ffn=norm_ffn, norm_mem=norm_mem, norm_final=norm_final, ab_w_in=ab_w_in, pool_w=pool_w, pool_scale=pool_scale, ab_w_out=ab_w_out, ssm_w_in=ssm_w_in, ssm_lam_re=ssm_lam_re, ssm_lam_im=ssm_lam_im, ssm_log_dt=ssm_log_dt, ssm_b_re=ssm_b_re, ssm_b_im=ssm_b_im, ssm_c_re=ssm_c_re, ssm_c_im=ssm_c_im, ssm_d=ssm_d, ssm_w_glu=ssm_w_glu, xa_w_q=xa_w_q, xa_w_kv=xa_w_kv, xa_w_o=xa_w_o, ffn_w_up=ffn_w_up, ffn_conv_w=ffn_conv_w, ffn_conv_b=ffn_conv_b, ffn_w_down=ffn_w_down, loss_target=loss_target, m_norm_mix=m_norm_mix, m_norm_xattn=m_norm_xattn, m_norm_ffn=m_norm_ffn, m_norm_mem=m_norm_mem, m_norm_final=m_norm_final, m_ab_w_in=m_ab_w_in, m_pool_w=m_pool_w, m_pool_scale=m_pool_scale, m_ab_w_out=m_ab_w_out, m_ssm_w_in=m_ssm_w_in, m_ssm_lam_re=m_ssm_lam_re, m_ssm_lam_im=m_ssm_lam_im, m_ssm_log_dt=m_ssm_log_dt, m_ssm_b_re=m_ssm_b_re, m_ssm_b_im=m_ssm_b_im, m_ssm_c_re=m_ssm_c_re, m_ssm_c_im=m_ssm_c_im, m_ssm_d=m_ssm_d, m_ssm_w_glu=m_ssm_w_glu, m_xa_w_q=m_xa_w_q, m_xa_w_kv=m_xa_w_kv, m_xa_w_o=m_xa_w_o, m_ffn_w_up=m_ffn_w_up, m_ffn_conv_w=m_ffn_conv_w, m_ffn_conv_b=m_ffn_conv_b, m_ffn_w_down=m_ffn_w_down, v_norm_mix=v_norm_mix, v_norm_xattn=v_norm_xattn, v_norm_ffn=v_norm_ffn, v_norm_mem=v_norm_mem, v_norm_final=v_norm_final, v_ab_w_in=v_ab_w_in, v_pool_w=v_pool_w, v_pool_scale=v_pool_scale, v_ab_w_out=v_ab_w_out, v_ssm_w_in=v_ssm_w_in, v_ssm_lam_re=v_ssm_lam_re, v_ssm_lam_im=v_ssm_lam_im, v_ssm_log_dt=v_ssm_log_dt, v_ssm_b_re=v_ssm_b_re, v_ssm_b_im=v_ssm_b_im, v_ssm_c_re=v_ssm_c_re, v_ssm_c_im=v_ssm_c_im, v_ssm_d=v_ssm_d, v_ssm_w_glu=v_ssm_w_glu, v_xa_w_q=v_xa_w_q, v_xa_w_kv=v_xa_w_kv, v_xa_w_o=v_xa_w_o, v_ffn_w_up=v_ffn_w_up, v_ffn_conv_w=v_ffn_conv_w, v_ffn_conv_b=v_ffn_conv_b, v_ffn_w_down=v_ffn_w_down)
    weights = {n: given[n] for n in TWIN_WEIGHTS}
    shared = {n: given[n] for n in SHARED_INPUTS}
    per_example = {n: given[n] for n in ['x', 'mem']}
    grad_fn = _jax.value_and_grad(_loss, argnums=(0, 1))

    def one_microbatch(ex, loss_target):
        ex = dict(ex)
        diff = ex.pop(TWIN_DIFF_INPUT)
        return grad_fn(weights, diff, {**shared, **ex}, loss_target)

    if N_MICROBATCH == 1:
        loss, (grad_w, grad_x) = one_microbatch(per_example, given["loss_target"])
    else:
        def body(carry, xs):
            loss_sum, grad_sum = carry
            l_k, (gw_k, gx_k) = one_microbatch(xs[0], xs[1])
            with _jax.named_scope("update"):
                return (loss_sum + l_k, _jax.tree.map(_jnp.add, grad_sum, gw_k)), gx_k

        init = (_jnp.zeros((), _jnp.float32), _jax.tree.map(_jnp.zeros_like, weights))
        (loss, grad_w), grad_x = _jax.lax.scan(body, init, (per_example, given["loss_target"]))
    with _jax.named_scope("update"):
        delta_w, new_m, new_v = {}, {}, {}
        for n in TWIN_WEIGHTS:
            delta_w[n], new_m[n], new_v[n] = _adamw(weights[n], grad_w[n], given["m_" + n], given["v_" + n])
    return (loss, grad_x, *[grad_w[n] for n in TWIN_WEIGHTS], *[delta_w[n] for n in TWIN_WEIGHTS],
            *[new_m[n] for n in TWIN_WEIGHTS], *[new_v[n] for n in TWIN_WEIGHTS])
```

```python
import functools
import math

import jax
import jax.numpy as jnp
from jax import lax
from jax.experimental import pallas as pl
from jax.experimental.pallas import tpu as pltpu

f32 = jnp.float32
bf16 = jnp.bfloat16
SDS = jax.ShapeDtypeStruct
MESH = pl.DeviceIdType.MESH
ANY = pl.BlockSpec(memory_space=pl.ANY)

SB_HEAD_DIM = 64
POOL_WINDOWS = (2, 4, 8, 16)
POOL_GROUP = 128
XA_HEADS = 4
SSM_GROUPS = 64
SSM_GROUP = 16
SSM_STATE = 64
EPS = 1e-6
ADAM_LR, ADAM_B1, ADAM_B2, ADAM_EPS, ADAM_WD, ADAM_STEP = 0.001, 0.9, 0.999, 1e-08, 0.01, 10

LANES = 128
SUBLANES = 8
N_CHIPS = 4
VMEM_LIMIT = 56 * 1024 * 1024

NN = ((1,), (0,))
NT = ((1,), (1,))
TN = ((0,), (0,))


def _dot(a, b, dims):
    return lax.dot_general(a, b, (dims, ((), ())), preferred_element_type=f32)


def _params(n_grid):
    return pltpu.CompilerParams(dimension_semantics=("arbitrary",) * n_grid, vmem_limit_bytes=VMEM_LIMIT)


def _sum8(x):
    r, n = x.shape
    return jnp.sum(x.reshape(r // SUBLANES, SUBLANES, n), axis=0)


def _split_bf16(x):
    hi = x.astype(bf16)
    lo = (x - hi.astype(f32)).astype(bf16)
    return hi, lo


def _sigmoid(x):
    return 1.0 / (1.0 + jnp.exp(-x))


def _mm(a, b, *, mode, name, out_dtype, bm=512, bn=512, bk=None, a_l=None, b_l=None, b_n0=0, n=None,
        res=None, out_l=None, out_layers=None, out_prev=None):
    dims = {"nn": NN, "nt": NT, "tn": TN}[mode]
    a2, b2 = a.shape[-2:], b.shape[-2:]
    if mode == "nn":
        (m, k), nfull = a2, b2[1]
    elif mode == "nt":
        (m, k), nfull = a2, b2[0]
    else:
        (k, m), nfull = a2, b2[1]
    n = nfull if n is None else n
    bm, bn = min(bm, m), min(bn, n)
    bk = k if bk is None else min(bk, k)
    assert m % bm == 0 and n % bn == 0 and k % bk == 0 and b_n0 % bn == 0, (name, m, n, k, bm, bn, bk)
    nk, n0b = k // bk, b_n0 // bn

    def with_layer(layer, blk, idx_fn):
        if layer is None:
            return pl.BlockSpec(blk, idx_fn)
        return pl.BlockSpec((None,) + blk, lambda i, j, kk: (layer,) + idx_fn(i, j, kk))

    if mode == "tn":
        a_spec = with_layer(a_l, (bk, bm), lambda i, j, kk: (kk, i))
    else:
        a_spec = with_layer(a_l, (bm, bk), lambda i, j, kk: (i, kk))
    if mode == "nt":
        b_spec = with_layer(b_l, (bn, bk), lambda i, j, kk: (j, kk))
    else:
        b_spec = with_layer(b_l, (bk, bn), lambda i, j, kk: (kk, j + n0b))
    o_spec = with_layer(out_l, (bm, bn), lambda i, j, kk: (i, j))
    ins, in_specs = [a, b], [a_spec, b_spec]
    if res is not None:
        ins.append(res)
        in_specs.append(pl.BlockSpec((bm, bn), lambda i, j, kk: (i, j)))
    aliases = {}
    if out_prev is not None:
        aliases = {len(ins): 0}
        ins.append(out_prev)
        in_specs.append(ANY)
    has_res, has_prev = res is not None, out_prev is not None

    def body(*refs):
        a_ref, b_ref = refs[0], refs[1]
        res_ref = refs[2] if has_res else None
        o_ref = refs[2 + has_res + has_prev]
        part = _dot(a_ref[...].astype(bf16), b_ref[...].astype(bf16), dims)

        def finish(r):
            if has_res:
                r = r + res_ref[...]
            o_ref[...] = r.astype(o_ref.dtype)

        if nk == 1:
            finish(part)
        else:
            acc_ref = refs[-1]
            kk = pl.program_id(2)

            @pl.when(kk == 0)
            def _():
                acc_ref[...] = part

            @pl.when(kk > 0)
            def _():
                acc_ref[...] += part

            @pl.when(kk == nk - 1)
            def _():
                finish(acc_ref[...])

    out_shape = SDS((m, n) if out_l is None else (out_layers, m, n), out_dtype)
    return pl.pallas_call(
        body, out_shape=out_shape, grid=(m // bm, n // bn, nk), in_specs=in_specs, out_specs=o_spec,
        scratch_shapes=[] if nk == 1 else [pltpu.VMEM((bm, bn), f32)],
        input_output_aliases=aliases, name=name, compiler_params=_params(3))(*ins)


def _rowwise(fn, row_ins, full_ins, row_outs, acc_outs, *, name, br=256):
    t = row_ins[0].shape[0]
    br = next(b for b in (br, 128, 64, 32, 16, 8, t) if b <= t and t % b == 0)
    nr, nf, no = len(row_ins), len(full_ins), len(row_outs)

    def body(*refs):
        rv = [r[...] for r in refs[:nr]]
        fv = [r[...] for r in refs[nr:nr + nf]]
        o_refs = refs[nr + nf:nr + nf + no]
        a_refs = refs[nr + nf + no:]
        outs, accs = fn(rv, fv)
        for o_ref, v in zip(o_refs, outs):
            o_ref[...] = v.astype(o_ref.dtype)
        if a_refs:
            i = pl.program_id(0)

            @pl.when(i == 0)
            def _():
                for a_ref, v in zip(a_refs, accs):
                    a_ref[...] = v

            @pl.when(i > 0)
            def _():
                for a_ref, v in zip(a_refs, accs):
                    a_ref[...] += v

    in_specs = [pl.BlockSpec((br, x.shape[1]), lambda i: (i, 0)) for x in row_ins]
    in_specs += [pl.BlockSpec(x.shape, lambda i, nd=x.ndim: (0,) * nd) for x in full_ins]
    out_specs = [pl.BlockSpec((br, s.shape[1]), lambda i: (i, 0)) for s in row_outs]
    out_specs += [pl.BlockSpec(s.shape, lambda i: (0, 0)) for s in acc_outs]
    res = pl.pallas_call(body, out_shape=tuple(row_outs) + tuple(acc_outs), grid=(t // br,), in_specs=in_specs,
                         out_specs=tuple(out_specs), name=name, compiler_params=_params(1))(*row_ins, *full_ins)
    return res


def _norm_fwd(x, g, name):
    def fn(rv, fv):
        (xv,), (gv,) = rv, fv
        r = lax.rsqrt(jnp.mean(xv * xv, axis=1, keepdims=True) + EPS)
        return [xv * r * gv], []
    return _rowwise(fn, [x], [g], [SDS(x.shape, bf16)], [], name=name)[0]


def _norm_bwd(dh, x, dres, g, name):
    d = x.shape[1]

    def fn(rv, fv):
        (dhv, xv, drv), (gv,) = rv, fv
        r = lax.rsqrt(jnp.mean(xv * xv, axis=1, keepdims=True) + EPS)
        xh = xv * r
        dxh = dhv * gv
        dx = drv + r * (dxh - xh * jnp.mean(dxh * xh, axis=1, keepdims=True))
        return [dx], [_sum8(dhv * xh)]
    return _rowwise(fn, [dh, x, dres], [g], [SDS(x.shape, f32)], [SDS((SUBLANES, d), f32)], name=name)


def _norm_bwd_gain_only(dh, x, name):
    d = x.shape[1]

    def fn(rv, fv):
        dhv, xv = rv
        r = lax.rsqrt(jnp.mean(xv * xv, axis=1, keepdims=True) + EPS)
        return [], [_sum8(dhv * xv * r)]
    return _rowwise(fn, [dh, x], [], [], [SDS((SUBLANES, d), f32)], name=name)[0]


def _loss_head(x, target, g, name):
    d = x.shape[1]

    def fn(rv, fv):
        (xv, tv), (gv,) = rv, fv
        r = lax.rsqrt(jnp.mean(xv * xv, axis=1, keepdims=True) + EPS)
        xh = xv * r
        err = xh * gv - tv
        dy = err * (1.0 / d)
        dxh = dy * gv
        dx = r * (dxh - xh * jnp.mean(dxh * xh, axis=1, keepdims=True))
        return [dx], [_sum8(dy * xh), _sum8(err * err)]
    return _rowwise(fn, [x, target], [g], [SDS(x.shape, f32)], [SDS((SUBLANES, d), f32), SDS((SUBLANES, d), f32)], name=name)


_GELU_C = math.sqrt(2.0 / math.pi)


def _gelu_fwd(y, name):
    def fn(rv, fv):
        (v,) = rv
        t = jnp.tanh(_GELU_C * (v + 0.044715 * v * v * v))
        return [0.5 * v * (1.0 + t)], []
    return _rowwise(fn, [y], [], [SDS(y.shape, bf16)], [], name=name)[0]


def _gelu_bwd(dg, y, name):
    def fn(rv, fv):
        dgv, v = rv
        t = jnp.tanh(_GELU_C * (v + 0.044715 * v * v * v))
        dt = (1.0 - t * t) * _GELU_C * (1.0 + 3.0 * 0.044715 * v * v)
        return [dgv * (0.5 * (1.0 + t) + 0.5 * v * dt)], []
    return _rowwise(fn, [dg, y], [], [SDS(y.shape, f32)], [], name=name)[0]


def _glu_fwd(glu, x, name):
    d = x.shape[1]

    def fn(rv, fv):
        gl, xv = rv
        return [xv + gl[:, :d] * _sigmoid(gl[:, d:])], []
    return _rowwise(fn, [glu, x], [], [SDS(x.shape, f32)], [], name=name)[0]


def _glu_bwd(dx, glu, name):
    d = dx.shape[1]

    def fn(rv, fv):
        dxv, gl = rv
        sg = _sigmoid(gl[:, d:])
        return [jnp.concatenate([dxv * sg, dxv * gl[:, :d] * sg * (1.0 - sg)], axis=1)], []
    return _rowwise(fn, [dx, glu], [], [SDS(glu.shape, bf16)], [], name=name)[0]


def _adamw(w, g, m, v, name):
    c1 = 1.0 - ADAM_B1 ** ADAM_STEP
    c2 = 1.0 - ADAM_B2 ** ADAM_STEP

    def fn(rv, fv):
        wv, gv, mv, vv = rv
        m2 = ADAM_B1 * mv + (1.0 - ADAM_B1) * gv
        v2 = ADAM_B2 * vv + (1.0 - ADAM_B2) * (gv * gv)
        delta = -ADAM_LR * ((m2 / c1) / (jnp.sqrt(v2 / c2) + ADAM_EPS) + ADAM_WD * wv)
        return [delta, m2, v2], []
    s = SDS(w.shape, f32)
    return _rowwise(fn, [w, g, m, v], [], [s, s, s], [], name=name)


SB_TQ = 128


def _sb_logits(qh, kb, i, j, t):
    z = _dot(qh, kb, NT) * (SB_HEAD_DIM ** -0.5)
    row = lax.broadcasted_iota(jnp.int32, (t, t), 0)
    col = lax.broadcasted_iota(jnp.int32, (t, t), 1)
    valid = (col + j * t) < (row + i * t)
    sp = jnp.log(1.0 + jnp.exp(-jnp.abs(z)))
    lb = jnp.minimum(z, 0.0) - sp
    lk_raw = jnp.minimum(-z, 0.0) - sp
    return lb, lk_raw, jnp.where(valid, lk_raw, 0.0), valid


def _tri(t, op):
    row = lax.broadcasted_iota(jnp.int32, (t, t), 0)
    col = lax.broadcasted_iota(jnp.int32, (t, t), 1)
    return jnp.where(op(row, col), 1.0, 0.0).astype(bf16)


def _dot_split(x, u):
    hi, lo = _split_bf16(x)
    return _dot(hi, u, NN) + _dot(lo, u, NN)


def _sb_fwd(qkv, seq, name):
    t_all, w3 = qkv.shape
    w = w3 // 3
    hp, tq = w // LANES, SB_TQ
    nb, nq = t_all // seq, seq // tq

    def body(q_ref, k_ref, v_ref, o_ref, lt_ref):
        i = pl.program_id(2)
        q = q_ref[...]
        lane = lax.broadcasted_iota(jnp.int32, (tq, LANES), 1)
        u_after = _tri(tq, lambda r, c: r > c)
        out = jnp.zeros((tq, LANES), f32)
        ltot = jnp.zeros((tq, LANES), f32)
        for hh in range(LANES // SB_HEAD_DIM):
            m = (lane >= hh * SB_HEAD_DIM) & (lane < (hh + 1) * SB_HEAD_DIM)
            qh = q * jnp.where(m, 1.0, 0.0).astype(bf16)

            def step(jj, carry, qh=qh):
                c, acc = carry
                j = i - jj
                off = pl.multiple_of(j * tq, tq)
                kb = k_ref[pl.ds(off, tq), :]
                vb = v_ref[pl.ds(off, tq), :]
                lb, _, lk, valid = _sb_logits(qh, kb, i, j, tq)
                aft = c + _dot_split(lk, u_after)
                wgt = jnp.where(valid, jnp.exp(lb + aft), 0.0)
                acc = acc + _dot(wgt.astype(bf16), vb, NN)
                return c + jnp.sum(lk, axis=1, keepdims=True), acc

            c, acc = lax.fori_loop(0, i + 1, step, (jnp.zeros((tq, 1), f32), jnp.zeros((tq, LANES), f32)))
            out = out + jnp.where(m, acc, 0.0)
            ltot = ltot + jnp.where(m, c, 0.0)
        o_ref[...] = out
        lt_ref[...] = ltot

    return pl.pallas_call(
        body, out_shape=(SDS((t_all, 2 * w), f32), SDS((t_all, w), f32)), grid=(nb, hp, nq),
        in_specs=[pl.BlockSpec((tq, LANES), lambda b, p, i: (b * nq + i, p)),
                  pl.BlockSpec((seq, LANES), lambda b, p, i: (b, hp + p)),
                  pl.BlockSpec((seq, LANES), lambda b, p, i: (b, 2 * hp + p))],
        out_specs=(pl.BlockSpec((tq, LANES), lambda b, p, i: (b * nq + i, p)),
                   pl.BlockSpec((tq, LANES), lambda b, p, i: (b * nq + i, p))),
        name=name, compiler_params=_params(3))(qkv, qkv, qkv)


def _sb_bwd(qkv, ltot, dmix, seq, name):
    t_all, w3 = qkv.shape
    w = w3 // 3
    hp, tq = w // LANES, SB_TQ
    nb, nq = t_all // seq, seq // tq

    def body(q_ref, k_ref, v_ref, lt_ref, do_ref, dq_ref, dk_ref, dv_ref, dk_acc, dv_acc):
        i = pl.program_id(2)

        @pl.when(i == 0)
        def _():
            dk_acc[...] = jnp.zeros_like(dk_acc)
            dv_acc[...] = jnp.zeros_like(dv_acc)

        q = q_ref[...]
        do = do_ref[...]
        lane = lax.broadcasted_iota(jnp.int32, (tq, LANES), 1)
        u_incl = _tri(tq, lambda r, c: r <= c)
        u_excl = _tri(tq, lambda r, c: r < c)
        dq_all = jnp.zeros((tq, LANES), f32)
        for hh in range(LANES // SB_HEAD_DIM):
            m = (lane >= hh * SB_HEAD_DIM) & (lane < (hh + 1) * SB_HEAD_DIM)
            qh = q * jnp.where(m, 1.0, 0.0).astype(bf16)
            doh = jnp.where(m, do, 0.0).astype(bf16)
            lt = jnp.sum(jnp.where(m, lt_ref[...], 0.0), axis=1, keepdims=True) * (1.0 / SB_HEAD_DIM)

            def step(j, carry, qh=qh, doh=doh, lt=lt):
                cp, cg, dq = carry
                off = pl.multiple_of(j * tq, tq)
                kb = k_ref[pl.ds(off, tq), :]
                vb = v_ref[pl.ds(off, tq), :]
                lb, lk_raw, lk, valid = _sb_logits(qh, kb, i, j, tq)
                aft = lt - (cp + _dot_split(lk, u_incl))
                wgt = jnp.where(valid, jnp.exp(lb + aft), 0.0)
                g = _dot(doh, vb, NT) * wgt
                gpre = cg + _dot_split(g, u_excl)
                dz = jnp.where(valid, g * jnp.exp(lk_raw) - gpre * jnp.exp(lb), 0.0) * (SB_HEAD_DIM ** -0.5)
                dzb = dz.astype(bf16)
                dk_acc[pl.ds(off, tq), :] += _dot(dzb, qh, TN)
                dv_acc[pl.ds(off, tq), :] += _dot(wgt.astype(bf16), doh, TN)
                return (cp + jnp.sum(lk, axis=1, keepdims=True), cg + jnp.sum(g, axis=1, keepdims=True),
                        dq + _dot(dzb, kb, NN))

            zero1 = jnp.zeros((tq, 1), f32)
            _, _, dq = lax.fori_loop(0, i + 1, step, (zero1, zero1, jnp.zeros((tq, LANES), f32)))
            dq_all = dq_all + jnp.where(m, dq, 0.0)
        dq_ref[...] = dq_all.astype(bf16)

        @pl.when(i == nq - 1)
        def _():
            dk_ref[...] = dk_acc[...].astype(bf16)
            dv_ref[...] = dv_acc[...].astype(bf16)

    row_blk = pl.BlockSpec((tq, LANES), lambda b, p, i: (b * nq + i, p))
    seq_blk = pl.BlockSpec((seq, LANES), lambda b, p, i: (b, p))
    out = SDS((t_all, w), bf16)
    return pl.pallas_call(
        body, out_shape=(out, out, out), grid=(nb, hp, nq),
        in_specs=[row_blk,
                  pl.BlockSpec((seq, LANES), lambda b, p, i: (b, hp + p)),
                  pl.BlockSpec((seq, LANES), lambda b, p, i: (b, 2 * hp + p)),
                  row_blk, row_blk],
        out_specs=(row_blk, seq_blk, seq_blk),
        scratch_shapes=[pltpu.VMEM((seq, LANES), f32), pltpu.VMEM((seq, LANES), f32)],
        name=name, compiler_params=_params(3))(qkv, qkv, qkv, ltot, dmix)


POOL_CHUNK = 256
POOL_HALO = 16


def _band(rows, cols, lo, hi):
    r = lax.broadcasted_iota(jnp.int32, (rows, cols), 0)
    c = lax.broadcasted_iota(jnp.int32, (rows, cols), 1)
    d = c - r
    return jnp.where((d >= lo) & (d < hi), 1.0, 0.0).astype(bf16)


def _pool_counts(r0, rows, win):
    t = lax.broadcasted_iota(jnp.int32, (rows, 1), 0) + r0
    return jnp.minimum(t + 1, win).astype(f32)


def _pool_fwd(u, mix, pool_w, scale, seq, name):
    t_all, w = u.shape
    ng, rc = w // POOL_GROUP, min(POOL_CHUNK, seq)

    def body(u_ref, w_ref, s_ref, mix_in, p_ref, o_ref, pad):
        del mix_in
        pad[0:POOL_HALO, :] = jnp.zeros((POOL_HALO, POOL_GROUP), f32)
        for g in range(ng):
            cols = slice(g * POOL_GROUP, (g + 1) * POOL_GROUP)
            win = POOL_WINDOWS[g]
            pad[POOL_HALO:POOL_HALO + seq, :] = u_ref[:, cols]
            band = _band(rc, rc + POOL_HALO, POOL_HALO - win + 1, POOL_HALO + 1)
            wg = w_ref[g].astype(bf16)
            for r0 in range(0, seq, rc):
                ue = pad[r0:r0 + rc + POOL_HALO, :]
                hi, lo = _split_bf16(ue)
                sm = _dot(band, hi, NN) + _dot(band, lo, NN)
                pch = sm / _pool_counts(r0, rc, win) - ue[POOL_HALO:, :]
                pb = pch.astype(bf16)
                p_ref[r0:r0 + rc, cols] = pb
                o_ref[r0:r0 + rc, cols] = _dot(pb, wg, NN) * s_ref[:, cols]

    return pl.pallas_call(
        body, out_shape=(SDS((t_all, w), bf16), SDS(mix.shape, f32)), grid=(t_all // seq,),
        in_specs=[pl.BlockSpec((seq, w), lambda b: (b, 0)), pl.BlockSpec(pool_w.shape, lambda b: (0, 0, 0)),
                  pl.BlockSpec(scale.shape, lambda b: (0, 0)), ANY],
        out_specs=(pl.BlockSpec((seq, w), lambda b: (b, 0)), pl.BlockSpec((seq, w), lambda b: (b, 1))),
        scratch_shapes=[pltpu.VMEM((seq + POOL_HALO, POOL_GROUP), f32)],
        input_output_aliases={3: 1}, name=name, compiler_params=_params(1))(u, pool_w, scale, mix)


def _pool_bwd(dmix, p, pool_w, scale, seq, name):
    t_all, w = p.shape
    ng, rc = w // POOL_GROUP, min(POOL_CHUNK, seq)

    def body(dy_ref, p_ref, w_ref, s_ref, du_ref, dw_ref, ds_ref, dpn, dpr):
        b = pl.program_id(0)

        @pl.when(b == 0)
        def _():
            dw_ref[...] = jnp.zeros_like(dw_ref)
            ds_ref[...] = jnp.zeros_like(ds_ref)

        dpn[seq:seq + POOL_HALO, :] = jnp.zeros((POOL_HALO, POOL_GROUP), f32)
        for g in range(ng):
            cols = slice(g * POOL_GROUP, (g + 1) * POOL_GROUP)
            win = POOL_WINDOWS[g]
            wg = w_ref[g].astype(bf16)
            sg = s_ref[:, cols]
            dwg = jnp.zeros((POOL_GROUP, POOL_GROUP), f32)
            dsg = jnp.zeros((SUBLANES, POOL_GROUP), f32)
            for r0 in range(0, seq, rc):
                dy = dy_ref[r0:r0 + rc, cols]
                pb = p_ref[r0:r0 + rc, cols]
                dsg = dsg + _sum8(dy * _dot(pb, wg, NN))
                dyw = (dy * sg).astype(bf16)
                dwg = dwg + _dot(pb, dyw, TN)
                dp = _dot(dyw, wg, NT)
                dpr[r0:r0 + rc, :] = dp
                dpn[r0:r0 + rc, :] = dp / _pool_counts(r0, rc, win)
            dw_ref[g] += dwg
            ds_ref[:, cols] += dsg
            band = _band(rc, rc + POOL_HALO, 0, win)
            for r0 in range(0, seq, rc):
                hi, lo = _split_bf16(dpn[r0:r0 + rc + POOL_HALO, :])
                du = _dot(band, hi, NN) + _dot(band, lo, NN) - dpr[r0:r0 + rc, :]
                du_ref[r0:r0 + rc, cols] = du.astype(bf16)

    return pl.pallas_call(
        body, out_shape=(SDS((t_all, w), bf16), SDS(pool_w.shape, f32), SDS((SUBLANES, w), f32)), grid=(t_all // seq,),
        in_specs=[pl.BlockSpec((seq, w), lambda b: (b, 1)), pl.BlockSpec((seq, w), lambda b: (b, 0)),
                  pl.BlockSpec(pool_w.shape, lambda b: (0, 0, 0)), pl.BlockSpec(scale.shape, lambda b: (0, 0))],
        out_specs=(pl.BlockSpec((seq, w), lambda b: (b, 0)), pl.BlockSpec(pool_w.shape, lambda b: (0, 0, 0)),
                   pl.BlockSpec((SUBLANES, w), lambda b: (0, 0))),
        scratch_shapes=[pltpu.VMEM((seq + POOL_HALO, POOL_GROUP), f32), pltpu.VMEM((seq, POOL_GROUP), f32)],
        name=name, compiler_params=_params(1))(dmix, p, pool_w, scale)


XA_TQ = 256


def _xa_probs(qh, kh, dh):
    s = _dot(qh, kh, NT) * (dh ** -0.5)
    e = jnp.exp(s - jnp.max(s, axis=1, keepdims=True))
    return e / jnp.sum(e, axis=1, keepdims=True)


def _xa_fwd(q, kv, seq, name):
    t_all, d = q.shape
    nb = t_all // seq
    mem, dh, tq = kv.shape[0] // nb, d // XA_HEADS, min(XA_TQ, seq)
    nq = seq // tq

    def body(q_ref, kv_ref, o_ref):
        for h in range(XA_HEADS):
            cols = slice(h * dh, (h + 1) * dh)
            p = _xa_probs(q_ref[:, cols], kv_ref[:, cols], dh)
            o_ref[:, cols] = _dot(p.astype(bf16), kv_ref[:, d + h * dh:d + (h + 1) * dh], NN).astype(bf16)

    return pl.pallas_call(
        body, out_shape=SDS((t_all, d), bf16), grid=(nb, nq),
        in_specs=[pl.BlockSpec((tq, d), lambda b, i: (b * nq + i, 0)), pl.BlockSpec((mem, 2 * d), lambda b, i: (b, 0))],
        out_specs=pl.BlockSpec((tq, d), lambda b, i: (b * nq + i, 0)), name=name, compiler_params=_params(2))(q, kv)


def _xa_bwd(q, kv, do, seq, name):
    t_all, d = q.shape
    nb = t_all // seq
    mem, dh, tq = kv.shape[0] // nb, d // XA_HEADS, min(XA_TQ, seq)
    nq = seq // tq

    def body(q_ref, kv_ref, do_ref, dq_ref, dkv_ref):
        i = pl.program_id(1)

        @pl.when(i == 0)
        def _():
            dkv_ref[...] = jnp.zeros_like(dkv_ref)

        for h in range(XA_HEADS):
            cols = slice(h * dh, (h + 1) * dh)
            vcols = slice(d + h * dh, d + (h + 1) * dh)
            qh, kh, doh = q_ref[:, cols], kv_ref[:, cols], do_ref[:, cols]
            p = _xa_probs(qh, kh, dh)
            dkv_ref[:, vcols] += _dot(p.astype(bf16), doh, TN)
            dp = _dot(doh, kv_ref[:, vcols], NT)
            ds = (p * (dp - jnp.sum(dp * p, axis=1, keepdims=True)) * (dh ** -0.5)).astype(bf16)
            dq_ref[:, cols] = _dot(ds, kh, NN).astype(bf16)
            dkv_ref[:, cols] += _dot(ds, qh, TN)

    row = pl.BlockSpec((tq, d), lambda b, i: (b * nq + i, 0))
    kvs = pl.BlockSpec((mem, 2 * d), lambda b, i: (b, 0))
    return pl.pallas_call(body, out_shape=(SDS((t_all, d), bf16), SDS(kv.shape, f32)), grid=(nb, nq),
                          in_specs=[row, kvs, row], out_specs=(row, kvs), name=name, compiler_params=_params(2))(q, kv, do)


FFN_BR = 256
FFN_CHUNK = 256


def _conv3(ext, w_ref, b, cols, lo, rows):
    return (b + w_ref[2:3, cols] * ext[lo:lo + rows, :] + w_ref[1:2, cols] * ext[lo - 1:lo - 1 + rows, :]
            + w_ref[0:1, cols] * ext[lo - 2:lo - 2 + rows, :])


def _ffn_gate_fwd(up, cw, cb, seq, name):
    t_all, f2 = up.shape
    ff, br, ch = f2 // 2, min(FFN_BR, seq), FFN_CHUNK
    per_seq, hb = seq // br, br // SUBLANES

    def body(up_ref, halo_ref, cw_ref, cb_ref, o_ref, ev, eg):
        i = pl.program_id(0)
        keep = jnp.where(i % per_seq == 0, 0.0, 1.0)
        for c0 in range(0, ff, ch):
            convs = []
            for ext, off in ((ev, c0), (eg, ff + c0)):
                cols = slice(off, off + ch)
                ext[0:SUBLANES, :] = halo_ref[:, cols] * keep
                ext[SUBLANES:SUBLANES + br, :] = up_ref[:, cols]
                convs.append(_conv3(ext, cw_ref, cb_ref[:, cols], cols, SUBLANES, br))
            val, gate = convs
            o_ref[:, c0:c0 + ch] = (gate * _sigmoid(gate) * val).astype(bf16)

    return pl.pallas_call(
        body, out_shape=SDS((t_all, ff), bf16), grid=(t_all // br,),
        in_specs=[pl.BlockSpec((br, f2), lambda i: (i, 0)),
                  pl.BlockSpec((SUBLANES, f2), lambda i: (jnp.maximum(i * hb - 1, 0), 0)),
                  pl.BlockSpec(cw.shape, lambda i: (0, 0)), pl.BlockSpec(cb.shape, lambda i: (0, 0))],
        out_specs=pl.BlockSpec((br, ff), lambda i: (i, 0)),
        scratch_shapes=[pltpu.VMEM((br + SUBLANES, ch), f32), pltpu.VMEM((br + SUBLANES, ch), f32)],
        name=name, compiler_params=_params(1))(up, up, cw, cb)


def _ffn_gate_bwd(dact, up, cw, cb, seq, name):
    t_all, f2 = up.shape
    ff, br, ch = f2 // 2, min(FFN_BR, seq), FFN_CHUNK
    per_seq, hb, last = seq // br, br // SUBLANES, t_all // SUBLANES - 1
    ext_rows = br + SUBLANES

    def body(da_ref, dan_ref, up_ref, upp_ref, upn_ref, cw_ref, cb_ref, du_ref, dcw_ref, dcb_ref, uv, ug, dav, dcv, dcg):
        i = pl.program_id(0)

        @pl.when(i == 0)
        def _():
            dcw_ref[...] = jnp.zeros_like(dcw_ref)
            dcb_ref[...] = jnp.zeros_like(dcb_ref)

        keep_prev = jnp.where(i % per_seq == 0, 0.0, 1.0)
        keep_next = jnp.where((i + 1) % per_seq == 0, 0.0, 1.0)
        for c0 in range(0, ff, ch):
            convs = []
            for ext, off in ((uv, c0), (ug, ff + c0)):
                cols = slice(off, off + ch)
                ext[0:SUBLANES, :] = upp_ref[:, cols] * keep_prev
                ext[SUBLANES:SUBLANES + br, :] = up_ref[:, cols]
                ext[SUBLANES + br:2 * SUBLANES + br, :] = upn_ref[:, cols] * keep_next
                convs.append(_conv3(ext, cw_ref, cb_ref[:, cols], cols, SUBLANES, ext_rows))
            val, gate = convs
            dav[0:br, :] = da_ref[:, c0:c0 + ch]
            dav[br:ext_rows, :] = dan_ref[:, c0:c0 + ch] * keep_next
            da = dav[...]
            sg = _sigmoid(gate)
            dcv[...] = da * gate * sg
            dcg[...] = da * val * sg * (1.0 + gate * (1.0 - sg))
            for ext, dc, off in ((uv, dcv, c0), (ug, dcg, ff + c0)):
                cols = slice(off, off + ch)
                du = (cw_ref[2:3, cols] * dc[0:br, :] + cw_ref[1:2, cols] * dc[1:br + 1, :]
                      + cw_ref[0:1, cols] * dc[2:br + 2, :])
                du_ref[:, cols] = du.astype(bf16)
                d0 = dc[0:br, :]
                dcb_ref[:, cols] += _sum8(d0)
                for tap in range(3):
                    lo = SUBLANES - (2 - tap)
                    dcw_ref[tap, :, cols] += _sum8(d0 * ext[lo:lo + br, :])

    blk = lambda n: pl.BlockSpec((br, n), lambda i: (i, 0))
    prev = lambda n: pl.BlockSpec((SUBLANES, n), lambda i: (jnp.maximum(i * hb - 1, 0), 0))
    nxt = lambda n: pl.BlockSpec((SUBLANES, n), lambda i: (jnp.minimum((i + 1) * hb, last), 0))
    return pl.pallas_call(
        body, out_shape=(SDS((t_all, f2), bf16), SDS((3, SUBLANES, f2), f32), SDS((SUBLANES, f2), f32)), grid=(t_all // br,),
        in_specs=[blk(ff), nxt(ff), blk(f2), prev(f2), nxt(f2), pl.BlockSpec(cw.shape, lambda i: (0, 0)),
                  pl.BlockSpec(cb.shape, lambda i: (0, 0))],
        out_specs=(blk(f2), pl.BlockSpec((3, SUBLANES, f2), lambda i: (0, 0, 0)), pl.BlockSpec((SUBLANES, f2), lambda i: (0, 0))),
        scratch_shapes=[pltpu.VMEM((br + 2 * SUBLANES, ch), f32), pltpu.VMEM((br + 2 * SUBLANES, ch), f32),
                        pltpu.VMEM((ext_rows, ch), f32), pltpu.VMEM((ext_rows, ch), f32), pltpu.VMEM((ext_rows, ch), f32)],
        name=name, compiler_params=_params(1))(dact, dact, up, up, up, cw, cb)


SSM_GB = 8
SSM_PLANES = 8
SSM_ROWS = 256


def _ssm_pitch(seq):
    p = seq + SUBLANES
    assert (p // SUBLANES) % 2 == 1
    return p


def _rows(base, rc):
    return pl.ds(pl.multiple_of(base + rc * SSM_ROWS, SUBLANES), SSM_ROWS)


def _ssm_project_in(u_ref, b_ref, planes, e, seq, pitch):
    def chunk(rc, _):
        uc = u_ref[_rows(e * seq, rc), :].astype(bf16)
        for j in range(SSM_PLANES):
            planes[_rows(j * pitch, rc), :] = _dot(uc, b_ref[:, j * LANES:(j + 1) * LANES], NN)
        return 0
    lax.fori_loop(0, seq // SSM_ROWS, chunk, 0)


def _ssm_rows(planes, rc, pitch):
    return jnp.concatenate([planes[_rows(j * pitch, rc), :].astype(bf16) for j in range(SSM_PLANES)], axis=1)


def _ssm_scan(planes_list, l1, l2, seq, pitch, reverse=False):
    def step(s, hs):
        t = seq - 1 - s if reverse else s
        out = []
        for planes, h in zip(planes_list, hs):
            h = h * l1 + pltpu.roll(h, 4, 0) * l2 + planes[pl.ds(t, SUBLANES, stride=pitch), :]
            planes[pl.ds(t, SUBLANES, stride=pitch), :] = h
            out.append(h)
        return tuple(out)
    zero = jnp.zeros((SUBLANES, LANES), f32)
    lax.fori_loop(0, seq, step, tuple(zero for _ in planes_list))


def _ssm_fwd(u, b_big, c_big, lslab, dskip, seq, name):
    t_all, w = u.shape
    nb, gw, pitch = t_all // seq, SSM_GB * SSM_GROUP, _ssm_pitch(seq)
    assert gw == LANES

    def body(u_ref, b_ref, c_ref, l_ref, d_ref, y_ref, *planes):
        l1, l2 = l_ref[0:SUBLANES, :], l_ref[SUBLANES:2 * SUBLANES, :]
        for e in range(nb):
            _ssm_project_in(u_ref, b_ref, planes[e], e, seq, pitch)
        _ssm_scan(planes, l1, l2, seq, pitch)
        for e in range(nb):
            def chunk(rc, _, e=e):
                rows = _rows(e * seq, rc)
                y_ref[rows, :] = _dot(_ssm_rows(planes[e], rc, pitch), c_ref[...], NN) + d_ref[...] * u_ref[rows, :]
                return 0
            lax.fori_loop(0, seq // SSM_ROWS, chunk, 0)

    return pl.pallas_call(
        body, out_shape=SDS((t_all, w), f32), grid=(w // gw,),
        in_specs=[pl.BlockSpec((t_all, gw), lambda k: (0, k)), pl.BlockSpec((None,) + b_big.shape[1:], lambda k: (k, 0, 0)),
                  pl.BlockSpec((None,) + c_big.shape[1:], lambda k: (k, 0, 0)),
                  pl.BlockSpec((None,) + lslab.shape[1:], lambda k: (k, 0, 0)), pl.BlockSpec((1, gw), lambda k: (0, k))],
        out_specs=pl.BlockSpec((t_all, gw), lambda k: (0, k)),
        scratch_shapes=[pltpu.VMEM((SSM_PLANES * pitch, LANES), f32) for _ in range(nb)],
        name=name, compiler_params=_params(1))(u, b_big, c_big, lslab, dskip)


def _ssm_bwd(u, dy, b_big, c_big, lslab, dskip, seq, name):
    t_all, w = u.shape
    nb, gw, pitch = t_all // seq, SSM_GB * SSM_GROUP, _ssm_pitch(seq)
    ns = SSM_PLANES * LANES

    def body(u_ref, dy_ref, b_ref, c_ref, l_ref, d_ref, du_ref, db_ref, dc_ref, dl_ref, dd_ref, *planes):
        hp, ap = planes[:nb], planes[nb:]
        l1, l2 = l_ref[0:SUBLANES, :], l_ref[SUBLANES:2 * SUBLANES, :]
        for e in range(nb):
            _ssm_project_in(u_ref, b_ref, hp[e], e, seq, pitch)
        _ssm_scan(hp, l1, l2, seq, pitch)
        dd_ref[...] = jnp.zeros_like(dd_ref)
        dc_ref[...] = jnp.zeros_like(dc_ref)
        db_ref[...] = jnp.zeros_like(db_ref)
        for e in range(nb):
            def chunk(rc, _, e=e):
                rows = _rows(e * seq, rc)
                dyc = dy_ref[rows, :]
                dyb = dyc.astype(bf16)
                for j in range(SSM_PLANES):
                    ap[e][_rows(j * pitch, rc), :] = _dot(dyb, c_ref[j * LANES:(j + 1) * LANES, :], NT)
                dd_ref[...] += _sum8(dyc * u_ref[rows, :])
                dc_ref[...] += _dot(_ssm_rows(hp[e], rc, pitch), dyb, TN)
                return 0
            lax.fori_loop(0, seq // SSM_ROWS, chunk, 0)

        def step(s, carry):
            t = seq - 1 - s
            out = []
            for e in range(nb):
                a, s1, s2 = carry[e]
                a = a * l1 - pltpu.roll(a, 4, 0) * l2 + ap[e][pl.ds(t, SUBLANES, stride=pitch), :]
                ap[e][pl.ds(t, SUBLANES, stride=pitch), :] = a
                hprev = hp[e][pl.ds(jnp.maximum(t - 1, 0), SUBLANES, stride=pitch), :] * jnp.where(t > 0, 1.0, 0.0)
                out.append((a, s1 + a * hprev, s2 + a * pltpu.roll(hprev, 4, 0)))
            return tuple(out)
        zero = jnp.zeros((SUBLANES, LANES), f32)
        fin = lax.fori_loop(0, seq, step, tuple((zero, zero, zero) for _ in range(nb)))
        dl_ref[0:SUBLANES, :] = sum(f[1] for f in fin)
        dl_ref[SUBLANES:2 * SUBLANES, :] = sum(f[2] for f in fin)

        for e in range(nb):
            def chunk2(rc, _, e=e):
                rows = _rows(e * seq, rc)
                ar = _ssm_rows(ap[e], rc, pitch)
                du_ref[rows, :] = (_dot(ar, b_ref[...], NT) + d_ref[...] * dy_ref[rows, :]).astype(bf16)
                db_ref[...] += _dot(u_ref[rows, :].astype(bf16), ar, TN)
                return 0
            lax.fori_loop(0, seq // SSM_ROWS, chunk2, 0)

    col = pl.BlockSpec((t_all, gw), lambda k: (0, k))
    per = lambda s: pl.BlockSpec((None,) + s[1:], lambda k: (k, 0, 0))
    ng = w // gw
    return pl.pallas_call(
        body, out_shape=(SDS((t_all, w), bf16), SDS(b_big.shape, f32), SDS(c_big.shape, f32), SDS((ng, 2 * SUBLANES, LANES), f32),
                         SDS((SUBLANES, w), f32)),
        grid=(ng,),
        in_specs=[col, col, per(b_big.shape), per(c_big.shape), per(lslab.shape), pl.BlockSpec((1, gw), lambda k: (0, k))],
        out_specs=(col, per(b_big.shape), per(c_big.shape), per((ng, 2 * SUBLANES, LANES)), pl.BlockSpec((SUBLANES, gw), lambda k: (0, k))),
        scratch_shapes=[pltpu.VMEM((SSM_PLANES * pitch, LANES), f32) for _ in range(2 * nb)],
        name=name, compiler_params=_params(1))(u, dy, b_big, c_big, lslab, dskip)


def _ssm_disc_fwd(lam_re, lam_im, dt, b_re, b_im, name):
    def body(a_ref, b_ref, dt_ref, br_ref, bi_ref, lr_ref, li_ref, cr_ref, ci_ref, bbr_ref, bbi_ref):
        a, b, dtv = a_ref[...], b_ref[...], dt_ref[...]
        mag, ang = jnp.exp(a * dtv), b * dtv
        lr, li = mag * jnp.cos(ang), mag * jnp.sin(ang)
        nr, den = lr - 1.0, a * a + b * b
        cr, ci = (nr * a + li * b) / den, (li * a - nr * b) / den
        lr_ref[...], li_ref[...], cr_ref[...], ci_ref[...] = lr, li, cr, ci
        bbr_ref[...] = cr * br_ref[...] - ci * bi_ref[...]
        bbi_ref[...] = cr * bi_ref[...] + ci * br_ref[...]
    c, m = SDS(lam_re.shape, f32), SDS(b_re.shape, f32)
    return pl.pallas_call(body, out_shape=(c, c, c, c, m, m), name=name)(lam_re, lam_im, dt, b_re, b_im)


def _ssm_disc_bwd(lam_re, lam_im, dt, b_re, b_im, g_lr, g_li, g_bbr, g_bbi, name):
    def body(a_ref, b_ref, dt_ref, br_ref, bi_ref, glr_ref, gli_ref, gbr_ref, gbi_ref, da_ref, db_ref, ddt_ref, dbr_ref, dbi_ref):
        a, b, dtv = a_ref[...], b_ref[...], dt_ref[...]
        mag, ang = jnp.exp(a * dtv), b * dtv
        cs, sn = jnp.cos(ang), jnp.sin(ang)
        lr, li = mag * cs, mag * sn
        nr, den = lr - 1.0, a * a + b * b
        cr, ci = (nr * a + li * b) / den, (li * a - nr * b) / den
        gbr, gbi, brv, biv = gbr_ref[...], gbi_ref[...], br_ref[...], bi_ref[...]
        dbr_ref[...] = cr * gbr + ci * gbi
        dbi_ref[...] = cr * gbi - ci * gbr
        dcr = jnp.sum(brv * gbr + biv * gbi, axis=1, keepdims=True)
        dci = jnp.sum(brv * gbi - biv * gbr, axis=1, keepdims=True)
        dnum_r, dnum_i = dcr / den, dci / den
        dden = -(dcr * cr + dci * ci) / den
        dnr = dnum_r * a - dnum_i * b
        dli = gli_ref[...] + dnum_r * b + dnum_i * a
        dlr = glr_ref[...] + dnr
        dmag, dang = dlr * cs + dli * sn, dli * lr - dlr * li
        dadt = dmag * mag
        da_ref[...] = dnum_r * nr + dnum_i * li + dden * 2.0 * a + dadt * dtv
        db_ref[...] = dnum_r * li - dnum_i * nr + dden * 2.0 * b + dang * dtv
        ddt_ref[...] = dadt * a + dang * b
    c, m = SDS(lam_re.shape, f32), SDS(b_re.shape, f32)
    return pl.pallas_call(body, out_shape=(c, c, c, m, m), name=name)(lam_re, lam_im, dt, b_re, b_im, g_lr, g_li, g_bbr, g_bbi)


def _place():
    x, y, c = lax.axis_index("x"), lax.axis_index("y"), lax.axis_index("c")
    return x, y, c, 2 * x + y


def _half_axis(shape, ax):
    return 0 if shape[0] == 2 else (3 - ax)


def _sub(ref, axis, start, size):
    idx = [slice(None)] * len(ref.shape)
    idx[axis] = pl.ds(start, size)
    return ref.at[tuple(idx)]


def _region(ref, full_shape, ax, slot=None, half=None):
    if slot is not None:
        n = full_shape[ax] // N_CHIPS
        ref = _sub(ref, ax, slot * n, n)
    if half is not None:
        ha = _half_axis(full_shape, ax)
        n = full_shape[ha] // 2
        ref = _sub(ref, ha, half * n, n)
    return ref


def _halved(shape, axis):
    return tuple(s // 2 if a == axis else s for a, s in enumerate(shape))


def _all_gather(shards, axes, name):
    n = len(shards)
    fulls = [tuple(s * N_CHIPS if a == ax else s for a, s in enumerate(sh.shape)) for sh, ax in zip(shards, axes)]

    def body(*refs):
        src, dst = refs[:n], refs[n:2 * n]
        send_sems, recv_sems, local_sems = refs[2 * n:]
        x, y, c, p = _place()
        chips = [(1 - x, y), (x, 1 - y), (1 - x, 1 - y)]
        slots = [2 * cx + cy for cx, cy in chips]

        def copy(a, k, slot, half, to, from_shard):
            where = _region(dst[a], fulls[a], axes[a], slot, half)
            if from_shard:
                ha = _half_axis(fulls[a], axes[a])
                hn = fulls[a][ha] // 2
                source = _sub(src[a], ha, half * hn, hn)
            else:
                source = where
            return pltpu.make_async_remote_copy(src_ref=source, dst_ref=where, send_sem=send_sems.at[a, k],
                                                recv_sem=recv_sems.at[a, k], device_id=to, device_id_type=MESH)

        mine = [pltpu.make_async_copy(src[a], _region(dst[a], fulls[a], axes[a], p), local_sems.at[a]) for a in range(n)]
        for cp in mine:
            cp.start()
        first = [copy(a, j, p, c, (*chips[j], c), True) for a in range(n) for j in range(3)]
        for cp in first:
            cp.start()
        passed = []
        for a in range(n):
            for j in range(3):
                copy(a, j, slots[j], c, (x, y, c), False).wait_recv()
                fwd = copy(a, 3 + j, slots[j], c, (x, y, 1 - c), False)
                fwd.start()
                passed.append(fwd)
        for a in range(n):
            for j in range(3):
                copy(a, 3 + j, slots[j], 1 - c, (x, y, c), False).wait_recv()
        for cp in first + passed:
            cp.wait_send()
        for cp in mine:
            cp.wait()

    return pl.pallas_call(
        body, out_shape=tuple(SDS(f, s.dtype) for f, s in zip(fulls, shards)), in_specs=[ANY] * n, out_specs=tuple([ANY] * n),
        scratch_shapes=[pltpu.SemaphoreType.DMA((n, 6)), pltpu.SemaphoreType.DMA((n, 6)), pltpu.SemaphoreType.DMA((n,))],
        name=name)(*shards)


def _swap_halves(grads, axes, name):
    n = len(grads)
    shapes = [g.shape for g in grads]

    def body(*refs):
        src, dst = refs[:n], refs[n:2 * n]
        send_sems, recv_sems = refs[2 * n:]
        x, y, c, _ = _place()
        cps = [pltpu.make_async_remote_copy(src_ref=_region(src[a], shapes[a], axes[a], None, 1 - c), dst_ref=dst[a],
                                            send_sem=send_sems.at[a], recv_sem=recv_sems.at[a],
                                            device_id=(x, y, 1 - c), device_id_type=MESH) for a in range(n)]
        for cp in cps:
            cp.start()
        for cp in cps:
            cp.wait()

    outs = tuple(SDS(_halved(s, _half_axis(s, ax)), g.dtype) for s, ax, g in zip(shapes, axes, grads))
    return pl.pallas_call(body, out_shape=outs, in_specs=[ANY] * n, out_specs=tuple([ANY] * n),
                          scratch_shapes=[pltpu.SemaphoreType.DMA((n,)), pltpu.SemaphoreType.DMA((n,))], name=name)(*grads)


def _row_block(rows, row_bytes, limit=3 << 20):
    for b in (1024, 512, 256, 128, 64, 32, 16, 8):
        if rows % b == 0 and b * row_bytes <= limit:
            return b
    return rows


def _add_own_half(g, other, ax, cidx, name):
    ls, ks, ns = other.shape
    ha = _half_axis(g.shape, ax)
    bk = _row_block(ks, ns * 4)
    nkb = ks // bk

    def g_map(l, i, cref):
        c = cref[0]
        if ha == 0:
            return (c, i, 0)
        if ha == 1:
            return (0, i + c * nkb, 0)
        return (0, i, c)

    def body(c_ref, g_ref, o_ref, out_ref):
        del c_ref
        out_ref[...] = (g_ref[...].astype(f32) + o_ref[...].astype(f32)).astype(out_ref.dtype)

    grid_spec = pltpu.PrefetchScalarGridSpec(
        num_scalar_prefetch=1, grid=(ls, nkb),
        in_specs=[pl.BlockSpec((None, bk, ns), g_map), pl.BlockSpec((None, bk, ns), lambda l, i, cref: (l, i, 0))],
        out_specs=pl.BlockSpec((None, bk, ns), lambda l, i, cref: (l, i, 0)))
    return pl.pallas_call(body, out_shape=SDS(other.shape, g.dtype), grid_spec=grid_spec, name=name,
                          compiler_params=_params(2))(cidx, g, other)


def _to_owners(parts, axes, name):
    n = len(parts)
    slot_shapes = [tuple(s // N_CHIPS if a == ax else s for a, s in enumerate(pt.shape)) for pt, ax in zip(parts, axes)]

    def body(*refs):
        src, dst = refs[:n], refs[n:2 * n]
        send_sems, recv_sems, local_sems = refs[2 * n:]
        x, y, c, p = _place()
        chips = [(1 - x, y), (x, 1 - y), (1 - x, 1 - y)]
        slots = [2 * cx + cy for cx, cy in chips]

        def piece(a, slot):
            nn = parts[a].shape[axes[a]] // N_CHIPS
            return _sub(src[a], axes[a], slot * nn, nn)

        mine = [pltpu.make_async_copy(piece(a, p), dst[a].at[p], local_sems.at[a]) for a in range(n)]
        for cp in mine:
            cp.start()
        cps = [pltpu.make_async_remote_copy(src_ref=piece(a, slots[j]), dst_ref=dst[a].at[p], send_sem=send_sems.at[a, j],
                                            recv_sem=recv_sems.at[a, j], device_id=(*chips[j], c), device_id_type=MESH)
               for a in range(n) for j in range(3)]
        for cp in cps:
            cp.start()
        for a in range(n):
            for j in range(3):
                pltpu.make_async_remote_copy(src_ref=piece(a, p), dst_ref=dst[a].at[slots[j]], send_sem=send_sems.at[a, j],
                                             recv_sem=recv_sems.at[a, j], device_id=(x, y, c), device_id_type=MESH).wait_recv()
        for cp in cps:
            cp.wait_send()
        for cp in mine:
            cp.wait()

    outs = tuple(SDS((N_CHIPS,) + s, pt.dtype) for s, pt in zip(slot_shapes, parts))
    return pl.pallas_call(
        body, out_shape=outs, in_specs=[ANY] * n, out_specs=tuple([ANY] * n),
        scratch_shapes=[pltpu.SemaphoreType.DMA((n, 3)), pltpu.SemaphoreType.DMA((n, 3)), pltpu.SemaphoreType.DMA((n,))],
        name=name)(*parts)


def _sum_chips(stack, name):
    _, ls, ks, ns = stack.shape
    bk = _row_block(ks, ns * 4 * N_CHIPS)

    def body(s_ref, o_ref):
        acc = s_ref[0].astype(f32)
        for q in range(1, N_CHIPS):
            acc = acc + s_ref[q].astype(f32)
        o_ref[...] = acc

    return pl.pallas_call(
        body, out_shape=SDS((ls, ks, ns), f32), grid=(ls, ks // bk),
        in_specs=[pl.BlockSpec((N_CHIPS, None, bk, ns), lambda l, i: (0, l, i, 0))],
        out_specs=pl.BlockSpec((None, bk, ns), lambda l, i: (l, i, 0)), name=name, compiler_params=_params(2))(stack)


def _join_halves(halves, axes, shard_shapes, name):
    n = len(halves)

    def body(*refs):
        src, dst = refs[:n], refs[n:2 * n]
        send_sems, recv_sems, local_sems = refs[2 * n:]
        x, y, c, _ = _place()

        def where(a, half):
            ha = _half_axis(shard_shapes[a], axes[a])
            hn = shard_shapes[a][ha] // 2
            return _sub(dst[a], ha, half * hn, hn)

        mine = [pltpu.make_async_copy(src[a], where(a, c), local_sems.at[a]) for a in range(n)]
        cps = [pltpu.make_async_remote_copy(src_ref=src[a], dst_ref=where(a, c), send_sem=send_sems.at[a], recv_sem=recv_sems.at[a],
                                            device_id=(x, y, 1 - c), device_id_type=MESH) for a in range(n)]
        for cp in mine + cps:
            cp.start()
        for a in range(n):
            pltpu.make_async_remote_copy(src_ref=src[a], dst_ref=where(a, 1 - c), send_sem=send_sems.at[a], recv_sem=recv_sems.at[a],
                                         device_id=(x, y, c), device_id_type=MESH).wait_recv()
        for cp in cps:
            cp.wait_send()
        for cp in mine:
            cp.wait()

    outs = tuple(SDS(s, f32) for s in shard_shapes)
    return pl.pallas_call(
        body, out_shape=outs, in_specs=[ANY] * n, out_specs=tuple([ANY] * n),
        scratch_shapes=[pltpu.SemaphoreType.DMA((n,)), pltpu.SemaphoreType.DMA((n,)), pltpu.SemaphoreType.DMA((n,))],
        name=name)(*halves)


def _reduce_scatter(grads, axes, tag):
    cidx = jnp.reshape(lax.axis_index("c"), (1,)).astype(jnp.int32)
    others = _swap_halves(grads, axes, f"rs_swap_{tag}")
    parts = [_add_own_half(g, o, ax, cidx, f"rs_add_{tag}_{a}") for a, (g, o, ax) in enumerate(zip(grads, others, axes))]
    stacks = _to_owners(parts, axes, f"rs_owner_{tag}")
    halves = [_sum_chips(s, f"rs_sum_{tag}_{a}") for a, s in enumerate(stacks)]
    shard_shapes = [tuple(s // N_CHIPS if i == ax else s for i, s in enumerate(g.shape)) for g, ax in zip(grads, axes)]
    return _join_halves(halves, axes, shard_shapes, f"rs_join_{tag}")


SMALL_COLS = 256


def _pack(arrays, rows_multiple):
    flat = jnp.concatenate([a.reshape(-1).astype(f32) for a in arrays])
    rows = -(-flat.shape[0] // SMALL_COLS)
    rows = -(-rows // rows_multiple) * rows_multiple
    flat = jnp.pad(flat, (0, rows * SMALL_COLS - flat.shape[0]))
    return flat.reshape(1, rows, SMALL_COLS)


def _unpack(buf, shapes):
    flat, out, off = buf.reshape(-1), [], 0
    for s in shapes:
        n = math.prod(s)
        out.append(flat[off:off + n].reshape(s))
        off += n
    return out


def _block_diag_in(bb):
    g, p, c = bb.shape
    k = g // SSM_GB
    eye = jnp.eye(SSM_GB, dtype=bb.dtype)
    return jnp.einsum("kgpc,gh->kgchp", bb.reshape(k, SSM_GB, p, c), eye).reshape(k, SSM_GB * c, SSM_GB * p)


def _block_diag_out(cc):
    g, c, p = cc.shape
    k = g // SSM_GB
    eye = jnp.eye(SSM_GB, dtype=cc.dtype)
    return jnp.einsum("kgcp,gh->kgphc", cc.reshape(k, SSM_GB, c, p), eye).reshape(k, SSM_GB * p, SSM_GB * c)


def _diag_in(db, p, c):
    k = db.shape[0]
    return jnp.einsum("kgcgp->kgpc", db.reshape(k, SSM_GB, c, SSM_GB, p)).reshape(k * SSM_GB, p, c)


def _diag_out(dc, p, c):
    k = dc.shape[0]
    return jnp.einsum("kgpgc->kgcp", dc.reshape(k, SSM_GB, p, SSM_GB, c)).reshape(k * SSM_GB, c, p)


def _state_slab(v):
    g, p = v.shape
    return v.reshape(g // SSM_GB, SSM_GB * p // LANES, LANES)


BIG = ("ab_w_in", "ab_w_out", "ssm_w_in", "ssm_w_glu", "xa_w_q", "xa_w_kv", "xa_w_o", "ffn_w_up", "ffn_w_down")
BIG_AXIS = dict(ab_w_in=2, ab_w_out=1, ssm_w_in=1, ssm_w_glu=2, xa_w_q=1, xa_w_kv=2, xa_w_o=1, ffn_w_up=2, ffn_w_down=1)
SMALL_REPL = ("norm_mix", "norm_xattn", "norm_ffn", "norm_mem", "norm_final", "pool_w", "pool_scale", "ssm_lam_re", "ssm_lam_im",
              "ssm_log_dt", "ssm_b_re", "ssm_b_im", "ssm_c_re", "ssm_c_im", "ffn_conv_b")
SMALL_SHARDED = ("ssm_d", "ffn_conv_w")
WEIGHTS = ("norm_mix", "norm_xattn", "norm_ffn", "norm_mem", "norm_final", "ab_w_in", "pool_w", "pool_scale", "ab_w_out", "ssm_w_in",
           "ssm_lam_re", "ssm_lam_im", "ssm_log_dt", "ssm_b_re", "ssm_b_im", "ssm_c_re", "ssm_c_im", "ssm_d", "ssm_w_glu", "xa_w_q",
           "xa_w_kv", "xa_w_o", "ffn_w_up", "ffn_conv_w", "ffn_conv_b", "ffn_w_down")


def _local_step(xf, memf, tgt, w, wf, conv_w, ssm_d, seq):
    d = xf.shape[1]
    depth = w["norm_mix"].shape[0]
    sbw = wf["ab_w_in"].shape[2] // 4
    row = lambda a: a.reshape(1, -1)

    gs, ps = w["ssm_lam_re"].shape[1:]
    col = lambda a: a.reshape(gs * ps, 1)
    lam_re, lam_im = col(w["ssm_lam_re"][0]), col(w["ssm_lam_im"][0])
    dt = col(jnp.broadcast_to(jnp.exp(w["ssm_log_dt"][0])[:, None], (gs, ps)))
    b_re, b_im = w["ssm_b_re"][0].reshape(gs * ps, -1), w["ssm_b_im"][0].reshape(gs * ps, -1)
    lb_re, lb_im, _, _, bb_re, bb_im = _ssm_disc_fwd(lam_re, lam_im, dt, b_re, b_im, "ssm_disc")
    cgrp = b_re.shape[1]
    b_big = jnp.concatenate([_block_diag_in(bb_re.reshape(gs, ps, cgrp)), _block_diag_in(bb_im.reshape(gs, ps, cgrp))], axis=2).astype(bf16)
    c_big = jnp.concatenate([_block_diag_out(w["ssm_c_re"][0]), -_block_diag_out(w["ssm_c_im"][0])], axis=1).astype(bf16)
    lr_s, li_s = _state_slab(lb_re.reshape(gs, ps)), _state_slab(lb_im.reshape(gs, ps))
    lslab = jnp.concatenate([lr_s, lr_s, -li_s, li_s], axis=1)

    mem_n = _norm_fwd(memf, row(w["norm_mem"]), "norm_mem")
    kv = [_mm(mem_n, wf["xa_w_kv"], mode="nn", b_l=l, out_dtype=bf16, name=f"kv{l}") for l in range(depth)]
    xs, saved = [xf], []
    cur = xf
    for l in range(depth):
        sv = {}
        h = _norm_fwd(cur, row(w["norm_mix"][l]), f"norm_mix{l}")
        sv["h"] = h
        if l % 2 == 0:
            qkv = _mm(h, wf["ab_w_in"], mode="nn", b_l=0, n=3 * sbw, out_dtype=bf16, name=f"qkv{l}")
            u = _mm(h, wf["ab_w_in"], mode="nn", b_l=0, b_n0=3 * sbw, n=sbw, out_dtype=f32, name=f"poolin{l}")
            mix, ltot = _sb_fwd(qkv, seq, f"sb_fwd{l}")
            pooled, mix = _pool_fwd(u, mix, w["pool_w"][0], w["pool_scale"], seq, f"pool_fwd{l}")
            sv.update(qkv=qkv, mix=mix, ltot=ltot, pooled=pooled)
            cur = _mm(mix, wf["ab_w_out"], mode="nn", b_l=0, res=cur, out_dtype=f32, name=f"mixout{l}")
        else:
            us = _mm(h, wf["ssm_w_in"], mode="nn", b_l=0, out_dtype=f32, name=f"ssmin{l}")
            ys = _ssm_fwd(us, b_big, c_big, lslab, ssm_d, seq, f"ssm_fwd{l}")
            gl = _gelu_fwd(ys, f"gelu{l}")
            glu = _mm(gl, wf["ssm_w_glu"], mode="nn", b_l=0, out_dtype=f32, name=f"glu{l}")
            sv.update(us=us, ys=ys, gl=gl, glu=glu)
            cur = _glu_fwd(glu, cur, f"glugate{l}")
        sv["x1"] = cur
        hx = _norm_fwd(cur, row(w["norm_xattn"][l]), f"norm_xa{l}")
        qx = _mm(hx, wf["xa_w_q"], mode="nn", b_l=l, out_dtype=bf16, name=f"xaq{l}")
        ox = _xa_fwd(qx, kv[l], seq, f"xa_fwd{l}")
        cur = _mm(ox, wf["xa_w_o"], mode="nn", b_l=l, res=cur, out_dtype=f32, name=f"xao{l}")
        sv.update(hx=hx, qx=qx, ox=ox, x2=cur)
        hf = _norm_fwd(cur, row(w["norm_ffn"][l]), f"norm_ffn{l}")
        up = _mm(hf, wf["ffn_w_up"], mode="nn", b_l=l, out_dtype=f32, name=f"ffnup{l}")
        act = _ffn_gate_fwd(up, conv_w[l], row(w["ffn_conv_b"][l]), seq, f"ffn_gate{l}")
        cur = _mm(act, wf["ffn_w_down"], mode="nn", b_l=l, res=cur, out_dtype=f32, name=f"ffndown{l}", bn=512)
        sv.update(hf=hf, up=up, act=act)
        saved.append(sv)
        xs.append(cur)

    dx, g_final8, loss8 = _loss_head(cur, tgt, row(w["norm_final"]), "loss_head")

    gw = {}
    small = {"norm_final": jnp.sum(g_final8, axis=0)}
    g_mix, g_xa, g_ffn, g_cw, g_cb = [None] * depth, [None] * depth, [None] * depth, [None] * depth, [None] * depth
    dkv = [None] * depth

    def wgrad(key, a, b, l, layers, **kw):
        kw.setdefault("bk", 1024)
        gw[key] = _mm(a, b, mode="tn", out_dtype=bf16, out_l=l, out_layers=layers, out_prev=gw.get(key),
                      name=f"dw_{key}{l}", **kw)

    for l in reversed(range(depth)):
        sv = saved[l]
        dact = _mm(dx, wf["ffn_w_down"], mode="nt", b_l=l, out_dtype=f32, name=f"d_act{l}", bn=256)
        wgrad("ffn_w_down", sv["act"], dx, l, depth, bm=256)
        dup, dcw8, dcb8 = _ffn_gate_bwd(dact, sv["up"], conv_w[l], row(w["ffn_conv_b"][l]), seq, f"ffn_gate_bwd{l}")
        g_cw[l], g_cb[l] = jnp.sum(dcw8, axis=1), jnp.sum(dcb8, axis=0)
        wgrad("ffn_w_up", sv["hf"], dup, l, depth)
        dhf = _mm(dup, wf["ffn_w_up"], mode="nt", b_l=l, out_dtype=f32, name=f"d_hf{l}", bk=2816)
        dx, g8 = _norm_bwd(dhf, sv["x2"], dx, row(w["norm_ffn"][l]), f"norm_ffn_bwd{l}")
        g_ffn[l] = jnp.sum(g8, axis=0)
        dox = _mm(dx, wf["xa_w_o"], mode="nt", b_l=l, out_dtype=bf16, name=f"d_ox{l}")
        wgrad("xa_w_o", sv["ox"], dx, l, depth)
        dqx, dkv[l] = _xa_bwd(sv["qx"], kv[l], dox, seq, f"xa_bwd{l}")
        wgrad("xa_w_q", sv["hx"], dqx, l, depth)
        dhx = _mm(dqx, wf["xa_w_q"], mode="nt", b_l=l, out_dtype=f32, name=f"d_hx{l}")
        dx, g8 = _norm_bwd(dhx, sv["x1"], dx, row(w["norm_xattn"][l]), f"norm_xa_bwd{l}")
        g_xa[l] = jnp.sum(g8, axis=0)
        if l % 2 == 0:
            dmix = _mm(dx, wf["ab_w_out"], mode="nt", b_l=0, out_dtype=f32, name=f"d_mix{l}")
            wgrad("ab_w_out", sv["mix"], dx, 0, 1)
            dq, dk, dv = _sb_bwd(sv["qkv"], sv["ltot"], dmix, seq, f"sb_bwd{l}")
            du, dpw, dps8 = _pool_bwd(dmix, sv["pooled"], w["pool_w"][0], w["pool_scale"], seq, f"pool_bwd{l}")
            small["pool_w"], small["pool_scale"] = dpw[None], jnp.sum(dps8, axis=0)[None]
            dproj = jnp.concatenate([dq, dk, dv, du], axis=1)
            wgrad("ab_w_in", sv["h"], dproj, 0, 1)
            dh = _mm(dproj, wf["ab_w_in"], mode="nt", b_l=0, out_dtype=f32, name=f"d_h{l}")
        else:
            dglu = _glu_bwd(dx, sv["glu"], f"glugate_bwd{l}")
            wgrad("ssm_w_glu", sv["gl"], dglu, 0, 1)
            dgl = _mm(dglu, wf["ssm_w_glu"], mode="nt", b_l=0, out_dtype=f32, name=f"d_gelu{l}")
            dys = _gelu_bwd(dgl, sv["ys"], f"gelu_bwd{l}")
            dus, db_big, dc_big, dl, dd8 = _ssm_bwd(sv["us"], dys, b_big, c_big, lslab, ssm_d, seq, f"ssm_bwd{l}")
            small["ssm_d"] = jnp.sum(dd8, axis=0)[None]
            half = SSM_PLANES // 2
            g_lr = (dl[:, 0:half] + dl[:, half:SUBLANES]).reshape(gs * ps, 1)
            g_li = (dl[:, SUBLANES + half:] - dl[:, SUBLANES:SUBLANES + half]).reshape(gs * ps, 1)
            g_bbr = _diag_in(db_big[:, :, :SSM_GB * ps], ps, cgrp).reshape(gs * ps, cgrp)
            g_bbi = _diag_in(db_big[:, :, SSM_GB * ps:], ps, cgrp).reshape(gs * ps, cgrp)
            d_a, d_b, d_dt, d_br, d_bi = _ssm_disc_bwd(lam_re, lam_im, dt, b_re, b_im, g_lr, g_li, g_bbr, g_bbi, "ssm_disc_bwd")
            small["ssm_lam_re"], small["ssm_lam_im"] = d_a.reshape(1, gs, ps), d_b.reshape(1, gs, ps)
            small["ssm_log_dt"] = (jnp.sum(d_dt.reshape(gs, ps), axis=1) * dt.reshape(gs, ps)[:, 0])[None]
            small["ssm_b_re"], small["ssm_b_im"] = d_br.reshape(1, gs, ps, cgrp), d_bi.reshape(1, gs, ps, cgrp)
            small["ssm_c_re"] = _diag_out(dc_big[:, :SSM_GB * ps], ps, cgrp)[None]
            small["ssm_c_im"] = -_diag_out(dc_big[:, SSM_GB * ps:], ps, cgrp)[None]
            wgrad("ssm_w_in", sv["h"], dus, 0, 1)
            dh = _mm(dus, wf["ssm_w_in"], mode="nt", b_l=0, out_dtype=f32, name=f"d_h{l}")
        dx, g8 = _norm_bwd(dh, xs[l], dx, row(w["norm_mix"][l]), f"norm_mix_bwd{l}")
        g_mix[l] = jnp.sum(g8, axis=0)

    dmem_n = None
    for l in range(depth):
        wgrad("xa_w_kv", mem_n, dkv[l], l, depth, bk=mem_n.shape[0])
        dmem_n = _mm(dkv[l], wf["xa_w_kv"], mode="nt", b_l=l, res=dmem_n, out_dtype=f32, name=f"d_memn{l}")
    small["norm_mem"] = jnp.sum(_norm_bwd_gain_only(dmem_n, memf, "norm_mem_bwd"), axis=0)
    small["norm_mix"], small["norm_xattn"], small["norm_ffn"] = jnp.stack(g_mix), jnp.stack(g_xa), jnp.stack(g_ffn)
    small["ffn_conv_w"], small["ffn_conv_b"] = jnp.stack(g_cw), jnp.stack(g_cb)
    return loss8, dx, gw, small


def _step(x, mem, loss_target, w, m, v):
    nb, seq, d = x.shape
    t_all = nb * seq
    depth = w["norm_mix"].shape[0]
    chip = 2 * lax.axis_index("x") + lax.axis_index("y")

    big_axes = [BIG_AXIS[k] for k in BIG]
    small_mine = _pack([w[k] for k in SMALL_SHARDED], SUBLANES)
    gathered = _all_gather([w[k].astype(bf16) for k in BIG] + [small_mine], big_axes + [1], "gather_weights")
    wf = dict(zip(BIG, gathered[:-1]))
    per_chip = gathered[-1].reshape(N_CHIPS, -1)
    pieces = [_unpack(per_chip[q], [w[k].shape for k in SMALL_SHARDED]) for q in range(N_CHIPS)]
    ssm_d = jnp.concatenate([pc[0] for pc in pieces], axis=-1)
    conv_w = jnp.concatenate([pc[1] for pc in pieces], axis=-1)
    ff2 = conv_w.shape[-1]

    loss8, dx, gw, small = _local_step(x.reshape(t_all, d), mem.reshape(-1, d), loss_target.reshape(t_all, d), w, wf, conv_w,
                                       ssm_d, seq)
    loss = lax.psum(0.5 * jnp.sum(loss8) / d, ("x", "y", "c"))

    small_names = SMALL_REPL + SMALL_SHARDED
    small_full_shapes = [w[k].shape for k in SMALL_REPL] + [(1, d), (depth, 3, ff2)]
    small_buf = _pack([small[k] for k in small_names], 2 * N_CHIPS * SUBLANES)
    reduced = _reduce_scatter([gw[k] for k in BIG] + [small_buf], big_axes + [1], "grads")
    g_big = dict(zip(BIG, reduced[:-1]))
    small_all = _all_gather([reduced[-1]], [1], "gather_small_grads")[0]
    g_small = dict(zip(small_names, _unpack(small_all, small_full_shapes)))
    g_small["ssm_d"] = lax.dynamic_slice_in_dim(g_small["ssm_d"], chip * (d // N_CHIPS), d // N_CHIPS, axis=1)
    g_small["ffn_conv_w"] = lax.dynamic_slice_in_dim(g_small["ffn_conv_w"], chip * (ff2 // N_CHIPS), ff2 // N_CHIPS, axis=2)
    grads = {**g_big, **g_small}

    delta, new_m, new_v = {}, {}, {}
    for k in BIG:
        n_cols = w[k].shape[-1]
        two = lambda a: a.reshape(-1, n_cols)
        dl_, m_, v_ = _adamw(two(w[k]), two(grads[k]), two(m[k]), two(v[k]), f"adamw_{k}")
        delta[k], new_m[k], new_v[k] = dl_.reshape(w[k].shape), m_.reshape(w[k].shape), v_.reshape(w[k].shape)
    pk = lambda tree: _pack([tree[k] for k in small_names], SUBLANES)[0]
    small_shapes = [w[k].shape for k in small_names]
    outs = _adamw(pk(w), pk(grads), pk(m), pk(v), "adamw_small")
    for tree, buf in zip((delta, new_m, new_v), outs):
        tree.update(zip(small_names, _unpack(buf, small_shapes)))

    grad_x = dx.reshape(nb, seq, d)
    return (loss, grad_x, *[grads[k] for k in WEIGHTS], *[delta[k] for k in WEIGHTS], *[new_m[k] for k in WEIGHTS],
            *[new_v[k] for k in WEIGHTS])


def kernel(x, mem, norm_mix, norm_xattn, norm_ffn, norm_mem, norm_final, ab_w_in, pool_w, pool_scale, ab_w_out, ssm_w_in, ssm_lam_re, ssm_lam_im, ssm_log_dt, ssm_b_re, ssm_b_im, ssm_c_re, ssm_c_im, ssm_d, ssm_w_glu, xa_w_q, xa_w_kv, xa_w_o, ffn_w_up, ffn_conv_w, ffn_conv_b, ffn_w_down, loss_target, m_norm_mix, m_norm_xattn, m_norm_ffn, m_norm_mem, m_norm_final, m_ab_w_in, m_pool_w, m_pool_scale, m_ab_w_out, m_ssm_w_in, m_ssm_lam_re, m_ssm_lam_im, m_ssm_log_dt, m_ssm_b_re, m_ssm_b_im, m_ssm_c_re, m_ssm_c_im, m_ssm_d, m_ssm_w_glu, m_xa_w_q, m_xa_w_kv, m_xa_w_o, m_ffn_w_up, m_ffn_conv_w, m_ffn_conv_b, m_ffn_w_down, v_norm_mix, v_norm_xattn, v_norm_ffn, v_norm_mem, v_norm_final, v_ab_w_in, v_pool_w, v_pool_scale, v_ab_w_out, v_ssm_w_in, v_ssm_lam_re, v_ssm_lam_im, v_ssm_log_dt, v_ssm_b_re, v_ssm_b_im, v_ssm_c_re, v_ssm_c_im, v_ssm_d, v_ssm_w_glu, v_xa_w_q, v_xa_w_kv, v_xa_w_o, v_ffn_w_up, v_ffn_conv_w, v_ffn_conv_b, v_ffn_w_down):
    args = dict(locals())
    w = {k: args[k] for k in WEIGHTS}
    m = {k: args["m_" + k] for k in WEIGHTS}
    v = {k: args["v_" + k] for k in WEIGHTS}
    return _step(x, mem, loss_target, w, m, v)
```

```python
import functools
import math

import jax
import jax.numpy as jnp
from jax import lax
from jax.experimental import pallas as pl
from jax.experimental.pallas import tpu as pltpu

f32 = jnp.float32
bf16 = jnp.bfloat16
SDS = jax.ShapeDtypeStruct
MESH = pl.DeviceIdType.MESH
ANY = pl.BlockSpec(memory_space=pl.ANY)

SB_HEAD_DIM = 64
POOL_WINDOWS = (2, 4, 8, 16)
POOL_GROUP = 128
XA_HEADS = 4
SSM_GROUPS = 64
SSM_GROUP = 16
SSM_STATE = 64
EPS = 1e-6
ADAM_LR, ADAM_B1, ADAM_B2, ADAM_EPS, ADAM_WD, ADAM_STEP = 0.001, 0.9, 0.999, 1e-08, 0.01, 10

LANES = 128
SUBLANES = 8
N_CHIPS = 4
VMEM_LIMIT = 56 * 1024 * 1024

NN = ((1,), (0,))
NT = ((1,), (1,))
TN = ((0,), (0,))


def _dot(a, b, dims):
    return lax.dot_general(a, b, (dims, ((), ())), preferred_element_type=f32)


def _params(n_grid):
    return pltpu.CompilerParams(dimension_semantics=("arbitrary",) * n_grid, vmem_limit_bytes=VMEM_LIMIT)


def _sum8(x):
    r, n = x.shape
    return jnp.sum(x.reshape(r // SUBLANES, SUBLANES, n), axis=0)


def _split_bf16(x):
    hi = x.astype(bf16)
    lo = (x - hi.astype(f32)).astype(bf16)
    return hi, lo


def _sigmoid(x):
    return 1.0 / (1.0 + jnp.exp(-x))


def _mm(a, b, *, mode, name, out_dtype, bm=512, bn=512, bk=None, a_l=None, b_l=None, b_n0=0, n=None,
        res=None, out_l=None, out_layers=None, out_prev=None):
    dims = {"nn": NN, "nt": NT, "tn": TN}[mode]
    a2, b2 = a.shape[-2:], b.shape[-2:]
    if mode == "nn":
        (m, k), nfull = a2, b2[1]
    elif mode == "nt":
        (m, k), nfull = a2, b2[0]
    else:
        (k, m), nfull = a2, b2[1]
    n = nfull if n is None else n
    bm, bn = min(bm, m), min(bn, n)
    bk = k if bk is None else min(bk, k)
    assert m % bm == 0 and n % bn == 0 and k % bk == 0 and b_n0 % bn == 0, (name, m, n, k, bm, bn, bk)
    nk, n0b = k // bk, b_n0 // bn

    def with_layer(layer, blk, idx_fn):
        if layer is None:
            return pl.BlockSpec(blk, idx_fn)
        return pl.BlockSpec((None,) + blk, lambda i, j, kk: (layer,) + idx_fn(i, j, kk))

    if mode == "tn":
        a_spec = with_layer(a_l, (bk, bm), lambda i, j, kk: (kk, i))
    else:
        a_spec = with_layer(a_l, (bm, bk), lambda i, j, kk: (i, kk))
    if mode == "nt":
        b_spec = with_layer(b_l, (bn, bk), lambda i, j, kk: (j, kk))
    else:
        b_spec = with_layer(b_l, (bk, bn), lambda i, j, kk: (kk, j + n0b))
    o_spec = with_layer(out_l, (bm, bn), lambda i, j, kk: (i, j))
    ins, in_specs = [a, b], [a_spec, b_spec]
    if res is not None:
        ins.append(res)
        in_specs.append(pl.BlockSpec((bm, bn), lambda i, j, kk: (i, j)))
    aliases = {}
    if out_prev is not None:
        aliases = {len(ins): 0}
        ins.append(out_prev)
        in_specs.append(ANY)
    has_res, has_prev = res is not None, out_prev is not None

    def body(*refs):
        a_ref, b_ref = refs[0], refs[1]
        res_ref = refs[2] if has_res else None
        o_ref = refs[2 + has_res + has_prev]
        part = _dot(a_ref[...].astype(bf16), b_ref[...].astype(bf16), dims)

        def finish(r):
            if has_res:
                r = r + res_ref[...]
            o_ref[...] = r.astype(o_ref.dtype)

        if nk == 1:
            finish(part)
        else:
            acc_ref = refs[-1]
            kk = pl.program_id(2)

            @pl.when(kk == 0)
            def _():
                acc_ref[...] = part

            @pl.when(kk > 0)
            def _():
                acc_ref[...] += part

            @pl.when(kk == nk - 1)
            def _():
                finish(acc_ref[...])

    out_shape = SDS((m, n) if out_l is None else (out_layers, m, n), out_dtype)
    return pl.pallas_call(
        body, out_shape=out_shape, grid=(m // bm, n // bn, nk), in_specs=in_specs, out_specs=o_spec,
        scratch_shapes=[] if nk == 1 else [pltpu.VMEM((bm, bn), f32)],
        input_output_aliases=aliases, name=name, compiler_params=_params(3))(*ins)


def _rowwise(fn, row_ins, full_ins, row_outs, acc_outs, *, name, br=256):
    t = row_ins[0].shape[0]
    br = next(b for b in (br, 128, 64, 32, 16, 8, t) if b <= t and t % b == 0)
    nr, nf, no = len(row_ins), len(full_ins), len(row_outs)

    def body(*refs):
        rv = [r[...] for r in refs[:nr]]
        fv = [r[...] for r in refs[nr:nr + nf]]
        o_refs = refs[nr + nf:nr + nf + no]
        a_refs = refs[nr + nf + no:]
        outs, accs = fn(rv, fv)
        for o_ref, v in zip(o_refs, outs):
            o_ref[...] = v.astype(o_ref.dtype)
        if a_refs:
            i = pl.program_id(0)

            @pl.when(i == 0)
            def _():
                for a_ref, v in zip(a_refs, accs):
                    a_ref[...] = v

            @pl.when(i > 0)
            def _():
                for a_ref, v in zip(a_refs, accs):
                    a_ref[...] += v

    in_specs = [pl.BlockSpec((br, x.shape[1]), lambda i: (i, 0)) for x in row_ins]
    in_specs += [pl.BlockSpec(x.shape, lambda i, nd=x.ndim: (0,) * nd) for x in full_ins]
    out_specs = [pl.BlockSpec((br, s.shape[1]), lambda i: (i, 0)) for s in row_outs]
    out_specs += [pl.BlockSpec(s.shape, lambda i: (0, 0)) for s in acc_outs]
    res = pl.pallas_call(body, out_shape=tuple(row_outs) + tuple(acc_outs), grid=(t // br,), in_specs=in_specs,
                         out_specs=tuple(out_specs), name=name, compiler_params=_params(1))(*row_ins, *full_ins)
    return res


def _norm_fwd(x, g, name):
    def fn(rv, fv):
        (xv,), (gv,) = rv, fv
        r = lax.rsqrt(jnp.mean(xv * xv, axis=1, keepdims=True) + EPS)
        return [xv * r * gv], []
    return _rowwise(fn, [x], [g], [SDS(x.shape, bf16)], [], name=name)[0]


def _norm_bwd(dh, x, dres, g, name):
    d = x.shape[1]

    def fn(rv, fv):
        (dhv, xv, drv), (gv,) = rv, fv
        r = lax.rsqrt(jnp.mean(xv * xv, axis=1, keepdims=True) + EPS)
        xh = xv * r
        dxh = dhv * gv
        dx = drv + r * (dxh - xh * jnp.mean(dxh * xh, axis=1, keepdims=True))
        return [dx], [_sum8(dhv * xh)]
    return _rowwise(fn, [dh, x, dres], [g], [SDS(x.shape, f32)], [SDS((SUBLANES, d), f32)], name=name)


def _norm_bwd_gain_only(dh, x, name):
    d = x.shape[1]

    def fn(rv, fv):
        dhv, xv = rv
        r = lax.rsqrt(jnp.mean(xv * xv, axis=1, keepdims=True) + EPS)
        return [], [_sum8(dhv * xv * r)]
    return _rowwise(fn, [dh, x], [], [], [SDS((SUBLANES, d), f32)], name=name)[0]


def _loss_head(x, target, g, name):
    d = x.shape[1]

    def fn(rv, fv):
        (xv, tv), (gv,) = rv, fv
        r = lax.rsqrt(jnp.mean(xv * xv, axis=1, keepdims=True) + EPS)
        xh = xv * r
        err = xh * gv - tv
        dy = err * (1.0 / d)
        dxh = dy * gv
        dx = r * (dxh - xh * jnp.mean(dxh * xh, axis=1, keepdims=True))
        return [dx], [_sum8(dy * xh), _sum8(err * err)]
    return _rowwise(fn, [x, target], [g], [SDS(x.shape, f32)], [SDS((SUBLANES, d), f32), SDS((SUBLANES, d), f32)], name=name)


_GELU_C = math.sqrt(2.0 / math.pi)


def _gelu_fwd(y, name):
    def fn(rv, fv):
        (v,) = rv
        t = jnp.tanh(_GELU_C * (v + 0.044715 * v * v * v))
        return [0.5 * v * (1.0 + t)], []
    return _rowwise(fn, [y], [], [SDS(y.shape, bf16)], [], name=name)[0]


def _gelu_bwd(dg, y, name):
    def fn(rv, fv):
        dgv, v = rv
        t = jnp.tanh(_GELU_C * (v + 0.044715 * v * v * v))
        dt = (1.0 - t * t) * _GELU_C * (1.0 + 3.0 * 0.044715 * v * v)
        return [dgv * (0.5 * (1.0 + t) + 0.5 * v * dt)], []
    return _rowwise(fn, [dg, y], [], [SDS(y.shape, f32)], [], name=name)[0]


def _glu_fwd(glu, x, name):
    d = x.shape[1]

    def fn(rv, fv):
        gl, xv = rv
        return [xv + gl[:, :d] * _sigmoid(gl[:, d:])], []
    return _rowwise(fn, [glu, x], [], [SDS(x.shape, f32)], [], name=name)[0]


def _glu_bwd(dx, glu, name):
    d = dx.shape[1]

    def fn(rv, fv):
        dxv, gl = rv
        sg = _sigmoid(gl[:, d:])
        return [jnp.concatenate([dxv * sg, dxv * gl[:, :d] * sg * (1.0 - sg)], axis=1)], []
    return _rowwise(fn, [dx, glu], [], [SDS(glu.shape, bf16)], [], name=name)[0]


def _adamw(w, g, m, v, name):
    c1 = 1.0 - ADAM_B1 ** ADAM_STEP
    c2 = 1.0 - ADAM_B2 ** ADAM_STEP

    def fn(rv, fv):
        wv, gv, mv, vv = rv
        m2 = ADAM_B1 * mv + (1.0 - ADAM_B1) * gv
        v2 = ADAM_B2 * vv + (1.0 - ADAM_B2) * (gv * gv)
        delta = -ADAM_LR * ((m2 / c1) / (jnp.sqrt(v2 / c2) + ADAM_EPS) + ADAM_WD * wv)
        return [delta, m2, v2], []
    s = SDS(w.shape, f32)
    return _rowwise(fn, [w, g, m, v], [], [s, s, s], [], name=name)


SB_TQ = 128
SB_KB = 4


def _sb_logits(qh, kb, valid):
    z = _dot(qh, kb, NT) * (SB_HEAD_DIM ** -0.5)
    sp = jnp.log(1.0 + jnp.exp(-jnp.abs(z)))
    lb = jnp.minimum(z, 0.0) - sp
    lk_raw = jnp.minimum(-z, 0.0) - sp
    return lb, lk_raw, jnp.where(valid, lk_raw, 0.0)


def _sb_heads(q, t):
    lane = lax.broadcasted_iota(jnp.int32, (t, LANES), 1)
    masks = [(lane >= hh * SB_HEAD_DIM) & (lane < (hh + 1) * SB_HEAD_DIM) for hh in range(LANES // SB_HEAD_DIM)]
    return [(m, q * jnp.where(m, 1.0, 0.0).astype(bf16)) for m in masks]


def _sb_key_minus_query(t):
    return lax.broadcasted_iota(jnp.int32, (t, t), 1) - lax.broadcasted_iota(jnp.int32, (t, t), 0)


def _tri(t, op):
    row = lax.broadcasted_iota(jnp.int32, (t, t), 0)
    col = lax.broadcasted_iota(jnp.int32, (t, t), 1)
    return jnp.where(op(row, col), 1.0, 0.0).astype(bf16)


def _dot_split(x, u):
    hi, lo = _split_bf16(x)
    return _dot(hi, u, NN) + _dot(lo, u, NN)


def _sb_fwd(qkv, seq, name):
    t_all, w3 = qkv.shape
    w = w3 // 3
    hp, tq = w // LANES, SB_TQ
    nb, nq = t_all // seq, seq // tq
    kbn = min(SB_KB, nq)
    assert nq % kbn == 0

    def body(q_ref, k_ref, v_ref, o_ref, lt_ref):
        i = pl.program_id(2)
        heads = _sb_heads(q_ref[...], tq)
        kmq = _sb_key_minus_query(tq)
        u_after = _tri(tq, lambda r, c: r > c)
        n_it = (i + kbn) // kbn

        def step(it, carry):
            carry = list(carry)
            for kk in reversed(range(kbn)):
                j = (n_it - 1 - it) * kbn + kk
                off = pl.multiple_of(j * tq, tq)
                kb = k_ref[pl.ds(off, tq), :]
                vb = v_ref[pl.ds(off, tq), :]
                valid = kmq < (i - j) * tq
                for hh, (_, qh) in enumerate(heads):
                    c, acc = carry[2 * hh], carry[2 * hh + 1]
                    lb, _, lk = _sb_logits(qh, kb, valid)
                    aft = c + _dot_split(lk, u_after)
                    wgt = jnp.where(valid, jnp.exp(lb + aft), 0.0)
                    carry[2 * hh + 1] = acc + _dot(wgt.astype(bf16), vb, NN)
                    carry[2 * hh] = c + jnp.sum(lk, axis=1, keepdims=True)
            return tuple(carry)

        init = (jnp.zeros((tq, 1), f32), jnp.zeros((tq, LANES), f32)) * len(heads)
        fin = lax.fori_loop(0, n_it, step, init)
        out = jnp.zeros((tq, LANES), f32)
        ltot = jnp.zeros((tq, LANES), f32)
        for hh, (m, _) in enumerate(heads):
            out = out + jnp.where(m, fin[2 * hh + 1], 0.0)
            ltot = ltot + jnp.where(m, fin[2 * hh], 0.0)
        o_ref[...] = out
        lt_ref[...] = ltot

    return pl.pallas_call(
        body, out_shape=(SDS((t_all, 2 * w), f32), SDS((t_all, w), f32)), grid=(nb, hp, nq),
        in_specs=[pl.BlockSpec((tq, LANES), lambda b, p, i: (b * nq + i, p)),
                  pl.BlockSpec((seq, LANES), lambda b, p, i: (b, hp + p)),
                  pl.BlockSpec((seq, LANES), lambda b, p, i: (b, 2 * hp + p))],
        out_specs=(pl.BlockSpec((tq, LANES), lambda b, p, i: (b * nq + i, p)),
                   pl.BlockSpec((tq, LANES), lambda b, p, i: (b * nq + i, p))),
        name=name, compiler_params=_params(3))(qkv, qkv, qkv)


def _sb_bwd(qkv, ltot, dmix, seq, name):
    t_all, w3 = qkv.shape
    w = w3 // 3
    hp, tq = w // LANES, SB_TQ
    nb, nq = t_all // seq, seq // tq
    kbn = min(SB_KB, nq)
    assert nq % kbn == 0

    def body(q_ref, k_ref, v_ref, lt_ref, do_ref, dq_ref, dk_ref, dv_ref, dk_acc, dv_acc):
        i = pl.program_id(2)

        @pl.when(i == 0)
        def _():
            dk_acc[...] = jnp.zeros_like(dk_acc)
            dv_acc[...] = jnp.zeros_like(dv_acc)

        heads = _sb_heads(q_ref[...], tq)
        do = do_ref[...]
        ltv = lt_ref[...]
        dos = [jnp.where(m, do, 0.0).astype(bf16) for m, _ in heads]
        lts = [jnp.sum(jnp.where(m, ltv, 0.0), axis=1, keepdims=True) * (1.0 / SB_HEAD_DIM) for m, _ in heads]
        kmq = _sb_key_minus_query(tq)
        u_incl = _tri(tq, lambda r, c: r <= c)
        u_excl = _tri(tq, lambda r, c: r < c)
        n_it = (i + kbn) // kbn

        def step(it, carry):
            carry = list(carry)
            for kk in range(kbn):
                j = it * kbn + kk
                off = pl.multiple_of(j * tq, tq)
                kb = k_ref[pl.ds(off, tq), :]
                vb = v_ref[pl.ds(off, tq), :]
                valid = kmq < (i - j) * tq
                dk_j = jnp.zeros((tq, LANES), f32)
                dv_j = jnp.zeros((tq, LANES), f32)
                for hh, (_, qh) in enumerate(heads):
                    cp, cg, dq = carry[3 * hh:3 * hh + 3]
                    lb, lk_raw, lk = _sb_logits(qh, kb, valid)
                    aft = lts[hh] - (cp + _dot_split(lk, u_incl))
                    wgt = jnp.where(valid, jnp.exp(lb + aft), 0.0)
                    g = _dot(dos[hh], vb, NT) * wgt
                    gpre = cg + _dot_split(g, u_excl)
                    dz = jnp.where(valid, g * jnp.exp(lk_raw) - gpre * jnp.exp(lb), 0.0) * (SB_HEAD_DIM ** -0.5)
                    dzb = dz.astype(bf16)
                    dk_j = dk_j + _dot(dzb, qh, TN)
                    dv_j = dv_j + _dot(wgt.astype(bf16), dos[hh], TN)
                    carry[3 * hh:3 * hh + 3] = [cp + jnp.sum(lk, axis=1, keepdims=True), cg + jnp.sum(g, axis=1, keepdims=True),
                                                dq + _dot(dzb, kb, NN)]
                dk_acc[pl.ds(off, tq), :] += dk_j
                dv_acc[pl.ds(off, tq), :] += dv_j
            return tuple(carry)

        zero1 = jnp.zeros((tq, 1), f32)
        fin = lax.fori_loop(0, n_it, step, (zero1, zero1, jnp.zeros((tq, LANES), f32)) * len(heads))
        dq_all = jnp.zeros((tq, LANES), f32)
        for hh, (m, _) in enumerate(heads):
            dq_all = dq_all + jnp.where(m, fin[3 * hh + 2], 0.0)
        dq_ref[...] = dq_all.astype(bf16)

        @pl.when(i == nq - 1)
        def _():
            dk_ref[...] = dk_acc[...].astype(bf16)
            dv_ref[...] = dv_acc[...].astype(bf16)

    row_blk = pl.BlockSpec((tq, LANES), lambda b, p, i: (b * nq + i, p))
    seq_blk = pl.BlockSpec((seq, LANES), lambda b, p, i: (b, p))
    out = SDS((t_all, w), bf16)
    return pl.pallas_call(
        body, out_shape=(out, out, out), grid=(nb, hp, nq),
        in_specs=[row_blk,
                  pl.BlockSpec((seq, LANES), lambda b, p, i: (b, hp + p)),
                  pl.BlockSpec((seq, LANES), lambda b, p, i: (b, 2 * hp + p)),
                  row_blk, row_blk],
        out_specs=(row_blk, seq_blk, seq_blk),
        scratch_shapes=[pltpu.VMEM((seq, LANES), f32), pltpu.VMEM((seq, LANES), f32)],
        name=name, compiler_params=_params(3))(qkv, qkv, qkv, ltot, dmix)


POOL_CHUNK = 256
POOL_HALO = 16


def _band(rows, cols, lo, hi):
    r = lax.broadcasted_iota(jnp.int32, (rows, cols), 0)
    c = lax.broadcasted_iota(jnp.int32, (rows, cols), 1)
    d = c - r
    return jnp.where((d >= lo) & (d < hi), 1.0, 0.0).astype(bf16)


def _pool_counts(r0, rows, win):
    t = lax.broadcasted_iota(jnp.int32, (rows, 1), 0) + r0
    return jnp.minimum(t + 1, win).astype(f32)


def _pool_fwd(u, mix, pool_w, scale, seq, name):
    t_all, w = u.shape
    ng, rc = w // POOL_GROUP, min(POOL_CHUNK, seq)

    def body(u_ref, w_ref, s_ref, mix_in, p_ref, o_ref, pad):
        del mix_in
        pad[0:POOL_HALO, :] = jnp.zeros((POOL_HALO, POOL_GROUP), f32)
        for g in range(ng):
            cols = slice(g * POOL_GROUP, (g + 1) * POOL_GROUP)
            win = POOL_WINDOWS[g]
            pad[POOL_HALO:POOL_HALO + seq, :] = u_ref[:, cols]
            band = _band(rc, rc + POOL_HALO, POOL_HALO - win + 1, POOL_HALO + 1)
            wg = w_ref[g].astype(bf16)
            for r0 in range(0, seq, rc):
                ue = pad[r0:r0 + rc + POOL_HALO, :]
                hi, lo = _split_bf16(ue)
                sm = _dot(band, hi, NN) + _dot(band, lo, NN)
                pch = sm / _pool_counts(r0, rc, win) - ue[POOL_HALO:, :]
                pb = pch.astype(bf16)
                p_ref[r0:r0 + rc, cols] = pb
                o_ref[r0:r0 + rc, cols] = _dot(pb, wg, NN) * s_ref[:, cols]

    return pl.pallas_call(
        body, out_shape=(SDS((t_all, w), bf16), SDS(mix.shape, f32)), grid=(t_all // seq,),
        in_specs=[pl.BlockSpec((seq, w), lambda b: (b, 0)), pl.BlockSpec(pool_w.shape, lambda b: (0, 0, 0)),
                  pl.BlockSpec(scale.shape, lambda b: (0, 0)), ANY],
        out_specs=(pl.BlockSpec((seq, w), lambda b: (b, 0)), pl.BlockSpec((seq, w), lambda b: (b, 1))),
        scratch_shapes=[pltpu.VMEM((seq + POOL_HALO, POOL_GROUP), f32)],
        input_output_aliases={3: 1}, name=name, compiler_params=_params(1))(u, pool_w, scale, mix)


def _pool_bwd(dmix, p, pool_w, scale, seq, name):
    t_all, w = p.shape
    ng, rc = w // POOL_GROUP, min(POOL_CHUNK, seq)

    def body(dy_ref, p_ref, w_ref, s_ref, du_ref, dw_ref, ds_ref, dpn, dpr):
        b = pl.program_id(0)

        @pl.when(b == 0)
        def _():
            dw_ref[...] = jnp.zeros_like(dw_ref)
            ds_ref[...] = jnp.zeros_like(ds_ref)

        dpn[seq:seq + POOL_HALO, :] = jnp.zeros((POOL_HALO, POOL_GROUP), f32)
        for g in range(ng):
            cols = slice(g * POOL_GROUP, (g + 1) * POOL_GROUP)
            win = POOL_WINDOWS[g]
            wg = w_ref[g].astype(bf16)
            sg = s_ref[:, cols]
            dwg = jnp.zeros((POOL_GROUP, POOL_GROUP), f32)
            dsg = jnp.zeros((SUBLANES, POOL_GROUP), f32)
            for r0 in range(0, seq, rc):
                dy = dy_ref[r0:r0 + rc, cols]
                pb = p_ref[r0:r0 + rc, cols]
                dsg = dsg + _sum8(dy * _dot(pb, wg, NN))
                dyw = (dy * sg).astype(bf16)
                dwg = dwg + _dot(pb, dyw, TN)
                dp = _dot(dyw, wg, NT)
                dpr[r0:r0 + rc, :] = dp
                dpn[r0:r0 + rc, :] = dp / _pool_counts(r0, rc, win)
            dw_ref[g] += dwg
            ds_ref[:, cols] += dsg
            band = _band(rc, rc + POOL_HALO, 0, win)
            for r0 in range(0, seq, rc):
                hi, lo = _split_bf16(dpn[r0:r0 + rc + POOL_HALO, :])
                du = _dot(band, hi, NN) + _dot(band, lo, NN) - dpr[r0:r0 + rc, :]
                du_ref[r0:r0 + rc, cols] = du.astype(bf16)

    return pl.pallas_call(
        body, out_shape=(SDS((t_all, w), bf16), SDS(pool_w.shape, f32), SDS((SUBLANES, w), f32)), grid=(t_all // seq,),
        in_specs=[pl.BlockSpec((seq, w), lambda b: (b, 1)), pl.BlockSpec((seq, w), lambda b: (b, 0)),
                  pl.BlockSpec(pool_w.shape, lambda b: (0, 0, 0)), pl.BlockSpec(scale.shape, lambda b: (0, 0))],
        out_specs=(pl.BlockSpec((seq, w), lambda b: (b, 0)), pl.BlockSpec(pool_w.shape, lambda b: (0, 0, 0)),
                   pl.BlockSpec((SUBLANES, w), lambda b: (0, 0))),
        scratch_shapes=[pltpu.VMEM((seq + POOL_HALO, POOL_GROUP), f32), pltpu.VMEM((seq, POOL_GROUP), f32)],
        name=name, compiler_params=_params(1))(dmix, p, pool_w, scale)


XA_TQ = 256


def _xa_probs(qh, kh, dh):
    s = _dot(qh, kh, NT) * (dh ** -0.5)
    e = jnp.exp(s - jnp.max(s, axis=1, keepdims=True))
    return e / jnp.sum(e, axis=1, keepdims=True)


def _xa_fwd(q, kv, seq, name):
    t_all, d = q.shape
    nb = t_all // seq
    mem, dh, tq = kv.shape[0] // nb, d // XA_HEADS, min(XA_TQ, seq)
    nq = seq // tq

    def body(q_ref, kv_ref, o_ref):
        for h in range(XA_HEADS):
            cols = slice(h * dh, (h + 1) * dh)
            p = _xa_probs(q_ref[:, cols], kv_ref[:, cols], dh)
            o_ref[:, cols] = _dot(p.astype(bf16), kv_ref[:, d + h * dh:d + (h + 1) * dh], NN).astype(bf16)

    return pl.pallas_call(
        body, out_shape=SDS((t_all, d), bf16), grid=(nb, nq),
        in_specs=[pl.BlockSpec((tq, d), lambda b, i: (b * nq + i, 0)), pl.BlockSpec((mem, 2 * d), lambda b, i: (b, 0))],
        out_specs=pl.BlockSpec((tq, d), lambda b, i: (b * nq + i, 0)), name=name, compiler_params=_params(2))(q, kv)


def _xa_bwd(q, kv, do, seq, name):
    t_all, d = q.shape
    nb = t_all // seq
    mem, dh, tq = kv.shape[0] // nb, d // XA_HEADS, min(XA_TQ, seq)
    nq = seq // tq

    def body(q_ref, kv_ref, do_ref, dq_ref, dkv_ref):
        i = pl.program_id(1)

        @pl.when(i == 0)
        def _():
            dkv_ref[...] = jnp.zeros_like(dkv_ref)

        for h in range(XA_HEADS):
            cols = slice(h * dh, (h + 1) * dh)
            vcols = slice(d + h * dh, d + (h + 1) * dh)
            qh, kh, doh = q_ref[:, cols], kv_ref[:, cols], do_ref[:, cols]
            p = _xa_probs(qh, kh, dh)
            dkv_ref[:, vcols] += _dot(p.astype(bf16), doh, TN)
            dp = _dot(doh, kv_ref[:, vcols], NT)
            ds = (p * (dp - jnp.sum(dp * p, axis=1, keepdims=True)) * (dh ** -0.5)).astype(bf16)
            dq_ref[:, cols] = _dot(ds, kh, NN).astype(bf16)
            dkv_ref[:, cols] += _dot(ds, qh, TN)

    row = pl.BlockSpec((tq, d), lambda b, i: (b * nq + i, 0))
    kvs = pl.BlockSpec((mem, 2 * d), lambda b, i: (b, 0))
    return pl.pallas_call(body, out_shape=(SDS((t_all, d), bf16), SDS(kv.shape, f32)), grid=(nb, nq),
                          in_specs=[row, kvs, row], out_specs=(row, kvs), name=name, compiler_params=_params(2))(q, kv, do)


FFN_BR = 256
FFN_CHUNK = 256


def _conv3(ext, w_ref, b, cols, lo, rows):
    return (b + w_ref[2:3, cols] * ext[lo:lo + rows, :] + w_ref[1:2, cols] * ext[lo - 1:lo - 1 + rows, :]
            + w_ref[0:1, cols] * ext[lo - 2:lo - 2 + rows, :])


def _ffn_gate_fwd(up, cw, cb, seq, name):
    t_all, f2 = up.shape
    ff, br, ch = f2 // 2, min(FFN_BR, seq), FFN_CHUNK
    per_seq, hb = seq // br, br // SUBLANES

    def body(up_ref, halo_ref, cw_ref, cb_ref, o_ref, ev, eg):
        i = pl.program_id(0)
        keep = jnp.where(i % per_seq == 0, 0.0, 1.0)
        for c0 in range(0, ff, ch):
            convs = []
            for ext, off in ((ev, c0), (eg, ff + c0)):
                cols = slice(off, off + ch)
                ext[0:SUBLANES, :] = halo_ref[:, cols] * keep
                ext[SUBLANES:SUBLANES + br, :] = up_ref[:, cols]
                convs.append(_conv3(ext, cw_ref, cb_ref[:, cols], cols, SUBLANES, br))
            val, gate = convs
            o_ref[:, c0:c0 + ch] = (gate * _sigmoid(gate) * val).astype(bf16)

    return pl.pallas_call(
        body, out_shape=SDS((t_all, ff), bf16), grid=(t_all // br,),
        in_specs=[pl.BlockSpec((br, f2), lambda i: (i, 0)),
                  pl.BlockSpec((SUBLANES, f2), lambda i: (jnp.maximum(i * hb - 1, 0), 0)),
                  pl.BlockSpec(cw.shape, lambda i: (0, 0)), pl.BlockSpec(cb.shape, lambda i: (0, 0))],
        out_specs=pl.BlockSpec((br, ff), lambda i: (i, 0)),
        scratch_shapes=[pltpu.VMEM((br + SUBLANES, ch), f32), pltpu.VMEM((br + SUBLANES, ch), f32)],
        name=name, compiler_params=_params(1))(up, up, cw, cb)


def _ffn_gate_bwd(dact, up, cw, cb, seq, name):
    t_all, f2 = up.shape
    ff, br, ch = f2 // 2, min(FFN_BR, seq), FFN_CHUNK
    per_seq, hb, last = seq // br, br // SUBLANES, t_all // SUBLANES - 1
    ext_rows = br + SUBLANES

    def body(da_ref, dan_ref, up_ref, upp_ref, upn_ref, cw_ref, cb_ref, du_ref, dcw_ref, dcb_ref, uv, ug, dav, dcv, dcg):
        i = pl.program_id(0)

        @pl.when(i == 0)
        def _():
            dcw_ref[...] = jnp.zeros_like(dcw_ref)
            dcb_ref[...] = jnp.zeros_like(dcb_ref)

        keep_prev = jnp.where(i % per_seq == 0, 0.0, 1.0)
        keep_next = jnp.where((i + 1) % per_seq == 0, 0.0, 1.0)
        for c0 in range(0, ff, ch):
            convs = []
            for ext, off in ((uv, c0), (ug, ff + c0)):
                cols = slice(off, off + ch)
                ext[0:SUBLANES, :] = upp_ref[:, cols] * keep_prev
                ext[SUBLANES:SUBLANES + br, :] = up_ref[:, cols]
                ext[SUBLANES + br:2 * SUBLANES + br, :] = upn_ref[:, cols] * keep_next
                convs.append(_conv3(ext, cw_ref, cb_ref[:, cols], cols, SUBLANES, ext_rows))
            val, gate = convs
            dav[0:br, :] = da_ref[:, c0:c0 + ch]
            dav[br:ext_rows, :] = dan_ref[:, c0:c0 + ch] * keep_next
            da = dav[...]
            sg = _sigmoid(gate)
            dcv[...] = da * gate * sg
            dcg[...] = da * val * sg * (1.0 + gate * (1.0 - sg))
            for ext, dc, off in ((uv, dcv, c0), (ug, dcg, ff + c0)):
                cols = slice(off, off + ch)
                du = (cw_ref[2:3, cols] * dc[0:br, :] + cw_ref[1:2, cols] * dc[1:br + 1, :]
                      + cw_ref[0:1, cols] * dc[2:br + 2, :])
                du_ref[:, cols] = du.astype(bf16)
                d0 = dc[0:br, :]
                dcb_ref[:, cols] += _sum8(d0)
                for tap in range(3):
                    lo = SUBLANES - (2 - tap)
                    dcw_ref[tap, :, cols] += _sum8(d0 * ext[lo:lo + br, :])

    blk = lambda n: pl.BlockSpec((br, n), lambda i: (i, 0))
    prev = lambda n: pl.BlockSpec((SUBLANES, n), lambda i: (jnp.maximum(i * hb - 1, 0), 0))
    nxt = lambda n: pl.BlockSpec((SUBLANES, n), lambda i: (jnp.minimum((i + 1) * hb, last), 0))
    return pl.pallas_call(
        body, out_shape=(SDS((t_all, f2), bf16), SDS((3, SUBLANES, f2), f32), SDS((SUBLANES, f2), f32)), grid=(t_all // br,),
        in_specs=[blk(ff), nxt(ff), blk(f2), prev(f2), nxt(f2), pl.BlockSpec(cw.shape, lambda i: (0, 0)),
                  pl.BlockSpec(cb.shape, lambda i: (0, 0))],
        out_specs=(blk(f2), pl.BlockSpec((3, SUBLANES, f2), lambda i: (0, 0, 0)), pl.BlockSpec((SUBLANES, f2), lambda i: (0, 0))),
        scratch_shapes=[pltpu.VMEM((br + 2 * SUBLANES, ch), f32), pltpu.VMEM((br + 2 * SUBLANES, ch), f32),
                        pltpu.VMEM((ext_rows, ch), f32), pltpu.VMEM((ext_rows, ch), f32), pltpu.VMEM((ext_rows, ch), f32)],
        name=name, compiler_params=_params(1))(dact, dact, up, up, up, cw, cb)


SSM_GB = 8
SSM_PLANES = 8
SSM_ROWS = 256


def _ssm_pitch(seq):
    p = seq + SUBLANES
    assert (p // SUBLANES) % 2 == 1
    return p


def _rows(base, rc):
    return pl.ds(pl.multiple_of(base + rc * SSM_ROWS, SUBLANES), SSM_ROWS)


def _ssm_project_in(u_ref, b_ref, planes, e, seq, pitch):
    def chunk(rc, _):
        uc = u_ref[_rows(e * seq, rc), :].astype(bf16)
        for j in range(SSM_PLANES):
            planes[_rows(j * pitch, rc), :] = _dot(uc, b_ref[:, j * LANES:(j + 1) * LANES], NN)
        return 0
    lax.fori_loop(0, seq // SSM_ROWS, chunk, 0)


def _ssm_rows(planes, rc, pitch):
    return jnp.concatenate([planes[_rows(j * pitch, rc), :].astype(bf16) for j in range(SSM_PLANES)], axis=1)


def _ssm_scan(planes_list, l1, l2, seq, pitch, reverse=False):
    def step(s, hs):
        t = seq - 1 - s if reverse else s
        out = []
        for planes, h in zip(planes_list, hs):
            h = h * l1 + pltpu.roll(h, 4, 0) * l2 + planes[pl.ds(t, SUBLANES, stride=pitch), :]
            planes[pl.ds(t, SUBLANES, stride=pitch), :] = h
            out.append(h)
        return tuple(out)
    zero = jnp.zeros((SUBLANES, LANES), f32)
    lax.fori_loop(0, seq, step, tuple(zero for _ in planes_list))


def _ssm_fwd(u, b_big, c_big, lslab, dskip, seq, name):
    t_all, w = u.shape
    nb, gw, pitch = t_all // seq, SSM_GB * SSM_GROUP, _ssm_pitch(seq)
    assert gw == LANES

    def body(u_ref, b_ref, c_ref, l_ref, d_ref, y_ref, *planes):
        l1, l2 = l_ref[0:SUBLANES, :], l_ref[SUBLANES:2 * SUBLANES, :]
        for e in range(nb):
            _ssm_project_in(u_ref, b_ref, planes[e], e, seq, pitch)
        _ssm_scan(planes, l1, l2, seq, pitch)
        for e in range(nb):
            def chunk(rc, _, e=e):
                rows = _rows(e * seq, rc)
                y_ref[rows, :] = _dot(_ssm_rows(planes[e], rc, pitch), c_ref[...], NN) + d_ref[...] * u_ref[rows, :]
                return 0
            lax.fori_loop(0, seq // SSM_ROWS, chunk, 0)

    return pl.pallas_call(
        body, out_shape=SDS((t_all, w), f32), grid=(w // gw,),
        in_specs=[pl.BlockSpec((t_all, gw), lambda k: (0, k)), pl.BlockSpec((None,) + b_big.shape[1:], lambda k: (k, 0, 0)),
                  pl.BlockSpec((None,) + c_big.shape[1:], lambda k: (k, 0, 0)),
                  pl.BlockSpec((None,) + lslab.shape[1:], lambda k: (k, 0, 0)), pl.BlockSpec((1, gw), lambda k: (0, k))],
        out_specs=pl.BlockSpec((t_all, gw), lambda k: (0, k)),
        scratch_shapes=[pltpu.VMEM((SSM_PLANES * pitch, LANES), f32) for _ in range(nb)],
        name=name, compiler_params=_params(1))(u, b_big, c_big, lslab, dskip)


def _ssm_bwd(u, dy, b_big, c_big, lslab, dskip, seq, name):
    t_all, w = u.shape
    nb, gw, pitch = t_all // seq, SSM_GB * SSM_GROUP, _ssm_pitch(seq)
    ns = SSM_PLANES * LANES

    def body(u_ref, dy_ref, b_ref, c_ref, l_ref, d_ref, du_ref, db_ref, dc_ref, dl_ref, dd_ref, *planes):
        hp, ap = planes[:nb], planes[nb:]
        l1, l2 = l_ref[0:SUBLANES, :], l_ref[SUBLANES:2 * SUBLANES, :]
        for e in range(nb):
            _ssm_project_in(u_ref, b_ref, hp[e], e, seq, pitch)
        _ssm_scan(hp, l1, l2, seq, pitch)
        dd_ref[...] = jnp.zeros_like(dd_ref)
        dc_ref[...] = jnp.zeros_like(dc_ref)
        db_ref[...] = jnp.zeros_like(db_ref)
        for e in range(nb):
            def chunk(rc, _, e=e):
                rows = _rows(e * seq, rc)
                dyc = dy_ref[rows, :]
                dyb = dyc.astype(bf16)
                for j in range(SSM_PLANES):
                    ap[e][_rows(j * pitch, rc), :] = _dot(dyb, c_ref[j * LANES:(j + 1) * LANES, :], NT)
                dd_ref[...] += _sum8(dyc * u_ref[rows, :])
                dc_ref[...] += _dot(_ssm_rows(hp[e], rc, pitch), dyb, TN)
                return 0
            lax.fori_loop(0, seq // SSM_ROWS, chunk, 0)

        def step(s, carry):
            t = seq - 1 - s
            out = []
            for e in range(nb):
                a, s1, s2 = carry[e]
                a = a * l1 - pltpu.roll(a, 4, 0) * l2 + ap[e][pl.ds(t, SUBLANES, stride=pitch), :]
                ap[e][pl.ds(t, SUBLANES, stride=pitch), :] = a
                hprev = hp[e][pl.ds(jnp.maximum(t - 1, 0), SUBLANES, stride=pitch), :] * jnp.where(t > 0, 1.0, 0.0)
                out.append((a, s1 + a * hprev, s2 + a * pltpu.roll(hprev, 4, 0)))
            return tuple(out)
        zero = jnp.zeros((SUBLANES, LANES), f32)
        fin = lax.fori_loop(0, seq, step, tuple((zero, zero, zero) for _ in range(nb)))
        dl_ref[0:SUBLANES, :] = sum(f[1] for f in fin)
        dl_ref[SUBLANES:2 * SUBLANES, :] = sum(f[2] for f in fin)

        for e in range(nb):
            def chunk2(rc, _, e=e):
                rows = _rows(e * seq, rc)
                ar = _ssm_rows(ap[e], rc, pitch)
                du_ref[rows, :] = (_dot(ar, b_ref[...], NT) + d_ref[...] * dy_ref[rows, :]).astype(bf16)
                db_ref[...] += _dot(u_ref[rows, :].astype(bf16), ar, TN)
                return 0
            lax.fori_loop(0, seq // SSM_ROWS, chunk2, 0)

    col = pl.BlockSpec((t_all, gw), lambda k: (0, k))
    per = lambda s: pl.BlockSpec((None,) + s[1:], lambda k: (k, 0, 0))
    ng = w // gw
    return pl.pallas_call(
        body, out_shape=(SDS((t_all, w), bf16), SDS(b_big.shape, f32), SDS(c_big.shape, f32), SDS((ng, 2 * SUBLANES, LANES), f32),
                         SDS((SUBLANES, w), f32)),
        grid=(ng,),
        in_specs=[col, col, per(b_big.shape), per(c_big.shape), per(lslab.shape), pl.BlockSpec((1, gw), lambda k: (0, k))],
        out_specs=(col, per(b_big.shape), per(c_big.shape), per((ng, 2 * SUBLANES, LANES)), pl.BlockSpec((SUBLANES, gw), lambda k: (0, k))),
        scratch_shapes=[pltpu.VMEM((SSM_PLANES * pitch, LANES), f32) for _ in range(2 * nb)],
        name=name, compiler_params=_params(1))(u, dy, b_big, c_big, lslab, dskip)


def _ssm_disc_fwd(lam_re, lam_im, dt, b_re, b_im, name):
    def body(a_ref, b_ref, dt_ref, br_ref, bi_ref, lr_ref, li_ref, cr_ref, ci_ref, bbr_ref, bbi_ref):
        a, b, dtv = a_ref[...], b_ref[...], dt_ref[...]
        mag, ang = jnp.exp(a * dtv), b * dtv
        lr, li = mag * jnp.cos(ang), mag * jnp.sin(ang)
        nr, den = lr - 1.0, a * a + b * b
        cr, ci = (nr * a + li * b) / den, (li * a - nr * b) / den
        lr_ref[...], li_ref[...], cr_ref[...], ci_ref[...] = lr, li, cr, ci
        bbr_ref[...] = cr * br_ref[...] - ci * bi_ref[...]
        bbi_ref[...] = cr * bi_ref[...] + ci * br_ref[...]
    c, m = SDS(lam_re.shape, f32), SDS(b_re.shape, f32)
    return pl.pallas_call(body, out_shape=(c, c, c, c, m, m), name=name)(lam_re, lam_im, dt, b_re, b_im)


def _ssm_disc_bwd(lam_re, lam_im, dt, b_re, b_im, g_lr, g_li, g_bbr, g_bbi, name):
    def body(a_ref, b_ref, dt_ref, br_ref, bi_ref, glr_ref, gli_ref, gbr_ref, gbi_ref, da_ref, db_ref, ddt_ref, dbr_ref, dbi_ref):
        a, b, dtv = a_ref[...], b_ref[...], dt_ref[...]
        mag, ang = jnp.exp(a * dtv), b * dtv
        cs, sn = jnp.cos(ang), jnp.sin(ang)
        lr, li = mag * cs, mag * sn
        nr, den = lr - 1.0, a * a + b * b
        cr, ci = (nr * a + li * b) / den, (li * a - nr * b) / den
        gbr, gbi, brv, biv = gbr_ref[...], gbi_ref[...], br_ref[...], bi_ref[...]
        dbr_ref[...] = cr * gbr + ci * gbi
        dbi_ref[...] = cr * gbi - ci * gbr
        dcr = jnp.sum(brv * gbr + biv * gbi, axis=1, keepdims=True)
        dci = jnp.sum(brv * gbi - biv * gbr, axis=1, keepdims=True)
        dnum_r, dnum_i = dcr / den, dci / den
        dden = -(dcr * cr + dci * ci) / den
        dnr = dnum_r * a - dnum_i * b
        dli = gli_ref[...] + dnum_r * b + dnum_i * a
        dlr = glr_ref[...] + dnr
        dmag, dang = dlr * cs + dli * sn, dli * lr - dlr * li
        dadt = dmag * mag
        da_ref[...] = dnum_r * nr + dnum_i * li + dden * 2.0 * a + dadt * dtv
        db_ref[...] = dnum_r * li - dnum_i * nr + dden * 2.0 * b + dang * dtv
        ddt_ref[...] = dadt * a + dang * b
    c, m = SDS(lam_re.shape, f32), SDS(b_re.shape, f32)
    return pl.pallas_call(body, out_shape=(c, c, c, m, m), name=name)(lam_re, lam_im, dt, b_re, b_im, g_lr, g_li, g_bbr, g_bbi)


def _place():
    x, y, c = lax.axis_index("x"), lax.axis_index("y"), lax.axis_index("c")
    return x, y, c, 2 * x + y


def _half_axis(shape, ax):
    return 0 if shape[0] == 2 else (3 - ax)


def _sub(ref, axis, start, size):
    idx = [slice(None)] * len(ref.shape)
    idx[axis] = pl.ds(start, size)
    return ref.at[tuple(idx)]


def _region(ref, full_shape, ax, slot=None, half=None):
    if slot is not None:
        n = full_shape[ax] // N_CHIPS
        ref = _sub(ref, ax, slot * n, n)
    if half is not None:
        ha = _half_axis(full_shape, ax)
        n = full_shape[ha] // 2
        ref = _sub(ref, ha, half * n, n)
    return ref


def _halved(shape, axis):
    return tuple(s // 2 if a == axis else s for a, s in enumerate(shape))


def _all_gather(shards, axes, name):
    n = len(shards)
    fulls = [tuple(s * N_CHIPS if a == ax else s for a, s in enumerate(sh.shape)) for sh, ax in zip(shards, axes)]
    own = 6

    def body(*refs):
        src, dst = refs[:n], refs[n:2 * n]
        send_sems, recv_sems = refs[2 * n:]
        x, y, c, p = _place()
        chips = [(1 - x, y), (x, 1 - y), (1 - x, 1 - y)]
        slots = [2 * cx + cy for cx, cy in chips]

        def copy(a, k, slot, half, to, from_shard):
            where = _region(dst[a], fulls[a], axes[a], slot, half)
            if from_shard:
                ha = _half_axis(fulls[a], axes[a])
                hn = fulls[a][ha] // 2
                source = _sub(src[a], ha, half * hn, hn)
            else:
                source = where
            return pltpu.make_async_remote_copy(src_ref=source, dst_ref=where, send_sem=send_sems.at[a, k],
                                                recv_sem=recv_sems.at[a, k], device_id=to, device_id_type=MESH)

        mine = [pltpu.make_async_remote_copy(src_ref=src[a], dst_ref=_region(dst[a], fulls[a], axes[a], p),
                                             send_sem=send_sems.at[a, own], recv_sem=recv_sems.at[a, own],
                                             device_id=(x, y, 1 - c), device_id_type=MESH) for a in range(n)]
        first = [copy(a, j, p, c, (*chips[j], c), True) for a in range(n) for j in range(3)]
        for cp in first + mine:
            cp.start()
        passed = []
        for a in range(n):
            for j in range(3):
                copy(a, j, slots[j], c, (x, y, c), False).wait_recv()
                fwd = copy(a, 3 + j, slots[j], c, (x, y, 1 - c), False)
                fwd.start()
                passed.append(fwd)
        for a in range(n):
            for j in range(3):
                copy(a, 3 + j, slots[j], 1 - c, (x, y, c), False).wait_recv()
        for cp in mine:
            cp.wait_recv()
        for cp in first + passed + mine:
            cp.wait_send()

    return pl.pallas_call(
        body, out_shape=tuple(SDS(f, s.dtype) for f, s in zip(fulls, shards)), in_specs=[ANY] * n, out_specs=tuple([ANY] * n),
        scratch_shapes=[pltpu.SemaphoreType.DMA((n, 7)), pltpu.SemaphoreType.DMA((n, 7))], name=name)(*shards)


def _swap_halves(grads, axes, name):
    n = len(grads)
    shapes = [g.shape for g in grads]

    def body(*refs):
        src, dst = refs[:n], refs[n:2 * n]
        send_sems, recv_sems = refs[2 * n:]
        x, y, c, _ = _place()
        cps = [pltpu.make_async_remote_copy(src_ref=_region(src[a], shapes[a], axes[a], None, 1 - c), dst_ref=dst[a],
                                            send_sem=send_sems.at[a], recv_sem=recv_sems.at[a],
                                            device_id=(x, y, 1 - c), device_id_type=MESH) for a in range(n)]
        for cp in cps:
            cp.start()
        for cp in cps:
            cp.wait()

    outs = tuple(SDS(_halved(s, _half_axis(s, ax)), g.dtype) for s, ax, g in zip(shapes, axes, grads))
    return pl.pallas_call(body, out_shape=outs, in_specs=[ANY] * n, out_specs=tuple([ANY] * n),
                          scratch_shapes=[pltpu.SemaphoreType.DMA((n,)), pltpu.SemaphoreType.DMA((n,))], name=name)(*grads)


def _row_block(rows, row_bytes, limit=3 << 20):
    for b in (1024, 512, 256, 128, 64, 32, 16, 8):
        if rows % b == 0 and b * row_bytes <= limit:
            return b
    return rows


def _add_own_half(g, other, ax, cidx, name):
    _, kp, np_ = other.shape
    ha = _half_axis(g.shape, ax)
    ks, ns = (kp // N_CHIPS, np_) if ax == 1 else (kp, np_ // N_CHIPS)
    bk = _row_block(ks, ns * 4)
    nkb = ks // bk

    def g_map(q, i, cref):
        c = cref[0]
        if ax == 1:
            return (c, q * nkb + i, 0) if ha == 0 else (0, q * nkb + i, c)
        return (c, i, q) if ha == 0 else (0, c * nkb + i, q)

    def o_map(q, i, cref):
        return (0, q * nkb + i, 0) if ax == 1 else (0, i, q)

    def body(c_ref, g_ref, o_ref, send_ref, land_ref):
        del c_ref
        s = (g_ref[...].astype(f32) + o_ref[...].astype(f32)).astype(send_ref.dtype)
        send_ref[...] = s
        land_ref[...] = s

    out = pl.BlockSpec((None, bk, ns), lambda q, i, cref: (q, i, 0))
    grid_spec = pltpu.PrefetchScalarGridSpec(
        num_scalar_prefetch=1, grid=(N_CHIPS, nkb),
        in_specs=[pl.BlockSpec((None, bk, ns), g_map), pl.BlockSpec((None, bk, ns), o_map)], out_specs=(out, out))
    shape = SDS((N_CHIPS, ks, ns), g.dtype)
    return pl.pallas_call(body, out_shape=(shape, shape), grid_spec=grid_spec, name=name, compiler_params=_params(2))(cidx, g, other)


def _to_owners(sends, lands, name):
    n = len(sends)

    def body(*refs):
        src, dst = refs[:n], refs[2 * n:3 * n]
        send_sems, recv_sems = refs[3 * n:]
        x, y, c, p = _place()
        chips = [(1 - x, y), (x, 1 - y), (1 - x, 1 - y)]
        slots = [2 * cx + cy for cx, cy in chips]
        cps = [pltpu.make_async_remote_copy(src_ref=src[a].at[slots[j]], dst_ref=dst[a].at[p], send_sem=send_sems.at[a, j],
                                            recv_sem=recv_sems.at[a, j], device_id=(*chips[j], c), device_id_type=MESH)
               for a in range(n) for j in range(3)]
        for cp in cps:
            cp.start()
        for a in range(n):
            for j in range(3):
                pltpu.make_async_remote_copy(src_ref=src[a].at[p], dst_ref=dst[a].at[slots[j]], send_sem=send_sems.at[a, j],
                                             recv_sem=recv_sems.at[a, j], device_id=(x, y, c), device_id_type=MESH).wait_recv()
        for cp in cps:
            cp.wait_send()

    return pl.pallas_call(
        body, out_shape=tuple(SDS(l.shape, l.dtype) for l in lands), in_specs=[ANY] * (2 * n), out_specs=tuple([ANY] * n),
        scratch_shapes=[pltpu.SemaphoreType.DMA((n, 3)), pltpu.SemaphoreType.DMA((n, 3))],
        input_output_aliases={n + a: a for a in range(n)}, name=name)(*sends, *lands)


def _sum_chips(stack, shard_shape, ax, cidx, name):
    _, ks, ns = stack.shape
    ha = _half_axis(shard_shape, ax)
    bk = _row_block(ks, ns * 4 * N_CHIPS)
    nkb = ks // bk

    def o_map(i, cref):
        c = cref[0]
        return (c, i, 0) if ha == 0 else ((0, c * nkb + i, 0) if ha == 1 else (0, i, c))

    def body(c_ref, s_ref, o_ref):
        del c_ref
        acc = s_ref[0].astype(f32)
        for q in range(1, N_CHIPS):
            acc = acc + s_ref[q].astype(f32)
        o_ref[...] = acc

    grid_spec = pltpu.PrefetchScalarGridSpec(
        num_scalar_prefetch=1, grid=(nkb,), in_specs=[pl.BlockSpec((N_CHIPS, bk, ns), lambda i, cref: (0, i, 0))],
        out_specs=pl.BlockSpec((None, bk, ns), o_map))
    return pl.pallas_call(body, out_shape=SDS(shard_shape, f32), grid_spec=grid_spec, name=name, compiler_params=_params(1))(cidx, stack)


def _join_halves(slices, axes, name):
    n = len(slices)

    def body(*refs):
        dst = refs[n:2 * n]
        send_sems, recv_sems = refs[2 * n:]
        x, y, c, _ = _place()

        def half(a, h):
            ha = _half_axis(slices[a].shape, axes[a])
            hn = slices[a].shape[ha] // 2
            return _sub(dst[a], ha, h * hn, hn)

        cps = [pltpu.make_async_remote_copy(src_ref=half(a, c), dst_ref=half(a, c), send_sem=send_sems.at[a], recv_sem=recv_sems.at[a],
                                            device_id=(x, y, 1 - c), device_id_type=MESH) for a in range(n)]
        for cp in cps:
            cp.start()
        for a in range(n):
            pltpu.make_async_remote_copy(src_ref=half(a, c), dst_ref=half(a, 1 - c), send_sem=send_sems.at[a], recv_sem=recv_sems.at[a],
                                         device_id=(x, y, c), device_id_type=MESH).wait_recv()
        for cp in cps:
            cp.wait_send()

    return pl.pallas_call(
        body, out_shape=tuple(SDS(s.shape, s.dtype) for s in slices), in_specs=[ANY] * n, out_specs=tuple([ANY] * n),
        scratch_shapes=[pltpu.SemaphoreType.DMA((n,)), pltpu.SemaphoreType.DMA((n,))],
        input_output_aliases={a: a for a in range(n)}, name=name)(*slices)


def _reduce_scatter(grads, axes, tag):
    cidx = jnp.reshape(lax.axis_index("c"), (1,)).astype(jnp.int32)
    others = _swap_halves(grads, axes, f"rs_swap_{tag}")
    pairs = [_add_own_half(g, o, ax, cidx, f"rs_add_{tag}_{a}") for a, (g, o, ax) in enumerate(zip(grads, others, axes))]
    stacks = _to_owners([s for s, _ in pairs], [l for _, l in pairs], f"rs_owner_{tag}")
    shard_shapes = [tuple(s // N_CHIPS if i == ax else s for i, s in enumerate(g.shape)) for g, ax in zip(grads, axes)]
    slices = [_sum_chips(s, sh, ax, cidx, f"rs_sum_{tag}_{a}") for a, (s, sh, ax) in enumerate(zip(stacks, shard_shapes, axes))]
    return _join_halves(slices, axes, f"rs_join_{tag}")


SMALL_COLS = 256


def _pack(arrays, rows_multiple):
    flat = jnp.concatenate([a.reshape(-1).astype(f32) for a in arrays])
    rows = -(-flat.shape[0] // SMALL_COLS)
    rows = -(-rows // rows_multiple) * rows_multiple
    flat = jnp.pad(flat, (0, rows * SMALL_COLS - flat.shape[0]))
    return flat.reshape(1, rows, SMALL_COLS)


def _unpack(buf, shapes):
    flat, out, off = buf.reshape(-1), [], 0
    for s in shapes:
        n = math.prod(s)
        out.append(flat[off:off + n].reshape(s))
        off += n
    return out


def _block_diag_in(bb):
    g, p, c = bb.shape
    k = g // SSM_GB
    eye = jnp.eye(SSM_GB, dtype=bb.dtype)
    return jnp.einsum("kgpc,gh->kgchp", bb.reshape(k, SSM_GB, p, c), eye).reshape(k, SSM_GB * c, SSM_GB * p)


def _block_diag_out(cc):
    g, c, p = cc.shape
    k = g // SSM_GB
    eye = jnp.eye(SSM_GB, dtype=cc.dtype)
    return jnp.einsum("kgcp,gh->kgphc", cc.reshape(k, SSM_GB, c, p), eye).reshape(k, SSM_GB * p, SSM_GB * c)


def _diag_in(db, p, c):
    k = db.shape[0]
    return jnp.einsum("kgcgp->kgpc", db.reshape(k, SSM_GB, c, SSM_GB, p)).reshape(k * SSM_GB, p, c)


def _diag_out(dc, p, c):
    k = dc.shape[0]
    return jnp.einsum("kgpgc->kgcp", dc.reshape(k, SSM_GB, p, SSM_GB, c)).reshape(k * SSM_GB, c, p)


def _state_slab(v):
    g, p = v.shape
    return v.reshape(g // SSM_GB, SSM_GB * p // LANES, LANES)


BIG = ("ab_w_in", "ab_w_out", "ssm_w_in", "ssm_w_glu", "xa_w_q", "xa_w_kv", "xa_w_o", "ffn_w_up", "ffn_w_down")
BIG_AXIS = dict(ab_w_in=2, ab_w_out=1, ssm_w_in=1, ssm_w_glu=2, xa_w_q=1, xa_w_kv=2, xa_w_o=1, ffn_w_up=2, ffn_w_down=1)
SMALL_REPL = ("norm_mix", "norm_xattn", "norm_ffn", "norm_mem", "norm_final", "pool_w", "pool_scale", "ssm_lam_re", "ssm_lam_im",
              "ssm_log_dt", "ssm_b_re", "ssm_b_im", "ssm_c_re", "ssm_c_im", "ffn_conv_b")
SMALL_SHARDED = ("ssm_d", "ffn_conv_w")
WEIGHTS = ("norm_mix", "norm_xattn", "norm_ffn", "norm_mem", "norm_final", "ab_w_in", "pool_w", "pool_scale", "ab_w_out", "ssm_w_in",
           "ssm_lam_re", "ssm_lam_im", "ssm_log_dt", "ssm_b_re", "ssm_b_im", "ssm_c_re", "ssm_c_im", "ssm_d", "ssm_w_glu", "xa_w_q",
           "xa_w_kv", "xa_w_o", "ffn_w_up", "ffn_conv_w", "ffn_conv_b", "ffn_w_down")


def _local_step(xf, memf, tgt, w, wf, conv_w, ssm_d, seq):
    d = xf.shape[1]
    depth = w["norm_mix"].shape[0]
    sbw = wf["ab_w_in"].shape[2] // 4
    row = lambda a: a.reshape(1, -1)

    gs, ps = w["ssm_lam_re"].shape[1:]
    col = lambda a: a.reshape(gs * ps, 1)
    lam_re, lam_im = col(w["ssm_lam_re"][0]), col(w["ssm_lam_im"][0])
    dt = col(jnp.broadcast_to(jnp.exp(w["ssm_log_dt"][0])[:, None], (gs, ps)))
    b_re, b_im = w["ssm_b_re"][0].reshape(gs * ps, -1), w["ssm_b_im"][0].reshape(gs * ps, -1)
    lb_re, lb_im, _, _, bb_re, bb_im = _ssm_disc_fwd(lam_re, lam_im, dt, b_re, b_im, "ssm_disc")
    cgrp = b_re.shape[1]
    b_big = jnp.concatenate([_block_diag_in(bb_re.reshape(gs, ps, cgrp)), _block_diag_in(bb_im.reshape(gs, ps, cgrp))], axis=2).astype(bf16)
    c_big = jnp.concatenate([_block_diag_out(w["ssm_c_re"][0]), -_block_diag_out(w["ssm_c_im"][0])], axis=1).astype(bf16)
    lr_s, li_s = _state_slab(lb_re.reshape(gs, ps)), _state_slab(lb_im.reshape(gs, ps))
    lslab = jnp.concatenate([lr_s, lr_s, -li_s, li_s], axis=1)

    mem_n = _norm_fwd(memf, row(w["norm_mem"]), "norm_mem")
    kv = [_mm(mem_n, wf["xa_w_kv"], mode="nn", b_l=l, out_dtype=bf16, name=f"kv{l}") for l in range(depth)]
    xs, saved = [xf], []
    cur = xf
    for l in range(depth):
        sv = {}
        h = _norm_fwd(cur, row(w["norm_mix"][l]), f"norm_mix{l}")
        sv["h"] = h
        if l % 2 == 0:
            qkv = _mm(h, wf["ab_w_in"], mode="nn", b_l=0, n=3 * sbw, out_dtype=bf16, name=f"qkv{l}")
            u = _mm(h, wf["ab_w_in"], mode="nn", b_l=0, b_n0=3 * sbw, n=sbw, out_dtype=f32, name=f"poolin{l}")
            mix, ltot = _sb_fwd(qkv, seq, f"sb_fwd{l}")
            pooled, mix = _pool_fwd(u, mix, w["pool_w"][0], w["pool_scale"], seq, f"pool_fwd{l}")
            sv.update(qkv=qkv, mix=mix, ltot=ltot, pooled=pooled)
            cur = _mm(mix, wf["ab_w_out"], mode="nn", b_l=0, res=cur, out_dtype=f32, name=f"mixout{l}")
        else:
            us = _mm(h, wf["ssm_w_in"], mode="nn", b_l=0, out_dtype=f32, name=f"ssmin{l}")
            ys = _ssm_fwd(us, b_big, c_big, lslab, ssm_d, seq, f"ssm_fwd{l}")
            gl = _gelu_fwd(ys, f"gelu{l}")
            glu = _mm(gl, wf["ssm_w_glu"], mode="nn", b_l=0, out_dtype=f32, name=f"glu{l}")
            sv.update(us=us, ys=ys, gl=gl, glu=glu)
            cur = _glu_fwd(glu, cur, f"glugate{l}")
        sv["x1"] = cur
        hx = _norm_fwd(cur, row(w["norm_xattn"][l]), f"norm_xa{l}")
        qx = _mm(hx, wf["xa_w_q"], mode="nn", b_l=l, out_dtype=bf16, name=f"xaq{l}")
        ox = _xa_fwd(qx, kv[l], seq, f"xa_fwd{l}")
        cur = _mm(ox, wf["xa_w_o"], mode="nn", b_l=l, res=cur, out_dtype=f32, name=f"xao{l}")
        sv.update(hx=hx, qx=qx, ox=ox, x2=cur)
        hf = _norm_fwd(cur, row(w["norm_ffn"][l]), f"norm_ffn{l}")
        up = _mm(hf, wf["ffn_w_up"], mode="nn", b_l=l, out_dtype=f32, name=f"ffnup{l}")
        act = _ffn_gate_fwd(up, conv_w[l], row(w["ffn_conv_b"][l]), seq, f"ffn_gate{l}")
        cur = _mm(act, wf["ffn_w_down"], mode="nn", b_l=l, res=cur, out_dtype=f32, name=f"ffndown{l}", bn=512)
        sv.update(hf=hf, up=up, act=act)
        saved.append(sv)
        xs.append(cur)

    dx, g_final8, loss8 = _loss_head(cur, tgt, row(w["norm_final"]), "loss_head")

    gw = {}
    small = {"norm_final": jnp.sum(g_final8, axis=0)}
    g_mix, g_xa, g_ffn, g_cw, g_cb = [None] * depth, [None] * depth, [None] * depth, [None] * depth, [None] * depth
    dkv = [None] * depth

    def wgrad(key, a, b, l, layers, **kw):
        kw.setdefault("bk", 1024)
        gw[key] = _mm(a, b, mode="tn", out_dtype=bf16, out_l=l, out_layers=layers, out_prev=gw.get(key),
                      name=f"dw_{key}{l}", **kw)

    for l in reversed(range(depth)):
        sv = saved[l]
        dact = _mm(dx, wf["ffn_w_down"], mode="nt", b_l=l, out_dtype=f32, name=f"d_act{l}", bn=256)
        wgrad("ffn_w_down", sv["act"], dx, l, depth, bm=256)
        dup, dcw8, dcb8 = _ffn_gate_bwd(dact, sv["up"], conv_w[l], row(w["ffn_conv_b"][l]), seq, f"ffn_gate_bwd{l}")
        g_cw[l], g_cb[l] = jnp.sum(dcw8, axis=1), jnp.sum(dcb8, axis=0)
        wgrad("ffn_w_up", sv["hf"], dup, l, depth)
        dhf = _mm(dup, wf["ffn_w_up"], mode="nt", b_l=l, out_dtype=f32, name=f"d_hf{l}", bk=2816)
        dx, g8 = _norm_bwd(dhf, sv["x2"], dx, row(w["norm_ffn"][l]), f"norm_ffn_bwd{l}")
        g_ffn[l] = jnp.sum(g8, axis=0)
        dox = _mm(dx, wf["xa_w_o"], mode="nt", b_l=l, out_dtype=bf16, name=f"d_ox{l}")
        wgrad("xa_w_o", sv["ox"], dx, l, depth)
        dqx, dkv[l] = _xa_bwd(sv["qx"], kv[l], dox, seq, f"xa_bwd{l}")
        wgrad("xa_w_q", sv["hx"], dqx, l, depth)
        dhx = _mm(dqx, wf["xa_w_q"], mode="nt", b_l=l, out_dtype=f32, name=f"d_hx{l}")
        dx, g8 = _norm_bwd(dhx, sv["x1"], dx, row(w["norm_xattn"][l]), f"norm_xa_bwd{l}")
        g_xa[l] = jnp.sum(g8, axis=0)
        if l % 2 == 0:
            dmix = _mm(dx, wf["ab_w_out"], mode="nt", b_l=0, out_dtype=f32, name=f"d_mix{l}")
            wgrad("ab_w_out", sv["mix"], dx, 0, 1)
            dq, dk, dv = _sb_bwd(sv["qkv"], sv["ltot"], dmix, seq, f"sb_bwd{l}")
            du, dpw, dps8 = _pool_bwd(dmix, sv["pooled"], w["pool_w"][0], w["pool_scale"], seq, f"pool_bwd{l}")
            small["pool_w"], small["pool_scale"] = dpw[None], jnp.sum(dps8, axis=0)[None]
            dproj = jnp.concatenate([dq, dk, dv, du], axis=1)
            wgrad("ab_w_in", sv["h"], dproj, 0, 1)
            dh = _mm(dproj, wf["ab_w_in"], mode="nt", b_l=0, out_dtype=f32, name=f"d_h{l}")
        else:
            dglu = _glu_bwd(dx, sv["glu"], f"glugate_bwd{l}")
            wgrad("ssm_w_glu", sv["gl"], dglu, 0, 1)
            dgl = _mm(dglu, wf["ssm_w_glu"], mode="nt", b_l=0, out_dtype=f32, name=f"d_gelu{l}")
            dys = _gelu_bwd(dgl, sv["ys"], f"gelu_bwd{l}")
            dus, db_big, dc_big, dl, dd8 = _ssm_bwd(sv["us"], dys, b_big, c_big, lslab, ssm_d, seq, f"ssm_bwd{l}")
            small["ssm_d"] = jnp.sum(dd8, axis=0)[None]
            half = SSM_PLANES // 2
            g_lr = (dl[:, 0:half] + dl[:, half:SUBLANES]).reshape(gs * ps, 1)
            g_li = (dl[:, SUBLANES + half:] - dl[:, SUBLANES:SUBLANES + half]).reshape(gs * ps, 1)
            g_bbr = _diag_in(db_big[:, :, :SSM_GB * ps], ps, cgrp).reshape(gs * ps, cgrp)
            g_bbi = _diag_in(db_big[:, :, SSM_GB * ps:], ps, cgrp).reshape(gs * ps, cgrp)
            d_a, d_b, d_dt, d_br, d_bi = _ssm_disc_bwd(lam_re, lam_im, dt, b_re, b_im, g_lr, g_li, g_bbr, g_bbi, "ssm_disc_bwd")
            small["ssm_lam_re"], small["ssm_lam_im"] = d_a.reshape(1, gs, ps), d_b.reshape(1, gs, ps)
            small["ssm_log_dt"] = (jnp.sum(d_dt.reshape(gs, ps), axis=1) * dt.reshape(gs, ps)[:, 0])[None]
            small["ssm_b_re"], small["ssm_b_im"] = d_br.reshape(1, gs, ps, cgrp), d_bi.reshape(1, gs, ps, cgrp)
            small["ssm_c_re"] = _diag_out(dc_big[:, :SSM_GB * ps], ps, cgrp)[None]
            small["ssm_c_im"] = -_diag_out(dc_big[:, SSM_GB * ps:], ps, cgrp)[None]
            wgrad("ssm_w_in", sv["h"], dus, 0, 1)
            dh = _mm(dus, wf["ssm_w_in"], mode="nt", b_l=0, out_dtype=f32, name=f"d_h{l}")
        dx, g8 = _norm_bwd(dh, xs[l], dx, row(w["norm_mix"][l]), f"norm_mix_bwd{l}")
        g_mix[l] = jnp.sum(g8, axis=0)

    dmem_n = None
    for l in range(depth):
        wgrad("xa_w_kv", mem_n, dkv[l], l, depth, bk=mem_n.shape[0])
        dmem_n = _mm(dkv[l], wf["xa_w_kv"], mode="nt", b_l=l, res=dmem_n, out_dtype=f32, name=f"d_memn{l}")
    small["norm_mem"] = jnp.sum(_norm_bwd_gain_only(dmem_n, memf, "norm_mem_bwd"), axis=0)
    small["norm_mix"], small["norm_xattn"], small["norm_ffn"] = jnp.stack(g_mix), jnp.stack(g_xa), jnp.stack(g_ffn)
    small["ffn_conv_w"], small["ffn_conv_b"] = jnp.stack(g_cw), jnp.stack(g_cb)
    return loss8, dx, gw, small


def _step(x, mem, loss_target, w, m, v):
    nb, seq, d = x.shape
    t_all = nb * seq
    depth = w["norm_mix"].shape[0]
    chip = 2 * lax.axis_index("x") + lax.axis_index("y")

    big_axes = [BIG_AXIS[k] for k in BIG]
    small_mine = _pack([w[k] for k in SMALL_SHARDED], SUBLANES)
    gathered = _all_gather([w[k].astype(bf16) for k in BIG] + [small_mine], big_axes + [1], "gather_weights")
    wf = dict(zip(BIG, gathered[:-1]))
    per_chip = gathered[-1].reshape(N_CHIPS, -1)
    pieces = [_unpack(per_chip[q], [w[k].shape for k in SMALL_SHARDED]) for q in range(N_CHIPS)]
    ssm_d = jnp.concatenate([pc[0] for pc in pieces], axis=-1)
    conv_w = jnp.concatenate([pc[1] for pc in pieces], axis=-1)
    ff2 = conv_w.shape[-1]

    loss8, dx, gw, small = _local_step(x.reshape(t_all, d), mem.reshape(-1, d), loss_target.reshape(t_all, d), w, wf, conv_w,
                                       ssm_d, seq)
    loss = lax.psum(0.5 * jnp.sum(loss8) / d, ("x", "y", "c"))

    small_names = SMALL_REPL + SMALL_SHARDED
    small_full_shapes = [w[k].shape for k in SMALL_REPL] + [(1, d), (depth, 3, ff2)]
    small_buf = _pack([small[k] for k in small_names], 2 * N_CHIPS * SUBLANES)
    reduced = _reduce_scatter([gw[k] for k in BIG] + [small_buf], big_axes + [1], "grads")
    g_big = dict(zip(BIG, reduced[:-1]))
    small_all = _all_gather([reduced[-1]], [1], "gather_small_grads")[0]
    g_small = dict(zip(small_names, _unpack(small_all, small_full_shapes)))
    g_small["ssm_d"] = lax.dynamic_slice_in_dim(g_small["ssm_d"], chip * (d // N_CHIPS), d // N_CHIPS, axis=1)
    g_small["ffn_conv_w"] = lax.dynamic_slice_in_dim(g_small["ffn_conv_w"], chip * (ff2 // N_CHIPS), ff2 // N_CHIPS, axis=2)
    grads = {**g_big, **g_small}

    delta, new_m, new_v = {}, {}, {}
    for k in BIG:
        n_cols = w[k].shape[-1]
        two = lambda a: a.reshape(-1, n_cols)
        dl_, m_, v_ = _adamw(two(w[k]), two(grads[k]), two(m[k]), two(v[k]), f"adamw_{k}")
        delta[k], new_m[k], new_v[k] = dl_.reshape(w[k].shape), m_.reshape(w[k].shape), v_.reshape(w[k].shape)
    pk = lambda tree: _pack([tree[k] for k in small_names], 256)[0]
    small_shapes = [w[k].shape for k in small_names]
    outs = _adamw(pk(w), pk(grads), pk(m), pk(v), "adamw_small")
    for tree, buf in zip((delta, new_m, new_v), outs):
        tree.update(zip(small_names, _unpack(buf, small_shapes)))

    grad_x = dx.reshape(nb, seq, d)
    return (loss, grad_x, *[grads[k] for k in WEIGHTS], *[delta[k] for k in WEIGHTS], *[new_m[k] for k in WEIGHTS],
            *[new_v[k] for k in WEIGHTS])


def kernel(x, mem, norm_mix, norm_xattn, norm_ffn, norm_mem, norm_final, ab_w_in, pool_w, pool_scale, ab_w_out, ssm_w_in, ssm_lam_re, ssm_lam_im, ssm_log_dt, ssm_b_re, ssm_b_im, ssm_c_re, ssm_c_im, ssm_d, ssm_w_glu, xa_w_q, xa_w_kv, xa_w_o, ffn_w_up, ffn_conv_w, ffn_conv_b, ffn_w_down, loss_target, m_norm_mix, m_norm_xattn, m_norm_ffn, m_norm_mem, m_norm_final, m_ab_w_in, m_pool_w, m_pool_scale, m_ab_w_out, m_ssm_w_in, m_ssm_lam_re, m_ssm_lam_im, m_ssm_log_dt, m_ssm_b_re, m_ssm_b_im, m_ssm_c_re, m_ssm_c_im, m_ssm_d, m_ssm_w_glu, m_xa_w_q, m_xa_w_kv, m_xa_w_o, m_ffn_w_up, m_ffn_conv_w, m_ffn_conv_b, m_ffn_w_down, v_norm_mix, v_norm_xattn, v_norm_ffn, v_norm_mem, v_norm_final, v_ab_w_in, v_pool_w, v_pool_scale, v_ab_w_out, v_ssm_w_in, v_ssm_lam_re, v_ssm_lam_im, v_ssm_log_dt, v_ssm_b_re, v_ssm_b_im, v_ssm_c_re, v_ssm_c_im, v_ssm_d, v_ssm_w_glu, v_xa_w_q, v_xa_w_kv, v_xa_w_o, v_ffn_w_up, v_ffn_conv_w, v_ffn_conv_b, v_ffn_w_down):
    args = dict(locals())
    w = {k: args[k] for k in WEIGHTS}
    m = {k: args["m_" + k] for k in WEIGHTS}
    v = {k: args["v_" + k] for k in WEIGHTS}
    return _step(x, mem, loss_target, w, m, v)
```

```python
import functools
import math

import jax
import jax.numpy as jnp
from jax import lax
from jax.experimental import pallas as pl
from jax.experimental.pallas import tpu as pltpu

f32 = jnp.float32
bf16 = jnp.bfloat16
SDS = jax.ShapeDtypeStruct
MESH = pl.DeviceIdType.MESH
ANY = pl.BlockSpec(memory_space=pl.ANY)

SB_HEAD_DIM = 64
POOL_WINDOWS = (2, 4, 8, 16)
POOL_GROUP = 128
XA_HEADS = 4
SSM_GROUPS = 64
SSM_GROUP = 16
SSM_STATE = 64
EPS = 1e-6
ADAM_LR, ADAM_B1, ADAM_B2, ADAM_EPS, ADAM_WD, ADAM_STEP = 0.001, 0.9, 0.999, 1e-08, 0.01, 10

LANES = 128
SUBLANES = 8
N_CHIPS = 4
VMEM_LIMIT = 56 * 1024 * 1024

NN = ((1,), (0,))
NT = ((1,), (1,))
TN = ((0,), (0,))


def _dot(a, b, dims):
    return lax.dot_general(a, b, (dims, ((), ())), preferred_element_type=f32)


def _params(n_grid):
    return pltpu.CompilerParams(dimension_semantics=("arbitrary",) * n_grid, vmem_limit_bytes=VMEM_LIMIT)


def _sum8(x):
    r, n = x.shape
    return jnp.sum(x.reshape(r // SUBLANES, SUBLANES, n), axis=0)


def _split_bf16(x):
    hi = x.astype(bf16)
    lo = (x - hi.astype(f32)).astype(bf16)
    return hi, lo


def _sigmoid(x):
    return 1.0 / (1.0 + jnp.exp(-x))


MM_BM = (1024, 1408, 512, 256, 128)
MM_BN = (1536, 1408, 1024, 512, 256, 128)
MM_BK = (2816, 2048, 1024, 512)


def _divisor(n, cands):
    return next((c for c in cands if n % c == 0), n)


def _mm(a, b, *, mode, name, out_dtype, bm=None, bn=None, bk=None, a_l=None, b_l=None, b_n0=0, n=None,
        res=None, out_l=None, out_layers=None, out_prev=None):
    dims = {"nn": NN, "nt": NT, "tn": TN}[mode]
    a2, b2 = a.shape[-2:], b.shape[-2:]
    if mode == "nn":
        (m, k), nfull = a2, b2[1]
    elif mode == "nt":
        (m, k), nfull = a2, b2[0]
    else:
        (k, m), nfull = a2, b2[1]
    n = nfull if n is None else n
    bm = _divisor(m, MM_BM) if bm is None else min(bm, m)
    bn = _divisor(n, MM_BN) if bn is None else min(bn, n)
    if bk is None:
        bk = _divisor(k, (1024, 512)) if mode == "tn" else (k if k <= MM_BK[0] else _divisor(k, MM_BK))
    bk = min(bk, k)
    assert m % bm == 0 and n % bn == 0 and k % bk == 0 and b_n0 % bn == 0, (name, m, n, k, bm, bn, bk)
    nk, n0b = k // bk, b_n0 // bn
    a_bytes, b_bytes = m * k * a.dtype.itemsize, k * n * b.dtype.itemsize
    rows_outer = a_bytes + b_bytes * (m // bm) <= b_bytes + a_bytes * (n // bn)

    def with_layer(layer, blk, idx_fn):
        def idx(g0, g1, kk):
            i, j = (g0, g1) if rows_outer else (g1, g0)
            return idx_fn(i, j, kk) if layer is None else (layer,) + idx_fn(i, j, kk)
        return pl.BlockSpec(blk if layer is None else (None,) + blk, idx)

    if mode == "tn":
        a_spec = with_layer(a_l, (bk, bm), lambda i, j, kk: (kk, i))
    else:
        a_spec = with_layer(a_l, (bm, bk), lambda i, j, kk: (i, kk))
    if mode == "nt":
        b_spec = with_layer(b_l, (bn, bk), lambda i, j, kk: (j, kk))
    else:
        b_spec = with_layer(b_l, (bk, bn), lambda i, j, kk: (kk, j + n0b))
    o_spec = with_layer(out_l, (bm, bn), lambda i, j, kk: (i, j))
    ins, in_specs = [a, b], [a_spec, b_spec]
    if res is not None:
        ins.append(res)
        in_specs.append(with_layer(None, (bm, bn), lambda i, j, kk: (i, j)))
    aliases = {}
    if out_prev is not None:
        aliases = {len(ins): 0}
        ins.append(out_prev)
        in_specs.append(ANY)
    has_res, has_prev = res is not None, out_prev is not None

    def body(*refs):
        a_ref, b_ref = refs[0], refs[1]
        res_ref = refs[2] if has_res else None
        o_ref = refs[2 + has_res + has_prev]
        part = _dot(a_ref[...].astype(bf16), b_ref[...].astype(bf16), dims)

        def finish(r):
            if has_res:
                r = r + res_ref[...]
            o_ref[...] = r.astype(o_ref.dtype)

        if nk == 1:
            finish(part)
        else:
            acc_ref = refs[-1]
            kk = pl.program_id(2)

            @pl.when(kk == 0)
            def _():
                acc_ref[...] = part

            @pl.when(kk > 0)
            def _():
                acc_ref[...] += part

            @pl.when(kk == nk - 1)
            def _():
                finish(acc_ref[...])

    out_shape = SDS((m, n) if out_l is None else (out_layers, m, n), out_dtype)
    grid = (m // bm, n // bn, nk) if rows_outer else (n // bn, m // bm, nk)
    return pl.pallas_call(
        body, out_shape=out_shape, grid=grid, in_specs=in_specs, out_specs=o_spec,
        scratch_shapes=[] if nk == 1 else [pltpu.VMEM((bm, bn), f32)],
        input_output_aliases=aliases, name=name, compiler_params=_params(3))(*ins)


def _rowwise(fn, row_ins, full_ins, row_outs, acc_outs, *, name, br=256):
    t = row_ins[0].shape[0]
    br = next(b for b in (br, 128, 64, 32, 16, 8, t) if b <= t and t % b == 0)
    nr, nf, no = len(row_ins), len(full_ins), len(row_outs)

    def body(*refs):
        rv = [r[...] for r in refs[:nr]]
        fv = [r[...] for r in refs[nr:nr + nf]]
        o_refs = refs[nr + nf:nr + nf + no]
        a_refs = refs[nr + nf + no:]
        outs, accs = fn(rv, fv)
        for o_ref, v in zip(o_refs, outs):
            o_ref[...] = v.astype(o_ref.dtype)
        if a_refs:
            i = pl.program_id(0)

            @pl.when(i == 0)
            def _():
                for a_ref, v in zip(a_refs, accs):
                    a_ref[...] = v

            @pl.when(i > 0)
            def _():
                for a_ref, v in zip(a_refs, accs):
                    a_ref[...] += v

    in_specs = [pl.BlockSpec((br, x.shape[1]), lambda i: (i, 0)) for x in row_ins]
    in_specs += [pl.BlockSpec(x.shape, lambda i, nd=x.ndim: (0,) * nd) for x in full_ins]
    out_specs = [pl.BlockSpec((br, s.shape[1]), lambda i: (i, 0)) for s in row_outs]
    out_specs += [pl.BlockSpec(s.shape, lambda i: (0, 0)) for s in acc_outs]
    res = pl.pallas_call(body, out_shape=tuple(row_outs) + tuple(acc_outs), grid=(t // br,), in_specs=in_specs,
                         out_specs=tuple(out_specs), name=name, compiler_params=_params(1))(*row_ins, *full_ins)
    return res


def _norm_fwd(x, g, name):
    def fn(rv, fv):
        (xv,), (gv,) = rv, fv
        r = lax.rsqrt(jnp.mean(xv * xv, axis=1, keepdims=True) + EPS)
        return [xv * r * gv], []
    return _rowwise(fn, [x], [g], [SDS(x.shape, bf16)], [], name=name)[0]


def _norm_bwd(dh, x, dres, g, name):
    d = x.shape[1]

    def fn(rv, fv):
        (dhv, xv, drv), (gv,) = rv, fv
        r = lax.rsqrt(jnp.mean(xv * xv, axis=1, keepdims=True) + EPS)
        xh = xv * r
        dxh = dhv * gv
        dx = drv + r * (dxh - xh * jnp.mean(dxh * xh, axis=1, keepdims=True))
        return [dx], [_sum8(dhv * xh)]
    return _rowwise(fn, [dh, x, dres], [g], [SDS(x.shape, f32)], [SDS((SUBLANES, d), f32)], name=name)


def _norm_bwd_gain_only(dh, x, name):
    d = x.shape[1]

    def fn(rv, fv):
        dhv, xv = rv
        r = lax.rsqrt(jnp.mean(xv * xv, axis=1, keepdims=True) + EPS)
        return [], [_sum8(dhv * xv * r)]
    return _rowwise(fn, [dh, x], [], [], [SDS((SUBLANES, d), f32)], name=name)[0]


def _loss_head(x, target, g, name):
    d = x.shape[1]

    def fn(rv, fv):
        (xv, tv), (gv,) = rv, fv
        r = lax.rsqrt(jnp.mean(xv * xv, axis=1, keepdims=True) + EPS)
        xh = xv * r
        err = xh * gv - tv
        dy = err * (1.0 / d)
        dxh = dy * gv
        dx = r * (dxh - xh * jnp.mean(dxh * xh, axis=1, keepdims=True))
        return [dx], [_sum8(dy * xh), _sum8(err * err)]
    return _rowwise(fn, [x, target], [g], [SDS(x.shape, f32)], [SDS((SUBLANES, d), f32), SDS((SUBLANES, d), f32)], name=name)


_GELU_C = math.sqrt(2.0 / math.pi)


def _gelu_fwd(y, name):
    def fn(rv, fv):
        (v,) = rv
        t = jnp.tanh(_GELU_C * (v + 0.044715 * v * v * v))
        return [0.5 * v * (1.0 + t)], []
    return _rowwise(fn, [y], [], [SDS(y.shape, bf16)], [], name=name)[0]


def _gelu_bwd(dg, y, name):
    def fn(rv, fv):
        dgv, v = rv
        t = jnp.tanh(_GELU_C * (v + 0.044715 * v * v * v))
        dt = (1.0 - t * t) * _GELU_C * (1.0 + 3.0 * 0.044715 * v * v)
        return [dgv * (0.5 * (1.0 + t) + 0.5 * v * dt)], []
    return _rowwise(fn, [dg, y], [], [SDS(y.shape, f32)], [], name=name)[0]


def _glu_fwd(glu, x, name):
    d = x.shape[1]

    def fn(rv, fv):
        gl, xv = rv
        return [xv + gl[:, :d] * _sigmoid(gl[:, d:])], []
    return _rowwise(fn, [glu, x], [], [SDS(x.shape, f32)], [], name=name)[0]


def _glu_bwd(dx, glu, name):
    d = dx.shape[1]

    def fn(rv, fv):
        dxv, gl = rv
        sg = _sigmoid(gl[:, d:])
        return [jnp.concatenate([dxv * sg, dxv * gl[:, :d] * sg * (1.0 - sg)], axis=1)], []
    return _rowwise(fn, [dx, glu], [], [SDS(glu.shape, bf16)], [], name=name)[0]


def _adamw(w, g, m, v, name):
    c1 = 1.0 - ADAM_B1 ** ADAM_STEP
    c2 = 1.0 - ADAM_B2 ** ADAM_STEP

    def fn(rv, fv):
        wv, gv, mv, vv = rv
        m2 = ADAM_B1 * mv + (1.0 - ADAM_B1) * gv
        v2 = ADAM_B2 * vv + (1.0 - ADAM_B2) * (gv * gv)
        delta = -ADAM_LR * ((m2 / c1) / (jnp.sqrt(v2 / c2) + ADAM_EPS) + ADAM_WD * wv)
        return [delta, m2, v2], []
    s = SDS(w.shape, f32)
    return _rowwise(fn, [w, g, m, v], [], [s, s, s], [], name=name)


SB_TQ = 128
SB_KB = 4


def _sb_logits(qh, kb, valid):
    z = _dot(qh, kb, NT) * (SB_HEAD_DIM ** -0.5)
    sp = jnp.log(1.0 + jnp.exp(-jnp.abs(z)))
    lb = jnp.minimum(z, 0.0) - sp
    lk_raw = jnp.minimum(-z, 0.0) - sp
    return lb, lk_raw, jnp.where(valid, lk_raw, 0.0)


def _sb_heads(q, t):
    lane = lax.broadcasted_iota(jnp.int32, (t, LANES), 1)
    masks = [(lane >= hh * SB_HEAD_DIM) & (lane < (hh + 1) * SB_HEAD_DIM) for hh in range(LANES // SB_HEAD_DIM)]
    return [(m, q * jnp.where(m, 1.0, 0.0).astype(bf16)) for m in masks]


def _sb_key_minus_query(t):
    return lax.broadcasted_iota(jnp.int32, (t, t), 1) - lax.broadcasted_iota(jnp.int32, (t, t), 0)


def _tri(t, op):
    row = lax.broadcasted_iota(jnp.int32, (t, t), 0)
    col = lax.broadcasted_iota(jnp.int32, (t, t), 1)
    return jnp.where(op(row, col), 1.0, 0.0).astype(bf16)


def _dot_split(x, u):
    hi, lo = _split_bf16(x)
    return _dot(hi, u, NN) + _dot(lo, u, NN)


def _sb_fwd(qkv, seq, name):
    t_all, w3 = qkv.shape
    w = w3 // 3
    hp, tq = w // LANES, SB_TQ
    nb, nq = t_all // seq, seq // tq
    kbn = min(SB_KB, nq)
    assert nq % kbn == 0

    def body(q_ref, k_ref, v_ref, o_ref, lt_ref):
        i = pl.program_id(2)
        heads = _sb_heads(q_ref[...], tq)
        kmq = _sb_key_minus_query(tq)
        u_after = _tri(tq, lambda r, c: r > c)
        n_it = (i + kbn) // kbn

        def step(it, carry):
            carry = list(carry)
            blocks = []
            for kk in reversed(range(kbn)):
                j = (n_it - 1 - it) * kbn + kk
                off = pl.multiple_of(j * tq, tq)
                blocks.append((k_ref[pl.ds(off, tq), :], v_ref[pl.ds(off, tq), :], kmq < (i - j) * tq))
            chains = [(hh, qh, kb, vb, valid) for kb, vb, valid in blocks for hh, (_, qh) in enumerate(heads)]
            zs = [_dot(qh, kb, NT) for _, qh, kb, _, _ in chains]
            lbs, his, los, sums = [], [], [], []
            for z, (_, _, _, _, valid) in zip(zs, chains):
                z = z * (SB_HEAD_DIM ** -0.5)
                sp = jnp.log(1.0 + jnp.exp(-jnp.abs(z)))
                lb = jnp.minimum(z, 0.0) - sp
                lk = jnp.where(valid, lb - z, 0.0)
                hi, lo = _split_bf16(lk)
                lbs.append(lb), his.append(hi), los.append(lo), sums.append(jnp.sum(lk, axis=1, keepdims=True))
            afts = [_dot(hi, u_after, NN) + _dot(lo, u_after, NN) for hi, lo in zip(his, los)]
            wgts = []
            for (hh, _, _, _, valid), lb, aft, sm in zip(chains, lbs, afts, sums):
                wgts.append(jnp.where(valid, jnp.exp(lb + (carry[2 * hh] + aft)), 0.0).astype(bf16))
                carry[2 * hh] = carry[2 * hh] + sm
            for (hh, _, _, vb, _), wgt in zip(chains, wgts):
                carry[2 * hh + 1] = carry[2 * hh + 1] + _dot(wgt, vb, NN)
            return tuple(carry)

        init = (jnp.zeros((tq, 1), f32), jnp.zeros((tq, LANES), f32)) * len(heads)
        fin = lax.fori_loop(0, n_it, step, init)
        out = jnp.zeros((tq, LANES), f32)
        ltot = jnp.zeros((tq, LANES), f32)
        for hh, (m, _) in enumerate(heads):
            out = out + jnp.where(m, fin[2 * hh + 1], 0.0)
            ltot = ltot + jnp.where(m, fin[2 * hh], 0.0)
        o_ref[...] = out
        lt_ref[...] = ltot

    return pl.pallas_call(
        body, out_shape=(SDS((t_all, 2 * w), f32), SDS((t_all, w), f32)), grid=(nb, hp, nq),
        in_specs=[pl.BlockSpec((tq, LANES), lambda b, p, i: (b * nq + i, p)),
                  pl.BlockSpec((seq, LANES), lambda b, p, i: (b, hp + p)),
                  pl.BlockSpec((seq, LANES), lambda b, p, i: (b, 2 * hp + p))],
        out_specs=(pl.BlockSpec((tq, LANES), lambda b, p, i: (b * nq + i, p)),
                   pl.BlockSpec((tq, LANES), lambda b, p, i: (b * nq + i, p))),
        name=name, compiler_params=_params(3))(qkv, qkv, qkv)


def _sb_bwd(qkv, ltot, dmix, seq, name):
    t_all, w3 = qkv.shape
    w = w3 // 3
    hp, tq = w // LANES, SB_TQ
    nb, nq = t_all // seq, seq // tq
    kbn = min(SB_KB, nq)
    assert nq % kbn == 0

    def body(q_ref, k_ref, v_ref, lt_ref, do_ref, dq_ref, dk_ref, dv_ref, dk_acc, dv_acc):
        i = pl.program_id(2)

        @pl.when(i == 0)
        def _():
            dk_acc[...] = jnp.zeros_like(dk_acc)
            dv_acc[...] = jnp.zeros_like(dv_acc)

        heads = _sb_heads(q_ref[...], tq)
        do = do_ref[...]
        ltv = lt_ref[...]
        dos = [jnp.where(m, do, 0.0).astype(bf16) for m, _ in heads]
        lts = [jnp.sum(jnp.where(m, ltv, 0.0), axis=1, keepdims=True) * (1.0 / SB_HEAD_DIM) for m, _ in heads]
        kmq = _sb_key_minus_query(tq)
        u_incl = _tri(tq, lambda r, c: r <= c)
        u_excl = _tri(tq, lambda r, c: r < c)
        n_it = (i + kbn) // kbn

        def step(it, carry):
            carry = list(carry)
            blocks = []
            for kk in range(kbn):
                off = pl.multiple_of((it * kbn + kk) * tq, tq)
                blocks.append((off, k_ref[pl.ds(off, tq), :], v_ref[pl.ds(off, tq), :], kmq < (i - (it * kbn + kk)) * tq))
            chains = [(hh, qh, kb, vb, valid) for _, kb, vb, valid in blocks for hh, (_, qh) in enumerate(heads)]
            zs = [_dot(qh, kb, NT) for _, qh, kb, _, _ in chains]
            dws = [_dot(dos[hh], vb, NT) for hh, _, _, vb, _ in chains]
            lbs, lkrs, his, los, sums = [], [], [], [], []
            for z, (_, _, _, _, valid) in zip(zs, chains):
                z = z * (SB_HEAD_DIM ** -0.5)
                sp = jnp.log(1.0 + jnp.exp(-jnp.abs(z)))
                lb = jnp.minimum(z, 0.0) - sp
                lk_raw = lb - z
                lk = jnp.where(valid, lk_raw, 0.0)
                hi, lo = _split_bf16(lk)
                lbs.append(lb), lkrs.append(lk_raw), his.append(hi), los.append(lo)
                sums.append(jnp.sum(lk, axis=1, keepdims=True))
            pins = [_dot(hi, u_incl, NN) + _dot(lo, u_incl, NN) for hi, lo in zip(his, los)]
            wbs, gs, ghis, glos, gpres = [], [], [], [], []
            for (hh, _, _, _, valid), lb, pin, sm, dw in zip(chains, lbs, pins, sums, dws):
                wgt = jnp.where(valid, jnp.exp(lb + (lts[hh] - (carry[3 * hh] + pin))), 0.0)
                carry[3 * hh] = carry[3 * hh] + sm
                g = dw * wgt
                hi, lo = _split_bf16(g)
                wbs.append(wgt.astype(bf16)), gs.append(g), ghis.append(hi), glos.append(lo)
                gpres.append(carry[3 * hh + 1])
                carry[3 * hh + 1] = carry[3 * hh + 1] + jnp.sum(g, axis=1, keepdims=True)
            gins = [_dot(hi, u_excl, NN) + _dot(lo, u_excl, NN) for hi, lo in zip(ghis, glos)]
            dzbs = []
            for (_, _, _, _, valid), lb, lk_raw, g, gpre, gin in zip(chains, lbs, lkrs, gs, gpres, gins):
                dz = jnp.where(valid, g * jnp.exp(lk_raw) - (gpre + gin) * jnp.exp(lb), 0.0) * (SB_HEAD_DIM ** -0.5)
                dzbs.append(dz.astype(bf16))
            for (hh, _, kb, _, _), dzb in zip(chains, dzbs):
                carry[3 * hh + 2] = carry[3 * hh + 2] + _dot(dzb, kb, NN)
            nh = len(heads)
            for bi, (off, _, _, _) in enumerate(blocks):
                dk_j = jnp.zeros((tq, LANES), f32)
                dv_j = jnp.zeros((tq, LANES), f32)
                for hh, (_, qh) in enumerate(heads):
                    dk_j = dk_j + _dot(dzbs[bi * nh + hh], qh, TN)
                    dv_j = dv_j + _dot(wbs[bi * nh + hh], dos[hh], TN)
                dk_acc[pl.ds(off, tq), :] += dk_j
                dv_acc[pl.ds(off, tq), :] += dv_j
            return tuple(carry)

        zero1 = jnp.zeros((tq, 1), f32)
        fin = lax.fori_loop(0, n_it, step, (zero1, zero1, jnp.zeros((tq, LANES), f32)) * len(heads))
        dq_all = jnp.zeros((tq, LANES), f32)
        for hh, (m, _) in enumerate(heads):
            dq_all = dq_all + jnp.where(m, fin[3 * hh + 2], 0.0)
        dq_ref[...] = dq_all.astype(bf16)

        @pl.when(i == nq - 1)
        def _():
            dk_ref[...] = dk_acc[...].astype(bf16)
            dv_ref[...] = dv_acc[...].astype(bf16)

    row_blk = pl.BlockSpec((tq, LANES), lambda b, p, i: (b * nq + i, p))
    seq_blk = pl.BlockSpec((seq, LANES), lambda b, p, i: (b, p))
    out = SDS((t_all, w), bf16)
    return pl.pallas_call(
        body, out_shape=(out, out, out), grid=(nb, hp, nq),
        in_specs=[row_blk,
                  pl.BlockSpec((seq, LANES), lambda b, p, i: (b, hp + p)),
                  pl.BlockSpec((seq, LANES), lambda b, p, i: (b, 2 * hp + p)),
                  row_blk, row_blk],
        out_specs=(row_blk, seq_blk, seq_blk),
        scratch_shapes=[pltpu.VMEM((seq, LANES), f32), pltpu.VMEM((seq, LANES), f32)],
        name=name, compiler_params=_params(3))(qkv, qkv, qkv, ltot, dmix)


POOL_CHUNK = 256
POOL_HALO = 16


def _band(rows, cols, lo, hi):
    r = lax.broadcasted_iota(jnp.int32, (rows, cols), 0)
    c = lax.broadcasted_iota(jnp.int32, (rows, cols), 1)
    d = c - r
    return jnp.where((d >= lo) & (d < hi), 1.0, 0.0).astype(bf16)


def _pool_counts(r0, rows, win):
    t = lax.broadcasted_iota(jnp.int32, (rows, 1), 0) + r0
    return jnp.minimum(t + 1, win).astype(f32)


def _pool_fwd(u, mix, pool_w, scale, seq, name):
    t_all, w = u.shape
    ng, rc = w // POOL_GROUP, min(POOL_CHUNK, seq)

    def body(u_ref, w_ref, s_ref, mix_in, p_ref, o_ref, pad):
        del mix_in
        pad[0:POOL_HALO, :] = jnp.zeros((POOL_HALO, POOL_GROUP), f32)
        for g in range(ng):
            cols = slice(g * POOL_GROUP, (g + 1) * POOL_GROUP)
            win = POOL_WINDOWS[g]
            pad[POOL_HALO:POOL_HALO + seq, :] = u_ref[:, cols]
            band = _band(rc, rc + POOL_HALO, POOL_HALO - win + 1, POOL_HALO + 1)
            wg = w_ref[g].astype(bf16)
            for r0 in range(0, seq, rc):
                ue = pad[r0:r0 + rc + POOL_HALO, :]
                hi, lo = _split_bf16(ue)
                sm = _dot(band, hi, NN) + _dot(band, lo, NN)
                pch = sm / _pool_counts(r0, rc, win) - ue[POOL_HALO:, :]
                pb = pch.astype(bf16)
                p_ref[r0:r0 + rc, cols] = pb
                o_ref[r0:r0 + rc, cols] = _dot(pb, wg, NN) * s_ref[:, cols]

    return pl.pallas_call(
        body, out_shape=(SDS((t_all, w), bf16), SDS(mix.shape, f32)), grid=(t_all // seq,),
        in_specs=[pl.BlockSpec((seq, w), lambda b: (b, 0)), pl.BlockSpec(pool_w.shape, lambda b: (0, 0, 0)),
                  pl.BlockSpec(scale.shape, lambda b: (0, 0)), ANY],
        out_specs=(pl.BlockSpec((seq, w), lambda b: (b, 0)), pl.BlockSpec((seq, w), lambda b: (b, 1))),
        scratch_shapes=[pltpu.VMEM((seq + POOL_HALO, POOL_GROUP), f32)],
        input_output_aliases={3: 1}, name=name, compiler_params=_params(1))(u, pool_w, scale, mix)


def _pool_bwd(dmix, p, pool_w, scale, seq, name):
    t_all, w = p.shape
    ng, rc = w // POOL_GROUP, min(POOL_CHUNK, seq)

    def body(dy_ref, p_ref, w_ref, s_ref, du_ref, dw_ref, ds_ref, dpn, dpr):
        b = pl.program_id(0)

        @pl.when(b == 0)
        def _():
            dw_ref[...] = jnp.zeros_like(dw_ref)
            ds_ref[...] = jnp.zeros_like(ds_ref)

        dpn[seq:seq + POOL_HALO, :] = jnp.zeros((POOL_HALO, POOL_GROUP), f32)
        for g in range(ng):
            cols = slice(g * POOL_GROUP, (g + 1) * POOL_GROUP)
            win = POOL_WINDOWS[g]
            wg = w_ref[g].astype(bf16)
            sg = s_ref[:, cols]
            dwg = jnp.zeros((POOL_GROUP, POOL_GROUP), f32)
            dsg = jnp.zeros((SUBLANES, POOL_GROUP), f32)
            for r0 in range(0, seq, rc):
                dy = dy_ref[r0:r0 + rc, cols]
                pb = p_ref[r0:r0 + rc, cols]
                dsg = dsg + _sum8(dy * _dot(pb, wg, NN))
                dyw = (dy * sg).astype(bf16)
                dwg = dwg + _dot(pb, dyw, TN)
                dp = _dot(dyw, wg, NT)
                dpr[r0:r0 + rc, :] = dp
                dpn[r0:r0 + rc, :] = dp / _pool_counts(r0, rc, win)
            dw_ref[g] += dwg
            ds_ref[:, cols] += dsg
            band = _band(rc, rc + POOL_HALO, 0, win)
            for r0 in range(0, seq, rc):
                hi, lo = _split_bf16(dpn[r0:r0 + rc + POOL_HALO, :])
                du = _dot(band, hi, NN) + _dot(band, lo, NN) - dpr[r0:r0 + rc, :]
                du_ref[r0:r0 + rc, cols] = du.astype(bf16)

    return pl.pallas_call(
        body, out_shape=(SDS((t_all, w), bf16), SDS(pool_w.shape, f32), SDS((SUBLANES, w), f32)), grid=(t_all // seq,),
        in_specs=[pl.BlockSpec((seq, w), lambda b: (b, 1)), pl.BlockSpec((seq, w), lambda b: (b, 0)),
                  pl.BlockSpec(pool_w.shape, lambda b: (0, 0, 0)), pl.BlockSpec(scale.shape, lambda b: (0, 0))],
        out_specs=(pl.BlockSpec((seq, w), lambda b: (b, 0)), pl.BlockSpec(pool_w.shape, lambda b: (0, 0, 0)),
                   pl.BlockSpec((SUBLANES, w), lambda b: (0, 0))),
        scratch_shapes=[pltpu.VMEM((seq + POOL_HALO, POOL_GROUP), f32), pltpu.VMEM((seq, POOL_GROUP), f32)],
        name=name, compiler_params=_params(1))(dmix, p, pool_w, scale)


XA_TQ = 256


def _xa_probs(qh, kh, dh):
    s = _dot(qh, kh, NT) * (dh ** -0.5)
    e = jnp.exp(s - jnp.max(s, axis=1, keepdims=True))
    return e / jnp.sum(e, axis=1, keepdims=True)


def _xa_fwd(q, kv, seq, name):
    t_all, d = q.shape
    nb = t_all // seq
    mem, dh, tq = kv.shape[0] // nb, d // XA_HEADS, min(XA_TQ, seq)
    nq = seq // tq

    def body(q_ref, kv_ref, o_ref):
        for h in range(XA_HEADS):
            cols = slice(h * dh, (h + 1) * dh)
            p = _xa_probs(q_ref[:, cols], kv_ref[:, cols], dh)
            o_ref[:, cols] = _dot(p.astype(bf16), kv_ref[:, d + h * dh:d + (h + 1) * dh], NN).astype(bf16)

    return pl.pallas_call(
        body, out_shape=SDS((t_all, d), bf16), grid=(nb, nq),
        in_specs=[pl.BlockSpec((tq, d), lambda b, i: (b * nq + i, 0)), pl.BlockSpec((mem, 2 * d), lambda b, i: (b, 0))],
        out_specs=pl.BlockSpec((tq, d), lambda b, i: (b * nq + i, 0)), name=name, compiler_params=_params(2))(q, kv)


def _xa_bwd(q, kv, do, seq, name):
    t_all, d = q.shape
    nb = t_all // seq
    mem, dh, tq = kv.shape[0] // nb, d // XA_HEADS, min(XA_TQ, seq)
    nq = seq // tq

    def body(q_ref, kv_ref, do_ref, dq_ref, dkv_ref):
        i = pl.program_id(1)

        @pl.when(i == 0)
        def _():
            dkv_ref[...] = jnp.zeros_like(dkv_ref)

        for h in range(XA_HEADS):
            cols = slice(h * dh, (h + 1) * dh)
            vcols = slice(d + h * dh, d + (h + 1) * dh)
            qh, kh, doh = q_ref[:, cols], kv_ref[:, cols], do_ref[:, cols]
            p = _xa_probs(qh, kh, dh)
            dkv_ref[:, vcols] += _dot(p.astype(bf16), doh, TN)
            dp = _dot(doh, kv_ref[:, vcols], NT)
            ds = (p * (dp - jnp.sum(dp * p, axis=1, keepdims=True)) * (dh ** -0.5)).astype(bf16)
            dq_ref[:, cols] = _dot(ds, kh, NN).astype(bf16)
            dkv_ref[:, cols] += _dot(ds, qh, TN)

    row = pl.BlockSpec((tq, d), lambda b, i: (b * nq + i, 0))
    kvs = pl.BlockSpec((mem, 2 * d), lambda b, i: (b, 0))
    return pl.pallas_call(body, out_shape=(SDS((t_all, d), bf16), SDS(kv.shape, f32)), grid=(nb, nq),
                          in_specs=[row, kvs, row], out_specs=(row, kvs), name=name, compiler_params=_params(2))(q, kv, do)


FFN_BR = 256
FFN_CHUNK = 256


def _conv3(ext, w_ref, b, cols, lo, rows):
    return (b + w_ref[2:3, cols] * ext[lo:lo + rows, :] + w_ref[1:2, cols] * ext[lo - 1:lo - 1 + rows, :]
            + w_ref[0:1, cols] * ext[lo - 2:lo - 2 + rows, :])


FFN_HALO = 16


def _ffn_gate_fwd(up, cw, cb, seq, name):
    t_all, f2 = up.shape
    ff, br, ch, hl = f2 // 2, min(FFN_BR, seq), FFN_CHUNK, FFN_HALO
    per_seq, hb = seq // br, br // hl

    def body(up_ref, halo_ref, cw_ref, cb_ref, o_ref, ev, eg):
        i = pl.program_id(0)
        keep = jnp.where(i % per_seq == 0, 0.0, 1.0)
        for c0 in range(0, ff, ch):
            convs = []
            for ext, off in ((ev, c0), (eg, ff + c0)):
                cols = slice(off, off + ch)
                ext[0:hl, :] = halo_ref[:, cols].astype(f32) * keep
                ext[hl:hl + br, :] = up_ref[:, cols].astype(f32)
                convs.append(_conv3(ext, cw_ref, cb_ref[:, cols], cols, hl, br))
            val, gate = convs
            o_ref[:, c0:c0 + ch] = (gate * _sigmoid(gate) * val).astype(bf16)

    return pl.pallas_call(
        body, out_shape=SDS((t_all, ff), bf16), grid=(t_all // br,),
        in_specs=[pl.BlockSpec((br, f2), lambda i: (i, 0)),
                  pl.BlockSpec((hl, f2), lambda i: (jnp.maximum(i * hb - 1, 0), 0)),
                  pl.BlockSpec(cw.shape, lambda i: (0, 0)), pl.BlockSpec(cb.shape, lambda i: (0, 0))],
        out_specs=pl.BlockSpec((br, ff), lambda i: (i, 0)),
        scratch_shapes=[pltpu.VMEM((br + hl, ch), f32), pltpu.VMEM((br + hl, ch), f32)],
        name=name, compiler_params=_params(1))(up, up, cw, cb)


def _ffn_gate_bwd(dact, up, cw, cb, seq, name):
    t_all, f2 = up.shape
    ff, br, ch, hl = f2 // 2, min(FFN_BR, seq), FFN_CHUNK, FFN_HALO
    per_seq, hb, last = seq // br, br // hl, t_all // hl - 1
    ext_rows = br + SUBLANES

    def body(da_ref, dan_ref, up_ref, upp_ref, upn_ref, cw_ref, cb_ref, du_ref, dcw_ref, dcb_ref, uv, ug, dav, dcv, dcg):
        i = pl.program_id(0)

        @pl.when(i == 0)
        def _():
            dcw_ref[...] = jnp.zeros_like(dcw_ref)
            dcb_ref[...] = jnp.zeros_like(dcb_ref)

        keep_prev = jnp.where(i % per_seq == 0, 0.0, 1.0)
        keep_next = jnp.where((i + 1) % per_seq == 0, 0.0, 1.0)
        for c0 in range(0, ff, ch):
            convs = []
            for ext, off in ((uv, c0), (ug, ff + c0)):
                cols = slice(off, off + ch)
                ext[0:hl, :] = upp_ref[:, cols].astype(f32) * keep_prev
                ext[hl:hl + br, :] = up_ref[:, cols].astype(f32)
                ext[hl + br:2 * hl + br, :] = upn_ref[:, cols].astype(f32) * keep_next
                convs.append(_conv3(ext, cw_ref, cb_ref[:, cols], cols, hl, ext_rows))
            val, gate = convs
            dav[0:br, :] = da_ref[:, c0:c0 + ch].astype(f32)
            dav[br:br + hl, :] = dan_ref[:, c0:c0 + ch].astype(f32) * keep_next
            da = dav[0:ext_rows, :]
            sg = _sigmoid(gate)
            dcv[...] = da * gate * sg
            dcg[...] = da * val * sg * (1.0 + gate * (1.0 - sg))
            for ext, dc, off in ((uv, dcv, c0), (ug, dcg, ff + c0)):
                cols = slice(off, off + ch)
                du = (cw_ref[2:3, cols] * dc[0:br, :] + cw_ref[1:2, cols] * dc[1:br + 1, :]
                      + cw_ref[0:1, cols] * dc[2:br + 2, :])
                du_ref[:, cols] = du.astype(bf16)
                d0 = dc[0:br, :]
                dcb_ref[:, cols] += _sum8(d0)
                for tap in range(3):
                    lo = hl - (2 - tap)
                    dcw_ref[tap, :, cols] += _sum8(d0 * ext[lo:lo + br, :])

    blk = lambda n: pl.BlockSpec((br, n), lambda i: (i, 0))
    prev = lambda n: pl.BlockSpec((hl, n), lambda i: (jnp.maximum(i * hb - 1, 0), 0))
    nxt = lambda n: pl.BlockSpec((hl, n), lambda i: (jnp.minimum((i + 1) * hb, last), 0))
    return pl.pallas_call(
        body, out_shape=(SDS((t_all, f2), bf16), SDS((3, SUBLANES, f2), f32), SDS((SUBLANES, f2), f32)), grid=(t_all // br,),
        in_specs=[blk(ff), nxt(ff), blk(f2), prev(f2), nxt(f2), pl.BlockSpec(cw.shape, lambda i: (0, 0)),
                  pl.BlockSpec(cb.shape, lambda i: (0, 0))],
        out_specs=(blk(f2), pl.BlockSpec((3, SUBLANES, f2), lambda i: (0, 0, 0)), pl.BlockSpec((SUBLANES, f2), lambda i: (0, 0))),
        scratch_shapes=[pltpu.VMEM((br + 2 * hl, ch), f32), pltpu.VMEM((br + 2 * hl, ch), f32),
                        pltpu.VMEM((br + hl, ch), f32), pltpu.VMEM((ext_rows, ch), f32), pltpu.VMEM((ext_rows, ch), f32)],
        name=name, compiler_params=_params(1))(dact, dact, up, up, up, cw, cb)


SSM_GB = 8
SSM_PLANES = 8
SSM_ROWS = 256
SSM_UNROLL = 8


def _ssm_pitch(seq):
    p = seq + SUBLANES
    assert (p // SUBLANES) % 2 == 1
    return p


def _rows(base, rc):
    return pl.ds(pl.multiple_of(base + rc * SSM_ROWS, SUBLANES), SSM_ROWS)


def _ssm_project_in(u_ref, b_ref, planes, e, seq, pitch):
    def chunk(rc, _):
        uc = u_ref[_rows(e * seq, rc), :].astype(bf16)
        for j in range(SSM_PLANES):
            planes[_rows(j * pitch, rc), :] = _dot(uc, b_ref[:, j * LANES:(j + 1) * LANES], NN)
        return 0
    lax.fori_loop(0, seq // SSM_ROWS, chunk, 0)


def _ssm_rows(planes, rc, pitch):
    return jnp.concatenate([planes[_rows(j * pitch, rc), :].astype(bf16) for j in range(SSM_PLANES)], axis=1)


def _ssm_scan(planes_list, l1, l2, seq, pitch, reverse=False):
    def step(s, hs):
        t = seq - 1 - s if reverse else s
        out = []
        for planes, h in zip(planes_list, hs):
            h = h * l1 + pltpu.roll(h, 4, 0) * l2 + planes[pl.ds(t, SUBLANES, stride=pitch), :]
            planes[pl.ds(t, SUBLANES, stride=pitch), :] = h
            out.append(h)
        return tuple(out)
    zero = jnp.zeros((SUBLANES, LANES), f32)
    lax.fori_loop(0, seq, step, tuple(zero for _ in planes_list), unroll=SSM_UNROLL)


def _ssm_fwd(u, b_big, c_big, lslab, dskip, seq, name):
    t_all, w = u.shape
    nb, gw, pitch = t_all // seq, SSM_GB * SSM_GROUP, _ssm_pitch(seq)
    assert gw == LANES

    def body(u_ref, b_ref, c_ref, l_ref, d_ref, y_ref, *planes):
        l1, l2 = l_ref[0:SUBLANES, :], l_ref[SUBLANES:2 * SUBLANES, :]
        for e in range(nb):
            _ssm_project_in(u_ref, b_ref, planes[e], e, seq, pitch)
        _ssm_scan(planes, l1, l2, seq, pitch)
        for e in range(nb):
            def chunk(rc, _, e=e):
                rows = _rows(e * seq, rc)
                y_ref[rows, :] = _dot(_ssm_rows(planes[e], rc, pitch), c_ref[...], NN) + d_ref[...] * u_ref[rows, :]
                return 0
            lax.fori_loop(0, seq // SSM_ROWS, chunk, 0)

    return pl.pallas_call(
        body, out_shape=SDS((t_all, w), f32), grid=(w // gw,),
        in_specs=[pl.BlockSpec((t_all, gw), lambda k: (0, k)), pl.BlockSpec((None,) + b_big.shape[1:], lambda k: (k, 0, 0)),
                  pl.BlockSpec((None,) + c_big.shape[1:], lambda k: (k, 0, 0)),
                  pl.BlockSpec((None,) + lslab.shape[1:], lambda k: (k, 0, 0)), pl.BlockSpec((1, gw), lambda k: (0, k))],
        out_specs=pl.BlockSpec((t_all, gw), lambda k: (0, k)),
        scratch_shapes=[pltpu.VMEM((SSM_PLANES * pitch, LANES), f32) for _ in range(nb)],
        name=name, compiler_params=_params(1))(u, b_big, c_big, lslab, dskip)


def _ssm_bwd(u, dy, b_big, c_big, lslab, dskip, seq, name):
    t_all, w = u.shape
    nb, gw, pitch = t_all // seq, SSM_GB * SSM_GROUP, _ssm_pitch(seq)
    ns = SSM_PLANES * LANES

    def body(u_ref, dy_ref, b_ref, c_ref, l_ref, d_ref, du_ref, db_ref, dc_ref, dl_ref, dd_ref, *planes):
        hp, ap = planes[:nb], planes[nb:]
        l1, l2 = l_ref[0:SUBLANES, :], l_ref[SUBLANES:2 * SUBLANES, :]
        for e in range(nb):
            _ssm_project_in(u_ref, b_ref, hp[e], e, seq, pitch)
        _ssm_scan(hp, l1, l2, seq, pitch)
        dd_ref[...] = jnp.zeros_like(dd_ref)
        dc_ref[...] = jnp.zeros_like(dc_ref)
        db_ref[...] = jnp.zeros_like(db_ref)
        for e in range(nb):
            def chunk(rc, _, e=e):
                rows = _rows(e * seq, rc)
                dyc = dy_ref[rows, :]
                dyb = dyc.astype(bf16)
                for j in range(SSM_PLANES):
                    ap[e][_rows(j * pitch, rc), :] = _dot(dyb, c_ref[j * LANES:(j + 1) * LANES, :], NT)
                dd_ref[...] += _sum8(dyc * u_ref[rows, :])
                dc_ref[...] += _dot(_ssm_rows(hp[e], rc, pitch), dyb, TN)
                return 0
            lax.fori_loop(0, seq // SSM_ROWS, chunk, 0)

        def step(s, carry):
            t = seq - 1 - s
            out = []
            for e in range(nb):
                a, s1, s2 = carry[e]
                a = a * l1 - pltpu.roll(a, 4, 0) * l2 + ap[e][pl.ds(t, SUBLANES, stride=pitch), :]
                ap[e][pl.ds(t, SUBLANES, stride=pitch), :] = a
                hprev = hp[e][pl.ds(jnp.maximum(t - 1, 0), SUBLANES, stride=pitch), :] * jnp.where(t > 0, 1.0, 0.0)
                out.append((a, s1 + a * hprev, s2 + a * pltpu.roll(hprev, 4, 0)))
            return tuple(out)
        zero = jnp.zeros((SUBLANES, LANES), f32)
        fin = lax.fori_loop(0, seq, step, tuple((zero, zero, zero) for _ in range(nb)), unroll=SSM_UNROLL)
        dl_ref[0:SUBLANES, :] = sum(f[1] for f in fin)
        dl_ref[SUBLANES:2 * SUBLANES, :] = sum(f[2] for f in fin)

        for e in range(nb):
            def chunk2(rc, _, e=e):
                rows = _rows(e * seq, rc)
                ar = _ssm_rows(ap[e], rc, pitch)
                du_ref[rows, :] = (_dot(ar, b_ref[...], NT) + d_ref[...] * dy_ref[rows, :]).astype(bf16)
                db_ref[...] += _dot(u_ref[rows, :].astype(bf16), ar, TN)
                return 0
            lax.fori_loop(0, seq // SSM_ROWS, chunk2, 0)

    col = pl.BlockSpec((t_all, gw), lambda k: (0, k))
    per = lambda s: pl.BlockSpec((None,) + s[1:], lambda k: (k, 0, 0))
    ng = w // gw
    return pl.pallas_call(
        body, out_shape=(SDS((t_all, w), bf16), SDS(b_big.shape, f32), SDS(c_big.shape, f32), SDS((ng, 2 * SUBLANES, LANES), f32),
                         SDS((SUBLANES, w), f32)),
        grid=(ng,),
        in_specs=[col, col, per(b_big.shape), per(c_big.shape), per(lslab.shape), pl.BlockSpec((1, gw), lambda k: (0, k))],
        out_specs=(col, per(b_big.shape), per(c_big.shape), per((ng, 2 * SUBLANES, LANES)), pl.BlockSpec((SUBLANES, gw), lambda k: (0, k))),
        scratch_shapes=[pltpu.VMEM((SSM_PLANES * pitch, LANES), f32) for _ in range(2 * nb)],
        name=name, compiler_params=_params(1))(u, dy, b_big, c_big, lslab, dskip)


def _ssm_disc_fwd(lam_re, lam_im, dt, b_re, b_im, name):
    def body(a_ref, b_ref, dt_ref, br_ref, bi_ref, lr_ref, li_ref, cr_ref, ci_ref, bbr_ref, bbi_ref):
        a, b, dtv = a_ref[...], b_ref[...], dt_ref[...]
        mag, ang = jnp.exp(a * dtv), b * dtv
        lr, li = mag * jnp.cos(ang), mag * jnp.sin(ang)
        nr, den = lr - 1.0, a * a + b * b
        cr, ci = (nr * a + li * b) / den, (li * a - nr * b) / den
        lr_ref[...], li_ref[...], cr_ref[...], ci_ref[...] = lr, li, cr, ci
        bbr_ref[...] = cr * br_ref[...] - ci * bi_ref[...]
        bbi_ref[...] = cr * bi_ref[...] + ci * br_ref[...]
    c, m = SDS(lam_re.shape, f32), SDS(b_re.shape, f32)
    return pl.pallas_call(body, out_shape=(c, c, c, c, m, m), name=name)(lam_re, lam_im, dt, b_re, b_im)


def _ssm_disc_bwd(lam_re, lam_im, dt, b_re, b_im, g_lr, g_li, g_bbr, g_bbi, name):
    def body(a_ref, b_ref, dt_ref, br_ref, bi_ref, glr_ref, gli_ref, gbr_ref, gbi_ref, da_ref, db_ref, ddt_ref, dbr_ref, dbi_ref):
        a, b, dtv = a_ref[...], b_ref[...], dt_ref[...]
        mag, ang = jnp.exp(a * dtv), b * dtv
        cs, sn = jnp.cos(ang), jnp.sin(ang)
        lr, li = mag * cs, mag * sn
        nr, den = lr - 1.0, a * a + b * b
        cr, ci = (nr * a + li * b) / den, (li * a - nr * b) / den
        gbr, gbi, brv, biv = gbr_ref[...], gbi_ref[...], br_ref[...], bi_ref[...]
        dbr_ref[...] = cr * gbr + ci * gbi
        dbi_ref[...] = cr * gbi - ci * gbr
        dcr = jnp.sum(brv * gbr + biv * gbi, axis=1, keepdims=True)
        dci = jnp.sum(brv * gbi - biv * gbr, axis=1, keepdims=True)
        dnum_r, dnum_i = dcr / den, dci / den
        dden = -(dcr * cr + dci * ci) / den
        dnr = dnum_r * a - dnum_i * b
        dli = gli_ref[...] + dnum_r * b + dnum_i * a
        dlr = glr_ref[...] + dnr
        dmag, dang = dlr * cs + dli * sn, dli * lr - dlr * li
        dadt = dmag * mag
        da_ref[...] = dnum_r * nr + dnum_i * li + dden * 2.0 * a + dadt * dtv
        db_ref[...] = dnum_r * li - dnum_i * nr + dden * 2.0 * b + dang * dtv
        ddt_ref[...] = dadt * a + dang * b
    c, m = SDS(lam_re.shape, f32), SDS(b_re.shape, f32)
    return pl.pallas_call(body, out_shape=(c, c, c, m, m), name=name)(lam_re, lam_im, dt, b_re, b_im, g_lr, g_li, g_bbr, g_bbi)


def _place():
    x, y, c = lax.axis_index("x"), lax.axis_index("y"), lax.axis_index("c")
    return x, y, c, 2 * x + y


def _half_axis(shape, ax):
    return 0 if shape[0] == 2 else (3 - ax)


def _sub(ref, axis, start, size):
    idx = [slice(None)] * len(ref.shape)
    idx[axis] = pl.ds(start, size)
    return ref.at[tuple(idx)]


def _region(ref, full_shape, ax, slot=None, half=None):
    if slot is not None:
        n = full_shape[ax] // N_CHIPS
        ref = _sub(ref, ax, slot * n, n)
    if half is not None:
        ha = _half_axis(full_shape, ax)
        n = full_shape[ha] // 2
        ref = _sub(ref, ha, half * n, n)
    return ref


def _halved(shape, axis):
    return tuple(s // 2 if a == axis else s for a, s in enumerate(shape))


def _all_gather(shards, axes, name):
    n = len(shards)
    fulls = [tuple(s * N_CHIPS if a == ax else s for a, s in enumerate(sh.shape)) for sh, ax in zip(shards, axes)]
    own = 6

    def body(*refs):
        src, dst = refs[:n], refs[n:2 * n]
        send_sems, recv_sems = refs[2 * n:]
        x, y, c, p = _place()
        chips = [(1 - x, y), (x, 1 - y), (1 - x, 1 - y)]
        slots = [2 * cx + cy for cx, cy in chips]

        def copy(a, k, slot, half, to, from_shard):
            where = _region(dst[a], fulls[a], axes[a], slot, half)
            if from_shard:
                ha = _half_axis(fulls[a], axes[a])
                hn = fulls[a][ha] // 2
                source = _sub(src[a], ha, half * hn, hn)
            else:
                source = where
            return pltpu.make_async_remote_copy(src_ref=source, dst_ref=where, send_sem=send_sems.at[a, k],
                                                recv_sem=recv_sems.at[a, k], device_id=to, device_id_type=MESH)

        mine = [pltpu.make_async_remote_copy(src_ref=src[a], dst_ref=_region(dst[a], fulls[a], axes[a], p),
                                             send_sem=send_sems.at[a, own], recv_sem=recv_sems.at[a, own],
                                             device_id=(x, y, 1 - c), device_id_type=MESH) for a in range(n)]
        first = [copy(a, j, p, c, (*chips[j], c), True) for a in range(n) for j in range(3)]
        for cp in first + mine:
            cp.start()
        passed = []
        for a in range(n):
            for j in range(3):
                copy(a, j, slots[j], c, (x, y, c), False).wait_recv()
                fwd = copy(a, 3 + j, slots[j], c, (x, y, 1 - c), False)
                fwd.start()
                passed.append(fwd)
        for a in range(n):
            for j in range(3):
                copy(a, 3 + j, slots[j], 1 - c, (x, y, c), False).wait_recv()
        for cp in mine:
            cp.wait_recv()
        for cp in first + passed + mine:
            cp.wait_send()

    return pl.pallas_call(
        body, out_shape=tuple(SDS(f, s.dtype) for f, s in zip(fulls, shards)), in_specs=[ANY] * n, out_specs=tuple([ANY] * n),
        scratch_shapes=[pltpu.SemaphoreType.DMA((n, 7)), pltpu.SemaphoreType.DMA((n, 7))], name=name)(*shards)


def _swap_halves(grads, axes, name):
    n = len(grads)
    shapes = [g.shape for g in grads]

    def body(*refs):
        src, dst = refs[:n], refs[n:2 * n]
        send_sems, recv_sems = refs[2 * n:]
        x, y, c, _ = _place()
        cps = [pltpu.make_async_remote_copy(src_ref=_region(src[a], shapes[a], axes[a], None, 1 - c), dst_ref=dst[a],
                                            send_sem=send_sems.at[a], recv_sem=recv_sems.at[a],
                                            device_id=(x, y, 1 - c), device_id_type=MESH) for a in range(n)]
        for cp in cps:
            cp.start()
        for cp in cps:
            cp.wait()

    outs = tuple(SDS(_halved(s, _half_axis(s, ax)), g.dtype) for s, ax, g in zip(shapes, axes, grads))
    return pl.pallas_call(body, out_shape=outs, in_specs=[ANY] * n, out_specs=tuple([ANY] * n),
                          scratch_shapes=[pltpu.SemaphoreType.DMA((n,)), pltpu.SemaphoreType.DMA((n,))], name=name)(*grads)


def _row_block(rows, row_bytes, limit=3 << 20):
    for b in (1024, 512, 256, 128, 64, 32, 16, 8):
        if rows % b == 0 and b * row_bytes <= limit:
            return b
    return rows


def _add_own_half(g, other, ax, cidx, name):
    _, kp, np_ = other.shape
    ha = _half_axis(g.shape, ax)
    ks, ns = (kp // N_CHIPS, np_) if ax == 1 else (kp, np_ // N_CHIPS)
    bk = _row_block(ks, ns * 4)
    nkb = ks // bk

    def g_map(q, i, cref):
        c = cref[0]
        if ax == 1:
            return (c, q * nkb + i, 0) if ha == 0 else (0, q * nkb + i, c)
        return (c, i, q) if ha == 0 else (0, c * nkb + i, q)

    def o_map(q, i, cref):
        return (0, q * nkb + i, 0) if ax == 1 else (0, i, q)

    def body(c_ref, g_ref, o_ref, send_ref, land_ref):
        del c_ref
        s = (g_ref[...].astype(f32) + o_ref[...].astype(f32)).astype(send_ref.dtype)
        send_ref[...] = s
        land_ref[...] = s

    out = pl.BlockSpec((None, bk, ns), lambda q, i, cref: (q, i, 0))
    grid_spec = pltpu.PrefetchScalarGridSpec(
        num_scalar_prefetch=1, grid=(N_CHIPS, nkb),
        in_specs=[pl.BlockSpec((None, bk, ns), g_map), pl.BlockSpec((None, bk, ns), o_map)], out_specs=(out, out))
    shape = SDS((N_CHIPS, ks, ns), g.dtype)
    return pl.pallas_call(body, out_shape=(shape, shape), grid_spec=grid_spec, name=name, compiler_params=_params(2))(cidx, g, other)


def _to_owners(sends, lands, name):
    n = len(sends)

    def body(*refs):
        src, dst = refs[:n], refs[2 * n:3 * n]
        send_sems, recv_sems = refs[3 * n:]
        x, y, c, p = _place()
        chips = [(1 - x, y), (x, 1 - y), (1 - x, 1 - y)]
        slots = [2 * cx + cy for cx, cy in chips]
        cps = [pltpu.make_async_remote_copy(src_ref=src[a].at[slots[j]], dst_ref=dst[a].at[p], send_sem=send_sems.at[a, j],
                                            recv_sem=recv_sems.at[a, j], device_id=(*chips[j], c), device_id_type=MESH)
               for a in range(n) for j in range(3)]
        for cp in cps:
            cp.start()
        for a in range(n):
            for j in range(3):
                pltpu.make_async_remote_copy(src_ref=src[a].at[p], dst_ref=dst[a].at[slots[j]], send_sem=send_sems.at[a, j],
                                             recv_sem=recv_sems.at[a, j], device_id=(x, y, c), device_id_type=MESH).wait_recv()
        for cp in cps:
            cp.wait_send()

    return pl.pallas_call(
        body, out_shape=tuple(SDS(l.shape, l.dtype) for l in lands), in_specs=[ANY] * (2 * n), out_specs=tuple([ANY] * n),
        scratch_shapes=[pltpu.SemaphoreType.DMA((n, 3)), pltpu.SemaphoreType.DMA((n, 3))],
        input_output_aliases={n + a: a for a in range(n)}, name=name)(*sends, *lands)


def _sum_chips(stack, shard_shape, ax, cidx, name):
    _, ks, ns = stack.shape
    ha = _half_axis(shard_shape, ax)
    bk = _row_block(ks, ns * 4 * N_CHIPS)
    nkb = ks // bk

    def o_map(i, cref):
        c = cref[0]
        return (c, i, 0) if ha == 0 else ((0, c * nkb + i, 0) if ha == 1 else (0, i, c))

    def body(c_ref, s_ref, o_ref):
        del c_ref
        acc = s_ref[0].astype(f32)
        for q in range(1, N_CHIPS):
            acc = acc + s_ref[q].astype(f32)
        o_ref[...] = acc

    grid_spec = pltpu.PrefetchScalarGridSpec(
        num_scalar_prefetch=1, grid=(nkb,), in_specs=[pl.BlockSpec((N_CHIPS, bk, ns), lambda i, cref: (0, i, 0))],
        out_specs=pl.BlockSpec((None, bk, ns), o_map))
    return pl.pallas_call(body, out_shape=SDS(shard_shape, f32), grid_spec=grid_spec, name=name, compiler_params=_params(1))(cidx, stack)


def _join_halves(slices, axes, name):
    n = len(slices)

    def body(*refs):
        dst = refs[n:2 * n]
        send_sems, recv_sems = refs[2 * n:]
        x, y, c, _ = _place()

        def half(a, h):
            ha = _half_axis(slices[a].shape, axes[a])
            hn = slices[a].shape[ha] // 2
            return _sub(dst[a], ha, h * hn, hn)

        cps = [pltpu.make_async_remote_copy(src_ref=half(a, c), dst_ref=half(a, c), send_sem=send_sems.at[a], recv_sem=recv_sems.at[a],
                                            device_id=(x, y, 1 - c), device_id_type=MESH) for a in range(n)]
        for cp in cps:
            cp.start()
        for a in range(n):
            pltpu.make_async_remote_copy(src_ref=half(a, c), dst_ref=half(a, 1 - c), send_sem=send_sems.at[a], recv_sem=recv_sems.at[a],
                                         device_id=(x, y, c), device_id_type=MESH).wait_recv()
        for cp in cps:
            cp.wait_send()

    return pl.pallas_call(
        body, out_shape=tuple(SDS(s.shape, s.dtype) for s in slices), in_specs=[ANY] * n, out_specs=tuple([ANY] * n),
        scratch_shapes=[pltpu.SemaphoreType.DMA((n,)), pltpu.SemaphoreType.DMA((n,))],
        input_output_aliases={a: a for a in range(n)}, name=name)(*slices)


def _reduce_scatter(grads, axes, tag):
    cidx = jnp.reshape(lax.axis_index("c"), (1,)).astype(jnp.int32)
    others = _swap_halves(grads, axes, f"rs_swap_{tag}")
    pairs = [_add_own_half(g, o, ax, cidx, f"rs_add_{tag}_{a}") for a, (g, o, ax) in enumerate(zip(grads, others, axes))]
    stacks = _to_owners([s for s, _ in pairs], [l for _, l in pairs], f"rs_owner_{tag}")
    shard_shapes = [tuple(s // N_CHIPS if i == ax else s for i, s in enumerate(g.shape)) for g, ax in zip(grads, axes)]
    slices = [_sum_chips(s, sh, ax, cidx, f"rs_sum_{tag}_{a}") for a, (s, sh, ax) in enumerate(zip(stacks, shard_shapes, axes))]
    return _join_halves(slices, axes, f"rs_join_{tag}")


SMALL_COLS = 256


def _pack(arrays, rows_multiple):
    flat = jnp.concatenate([a.reshape(-1).astype(f32) for a in arrays])
    rows = -(-flat.shape[0] // SMALL_COLS)
    rows = -(-rows // rows_multiple) * rows_multiple
    flat = jnp.pad(flat, (0, rows * SMALL_COLS - flat.shape[0]))
    return flat.reshape(1, rows, SMALL_COLS)


def _unpack(buf, shapes):
    flat, out, off = buf.reshape(-1), [], 0
    for s in shapes:
        n = math.prod(s)
        out.append(flat[off:off + n].reshape(s))
        off += n
    return out


def _block_diag_in(bb):
    g, p, c = bb.shape
    k = g // SSM_GB
    eye = jnp.eye(SSM_GB, dtype=bb.dtype)
    return jnp.einsum("kgpc,gh->kgchp", bb.reshape(k, SSM_GB, p, c), eye).reshape(k, SSM_GB * c, SSM_GB * p)


def _block_diag_out(cc):
    g, c, p = cc.shape
    k = g // SSM_GB
    eye = jnp.eye(SSM_GB, dtype=cc.dtype)
    return jnp.einsum("kgcp,gh->kgphc", cc.reshape(k, SSM_GB, c, p), eye).reshape(k, SSM_GB * p, SSM_GB * c)


def _diag_in(db, p, c):
    k = db.shape[0]
    return jnp.einsum("kgcgp->kgpc", db.reshape(k, SSM_GB, c, SSM_GB, p)).reshape(k * SSM_GB, p, c)


def _diag_out(dc, p, c):
    k = dc.shape[0]
    return jnp.einsum("kgpgc->kgcp", dc.reshape(k, SSM_GB, p, SSM_GB, c)).reshape(k * SSM_GB, c, p)


def _state_slab(v):
    g, p = v.shape
    return v.reshape(g // SSM_GB, SSM_GB * p // LANES, LANES)


BIG = ("ab_w_in", "ab_w_out", "ssm_w_in", "ssm_w_glu", "xa_w_q", "xa_w_kv", "xa_w_o", "ffn_w_up", "ffn_w_down")
BIG_AXIS = dict(ab_w_in=2, ab_w_out=1, ssm_w_in=1, ssm_w_glu=2, xa_w_q=1, xa_w_kv=2, xa_w_o=1, ffn_w_up=2, ffn_w_down=1)
SMALL_REPL = ("norm_mix", "norm_xattn", "norm_ffn", "norm_mem", "norm_final", "pool_w", "pool_scale", "ssm_lam_re", "ssm_lam_im",
              "ssm_log_dt", "ssm_b_re", "ssm_b_im", "ssm_c_re", "ssm_c_im", "ffn_conv_b")
SMALL_SHARDED = ("ssm_d", "ffn_conv_w")
WEIGHTS = ("norm_mix", "norm_xattn", "norm_ffn", "norm_mem", "norm_final", "ab_w_in", "pool_w", "pool_scale", "ab_w_out", "ssm_w_in",
           "ssm_lam_re", "ssm_lam_im", "ssm_log_dt", "ssm_b_re", "ssm_b_im", "ssm_c_re", "ssm_c_im", "ssm_d", "ssm_w_glu", "xa_w_q",
           "xa_w_kv", "xa_w_o", "ffn_w_up", "ffn_conv_w", "ffn_conv_b", "ffn_w_down")


def _local_step(xf, memf, tgt, w, wf, conv_w, ssm_d, seq):
    d = xf.shape[1]
    depth = w["norm_mix"].shape[0]
    sbw = wf["ab_w_in"].shape[2] // 4
    row = lambda a: a.reshape(1, -1)

    gs, ps = w["ssm_lam_re"].shape[1:]
    col = lambda a: a.reshape(gs * ps, 1)
    lam_re, lam_im = col(w["ssm_lam_re"][0]), col(w["ssm_lam_im"][0])
    dt = col(jnp.broadcast_to(jnp.exp(w["ssm_log_dt"][0])[:, None], (gs, ps)))
    b_re, b_im = w["ssm_b_re"][0].reshape(gs * ps, -1), w["ssm_b_im"][0].reshape(gs * ps, -1)
    lb_re, lb_im, _, _, bb_re, bb_im = _ssm_disc_fwd(lam_re, lam_im, dt, b_re, b_im, "ssm_disc")
    cgrp = b_re.shape[1]
    b_big = jnp.concatenate([_block_diag_in(bb_re.reshape(gs, ps, cgrp)), _block_diag_in(bb_im.reshape(gs, ps, cgrp))], axis=2).astype(bf16)
    c_big = jnp.concatenate([_block_diag_out(w["ssm_c_re"][0]), -_block_diag_out(w["ssm_c_im"][0])], axis=1).astype(bf16)
    lr_s, li_s = _state_slab(lb_re.reshape(gs, ps)), _state_slab(lb_im.reshape(gs, ps))
    lslab = jnp.concatenate([lr_s, lr_s, -li_s, li_s], axis=1)

    mem_n = _norm_fwd(memf, row(w["norm_mem"]), "norm_mem")
    kv = [_mm(mem_n, wf["xa_w_kv"], mode="nn", b_l=l, out_dtype=bf16, name=f"kv{l}") for l in range(depth)]
    xs, saved = [xf], []
    cur = xf
    for l in range(depth):
        sv = {}
        h = _norm_fwd(cur, row(w["norm_mix"][l]), f"norm_mix{l}")
        sv["h"] = h
        if l % 2 == 0:
            qkv = _mm(h, wf["ab_w_in"], mode="nn", b_l=0, n=3 * sbw, out_dtype=bf16, name=f"qkv{l}")
            u = _mm(h, wf["ab_w_in"], mode="nn", b_l=0, b_n0=3 * sbw, n=sbw, out_dtype=f32, name=f"poolin{l}")
            mix, ltot = _sb_fwd(qkv, seq, f"sb_fwd{l}")
            pooled, mix = _pool_fwd(u, mix, w["pool_w"][0], w["pool_scale"], seq, f"pool_fwd{l}")
            sv.update(qkv=qkv, mix=mix, ltot=ltot, pooled=pooled)
            cur = _mm(mix, wf["ab_w_out"], mode="nn", b_l=0, res=cur, out_dtype=f32, name=f"mixout{l}")
        else:
            us = _mm(h, wf["ssm_w_in"], mode="nn", b_l=0, out_dtype=f32, name=f"ssmin{l}")
            ys = _ssm_fwd(us, b_big, c_big, lslab, ssm_d, seq, f"ssm_fwd{l}")
            gl = _gelu_fwd(ys, f"gelu{l}")
            glu = _mm(gl, wf["ssm_w_glu"], mode="nn", b_l=0, out_dtype=f32, name=f"glu{l}")
            sv.update(us=us, ys=ys, gl=gl, glu=glu)
            cur = _glu_fwd(glu, cur, f"glugate{l}")
        sv["x1"] = cur
        hx = _norm_fwd(cur, row(w["norm_xattn"][l]), f"norm_xa{l}")
        qx = _mm(hx, wf["xa_w_q"], mode="nn", b_l=l, out_dtype=bf16, name=f"xaq{l}")
        ox = _xa_fwd(qx, kv[l], seq, f"xa_fwd{l}")
        cur = _mm(ox, wf["xa_w_o"], mode="nn", b_l=l, res=cur, out_dtype=f32, name=f"xao{l}")
        sv.update(hx=hx, qx=qx, ox=ox, x2=cur)
        hf = _norm_fwd(cur, row(w["norm_ffn"][l]), f"norm_ffn{l}")
        up = _mm(hf, wf["ffn_w_up"], mode="nn", b_l=l, out_dtype=bf16, name=f"ffnup{l}")
        act = _ffn_gate_fwd(up, conv_w[l], row(w["ffn_conv_b"][l]), seq, f"ffn_gate{l}")
        cur = _mm(act, wf["ffn_w_down"], mode="nn", b_l=l, res=cur, out_dtype=f32, name=f"ffndown{l}")
        sv.update(hf=hf, up=up, act=act)
        saved.append(sv)
        xs.append(cur)

    dx, g_final8, loss8 = _loss_head(cur, tgt, row(w["norm_final"]), "loss_head")

    gw = {}
    small = {"norm_final": jnp.sum(g_final8, axis=0)}
    g_mix, g_xa, g_ffn, g_cw, g_cb = [None] * depth, [None] * depth, [None] * depth, [None] * depth, [None] * depth
    dkv = [None] * depth

    def wgrad(key, a, b, l, layers, **kw):
        kw.setdefault("bk", 1024)
        gw[key] = _mm(a, b, mode="tn", out_dtype=bf16, out_l=l, out_layers=layers, out_prev=gw.get(key),
                      name=f"dw_{key}{l}", **kw)

    for l in reversed(range(depth)):
        sv = saved[l]
        dact = _mm(dx, wf["ffn_w_down"], mode="nt", b_l=l, out_dtype=bf16, name=f"d_act{l}")
        wgrad("ffn_w_down", sv["act"], dx, l, depth)
        dup, dcw8, dcb8 = _ffn_gate_bwd(dact, sv["up"], conv_w[l], row(w["ffn_conv_b"][l]), seq, f"ffn_gate_bwd{l}")
        g_cw[l], g_cb[l] = jnp.sum(dcw8, axis=1), jnp.sum(dcb8, axis=0)
        wgrad("ffn_w_up", sv["hf"], dup, l, depth)
        dhf = _mm(dup, wf["ffn_w_up"], mode="nt", b_l=l, out_dtype=f32, name=f"d_hf{l}")
        dx, g8 = _norm_bwd(dhf, sv["x2"], dx, row(w["norm_ffn"][l]), f"norm_ffn_bwd{l}")
        g_ffn[l] = jnp.sum(g8, axis=0)
        dox = _mm(dx, wf["xa_w_o"], mode="nt", b_l=l, out_dtype=bf16, name=f"d_ox{l}")
        wgrad("xa_w_o", sv["ox"], dx, l, depth)
        dqx, dkv[l] = _xa_bwd(sv["qx"], kv[l], dox, seq, f"xa_bwd{l}")
        wgrad("xa_w_q", sv["hx"], dqx, l, depth)
        dhx = _mm(dqx, wf["xa_w_q"], mode="nt", b_l=l, out_dtype=f32, name=f"d_hx{l}")
        dx, g8 = _norm_bwd(dhx, sv["x1"], dx, row(w["norm_xattn"][l]), f"norm_xa_bwd{l}")
        g_xa[l] = jnp.sum(g8, axis=0)
        if l % 2 == 0:
            dmix = _mm(dx, wf["ab_w_out"], mode="nt", b_l=0, out_dtype=f32, name=f"d_mix{l}")
            wgrad("ab_w_out", sv["mix"], dx, 0, 1)
            dq, dk, dv = _sb_bwd(sv["qkv"], sv["ltot"], dmix, seq, f"sb_bwd{l}")
            du, dpw, dps8 = _pool_bwd(dmix, sv["pooled"], w["pool_w"][0], w["pool_scale"], seq, f"pool_bwd{l}")
            small["pool_w"], small["pool_scale"] = dpw[None], jnp.sum(dps8, axis=0)[None]
            dproj = jnp.concatenate([dq, dk, dv, du], axis=1)
            wgrad("ab_w_in", sv["h"], dproj, 0, 1)
            dh = _mm(dproj, wf["ab_w_in"], mode="nt", b_l=0, out_dtype=f32, name=f"d_h{l}")
        else:
            dglu = _glu_bwd(dx, sv["glu"], f"glugate_bwd{l}")
            wgrad("ssm_w_glu", sv["gl"], dglu, 0, 1)
            dgl = _mm(dglu, wf["ssm_w_glu"], mode="nt", b_l=0, out_dtype=f32, name=f"d_gelu{l}")
            dys = _gelu_bwd(dgl, sv["ys"], f"gelu_bwd{l}")
            dus, db_big, dc_big, dl, dd8 = _ssm_bwd(sv["us"], dys, b_big, c_big, lslab, ssm_d, seq, f"ssm_bwd{l}")
            small["ssm_d"] = jnp.sum(dd8, axis=0)[None]
            half = SSM_PLANES // 2
            g_lr = (dl[:, 0:half] + dl[:, half:SUBLANES]).reshape(gs * ps, 1)
            g_li = (dl[:, SUBLANES + half:] - dl[:, SUBLANES:SUBLANES + half]).reshape(gs * ps, 1)
            g_bbr = _diag_in(db_big[:, :, :SSM_GB * ps], ps, cgrp).reshape(gs * ps, cgrp)
            g_bbi = _diag_in(db_big[:, :, SSM_GB * ps:], ps, cgrp).reshape(gs * ps, cgrp)
            d_a, d_b, d_dt, d_br, d_bi = _ssm_disc_bwd(lam_re, lam_im, dt, b_re, b_im, g_lr, g_li, g_bbr, g_bbi, "ssm_disc_bwd")
            small["ssm_lam_re"], small["ssm_lam_im"] = d_a.reshape(1, gs, ps), d_b.reshape(1, gs, ps)
            small["ssm_log_dt"] = (jnp.sum(d_dt.reshape(gs, ps), axis=1) * dt.reshape(gs, ps)[:, 0])[None]
            small["ssm_b_re"], small["ssm_b_im"] = d_br.reshape(1, gs, ps, cgrp), d_bi.reshape(1, gs, ps, cgrp)
            small["ssm_c_re"] = _diag_out(dc_big[:, :SSM_GB * ps], ps, cgrp)[None]
            small["ssm_c_im"] = -_diag_out(dc_big[:, SSM_GB * ps:], ps, cgrp)[None]
            wgrad("ssm_w_in", sv["h"], dus, 0, 1)
            dh = _mm(dus, wf["ssm_w_in"], mode="nt", b_l=0, out_dtype=f32, name=f"d_h{l}")
        dx, g8 = _norm_bwd(dh, xs[l], dx, row(w["norm_mix"][l]), f"norm_mix_bwd{l}")
        g_mix[l] = jnp.sum(g8, axis=0)

    dmem_n = None
    for l in range(depth):
        wgrad("xa_w_kv", mem_n, dkv[l], l, depth, bk=mem_n.shape[0])
        dmem_n = _mm(dkv[l], wf["xa_w_kv"], mode="nt", b_l=l, res=dmem_n, out_dtype=f32, name=f"d_memn{l}")
    small["norm_mem"] = jnp.sum(_norm_bwd_gain_only(dmem_n, memf, "norm_mem_bwd"), axis=0)
    small["norm_mix"], small["norm_xattn"], small["norm_ffn"] = jnp.stack(g_mix), jnp.stack(g_xa), jnp.stack(g_ffn)
    small["ffn_conv_w"], small["ffn_conv_b"] = jnp.stack(g_cw), jnp.stack(g_cb)
    return loss8, dx, gw, small


def _step(x, mem, loss_target, w, m, v):
    nb, seq, d = x.shape
    t_all = nb * seq
    depth = w["norm_mix"].shape[0]
    chip = 2 * lax.axis_index("x") + lax.axis_index("y")

    big_axes = [BIG_AXIS[k] for k in BIG]
    small_mine = _pack([w[k] for k in SMALL_SHARDED], SUBLANES)
    gathered = _all_gather([w[k].astype(bf16) for k in BIG] + [small_mine], big_axes + [1], "gather_weights")
    wf = dict(zip(BIG, gathered[:-1]))
    per_chip = gathered[-1].reshape(N_CHIPS, -1)
    pieces = [_unpack(per_chip[q], [w[k].shape for k in SMALL_SHARDED]) for q in range(N_CHIPS)]
    ssm_d = jnp.concatenate([pc[0] for pc in pieces], axis=-1)
    conv_w = jnp.concatenate([pc[1] for pc in pieces], axis=-1)
    ff2 = conv_w.shape[-1]

    loss8, dx, gw, small = _local_step(x.reshape(t_all, d), mem.reshape(-1, d), loss_target.reshape(t_all, d), w, wf, conv_w,
                                       ssm_d, seq)
    loss = lax.psum(0.5 * jnp.sum(loss8) / d, ("x", "y", "c"))

    small_names = SMALL_REPL + SMALL_SHARDED
    small_full_shapes = [w[k].shape for k in SMALL_REPL] + [(1, d), (depth, 3, ff2)]
    small_buf = _pack([small[k] for k in small_names], 2 * N_CHIPS * SUBLANES)
    reduced = _reduce_scatter([gw[k] for k in BIG] + [small_buf], big_axes + [1], "grads")
    g_big = dict(zip(BIG, reduced[:-1]))
    small_all = _all_gather([reduced[-1]], [1], "gather_small_grads")[0]
    g_small = dict(zip(small_names, _unpack(small_all, small_full_shapes)))
    g_small["ssm_d"] = lax.dynamic_slice_in_dim(g_small["ssm_d"], chip * (d // N_CHIPS), d // N_CHIPS, axis=1)
    g_small["ffn_conv_w"] = lax.dynamic_slice_in_dim(g_small["ffn_conv_w"], chip * (ff2 // N_CHIPS), ff2 // N_CHIPS, axis=2)
    grads = {**g_big, **g_small}

    delta, new_m, new_v = {}, {}, {}
    for k in BIG:
        n_cols = w[k].shape[-1]
        two = lambda a: a.reshape(-1, n_cols)
        dl_, m_, v_ = _adamw(two(w[k]), two(grads[k]), two(m[k]), two(v[k]), f"adamw_{k}")
        delta[k], new_m[k], new_v[k] = dl_.reshape(w[k].shape), m_.reshape(w[k].shape), v_.reshape(w[k].shape)
    pk = lambda tree: _pack([tree[k] for k in small_names], 256)[0]
    small_shapes = [w[k].shape for k in small_names]
    outs = _adamw(pk(w), pk(grads), pk(m), pk(v), "adamw_small")
    for tree, buf in zip((delta, new_m, new_v), outs):
        tree.update(zip(small_names, _unpack(buf, small_shapes)))

    grad_x = dx.reshape(nb, seq, d)
    return (loss, grad_x, *[grads[k] for k in WEIGHTS], *[delta[k] for k in WEIGHTS], *[new_m[k] for k in WEIGHTS],
            *[new_v[k] for k in WEIGHTS])


def kernel(x, mem, norm_mix, norm_xattn, norm_ffn, norm_mem, norm_final, ab_w_in, pool_w, pool_scale, ab_w_out, ssm_w_in, ssm_lam_re, ssm_lam_im, ssm_log_dt, ssm_b_re, ssm_b_im, ssm_c_re, ssm_c_im, ssm_d, ssm_w_glu, xa_w_q, xa_w_kv, xa_w_o, ffn_w_up, ffn_conv_w, ffn_conv_b, ffn_w_down, loss_target, m_norm_mix, m_norm_xattn, m_norm_ffn, m_norm_mem, m_norm_final, m_ab_w_in, m_pool_w, m_pool_scale, m_ab_w_out, m_ssm_w_in, m_ssm_lam_re, m_ssm_lam_im, m_ssm_log_dt, m_ssm_b_re, m_ssm_b_im, m_ssm_c_re, m_ssm_c_im, m_ssm_d, m_ssm_w_glu, m_xa_w_q, m_xa_w_kv, m_xa_w_o, m_ffn_w_up, m_ffn_conv_w, m_ffn_conv_b, m_ffn_w_down, v_norm_mix, v_norm_xattn, v_norm_ffn, v_norm_mem, v_norm_final, v_ab_w_in, v_pool_w, v_pool_scale, v_ab_w_out, v_ssm_w_in, v_ssm_lam_re, v_ssm_lam_im, v_ssm_log_dt, v_ssm_b_re, v_ssm_b_im, v_ssm_c_re, v_ssm_c_im, v_ssm_d, v_ssm_w_glu, v_xa_w_q, v_xa_w_kv, v_xa_w_o, v_ffn_w_up, v_ffn_conv_w, v_ffn_conv_b, v_ffn_w_down):
    args = dict(locals())
    w = {k: args[k] for k in WEIGHTS}
    m = {k: args["m_" + k] for k in WEIGHTS}
    v = {k: args["v_" + k] for k in WEIGHTS}
    return _step(x, mem, loss_target, w, m, v)
```

```python
import functools
import math

import jax
import jax.numpy as jnp
from jax import lax
from jax.experimental import pallas as pl
from jax.experimental.pallas import tpu as pltpu

f32 = jnp.float32
bf16 = jnp.bfloat16
SDS = jax.ShapeDtypeStruct
MESH = pl.DeviceIdType.MESH
ANY = pl.BlockSpec(memory_space=pl.ANY)

SB_HEAD_DIM = 64
POOL_WINDOWS = (2, 4, 8, 16)
POOL_GROUP = 128
XA_HEADS = 4
SSM_GROUPS = 64
SSM_GROUP = 16
SSM_STATE = 64
EPS = 1e-6
ADAM_LR, ADAM_B1, ADAM_B2, ADAM_EPS, ADAM_WD, ADAM_STEP = 0.001, 0.9, 0.999, 1e-08, 0.01, 10

LANES = 128
SUBLANES = 8
N_CHIPS = 4
VMEM_LIMIT = 56 * 1024 * 1024

NN = ((1,), (0,))
NT = ((1,), (1,))
TN = ((0,), (0,))


def _dot(a, b, dims):
    return lax.dot_general(a, b, (dims, ((), ())), preferred_element_type=f32)


def _params(n_grid):
    return pltpu.CompilerParams(dimension_semantics=("arbitrary",) * n_grid, vmem_limit_bytes=VMEM_LIMIT)


def _sum8(x):
    r, n = x.shape
    return jnp.sum(x.reshape(r // SUBLANES, SUBLANES, n), axis=0)


def _split_bf16(x):
    hi = x.astype(bf16)
    lo = (x - hi.astype(f32)).astype(bf16)
    return hi, lo


def _sigmoid(x):
    return 1.0 / (1.0 + jnp.exp(-x))


MM_BM = (1024, 1408, 512, 256, 128)
MM_BN = (1536, 1408, 1024, 512, 256, 128)
MM_BK = (2816, 2048, 1024, 512)


def _divisor(n, cands):
    return next((c for c in cands if n % c == 0), n)


def _mm(a, b, *, mode, name, out_dtype, bm=None, bn=None, bk=None, a_l=None, b_l=None, b_n0=0, n=None,
        res=None, out_l=None, out_layers=None, out_prev=None):
    dims = {"nn": NN, "nt": NT, "tn": TN}[mode]
    a2, b2 = a.shape[-2:], b.shape[-2:]
    if mode == "nn":
        (m, k), nfull = a2, b2[1]
    elif mode == "nt":
        (m, k), nfull = a2, b2[0]
    else:
        (k, m), nfull = a2, b2[1]
    n = nfull if n is None else n
    bm = _divisor(m, MM_BM) if bm is None else min(bm, m)
    bn = _divisor(n, MM_BN) if bn is None else min(bn, n)
    if bk is None:
        bk = _divisor(k, (1024, 512)) if mode == "tn" else (k if k <= MM_BK[0] else _divisor(k, MM_BK))
    bk = min(bk, k)
    assert m % bm == 0 and n % bn == 0 and k % bk == 0 and b_n0 % bn == 0, (name, m, n, k, bm, bn, bk)
    nk, n0b = k // bk, b_n0 // bn
    a_bytes, b_bytes = m * k * a.dtype.itemsize, k * n * b.dtype.itemsize
    rows_outer = a_bytes + b_bytes * (m // bm) <= b_bytes + a_bytes * (n // bn)

    def with_layer(layer, blk, idx_fn):
        def idx(g0, g1, kk):
            i, j = (g0, g1) if rows_outer else (g1, g0)
            return idx_fn(i, j, kk) if layer is None else (layer,) + idx_fn(i, j, kk)
        return pl.BlockSpec(blk if layer is None else (None,) + blk, idx)

    if mode == "tn":
        a_spec = with_layer(a_l, (bk, bm), lambda i, j, kk: (kk, i))
    else:
        a_spec = with_layer(a_l, (bm, bk), lambda i, j, kk: (i, kk))
    if mode == "nt":
        b_spec = with_layer(b_l, (bn, bk), lambda i, j, kk: (j, kk))
    else:
        b_spec = with_layer(b_l, (bk, bn), lambda i, j, kk: (kk, j + n0b))
    o_spec = with_layer(out_l, (bm, bn), lambda i, j, kk: (i, j))
    ins, in_specs = [a, b], [a_spec, b_spec]
    if res is not None:
        ins.append(res)
        in_specs.append(with_layer(None, (bm, bn), lambda i, j, kk: (i, j)))
    aliases = {}
    if out_prev is not None:
        aliases = {len(ins): 0}
        ins.append(out_prev)
        in_specs.append(ANY)
    has_res, has_prev = res is not None, out_prev is not None

    def body(*refs):
        a_ref, b_ref = refs[0], refs[1]
        res_ref = refs[2] if has_res else None
        o_ref = refs[2 + has_res + has_prev]
        part = _dot(a_ref[...].astype(bf16), b_ref[...].astype(bf16), dims)

        def finish(r):
            if has_res:
                r = r + res_ref[...]
            o_ref[...] = r.astype(o_ref.dtype)

        if nk == 1:
            finish(part)
        else:
            acc_ref = refs[-1]
            kk = pl.program_id(2)

            @pl.when(kk == 0)
            def _():
                acc_ref[...] = part

            @pl.when(kk > 0)
            def _():
                acc_ref[...] += part

            @pl.when(kk == nk - 1)
            def _():
                finish(acc_ref[...])

    out_shape = SDS((m, n) if out_l is None else (out_layers, m, n), out_dtype)
    grid = (m // bm, n // bn, nk) if rows_outer else (n // bn, m // bm, nk)
    return pl.pallas_call(
        body, out_shape=out_shape, grid=grid, in_specs=in_specs, out_specs=o_spec,
        scratch_shapes=[] if nk == 1 else [pltpu.VMEM((bm, bn), f32)],
        input_output_aliases=aliases, name=name, compiler_params=_params(3))(*ins)


def _rowwise(fn, row_ins, full_ins, row_outs, acc_outs, *, name, br=256):
    t = row_ins[0].shape[0]
    br = next(b for b in (br, 128, 64, 32, 16, 8, t) if b <= t and t % b == 0)
    nr, nf, no = len(row_ins), len(full_ins), len(row_outs)

    def body(*refs):
        rv = [r[...] for r in refs[:nr]]
        fv = [r[...] for r in refs[nr:nr + nf]]
        o_refs = refs[nr + nf:nr + nf + no]
        a_refs = refs[nr + nf + no:]
        outs, accs = fn(rv, fv)
        for o_ref, v in zip(o_refs, outs):
            o_ref[...] = v.astype(o_ref.dtype)
        if a_refs:
            i = pl.program_id(0)

            @pl.when(i == 0)
            def _():
                for a_ref, v in zip(a_refs, accs):
                    a_ref[...] = v

            @pl.when(i > 0)
            def _():
                for a_ref, v in zip(a_refs, accs):
                    a_ref[...] += v

    in_specs = [pl.BlockSpec((br, x.shape[1]), lambda i: (i, 0)) for x in row_ins]
    in_specs += [pl.BlockSpec(x.shape, lambda i, nd=x.ndim: (0,) * nd) for x in full_ins]
    out_specs = [pl.BlockSpec((br, s.shape[1]), lambda i: (i, 0)) for s in row_outs]
    out_specs += [pl.BlockSpec(s.shape, lambda i: (0, 0)) for s in acc_outs]
    res = pl.pallas_call(body, out_shape=tuple(row_outs) + tuple(acc_outs), grid=(t // br,), in_specs=in_specs,
                         out_specs=tuple(out_specs), name=name, compiler_params=_params(1))(*row_ins, *full_ins)
    return res


def _norm_fwd(x, g, name):
    def fn(rv, fv):
        (xv,), (gv,) = rv, fv
        r = lax.rsqrt(jnp.mean(xv * xv, axis=1, keepdims=True) + EPS)
        return [xv * r * gv], []
    return _rowwise(fn, [x], [g], [SDS(x.shape, bf16)], [], name=name)[0]


def _norm_bwd(dh, x, dres, g, name):
    d = x.shape[1]

    def fn(rv, fv):
        (dhv, xv, drv), (gv,) = rv, fv
        r = lax.rsqrt(jnp.mean(xv * xv, axis=1, keepdims=True) + EPS)
        xh = xv * r
        dxh = dhv * gv
        dx = drv + r * (dxh - xh * jnp.mean(dxh * xh, axis=1, keepdims=True))
        return [dx], [_sum8(dhv * xh)]
    return _rowwise(fn, [dh, x, dres], [g], [SDS(x.shape, f32)], [SDS((SUBLANES, d), f32)], name=name)


def _norm_bwd_gain_only(dh, x, name):
    d = x.shape[1]

    def fn(rv, fv):
        dhv, xv = rv
        r = lax.rsqrt(jnp.mean(xv * xv, axis=1, keepdims=True) + EPS)
        return [], [_sum8(dhv * xv * r)]
    return _rowwise(fn, [dh, x], [], [], [SDS((SUBLANES, d), f32)], name=name)[0]


def _loss_head(x, target, g, name):
    d = x.shape[1]

    def fn(rv, fv):
        (xv, tv), (gv,) = rv, fv
        r = lax.rsqrt(jnp.mean(xv * xv, axis=1, keepdims=True) + EPS)
        xh = xv * r
        err = xh * gv - tv
        dy = err * (1.0 / d)
        dxh = dy * gv
        dx = r * (dxh - xh * jnp.mean(dxh * xh, axis=1, keepdims=True))
        return [dx], [_sum8(dy * xh), _sum8(err * err)]
    return _rowwise(fn, [x, target], [g], [SDS(x.shape, f32)], [SDS((SUBLANES, d), f32), SDS((SUBLANES, d), f32)], name=name)


_GELU_C = math.sqrt(2.0 / math.pi)


def _gelu_fwd(y, name):
    def fn(rv, fv):
        (v,) = rv
        t = jnp.tanh(_GELU_C * (v + 0.044715 * v * v * v))
        return [0.5 * v * (1.0 + t)], []
    return _rowwise(fn, [y], [], [SDS(y.shape, bf16)], [], name=name)[0]


def _gelu_bwd(dg, y, name):
    def fn(rv, fv):
        dgv, v = rv
        t = jnp.tanh(_GELU_C * (v + 0.044715 * v * v * v))
        dt = (1.0 - t * t) * _GELU_C * (1.0 + 3.0 * 0.044715 * v * v)
        return [dgv * (0.5 * (1.0 + t) + 0.5 * v * dt)], []
    return _rowwise(fn, [dg, y], [], [SDS(y.shape, f32)], [], name=name)[0]


def _glu_fwd(glu, x, name):
    d = x.shape[1]

    def fn(rv, fv):
        gl, xv = rv
        return [xv + gl[:, :d] * _sigmoid(gl[:, d:])], []
    return _rowwise(fn, [glu, x], [], [SDS(x.shape, f32)], [], name=name)[0]


def _glu_bwd(dx, glu, name):
    d = dx.shape[1]

    def fn(rv, fv):
        dxv, gl = rv
        sg = _sigmoid(gl[:, d:])
        return [jnp.concatenate([dxv * sg, dxv * gl[:, :d] * sg * (1.0 - sg)], axis=1)], []
    return _rowwise(fn, [dx, glu], [], [SDS(glu.shape, bf16)], [], name=name)[0]


def _adamw(w, g, m, v, name):
    c1 = 1.0 - ADAM_B1 ** ADAM_STEP
    c2 = 1.0 - ADAM_B2 ** ADAM_STEP

    def fn(rv, fv):
        wv, gv, mv, vv = rv
        m2 = ADAM_B1 * mv + (1.0 - ADAM_B1) * gv
        v2 = ADAM_B2 * vv + (1.0 - ADAM_B2) * (gv * gv)
        delta = -ADAM_LR * ((m2 / c1) / (jnp.sqrt(v2 / c2) + ADAM_EPS) + ADAM_WD * wv)
        return [delta, m2, v2], []
    s = SDS(w.shape, f32)
    return _rowwise(fn, [w, g, m, v], [], [s, s, s], [], name=name)


SB_TQ = 128
SB_KB = 4


def _sb_logits(qh, kb, valid):
    z = _dot(qh, kb, NT) * (SB_HEAD_DIM ** -0.5)
    sp = jnp.log(1.0 + jnp.exp(-jnp.abs(z)))
    lb = jnp.minimum(z, 0.0) - sp
    lk_raw = jnp.minimum(-z, 0.0) - sp
    return lb, lk_raw, jnp.where(valid, lk_raw, 0.0)


def _sb_heads(q, t):
    lane = lax.broadcasted_iota(jnp.int32, (t, LANES), 1)
    masks = [(lane >= hh * SB_HEAD_DIM) & (lane < (hh + 1) * SB_HEAD_DIM) for hh in range(LANES // SB_HEAD_DIM)]
    return [(m, q * jnp.where(m, 1.0, 0.0).astype(bf16)) for m in masks]


def _sb_key_minus_query(t):
    return lax.broadcasted_iota(jnp.int32, (t, t), 1) - lax.broadcasted_iota(jnp.int32, (t, t), 0)


def _tri(t, op):
    row = lax.broadcasted_iota(jnp.int32, (t, t), 0)
    col = lax.broadcasted_iota(jnp.int32, (t, t), 1)
    return jnp.where(op(row, col), 1.0, 0.0).astype(bf16)


def _dot_split(x, u):
    hi, lo = _split_bf16(x)
    return _dot(hi, u, NN) + _dot(lo, u, NN)


def _sb_fwd(qkv, seq, name, comm=None):
    t_all, w3 = qkv.shape
    w = w3 // 3
    hp, tq = w // LANES, SB_TQ
    nb, nq = t_all // seq, seq // tq
    kbn = min(SB_KB, nq)
    assert nq % kbn == 0

    def body(q_ref, k_ref, v_ref, o_ref, lt_ref):
        i = pl.program_id(2)
        heads = _sb_heads(q_ref[...], tq)
        kmq = _sb_key_minus_query(tq)
        u_after = _tri(tq, lambda r, c: r > c)
        n_it = (i + kbn) // kbn

        def step(it, carry):
            carry = list(carry)
            blocks = []
            for kk in reversed(range(kbn)):
                j = (n_it - 1 - it) * kbn + kk
                off = pl.multiple_of(j * tq, tq)
                blocks.append((k_ref[pl.ds(off, tq), :], v_ref[pl.ds(off, tq), :], kmq < (i - j) * tq))
            chains = [(hh, qh, kb, vb, valid) for kb, vb, valid in blocks for hh, (_, qh) in enumerate(heads)]
            zs = [_dot(qh, kb, NT) for _, qh, kb, _, _ in chains]
            lbs, his, los, sums = [], [], [], []
            for z, (_, _, _, _, valid) in zip(zs, chains):
                z = z * (SB_HEAD_DIM ** -0.5)
                sp = jnp.log(1.0 + jnp.exp(-jnp.abs(z)))
                lb = jnp.minimum(z, 0.0) - sp
                lk = jnp.where(valid, lb - z, 0.0)
                hi, lo = _split_bf16(lk)
                lbs.append(lb), his.append(hi), los.append(lo), sums.append(jnp.sum(lk, axis=1, keepdims=True))
            afts = [_dot(hi, u_after, NN) + _dot(lo, u_after, NN) for hi, lo in zip(his, los)]
            wgts = []
            for (hh, _, _, _, valid), lb, aft, sm in zip(chains, lbs, afts, sums):
                wgts.append(jnp.where(valid, jnp.exp(lb + (carry[2 * hh] + aft)), 0.0).astype(bf16))
                carry[2 * hh] = carry[2 * hh] + sm
            for (hh, _, _, vb, _), wgt in zip(chains, wgts):
                carry[2 * hh + 1] = carry[2 * hh + 1] + _dot(wgt, vb, NN)
            return tuple(carry)

        init = (jnp.zeros((tq, 1), f32), jnp.zeros((tq, LANES), f32)) * len(heads)
        fin = lax.fori_loop(0, n_it, step, init)
        out = jnp.zeros((tq, LANES), f32)
        ltot = jnp.zeros((tq, LANES), f32)
        for hh, (m, _) in enumerate(heads):
            out = out + jnp.where(m, fin[2 * hh + 1], 0.0)
            ltot = ltot + jnp.where(m, fin[2 * hh], 0.0)
        o_ref[...] = out
        lt_ref[...] = ltot

    row_blk = pl.BlockSpec((tq, LANES), lambda b, p, i: (b * nq + i, p))
    (mix, ltot), extra = _call(
        body, ins=[qkv, qkv, qkv], out_shape=[SDS((t_all, 2 * w), f32), SDS((t_all, w), f32)], grid=(nb, hp, nq),
        in_specs=[row_blk, pl.BlockSpec((seq, LANES), lambda b, p, i: (b, hp + p)),
                  pl.BlockSpec((seq, LANES), lambda b, p, i: (b, 2 * hp + p))],
        out_specs=[row_blk, row_blk], scratch_shapes=[], name=name, comm=comm)
    return mix, ltot, extra


def _sb_bwd(qkv, ltot, dmix, seq, name, comm=None):
    t_all, w3 = qkv.shape
    w = w3 // 3
    hp, tq = w // LANES, SB_TQ
    nb, nq = t_all // seq, seq // tq
    kbn = min(SB_KB, nq)
    assert nq % kbn == 0

    def body(q_ref, k_ref, v_ref, lt_ref, do_ref, dq_ref, dk_ref, dv_ref, dk_acc, dv_acc):
        i = pl.program_id(2)

        @pl.when(i == 0)
        def _():
            dk_acc[...] = jnp.zeros_like(dk_acc)
            dv_acc[...] = jnp.zeros_like(dv_acc)

        heads = _sb_heads(q_ref[...], tq)
        do = do_ref[...]
        ltv = lt_ref[...]
        dos = [jnp.where(m, do, 0.0).astype(bf16) for m, _ in heads]
        lts = [jnp.sum(jnp.where(m, ltv, 0.0), axis=1, keepdims=True) * (1.0 / SB_HEAD_DIM) for m, _ in heads]
        kmq = _sb_key_minus_query(tq)
        u_incl = _tri(tq, lambda r, c: r <= c)
        u_excl = _tri(tq, lambda r, c: r < c)
        n_it = (i + kbn) // kbn

        def step(it, carry):
            carry = list(carry)
            blocks = []
            for kk in range(kbn):
                off = pl.multiple_of((it * kbn + kk) * tq, tq)
                blocks.append((off, k_ref[pl.ds(off, tq), :], v_ref[pl.ds(off, tq), :], kmq < (i - (it * kbn + kk)) * tq))
            chains = [(hh, qh, kb, vb, valid) for _, kb, vb, valid in blocks for hh, (_, qh) in enumerate(heads)]
            zs = [_dot(qh, kb, NT) for _, qh, kb, _, _ in chains]
            dws = [_dot(dos[hh], vb, NT) for hh, _, _, vb, _ in chains]
            lbs, lkrs, his, los, sums = [], [], [], [], []
            for z, (_, _, _, _, valid) in zip(zs, chains):
                z = z * (SB_HEAD_DIM ** -0.5)
                sp = jnp.log(1.0 + jnp.exp(-jnp.abs(z)))
                lb = jnp.minimum(z, 0.0) - sp
                lk_raw = lb - z
                lk = jnp.where(valid, lk_raw, 0.0)
                hi, lo = _split_bf16(lk)
                lbs.append(lb), lkrs.append(lk_raw), his.append(hi), los.append(lo)
                sums.append(jnp.sum(lk, axis=1, keepdims=True))
            pins = [_dot(hi, u_incl, NN) + _dot(lo, u_incl, NN) for hi, lo in zip(his, los)]
            wbs, gs, ghis, glos, gpres = [], [], [], [], []
            for (hh, _, _, _, valid), lb, pin, sm, dw in zip(chains, lbs, pins, sums, dws):
                wgt = jnp.where(valid, jnp.exp(lb + (lts[hh] - (carry[3 * hh] + pin))), 0.0)
                carry[3 * hh] = carry[3 * hh] + sm
                g = dw * wgt
                hi, lo = _split_bf16(g)
                wbs.append(wgt.astype(bf16)), gs.append(g), ghis.append(hi), glos.append(lo)
                gpres.append(carry[3 * hh + 1])
                carry[3 * hh + 1] = carry[3 * hh + 1] + jnp.sum(g, axis=1, keepdims=True)
            gins = [_dot(hi, u_excl, NN) + _dot(lo, u_excl, NN) for hi, lo in zip(ghis, glos)]
            dzbs = []
            for (_, _, _, _, valid), lb, lk_raw, g, gpre, gin in zip(chains, lbs, lkrs, gs, gpres, gins):
                dz = jnp.where(valid, g * jnp.exp(lk_raw) - (gpre + gin) * jnp.exp(lb), 0.0) * (SB_HEAD_DIM ** -0.5)
                dzbs.append(dz.astype(bf16))
            for (hh, _, kb, _, _), dzb in zip(chains, dzbs):
                carry[3 * hh + 2] = carry[3 * hh + 2] + _dot(dzb, kb, NN)
            nh = len(heads)
            for bi, (off, _, _, _) in enumerate(blocks):
                dk_j = jnp.zeros((tq, LANES), f32)
                dv_j = jnp.zeros((tq, LANES), f32)
                for hh, (_, qh) in enumerate(heads):
                    dk_j = dk_j + _dot(dzbs[bi * nh + hh], qh, TN)
                    dv_j = dv_j + _dot(wbs[bi * nh + hh], dos[hh], TN)
                dk_acc[pl.ds(off, tq), :] += dk_j
                dv_acc[pl.ds(off, tq), :] += dv_j
            return tuple(carry)

        zero1 = jnp.zeros((tq, 1), f32)
        fin = lax.fori_loop(0, n_it, step, (zero1, zero1, jnp.zeros((tq, LANES), f32)) * len(heads))
        dq_all = jnp.zeros((tq, LANES), f32)
        for hh, (m, _) in enumerate(heads):
            dq_all = dq_all + jnp.where(m, fin[3 * hh + 2], 0.0)
        dq_ref[...] = dq_all.astype(bf16)

        @pl.when(i == nq - 1)
        def _():
            dk_ref[...] = dk_acc[...].astype(bf16)
            dv_ref[...] = dv_acc[...].astype(bf16)

    row_blk = pl.BlockSpec((tq, LANES), lambda b, p, i: (b * nq + i, p))
    seq_blk = pl.BlockSpec((seq, LANES), lambda b, p, i: (b, p))
    out = SDS((t_all, w), bf16)
    (dq, dk, dv), extra = _call(
        body, ins=[qkv, qkv, qkv, ltot, dmix], out_shape=[out, out, out], grid=(nb, hp, nq),
        in_specs=[row_blk,
                  pl.BlockSpec((seq, LANES), lambda b, p, i: (b, hp + p)),
                  pl.BlockSpec((seq, LANES), lambda b, p, i: (b, 2 * hp + p)),
                  row_blk, row_blk],
        out_specs=[row_blk, seq_blk, seq_blk],
        scratch_shapes=[pltpu.VMEM((seq, LANES), f32), pltpu.VMEM((seq, LANES), f32)], name=name, comm=comm)
    return dq, dk, dv, extra


POOL_CHUNK = 256
POOL_HALO = 16


def _band(rows, cols, lo, hi):
    r = lax.broadcasted_iota(jnp.int32, (rows, cols), 0)
    c = lax.broadcasted_iota(jnp.int32, (rows, cols), 1)
    d = c - r
    return jnp.where((d >= lo) & (d < hi), 1.0, 0.0).astype(bf16)


def _pool_counts(r0, rows, win):
    t = lax.broadcasted_iota(jnp.int32, (rows, 1), 0) + r0
    return jnp.minimum(t + 1, win).astype(f32)


def _pool_fwd(u, mix, pool_w, scale, seq, name):
    t_all, w = u.shape
    ng, rc = w // POOL_GROUP, min(POOL_CHUNK, seq)

    def body(u_ref, w_ref, s_ref, mix_in, p_ref, o_ref, pad):
        del mix_in
        pad[0:POOL_HALO, :] = jnp.zeros((POOL_HALO, POOL_GROUP), f32)
        for g in range(ng):
            cols = slice(g * POOL_GROUP, (g + 1) * POOL_GROUP)
            win = POOL_WINDOWS[g]
            pad[POOL_HALO:POOL_HALO + seq, :] = u_ref[:, cols]
            band = _band(rc, rc + POOL_HALO, POOL_HALO - win + 1, POOL_HALO + 1)
            wg = w_ref[g].astype(bf16)
            for r0 in range(0, seq, rc):
                ue = pad[r0:r0 + rc + POOL_HALO, :]
                hi, lo = _split_bf16(ue)
                sm = _dot(band, hi, NN) + _dot(band, lo, NN)
                pch = sm / _pool_counts(r0, rc, win) - ue[POOL_HALO:, :]
                pb = pch.astype(bf16)
                p_ref[r0:r0 + rc, cols] = pb
                o_ref[r0:r0 + rc, cols] = _dot(pb, wg, NN) * s_ref[:, cols]

    return pl.pallas_call(
        body, out_shape=(SDS((t_all, w), bf16), SDS(mix.shape, f32)), grid=(t_all // seq,),
        in_specs=[pl.BlockSpec((seq, w), lambda b: (b, 0)), pl.BlockSpec(pool_w.shape, lambda b: (0, 0, 0)),
                  pl.BlockSpec(scale.shape, lambda b: (0, 0)), ANY],
        out_specs=(pl.BlockSpec((seq, w), lambda b: (b, 0)), pl.BlockSpec((seq, w), lambda b: (b, 1))),
        scratch_shapes=[pltpu.VMEM((seq + POOL_HALO, POOL_GROUP), f32)],
        input_output_aliases={3: 1}, name=name, compiler_params=_params(1))(u, pool_w, scale, mix)


def _pool_bwd(dmix, p, pool_w, scale, seq, name):
    t_all, w = p.shape
    ng, rc = w // POOL_GROUP, min(POOL_CHUNK, seq)

    def body(dy_ref, p_ref, w_ref, s_ref, du_ref, dw_ref, ds_ref, dpn, dpr):
        b = pl.program_id(0)

        @pl.when(b == 0)
        def _():
            dw_ref[...] = jnp.zeros_like(dw_ref)
            ds_ref[...] = jnp.zeros_like(ds_ref)

        dpn[seq:seq + POOL_HALO, :] = jnp.zeros((POOL_HALO, POOL_GROUP), f32)
        for g in range(ng):
            cols = slice(g * POOL_GROUP, (g + 1) * POOL_GROUP)
            win = POOL_WINDOWS[g]
            wg = w_ref[g].astype(bf16)
            sg = s_ref[:, cols]
            dwg = jnp.zeros((POOL_GROUP, POOL_GROUP), f32)
            dsg = jnp.zeros((SUBLANES, POOL_GROUP), f32)
            for r0 in range(0, seq, rc):
                dy = dy_ref[r0:r0 + rc, cols]
                pb = p_ref[r0:r0 + rc, cols]
                dsg = dsg + _sum8(dy * _dot(pb, wg, NN))
                dyw = (dy * sg).astype(bf16)
                dwg = dwg + _dot(pb, dyw, TN)
                dp = _dot(dyw, wg, NT)
                dpr[r0:r0 + rc, :] = dp
                dpn[r0:r0 + rc, :] = dp / _pool_counts(r0, rc, win)
            dw_ref[g] += dwg
            ds_ref[:, cols] += dsg
            band = _band(rc, rc + POOL_HALO, 0, win)
            for r0 in range(0, seq, rc):
                hi, lo = _split_bf16(dpn[r0:r0 + rc + POOL_HALO, :])
                du = _dot(band, hi, NN) + _dot(band, lo, NN) - dpr[r0:r0 + rc, :]
                du_ref[r0:r0 + rc, cols] = du.astype(bf16)

    return pl.pallas_call(
        body, out_shape=(SDS((t_all, w), bf16), SDS(pool_w.shape, f32), SDS((SUBLANES, w), f32)), grid=(t_all // seq,),
        in_specs=[pl.BlockSpec((seq, w), lambda b: (b, 1)), pl.BlockSpec((seq, w), lambda b: (b, 0)),
                  pl.BlockSpec(pool_w.shape, lambda b: (0, 0, 0)), pl.BlockSpec(scale.shape, lambda b: (0, 0))],
        out_specs=(pl.BlockSpec((seq, w), lambda b: (b, 0)), pl.BlockSpec(pool_w.shape, lambda b: (0, 0, 0)),
                   pl.BlockSpec((SUBLANES, w), lambda b: (0, 0))),
        scratch_shapes=[pltpu.VMEM((seq + POOL_HALO, POOL_GROUP), f32), pltpu.VMEM((seq, POOL_GROUP), f32)],
        name=name, compiler_params=_params(1))(dmix, p, pool_w, scale)


XA_TQ = 256


def _xa_probs(qh, kh, dh):
    s = _dot(qh, kh, NT) * (dh ** -0.5)
    e = jnp.exp(s - jnp.max(s, axis=1, keepdims=True))
    return e / jnp.sum(e, axis=1, keepdims=True)


def _xa_fwd(q, kv, seq, name):
    t_all, d = q.shape
    nb = t_all // seq
    mem, dh, tq = kv.shape[0] // nb, d // XA_HEADS, min(XA_TQ, seq)
    nq = seq // tq

    def body(q_ref, kv_ref, o_ref):
        for h in range(XA_HEADS):
            cols = slice(h * dh, (h + 1) * dh)
            p = _xa_probs(q_ref[:, cols], kv_ref[:, cols], dh)
            o_ref[:, cols] = _dot(p.astype(bf16), kv_ref[:, d + h * dh:d + (h + 1) * dh], NN).astype(bf16)

    return pl.pallas_call(
        body, out_shape=SDS((t_all, d), bf16), grid=(nb, nq),
        in_specs=[pl.BlockSpec((tq, d), lambda b, i: (b * nq + i, 0)), pl.BlockSpec((mem, 2 * d), lambda b, i: (b, 0))],
        out_specs=pl.BlockSpec((tq, d), lambda b, i: (b * nq + i, 0)), name=name, compiler_params=_params(2))(q, kv)


def _xa_bwd(q, kv, do, seq, name):
    t_all, d = q.shape
    nb = t_all // seq
    mem, dh, tq = kv.shape[0] // nb, d // XA_HEADS, min(XA_TQ, seq)
    nq = seq // tq

    def body(q_ref, kv_ref, do_ref, dq_ref, dkv_ref):
        i = pl.program_id(1)

        @pl.when(i == 0)
        def _():
            dkv_ref[...] = jnp.zeros_like(dkv_ref)

        for h in range(XA_HEADS):
            cols = slice(h * dh, (h + 1) * dh)
            vcols = slice(d + h * dh, d + (h + 1) * dh)
            qh, kh, doh = q_ref[:, cols], kv_ref[:, cols], do_ref[:, cols]
            p = _xa_probs(qh, kh, dh)
            dkv_ref[:, vcols] += _dot(p.astype(bf16), doh, TN)
            dp = _dot(doh, kv_ref[:, vcols], NT)
            ds = (p * (dp - jnp.sum(dp * p, axis=1, keepdims=True)) * (dh ** -0.5)).astype(bf16)
            dq_ref[:, cols] = _dot(ds, kh, NN).astype(bf16)
            dkv_ref[:, cols] += _dot(ds, qh, TN)

    row = pl.BlockSpec((tq, d), lambda b, i: (b * nq + i, 0))
    kvs = pl.BlockSpec((mem, 2 * d), lambda b, i: (b, 0))
    return pl.pallas_call(body, out_shape=(SDS((t_all, d), bf16), SDS(kv.shape, f32)), grid=(nb, nq),
                          in_specs=[row, kvs, row], out_specs=(row, kvs), name=name, compiler_params=_params(2))(q, kv, do)


FFN_BR = 256
FFN_CHUNK = 256


def _conv3(ext, w_ref, b, cols, lo, rows):
    return (b + w_ref[2:3, cols] * ext[lo:lo + rows, :] + w_ref[1:2, cols] * ext[lo - 1:lo - 1 + rows, :]
            + w_ref[0:1, cols] * ext[lo - 2:lo - 2 + rows, :])


FFN_HALO = 16


def _ffn_gate_fwd(up, cw, cb, seq, name):
    t_all, f2 = up.shape
    ff, br, ch, hl = f2 // 2, min(FFN_BR, seq), FFN_CHUNK, FFN_HALO
    per_seq, hb = seq // br, br // hl

    def body(up_ref, halo_ref, cw_ref, cb_ref, o_ref, ev, eg):
        i = pl.program_id(0)
        keep = jnp.where(i % per_seq == 0, 0.0, 1.0)
        for c0 in range(0, ff, ch):
            convs = []
            for ext, off in ((ev, c0), (eg, ff + c0)):
                cols = slice(off, off + ch)
                ext[0:hl, :] = halo_ref[:, cols].astype(f32) * keep
                ext[hl:hl + br, :] = up_ref[:, cols].astype(f32)
                convs.append(_conv3(ext, cw_ref, cb_ref[:, cols], cols, hl, br))
            val, gate = convs
            o_ref[:, c0:c0 + ch] = (gate * _sigmoid(gate) * val).astype(bf16)

    return pl.pallas_call(
        body, out_shape=SDS((t_all, ff), bf16), grid=(t_all // br,),
        in_specs=[pl.BlockSpec((br, f2), lambda i: (i, 0)),
                  pl.BlockSpec((hl, f2), lambda i: (jnp.maximum(i * hb - 1, 0), 0)),
                  pl.BlockSpec(cw.shape, lambda i: (0, 0)), pl.BlockSpec(cb.shape, lambda i: (0, 0))],
        out_specs=pl.BlockSpec((br, ff), lambda i: (i, 0)),
        scratch_shapes=[pltpu.VMEM((br + hl, ch), f32), pltpu.VMEM((br + hl, ch), f32)],
        name=name, compiler_params=_params(1))(up, up, cw, cb)


def _ffn_gate_bwd(dact, up, cw, cb, seq, name):
    t_all, f2 = up.shape
    ff, br, ch, hl = f2 // 2, min(FFN_BR, seq), FFN_CHUNK, FFN_HALO
    per_seq, hb, last = seq // br, br // hl, t_all // hl - 1
    ext_rows = br + SUBLANES

    def body(da_ref, dan_ref, up_ref, upp_ref, upn_ref, cw_ref, cb_ref, du_ref, dcw_ref, dcb_ref, uv, ug, dav, dcv, dcg):
        i = pl.program_id(0)

        @pl.when(i == 0)
        def _():
            dcw_ref[...] = jnp.zeros_like(dcw_ref)
            dcb_ref[...] = jnp.zeros_like(dcb_ref)

        keep_prev = jnp.where(i % per_seq == 0, 0.0, 1.0)
        keep_next = jnp.where((i + 1) % per_seq == 0, 0.0, 1.0)
        for c0 in range(0, ff, ch):
            convs = []
            for ext, off in ((uv, c0), (ug, ff + c0)):
                cols = slice(off, off + ch)
                ext[0:hl, :] = upp_ref[:, cols].astype(f32) * keep_prev
                ext[hl:hl + br, :] = up_ref[:, cols].astype(f32)
                ext[hl + br:2 * hl + br, :] = upn_ref[:, cols].astype(f32) * keep_next
                convs.append(_conv3(ext, cw_ref, cb_ref[:, cols], cols, hl, ext_rows))
            val, gate = convs
            dav[0:br, :] = da_ref[:, c0:c0 + ch].astype(f32)
            dav[br:br + hl, :] = dan_ref[:, c0:c0 + ch].astype(f32) * keep_next
            da = dav[0:ext_rows, :]
            sg = _sigmoid(gate)
            dcv[...] = da * gate * sg
            dcg[...] = da * val * sg * (1.0 + gate * (1.0 - sg))
            for ext, dc, off in ((uv, dcv, c0), (ug, dcg, ff + c0)):
                cols = slice(off, off + ch)
                du = (cw_ref[2:3, cols] * dc[0:br, :] + cw_ref[1:2, cols] * dc[1:br + 1, :]
                      + cw_ref[0:1, cols] * dc[2:br + 2, :])
                du_ref[:, cols] = du.astype(bf16)
                d0 = dc[0:br, :]
                dcb_ref[:, cols] += _sum8(d0)
                for tap in range(3):
                    lo = hl - (2 - tap)
                    dcw_ref[tap, :, cols] += _sum8(d0 * ext[lo:lo + br, :])

    blk = lambda n: pl.BlockSpec((br, n), lambda i: (i, 0))
    prev = lambda n: pl.BlockSpec((hl, n), lambda i: (jnp.maximum(i * hb - 1, 0), 0))
    nxt = lambda n: pl.BlockSpec((hl, n), lambda i: (jnp.minimum((i + 1) * hb, last), 0))
    return pl.pallas_call(
        body, out_shape=(SDS((t_all, f2), bf16), SDS((3, SUBLANES, f2), f32), SDS((SUBLANES, f2), f32)), grid=(t_all // br,),
        in_specs=[blk(ff), nxt(ff), blk(f2), prev(f2), nxt(f2), pl.BlockSpec(cw.shape, lambda i: (0, 0)),
                  pl.BlockSpec(cb.shape, lambda i: (0, 0))],
        out_specs=(blk(f2), pl.BlockSpec((3, SUBLANES, f2), lambda i: (0, 0, 0)), pl.BlockSpec((SUBLANES, f2), lambda i: (0, 0))),
        scratch_shapes=[pltpu.VMEM((br + 2 * hl, ch), f32), pltpu.VMEM((br + 2 * hl, ch), f32),
                        pltpu.VMEM((br + hl, ch), f32), pltpu.VMEM((ext_rows, ch), f32), pltpu.VMEM((ext_rows, ch), f32)],
        name=name, compiler_params=_params(1))(dact, dact, up, up, up, cw, cb)


SSM_GB = 8
SSM_PLANES = 8
SSM_ROWS = 256
SSM_UNROLL = 8


def _ssm_pitch(seq):
    p = seq + SUBLANES
    assert (p // SUBLANES) % 2 == 1
    return p


def _rows(base, rc):
    return pl.ds(pl.multiple_of(base + rc * SSM_ROWS, SUBLANES), SSM_ROWS)


def _ssm_project_in(u_ref, b_ref, planes, e, seq, pitch):
    def chunk(rc, _):
        uc = u_ref[_rows(e * seq, rc), :].astype(bf16)
        for j in range(SSM_PLANES):
            planes[_rows(j * pitch, rc), :] = _dot(uc, b_ref[:, j * LANES:(j + 1) * LANES], NN)
        return 0
    lax.fori_loop(0, seq // SSM_ROWS, chunk, 0)


def _ssm_rows(planes, rc, pitch):
    return jnp.concatenate([planes[_rows(j * pitch, rc), :].astype(bf16) for j in range(SSM_PLANES)], axis=1)


def _ssm_scan(planes_list, l1, l2, seq, pitch, reverse=False):
    def step(s, hs):
        t = seq - 1 - s if reverse else s
        out = []
        for planes, h in zip(planes_list, hs):
            h = h * l1 + pltpu.roll(h, 4, 0) * l2 + planes[pl.ds(t, SUBLANES, stride=pitch), :]
            planes[pl.ds(t, SUBLANES, stride=pitch), :] = h
            out.append(h)
        return tuple(out)
    zero = jnp.zeros((SUBLANES, LANES), f32)
    lax.fori_loop(0, seq, step, tuple(zero for _ in planes_list), unroll=SSM_UNROLL)


def _ssm_fwd(u, b_big, c_big, lslab, dskip, seq, name):
    t_all, w = u.shape
    nb, gw, pitch = t_all // seq, SSM_GB * SSM_GROUP, _ssm_pitch(seq)
    assert gw == LANES

    def body(u_ref, b_ref, c_ref, l_ref, d_ref, y_ref, *planes):
        l1, l2 = l_ref[0:SUBLANES, :], l_ref[SUBLANES:2 * SUBLANES, :]
        for e in range(nb):
            _ssm_project_in(u_ref, b_ref, planes[e], e, seq, pitch)
        _ssm_scan(planes, l1, l2, seq, pitch)
        for e in range(nb):
            def chunk(rc, _, e=e):
                rows = _rows(e * seq, rc)
                y_ref[rows, :] = _dot(_ssm_rows(planes[e], rc, pitch), c_ref[...], NN) + d_ref[...] * u_ref[rows, :]
                return 0
            lax.fori_loop(0, seq // SSM_ROWS, chunk, 0)

    return pl.pallas_call(
        body, out_shape=SDS((t_all, w), f32), grid=(w // gw,),
        in_specs=[pl.BlockSpec((t_all, gw), lambda k: (0, k)), pl.BlockSpec((None,) + b_big.shape[1:], lambda k: (k, 0, 0)),
                  pl.BlockSpec((None,) + c_big.shape[1:], lambda k: (k, 0, 0)),
                  pl.BlockSpec((None,) + lslab.shape[1:], lambda k: (k, 0, 0)), pl.BlockSpec((1, gw), lambda k: (0, k))],
        out_specs=pl.BlockSpec((t_all, gw), lambda k: (0, k)),
        scratch_shapes=[pltpu.VMEM((SSM_PLANES * pitch, LANES), f32) for _ in range(nb)],
        name=name, compiler_params=_params(1))(u, b_big, c_big, lslab, dskip)


def _ssm_bwd(u, dy, b_big, c_big, lslab, dskip, seq, name, comm=None):
    t_all, w = u.shape
    nb, gw, pitch = t_all // seq, SSM_GB * SSM_GROUP, _ssm_pitch(seq)
    ns = SSM_PLANES * LANES

    def body(u_ref, dy_ref, b_ref, c_ref, l_ref, d_ref, du_ref, db_ref, dc_ref, dl_ref, dd_ref, *planes):
        hp, ap = planes[:nb], planes[nb:]
        l1, l2 = l_ref[0:SUBLANES, :], l_ref[SUBLANES:2 * SUBLANES, :]
        for e in range(nb):
            _ssm_project_in(u_ref, b_ref, hp[e], e, seq, pitch)
        _ssm_scan(hp, l1, l2, seq, pitch)
        dd_ref[...] = jnp.zeros_like(dd_ref)
        dc_ref[...] = jnp.zeros_like(dc_ref)
        db_ref[...] = jnp.zeros_like(db_ref)
        for e in range(nb):
            def chunk(rc, _, e=e):
                rows = _rows(e * seq, rc)
                dyc = dy_ref[rows, :]
                dyb = dyc.astype(bf16)
                for j in range(SSM_PLANES):
                    ap[e][_rows(j * pitch, rc), :] = _dot(dyb, c_ref[j * LANES:(j + 1) * LANES, :], NT)
                dd_ref[...] += _sum8(dyc * u_ref[rows, :])
                dc_ref[...] += _dot(_ssm_rows(hp[e], rc, pitch), dyb, TN)
                return 0
            lax.fori_loop(0, seq // SSM_ROWS, chunk, 0)

        def step(s, carry):
            t = seq - 1 - s
            out = []
            for e in range(nb):
                a, s1, s2 = carry[e]
                a = a * l1 - pltpu.roll(a, 4, 0) * l2 + ap[e][pl.ds(t, SUBLANES, stride=pitch), :]
                ap[e][pl.ds(t, SUBLANES, stride=pitch), :] = a
                hprev = hp[e][pl.ds(jnp.maximum(t - 1, 0), SUBLANES, stride=pitch), :] * jnp.where(t > 0, 1.0, 0.0)
                out.append((a, s1 + a * hprev, s2 + a * pltpu.roll(hprev, 4, 0)))
            return tuple(out)
        zero = jnp.zeros((SUBLANES, LANES), f32)
        fin = lax.fori_loop(0, seq, step, tuple((zero, zero, zero) for _ in range(nb)), unroll=SSM_UNROLL)
        dl_ref[0:SUBLANES, :] = sum(f[1] for f in fin)
        dl_ref[SUBLANES:2 * SUBLANES, :] = sum(f[2] for f in fin)

        for e in range(nb):
            def chunk2(rc, _, e=e):
                rows = _rows(e * seq, rc)
                ar = _ssm_rows(ap[e], rc, pitch)
                du_ref[rows, :] = (_dot(ar, b_ref[...], NT) + d_ref[...] * dy_ref[rows, :]).astype(bf16)
                db_ref[...] += _dot(u_ref[rows, :].astype(bf16), ar, TN)
                return 0
            lax.fori_loop(0, seq // SSM_ROWS, chunk2, 0)

    col = pl.BlockSpec((t_all, gw), lambda k: (0, k))
    per = lambda s: pl.BlockSpec((None,) + s[1:], lambda k: (k, 0, 0))
    ng = w // gw
    res, extra = _call(
        body, ins=[u, dy, b_big, c_big, lslab, dskip],
        out_shape=[SDS((t_all, w), bf16), SDS(b_big.shape, f32), SDS(c_big.shape, f32), SDS((ng, 2 * SUBLANES, LANES), f32),
                   SDS((SUBLANES, w), f32)],
        grid=(ng,),
        in_specs=[col, col, per(b_big.shape), per(c_big.shape), per(lslab.shape), pl.BlockSpec((1, gw), lambda k: (0, k))],
        out_specs=[col, per(b_big.shape), per(c_big.shape), per((ng, 2 * SUBLANES, LANES)), pl.BlockSpec((SUBLANES, gw), lambda k: (0, k))],
        scratch_shapes=[pltpu.VMEM((SSM_PLANES * pitch, LANES), f32) for _ in range(2 * nb)], name=name, comm=comm)
    return (*res, extra)


def _ssm_disc_fwd(lam_re, lam_im, dt, b_re, b_im, name):
    def body(a_ref, b_ref, dt_ref, br_ref, bi_ref, lr_ref, li_ref, cr_ref, ci_ref, bbr_ref, bbi_ref):
        a, b, dtv = a_ref[...], b_ref[...], dt_ref[...]
        mag, ang = jnp.exp(a * dtv), b * dtv
        lr, li = mag * jnp.cos(ang), mag * jnp.sin(ang)
        nr, den = lr - 1.0, a * a + b * b
        cr, ci = (nr * a + li * b) / den, (li * a - nr * b) / den
        lr_ref[...], li_ref[...], cr_ref[...], ci_ref[...] = lr, li, cr, ci
        bbr_ref[...] = cr * br_ref[...] - ci * bi_ref[...]
        bbi_ref[...] = cr * bi_ref[...] + ci * br_ref[...]
    c, m = SDS(lam_re.shape, f32), SDS(b_re.shape, f32)
    return pl.pallas_call(body, out_shape=(c, c, c, c, m, m), name=name)(lam_re, lam_im, dt, b_re, b_im)


def _ssm_disc_bwd(lam_re, lam_im, dt, b_re, b_im, g_lr, g_li, g_bbr, g_bbi, name):
    def body(a_ref, b_ref, dt_ref, br_ref, bi_ref, glr_ref, gli_ref, gbr_ref, gbi_ref, da_ref, db_ref, ddt_ref, dbr_ref, dbi_ref):
        a, b, dtv = a_ref[...], b_ref[...], dt_ref[...]
        mag, ang = jnp.exp(a * dtv), b * dtv
        cs, sn = jnp.cos(ang), jnp.sin(ang)
        lr, li = mag * cs, mag * sn
        nr, den = lr - 1.0, a * a + b * b
        cr, ci = (nr * a + li * b) / den, (li * a - nr * b) / den
        gbr, gbi, brv, biv = gbr_ref[...], gbi_ref[...], br_ref[...], bi_ref[...]
        dbr_ref[...] = cr * gbr + ci * gbi
        dbi_ref[...] = cr * gbi - ci * gbr
        dcr = jnp.sum(brv * gbr + biv * gbi, axis=1, keepdims=True)
        dci = jnp.sum(brv * gbi - biv * gbr, axis=1, keepdims=True)
        dnum_r, dnum_i = dcr / den, dci / den
        dden = -(dcr * cr + dci * ci) / den
        dnr = dnum_r * a - dnum_i * b
        dli = gli_ref[...] + dnum_r * b + dnum_i * a
        dlr = glr_ref[...] + dnr
        dmag, dang = dlr * cs + dli * sn, dli * lr - dlr * li
        dadt = dmag * mag
        da_ref[...] = dnum_r * nr + dnum_i * li + dden * 2.0 * a + dadt * dtv
        db_ref[...] = dnum_r * li - dnum_i * nr + dden * 2.0 * b + dang * dtv
        ddt_ref[...] = dadt * a + dang * b
    c, m = SDS(lam_re.shape, f32), SDS(b_re.shape, f32)
    return pl.pallas_call(body, out_shape=(c, c, c, m, m), name=name)(lam_re, lam_im, dt, b_re, b_im, g_lr, g_li, g_bbr, g_bbi)


def _place():
    x, y, c = lax.axis_index("x"), lax.axis_index("y"), lax.axis_index("c")
    return x, y, c, 2 * x + y


def _half_axis(shape, ax):
    return 0 if shape[0] == 2 else (3 - ax)


def _sub(ref, axis, start, size):
    idx = [slice(None)] * len(ref.shape)
    idx[axis] = pl.ds(start, size)
    return ref.at[tuple(idx)]


def _region(ref, full_shape, ax, slot=None, half=None):
    if slot is not None:
        n = full_shape[ax] // N_CHIPS
        ref = _sub(ref, ax, slot * n, n)
    if half is not None:
        ha = _half_axis(full_shape, ax)
        n = full_shape[ha] // 2
        ref = _sub(ref, ha, half * n, n)
    return ref


def _halved(shape, axis):
    return tuple(s // 2 if a == axis else s for a, s in enumerate(shape))


class _Comm:
    def __init__(self, ins, out_shapes, aliases, scratch, start, finish):
        self.ins, self.out_shapes, self.aliases, self.scratch, self.start, self.finish = ins, out_shapes, aliases, scratch, start, finish


def _call(body, *, ins, in_specs, out_shape, out_specs, grid, scratch_shapes, name, comm=None):
    if comm is None:
        res = pl.pallas_call(body, out_shape=tuple(out_shape), grid=grid, in_specs=list(in_specs), out_specs=tuple(out_specs),
                             scratch_shapes=list(scratch_shapes), name=name, compiler_params=_params(len(grid)))(*ins)
        return list(res), []
    n_in, n_out, n_scr, c_in, c_out = len(ins), len(out_shape), len(scratch_shapes), len(comm.ins), len(comm.out_shapes)

    def fused(*refs):
        pos = [n_in, n_in + c_in, n_in + c_in + n_out, n_in + c_in + n_out + c_out, n_in + c_in + n_out + c_out + n_scr]
        in_refs, cin, out_refs, cout, scr, cscr = (refs[:pos[0]], refs[pos[0]:pos[1]], refs[pos[1]:pos[2]], refs[pos[2]:pos[3]],
                                                   refs[pos[3]:pos[4]], refs[pos[4]:])
        ids = [pl.program_id(a) for a in range(len(grid))]
        first, last = ids[0] == 0, ids[0] == grid[0] - 1
        for a in range(1, len(grid)):
            first, last = first & (ids[a] == 0), last & (ids[a] == grid[a] - 1)

        @pl.when(first)
        def _():
            comm.start(cin, cout, cscr)

        body(*in_refs, *out_refs, *scr)

        @pl.when(last)
        def _():
            comm.finish(cin, cout, cscr)

    res = pl.pallas_call(
        fused, out_shape=tuple(out_shape) + tuple(comm.out_shapes), grid=grid, in_specs=list(in_specs) + [ANY] * c_in,
        out_specs=tuple(out_specs) + tuple([ANY] * c_out), scratch_shapes=list(scratch_shapes) + list(comm.scratch),
        input_output_aliases={n_in + i: n_out + o for i, o in comm.aliases}, name=name, compiler_params=_params(len(grid)))(*ins, *comm.ins)
    return list(res[:n_out]), list(res[n_out:])


def _comm_only(comm, name):
    c_in, c_out = len(comm.ins), len(comm.out_shapes)

    def body(*refs):
        cin, cout, cscr = refs[:c_in], refs[c_in:c_in + c_out], refs[c_in + c_out:]
        comm.start(cin, cout, cscr)
        comm.finish(cin, cout, cscr)

    return pl.pallas_call(body, out_shape=tuple(comm.out_shapes), in_specs=[ANY] * c_in, out_specs=tuple([ANY] * c_out),
                          scratch_shapes=list(comm.scratch), input_output_aliases=dict(comm.aliases), name=name)(*comm.ins)


def _gather_plan(shards, axes):
    n = len(shards)
    fulls = [tuple(s * N_CHIPS if a == ax else s for a, s in enumerate(sh.shape)) for sh, ax in zip(shards, axes)]
    own = 6

    def copies(src, dst, scr):
        send_sems, recv_sems = scr
        x, y, c, p = _place()
        chips = [(1 - x, y), (x, 1 - y), (1 - x, 1 - y)]
        slots = [2 * cx + cy for cx, cy in chips]

        def copy(a, k, slot, half, to, from_shard=False):
            where = _region(dst[a], fulls[a], axes[a], slot, half)
            source = where
            if from_shard:
                ha = _half_axis(fulls[a], axes[a])
                hn = fulls[a][ha] // 2
                source = _sub(src[a], ha, half * hn, hn)
            return pltpu.make_async_remote_copy(src_ref=source, dst_ref=where, send_sem=send_sems.at[a, k],
                                                recv_sem=recv_sems.at[a, k], device_id=to, device_id_type=MESH)

        parts = range(n)
        mine = [pltpu.make_async_remote_copy(src_ref=src[a], dst_ref=_region(dst[a], fulls[a], axes[a], p),
                                             send_sem=send_sems.at[a, own], recv_sem=recv_sems.at[a, own],
                                             device_id=(x, y, 1 - c), device_id_type=MESH) for a in parts]
        first = [copy(a, j, p, c, (*chips[j], c), True) for a in parts for j in range(3)]
        landed = [copy(a, j, slots[j], c, (x, y, c)) for a in parts for j in range(3)]
        passed = [copy(a, 3 + j, slots[j], c, (x, y, 1 - c)) for a in parts for j in range(3)]
        handed = [copy(a, 3 + j, slots[j], 1 - c, (x, y, c)) for a in parts for j in range(3)]
        return mine, first, landed, passed, handed

    def start(src, dst, scr):
        mine, first, _, _, _ = copies(src, dst, scr)
        for cp in first + mine:
            cp.start()

    def finish(src, dst, scr):
        mine, first, landed, passed, handed = copies(src, dst, scr)
        for arrived, fwd in zip(landed, passed):
            arrived.wait_recv()
            fwd.start()
        for cp in handed + mine:
            cp.wait_recv()
        for cp in first + passed + mine:
            cp.wait_send()

    return _Comm(list(shards), [SDS(f, s.dtype) for f, s in zip(fulls, shards)], [],
                 [pltpu.SemaphoreType.DMA((n, 7)), pltpu.SemaphoreType.DMA((n, 7))], start, finish)


def _all_gather(shards, axes, name):
    return _comm_only(_gather_plan(shards, axes), name)


def _swap_halves(grads, axes, name):
    n = len(grads)
    shapes = [g.shape for g in grads]

    def body(*refs):
        src, dst = refs[:n], refs[n:2 * n]
        send_sems, recv_sems = refs[2 * n:]
        x, y, c, _ = _place()
        cps = [pltpu.make_async_remote_copy(src_ref=_region(src[a], shapes[a], axes[a], None, 1 - c), dst_ref=dst[a],
                                            send_sem=send_sems.at[a], recv_sem=recv_sems.at[a],
                                            device_id=(x, y, 1 - c), device_id_type=MESH) for a in range(n)]
        for cp in cps:
            cp.start()
        for cp in cps:
            cp.wait()

    outs = tuple(SDS(_halved(s, _half_axis(s, ax)), g.dtype) for s, ax, g in zip(shapes, axes, grads))
    return pl.pallas_call(body, out_shape=outs, in_specs=[ANY] * n, out_specs=tuple([ANY] * n),
                          scratch_shapes=[pltpu.SemaphoreType.DMA((n,)), pltpu.SemaphoreType.DMA((n,))], name=name)(*grads)


def _row_block(rows, row_bytes, limit=3 << 20):
    for b in (1024, 512, 256, 128, 64, 32, 16, 8):
        if rows % b == 0 and b * row_bytes <= limit:
            return b
    return rows


def _add_own_half(g, other, ax, cidx, name):
    _, kp, np_ = other.shape
    ha = _half_axis(g.shape, ax)
    ks, ns = (kp // N_CHIPS, np_) if ax == 1 else (kp, np_ // N_CHIPS)
    bk = _row_block(ks, ns * 4)
    nkb = ks // bk

    def g_map(q, i, cref):
        c = cref[0]
        if ax == 1:
            return (c, q * nkb + i, 0) if ha == 0 else (0, q * nkb + i, c)
        return (c, i, q) if ha == 0 else (0, c * nkb + i, q)

    def o_map(q, i, cref):
        return (0, q * nkb + i, 0) if ax == 1 else (0, i, q)

    def body(c_ref, g_ref, o_ref, send_ref, land_ref):
        del c_ref
        s = (g_ref[...].astype(f32) + o_ref[...].astype(f32)).astype(send_ref.dtype)
        send_ref[...] = s
        land_ref[...] = s

    out = pl.BlockSpec((None, bk, ns), lambda q, i, cref: (q, i, 0))
    grid_spec = pltpu.PrefetchScalarGridSpec(
        num_scalar_prefetch=1, grid=(N_CHIPS, nkb),
        in_specs=[pl.BlockSpec((None, bk, ns), g_map), pl.BlockSpec((None, bk, ns), o_map)], out_specs=(out, out))
    shape = SDS((N_CHIPS, ks, ns), g.dtype)
    return pl.pallas_call(body, out_shape=(shape, shape), grid_spec=grid_spec, name=name, compiler_params=_params(2))(cidx, g, other)


def _owner_plan(sends, lands):
    n = len(sends)

    def copies(cin, dst, scr):
        src = cin[:n]
        send_sems, recv_sems = scr
        x, y, c, p = _place()
        chips = [(1 - x, y), (x, 1 - y), (1 - x, 1 - y)]
        slots = [2 * cx + cy for cx, cy in chips]
        out = [pltpu.make_async_remote_copy(src_ref=src[a].at[slots[j]], dst_ref=dst[a].at[p], send_sem=send_sems.at[a, j],
                                            recv_sem=recv_sems.at[a, j], device_id=(*chips[j], c), device_id_type=MESH)
               for a in range(n) for j in range(3)]
        back = [pltpu.make_async_remote_copy(src_ref=src[a].at[p], dst_ref=dst[a].at[slots[j]], send_sem=send_sems.at[a, j],
                                             recv_sem=recv_sems.at[a, j], device_id=(x, y, c), device_id_type=MESH)
                for a in range(n) for j in range(3)]
        return out, back

    def start(cin, dst, scr):
        for cp in copies(cin, dst, scr)[0]:
            cp.start()

    def finish(cin, dst, scr):
        out, back = copies(cin, dst, scr)
        for cp in back:
            cp.wait_recv()
        for cp in out:
            cp.wait_send()

    return _Comm(list(sends) + list(lands), [SDS(l.shape, l.dtype) for l in lands], [(n + a, a) for a in range(n)],
                 [pltpu.SemaphoreType.DMA((n, 3)), pltpu.SemaphoreType.DMA((n, 3))], start, finish)


def _sum_chips(stack, shard_shape, ax, cidx, name):
    _, ks, ns = stack.shape
    ha = _half_axis(shard_shape, ax)
    bk = _row_block(ks, ns * 4 * N_CHIPS)
    nkb = ks // bk

    def o_map(i, cref):
        c = cref[0]
        return (c, i, 0) if ha == 0 else ((0, c * nkb + i, 0) if ha == 1 else (0, i, c))

    def body(c_ref, s_ref, o_ref):
        del c_ref
        acc = s_ref[0].astype(f32)
        for q in range(1, N_CHIPS):
            acc = acc + s_ref[q].astype(f32)
        o_ref[...] = acc

    grid_spec = pltpu.PrefetchScalarGridSpec(
        num_scalar_prefetch=1, grid=(nkb,), in_specs=[pl.BlockSpec((N_CHIPS, bk, ns), lambda i, cref: (0, i, 0))],
        out_specs=pl.BlockSpec((None, bk, ns), o_map))
    return pl.pallas_call(body, out_shape=SDS(shard_shape, f32), grid_spec=grid_spec, name=name, compiler_params=_params(1))(cidx, stack)


def _join_halves(slices, axes, name):
    n = len(slices)

    def body(*refs):
        dst = refs[n:2 * n]
        send_sems, recv_sems = refs[2 * n:]
        x, y, c, _ = _place()

        def half(a, h):
            ha = _half_axis(slices[a].shape, axes[a])
            hn = slices[a].shape[ha] // 2
            return _sub(dst[a], ha, h * hn, hn)

        cps = [pltpu.make_async_remote_copy(src_ref=half(a, c), dst_ref=half(a, c), send_sem=send_sems.at[a], recv_sem=recv_sems.at[a],
                                            device_id=(x, y, 1 - c), device_id_type=MESH) for a in range(n)]
        for cp in cps:
            cp.start()
        for a in range(n):
            pltpu.make_async_remote_copy(src_ref=half(a, c), dst_ref=half(a, 1 - c), send_sem=send_sems.at[a], recv_sem=recv_sems.at[a],
                                         device_id=(x, y, c), device_id_type=MESH).wait_recv()
        for cp in cps:
            cp.wait_send()

    return pl.pallas_call(
        body, out_shape=tuple(SDS(s.shape, s.dtype) for s in slices), in_specs=[ANY] * n, out_specs=tuple([ANY] * n),
        scratch_shapes=[pltpu.SemaphoreType.DMA((n,)), pltpu.SemaphoreType.DMA((n,))],
        input_output_aliases={a: a for a in range(n)}, name=name)(*slices)


def _core_index():
    return jnp.reshape(lax.axis_index("c"), (1,)).astype(jnp.int32)


def _reduce_begin(grads, axes, tag):
    cidx = _core_index()
    others = _swap_halves(grads, axes, f"rs_swap_{tag}")
    pairs = [_add_own_half(g, o, ax, cidx, f"rs_add_{tag}_{a}") for a, (g, o, ax) in enumerate(zip(grads, others, axes))]
    return _owner_plan([s for s, _ in pairs], [l for _, l in pairs])


def _reduce_end(stacks, shapes, axes, tag):
    cidx = _core_index()
    shard_shapes = [tuple(s // N_CHIPS if i == ax else s for i, s in enumerate(sh)) for sh, ax in zip(shapes, axes)]
    slices = [_sum_chips(s, sh, ax, cidx, f"rs_sum_{tag}_{a}") for a, (s, sh, ax) in enumerate(zip(stacks, shard_shapes, axes))]
    return _join_halves(slices, axes, f"rs_join_{tag}")


def _reduce_scatter(grads, axes, tag):
    stacks = _comm_only(_reduce_begin(grads, axes, tag), f"rs_owner_{tag}")
    return _reduce_end(stacks, [g.shape for g in grads], axes, tag)


SMALL_COLS = 256


def _pack(arrays, rows_multiple):
    flat = jnp.concatenate([a.reshape(-1).astype(f32) for a in arrays])
    rows = -(-flat.shape[0] // SMALL_COLS)
    rows = -(-rows // rows_multiple) * rows_multiple
    flat = jnp.pad(flat, (0, rows * SMALL_COLS - flat.shape[0]))
    return flat.reshape(1, rows, SMALL_COLS)


def _unpack(buf, shapes):
    flat, out, off = buf.reshape(-1), [], 0
    for s in shapes:
        n = math.prod(s)
        out.append(flat[off:off + n].reshape(s))
        off += n
    return out


def _block_diag_in(bb):
    g, p, c = bb.shape
    k = g // SSM_GB
    eye = jnp.eye(SSM_GB, dtype=bb.dtype)
    return jnp.einsum("kgpc,gh->kgchp", bb.reshape(k, SSM_GB, p, c), eye).reshape(k, SSM_GB * c, SSM_GB * p)


def _block_diag_out(cc):
    g, c, p = cc.shape
    k = g // SSM_GB
    eye = jnp.eye(SSM_GB, dtype=cc.dtype)
    return jnp.einsum("kgcp,gh->kgphc", cc.reshape(k, SSM_GB, c, p), eye).reshape(k, SSM_GB * p, SSM_GB * c)


def _diag_in(db, p, c):
    k = db.shape[0]
    return jnp.einsum("kgcgp->kgpc", db.reshape(k, SSM_GB, c, SSM_GB, p)).reshape(k * SSM_GB, p, c)


def _diag_out(dc, p, c):
    k = dc.shape[0]
    return jnp.einsum("kgpgc->kgcp", dc.reshape(k, SSM_GB, p, SSM_GB, c)).reshape(k * SSM_GB, c, p)


def _state_slab(v):
    g, p = v.shape
    return v.reshape(g // SSM_GB, SSM_GB * p // LANES, LANES)


BIG = ("ab_w_in", "ab_w_out", "ssm_w_in", "ssm_w_glu", "xa_w_q", "xa_w_kv", "xa_w_o", "ffn_w_up", "ffn_w_down")
BIG_AXIS = dict(ab_w_in=2, ab_w_out=1, ssm_w_in=1, ssm_w_glu=2, xa_w_q=1, xa_w_kv=2, xa_w_o=1, ffn_w_up=2, ffn_w_down=1)
SMALL_REPL = ("norm_mix", "norm_xattn", "norm_ffn", "norm_mem", "norm_final", "pool_w", "pool_scale", "ssm_lam_re", "ssm_lam_im",
              "ssm_log_dt", "ssm_b_re", "ssm_b_im", "ssm_c_re", "ssm_c_im", "ffn_conv_b")
SMALL_SHARDED = ("ssm_d", "ffn_conv_w")
FIRST_MIXER = ("ab_w_in", "ab_w_out")
WEIGHTS = ("norm_mix", "norm_xattn", "norm_ffn", "norm_mem", "norm_final", "ab_w_in", "pool_w", "pool_scale", "ab_w_out", "ssm_w_in",
           "ssm_lam_re", "ssm_lam_im", "ssm_log_dt", "ssm_b_re", "ssm_b_im", "ssm_c_re", "ssm_c_im", "ssm_d", "ssm_w_glu", "xa_w_q",
           "xa_w_kv", "xa_w_o", "ffn_w_up", "ffn_conv_w", "ffn_conv_b", "ffn_w_down")


class _Reducer:
    def __init__(self):
        self.done, self.groups = {}, 0

    def begin(self, keys, gw):
        self.groups += 1
        return _reduce_begin([gw[k] for k in keys], [BIG_AXIS.get(k[0], 1) for k in keys], f"g{self.groups}")

    def end(self, keys, gw, stacks):
        slices = _reduce_end(stacks, [gw[k].shape for k in keys], [BIG_AXIS.get(k[0], 1) for k in keys], f"g{self.groups}")
        self.done.update(zip(keys, slices))


def _local_step(xf, memf, tgt, w, wf, conv_w, ssm_d, seq, late_weights=None, reducer=None):
    d = xf.shape[1]
    depth = w["norm_mix"].shape[0]
    wf = dict(wf)
    sbw = wf["ab_w_in"].shape[2] // 4
    row = lambda a: a.reshape(1, -1)

    gs, ps = w["ssm_lam_re"].shape[1:]
    col = lambda a: a.reshape(gs * ps, 1)
    lam_re, lam_im = col(w["ssm_lam_re"][0]), col(w["ssm_lam_im"][0])
    dt = col(jnp.broadcast_to(jnp.exp(w["ssm_log_dt"][0])[:, None], (gs, ps)))
    b_re, b_im = w["ssm_b_re"][0].reshape(gs * ps, -1), w["ssm_b_im"][0].reshape(gs * ps, -1)
    lb_re, lb_im, _, _, bb_re, bb_im = _ssm_disc_fwd(lam_re, lam_im, dt, b_re, b_im, "ssm_disc")
    cgrp = b_re.shape[1]
    b_big = jnp.concatenate([_block_diag_in(bb_re.reshape(gs, ps, cgrp)), _block_diag_in(bb_im.reshape(gs, ps, cgrp))], axis=2).astype(bf16)
    c_big = jnp.concatenate([_block_diag_out(w["ssm_c_re"][0]), -_block_diag_out(w["ssm_c_im"][0])], axis=1).astype(bf16)
    lr_s, li_s = _state_slab(lb_re.reshape(gs, ps)), _state_slab(lb_im.reshape(gs, ps))
    lslab = jnp.concatenate([lr_s, lr_s, -li_s, li_s], axis=1)

    mem_n = _norm_fwd(memf, row(w["norm_mem"]), "norm_mem")
    kv = None
    xs, saved = [xf], []
    cur = xf
    for l in range(depth):
        sv = {}
        h = _norm_fwd(cur, row(w["norm_mix"][l]), f"norm_mix{l}")
        sv["h"] = h
        if l % 2 == 0:
            qkv = _mm(h, wf["ab_w_in"], mode="nn", b_l=0, n=3 * sbw, out_dtype=bf16, name=f"qkv{l}")
            u = _mm(h, wf["ab_w_in"], mode="nn", b_l=0, b_n0=3 * sbw, n=sbw, out_dtype=f32, name=f"poolin{l}")
            plan, names = late_weights if (late_weights is not None and l == 0) else (None, ())
            mix, ltot, late = _sb_fwd(qkv, seq, f"sb_fwd{l}", comm=plan)
            wf.update(zip(names, late))
            pooled, mix = _pool_fwd(u, mix, w["pool_w"][0], w["pool_scale"], seq, f"pool_fwd{l}")
            sv.update(qkv=qkv, mix=mix, ltot=ltot, pooled=pooled)
            cur = _mm(mix, wf["ab_w_out"], mode="nn", b_l=0, res=cur, out_dtype=f32, name=f"mixout{l}")
        else:
            us = _mm(h, wf["ssm_w_in"], mode="nn", b_l=0, out_dtype=f32, name=f"ssmin{l}")
            ys = _ssm_fwd(us, b_big, c_big, lslab, ssm_d, seq, f"ssm_fwd{l}")
            gl = _gelu_fwd(ys, f"gelu{l}")
            glu = _mm(gl, wf["ssm_w_glu"], mode="nn", b_l=0, out_dtype=f32, name=f"glu{l}")
            sv.update(us=us, ys=ys, gl=gl, glu=glu)
            cur = _glu_fwd(glu, cur, f"glugate{l}")
        sv["x1"] = cur
        if kv is None:
            kv = [_mm(mem_n, wf["xa_w_kv"], mode="nn", b_l=j, out_dtype=bf16, name=f"kv{j}") for j in range(depth)]
        hx = _norm_fwd(cur, row(w["norm_xattn"][l]), f"norm_xa{l}")
        qx = _mm(hx, wf["xa_w_q"], mode="nn", b_l=l, out_dtype=bf16, name=f"xaq{l}")
        ox = _xa_fwd(qx, kv[l], seq, f"xa_fwd{l}")
        cur = _mm(ox, wf["xa_w_o"], mode="nn", b_l=l, res=cur, out_dtype=f32, name=f"xao{l}")
        sv.update(hx=hx, qx=qx, ox=ox, x2=cur)
        hf = _norm_fwd(cur, row(w["norm_ffn"][l]), f"norm_ffn{l}")
        up = _mm(hf, wf["ffn_w_up"], mode="nn", b_l=l, out_dtype=bf16, name=f"ffnup{l}")
        act = _ffn_gate_fwd(up, conv_w[l], row(w["ffn_conv_b"][l]), seq, f"ffn_gate{l}")
        cur = _mm(act, wf["ffn_w_down"], mode="nn", b_l=l, res=cur, out_dtype=f32, name=f"ffndown{l}")
        sv.update(hf=hf, up=up, act=act)
        saved.append(sv)
        xs.append(cur)

    dx, g_final8, loss8 = _loss_head(cur, tgt, row(w["norm_final"]), "loss_head")

    gw = {}
    small = {"norm_final": jnp.sum(g_final8, axis=0)}
    g_mix, g_xa, g_ffn, g_cw, g_cb = [None] * depth, [None] * depth, [None] * depth, [None] * depth, [None] * depth
    dkv = [None] * depth

    pending = []

    def wgrad(key, a, b, l, **kw):
        kw.setdefault("bk", 1024)
        gw[key, l] = _mm(a, b, mode="tn", out_dtype=bf16, out_l=0, out_layers=1, name=f"dw_{key}{l}", **kw)
        pending.append((key, l))

    def reduce_beside():
        if reducer is None or not pending:
            return None, []
        keys = list(pending)
        pending.clear()
        return reducer.begin(keys, gw), keys

    for l in reversed(range(depth)):
        sv = saved[l]
        dact = _mm(dx, wf["ffn_w_down"], mode="nt", b_l=l, out_dtype=bf16, name=f"d_act{l}")
        wgrad("ffn_w_down", sv["act"], dx, l)
        dup, dcw8, dcb8 = _ffn_gate_bwd(dact, sv["up"], conv_w[l], row(w["ffn_conv_b"][l]), seq, f"ffn_gate_bwd{l}")
        g_cw[l], g_cb[l] = jnp.sum(dcw8, axis=1), jnp.sum(dcb8, axis=0)
        wgrad("ffn_w_up", sv["hf"], dup, l)
        dhf = _mm(dup, wf["ffn_w_up"], mode="nt", b_l=l, out_dtype=f32, name=f"d_hf{l}")
        dx, g8 = _norm_bwd(dhf, sv["x2"], dx, row(w["norm_ffn"][l]), f"norm_ffn_bwd{l}")
        g_ffn[l] = jnp.sum(g8, axis=0)
        dox = _mm(dx, wf["xa_w_o"], mode="nt", b_l=l, out_dtype=bf16, name=f"d_ox{l}")
        wgrad("xa_w_o", sv["ox"], dx, l)
        dqx, dkv[l] = _xa_bwd(sv["qx"], kv[l], dox, seq, f"xa_bwd{l}")
        wgrad("xa_w_q", sv["hx"], dqx, l)
        dhx = _mm(dqx, wf["xa_w_q"], mode="nt", b_l=l, out_dtype=f32, name=f"d_hx{l}")
        dx, g8 = _norm_bwd(dhx, sv["x1"], dx, row(w["norm_xattn"][l]), f"norm_xa_bwd{l}")
        g_xa[l] = jnp.sum(g8, axis=0)
        if l % 2 == 0:
            dmix = _mm(dx, wf["ab_w_out"], mode="nt", b_l=0, out_dtype=f32, name=f"d_mix{l}")
            comm, keys = reduce_beside()
            dq, dk, dv, stacks = _sb_bwd(sv["qkv"], sv["ltot"], dmix, seq, f"sb_bwd{l}", comm=comm)
            if comm is not None:
                reducer.end(keys, gw, stacks)
            wgrad("ab_w_out", sv["mix"], dx, 0)
            du, dpw, dps8 = _pool_bwd(dmix, sv["pooled"], w["pool_w"][0], w["pool_scale"], seq, f"pool_bwd{l}")
            small["pool_w"], small["pool_scale"] = dpw[None], jnp.sum(dps8, axis=0)[None]
            dproj = jnp.concatenate([dq, dk, dv, du], axis=1)
            wgrad("ab_w_in", sv["h"], dproj, 0)
            dh = _mm(dproj, wf["ab_w_in"], mode="nt", b_l=0, out_dtype=f32, name=f"d_h{l}")
        else:
            dglu = _glu_bwd(dx, sv["glu"], f"glugate_bwd{l}")
            dgl = _mm(dglu, wf["ssm_w_glu"], mode="nt", b_l=0, out_dtype=f32, name=f"d_gelu{l}")
            dys = _gelu_bwd(dgl, sv["ys"], f"gelu_bwd{l}")
            comm, keys = reduce_beside()
            dus, db_big, dc_big, dl, dd8, stacks = _ssm_bwd(sv["us"], dys, b_big, c_big, lslab, ssm_d, seq, f"ssm_bwd{l}", comm=comm)
            if comm is not None:
                reducer.end(keys, gw, stacks)
            wgrad("ssm_w_glu", sv["gl"], dglu, 0)
            small["ssm_d"] = jnp.sum(dd8, axis=0)[None]
            half = SSM_PLANES // 2
            g_lr = (dl[:, 0:half] + dl[:, half:SUBLANES]).reshape(gs * ps, 1)
            g_li = (dl[:, SUBLANES + half:] - dl[:, SUBLANES:SUBLANES + half]).reshape(gs * ps, 1)
            g_bbr = _diag_in(db_big[:, :, :SSM_GB * ps], ps, cgrp).reshape(gs * ps, cgrp)
            g_bbi = _diag_in(db_big[:, :, SSM_GB * ps:], ps, cgrp).reshape(gs * ps, cgrp)
            d_a, d_b, d_dt, d_br, d_bi = _ssm_disc_bwd(lam_re, lam_im, dt, b_re, b_im, g_lr, g_li, g_bbr, g_bbi, "ssm_disc_bwd")
            small["ssm_lam_re"], small["ssm_lam_im"] = d_a.reshape(1, gs, ps), d_b.reshape(1, gs, ps)
            small["ssm_log_dt"] = (jnp.sum(d_dt.reshape(gs, ps), axis=1) * dt.reshape(gs, ps)[:, 0])[None]
            small["ssm_b_re"], small["ssm_b_im"] = d_br.reshape(1, gs, ps, cgrp), d_bi.reshape(1, gs, ps, cgrp)
            small["ssm_c_re"] = _diag_out(dc_big[:, :SSM_GB * ps], ps, cgrp)[None]
            small["ssm_c_im"] = -_diag_out(dc_big[:, SSM_GB * ps:], ps, cgrp)[None]
            wgrad("ssm_w_in", sv["h"], dus, 0)
            dh = _mm(dus, wf["ssm_w_in"], mode="nt", b_l=0, out_dtype=f32, name=f"d_h{l}")
        dx, g8 = _norm_bwd(dh, xs[l], dx, row(w["norm_mix"][l]), f"norm_mix_bwd{l}")
        g_mix[l] = jnp.sum(g8, axis=0)

    dmem_n = None
    for l in range(depth):
        wgrad("xa_w_kv", mem_n, dkv[l], l, bk=mem_n.shape[0])
        dmem_n = _mm(dkv[l], wf["xa_w_kv"], mode="nt", b_l=l, res=dmem_n, out_dtype=f32, name=f"d_memn{l}")
    small["norm_mem"] = jnp.sum(_norm_bwd_gain_only(dmem_n, memf, "norm_mem_bwd"), axis=0)
    small["norm_mix"], small["norm_xattn"], small["norm_ffn"] = jnp.stack(g_mix), jnp.stack(g_xa), jnp.stack(g_ffn)
    small["ffn_conv_w"], small["ffn_conv_b"] = jnp.stack(g_cw), jnp.stack(g_cb)
    return loss8, dx, gw, small, pending


def _step(x, mem, loss_target, w, m, v):
    nb, seq, d = x.shape
    t_all = nb * seq
    depth = w["norm_mix"].shape[0]
    chip = 2 * lax.axis_index("x") + lax.axis_index("y")

    small_mine = _pack([w[k] for k in SMALL_SHARDED], SUBLANES)
    gathered = _all_gather([w[k].astype(bf16) for k in FIRST_MIXER] + [small_mine], [BIG_AXIS[k] for k in FIRST_MIXER] + [1],
                           "gather_first")
    wf = dict(zip(FIRST_MIXER, gathered[:-1]))
    per_chip = gathered[-1].reshape(N_CHIPS, -1)
    pieces = [_unpack(per_chip[q], [w[k].shape for k in SMALL_SHARDED]) for q in range(N_CHIPS)]
    ssm_d = jnp.concatenate([pc[0] for pc in pieces], axis=-1)
    conv_w = jnp.concatenate([pc[1] for pc in pieces], axis=-1)
    ff2 = conv_w.shape[-1]
    late = tuple(k for k in BIG if k not in FIRST_MIXER)
    late_plan = _gather_plan([w[k].astype(bf16) for k in late], [BIG_AXIS[k] for k in late])

    reducer = _Reducer()
    loss8, dx, gw, small, pending = _local_step(x.reshape(t_all, d), mem.reshape(-1, d), loss_target.reshape(t_all, d), w, wf,
                                                conv_w, ssm_d, seq, late_weights=(late_plan, late), reducer=reducer)
    loss = lax.psum(0.5 * jnp.sum(loss8) / d, ("x", "y", "c"))

    small_names = SMALL_REPL + SMALL_SHARDED
    small_full_shapes = [w[k].shape for k in SMALL_REPL] + [(1, d), (depth, 3, ff2)]
    gw["small", 0] = _pack([small[k] for k in small_names], 2 * N_CHIPS * SUBLANES)
    keys = pending + [("small", 0)]
    reducer.end(keys, gw, _comm_only(reducer.begin(keys, gw), "rs_owner_last"))
    g_big = {k: jnp.concatenate([reducer.done[k, l] for l in range(w[k].shape[0])], axis=0) for k in BIG}
    small_all = _all_gather([reducer.done["small", 0]], [1], "gather_small_grads")[0]
    g_small = dict(zip(small_names, _unpack(small_all, small_full_shapes)))
    g_small["ssm_d"] = lax.dynamic_slice_in_dim(g_small["ssm_d"], chip * (d // N_CHIPS), d // N_CHIPS, axis=1)
    g_small["ffn_conv_w"] = lax.dynamic_slice_in_dim(g_small["ffn_conv_w"], chip * (ff2 // N_CHIPS), ff2 // N_CHIPS, axis=2)
    grads = {**g_big, **g_small}

    delta, new_m, new_v = {}, {}, {}
    for k in BIG:
        n_cols = w[k].shape[-1]
        two = lambda a: a.reshape(-1, n_cols)
        dl_, m_, v_ = _adamw(two(w[k]), two(grads[k]), two(m[k]), two(v[k]), f"adamw_{k}")
        delta[k], new_m[k], new_v[k] = dl_.reshape(w[k].shape), m_.reshape(w[k].shape), v_.reshape(w[k].shape)
    pk = lambda tree: _pack([tree[k] for k in small_names], 256)[0]
    small_shapes = [w[k].shape for k in small_names]
    outs = _adamw(pk(w), pk(grads), pk(m), pk(v), "adamw_small")
    for tree, buf in zip((delta, new_m, new_v), outs):
        tree.update(zip(small_names, _unpack(buf, small_shapes)))

    grad_x = dx.reshape(nb, seq, d)
    return (loss, grad_x, *[grads[k] for k in WEIGHTS], *[delta[k] for k in WEIGHTS], *[new_m[k] for k in WEIGHTS],
            *[new_v[k] for k in WEIGHTS])


def kernel(x, mem, norm_mix, norm_xattn, norm_ffn, norm_mem, norm_final, ab_w_in, pool_w, pool_scale, ab_w_out, ssm_w_in, ssm_lam_re, ssm_lam_im, ssm_log_dt, ssm_b_re, ssm_b_im, ssm_c_re, ssm_c_im, ssm_d, ssm_w_glu, xa_w_q, xa_w_kv, xa_w_o, ffn_w_up, ffn_conv_w, ffn_conv_b, ffn_w_down, loss_target, m_norm_mix, m_norm_xattn, m_norm_ffn, m_norm_mem, m_norm_final, m_ab_w_in, m_pool_w, m_pool_scale, m_ab_w_out, m_ssm_w_in, m_ssm_lam_re, m_ssm_lam_im, m_ssm_log_dt, m_ssm_b_re, m_ssm_b_im, m_ssm_c_re, m_ssm_c_im, m_ssm_d, m_ssm_w_glu, m_xa_w_q, m_xa_w_kv, m_xa_w_o, m_ffn_w_up, m_ffn_conv_w, m_ffn_conv_b, m_ffn_w_down, v_norm_mix, v_norm_xattn, v_norm_ffn, v_norm_mem, v_norm_final, v_ab_w_in, v_pool_w, v_pool_scale, v_ab_w_out, v_ssm_w_in, v_ssm_lam_re, v_ssm_lam_im, v_ssm_log_dt, v_ssm_b_re, v_ssm_b_im, v_ssm_c_re, v_ssm_c_im, v_ssm_d, v_ssm_w_glu, v_xa_w_q, v_xa_w_kv, v_xa_w_o, v_ffn_w_up, v_ffn_conv_w, v_ffn_conv_b, v_ffn_w_down):
    args = dict(locals())
    w = {k: args[k] for k in WEIGHTS}
    m = {k: args["m_" + k] for k in WEIGHTS}
    v = {k: args["v_" + k] for k in WEIGHTS}
    return _step(x, mem, loss_target, w, m, v)
```

```python
import functools
import math

import jax
import jax.numpy as jnp
from jax import lax
from jax.experimental import pallas as pl
from jax.experimental.pallas import tpu as pltpu

f32 = jnp.float32
bf16 = jnp.bfloat16
SDS = jax.ShapeDtypeStruct
MESH = pl.DeviceIdType.MESH
ANY = pl.BlockSpec(memory_space=pl.ANY)

SB_HEAD_DIM = 64
POOL_WINDOWS = (2, 4, 8, 16)
POOL_GROUP = 128
XA_HEADS = 4
SSM_GROUPS = 64
SSM_GROUP = 16
SSM_STATE = 64
EPS = 1e-6
ADAM_LR, ADAM_B1, ADAM_B2, ADAM_EPS, ADAM_WD, ADAM_STEP = 0.001, 0.9, 0.999, 1e-08, 0.01, 10

LANES = 128
SUBLANES = 8
N_CHIPS = 4
VMEM_LIMIT = 56 * 1024 * 1024

NN = ((1,), (0,))
NT = ((1,), (1,))
TN = ((0,), (0,))


def _dot(a, b, dims):
    return lax.dot_general(a, b, (dims, ((), ())), preferred_element_type=f32)


def _params(n_grid):
    return pltpu.CompilerParams(dimension_semantics=("arbitrary",) * n_grid, vmem_limit_bytes=VMEM_LIMIT)


def _sum8(x):
    r, n = x.shape
    return jnp.sum(x.reshape(r // SUBLANES, SUBLANES, n), axis=0)


def _split_bf16(x):
    hi = x.astype(bf16)
    lo = (x - hi.astype(f32)).astype(bf16)
    return hi, lo


def _sigmoid(x):
    return 1.0 / (1.0 + jnp.exp(-x))


MM_BM = (1024, 1408, 512, 256, 128)
MM_BN = (1536, 1408, 1024, 512, 256, 128)
MM_BK = (2816, 2048, 1024, 512)


def _divisor(n, cands):
    return next((c for c in cands if n % c == 0), n)


def _mm(a, b, *, mode, name, out_dtype, bm=None, bn=None, bk=None, a_l=None, b_l=None, b_n0=0, n=None,
        res=None, out_l=None, out_layers=None, out_prev=None):
    dims = {"nn": NN, "nt": NT, "tn": TN}[mode]
    a2, b2 = a.shape[-2:], b.shape[-2:]
    if mode == "nn":
        (m, k), nfull = a2, b2[1]
    elif mode == "nt":
        (m, k), nfull = a2, b2[0]
    else:
        (k, m), nfull = a2, b2[1]
    n = nfull if n is None else n
    bm = _divisor(m, MM_BM) if bm is None else min(bm, m)
    bn = _divisor(n, MM_BN) if bn is None else min(bn, n)
    if bk is None:
        bk = _divisor(k, (1024, 512)) if mode == "tn" else (k if k <= MM_BK[0] else _divisor(k, MM_BK))
    bk = min(bk, k)
    assert m % bm == 0 and n % bn == 0 and k % bk == 0 and b_n0 % bn == 0, (name, m, n, k, bm, bn, bk)
    nk, n0b = k // bk, b_n0 // bn
    a_bytes, b_bytes = m * k * a.dtype.itemsize, k * n * b.dtype.itemsize
    rows_outer = a_bytes + b_bytes * (m // bm) <= b_bytes + a_bytes * (n // bn)

    def with_layer(layer, blk, idx_fn):
        def idx(g0, g1, kk):
            i, j = (g0, g1) if rows_outer else (g1, g0)
            return idx_fn(i, j, kk) if layer is None else (layer,) + idx_fn(i, j, kk)
        return pl.BlockSpec(blk if layer is None else (None,) + blk, idx)

    if mode == "tn":
        a_spec = with_layer(a_l, (bk, bm), lambda i, j, kk: (kk, i))
    else:
        a_spec = with_layer(a_l, (bm, bk), lambda i, j, kk: (i, kk))
    if mode == "nt":
        b_spec = with_layer(b_l, (bn, bk), lambda i, j, kk: (j, kk))
    else:
        b_spec = with_layer(b_l, (bk, bn), lambda i, j, kk: (kk, j + n0b))
    o_spec = with_layer(out_l, (bm, bn), lambda i, j, kk: (i, j))
    ins, in_specs = [a, b], [a_spec, b_spec]
    if res is not None:
        ins.append(res)
        in_specs.append(with_layer(None, (bm, bn), lambda i, j, kk: (i, j)))
    aliases = {}
    if out_prev is not None:
        aliases = {len(ins): 0}
        ins.append(out_prev)
        in_specs.append(ANY)
    has_res, has_prev = res is not None, out_prev is not None

    def body(*refs):
        a_ref, b_ref = refs[0], refs[1]
        res_ref = refs[2] if has_res else None
        o_ref = refs[2 + has_res + has_prev]
        part = _dot(a_ref[...].astype(bf16), b_ref[...].astype(bf16), dims)

        def finish(r):
            if has_res:
                r = r + res_ref[...]
            o_ref[...] = r.astype(o_ref.dtype)

        if nk == 1:
            finish(part)
        else:
            acc_ref = refs[-1]
            kk = pl.program_id(2)

            @pl.when(kk == 0)
            def _():
                acc_ref[...] = part

            @pl.when(kk > 0)
            def _():
                acc_ref[...] += part

            @pl.when(kk == nk - 1)
            def _():
                finish(acc_ref[...])

    out_shape = SDS((m, n) if out_l is None else (out_layers, m, n), out_dtype)
    grid = (m // bm, n // bn, nk) if rows_outer else (n // bn, m // bm, nk)
    return pl.pallas_call(
        body, out_shape=out_shape, grid=grid, in_specs=in_specs, out_specs=o_spec,
        scratch_shapes=[] if nk == 1 else [pltpu.VMEM((bm, bn), f32)],
        input_output_aliases=aliases, name=name, compiler_params=_params(3))(*ins)


def _rowwise(fn, row_ins, full_ins, row_outs, acc_outs, *, name, br=256):
    t = row_ins[0].shape[0]
    br = next(b for b in (br, 128, 64, 32, 16, 8, t) if b <= t and t % b == 0)
    nr, nf, no = len(row_ins), len(full_ins), len(row_outs)

    def body(*refs):
        rv = [r[...] for r in refs[:nr]]
        fv = [r[...] for r in refs[nr:nr + nf]]
        o_refs = refs[nr + nf:nr + nf + no]
        a_refs = refs[nr + nf + no:]
        outs, accs = fn(rv, fv)
        for o_ref, v in zip(o_refs, outs):
            o_ref[...] = v.astype(o_ref.dtype)
        if a_refs:
            i = pl.program_id(0)

            @pl.when(i == 0)
            def _():
                for a_ref, v in zip(a_refs, accs):
                    a_ref[...] = v

            @pl.when(i > 0)
            def _():
                for a_ref, v in zip(a_refs, accs):
                    a_ref[...] += v

    in_specs = [pl.BlockSpec((br, x.shape[1]), lambda i: (i, 0)) for x in row_ins]
    in_specs += [pl.BlockSpec(x.shape, lambda i, nd=x.ndim: (0,) * nd) for x in full_ins]
    out_specs = [pl.BlockSpec((br, s.shape[1]), lambda i: (i, 0)) for s in row_outs]
    out_specs += [pl.BlockSpec(s.shape, lambda i: (0, 0)) for s in acc_outs]
    res = pl.pallas_call(body, out_shape=tuple(row_outs) + tuple(acc_outs), grid=(t // br,), in_specs=in_specs,
                         out_specs=tuple(out_specs), name=name, compiler_params=_params(1))(*row_ins, *full_ins)
    return res


def _norm_fwd(x, g, name):
    def fn(rv, fv):
        (xv,), (gv,) = rv, fv
        r = lax.rsqrt(jnp.mean(xv * xv, axis=1, keepdims=True) + EPS)
        return [xv * r * gv], []
    return _rowwise(fn, [x], [g], [SDS(x.shape, bf16)], [], name=name)[0]


def _norm_bwd(dh, x, dres, g, name):
    d = x.shape[1]

    def fn(rv, fv):
        (dhv, xv, drv), (gv,) = rv, fv
        r = lax.rsqrt(jnp.mean(xv * xv, axis=1, keepdims=True) + EPS)
        xh = xv * r
        dxh = dhv * gv
        dx = drv + r * (dxh - xh * jnp.mean(dxh * xh, axis=1, keepdims=True))
        return [dx], [_sum8(dhv * xh)]
    return _rowwise(fn, [dh, x, dres], [g], [SDS(x.shape, f32)], [SDS((SUBLANES, d), f32)], name=name)


def _norm_bwd_gain_only(dh, x, name):
    d = x.shape[1]

    def fn(rv, fv):
        dhv, xv = rv
        r = lax.rsqrt(jnp.mean(xv * xv, axis=1, keepdims=True) + EPS)
        return [], [_sum8(dhv * xv * r)]
    return _rowwise(fn, [dh, x], [], [], [SDS((SUBLANES, d), f32)], name=name)[0]


def _loss_head(x, target, g, name):
    d = x.shape[1]

    def fn(rv, fv):
        (xv, tv), (gv,) = rv, fv
        r = lax.rsqrt(jnp.mean(xv * xv, axis=1, keepdims=True) + EPS)
        xh = xv * r
        err = xh * gv - tv
        dy = err * (1.0 / d)
        dxh = dy * gv
        dx = r * (dxh - xh * jnp.mean(dxh * xh, axis=1, keepdims=True))
        return [dx], [_sum8(dy * xh), _sum8(err * err)]
    return _rowwise(fn, [x, target], [g], [SDS(x.shape, f32)], [SDS((SUBLANES, d), f32), SDS((SUBLANES, d), f32)], name=name)


_GELU_C = math.sqrt(2.0 / math.pi)


def _gelu_fwd(y, name):
    def fn(rv, fv):
        (v,) = rv
        t = jnp.tanh(_GELU_C * (v + 0.044715 * v * v * v))
        return [0.5 * v * (1.0 + t)], []
    return _rowwise(fn, [y], [], [SDS(y.shape, bf16)], [], name=name)[0]


def _gelu_bwd(dg, y, name):
    def fn(rv, fv):
        dgv, v = rv
        t = jnp.tanh(_GELU_C * (v + 0.044715 * v * v * v))
        dt = (1.0 - t * t) * _GELU_C * (1.0 + 3.0 * 0.044715 * v * v)
        return [dgv * (0.5 * (1.0 + t) + 0.5 * v * dt)], []
    return _rowwise(fn, [dg, y], [], [SDS(y.shape, f32)], [], name=name)[0]


def _glu_fwd(glu, x, name):
    d = x.shape[1]

    def fn(rv, fv):
        gl, xv = rv
        return [xv + gl[:, :d] * _sigmoid(gl[:, d:])], []
    return _rowwise(fn, [glu, x], [], [SDS(x.shape, f32)], [], name=name)[0]


def _glu_bwd(dx, glu, name):
    d = dx.shape[1]

    def fn(rv, fv):
        dxv, gl = rv
        sg = _sigmoid(gl[:, d:])
        return [jnp.concatenate([dxv * sg, dxv * gl[:, :d] * sg * (1.0 - sg)], axis=1)], []
    return _rowwise(fn, [dx, glu], [], [SDS(glu.shape, bf16)], [], name=name)[0]


def _adamw(w, g, m, v, name):
    c1 = 1.0 - ADAM_B1 ** ADAM_STEP
    c2 = 1.0 - ADAM_B2 ** ADAM_STEP

    def fn(rv, fv):
        wv, gv, mv, vv = rv
        m2 = ADAM_B1 * mv + (1.0 - ADAM_B1) * gv
        v2 = ADAM_B2 * vv + (1.0 - ADAM_B2) * (gv * gv)
        delta = -ADAM_LR * ((m2 / c1) / (jnp.sqrt(v2 / c2) + ADAM_EPS) + ADAM_WD * wv)
        return [delta, m2, v2], []
    s = SDS(w.shape, f32)
    return _rowwise(fn, [w, g, m, v], [], [s, s, s], [], name=name)


SB_TQ = 128
SB_KB = 4
SB_DEAD = -110.0


def _sb_logits(qh, kb, valid):
    z = _dot(qh, kb, NT) * (SB_HEAD_DIM ** -0.5)
    sp = jnp.log(1.0 + jnp.exp(-jnp.abs(z)))
    lb = jnp.minimum(z, 0.0) - sp
    lk_raw = jnp.minimum(-z, 0.0) - sp
    return lb, lk_raw, jnp.where(valid, lk_raw, 0.0)


def _sb_heads(q, t):
    lane = lax.broadcasted_iota(jnp.int32, (t, LANES), 1)
    masks = [(lane >= hh * SB_HEAD_DIM) & (lane < (hh + 1) * SB_HEAD_DIM) for hh in range(LANES // SB_HEAD_DIM)]
    return [(m, q * jnp.where(m, 1.0, 0.0).astype(bf16)) for m in masks]


def _sb_key_minus_query(t):
    return lax.broadcasted_iota(jnp.int32, (t, t), 1) - lax.broadcasted_iota(jnp.int32, (t, t), 0)


def _tri(t, op):
    row = lax.broadcasted_iota(jnp.int32, (t, t), 0)
    col = lax.broadcasted_iota(jnp.int32, (t, t), 1)
    return jnp.where(op(row, col), 1.0, 0.0).astype(bf16)


def _dot_split(x, u):
    hi, lo = _split_bf16(x)
    return _dot(hi, u, NN) + _dot(lo, u, NN)


def _sb_fwd(qkv, seq, name, comm=None):
    t_all, w3 = qkv.shape
    w = w3 // 3
    hp, tq = w // LANES, SB_TQ
    nb, nq = t_all // seq, seq // tq
    kbn = min(SB_KB, nq)
    assert nq % kbn == 0

    def body(q_ref, k_ref, v_ref, o_ref, lt_ref, first_ref):
        i = pl.program_id(2)
        heads = _sb_heads(q_ref[...], tq)
        kmq = _sb_key_minus_query(tq)
        u_after = _tri(tq, lambda r, c: r > c)
        n_it = (i + kbn) // kbn

        def alive(state):
            return (state[0] < n_it) & (state[1] > SB_DEAD)

        def step(state):
            it, carry = state[0], list(state[2:])
            blocks = []
            for kk in reversed(range(kbn)):
                j = (n_it - 1 - it) * kbn + kk
                off = pl.multiple_of(j * tq, tq)
                blocks.append((k_ref[pl.ds(off, tq), :], v_ref[pl.ds(off, tq), :], kmq < (i - j) * tq))
            chains = [(hh, qh, kb, vb, valid) for kb, vb, valid in blocks for hh, (_, qh) in enumerate(heads)]
            zs = [_dot(qh, kb, NT) for _, qh, kb, _, _ in chains]
            lbs, his, los, sums = [], [], [], []
            for z, (_, _, _, _, valid) in zip(zs, chains):
                z = z * (SB_HEAD_DIM ** -0.5)
                sp = jnp.log(1.0 + jnp.exp(-jnp.abs(z)))
                lb = jnp.minimum(z, 0.0) - sp
                lk = jnp.where(valid, lb - z, 0.0)
                hi, lo = _split_bf16(lk)
                lbs.append(lb), his.append(hi), los.append(lo), sums.append(jnp.sum(lk, axis=1, keepdims=True))
            afts = [_dot(hi, u_after, NN) + _dot(lo, u_after, NN) for hi, lo in zip(his, los)]
            wgts = []
            for (hh, _, _, _, valid), lb, aft, sm in zip(chains, lbs, afts, sums):
                wgts.append(jnp.where(valid, jnp.exp(lb + (carry[2 * hh] + aft)), 0.0).astype(bf16))
                carry[2 * hh] = carry[2 * hh] + sm
            for (hh, _, _, vb, _), wgt in zip(chains, wgts):
                carry[2 * hh + 1] = carry[2 * hh + 1] + _dot(wgt, vb, NN)
            top = jnp.max(carry[0])
            for hh in range(1, len(heads)):
                top = jnp.maximum(top, jnp.max(carry[2 * hh]))
            return (it + 1, top, *carry)

        init = (jnp.int32(0), jnp.float32(0.0)) + (jnp.zeros((tq, 1), f32), jnp.zeros((tq, LANES), f32)) * len(heads)
        fin = lax.while_loop(alive, step, init)
        out = jnp.zeros((tq, LANES), f32)
        ltot = jnp.zeros((tq, LANES), f32)
        for hh, (m, _) in enumerate(heads):
            out = out + jnp.where(m, fin[2 * hh + 3], 0.0)
            ltot = ltot + jnp.where(m, fin[2 * hh + 2], 0.0)
        o_ref[...] = out
        lt_ref[...] = ltot
        first_ref[...] = jnp.zeros((SUBLANES, LANES), f32) + (n_it - fin[0]).astype(f32)

    row_blk = pl.BlockSpec((tq, LANES), lambda b, p, i: (b * nq + i, p))
    (mix, ltot, first), extra = _call(
        body, ins=[qkv, qkv, qkv], out_shape=[SDS((t_all, 2 * w), f32), SDS((t_all, w), f32), SDS((nb * nq * SUBLANES, w), f32)],
        grid=(nb, hp, nq),
        in_specs=[row_blk, pl.BlockSpec((seq, LANES), lambda b, p, i: (b, hp + p)),
                  pl.BlockSpec((seq, LANES), lambda b, p, i: (b, 2 * hp + p))],
        out_specs=[row_blk, row_blk, pl.BlockSpec((SUBLANES, LANES), lambda b, p, i: (b * nq + i, p))],
        scratch_shapes=[], name=name, comm=comm)
    return mix, ltot, first, extra


def _sb_bwd(qkv, ltot, first, dmix, seq, name, comm=None):
    t_all, w3 = qkv.shape
    w = w3 // 3
    hp, tq = w // LANES, SB_TQ
    nb, nq = t_all // seq, seq // tq
    kbn = min(SB_KB, nq)
    assert nq % kbn == 0

    def body(q_ref, k_ref, v_ref, lt_ref, first_ref, do_ref, dq_ref, dk_ref, dv_ref, dk_acc, dv_acc):
        i = pl.program_id(2)

        @pl.when(i == 0)
        def _():
            dk_acc[...] = jnp.zeros_like(dk_acc)
            dv_acc[...] = jnp.zeros_like(dv_acc)

        heads = _sb_heads(q_ref[...], tq)
        do = do_ref[...]
        ltv = lt_ref[...]
        dos = [jnp.where(m, do, 0.0).astype(bf16) for m, _ in heads]
        lts = [jnp.sum(jnp.where(m, ltv, 0.0), axis=1, keepdims=True) * (1.0 / SB_HEAD_DIM) for m, _ in heads]
        kmq = _sb_key_minus_query(tq)
        u_incl = _tri(tq, lambda r, c: r <= c)
        u_excl = _tri(tq, lambda r, c: r < c)
        n_it = (i + kbn) // kbn

        def step(it, carry):
            carry = list(carry)
            blocks = []
            for kk in range(kbn):
                off = pl.multiple_of((it * kbn + kk) * tq, tq)
                blocks.append((off, k_ref[pl.ds(off, tq), :], v_ref[pl.ds(off, tq), :], kmq < (i - (it * kbn + kk)) * tq))
            chains = [(hh, qh, kb, vb, valid) for _, kb, vb, valid in blocks for hh, (_, qh) in enumerate(heads)]
            zs = [_dot(qh, kb, NT) for _, qh, kb, _, _ in chains]
            dws = [_dot(dos[hh], vb, NT) for hh, _, _, vb, _ in chains]
            lbs, lkrs, his, los, sums = [], [], [], [], []
            for z, (_, _, _, _, valid) in zip(zs, chains):
                z = z * (SB_HEAD_DIM ** -0.5)
                sp = jnp.log(1.0 + jnp.exp(-jnp.abs(z)))
                lb = jnp.minimum(z, 0.0) - sp
                lk_raw = lb - z
                lk = jnp.where(valid, lk_raw, 0.0)
                hi, lo = _split_bf16(lk)
                lbs.append(lb), lkrs.append(lk_raw), his.append(hi), los.append(lo)
                sums.append(jnp.sum(lk, axis=1, keepdims=True))
            pins = [_dot(hi, u_incl, NN) + _dot(lo, u_incl, NN) for hi, lo in zip(his, los)]
            wbs, gs, ghis, glos, gpres = [], [], [], [], []
            for (hh, _, _, _, valid), lb, pin, sm, dw in zip(chains, lbs, pins, sums, dws):
                wgt = jnp.where(valid, jnp.exp(lb + (lts[hh] - (carry[3 * hh] + pin))), 0.0)
                carry[3 * hh] = carry[3 * hh] + sm
                g = dw * wgt
                hi, lo = _split_bf16(g)
                wbs.append(wgt.astype(bf16)), gs.append(g), ghis.append(hi), glos.append(lo)
                gpres.append(carry[3 * hh + 1])
                carry[3 * hh + 1] = carry[3 * hh + 1] + jnp.sum(g, axis=1, keepdims=True)
            gins = [_dot(hi, u_excl, NN) + _dot(lo, u_excl, NN) for hi, lo in zip(ghis, glos)]
            dzbs = []
            for (_, _, _, _, valid), lb, lk_raw, g, gpre, gin in zip(chains, lbs, lkrs, gs, gpres, gins):
                dz = jnp.where(valid, g * jnp.exp(lk_raw) - (gpre + gin) * jnp.exp(lb), 0.0) * (SB_HEAD_DIM ** -0.5)
                dzbs.append(dz.astype(bf16))
            for (hh, _, kb, _, _), dzb in zip(chains, dzbs):
                carry[3 * hh + 2] = carry[3 * hh + 2] + _dot(dzb, kb, NN)
            nh = len(heads)
            for bi, (off, _, _, _) in enumerate(blocks):
                dk_j = jnp.zeros((tq, LANES), f32)
                dv_j = jnp.zeros((tq, LANES), f32)
                for hh, (_, qh) in enumerate(heads):
                    dk_j = dk_j + _dot(dzbs[bi * nh + hh], qh, TN)
                    dv_j = dv_j + _dot(wbs[bi * nh + hh], dos[hh], TN)
                dk_acc[pl.ds(off, tq), :] += dk_j
                dv_acc[pl.ds(off, tq), :] += dv_j
            return tuple(carry)

        zero1 = jnp.zeros((tq, 1), f32)
        it0 = jnp.clip(jnp.max(first_ref[...]).astype(jnp.int32), 0, n_it - 1)
        fin = lax.fori_loop(it0, n_it, step, (zero1, zero1, jnp.zeros((tq, LANES), f32)) * len(heads))
        dq_all = jnp.zeros((tq, LANES), f32)
        for hh, (m, _) in enumerate(heads):
            dq_all = dq_all + jnp.where(m, fin[3 * hh + 2], 0.0)
        dq_ref[...] = dq_all.astype(bf16)

        @pl.when(i == nq - 1)
        def _():
            dk_ref[...] = dk_acc[...].astype(bf16)
            dv_ref[...] = dv_acc[...].astype(bf16)

    row_blk = pl.BlockSpec((tq, LANES), lambda b, p, i: (b * nq + i, p))
    seq_blk = pl.BlockSpec((seq, LANES), lambda b, p, i: (b, p))
    out = SDS((t_all, w), bf16)
    (dq, dk, dv), extra = _call(
        body, ins=[qkv, qkv, qkv, ltot, first, dmix], out_shape=[out, out, out], grid=(nb, hp, nq),
        in_specs=[row_blk,
                  pl.BlockSpec((seq, LANES), lambda b, p, i: (b, hp + p)),
                  pl.BlockSpec((seq, LANES), lambda b, p, i: (b, 2 * hp + p)),
                  row_blk, pl.BlockSpec((SUBLANES, LANES), lambda b, p, i: (b * nq + i, p)), row_blk],
        out_specs=[row_blk, seq_blk, seq_blk],
        scratch_shapes=[pltpu.VMEM((seq, LANES), f32), pltpu.VMEM((seq, LANES), f32)], name=name, comm=comm)
    return dq, dk, dv, extra


POOL_CHUNK = 256
POOL_HALO = 16


def _band(rows, cols, lo, hi):
    r = lax.broadcasted_iota(jnp.int32, (rows, cols), 0)
    c = lax.broadcasted_iota(jnp.int32, (rows, cols), 1)
    d = c - r
    return jnp.where((d >= lo) & (d < hi), 1.0, 0.0).astype(bf16)


def _pool_counts(r0, rows, win):
    t = lax.broadcasted_iota(jnp.int32, (rows, 1), 0) + r0
    return jnp.minimum(t + 1, win).astype(f32)


def _pool_fwd(u, mix, pool_w, scale, seq, name):
    t_all, w = u.shape
    ng, rc = w // POOL_GROUP, min(POOL_CHUNK, seq)

    def body(u_ref, w_ref, s_ref, mix_in, p_ref, o_ref, pad):
        del mix_in
        pad[0:POOL_HALO, :] = jnp.zeros((POOL_HALO, POOL_GROUP), f32)
        for g in range(ng):
            cols = slice(g * POOL_GROUP, (g + 1) * POOL_GROUP)
            win = POOL_WINDOWS[g]
            pad[POOL_HALO:POOL_HALO + seq, :] = u_ref[:, cols]
            band = _band(rc, rc + POOL_HALO, POOL_HALO - win + 1, POOL_HALO + 1)
            wg = w_ref[g].astype(bf16)
            for r0 in range(0, seq, rc):
                ue = pad[r0:r0 + rc + POOL_HALO, :]
                hi, lo = _split_bf16(ue)
                sm = _dot(band, hi, NN) + _dot(band, lo, NN)
                pch = sm / _pool_counts(r0, rc, win) - ue[POOL_HALO:, :]
                pb = pch.astype(bf16)
                p_ref[r0:r0 + rc, cols] = pb
                o_ref[r0:r0 + rc, cols] = _dot(pb, wg, NN) * s_ref[:, cols]

    return pl.pallas_call(
        body, out_shape=(SDS((t_all, w), bf16), SDS(mix.shape, f32)), grid=(t_all // seq,),
        in_specs=[pl.BlockSpec((seq, w), lambda b: (b, 0)), pl.BlockSpec(pool_w.shape, lambda b: (0, 0, 0)),
                  pl.BlockSpec(scale.shape, lambda b: (0, 0)), ANY],
        out_specs=(pl.BlockSpec((seq, w), lambda b: (b, 0)), pl.BlockSpec((seq, w), lambda b: (b, 1))),
        scratch_shapes=[pltpu.VMEM((seq + POOL_HALO, POOL_GROUP), f32)],
        input_output_aliases={3: 1}, name=name, compiler_params=_params(1))(u, pool_w, scale, mix)


def _pool_bwd(dmix, p, pool_w, scale, seq, name):
    t_all, w = p.shape
    ng, rc = w // POOL_GROUP, min(POOL_CHUNK, seq)

    def body(dy_ref, p_ref, w_ref, s_ref, du_ref, dw_ref, ds_ref, dpn, dpr):
        b = pl.program_id(0)

        @pl.when(b == 0)
        def _():
            dw_ref[...] = jnp.zeros_like(dw_ref)
            ds_ref[...] = jnp.zeros_like(ds_ref)

        dpn[seq:seq + POOL_HALO, :] = jnp.zeros((POOL_HALO, POOL_GROUP), f32)
        for g in range(ng):
            cols = slice(g * POOL_GROUP, (g + 1) * POOL_GROUP)
            win = POOL_WINDOWS[g]
            wg = w_ref[g].astype(bf16)
            sg = s_ref[:, cols]
            dwg = jnp.zeros((POOL_GROUP, POOL_GROUP), f32)
            dsg = jnp.zeros((SUBLANES, POOL_GROUP), f32)
            for r0 in range(0, seq, rc):
                dy = dy_ref[r0:r0 + rc, cols]
                pb = p_ref[r0:r0 + rc, cols]
                dsg = dsg + _sum8(dy * _dot(pb, wg, NN))
                dyw = (dy * sg).astype(bf16)
                dwg = dwg + _dot(pb, dyw, TN)
                dp = _dot(dyw, wg, NT)
                dpr[r0:r0 + rc, :] = dp
                dpn[r0:r0 + rc, :] = dp / _pool_counts(r0, rc, win)
            dw_ref[g] += dwg
            ds_ref[:, cols] += dsg
            band = _band(rc, rc + POOL_HALO, 0, win)
            for r0 in range(0, seq, rc):
                hi, lo = _split_bf16(dpn[r0:r0 + rc + POOL_HALO, :])
                du = _dot(band, hi, NN) + _dot(band, lo, NN) - dpr[r0:r0 + rc, :]
                du_ref[r0:r0 + rc, cols] = du.astype(bf16)

    return pl.pallas_call(
        body, out_shape=(SDS((t_all, w), bf16), SDS(pool_w.shape, f32), SDS((SUBLANES, w), f32)), grid=(t_all // seq,),
        in_specs=[pl.BlockSpec((seq, w), lambda b: (b, 1)), pl.BlockSpec((seq, w), lambda b: (b, 0)),
                  pl.BlockSpec(pool_w.shape, lambda b: (0, 0, 0)), pl.BlockSpec(scale.shape, lambda b: (0, 0))],
        out_specs=(pl.BlockSpec((seq, w), lambda b: (b, 0)), pl.BlockSpec(pool_w.shape, lambda b: (0, 0, 0)),
                   pl.BlockSpec((SUBLANES, w), lambda b: (0, 0))),
        scratch_shapes=[pltpu.VMEM((seq + POOL_HALO, POOL_GROUP), f32), pltpu.VMEM((seq, POOL_GROUP), f32)],
        name=name, compiler_params=_params(1))(dmix, p, pool_w, scale)


XA_TQ = 256


def _xa_probs(qh, kh, dh):
    s = _dot(qh, kh, NT) * (dh ** -0.5)
    e = jnp.exp(s - jnp.max(s, axis=1, keepdims=True))
    return e / jnp.sum(e, axis=1, keepdims=True)


def _xa_fwd(q, kv, seq, name):
    t_all, d = q.shape
    nb = t_all // seq
    mem, dh, tq = kv.shape[0] // nb, d // XA_HEADS, min(XA_TQ, seq)
    nq = seq // tq

    def body(q_ref, kv_ref, o_ref):
        for h in range(XA_HEADS):
            cols = slice(h * dh, (h + 1) * dh)
            p = _xa_probs(q_ref[:, cols], kv_ref[:, cols], dh)
            o_ref[:, cols] = _dot(p.astype(bf16), kv_ref[:, d + h * dh:d + (h + 1) * dh], NN).astype(bf16)

    return pl.pallas_call(
        body, out_shape=SDS((t_all, d), bf16), grid=(nb, nq),
        in_specs=[pl.BlockSpec((tq, d), lambda b, i: (b * nq + i, 0)), pl.BlockSpec((mem, 2 * d), lambda b, i: (b, 0))],
        out_specs=pl.BlockSpec((tq, d), lambda b, i: (b * nq + i, 0)), name=name, compiler_params=_params(2))(q, kv)


def _xa_bwd(q, kv, do, seq, name):
    t_all, d = q.shape
    nb = t_all // seq
    mem, dh, tq = kv.shape[0] // nb, d // XA_HEADS, min(XA_TQ, seq)
    nq = seq // tq

    def body(q_ref, kv_ref, do_ref, dq_ref, dkv_ref):
        i = pl.program_id(1)

        @pl.when(i == 0)
        def _():
            dkv_ref[...] = jnp.zeros_like(dkv_ref)

        for h in range(XA_HEADS):
            cols = slice(h * dh, (h + 1) * dh)
            vcols = slice(d + h * dh, d + (h + 1) * dh)
            qh, kh, doh = q_ref[:, cols], kv_ref[:, cols], do_ref[:, cols]
            p = _xa_probs(qh, kh, dh)
            dkv_ref[:, vcols] += _dot(p.astype(bf16), doh, TN)
            dp = _dot(doh, kv_ref[:, vcols], NT)
            ds = (p * (dp - jnp.sum(dp * p, axis=1, keepdims=True)) * (dh ** -0.5)).astype(bf16)
            dq_ref[:, cols] = _dot(ds, kh, NN).astype(bf16)
            dkv_ref[:, cols] += _dot(ds, qh, TN)

    row = pl.BlockSpec((tq, d), lambda b, i: (b * nq + i, 0))
    kvs = pl.BlockSpec((mem, 2 * d), lambda b, i: (b, 0))
    return pl.pallas_call(body, out_shape=(SDS((t_all, d), bf16), SDS(kv.shape, f32)), grid=(nb, nq),
                          in_specs=[row, kvs, row], out_specs=(row, kvs), name=name, compiler_params=_params(2))(q, kv, do)


FFN_BR = 256
FFN_CHUNK = 256


def _conv3(ext, w_ref, b, cols, lo, rows):
    return (b + w_ref[2:3, cols] * ext[lo:lo + rows, :] + w_ref[1:2, cols] * ext[lo - 1:lo - 1 + rows, :]
            + w_ref[0:1, cols] * ext[lo - 2:lo - 2 + rows, :])


FFN_HALO = 16


def _ffn_gate_fwd(up, cw, cb, seq, name):
    t_all, f2 = up.shape
    ff, br, ch, hl = f2 // 2, min(FFN_BR, seq), FFN_CHUNK, FFN_HALO
    per_seq, hb = seq // br, br // hl

    def body(up_ref, halo_ref, cw_ref, cb_ref, o_ref, ev, eg):
        i = pl.program_id(0)
        keep = jnp.where(i % per_seq == 0, 0.0, 1.0)
        for c0 in range(0, ff, ch):
            convs = []
            for ext, off in ((ev, c0), (eg, ff + c0)):
                cols = slice(off, off + ch)
                ext[0:hl, :] = halo_ref[:, cols].astype(f32) * keep
                ext[hl:hl + br, :] = up_ref[:, cols].astype(f32)
                convs.append(_conv3(ext, cw_ref, cb_ref[:, cols], cols, hl, br))
            val, gate = convs
            o_ref[:, c0:c0 + ch] = (gate * _sigmoid(gate) * val).astype(bf16)

    return pl.pallas_call(
        body, out_shape=SDS((t_all, ff), bf16), grid=(t_all // br,),
        in_specs=[pl.BlockSpec((br, f2), lambda i: (i, 0)),
                  pl.BlockSpec((hl, f2), lambda i: (jnp.maximum(i * hb - 1, 0), 0)),
                  pl.BlockSpec(cw.shape, lambda i: (0, 0)), pl.BlockSpec(cb.shape, lambda i: (0, 0))],
        out_specs=pl.BlockSpec((br, ff), lambda i: (i, 0)),
        scratch_shapes=[pltpu.VMEM((br + hl, ch), f32), pltpu.VMEM((br + hl, ch), f32)],
        name=name, compiler_params=_params(1))(up, up, cw, cb)


def _ffn_gate_bwd(dact, up, cw, cb, seq, name):
    t_all, f2 = up.shape
    ff, br, ch, hl = f2 // 2, min(FFN_BR, seq), FFN_CHUNK, FFN_HALO
    per_seq, hb, last = seq // br, br // hl, t_all // hl - 1
    ext_rows = br + SUBLANES

    def body(da_ref, dan_ref, up_ref, upp_ref, upn_ref, cw_ref, cb_ref, du_ref, dcw_ref, dcb_ref, uv, ug, dav, dcv, dcg):
        i = pl.program_id(0)

        @pl.when(i == 0)
        def _():
            dcw_ref[...] = jnp.zeros_like(dcw_ref)
            dcb_ref[...] = jnp.zeros_like(dcb_ref)

        keep_prev = jnp.where(i % per_seq == 0, 0.0, 1.0)
        keep_next = jnp.where((i + 1) % per_seq == 0, 0.0, 1.0)
        for c0 in range(0, ff, ch):
            convs = []
            for ext, off in ((uv, c0), (ug, ff + c0)):
                cols = slice(off, off + ch)
                ext[0:hl, :] = upp_ref[:, cols].astype(f32) * keep_prev
                ext[hl:hl + br, :] = up_ref[:, cols].astype(f32)
                ext[hl + br:2 * hl + br, :] = upn_ref[:, cols].astype(f32) * keep_next
                convs.append(_conv3(ext, cw_ref, cb_ref[:, cols], cols, hl, ext_rows))
            val, gate = convs
            dav[0:br, :] = da_ref[:, c0:c0 + ch].astype(f32)
            dav[br:br + hl, :] = dan_ref[:, c0:c0 + ch].astype(f32) * keep_next
            da = dav[0:ext_rows, :]
            sg = _sigmoid(gate)
            dcv[...] = da * gate * sg
            dcg[...] = da * val * sg * (1.0 + gate * (1.0 - sg))
            for ext, dc, off in ((uv, dcv, c0), (ug, dcg, ff + c0)):
                cols = slice(off, off + ch)
                du = (cw_ref[2:3, cols] * dc[0:br, :] + cw_ref[1:2, cols] * dc[1:br + 1, :]
                      + cw_ref[0:1, cols] * dc[2:br + 2, :])
                du_ref[:, cols] = du.astype(bf16)
                d0 = dc[0:br, :]
                dcb_ref[:, cols] += _sum8(d0)
                for tap in range(3):
                    lo = hl - (2 - tap)
                    dcw_ref[tap, :, cols] += _sum8(d0 * ext[lo:lo + br, :])

    blk = lambda n: pl.BlockSpec((br, n), lambda i: (i, 0))
    prev = lambda n: pl.BlockSpec((hl, n), lambda i: (jnp.maximum(i * hb - 1, 0), 0))
    nxt = lambda n: pl.BlockSpec((hl, n), lambda i: (jnp.minimum((i + 1) * hb, last), 0))
    return pl.pallas_call(
        body, out_shape=(SDS((t_all, f2), bf16), SDS((3, SUBLANES, f2), f32), SDS((SUBLANES, f2), f32)), grid=(t_all // br,),
        in_specs=[blk(ff), nxt(ff), blk(f2), prev(f2), nxt(f2), pl.BlockSpec(cw.shape, lambda i: (0, 0)),
                  pl.BlockSpec(cb.shape, lambda i: (0, 0))],
        out_specs=(blk(f2), pl.BlockSpec((3, SUBLANES, f2), lambda i: (0, 0, 0)), pl.BlockSpec((SUBLANES, f2), lambda i: (0, 0))),
        scratch_shapes=[pltpu.VMEM((br + 2 * hl, ch), f32), pltpu.VMEM((br + 2 * hl, ch), f32),
                        pltpu.VMEM((br + hl, ch), f32), pltpu.VMEM((ext_rows, ch), f32), pltpu.VMEM((ext_rows, ch), f32)],
        name=name, compiler_params=_params(1))(dact, dact, up, up, up, cw, cb)


SSM_GB = 8
SSM_PLANES = 8
SSM_ROWS = 256
SSM_UNROLL = 8


def _ssm_pitch(seq):
    p = seq + SUBLANES
    assert (p // SUBLANES) % 2 == 1
    return p


def _rows(base, rc):
    return pl.ds(pl.multiple_of(base + rc * SSM_ROWS, SUBLANES), SSM_ROWS)


def _ssm_project_in(u_ref, b_ref, planes, e, seq, pitch):
    def chunk(rc, _):
        uc = u_ref[_rows(e * seq, rc), :].astype(bf16)
        for j in range(SSM_PLANES):
            planes[_rows(j * pitch, rc), :] = _dot(uc, b_ref[:, j * LANES:(j + 1) * LANES], NN)
        return 0
    lax.fori_loop(0, seq // SSM_ROWS, chunk, 0)


def _ssm_rows(planes, rc, pitch):
    return jnp.concatenate([planes[_rows(j * pitch, rc), :].astype(bf16) for j in range(SSM_PLANES)], axis=1)


def _ssm_scan(planes_list, l1, l2, seq, pitch, reverse=False):
    def step(s, hs):
        t = seq - 1 - s if reverse else s
        out = []
        for planes, h in zip(planes_list, hs):
            h = h * l1 + pltpu.roll(h, 4, 0) * l2 + planes[pl.ds(t, SUBLANES, stride=pitch), :]
            planes[pl.ds(t, SUBLANES, stride=pitch), :] = h
            out.append(h)
        return tuple(out)
    zero = jnp.zeros((SUBLANES, LANES), f32)
    lax.fori_loop(0, seq, step, tuple(zero for _ in planes_list), unroll=SSM_UNROLL)


def _ssm_fwd(u, b_big, c_big, lslab, dskip, seq, name, comm=None):
    t_all, w = u.shape
    nb, gw, pitch = t_all // seq, SSM_GB * SSM_GROUP, _ssm_pitch(seq)
    assert gw == LANES

    def body(u_ref, b_ref, c_ref, l_ref, d_ref, y_ref, *planes):
        l1, l2 = l_ref[0:SUBLANES, :], l_ref[SUBLANES:2 * SUBLANES, :]
        for e in range(nb):
            _ssm_project_in(u_ref, b_ref, planes[e], e, seq, pitch)
        _ssm_scan(planes, l1, l2, seq, pitch)
        for e in range(nb):
            def chunk(rc, _, e=e):
                rows = _rows(e * seq, rc)
                y_ref[rows, :] = _dot(_ssm_rows(planes[e], rc, pitch), c_ref[...], NN) + d_ref[...] * u_ref[rows, :]
                return 0
            lax.fori_loop(0, seq // SSM_ROWS, chunk, 0)

    (y,), extra = _call(
        body, ins=[u, b_big, c_big, lslab, dskip], out_shape=[SDS((t_all, w), f32)], grid=(w // gw,),
        in_specs=[pl.BlockSpec((t_all, gw), lambda k: (0, k)), pl.BlockSpec((None,) + b_big.shape[1:], lambda k: (k, 0, 0)),
                  pl.BlockSpec((None,) + c_big.shape[1:], lambda k: (k, 0, 0)),
                  pl.BlockSpec((None,) + lslab.shape[1:], lambda k: (k, 0, 0)), pl.BlockSpec((1, gw), lambda k: (0, k))],
        out_specs=[pl.BlockSpec((t_all, gw), lambda k: (0, k))],
        scratch_shapes=[pltpu.VMEM((SSM_PLANES * pitch, LANES), f32) for _ in range(nb)], name=name, comm=comm)
    return y, extra


def _ssm_bwd(u, dy, b_big, c_big, lslab, dskip, seq, name, comm=None):
    t_all, w = u.shape
    nb, gw, pitch = t_all // seq, SSM_GB * SSM_GROUP, _ssm_pitch(seq)
    ns = SSM_PLANES * LANES

    def body(u_ref, dy_ref, b_ref, c_ref, l_ref, d_ref, du_ref, db_ref, dc_ref, dl_ref, dd_ref, *planes):
        hp, ap = planes[:nb], planes[nb:]
        l1, l2 = l_ref[0:SUBLANES, :], l_ref[SUBLANES:2 * SUBLANES, :]
        for e in range(nb):
            _ssm_project_in(u_ref, b_ref, hp[e], e, seq, pitch)
        _ssm_scan(hp, l1, l2, seq, pitch)
        dd_ref[...] = jnp.zeros_like(dd_ref)
        dc_ref[...] = jnp.zeros_like(dc_ref)
        db_ref[...] = jnp.zeros_like(db_ref)
        for e in range(nb):
            def chunk(rc, _, e=e):
                rows = _rows(e * seq, rc)
                dyc = dy_ref[rows, :]
                dyb = dyc.astype(bf16)
                for j in range(SSM_PLANES):
                    ap[e][_rows(j * pitch, rc), :] = _dot(dyb, c_ref[j * LANES:(j + 1) * LANES, :], NT)
                dd_ref[...] += _sum8(dyc * u_ref[rows, :])
                dc_ref[...] += _dot(_ssm_rows(hp[e], rc, pitch), dyb, TN)
                return 0
            lax.fori_loop(0, seq // SSM_ROWS, chunk, 0)

        def step(s, carry):
            t = seq - 1 - s
            out = []
            for e in range(nb):
                a, s1, s2 = carry[e]
                a = a * l1 - pltpu.roll(a, 4, 0) * l2 + ap[e][pl.ds(t, SUBLANES, stride=pitch), :]
                ap[e][pl.ds(t, SUBLANES, stride=pitch), :] = a
                hprev = hp[e][pl.ds(jnp.maximum(t - 1, 0), SUBLANES, stride=pitch), :] * jnp.where(t > 0, 1.0, 0.0)
                out.append((a, s1 + a * hprev, s2 + a * pltpu.roll(hprev, 4, 0)))
            return tuple(out)
        zero = jnp.zeros((SUBLANES, LANES), f32)
        fin = lax.fori_loop(0, seq, step, tuple((zero, zero, zero) for _ in range(nb)), unroll=SSM_UNROLL)
        dl_ref[0:SUBLANES, :] = sum(f[1] for f in fin)
        dl_ref[SUBLANES:2 * SUBLANES, :] = sum(f[2] for f in fin)

        for e in range(nb):
            def chunk2(rc, _, e=e):
                rows = _rows(e * seq, rc)
                ar = _ssm_rows(ap[e], rc, pitch)
                du_ref[rows, :] = (_dot(ar, b_ref[...], NT) + d_ref[...] * dy_ref[rows, :]).astype(bf16)
                db_ref[...] += _dot(u_ref[rows, :].astype(bf16), ar, TN)
                return 0
            lax.fori_loop(0, seq // SSM_ROWS, chunk2, 0)

    col = pl.BlockSpec((t_all, gw), lambda k: (0, k))
    per = lambda s: pl.BlockSpec((None,) + s[1:], lambda k: (k, 0, 0))
    ng = w // gw
    res, extra = _call(
        body, ins=[u, dy, b_big, c_big, lslab, dskip],
        out_shape=[SDS((t_all, w), bf16), SDS(b_big.shape, f32), SDS(c_big.shape, f32), SDS((ng, 2 * SUBLANES, LANES), f32),
                   SDS((SUBLANES, w), f32)],
        grid=(ng,),
        in_specs=[col, col, per(b_big.shape), per(c_big.shape), per(lslab.shape), pl.BlockSpec((1, gw), lambda k: (0, k))],
        out_specs=[col, per(b_big.shape), per(c_big.shape), per((ng, 2 * SUBLANES, LANES)), pl.BlockSpec((SUBLANES, gw), lambda k: (0, k))],
        scratch_shapes=[pltpu.VMEM((SSM_PLANES * pitch, LANES), f32) for _ in range(2 * nb)], name=name, comm=comm)
    return (*res, extra)


def _ssm_disc_fwd(lam_re, lam_im, dt, b_re, b_im, name):
    def body(a_ref, b_ref, dt_ref, br_ref, bi_ref, lr_ref, li_ref, cr_ref, ci_ref, bbr_ref, bbi_ref):
        a, b, dtv = a_ref[...], b_ref[...], dt_ref[...]
        mag, ang = jnp.exp(a * dtv), b * dtv
        lr, li = mag * jnp.cos(ang), mag * jnp.sin(ang)
        nr, den = lr - 1.0, a * a + b * b
        cr, ci = (nr * a + li * b) / den, (li * a - nr * b) / den
        lr_ref[...], li_ref[...], cr_ref[...], ci_ref[...] = lr, li, cr, ci
        bbr_ref[...] = cr * br_ref[...] - ci * bi_ref[...]
        bbi_ref[...] = cr * bi_ref[...] + ci * br_ref[...]
    c, m = SDS(lam_re.shape, f32), SDS(b_re.shape, f32)
    return pl.pallas_call(body, out_shape=(c, c, c, c, m, m), name=name)(lam_re, lam_im, dt, b_re, b_im)


def _ssm_disc_bwd(lam_re, lam_im, dt, b_re, b_im, g_lr, g_li, g_bbr, g_bbi, name):
    def body(a_ref, b_ref, dt_ref, br_ref, bi_ref, glr_ref, gli_ref, gbr_ref, gbi_ref, da_ref, db_ref, ddt_ref, dbr_ref, dbi_ref):
        a, b, dtv = a_ref[...], b_ref[...], dt_ref[...]
        mag, ang = jnp.exp(a * dtv), b * dtv
        cs, sn = jnp.cos(ang), jnp.sin(ang)
        lr, li = mag * cs, mag * sn
        nr, den = lr - 1.0, a * a + b * b
        cr, ci = (nr * a + li * b) / den, (li * a - nr * b) / den
        gbr, gbi, brv, biv = gbr_ref[...], gbi_ref[...], br_ref[...], bi_ref[...]
        dbr_ref[...] = cr * gbr + ci * gbi
        dbi_ref[...] = cr * gbi - ci * gbr
        dcr = jnp.sum(brv * gbr + biv * gbi, axis=1, keepdims=True)
        dci = jnp.sum(brv * gbi - biv * gbr, axis=1, keepdims=True)
        dnum_r, dnum_i = dcr / den, dci / den
        dden = -(dcr * cr + dci * ci) / den
        dnr = dnum_r * a - dnum_i * b
        dli = gli_ref[...] + dnum_r * b + dnum_i * a
        dlr = glr_ref[...] + dnr
        dmag, dang = dlr * cs + dli * sn, dli * lr - dlr * li
        dadt = dmag * mag
        da_ref[...] = dnum_r * nr + dnum_i * li + dden * 2.0 * a + dadt * dtv
        db_ref[...] = dnum_r * li - dnum_i * nr + dden * 2.0 * b + dang * dtv
        ddt_ref[...] = dadt * a + dang * b
    c, m = SDS(lam_re.shape, f32), SDS(b_re.shape, f32)
    return pl.pallas_call(body, out_shape=(c, c, c, m, m), name=name)(lam_re, lam_im, dt, b_re, b_im, g_lr, g_li, g_bbr, g_bbi)


def _place():
    x, y, c = lax.axis_index("x"), lax.axis_index("y"), lax.axis_index("c")
    return x, y, c, 2 * x + y


def _half_axis(shape, ax):
    return 0 if shape[0] == 2 else (3 - ax)


def _sub(ref, axis, start, size):
    idx = [slice(None)] * len(ref.shape)
    idx[axis] = pl.ds(start, size)
    return ref.at[tuple(idx)]


def _region(ref, full_shape, ax, slot=None, half=None):
    if slot is not None:
        n = full_shape[ax] // N_CHIPS
        ref = _sub(ref, ax, slot * n, n)
    if half is not None:
        ha = _half_axis(full_shape, ax)
        n = full_shape[ha] // 2
        ref = _sub(ref, ha, half * n, n)
    return ref


def _halved(shape, axis):
    return tuple(s // 2 if a == axis else s for a, s in enumerate(shape))


class _Comm:
    def __init__(self, ins, out_shapes, aliases, scratch, start, finish):
        self.ins, self.out_shapes, self.aliases, self.scratch, self.start, self.finish = ins, out_shapes, aliases, scratch, start, finish


def _call(body, *, ins, in_specs, out_shape, out_specs, grid, scratch_shapes, name, comm=None):
    if comm is None:
        res = pl.pallas_call(body, out_shape=tuple(out_shape), grid=grid, in_specs=list(in_specs), out_specs=tuple(out_specs),
                             scratch_shapes=list(scratch_shapes), name=name, compiler_params=_params(len(grid)))(*ins)
        return list(res), []
    n_in, n_out, n_scr, c_in, c_out = len(ins), len(out_shape), len(scratch_shapes), len(comm.ins), len(comm.out_shapes)

    def fused(*refs):
        pos = [n_in, n_in + c_in, n_in + c_in + n_out, n_in + c_in + n_out + c_out, n_in + c_in + n_out + c_out + n_scr]
        in_refs, cin, out_refs, cout, scr, cscr = (refs[:pos[0]], refs[pos[0]:pos[1]], refs[pos[1]:pos[2]], refs[pos[2]:pos[3]],
                                                   refs[pos[3]:pos[4]], refs[pos[4]:])
        ids = [pl.program_id(a) for a in range(len(grid))]
        first, last = ids[0] == 0, ids[0] == grid[0] - 1
        for a in range(1, len(grid)):
            first, last = first & (ids[a] == 0), last & (ids[a] == grid[a] - 1)

        @pl.when(first)
        def _():
            comm.start(cin, cout, cscr)

        body(*in_refs, *out_refs, *scr)

        @pl.when(last)
        def _():
            comm.finish(cin, cout, cscr)

    res = pl.pallas_call(
        fused, out_shape=tuple(out_shape) + tuple(comm.out_shapes), grid=grid, in_specs=list(in_specs) + [ANY] * c_in,
        out_specs=tuple(out_specs) + tuple([ANY] * c_out), scratch_shapes=list(scratch_shapes) + list(comm.scratch),
        input_output_aliases={n_in + i: n_out + o for i, o in comm.aliases}, name=name, compiler_params=_params(len(grid)))(*ins, *comm.ins)
    return list(res[:n_out]), list(res[n_out:])


def _comm_only(comm, name):
    c_in, c_out = len(comm.ins), len(comm.out_shapes)

    def body(*refs):
        cin, cout, cscr = refs[:c_in], refs[c_in:c_in + c_out], refs[c_in + c_out:]
        comm.start(cin, cout, cscr)
        comm.finish(cin, cout, cscr)

    return pl.pallas_call(body, out_shape=tuple(comm.out_shapes), in_specs=[ANY] * c_in, out_specs=tuple([ANY] * c_out),
                          scratch_shapes=list(comm.scratch), input_output_aliases=dict(comm.aliases), name=name)(*comm.ins)


def _gather_plan(shards, axes):
    n = len(shards)
    fulls = [tuple(s * N_CHIPS if a == ax else s for a, s in enumerate(sh.shape)) for sh, ax in zip(shards, axes)]
    own = 6

    def copies(src, dst, scr):
        send_sems, recv_sems = scr
        x, y, c, p = _place()
        chips = [(1 - x, y), (x, 1 - y), (1 - x, 1 - y)]
        slots = [2 * cx + cy for cx, cy in chips]

        def copy(a, k, slot, half, to, from_shard=False):
            where = _region(dst[a], fulls[a], axes[a], slot, half)
            source = where
            if from_shard:
                ha = _half_axis(fulls[a], axes[a])
                hn = fulls[a][ha] // 2
                source = _sub(src[a], ha, half * hn, hn)
            return pltpu.make_async_remote_copy(src_ref=source, dst_ref=where, send_sem=send_sems.at[a, k],
                                                recv_sem=recv_sems.at[a, k], device_id=to, device_id_type=MESH)

        parts = range(n)
        mine = [pltpu.make_async_remote_copy(src_ref=src[a], dst_ref=_region(dst[a], fulls[a], axes[a], p),
                                             send_sem=send_sems.at[a, own], recv_sem=recv_sems.at[a, own],
                                             device_id=(x, y, 1 - c), device_id_type=MESH) for a in parts]
        first = [copy(a, j, p, c, (*chips[j], c), True) for a in parts for j in range(3)]
        landed = [copy(a, j, slots[j], c, (x, y, c)) for a in parts for j in range(3)]
        passed = [copy(a, 3 + j, slots[j], c, (x, y, 1 - c)) for a in parts for j in range(3)]
        handed = [copy(a, 3 + j, slots[j], 1 - c, (x, y, c)) for a in parts for j in range(3)]
        return mine, first, landed, passed, handed

    def start(src, dst, scr):
        mine, first, _, _, _ = copies(src, dst, scr)
        for cp in first + mine:
            cp.start()

    def finish(src, dst, scr):
        mine, first, landed, passed, handed = copies(src, dst, scr)
        for arrived, fwd in zip(landed, passed):
            arrived.wait_recv()
            fwd.start()
        for cp in handed + mine:
            cp.wait_recv()
        for cp in first + passed + mine:
            cp.wait_send()

    return _Comm(list(shards), [SDS(f, s.dtype) for f, s in zip(fulls, shards)], [],
                 [pltpu.SemaphoreType.DMA((n, 7)), pltpu.SemaphoreType.DMA((n, 7))], start, finish)


def _all_gather(shards, axes, name):
    return _comm_only(_gather_plan(shards, axes), name)


def _swap_halves(grads, axes, name):
    n = len(grads)
    shapes = [g.shape for g in grads]

    def body(*refs):
        src, dst = refs[:n], refs[n:2 * n]
        send_sems, recv_sems = refs[2 * n:]
        x, y, c, _ = _place()
        cps = [pltpu.make_async_remote_copy(src_ref=_region(src[a], shapes[a], axes[a], None, 1 - c), dst_ref=dst[a],
                                            send_sem=send_sems.at[a], recv_sem=recv_sems.at[a],
                                            device_id=(x, y, 1 - c), device_id_type=MESH) for a in range(n)]
        for cp in cps:
            cp.start()
        for cp in cps:
            cp.wait()

    outs = tuple(SDS(_halved(s, _half_axis(s, ax)), g.dtype) for s, ax, g in zip(shapes, axes, grads))
    return pl.pallas_call(body, out_shape=outs, in_specs=[ANY] * n, out_specs=tuple([ANY] * n),
                          scratch_shapes=[pltpu.SemaphoreType.DMA((n,)), pltpu.SemaphoreType.DMA((n,))], name=name)(*grads)


def _row_block(rows, row_bytes, limit=3 << 20):
    for b in (1024, 512, 256, 128, 64, 32, 16, 8):
        if rows % b == 0 and b * row_bytes <= limit:
            return b
    return rows


def _add_own_half(g, other, ax, cidx, name):
    _, kp, np_ = other.shape
    ha = _half_axis(g.shape, ax)
    ks, ns = (kp // N_CHIPS, np_) if ax == 1 else (kp, np_ // N_CHIPS)
    bk = _row_block(ks, ns * 4)
    nkb = ks // bk

    def g_map(q, i, cref):
        c = cref[0]
        if ax == 1:
            return (c, q * nkb + i, 0) if ha == 0 else (0, q * nkb + i, c)
        return (c, i, q) if ha == 0 else (0, c * nkb + i, q)

    def o_map(q, i, cref):
        return (0, q * nkb + i, 0) if ax == 1 else (0, i, q)

    def body(c_ref, g_ref, o_ref, send_ref, land_ref):
        del c_ref
        s = (g_ref[...].astype(f32) + o_ref[...].astype(f32)).astype(send_ref.dtype)
        send_ref[...] = s
        land_ref[...] = s

    out = pl.BlockSpec((None, bk, ns), lambda q, i, cref: (q, i, 0))
    grid_spec = pltpu.PrefetchScalarGridSpec(
        num_scalar_prefetch=1, grid=(N_CHIPS, nkb),
        in_specs=[pl.BlockSpec((None, bk, ns), g_map), pl.BlockSpec((None, bk, ns), o_map)], out_specs=(out, out))
    shape = SDS((N_CHIPS, ks, ns), g.dtype)
    return pl.pallas_call(body, out_shape=(shape, shape), grid_spec=grid_spec, name=name, compiler_params=_params(2))(cidx, g, other)


def _owner_plan(sends, lands):
    n = len(sends)

    def copies(cin, dst, scr):
        src = cin[:n]
        send_sems, recv_sems = scr
        x, y, c, p = _place()
        chips = [(1 - x, y), (x, 1 - y), (1 - x, 1 - y)]
        slots = [2 * cx + cy for cx, cy in chips]
        out = [pltpu.make_async_remote_copy(src_ref=src[a].at[slots[j]], dst_ref=dst[a].at[p], send_sem=send_sems.at[a, j],
                                            recv_sem=recv_sems.at[a, j], device_id=(*chips[j], c), device_id_type=MESH)
               for a in range(n) for j in range(3)]
        back = [pltpu.make_async_remote_copy(src_ref=src[a].at[p], dst_ref=dst[a].at[slots[j]], send_sem=send_sems.at[a, j],
                                             recv_sem=recv_sems.at[a, j], device_id=(x, y, c), device_id_type=MESH)
                for a in range(n) for j in range(3)]
        return out, back

    def start(cin, dst, scr):
        for cp in copies(cin, dst, scr)[0]:
            cp.start()

    def finish(cin, dst, scr):
        out, back = copies(cin, dst, scr)
        for cp in back:
            cp.wait_recv()
        for cp in out:
            cp.wait_send()

    return _Comm(list(sends) + list(lands), [SDS(l.shape, l.dtype) for l in lands], [(n + a, a) for a in range(n)],
                 [pltpu.SemaphoreType.DMA((n, 3)), pltpu.SemaphoreType.DMA((n, 3))], start, finish)


def _sum_chips(stack, shard_shape, ax, cidx, name):
    _, ks, ns = stack.shape
    ha = _half_axis(shard_shape, ax)
    bk = _row_block(ks, ns * 4 * N_CHIPS)
    nkb = ks // bk

    def o_map(i, cref):
        c = cref[0]
        return (c, i, 0) if ha == 0 else ((0, c * nkb + i, 0) if ha == 1 else (0, i, c))

    def body(c_ref, s_ref, o_ref):
        del c_ref
        acc = s_ref[0].astype(f32)
        for q in range(1, N_CHIPS):
            acc = acc + s_ref[q].astype(f32)
        o_ref[...] = acc

    grid_spec = pltpu.PrefetchScalarGridSpec(
        num_scalar_prefetch=1, grid=(nkb,), in_specs=[pl.BlockSpec((N_CHIPS, bk, ns), lambda i, cref: (0, i, 0))],
        out_specs=pl.BlockSpec((None, bk, ns), o_map))
    return pl.pallas_call(body, out_shape=SDS(shard_shape, f32), grid_spec=grid_spec, name=name, compiler_params=_params(1))(cidx, stack)


def _join_halves(slices, axes, name):
    n = len(slices)

    def body(*refs):
        dst = refs[n:2 * n]
        send_sems, recv_sems = refs[2 * n:]
        x, y, c, _ = _place()

        def half(a, h):
            ha = _half_axis(slices[a].shape, axes[a])
            hn = slices[a].shape[ha] // 2
            return _sub(dst[a], ha, h * hn, hn)

        cps = [pltpu.make_async_remote_copy(src_ref=half(a, c), dst_ref=half(a, c), send_sem=send_sems.at[a], recv_sem=recv_sems.at[a],
                                            device_id=(x, y, 1 - c), device_id_type=MESH) for a in range(n)]
        for cp in cps:
            cp.start()
        for a in range(n):
            pltpu.make_async_remote_copy(src_ref=half(a, c), dst_ref=half(a, 1 - c), send_sem=send_sems.at[a], recv_sem=recv_sems.at[a],
                                         device_id=(x, y, c), device_id_type=MESH).wait_recv()
        for cp in cps:
            cp.wait_send()

    return pl.pallas_call(
        body, out_shape=tuple(SDS(s.shape, s.dtype) for s in slices), in_specs=[ANY] * n, out_specs=tuple([ANY] * n),
        scratch_shapes=[pltpu.SemaphoreType.DMA((n,)), pltpu.SemaphoreType.DMA((n,))],
        input_output_aliases={a: a for a in range(n)}, name=name)(*slices)


def _core_index():
    return jnp.reshape(lax.axis_index("c"), (1,)).astype(jnp.int32)


def _reduce_begin(grads, axes, tag):
    cidx = _core_index()
    others = _swap_halves(grads, axes, f"rs_swap_{tag}")
    pairs = [_add_own_half(g, o, ax, cidx, f"rs_add_{tag}_{a}") for a, (g, o, ax) in enumerate(zip(grads, others, axes))]
    return _owner_plan([s for s, _ in pairs], [l for _, l in pairs])


def _reduce_end(stacks, shapes, axes, tag):
    cidx = _core_index()
    shard_shapes = [tuple(s // N_CHIPS if i == ax else s for i, s in enumerate(sh)) for sh, ax in zip(shapes, axes)]
    slices = [_sum_chips(s, sh, ax, cidx, f"rs_sum_{tag}_{a}") for a, (s, sh, ax) in enumerate(zip(stacks, shard_shapes, axes))]
    return _join_halves(slices, axes, f"rs_join_{tag}")


def _reduce_scatter(grads, axes, tag):
    stacks = _comm_only(_reduce_begin(grads, axes, tag), f"rs_owner_{tag}")
    return _reduce_end(stacks, [g.shape for g in grads], axes, tag)


SMALL_COLS = 256


def _pack(arrays, rows_multiple):
    flat = jnp.concatenate([a.reshape(-1).astype(f32) for a in arrays])
    rows = -(-flat.shape[0] // SMALL_COLS)
    rows = -(-rows // rows_multiple) * rows_multiple
    flat = jnp.pad(flat, (0, rows * SMALL_COLS - flat.shape[0]))
    return flat.reshape(1, rows, SMALL_COLS)


def _unpack(buf, shapes):
    flat, out, off = buf.reshape(-1), [], 0
    for s in shapes:
        n = math.prod(s)
        out.append(flat[off:off + n].reshape(s))
        off += n
    return out


def _block_diag_in(bb):
    g, p, c = bb.shape
    k = g // SSM_GB
    eye = jnp.eye(SSM_GB, dtype=bb.dtype)
    return jnp.einsum("kgpc,gh->kgchp", bb.reshape(k, SSM_GB, p, c), eye).reshape(k, SSM_GB * c, SSM_GB * p)


def _block_diag_out(cc):
    g, c, p = cc.shape
    k = g // SSM_GB
    eye = jnp.eye(SSM_GB, dtype=cc.dtype)
    return jnp.einsum("kgcp,gh->kgphc", cc.reshape(k, SSM_GB, c, p), eye).reshape(k, SSM_GB * p, SSM_GB * c)


def _diag_in(db, p, c):
    k = db.shape[0]
    return jnp.einsum("kgcgp->kgpc", db.reshape(k, SSM_GB, c, SSM_GB, p)).reshape(k * SSM_GB, p, c)


def _diag_out(dc, p, c):
    k = dc.shape[0]
    return jnp.einsum("kgpgc->kgcp", dc.reshape(k, SSM_GB, p, SSM_GB, c)).reshape(k * SSM_GB, c, p)


def _state_slab(v):
    g, p = v.shape
    return v.reshape(g // SSM_GB, SSM_GB * p // LANES, LANES)


BIG = ("ab_w_in", "ab_w_out", "ssm_w_in", "ssm_w_glu", "xa_w_q", "xa_w_kv", "xa_w_o", "ffn_w_up", "ffn_w_down")
BIG_AXIS = dict(ab_w_in=2, ab_w_out=1, ssm_w_in=1, ssm_w_glu=2, xa_w_q=1, xa_w_kv=2, xa_w_o=1, ffn_w_up=2, ffn_w_down=1)
SMALL_REPL = ("norm_mix", "norm_xattn", "norm_ffn", "norm_mem", "norm_final", "pool_w", "pool_scale", "ssm_lam_re", "ssm_lam_im",
              "ssm_log_dt", "ssm_b_re", "ssm_b_im", "ssm_c_re", "ssm_c_im", "ffn_conv_b")
SMALL_SHARDED = ("ssm_d", "ffn_conv_w")
FIRST_MIXER = ("ab_w_in", "ab_w_out")
WEIGHTS = ("norm_mix", "norm_xattn", "norm_ffn", "norm_mem", "norm_final", "ab_w_in", "pool_w", "pool_scale", "ab_w_out", "ssm_w_in",
           "ssm_lam_re", "ssm_lam_im", "ssm_log_dt", "ssm_b_re", "ssm_b_im", "ssm_c_re", "ssm_c_im", "ssm_d", "ssm_w_glu", "xa_w_q",
           "xa_w_kv", "xa_w_o", "ffn_w_up", "ffn_conv_w", "ffn_conv_b", "ffn_w_down")


class _Reducer:
    def __init__(self):
        self.done, self.groups = {}, 0

    def begin(self, keys, gw):
        self.groups += 1
        return _reduce_begin([gw[k] for k in keys], [BIG_AXIS.get(k[0], 1) for k in keys], f"g{self.groups}")

    def end(self, keys, gw, stacks):
        slices = _reduce_end(stacks, [gw[k].shape for k in keys], [BIG_AXIS.get(k[0], 1) for k in keys], f"g{self.groups}")
        self.done.update(zip(keys, slices))


def _local_step(xf, memf, tgt, w, wf, conv_w, ssm_d, seq, late_weights=None, reducer=None):
    d = xf.shape[1]
    depth = w["norm_mix"].shape[0]
    wf = dict(wf)
    late_weights = late_weights or {}
    sbw = wf["ab_w_in", 0].shape[2] // 4
    row = lambda a: a.reshape(1, -1)

    gs, ps = w["ssm_lam_re"].shape[1:]
    col = lambda a: a.reshape(gs * ps, 1)
    lam_re, lam_im = col(w["ssm_lam_re"][0]), col(w["ssm_lam_im"][0])
    dt = col(jnp.broadcast_to(jnp.exp(w["ssm_log_dt"][0])[:, None], (gs, ps)))
    b_re, b_im = w["ssm_b_re"][0].reshape(gs * ps, -1), w["ssm_b_im"][0].reshape(gs * ps, -1)
    lb_re, lb_im, _, _, bb_re, bb_im = _ssm_disc_fwd(lam_re, lam_im, dt, b_re, b_im, "ssm_disc")
    cgrp = b_re.shape[1]
    b_big = jnp.concatenate([_block_diag_in(bb_re.reshape(gs, ps, cgrp)), _block_diag_in(bb_im.reshape(gs, ps, cgrp))], axis=2).astype(bf16)
    c_big = jnp.concatenate([_block_diag_out(w["ssm_c_re"][0]), -_block_diag_out(w["ssm_c_im"][0])], axis=1).astype(bf16)
    lr_s, li_s = _state_slab(lb_re.reshape(gs, ps)), _state_slab(lb_im.reshape(gs, ps))
    lslab = jnp.concatenate([lr_s, lr_s, -li_s, li_s], axis=1)

    mem_n = _norm_fwd(memf, row(w["norm_mem"]), "norm_mem")
    kv = [None] * depth
    xs, saved = [xf], []
    cur = xf
    for l in range(depth):
        sv = {}
        h = _norm_fwd(cur, row(w["norm_mix"][l]), f"norm_mix{l}")
        sv["h"] = h
        if l % 2 == 0:
            qkv = _mm(h, wf["ab_w_in", 0], mode="nn", b_l=0, n=3 * sbw, out_dtype=bf16, name=f"qkv{l}")
            u = _mm(h, wf["ab_w_in", 0], mode="nn", b_l=0, b_n0=3 * sbw, n=sbw, out_dtype=f32, name=f"poolin{l}")
            plan, names = late_weights.get(f"sb_fwd{l}", (None, ()))
            mix, ltot, first, late = _sb_fwd(qkv, seq, f"sb_fwd{l}", comm=plan)
            wf.update(zip(names, late))
            pooled, mix = _pool_fwd(u, mix, w["pool_w"][0], w["pool_scale"], seq, f"pool_fwd{l}")
            sv.update(qkv=qkv, mix=mix, ltot=ltot, first=first, pooled=pooled)
            cur = _mm(mix, wf["ab_w_out", 0], mode="nn", b_l=0, res=cur, out_dtype=f32, name=f"mixout{l}")
        else:
            us = _mm(h, wf["ssm_w_in", 0], mode="nn", b_l=0, out_dtype=f32, name=f"ssmin{l}")
            plan, names = late_weights.get(f"ssm_fwd{l}", (None, ()))
            ys, late = _ssm_fwd(us, b_big, c_big, lslab, ssm_d, seq, f"ssm_fwd{l}", comm=plan)
            wf.update(zip(names, late))
            gl = _gelu_fwd(ys, f"gelu{l}")
            glu = _mm(gl, wf["ssm_w_glu", 0], mode="nn", b_l=0, out_dtype=f32, name=f"glu{l}")
            sv.update(us=us, ys=ys, gl=gl, glu=glu)
            cur = _glu_fwd(glu, cur, f"glugate{l}")
        sv["x1"] = cur
        kv[l] = _mm(mem_n, wf["xa_w_kv", l], mode="nn", b_l=0, out_dtype=bf16, name=f"kv{l}")
        hx = _norm_fwd(cur, row(w["norm_xattn"][l]), f"norm_xa{l}")
        qx = _mm(hx, wf["xa_w_q", l], mode="nn", b_l=0, out_dtype=bf16, name=f"xaq{l}")
        ox = _xa_fwd(qx, kv[l], seq, f"xa_fwd{l}")
        cur = _mm(ox, wf["xa_w_o", l], mode="nn", b_l=0, res=cur, out_dtype=f32, name=f"xao{l}")
        sv.update(hx=hx, qx=qx, ox=ox, x2=cur)
        hf = _norm_fwd(cur, row(w["norm_ffn"][l]), f"norm_ffn{l}")
        up = _mm(hf, wf["ffn_w_up", l], mode="nn", b_l=0, out_dtype=bf16, name=f"ffnup{l}")
        act = _ffn_gate_fwd(up, conv_w[l], row(w["ffn_conv_b"][l]), seq, f"ffn_gate{l}")
        cur = _mm(act, wf["ffn_w_down", l], mode="nn", b_l=0, res=cur, out_dtype=f32, name=f"ffndown{l}")
        sv.update(hf=hf, up=up, act=act)
        saved.append(sv)
        xs.append(cur)

    dx, g_final8, loss8 = _loss_head(cur, tgt, row(w["norm_final"]), "loss_head")

    gw = {}
    small = {"norm_final": jnp.sum(g_final8, axis=0)}
    g_mix, g_xa, g_ffn, g_cw, g_cb = [None] * depth, [None] * depth, [None] * depth, [None] * depth, [None] * depth
    dmem_n = None

    pending = []

    def wgrad(key, a, b, l, **kw):
        kw.setdefault("bk", 1024)
        gw[key, l] = _mm(a, b, mode="tn", out_dtype=bf16, out_l=0, out_layers=1, name=f"dw_{key}{l}", **kw)
        pending.append((key, l))

    def reduce_beside():
        if reducer is None or not pending:
            return None, []
        keys = list(pending)
        pending.clear()
        return reducer.begin(keys, gw), keys

    for l in reversed(range(depth)):
        sv = saved[l]
        dact = _mm(dx, wf["ffn_w_down", l], mode="nt", b_l=0, out_dtype=bf16, name=f"d_act{l}")
        wgrad("ffn_w_down", sv["act"], dx, l)
        dup, dcw8, dcb8 = _ffn_gate_bwd(dact, sv["up"], conv_w[l], row(w["ffn_conv_b"][l]), seq, f"ffn_gate_bwd{l}")
        g_cw[l], g_cb[l] = jnp.sum(dcw8, axis=1), jnp.sum(dcb8, axis=0)
        wgrad("ffn_w_up", sv["hf"], dup, l)
        dhf = _mm(dup, wf["ffn_w_up", l], mode="nt", b_l=0, out_dtype=f32, name=f"d_hf{l}")
        dx, g8 = _norm_bwd(dhf, sv["x2"], dx, row(w["norm_ffn"][l]), f"norm_ffn_bwd{l}")
        g_ffn[l] = jnp.sum(g8, axis=0)
        dox = _mm(dx, wf["xa_w_o", l], mode="nt", b_l=0, out_dtype=bf16, name=f"d_ox{l}")
        wgrad("xa_w_o", sv["ox"], dx, l)
        dqx, dkv = _xa_bwd(sv["qx"], kv[l], dox, seq, f"xa_bwd{l}")
        wgrad("xa_w_kv", mem_n, dkv, l, bk=mem_n.shape[0])
        dmem_n = _mm(dkv, wf["xa_w_kv", l], mode="nt", b_l=0, res=dmem_n, out_dtype=f32, name=f"d_memn{l}")
        wgrad("xa_w_q", sv["hx"], dqx, l)
        dhx = _mm(dqx, wf["xa_w_q", l], mode="nt", b_l=0, out_dtype=f32, name=f"d_hx{l}")
        dx, g8 = _norm_bwd(dhx, sv["x1"], dx, row(w["norm_xattn"][l]), f"norm_xa_bwd{l}")
        g_xa[l] = jnp.sum(g8, axis=0)
        if l % 2 == 0:
            dmix = _mm(dx, wf["ab_w_out", 0], mode="nt", b_l=0, out_dtype=f32, name=f"d_mix{l}")
            comm, keys = reduce_beside()
            dq, dk, dv, stacks = _sb_bwd(sv["qkv"], sv["ltot"], sv["first"], dmix, seq, f"sb_bwd{l}", comm=comm)
            if comm is not None:
                reducer.end(keys, gw, stacks)
            wgrad("ab_w_out", sv["mix"], dx, 0)
            du, dpw, dps8 = _pool_bwd(dmix, sv["pooled"], w["pool_w"][0], w["pool_scale"], seq, f"pool_bwd{l}")
            small["pool_w"], small["pool_scale"] = dpw[None], jnp.sum(dps8, axis=0)[None]
            dproj = jnp.concatenate([dq, dk, dv, du], axis=1)
            wgrad("ab_w_in", sv["h"], dproj, 0)
            dh = _mm(dproj, wf["ab_w_in", 0], mode="nt", b_l=0, out_dtype=f32, name=f"d_h{l}")
        else:
            dglu = _glu_bwd(dx, sv["glu"], f"glugate_bwd{l}")
            dgl = _mm(dglu, wf["ssm_w_glu", 0], mode="nt", b_l=0, out_dtype=f32, name=f"d_gelu{l}")
            dys = _gelu_bwd(dgl, sv["ys"], f"gelu_bwd{l}")
            comm, keys = reduce_beside()
            dus, db_big, dc_big, dl, dd8, stacks = _ssm_bwd(sv["us"], dys, b_big, c_big, lslab, ssm_d, seq, f"ssm_bwd{l}", comm=comm)
            if comm is not None:
                reducer.end(keys, gw, stacks)
            wgrad("ssm_w_glu", sv["gl"], dglu, 0)
            small["ssm_d"] = jnp.sum(dd8, axis=0)[None]
            half = SSM_PLANES // 2
            g_lr = (dl[:, 0:half] + dl[:, half:SUBLANES]).reshape(gs * ps, 1)
            g_li = (dl[:, SUBLANES + half:] - dl[:, SUBLANES:SUBLANES + half]).reshape(gs * ps, 1)
            g_bbr = _diag_in(db_big[:, :, :SSM_GB * ps], ps, cgrp).reshape(gs * ps, cgrp)
            g_bbi = _diag_in(db_big[:, :, SSM_GB * ps:], ps, cgrp).reshape(gs * ps, cgrp)
            d_a, d_b, d_dt, d_br, d_bi = _ssm_disc_bwd(lam_re, lam_im, dt, b_re, b_im, g_lr, g_li, g_bbr, g_bbi, "ssm_disc_bwd")
            small["ssm_lam_re"], small["ssm_lam_im"] = d_a.reshape(1, gs, ps), d_b.reshape(1, gs, ps)
            small["ssm_log_dt"] = (jnp.sum(d_dt.reshape(gs, ps), axis=1) * dt.reshape(gs, ps)[:, 0])[None]
            small["ssm_b_re"], small["ssm_b_im"] = d_br.reshape(1, gs, ps, cgrp), d_bi.reshape(1, gs, ps, cgrp)
            small["ssm_c_re"] = _diag_out(dc_big[:, :SSM_GB * ps], ps, cgrp)[None]
            small["ssm_c_im"] = -_diag_out(dc_big[:, SSM_GB * ps:], ps, cgrp)[None]
            wgrad("ssm_w_in", sv["h"], dus, 0)
            dh = _mm(dus, wf["ssm_w_in", 0], mode="nt", b_l=0, out_dtype=f32, name=f"d_h{l}")
        dx, g8 = _norm_bwd(dh, xs[l], dx, row(w["norm_mix"][l]), f"norm_mix_bwd{l}")
        g_mix[l] = jnp.sum(g8, axis=0)

    small["norm_mem"] = jnp.sum(_norm_bwd_gain_only(dmem_n, memf, "norm_mem_bwd"), axis=0)
    small["norm_mix"], small["norm_xattn"], small["norm_ffn"] = jnp.stack(g_mix), jnp.stack(g_xa), jnp.stack(g_ffn)
    small["ffn_conv_w"], small["ffn_conv_b"] = jnp.stack(g_cw), jnp.stack(g_cb)
    return loss8, dx, gw, small, pending


def _step(x, mem, loss_target, w, m, v):
    nb, seq, d = x.shape
    t_all = nb * seq
    depth = w["norm_mix"].shape[0]
    chip = 2 * lax.axis_index("x") + lax.axis_index("y")

    small_mine = _pack([w[k] for k in SMALL_SHARDED], SUBLANES)
    gathered = _all_gather([w[k].astype(bf16) for k in FIRST_MIXER] + [small_mine], [BIG_AXIS[k] for k in FIRST_MIXER] + [1],
                           "gather_first")
    wf = {(k, 0): g for k, g in zip(FIRST_MIXER, gathered[:-1])}
    per_chip = gathered[-1].reshape(N_CHIPS, -1)
    pieces = [_unpack(per_chip[q], [w[k].shape for k in SMALL_SHARDED]) for q in range(N_CHIPS)]
    ssm_d = jnp.concatenate([pc[0] for pc in pieces], axis=-1)
    conv_w = jnp.concatenate([pc[1] for pc in pieces], axis=-1)
    ff2 = conv_w.shape[-1]
    late = [(k, l) for k in BIG if k not in FIRST_MIXER for l in range(w[k].shape[0])]
    groups = {"sb_fwd0": [kl for kl in late if kl[1] == 0], "ssm_fwd1": [kl for kl in late if kl[1] > 0]}
    late_weights = {hook: (_gather_plan([w[k][l:l + 1].astype(bf16) for k, l in keys], [BIG_AXIS[k] for k, _ in keys]), keys)
                    for hook, keys in groups.items()}

    reducer = _Reducer()
    loss8, dx, gw, small, pending = _local_step(x.reshape(t_all, d), mem.reshape(-1, d), loss_target.reshape(t_all, d), w, wf,
                                                conv_w, ssm_d, seq, late_weights=late_weights, reducer=reducer)
    loss = lax.psum(0.5 * jnp.sum(loss8) / d, ("x", "y", "c"))

    small_names = SMALL_REPL + SMALL_SHARDED
    small_full_shapes = [w[k].shape for k in SMALL_REPL] + [(1, d), (depth, 3, ff2)]
    gw["small", 0] = _pack([small[k] for k in small_names], 2 * N_CHIPS * SUBLANES)
    keys = pending + [("small", 0)]
    reducer.end(keys, gw, _comm_only(reducer.begin(keys, gw), "rs_owner_last"))
    g_big = {k: jnp.concatenate([reducer.done[k, l] for l in range(w[k].shape[0])], axis=0) for k in BIG}
    small_all = _all_gather([reducer.done["small", 0]], [1], "gather_small_grads")[0]
    g_small = dict(zip(small_names, _unpack(small_all, small_full_shapes)))
    g_small["ssm_d"] = lax.dynamic_slice_in_dim(g_small["ssm_d"], chip * (d // N_CHIPS), d // N_CHIPS, axis=1)
    g_small["ffn_conv_w"] = lax.dynamic_slice_in_dim(g_small["ffn_conv_w"], chip * (ff2 // N_CHIPS), ff2 // N_CHIPS, axis=2)
    grads = {**g_big, **g_small}

    delta, new_m, new_v = {}, {}, {}
    for k in BIG:
        n_cols = w[k].shape[-1]
        two = lambda a: a.reshape(-1, n_cols)
        dl_, m_, v_ = _adamw(two(w[k]), two(grads[k]), two(m[k]), two(v[k]), f"adamw_{k}")
        delta[k], new_m[k], new_v[k] = dl_.reshape(w[k].shape), m_.reshape(w[k].shape), v_.reshape(w[k].shape)
    pk = lambda tree: _pack([tree[k] for k in small_names], 256)[0]
    small_shapes = [w[k].shape for k in small_names]
    outs = _adamw(pk(w), pk(grads), pk(m), pk(v), "adamw_small")
    for tree, buf in zip((delta, new_m, new_v), outs):
        tree.update(zip(small_names, _unpack(buf, small_shapes)))

    grad_x = dx.reshape(nb, seq, d)
    return (loss, grad_x, *[grads[k] for k in WEIGHTS], *[delta[k] for k in WEIGHTS], *[new_m[k] for k in WEIGHTS],
            *[new_v[k] for k in WEIGHTS])


def kernel(x, mem, norm_mix, norm_xattn, norm_ffn, norm_mem, norm_final, ab_w_in, pool_w, pool_scale, ab_w_out, ssm_w_in, ssm_lam_re, ssm_lam_im, ssm_log_dt, ssm_b_re, ssm_b_im, ssm_c_re, ssm_c_im, ssm_d, ssm_w_glu, xa_w_q, xa_w_kv, xa_w_o, ffn_w_up, ffn_conv_w, ffn_conv_b, ffn_w_down, loss_target, m_norm_mix, m_norm_xattn, m_norm_ffn, m_norm_mem, m_norm_final, m_ab_w_in, m_pool_w, m_pool_scale, m_ab_w_out, m_ssm_w_in, m_ssm_lam_re, m_ssm_lam_im, m_ssm_log_dt, m_ssm_b_re, m_ssm_b_im, m_ssm_c_re, m_ssm_c_im, m_ssm_d, m_ssm_w_glu, m_xa_w_q, m_xa_w_kv, m_xa_w_o, m_ffn_w_up, m_ffn_conv_w, m_ffn_conv_b, m_ffn_w_down, v_norm_mix, v_norm_xattn, v_norm_ffn, v_norm_mem, v_norm_final, v_ab_w_in, v_pool_w, v_pool_scale, v_ab_w_out, v_ssm_w_in, v_ssm_lam_re, v_ssm_lam_im, v_ssm_log_dt, v_ssm_b_re, v_ssm_b_im, v_ssm_c_re, v_ssm_c_im, v_ssm_d, v_ssm_w_glu, v_xa_w_q, v_xa_w_kv, v_xa_w_o, v_ffn_w_up, v_ffn_conv_w, v_ffn_conv_b, v_ffn_w_down):
    args = dict(locals())
    w = {k: args[k] for k in WEIGHTS}
    m = {k: args["m_" + k] for k in WEIGHTS}
    v = {k: args["v_" + k] for k in WEIGHTS}
    return _step(x, mem, loss_target, w, m, v)
```

```python
import functools
import math

import jax
import jax.numpy as jnp
from jax import lax
from jax.experimental import pallas as pl
from jax.experimental.pallas import tpu as pltpu

f32 = jnp.float32
bf16 = jnp.bfloat16
SDS = jax.ShapeDtypeStruct
MESH = pl.DeviceIdType.MESH
ANY = pl.BlockSpec(memory_space=pl.ANY)

SB_HEAD_DIM = 64
POOL_WINDOWS = (2, 4, 8, 16)
POOL_GROUP = 128
XA_HEADS = 4
SSM_GROUPS = 64
SSM_GROUP = 16
SSM_STATE = 64
EPS = 1e-6
ADAM_LR, ADAM_B1, ADAM_B2, ADAM_EPS, ADAM_WD, ADAM_STEP = 0.001, 0.9, 0.999, 1e-08, 0.01, 10

LANES = 128
SUBLANES = 8
N_CHIPS = 4
VMEM_LIMIT = 56 * 1024 * 1024

NN = ((1,), (0,))
NT = ((1,), (1,))
TN = ((0,), (0,))


def _dot(a, b, dims):
    return lax.dot_general(a, b, (dims, ((), ())), preferred_element_type=f32)


def _params(n_grid):
    return pltpu.CompilerParams(dimension_semantics=("arbitrary",) * n_grid, vmem_limit_bytes=VMEM_LIMIT)


def _sum8(x):
    r, n = x.shape
    return jnp.sum(x.reshape(r // SUBLANES, SUBLANES, n), axis=0)


def _split_bf16(x):
    hi = x.astype(bf16)
    lo = (x - hi.astype(f32)).astype(bf16)
    return hi, lo


def _sigmoid(x):
    return 1.0 / (1.0 + jnp.exp(-x))


MM_BM = (1024, 1408, 512, 256, 128)
MM_BN = (1536, 1408, 1024, 512, 256, 128)
MM_BK = (2816, 2048, 1024, 512)


def _divisor(n, cands):
    return next((c for c in cands if n % c == 0), n)


def _mm(a, b, *, mode, name, out_dtype, bm=None, bn=None, bk=None, a_l=None, b_l=None, b_n0=0, n=None,
        res=None, out_l=None, out_layers=None, out_prev=None):
    dims = {"nn": NN, "nt": NT, "tn": TN}[mode]
    a2, b2 = a.shape[-2:], b.shape[-2:]
    if mode == "nn":
        (m, k), nfull = a2, b2[1]
    elif mode == "nt":
        (m, k), nfull = a2, b2[0]
    else:
        (k, m), nfull = a2, b2[1]
    n = nfull if n is None else n
    bm = _divisor(m, MM_BM) if bm is None else min(bm, m)
    bn = _divisor(n, MM_BN) if bn is None else min(bn, n)
    if bk is None:
        bk = _divisor(k, (1024, 512)) if mode == "tn" else (k if k <= MM_BK[0] else _divisor(k, MM_BK))
    bk = min(bk, k)
    assert m % bm == 0 and n % bn == 0 and k % bk == 0 and b_n0 % bn == 0, (name, m, n, k, bm, bn, bk)
    nk, n0b = k // bk, b_n0 // bn
    a_bytes, b_bytes = m * k * a.dtype.itemsize, k * n * b.dtype.itemsize
    rows_outer = a_bytes + b_bytes * (m // bm) <= b_bytes + a_bytes * (n // bn)

    def with_layer(layer, blk, idx_fn):
        def idx(g0, g1, kk):
            i, j = (g0, g1) if rows_outer else (g1, g0)
            return idx_fn(i, j, kk) if layer is None else (layer,) + idx_fn(i, j, kk)
        return pl.BlockSpec(blk if layer is None else (None,) + blk, idx)

    if mode == "tn":
        a_spec = with_layer(a_l, (bk, bm), lambda i, j, kk: (kk, i))
    else:
        a_spec = with_layer(a_l, (bm, bk), lambda i, j, kk: (i, kk))
    if mode == "nt":
        b_spec = with_layer(b_l, (bn, bk), lambda i, j, kk: (j, kk))
    else:
        b_spec = with_layer(b_l, (bk, bn), lambda i, j, kk: (kk, j + n0b))
    o_spec = with_layer(out_l, (bm, bn), lambda i, j, kk: (i, j))
    ins, in_specs = [a, b], [a_spec, b_spec]
    if res is not None:
        ins.append(res)
        in_specs.append(with_layer(None, (bm, bn), lambda i, j, kk: (i, j)))
    aliases = {}
    if out_prev is not None:
        aliases = {len(ins): 0}
        ins.append(out_prev)
        in_specs.append(ANY)
    has_res, has_prev = res is not None, out_prev is not None

    def body(*refs):
        a_ref, b_ref = refs[0], refs[1]
        res_ref = refs[2] if has_res else None
        o_ref = refs[2 + has_res + has_prev]
        part = _dot(a_ref[...].astype(bf16), b_ref[...].astype(bf16), dims)

        def finish(r):
            if has_res:
                r = r + res_ref[...]
            o_ref[...] = r.astype(o_ref.dtype)

        if nk == 1:
            finish(part)
        else:
            acc_ref = refs[-1]
            kk = pl.program_id(2)

            @pl.when(kk == 0)
            def _():
                acc_ref[...] = part

            @pl.when(kk > 0)
            def _():
                acc_ref[...] += part

            @pl.when(kk == nk - 1)
            def _():
                finish(acc_ref[...])

    out_shape = SDS((m, n) if out_l is None else (out_layers, m, n), out_dtype)
    grid = (m // bm, n // bn, nk) if rows_outer else (n // bn, m // bm, nk)
    return pl.pallas_call(
        body, out_shape=out_shape, grid=grid, in_specs=in_specs, out_specs=o_spec,
        scratch_shapes=[] if nk == 1 else [pltpu.VMEM((bm, bn), f32)],
        input_output_aliases=aliases, name=name, compiler_params=_params(3))(*ins)


def _rowwise(fn, row_ins, full_ins, row_outs, acc_outs, *, name, br=512):
    t = row_ins[0].shape[0]
    br = next(b for b in (br, 256, 128, 64, 32, 16, 8, t) if b <= t and t % b == 0)
    nr, nf, no = len(row_ins), len(full_ins), len(row_outs)

    def body(*refs):
        rv = [r[...] for r in refs[:nr]]
        fv = [r[...] for r in refs[nr:nr + nf]]
        o_refs = refs[nr + nf:nr + nf + no]
        a_refs = refs[nr + nf + no:]
        outs, accs = fn(rv, fv)
        for o_ref, v in zip(o_refs, outs):
            o_ref[...] = v.astype(o_ref.dtype)
        if a_refs:
            i = pl.program_id(0)

            @pl.when(i == 0)
            def _():
                for a_ref, v in zip(a_refs, accs):
                    a_ref[...] = v

            @pl.when(i > 0)
            def _():
                for a_ref, v in zip(a_refs, accs):
                    a_ref[...] += v

    in_specs = [pl.BlockSpec((br, x.shape[1]), lambda i: (i, 0)) for x in row_ins]
    in_specs += [pl.BlockSpec(x.shape, lambda i, nd=x.ndim: (0,) * nd) for x in full_ins]
    out_specs = [pl.BlockSpec((br, s.shape[1]), lambda i: (i, 0)) for s in row_outs]
    out_specs += [pl.BlockSpec(s.shape, lambda i: (0, 0)) for s in acc_outs]
    res = pl.pallas_call(body, out_shape=tuple(row_outs) + tuple(acc_outs), grid=(t // br,), in_specs=in_specs,
                         out_specs=tuple(out_specs), name=name, compiler_params=_params(1))(*row_ins, *full_ins)
    return res


def _norm_fwd(x, g, name):
    def fn(rv, fv):
        (xv,), (gv,) = rv, fv
        r = lax.rsqrt(jnp.mean(xv * xv, axis=1, keepdims=True) + EPS)
        return [xv * r * gv], []
    return _rowwise(fn, [x], [g], [SDS(x.shape, bf16)], [], name=name)[0]


def _norm_bwd(dh, x, dres, g, name):
    d = x.shape[1]

    def fn(rv, fv):
        (dhv, xv, drv), (gv,) = rv, fv
        r = lax.rsqrt(jnp.mean(xv * xv, axis=1, keepdims=True) + EPS)
        xh = xv * r
        dxh = dhv * gv
        dx = drv + r * (dxh - xh * jnp.mean(dxh * xh, axis=1, keepdims=True))
        return [dx], [_sum8(dhv * xh)]
    return _rowwise(fn, [dh, x, dres], [g], [SDS(x.shape, f32)], [SDS((SUBLANES, d), f32)], name=name)


def _norm_bwd_gain_only(dh, x, name):
    d = x.shape[1]

    def fn(rv, fv):
        dhv, xv = rv
        r = lax.rsqrt(jnp.mean(xv * xv, axis=1, keepdims=True) + EPS)
        return [], [_sum8(dhv * xv * r)]
    return _rowwise(fn, [dh, x], [], [], [SDS((SUBLANES, d), f32)], name=name)[0]


def _loss_head(x, target, g, name):
    d = x.shape[1]

    def fn(rv, fv):
        (xv, tv), (gv,) = rv, fv
        r = lax.rsqrt(jnp.mean(xv * xv, axis=1, keepdims=True) + EPS)
        xh = xv * r
        err = xh * gv - tv
        dy = err * (1.0 / d)
        dxh = dy * gv
        dx = r * (dxh - xh * jnp.mean(dxh * xh, axis=1, keepdims=True))
        return [dx], [_sum8(dy * xh), _sum8(err * err)]
    return _rowwise(fn, [x, target], [g], [SDS(x.shape, f32)], [SDS((SUBLANES, d), f32), SDS((SUBLANES, d), f32)], name=name)


_GELU_C = math.sqrt(2.0 / math.pi)


def _gelu_fwd(y, name):
    def fn(rv, fv):
        (v,) = rv
        t = jnp.tanh(_GELU_C * (v + 0.044715 * v * v * v))
        return [0.5 * v * (1.0 + t)], []
    return _rowwise(fn, [y], [], [SDS(y.shape, bf16)], [], name=name)[0]


def _gelu_bwd(dg, y, name):
    def fn(rv, fv):
        dgv, v = rv
        t = jnp.tanh(_GELU_C * (v + 0.044715 * v * v * v))
        dt = (1.0 - t * t) * _GELU_C * (1.0 + 3.0 * 0.044715 * v * v)
        return [dgv * (0.5 * (1.0 + t) + 0.5 * v * dt)], []
    return _rowwise(fn, [dg, y], [], [SDS(y.shape, f32)], [], name=name)[0]


def _glu_fwd(glu, x, name):
    d = x.shape[1]

    def fn(rv, fv):
        gl, xv = rv
        return [xv + gl[:, :d] * _sigmoid(gl[:, d:])], []
    return _rowwise(fn, [glu, x], [], [SDS(x.shape, f32)], [], name=name)[0]


def _glu_bwd(dx, glu, name):
    d = dx.shape[1]

    def fn(rv, fv):
        dxv, gl = rv
        sg = _sigmoid(gl[:, d:])
        return [jnp.concatenate([dxv * sg, dxv * gl[:, :d] * sg * (1.0 - sg)], axis=1)], []
    return _rowwise(fn, [dx, glu], [], [SDS(glu.shape, bf16)], [], name=name)[0]


def _adamw(w, g, m, v, name):
    c1 = 1.0 - ADAM_B1 ** ADAM_STEP
    c2 = 1.0 - ADAM_B2 ** ADAM_STEP

    def fn(rv, fv):
        wv, gv, mv, vv = rv
        m2 = ADAM_B1 * mv + (1.0 - ADAM_B1) * gv
        v2 = ADAM_B2 * vv + (1.0 - ADAM_B2) * (gv * gv)
        delta = -ADAM_LR * ((m2 / c1) / (jnp.sqrt(v2 / c2) + ADAM_EPS) + ADAM_WD * wv)
        return [delta, m2, v2], []
    s = SDS(w.shape, f32)
    return _rowwise(fn, [w, g, m, v], [], [s, s, s], [], name=name, br=256)


SB_TQ = 128
SB_KB = 4
SB_DEAD = -110.0


def _sb_logits(qh, kb, valid):
    z = _dot(qh, kb, NT) * (SB_HEAD_DIM ** -0.5)
    sp = jnp.log(1.0 + jnp.exp(-jnp.abs(z)))
    lb = jnp.minimum(z, 0.0) - sp
    lk_raw = jnp.minimum(-z, 0.0) - sp
    return lb, lk_raw, jnp.where(valid, lk_raw, 0.0)


def _sb_heads(q, t):
    lane = lax.broadcasted_iota(jnp.int32, (t, LANES), 1)
    masks = [(lane >= hh * SB_HEAD_DIM) & (lane < (hh + 1) * SB_HEAD_DIM) for hh in range(LANES // SB_HEAD_DIM)]
    return [(m, q * jnp.where(m, 1.0, 0.0).astype(bf16)) for m in masks]


def _sb_key_minus_query(t):
    return lax.broadcasted_iota(jnp.int32, (t, t), 1) - lax.broadcasted_iota(jnp.int32, (t, t), 0)


def _tri(t, op):
    row = lax.broadcasted_iota(jnp.int32, (t, t), 0)
    col = lax.broadcasted_iota(jnp.int32, (t, t), 1)
    return jnp.where(op(row, col), 1.0, 0.0).astype(bf16)


def _dot_split(x, u):
    hi, lo = _split_bf16(x)
    return _dot(hi, u, NN) + _dot(lo, u, NN)


def _sb_fwd(qkv, seq, name, comm=None):
    t_all, w3 = qkv.shape
    w = w3 // 3
    hp, tq = w // LANES, SB_TQ
    nb, nq = t_all // seq, seq // tq
    kbn = min(SB_KB, nq)
    assert nq % kbn == 0

    def body(q_ref, k_ref, v_ref, o_ref, lt_ref, first_ref):
        i = pl.program_id(2)
        heads = _sb_heads(q_ref[...], tq)
        kmq = _sb_key_minus_query(tq)
        u_after = _tri(tq, lambda r, c: r > c)
        n_it = (i + kbn) // kbn

        def alive(state):
            return (state[0] < n_it) & (state[1] > SB_DEAD)

        def step(state):
            it, carry = state[0], list(state[2:])
            blocks = []
            for kk in reversed(range(kbn)):
                j = (n_it - 1 - it) * kbn + kk
                off = pl.multiple_of(j * tq, tq)
                blocks.append((k_ref[pl.ds(off, tq), :], v_ref[pl.ds(off, tq), :], kmq < (i - j) * tq))
            chains = [(hh, qh, kb, vb, valid) for kb, vb, valid in blocks for hh, (_, qh) in enumerate(heads)]
            zs = [_dot(qh, kb, NT) for _, qh, kb, _, _ in chains]
            lbs, his, los, sums = [], [], [], []
            for z, (_, _, _, _, valid) in zip(zs, chains):
                z = z * (SB_HEAD_DIM ** -0.5)
                sp = jnp.log(1.0 + jnp.exp(-jnp.abs(z)))
                lb = jnp.minimum(z, 0.0) - sp
                lk = jnp.where(valid, lb - z, 0.0)
                hi, lo = _split_bf16(lk)
                lbs.append(lb), his.append(hi), los.append(lo), sums.append(jnp.sum(lk, axis=1, keepdims=True))
            afts = [_dot(hi, u_after, NN) + _dot(lo, u_after, NN) for hi, lo in zip(his, los)]
            wgts = []
            for (hh, _, _, _, valid), lb, aft, sm in zip(chains, lbs, afts, sums):
                wgts.append(jnp.where(valid, jnp.exp(lb + (carry[2 * hh] + aft)), 0.0).astype(bf16))
                carry[2 * hh] = carry[2 * hh] + sm
            for (hh, _, _, vb, _), wgt in zip(chains, wgts):
                carry[2 * hh + 1] = carry[2 * hh + 1] + _dot(wgt, vb, NN)
            top = jnp.max(carry[0])
            for hh in range(1, len(heads)):
                top = jnp.maximum(top, jnp.max(carry[2 * hh]))
            return (it + 1, top, *carry)

        init = (jnp.int32(0), jnp.float32(0.0)) + (jnp.zeros((tq, 1), f32), jnp.zeros((tq, LANES), f32)) * len(heads)
        fin = lax.while_loop(alive, step, init)
        out = jnp.zeros((tq, LANES), f32)
        ltot = jnp.zeros((tq, LANES), f32)
        for hh, (m, _) in enumerate(heads):
            out = out + jnp.where(m, fin[2 * hh + 3], 0.0)
            ltot = ltot + jnp.where(m, fin[2 * hh + 2], 0.0)
        o_ref[...] = out
        lt_ref[...] = ltot
        first_ref[...] = jnp.zeros((SUBLANES, LANES), f32) + (n_it - fin[0]).astype(f32)

    row_blk = pl.BlockSpec((tq, LANES), lambda b, p, i: (b * nq + i, p))
    (mix, ltot, first), extra = _call(
        body, ins=[qkv, qkv, qkv], out_shape=[SDS((t_all, 2 * w), f32), SDS((t_all, w), f32), SDS((nb * nq * SUBLANES, w), f32)],
        grid=(nb, hp, nq),
        in_specs=[row_blk, pl.BlockSpec((seq, LANES), lambda b, p, i: (b, hp + p)),
                  pl.BlockSpec((seq, LANES), lambda b, p, i: (b, 2 * hp + p))],
        out_specs=[row_blk, row_blk, pl.BlockSpec((SUBLANES, LANES), lambda b, p, i: (b * nq + i, p))],
        scratch_shapes=[], name=name, comm=comm)
    return mix, ltot, first, extra


def _sb_bwd(qkv, ltot, first, dmix, seq, name, comm=None):
    t_all, w3 = qkv.shape
    w = w3 // 3
    hp, tq = w // LANES, SB_TQ
    nb, nq = t_all // seq, seq // tq
    kbn = min(SB_KB, nq)
    assert nq % kbn == 0

    def body(q_ref, k_ref, v_ref, lt_ref, first_ref, do_ref, dq_ref, dk_ref, dv_ref, dk_acc, dv_acc):
        i = pl.program_id(2)

        @pl.when(i == 0)
        def _():
            dk_acc[...] = jnp.zeros_like(dk_acc)
            dv_acc[...] = jnp.zeros_like(dv_acc)

        heads = _sb_heads(q_ref[...], tq)
        do = do_ref[...]
        ltv = lt_ref[...]
        dos = [jnp.where(m, do, 0.0).astype(bf16) for m, _ in heads]
        lts = [jnp.sum(jnp.where(m, ltv, 0.0), axis=1, keepdims=True) * (1.0 / SB_HEAD_DIM) for m, _ in heads]
        kmq = _sb_key_minus_query(tq)
        u_incl = _tri(tq, lambda r, c: r <= c)
        u_excl = _tri(tq, lambda r, c: r < c)
        n_it = (i + kbn) // kbn

        def step(it, carry):
            carry = list(carry)
            blocks = []
            for kk in range(kbn):
                off = pl.multiple_of((it * kbn + kk) * tq, tq)
                blocks.append((off, k_ref[pl.ds(off, tq), :], v_ref[pl.ds(off, tq), :], kmq < (i - (it * kbn + kk)) * tq))
            chains = [(hh, qh, kb, vb, valid) for _, kb, vb, valid in blocks for hh, (_, qh) in enumerate(heads)]
            zs = [_dot(qh, kb, NT) for _, qh, kb, _, _ in chains]
            dws = [_dot(dos[hh], vb, NT) for hh, _, _, vb, _ in chains]
            lbs, lkrs, his, los, sums = [], [], [], [], []
            for z, (_, _, _, _, valid) in zip(zs, chains):
                z = z * (SB_HEAD_DIM ** -0.5)
                sp = jnp.log(1.0 + jnp.exp(-jnp.abs(z)))
                lb = jnp.minimum(z, 0.0) - sp
                lk_raw = lb - z
                lk = jnp.where(valid, lk_raw, 0.0)
                hi, lo = _split_bf16(lk)
                lbs.append(lb), lkrs.append(lk_raw), his.append(hi), los.append(lo)
                sums.append(jnp.sum(lk, axis=1, keepdims=True))
            pins = [_dot(hi, u_incl, NN) + _dot(lo, u_incl, NN) for hi, lo in zip(his, los)]
            wbs, gs, ghis, glos, gpres = [], [], [], [], []
            for (hh, _, _, _, valid), lb, pin, sm, dw in zip(chains, lbs, pins, sums, dws):
                wgt = jnp.where(valid, jnp.exp(lb + (lts[hh] - (carry[3 * hh] + pin))), 0.0)
                carry[3 * hh] = carry[3 * hh] + sm
                g = dw * wgt
                hi, lo = _split_bf16(g)
                wbs.append(wgt.astype(bf16)), gs.append(g), ghis.append(hi), glos.append(lo)
                gpres.append(carry[3 * hh + 1])
                carry[3 * hh + 1] = carry[3 * hh + 1] + jnp.sum(g, axis=1, keepdims=True)
            gins = [_dot(hi, u_excl, NN) + _dot(lo, u_excl, NN) for hi, lo in zip(ghis, glos)]
            dzbs = []
            for (_, _, _, _, valid), lb, lk_raw, g, gpre, gin in zip(chains, lbs, lkrs, gs, gpres, gins):
                dz = jnp.where(valid, g * jnp.exp(lk_raw) - (gpre + gin) * jnp.exp(lb), 0.0) * (SB_HEAD_DIM ** -0.5)
                dzbs.append(dz.astype(bf16))
            for (hh, _, kb, _, _), dzb in zip(chains, dzbs):
                carry[3 * hh + 2] = carry[3 * hh + 2] + _dot(dzb, kb, NN)
            nh = len(heads)
            for bi, (off, _, _, _) in enumerate(blocks):
                dk_j = jnp.zeros((tq, LANES), f32)
                dv_j = jnp.zeros((tq, LANES), f32)
                for hh, (_, qh) in enumerate(heads):
                    dk_j = dk_j + _dot(dzbs[bi * nh + hh], qh, TN)
                    dv_j = dv_j + _dot(wbs[bi * nh + hh], dos[hh], TN)
                dk_acc[pl.ds(off, tq), :] += dk_j
                dv_acc[pl.ds(off, tq), :] += dv_j
            return tuple(carry)

        zero1 = jnp.zeros((tq, 1), f32)
        it0 = jnp.clip(jnp.max(first_ref[...]).astype(jnp.int32), 0, n_it - 1)
        fin = lax.fori_loop(it0, n_it, step, (zero1, zero1, jnp.zeros((tq, LANES), f32)) * len(heads))
        dq_all = jnp.zeros((tq, LANES), f32)
        for hh, (m, _) in enumerate(heads):
            dq_all = dq_all + jnp.where(m, fin[3 * hh + 2], 0.0)
        dq_ref[...] = dq_all.astype(bf16)

        @pl.when(i == nq - 1)
        def _():
            dk_ref[...] = dk_acc[...].astype(bf16)
            dv_ref[...] = dv_acc[...].astype(bf16)

    row_blk = pl.BlockSpec((tq, LANES), lambda b, p, i: (b * nq + i, p))
    seq_blk = pl.BlockSpec((seq, LANES), lambda b, p, i: (b, p))
    out = SDS((t_all, w), bf16)
    (dq, dk, dv), extra = _call(
        body, ins=[qkv, qkv, qkv, ltot, first, dmix], out_shape=[out, out, out], grid=(nb, hp, nq),
        in_specs=[row_blk,
                  pl.BlockSpec((seq, LANES), lambda b, p, i: (b, hp + p)),
                  pl.BlockSpec((seq, LANES), lambda b, p, i: (b, 2 * hp + p)),
                  row_blk, pl.BlockSpec((SUBLANES, LANES), lambda b, p, i: (b * nq + i, p)), row_blk],
        out_specs=[row_blk, seq_blk, seq_blk],
        scratch_shapes=[pltpu.VMEM((seq, LANES), f32), pltpu.VMEM((seq, LANES), f32)], name=name, comm=comm)
    return dq, dk, dv, extra


POOL_CHUNK = 256
POOL_HALO = 16


def _band(rows, cols, lo, hi):
    r = lax.broadcasted_iota(jnp.int32, (rows, cols), 0)
    c = lax.broadcasted_iota(jnp.int32, (rows, cols), 1)
    d = c - r
    return jnp.where((d >= lo) & (d < hi), 1.0, 0.0).astype(bf16)


def _pool_counts(r0, rows, win):
    t = lax.broadcasted_iota(jnp.int32, (rows, 1), 0) + r0
    return jnp.minimum(t + 1, win).astype(f32)


def _pool_fwd(u, mix, pool_w, scale, seq, name):
    t_all, w = u.shape
    ng, rc = w // POOL_GROUP, min(POOL_CHUNK, seq)

    def body(u_ref, w_ref, s_ref, mix_in, p_ref, o_ref, pad):
        del mix_in
        pad[0:POOL_HALO, :] = jnp.zeros((POOL_HALO, POOL_GROUP), f32)
        for g in range(ng):
            cols = slice(g * POOL_GROUP, (g + 1) * POOL_GROUP)
            win = POOL_WINDOWS[g]
            pad[POOL_HALO:POOL_HALO + seq, :] = u_ref[:, cols]
            band = _band(rc, rc + POOL_HALO, POOL_HALO - win + 1, POOL_HALO + 1)
            wg = w_ref[g].astype(bf16)
            for r0 in range(0, seq, rc):
                ue = pad[r0:r0 + rc + POOL_HALO, :]
                hi, lo = _split_bf16(ue)
                sm = _dot(band, hi, NN) + _dot(band, lo, NN)
                pch = sm / _pool_counts(r0, rc, win) - ue[POOL_HALO:, :]
                pb = pch.astype(bf16)
                p_ref[r0:r0 + rc, cols] = pb
                o_ref[r0:r0 + rc, cols] = _dot(pb, wg, NN) * s_ref[:, cols]

    return pl.pallas_call(
        body, out_shape=(SDS((t_all, w), bf16), SDS(mix.shape, f32)), grid=(t_all // seq,),
        in_specs=[pl.BlockSpec((seq, w), lambda b: (b, 0)), pl.BlockSpec(pool_w.shape, lambda b: (0, 0, 0)),
                  pl.BlockSpec(scale.shape, lambda b: (0, 0)), ANY],
        out_specs=(pl.BlockSpec((seq, w), lambda b: (b, 0)), pl.BlockSpec((seq, w), lambda b: (b, 1))),
        scratch_shapes=[pltpu.VMEM((seq + POOL_HALO, POOL_GROUP), f32)],
        input_output_aliases={3: 1}, name=name, compiler_params=_params(1))(u, pool_w, scale, mix)


def _pool_bwd(dmix, p, pool_w, scale, seq, name):
    t_all, w = p.shape
    ng, rc = w // POOL_GROUP, min(POOL_CHUNK, seq)

    def body(dy_ref, p_ref, w_ref, s_ref, du_ref, dw_ref, ds_ref, dpn, dpr):
        b = pl.program_id(0)

        @pl.when(b == 0)
        def _():
            dw_ref[...] = jnp.zeros_like(dw_ref)
            ds_ref[...] = jnp.zeros_like(ds_ref)

        dpn[seq:seq + POOL_HALO, :] = jnp.zeros((POOL_HALO, POOL_GROUP), f32)
        for g in range(ng):
            cols = slice(g * POOL_GROUP, (g + 1) * POOL_GROUP)
            win = POOL_WINDOWS[g]
            wg = w_ref[g].astype(bf16)
            sg = s_ref[:, cols]
            dwg = jnp.zeros((POOL_GROUP, POOL_GROUP), f32)
            dsg = jnp.zeros((SUBLANES, POOL_GROUP), f32)
            for r0 in range(0, seq, rc):
                dy = dy_ref[r0:r0 + rc, cols]
                pb = p_ref[r0:r0 + rc, cols]
                dsg = dsg + _sum8(dy * _dot(pb, wg, NN))
                dyw = (dy * sg).astype(bf16)
                dwg = dwg + _dot(pb, dyw, TN)
                dp = _dot(dyw, wg, NT)
                dpr[r0:r0 + rc, :] = dp
                dpn[r0:r0 + rc, :] = dp / _pool_counts(r0, rc, win)
            dw_ref[g] += dwg
            ds_ref[:, cols] += dsg
            band = _band(rc, rc + POOL_HALO, 0, win)
            for r0 in range(0, seq, rc):
                hi, lo = _split_bf16(dpn[r0:r0 + rc + POOL_HALO, :])
                du = _dot(band, hi, NN) + _dot(band, lo, NN) - dpr[r0:r0 + rc, :]
                du_ref[r0:r0 + rc, cols] = du.astype(bf16)

    return pl.pallas_call(
        body, out_shape=(SDS((t_all, w), bf16), SDS(pool_w.shape, f32), SDS((SUBLANES, w), f32)), grid=(t_all // seq,),
        in_specs=[pl.BlockSpec((seq, w), lambda b: (b, 1)), pl.BlockSpec((seq, w), lambda b: (b, 0)),
                  pl.BlockSpec(pool_w.shape, lambda b: (0, 0, 0)), pl.BlockSpec(scale.shape, lambda b: (0, 0))],
        out_specs=(pl.BlockSpec((seq, w), lambda b: (b, 0)), pl.BlockSpec(pool_w.shape, lambda b: (0, 0, 0)),
                   pl.BlockSpec((SUBLANES, w), lambda b: (0, 0))),
        scratch_shapes=[pltpu.VMEM((seq + POOL_HALO, POOL_GROUP), f32), pltpu.VMEM((seq, POOL_GROUP), f32)],
        name=name, compiler_params=_params(1))(dmix, p, pool_w, scale)


XA_TQ = 256


def _xa_probs(qh, kh, dh):
    s = _dot(qh, kh, NT) * (dh ** -0.5)
    e = jnp.exp(s - jnp.max(s, axis=1, keepdims=True))
    return e / jnp.sum(e, axis=1, keepdims=True)


def _xa_fwd(q, kv, seq, name):
    t_all, d = q.shape
    nb = t_all // seq
    mem, dh, tq = kv.shape[0] // nb, d // XA_HEADS, min(XA_TQ, seq)
    nq = seq // tq

    def body(q_ref, kv_ref, o_ref):
        for h in range(XA_HEADS):
            cols = slice(h * dh, (h + 1) * dh)
            p = _xa_probs(q_ref[:, cols], kv_ref[:, cols], dh)
            o_ref[:, cols] = _dot(p.astype(bf16), kv_ref[:, d + h * dh:d + (h + 1) * dh], NN).astype(bf16)

    return pl.pallas_call(
        body, out_shape=SDS((t_all, d), bf16), grid=(nb, nq),
        in_specs=[pl.BlockSpec((tq, d), lambda b, i: (b * nq + i, 0)), pl.BlockSpec((mem, 2 * d), lambda b, i: (b, 0))],
        out_specs=pl.BlockSpec((tq, d), lambda b, i: (b * nq + i, 0)), name=name, compiler_params=_params(2))(q, kv)


def _xa_bwd(q, kv, do, seq, name):
    t_all, d = q.shape
    nb = t_all // seq
    mem, dh, tq = kv.shape[0] // nb, d // XA_HEADS, min(XA_TQ, seq)
    nq = seq // tq

    def body(q_ref, kv_ref, do_ref, dq_ref, dkv_ref):
        i = pl.program_id(1)

        @pl.when(i == 0)
        def _():
            dkv_ref[...] = jnp.zeros_like(dkv_ref)

        for h in range(XA_HEADS):
            cols = slice(h * dh, (h + 1) * dh)
            vcols = slice(d + h * dh, d + (h + 1) * dh)
            qh, kh, doh = q_ref[:, cols], kv_ref[:, cols], do_ref[:, cols]
            p = _xa_probs(qh, kh, dh)
            dkv_ref[:, vcols] += _dot(p.astype(bf16), doh, TN)
            dp = _dot(doh, kv_ref[:, vcols], NT)
            ds = (p * (dp - jnp.sum(dp * p, axis=1, keepdims=True)) * (dh ** -0.5)).astype(bf16)
            dq_ref[:, cols] = _dot(ds, kh, NN).astype(bf16)
            dkv_ref[:, cols] += _dot(ds, qh, TN)

    row = pl.BlockSpec((tq, d), lambda b, i: (b * nq + i, 0))
    kvs = pl.BlockSpec((mem, 2 * d), lambda b, i: (b, 0))
    return pl.pallas_call(body, out_shape=(SDS((t_all, d), bf16), SDS(kv.shape, f32)), grid=(nb, nq),
                          in_specs=[row, kvs, row], out_specs=(row, kvs), name=name, compiler_params=_params(2))(q, kv, do)


FFN_BR = 256
FFN_CHUNK = 256


def _conv3(ext, w_ref, b, cols, lo, rows):
    return (b + w_ref[2:3, cols] * ext[lo:lo + rows, :] + w_ref[1:2, cols] * ext[lo - 1:lo - 1 + rows, :]
            + w_ref[0:1, cols] * ext[lo - 2:lo - 2 + rows, :])


FFN_HALO = 16


def _ffn_gate_fwd(up, cw, cb, seq, name):
    t_all, f2 = up.shape
    ff, br, ch, hl = f2 // 2, min(FFN_BR, seq), FFN_CHUNK, FFN_HALO
    per_seq, hb = seq // br, br // hl

    def body(up_ref, halo_ref, cw_ref, cb_ref, o_ref, cv_ref, ev, eg):
        i = pl.program_id(0)
        keep = jnp.where(i % per_seq == 0, 0.0, 1.0)
        for c0 in range(0, ff, ch):
            convs = []
            for ext, off in ((ev, c0), (eg, ff + c0)):
                cols = slice(off, off + ch)
                ext[0:hl, :] = halo_ref[:, cols].astype(f32) * keep
                ext[hl:hl + br, :] = up_ref[:, cols].astype(f32)
                conv = _conv3(ext, cw_ref, cb_ref[:, cols], cols, hl, br)
                cv_ref[:, cols] = conv.astype(bf16)
                convs.append(conv)
            val, gate = convs
            o_ref[:, c0:c0 + ch] = (gate * _sigmoid(gate) * val).astype(bf16)

    return pl.pallas_call(
        body, out_shape=(SDS((t_all, ff), bf16), SDS((t_all, f2), bf16)), grid=(t_all // br,),
        in_specs=[pl.BlockSpec((br, f2), lambda i: (i, 0)),
                  pl.BlockSpec((hl, f2), lambda i: (jnp.maximum(i * hb - 1, 0), 0)),
                  pl.BlockSpec(cw.shape, lambda i: (0, 0)), pl.BlockSpec(cb.shape, lambda i: (0, 0))],
        out_specs=(pl.BlockSpec((br, ff), lambda i: (i, 0)), pl.BlockSpec((br, f2), lambda i: (i, 0))),
        scratch_shapes=[pltpu.VMEM((br + hl, ch), f32), pltpu.VMEM((br + hl, ch), f32)],
        name=name, compiler_params=_params(1))(up, up, cw, cb)


def _ffn_gate_bwd(dact, up, cv, cw, seq, name):
    t_all, f2 = up.shape
    ff, br, ch, hl = f2 // 2, min(FFN_BR, seq), FFN_CHUNK, FFN_HALO
    per_seq, hb, last = seq // br, br // hl, t_all // hl - 1
    ext_rows = br + SUBLANES

    def body(da_ref, dan_ref, cv_ref, cvn_ref, up_ref, upp_ref, cw_ref, du_ref, dcw_ref, dcb_ref, ext, e1, e2, e3, dcv, dcg):
        i = pl.program_id(0)

        @pl.when(i == 0)
        def _():
            dcw_ref[...] = jnp.zeros_like(dcw_ref)
            dcb_ref[...] = jnp.zeros_like(dcb_ref)

        keep_prev = jnp.where(i % per_seq == 0, 0.0, 1.0)
        keep_next = jnp.where((i + 1) % per_seq == 0, 0.0, 1.0)

        def with_next(scr, blk_ref, nxt_ref, cols, scale):
            scr[0:br, :] = blk_ref[:, cols].astype(f32)
            scr[br:br + hl, :] = nxt_ref[:, cols].astype(f32) * scale
            return scr[0:ext_rows, :]

        for c0 in range(0, ff, ch):
            da = with_next(e1, da_ref, dan_ref, slice(c0, c0 + ch), keep_next)
            val = with_next(e2, cv_ref, cvn_ref, slice(c0, c0 + ch), 1.0)
            gate = with_next(e3, cv_ref, cvn_ref, slice(ff + c0, ff + c0 + ch), 1.0)
            sg = _sigmoid(gate)
            dcv[...] = da * gate * sg
            dcg[...] = da * val * sg * (1.0 + gate * (1.0 - sg))
            for dc, off in ((dcv, c0), (dcg, ff + c0)):
                cols = slice(off, off + ch)
                du = (cw_ref[2:3, cols] * dc[0:br, :] + cw_ref[1:2, cols] * dc[1:br + 1, :]
                      + cw_ref[0:1, cols] * dc[2:br + 2, :])
                du_ref[:, cols] = du.astype(bf16)
                d0 = dc[0:br, :]
                dcb_ref[:, cols] += _sum8(d0)
                ext[0:hl, :] = upp_ref[:, cols].astype(f32) * keep_prev
                ext[hl:hl + br, :] = up_ref[:, cols].astype(f32)
                for tap in range(3):
                    lo = hl - (2 - tap)
                    dcw_ref[tap, :, cols] += _sum8(d0 * ext[lo:lo + br, :])

    blk = lambda n: pl.BlockSpec((br, n), lambda i: (i, 0))
    prev = lambda n: pl.BlockSpec((hl, n), lambda i: (jnp.maximum(i * hb - 1, 0), 0))
    nxt = lambda n: pl.BlockSpec((hl, n), lambda i: (jnp.minimum((i + 1) * hb, last), 0))
    return pl.pallas_call(
        body, out_shape=(SDS((t_all, f2), bf16), SDS((3, SUBLANES, f2), f32), SDS((SUBLANES, f2), f32)), grid=(t_all // br,),
        in_specs=[blk(ff), nxt(ff), blk(f2), nxt(f2), blk(f2), prev(f2), pl.BlockSpec(cw.shape, lambda i: (0, 0))],
        out_specs=(blk(f2), pl.BlockSpec((3, SUBLANES, f2), lambda i: (0, 0, 0)), pl.BlockSpec((SUBLANES, f2), lambda i: (0, 0))),
        scratch_shapes=[pltpu.VMEM((br + hl, ch), f32)] * 4 + [pltpu.VMEM((ext_rows, ch), f32)] * 2,
        name=name, compiler_params=_params(1))(dact, dact, cv, cv, up, up, cw)


SSM_GB = 8
SSM_PLANES = 8
SSM_ROWS = 256
SSM_UNROLL = 8


def _ssm_pitch(seq):
    p = seq + SUBLANES
    assert (p // SUBLANES) % 2 == 1
    return p


def _rows(base, rc):
    return pl.ds(pl.multiple_of(base + rc * SSM_ROWS, SUBLANES), SSM_ROWS)


def _ssm_project_in(u_ref, b_ref, planes, e, seq, pitch):
    def chunk(rc, _):
        uc = u_ref[_rows(e * seq, rc), :].astype(bf16)
        for j in range(SSM_PLANES):
            planes[_rows(j * pitch, rc), :] = _dot(uc, b_ref[:, j * LANES:(j + 1) * LANES], NN)
        return 0
    lax.fori_loop(0, seq // SSM_ROWS, chunk, 0)


def _ssm_rows(planes, rc, pitch):
    return jnp.concatenate([planes[_rows(j * pitch, rc), :].astype(bf16) for j in range(SSM_PLANES)], axis=1)


def _ssm_scan(planes_list, l1, l2, seq, pitch, reverse=False):
    def step(s, hs):
        hs = list(hs)
        for k in range(SSM_UNROLL):
            t = s * SSM_UNROLL + k
            t = seq - 1 - t if reverse else t
            for e, planes in enumerate(planes_list):
                hs[e] = hs[e] * l1 + pltpu.roll(hs[e], 4, 0) * l2 + planes[pl.ds(t, SUBLANES, stride=pitch), :]
                planes[pl.ds(t, SUBLANES, stride=pitch), :] = hs[e]
        return tuple(hs)
    zero = jnp.zeros((SUBLANES, LANES), f32)
    lax.fori_loop(0, seq // SSM_UNROLL, step, tuple(zero for _ in planes_list))


def _ssm_fwd(u, b_big, c_big, lslab, dskip, seq, name, comm=None):
    t_all, w = u.shape
    nb, gw, pitch = t_all // seq, SSM_GB * SSM_GROUP, _ssm_pitch(seq)
    assert gw == LANES

    def body(u_ref, b_ref, c_ref, l_ref, d_ref, y_ref, *planes):
        l1, l2 = l_ref[0:SUBLANES, :], l_ref[SUBLANES:2 * SUBLANES, :]
        for e in range(nb):
            _ssm_project_in(u_ref, b_ref, planes[e], e, seq, pitch)
        _ssm_scan(planes, l1, l2, seq, pitch)
        for e in range(nb):
            def chunk(rc, _, e=e):
                rows = _rows(e * seq, rc)
                y_ref[rows, :] = _dot(_ssm_rows(planes[e], rc, pitch), c_ref[...], NN) + d_ref[...] * u_ref[rows, :]
                return 0
            lax.fori_loop(0, seq // SSM_ROWS, chunk, 0)

    (y,), extra = _call(
        body, ins=[u, b_big, c_big, lslab, dskip], out_shape=[SDS((t_all, w), f32)], grid=(w // gw,),
        in_specs=[pl.BlockSpec((t_all, gw), lambda k: (0, k)), pl.BlockSpec((None,) + b_big.shape[1:], lambda k: (k, 0, 0)),
                  pl.BlockSpec((None,) + c_big.shape[1:], lambda k: (k, 0, 0)),
                  pl.BlockSpec((None,) + lslab.shape[1:], lambda k: (k, 0, 0)), pl.BlockSpec((1, gw), lambda k: (0, k))],
        out_specs=[pl.BlockSpec((t_all, gw), lambda k: (0, k))],
        scratch_shapes=[pltpu.VMEM((SSM_PLANES * pitch, LANES), f32) for _ in range(nb)], name=name, comm=comm)
    return y, extra


def _ssm_bwd(u, dy, b_big, c_big, lslab, dskip, seq, name, comm=None):
    t_all, w = u.shape
    nb, gw, pitch = t_all // seq, SSM_GB * SSM_GROUP, _ssm_pitch(seq)
    ns = SSM_PLANES * LANES

    def body(u_ref, dy_ref, b_ref, c_ref, l_ref, d_ref, du_ref, db_ref, dc_ref, dl_ref, dd_ref, *planes):
        hp, ap = planes[:nb], planes[nb:]
        l1, l2 = l_ref[0:SUBLANES, :], l_ref[SUBLANES:2 * SUBLANES, :]
        for e in range(nb):
            _ssm_project_in(u_ref, b_ref, hp[e], e, seq, pitch)
        _ssm_scan(hp, l1, l2, seq, pitch)
        dd_ref[...] = jnp.zeros_like(dd_ref)
        dc_ref[...] = jnp.zeros_like(dc_ref)
        db_ref[...] = jnp.zeros_like(db_ref)
        for e in range(nb):
            def chunk(rc, _, e=e):
                rows = _rows(e * seq, rc)
                dyc = dy_ref[rows, :]
                dyb = dyc.astype(bf16)
                for j in range(SSM_PLANES):
                    ap[e][_rows(j * pitch, rc), :] = _dot(dyb, c_ref[j * LANES:(j + 1) * LANES, :], NT)
                dd_ref[...] += _sum8(dyc * u_ref[rows, :])
                dc_ref[...] += _dot(_ssm_rows(hp[e], rc, pitch), dyb, TN)
                return 0
            lax.fori_loop(0, seq // SSM_ROWS, chunk, 0)

        def step(s, carry):
            carry = [list(c) for c in carry]
            for k in range(SSM_UNROLL):
                t = seq - 1 - (s * SSM_UNROLL + k)
                for e in range(nb):
                    a, s1, s2 = carry[e]
                    a = a * l1 - pltpu.roll(a, 4, 0) * l2 + ap[e][pl.ds(t, SUBLANES, stride=pitch), :]
                    ap[e][pl.ds(t, SUBLANES, stride=pitch), :] = a
                    hprev = hp[e][pl.ds(jnp.maximum(t - 1, 0), SUBLANES, stride=pitch), :] * jnp.where(t > 0, 1.0, 0.0)
                    carry[e] = [a, s1 + a * hprev, s2 + a * pltpu.roll(hprev, 4, 0)]
            return tuple(tuple(c) for c in carry)
        zero = jnp.zeros((SUBLANES, LANES), f32)
        fin = lax.fori_loop(0, seq // SSM_UNROLL, step, tuple((zero, zero, zero) for _ in range(nb)))
        dl_ref[0:SUBLANES, :] = sum(f[1] for f in fin)
        dl_ref[SUBLANES:2 * SUBLANES, :] = sum(f[2] for f in fin)

        for e in range(nb):
            def chunk2(rc, _, e=e):
                rows = _rows(e * seq, rc)
                ar = _ssm_rows(ap[e], rc, pitch)
                du_ref[rows, :] = (_dot(ar, b_ref[...], NT) + d_ref[...] * dy_ref[rows, :]).astype(bf16)
                db_ref[...] += _dot(u_ref[rows, :].astype(bf16), ar, TN)
                return 0
            lax.fori_loop(0, seq // SSM_ROWS, chunk2, 0)

    col = pl.BlockSpec((t_all, gw), lambda k: (0, k))
    per = lambda s: pl.BlockSpec((None,) + s[1:], lambda k: (k, 0, 0))
    ng = w // gw
    res, extra = _call(
        body, ins=[u, dy, b_big, c_big, lslab, dskip],
        out_shape=[SDS((t_all, w), bf16), SDS(b_big.shape, f32), SDS(c_big.shape, f32), SDS((ng, 2 * SUBLANES, LANES), f32),
                   SDS((SUBLANES, w), f32)],
        grid=(ng,),
        in_specs=[col, col, per(b_big.shape), per(c_big.shape), per(lslab.shape), pl.BlockSpec((1, gw), lambda k: (0, k))],
        out_specs=[col, per(b_big.shape), per(c_big.shape), per((ng, 2 * SUBLANES, LANES)), pl.BlockSpec((SUBLANES, gw), lambda k: (0, k))],
        scratch_shapes=[pltpu.VMEM((SSM_PLANES * pitch, LANES), f32) for _ in range(2 * nb)], name=name, comm=comm)
    return (*res, extra)


def _ssm_disc_fwd(lam_re, lam_im, dt, b_re, b_im, name):
    def body(a_ref, b_ref, dt_ref, br_ref, bi_ref, lr_ref, li_ref, cr_ref, ci_ref, bbr_ref, bbi_ref):
        a, b, dtv = a_ref[...], b_ref[...], dt_ref[...]
        mag, ang = jnp.exp(a * dtv), b * dtv
        lr, li = mag * jnp.cos(ang), mag * jnp.sin(ang)
        nr, den = lr - 1.0, a * a + b * b
        cr, ci = (nr * a + li * b) / den, (li * a - nr * b) / den
        lr_ref[...], li_ref[...], cr_ref[...], ci_ref[...] = lr, li, cr, ci
        bbr_ref[...] = cr * br_ref[...] - ci * bi_ref[...]
        bbi_ref[...] = cr * bi_ref[...] + ci * br_ref[...]
    c, m = SDS(lam_re.shape, f32), SDS(b_re.shape, f32)
    return pl.pallas_call(body, out_shape=(c, c, c, c, m, m), name=name)(lam_re, lam_im, dt, b_re, b_im)


def _ssm_disc_bwd(lam_re, lam_im, dt, b_re, b_im, g_lr, g_li, g_bbr, g_bbi, name):
    def body(a_ref, b_ref, dt_ref, br_ref, bi_ref, glr_ref, gli_ref, gbr_ref, gbi_ref, da_ref, db_ref, ddt_ref, dbr_ref, dbi_ref):
        a, b, dtv = a_ref[...], b_ref[...], dt_ref[...]
        mag, ang = jnp.exp(a * dtv), b * dtv
        cs, sn = jnp.cos(ang), jnp.sin(ang)
        lr, li = mag * cs, mag * sn
        nr, den = lr - 1.0, a * a + b * b
        cr, ci = (nr * a + li * b) / den, (li * a - nr * b) / den
        gbr, gbi, brv, biv = gbr_ref[...], gbi_ref[...], br_ref[...], bi_ref[...]
        dbr_ref[...] = cr * gbr + ci * gbi
        dbi_ref[...] = cr * gbi - ci * gbr
        dcr = jnp.sum(brv * gbr + biv * gbi, axis=1, keepdims=True)
        dci = jnp.sum(brv * gbi - biv * gbr, axis=1, keepdims=True)
        dnum_r, dnum_i = dcr / den, dci / den
        dden = -(dcr * cr + dci * ci) / den
        dnr = dnum_r * a - dnum_i * b
        dli = gli_ref[...] + dnum_r * b + dnum_i * a
        dlr = glr_ref[...] + dnr
        dmag, dang = dlr * cs + dli * sn, dli * lr - dlr * li
        dadt = dmag * mag
        da_ref[...] = dnum_r * nr + dnum_i * li + dden * 2.0 * a + dadt * dtv
        db_ref[...] = dnum_r * li - dnum_i * nr + dden * 2.0 * b + dang * dtv
        ddt_ref[...] = dadt * a + dang * b
    c, m = SDS(lam_re.shape, f32), SDS(b_re.shape, f32)
    return pl.pallas_call(body, out_shape=(c, c, c, m, m), name=name)(lam_re, lam_im, dt, b_re, b_im, g_lr, g_li, g_bbr, g_bbi)


def _place():
    x, y, c = lax.axis_index("x"), lax.axis_index("y"), lax.axis_index("c")
    return x, y, c, 2 * x + y


def _half_axis(shape, ax):
    return 0 if shape[0] == 2 else (3 - ax)


def _sub(ref, axis, start, size):
    idx = [slice(None)] * len(ref.shape)
    idx[axis] = pl.ds(start, size)
    return ref.at[tuple(idx)]


def _region(ref, full_shape, ax, slot=None, half=None):
    if slot is not None:
        n = full_shape[ax] // N_CHIPS
        ref = _sub(ref, ax, slot * n, n)
    if half is not None:
        ha = _half_axis(full_shape, ax)
        n = full_shape[ha] // 2
        ref = _sub(ref, ha, half * n, n)
    return ref


def _halved(shape, axis):
    return tuple(s // 2 if a == axis else s for a, s in enumerate(shape))


class _Comm:
    def __init__(self, ins, out_shapes, aliases, scratch, start, finish):
        self.ins, self.out_shapes, self.aliases, self.scratch, self.start, self.finish = ins, out_shapes, aliases, scratch, start, finish


def _call(body, *, ins, in_specs, out_shape, out_specs, grid, scratch_shapes, name, comm=None):
    if comm is None:
        res = pl.pallas_call(body, out_shape=tuple(out_shape), grid=grid, in_specs=list(in_specs), out_specs=tuple(out_specs),
                             scratch_shapes=list(scratch_shapes), name=name, compiler_params=_params(len(grid)))(*ins)
        return list(res), []
    n_in, n_out, n_scr, c_in, c_out = len(ins), len(out_shape), len(scratch_shapes), len(comm.ins), len(comm.out_shapes)

    def fused(*refs):
        pos = [n_in, n_in + c_in, n_in + c_in + n_out, n_in + c_in + n_out + c_out, n_in + c_in + n_out + c_out + n_scr]
        in_refs, cin, out_refs, cout, scr, cscr = (refs[:pos[0]], refs[pos[0]:pos[1]], refs[pos[1]:pos[2]], refs[pos[2]:pos[3]],
                                                   refs[pos[3]:pos[4]], refs[pos[4]:])
        ids = [pl.program_id(a) for a in range(len(grid))]
        first, last = ids[0] == 0, ids[0] == grid[0] - 1
        for a in range(1, len(grid)):
            first, last = first & (ids[a] == 0), last & (ids[a] == grid[a] - 1)

        @pl.when(first)
        def _():
            comm.start(cin, cout, cscr)

        body(*in_refs, *out_refs, *scr)

        @pl.when(last)
        def _():
            comm.finish(cin, cout, cscr)

    res = pl.pallas_call(
        fused, out_shape=tuple(out_shape) + tuple(comm.out_shapes), grid=grid, in_specs=list(in_specs) + [ANY] * c_in,
        out_specs=tuple(out_specs) + tuple([ANY] * c_out), scratch_shapes=list(scratch_shapes) + list(comm.scratch),
        input_output_aliases={n_in + i: n_out + o for i, o in comm.aliases}, name=name, compiler_params=_params(len(grid)))(*ins, *comm.ins)
    return list(res[:n_out]), list(res[n_out:])


def _comm_only(comm, name):
    c_in, c_out = len(comm.ins), len(comm.out_shapes)

    def body(*refs):
        cin, cout, cscr = refs[:c_in], refs[c_in:c_in + c_out], refs[c_in + c_out:]
        comm.start(cin, cout, cscr)
        comm.finish(cin, cout, cscr)

    return pl.pallas_call(body, out_shape=tuple(comm.out_shapes), in_specs=[ANY] * c_in, out_specs=tuple([ANY] * c_out),
                          scratch_shapes=list(comm.scratch), input_output_aliases=dict(comm.aliases), name=name)(*comm.ins)


def _gather_plan(shards, axes):
    n = len(shards)
    fulls = [tuple(s * N_CHIPS if a == ax else s for a, s in enumerate(sh.shape)) for sh, ax in zip(shards, axes)]
    own = 6

    def copies(src, dst, scr):
        send_sems, recv_sems = scr
        x, y, c, p = _place()
        chips = [(1 - x, y), (x, 1 - y), (1 - x, 1 - y)]
        slots = [2 * cx + cy for cx, cy in chips]

        def copy(a, k, slot, half, to, from_shard=False):
            where = _region(dst[a], fulls[a], axes[a], slot, half)
            source = where
            if from_shard:
                ha = _half_axis(fulls[a], axes[a])
                hn = fulls[a][ha] // 2
                source = _sub(src[a], ha, half * hn, hn)
            return pltpu.make_async_remote_copy(src_ref=source, dst_ref=where, send_sem=send_sems.at[a, k],
                                                recv_sem=recv_sems.at[a, k], device_id=to, device_id_type=MESH)

        parts = range(n)
        mine = [pltpu.make_async_remote_copy(src_ref=src[a], dst_ref=_region(dst[a], fulls[a], axes[a], p),
                                             send_sem=send_sems.at[a, own], recv_sem=recv_sems.at[a, own],
                                             device_id=(x, y, 1 - c), device_id_type=MESH) for a in parts]
        first = [copy(a, j, p, c, (*chips[j], c), True) for a in parts for j in range(3)]
        landed = [copy(a, j, slots[j], c, (x, y, c)) for a in parts for j in range(3)]
        passed = [copy(a, 3 + j, slots[j], c, (x, y, 1 - c)) for a in parts for j in range(3)]
        handed = [copy(a, 3 + j, slots[j], 1 - c, (x, y, c)) for a in parts for j in range(3)]
        return mine, first, landed, passed, handed

    def start(src, dst, scr):
        mine, first, _, _, _ = copies(src, dst, scr)
        for cp in first + mine:
            cp.start()

    def finish(src, dst, scr):
        mine, first, landed, passed, handed = copies(src, dst, scr)
        for arrived, fwd in zip(landed, passed):
            arrived.wait_recv()
            fwd.start()
        for cp in handed + mine:
            cp.wait_recv()
        for cp in first + passed + mine:
            cp.wait_send()

    return _Comm(list(shards), [SDS(f, s.dtype) for f, s in zip(fulls, shards)], [],
                 [pltpu.SemaphoreType.DMA((n, 7)), pltpu.SemaphoreType.DMA((n, 7))], start, finish)


def _all_gather(shards, axes, name):
    return _comm_only(_gather_plan(shards, axes), name)


def _swap_halves(grads, axes, name):
    n = len(grads)
    shapes = [g.shape for g in grads]

    def body(*refs):
        src, dst = refs[:n], refs[n:2 * n]
        send_sems, recv_sems = refs[2 * n:]
        x, y, c, _ = _place()
        cps = [pltpu.make_async_remote_copy(src_ref=_region(src[a], shapes[a], axes[a], None, 1 - c), dst_ref=dst[a],
                                            send_sem=send_sems.at[a], recv_sem=recv_sems.at[a],
                                            device_id=(x, y, 1 - c), device_id_type=MESH) for a in range(n)]
        for cp in cps:
            cp.start()
        for cp in cps:
            cp.wait()

    outs = tuple(SDS(_halved(s, _half_axis(s, ax)), g.dtype) for s, ax, g in zip(shapes, axes, grads))
    return pl.pallas_call(body, out_shape=outs, in_specs=[ANY] * n, out_specs=tuple([ANY] * n),
                          scratch_shapes=[pltpu.SemaphoreType.DMA((n,)), pltpu.SemaphoreType.DMA((n,))], name=name)(*grads)


def _row_block(rows, row_bytes, limit=3 << 20):
    for b in (1024, 512, 256, 128, 64, 32, 16, 8):
        if rows % b == 0 and b * row_bytes <= limit:
            return b
    return rows


def _add_own_half(g, other, ax, cidx, name):
    _, kp, np_ = other.shape
    ha = _half_axis(g.shape, ax)
    ks, ns = (kp // N_CHIPS, np_) if ax == 1 else (kp, np_ // N_CHIPS)
    bk = _row_block(ks, ns * 4)
    nkb = ks // bk

    def g_map(q, i, cref):
        c = cref[0]
        if ax == 1:
            return (c, q * nkb + i, 0) if ha == 0 else (0, q * nkb + i, c)
        return (c, i, q) if ha == 0 else (0, c * nkb + i, q)

    def o_map(q, i, cref):
        return (0, q * nkb + i, 0) if ax == 1 else (0, i, q)

    def body(c_ref, g_ref, o_ref, send_ref, land_ref):
        del c_ref
        s = (g_ref[...].astype(f32) + o_ref[...].astype(f32)).astype(send_ref.dtype)
        send_ref[...] = s
        land_ref[...] = s

    out = pl.BlockSpec((None, bk, ns), lambda q, i, cref: (q, i, 0))
    grid_spec = pltpu.PrefetchScalarGridSpec(
        num_scalar_prefetch=1, grid=(N_CHIPS, nkb),
        in_specs=[pl.BlockSpec((None, bk, ns), g_map), pl.BlockSpec((None, bk, ns), o_map)], out_specs=(out, out))
    shape = SDS((N_CHIPS, ks, ns), g.dtype)
    return pl.pallas_call(body, out_shape=(shape, shape), grid_spec=grid_spec, name=name, compiler_params=_params(2))(cidx, g, other)


def _owner_plan(sends, lands):
    n = len(sends)

    def copies(cin, dst, scr):
        src = cin[:n]
        send_sems, recv_sems = scr
        x, y, c, p = _place()
        chips = [(1 - x, y), (x, 1 - y), (1 - x, 1 - y)]
        slots = [2 * cx + cy for cx, cy in chips]
        out = [pltpu.make_async_remote_copy(src_ref=src[a].at[slots[j]], dst_ref=dst[a].at[p], send_sem=send_sems.at[a, j],
                                            recv_sem=recv_sems.at[a, j], device_id=(*chips[j], c), device_id_type=MESH)
               for a in range(n) for j in range(3)]
        back = [pltpu.make_async_remote_copy(src_ref=src[a].at[p], dst_ref=dst[a].at[slots[j]], send_sem=send_sems.at[a, j],
                                             recv_sem=recv_sems.at[a, j], device_id=(x, y, c), device_id_type=MESH)
                for a in range(n) for j in range(3)]
        return out, back

    def start(cin, dst, scr):
        for cp in copies(cin, dst, scr)[0]:
            cp.start()

    def finish(cin, dst, scr):
        out, back = copies(cin, dst, scr)
        for cp in back:
            cp.wait_recv()
        for cp in out:
            cp.wait_send()

    return _Comm(list(sends) + list(lands), [SDS(l.shape, l.dtype) for l in lands], [(n + a, a) for a in range(n)],
                 [pltpu.SemaphoreType.DMA((n, 3)), pltpu.SemaphoreType.DMA((n, 3))], start, finish)


def _sum_chips(stack, shard_shape, ax, cidx, name):
    _, ks, ns = stack.shape
    ha = _half_axis(shard_shape, ax)
    bk = _row_block(ks, ns * 4 * N_CHIPS)
    nkb = ks // bk

    def o_map(i, cref):
        c = cref[0]
        return (c, i, 0) if ha == 0 else ((0, c * nkb + i, 0) if ha == 1 else (0, i, c))

    def body(c_ref, s_ref, o_ref):
        del c_ref
        acc = s_ref[0].astype(f32)
        for q in range(1, N_CHIPS):
            acc = acc + s_ref[q].astype(f32)
        o_ref[...] = acc

    grid_spec = pltpu.PrefetchScalarGridSpec(
        num_scalar_prefetch=1, grid=(nkb,), in_specs=[pl.BlockSpec((N_CHIPS, bk, ns), lambda i, cref: (0, i, 0))],
        out_specs=pl.BlockSpec((None, bk, ns), o_map))
    return pl.pallas_call(body, out_shape=SDS(shard_shape, f32), grid_spec=grid_spec, name=name, compiler_params=_params(1))(cidx, stack)


def _join_halves(slices, axes, name):
    n = len(slices)

    def body(*refs):
        dst = refs[n:2 * n]
        send_sems, recv_sems = refs[2 * n:]
        x, y, c, _ = _place()

        def half(a, h):
            ha = _half_axis(slices[a].shape, axes[a])
            hn = slices[a].shape[ha] // 2
            return _sub(dst[a], ha, h * hn, hn)

        cps = [pltpu.make_async_remote_copy(src_ref=half(a, c), dst_ref=half(a, c), send_sem=send_sems.at[a], recv_sem=recv_sems.at[a],
                                            device_id=(x, y, 1 - c), device_id_type=MESH) for a in range(n)]
        for cp in cps:
            cp.start()
        for a in range(n):
            pltpu.make_async_remote_copy(src_ref=half(a, c), dst_ref=half(a, 1 - c), send_sem=send_sems.at[a], recv_sem=recv_sems.at[a],
                                         device_id=(x, y, c), device_id_type=MESH).wait_recv()
        for cp in cps:
            cp.wait_send()

    return pl.pallas_call(
        body, out_shape=tuple(SDS(s.shape, s.dtype) for s in slices), in_specs=[ANY] * n, out_specs=tuple([ANY] * n),
        scratch_shapes=[pltpu.SemaphoreType.DMA((n,)), pltpu.SemaphoreType.DMA((n,))],
        input_output_aliases={a: a for a in range(n)}, name=name)(*slices)


def _core_index():
    return jnp.reshape(lax.axis_index("c"), (1,)).astype(jnp.int32)


def _reduce_begin(grads, axes, tag):
    cidx = _core_index()
    others = _swap_halves(grads, axes, f"rs_swap_{tag}")
    pairs = [_add_own_half(g, o, ax, cidx, f"rs_add_{tag}_{a}") for a, (g, o, ax) in enumerate(zip(grads, others, axes))]
    return _owner_plan([s for s, _ in pairs], [l for _, l in pairs])


def _reduce_end(stacks, shapes, axes, tag):
    cidx = _core_index()
    shard_shapes = [tuple(s // N_CHIPS if i == ax else s for i, s in enumerate(sh)) for sh, ax in zip(shapes, axes)]
    slices = [_sum_chips(s, sh, ax, cidx, f"rs_sum_{tag}_{a}") for a, (s, sh, ax) in enumerate(zip(stacks, shard_shapes, axes))]
    return _join_halves(slices, axes, f"rs_join_{tag}")


def _reduce_scatter(grads, axes, tag):
    stacks = _comm_only(_reduce_begin(grads, axes, tag), f"rs_owner_{tag}")
    return _reduce_end(stacks, [g.shape for g in grads], axes, tag)


SMALL_COLS = 256


def _pack(arrays, rows_multiple):
    flat = jnp.concatenate([a.reshape(-1).astype(f32) for a in arrays])
    rows = -(-flat.shape[0] // SMALL_COLS)
    rows = -(-rows // rows_multiple) * rows_multiple
    flat = jnp.pad(flat, (0, rows * SMALL_COLS - flat.shape[0]))
    return flat.reshape(1, rows, SMALL_COLS)


def _unpack(buf, shapes):
    flat, out, off = buf.reshape(-1), [], 0
    for s in shapes:
        n = math.prod(s)
        out.append(flat[off:off + n].reshape(s))
        off += n
    return out


def _block_diag_in(bb):
    g, p, c = bb.shape
    k = g // SSM_GB
    eye = jnp.eye(SSM_GB, dtype=bb.dtype)
    return jnp.einsum("kgpc,gh->kgchp", bb.reshape(k, SSM_GB, p, c), eye).reshape(k, SSM_GB * c, SSM_GB * p)


def _block_diag_out(cc):
    g, c, p = cc.shape
    k = g // SSM_GB
    eye = jnp.eye(SSM_GB, dtype=cc.dtype)
    return jnp.einsum("kgcp,gh->kgphc", cc.reshape(k, SSM_GB, c, p), eye).reshape(k, SSM_GB * p, SSM_GB * c)


def _diag_in(db, p, c):
    k = db.shape[0]
    return jnp.einsum("kgcgp->kgpc", db.reshape(k, SSM_GB, c, SSM_GB, p)).reshape(k * SSM_GB, p, c)


def _diag_out(dc, p, c):
    k = dc.shape[0]
    return jnp.einsum("kgpgc->kgcp", dc.reshape(k, SSM_GB, p, SSM_GB, c)).reshape(k * SSM_GB, c, p)


def _state_slab(v):
    g, p = v.shape
    return v.reshape(g // SSM_GB, SSM_GB * p // LANES, LANES)


BIG = ("ab_w_in", "ab_w_out", "ssm_w_in", "ssm_w_glu", "xa_w_q", "xa_w_kv", "xa_w_o", "ffn_w_up", "ffn_w_down")
BIG_AXIS = dict(ab_w_in=2, ab_w_out=1, ssm_w_in=1, ssm_w_glu=2, xa_w_q=1, xa_w_kv=2, xa_w_o=1, ffn_w_up=2, ffn_w_down=1)
SMALL_REPL = ("norm_mix", "norm_xattn", "norm_ffn", "norm_mem", "norm_final", "pool_w", "pool_scale", "ssm_lam_re", "ssm_lam_im",
              "ssm_log_dt", "ssm_b_re", "ssm_b_im", "ssm_c_re", "ssm_c_im", "ffn_conv_b")
SMALL_SHARDED = ("ssm_d", "ffn_conv_w")
FIRST_MIXER = ("ab_w_in", "ab_w_out")
WEIGHTS = ("norm_mix", "norm_xattn", "norm_ffn", "norm_mem", "norm_final", "ab_w_in", "pool_w", "pool_scale", "ab_w_out", "ssm_w_in",
           "ssm_lam_re", "ssm_lam_im", "ssm_log_dt", "ssm_b_re", "ssm_b_im", "ssm_c_re", "ssm_c_im", "ssm_d", "ssm_w_glu", "xa_w_q",
           "xa_w_kv", "xa_w_o", "ffn_w_up", "ffn_conv_w", "ffn_conv_b", "ffn_w_down")


class _Reducer:
    def __init__(self):
        self.done, self.groups = {}, 0

    def begin(self, keys, gw):
        self.groups += 1
        return _reduce_begin([gw[k] for k in keys], [BIG_AXIS.get(k[0], 1) for k in keys], f"g{self.groups}")

    def end(self, keys, gw, stacks):
        slices = _reduce_end(stacks, [gw[k].shape for k in keys], [BIG_AXIS.get(k[0], 1) for k in keys], f"g{self.groups}")
        self.done.update(zip(keys, slices))


def _local_step(xf, memf, tgt, w, wf, conv_w, ssm_d, seq, late_weights=None, reducer=None):
    d = xf.shape[1]
    depth = w["norm_mix"].shape[0]
    wf = dict(wf)
    late_weights = late_weights or {}
    sbw = wf["ab_w_in", 0].shape[2] // 4
    row = lambda a: a.reshape(1, -1)

    gs, ps = w["ssm_lam_re"].shape[1:]
    col = lambda a: a.reshape(gs * ps, 1)
    lam_re, lam_im = col(w["ssm_lam_re"][0]), col(w["ssm_lam_im"][0])
    dt = col(jnp.broadcast_to(jnp.exp(w["ssm_log_dt"][0])[:, None], (gs, ps)))
    b_re, b_im = w["ssm_b_re"][0].reshape(gs * ps, -1), w["ssm_b_im"][0].reshape(gs * ps, -1)
    lb_re, lb_im, _, _, bb_re, bb_im = _ssm_disc_fwd(lam_re, lam_im, dt, b_re, b_im, "ssm_disc")
    cgrp = b_re.shape[1]
    b_big = jnp.concatenate([_block_diag_in(bb_re.reshape(gs, ps, cgrp)), _block_diag_in(bb_im.reshape(gs, ps, cgrp))], axis=2).astype(bf16)
    c_big = jnp.concatenate([_block_diag_out(w["ssm_c_re"][0]), -_block_diag_out(w["ssm_c_im"][0])], axis=1).astype(bf16)
    lr_s, li_s = _state_slab(lb_re.reshape(gs, ps)), _state_slab(lb_im.reshape(gs, ps))
    lslab = jnp.concatenate([lr_s, lr_s, -li_s, li_s], axis=1)

    mem_n = _norm_fwd(memf, row(w["norm_mem"]), "norm_mem")
    kv = [None] * depth
    xs, saved = [xf], []
    cur = xf
    for l in range(depth):
        sv = {}
        h = _norm_fwd(cur, row(w["norm_mix"][l]), f"norm_mix{l}")
        sv["h"] = h
        if l % 2 == 0:
            qkv = _mm(h, wf["ab_w_in", 0], mode="nn", b_l=0, n=3 * sbw, out_dtype=bf16, name=f"qkv{l}")
            u = _mm(h, wf["ab_w_in", 0], mode="nn", b_l=0, b_n0=3 * sbw, n=sbw, out_dtype=f32, name=f"poolin{l}")
            plan, names = late_weights.get(f"sb_fwd{l}", (None, ()))
            mix, ltot, first, late = _sb_fwd(qkv, seq, f"sb_fwd{l}", comm=plan)
            wf.update(zip(names, late))
            pooled, mix = _pool_fwd(u, mix, w["pool_w"][0], w["pool_scale"], seq, f"pool_fwd{l}")
            sv.update(qkv=qkv, mix=mix, ltot=ltot, first=first, pooled=pooled)
            cur = _mm(mix, wf["ab_w_out", 0], mode="nn", b_l=0, res=cur, out_dtype=f32, name=f"mixout{l}")
        else:
            us = _mm(h, wf["ssm_w_in", 0], mode="nn", b_l=0, out_dtype=f32, name=f"ssmin{l}")
            plan, names = late_weights.get(f"ssm_fwd{l}", (None, ()))
            ys, late = _ssm_fwd(us, b_big, c_big, lslab, ssm_d, seq, f"ssm_fwd{l}", comm=plan)
            wf.update(zip(names, late))
            gl = _gelu_fwd(ys, f"gelu{l}")
            glu = _mm(gl, wf["ssm_w_glu", 0], mode="nn", b_l=0, out_dtype=f32, name=f"glu{l}")
            sv.update(us=us, ys=ys, gl=gl, glu=glu)
            cur = _glu_fwd(glu, cur, f"glugate{l}")
        sv["x1"] = cur
        kv[l] = _mm(mem_n, wf["xa_w_kv", l], mode="nn", b_l=0, out_dtype=bf16, name=f"kv{l}")
        hx = _norm_fwd(cur, row(w["norm_xattn"][l]), f"norm_xa{l}")
        qx = _mm(hx, wf["xa_w_q", l], mode="nn", b_l=0, out_dtype=bf16, name=f"xaq{l}")
        ox = _xa_fwd(qx, kv[l], seq, f"xa_fwd{l}")
        cur = _mm(ox, wf["xa_w_o", l], mode="nn", b_l=0, res=cur, out_dtype=f32, name=f"xao{l}")
        sv.update(hx=hx, qx=qx, ox=ox, x2=cur)
        hf = _norm_fwd(cur, row(w["norm_ffn"][l]), f"norm_ffn{l}")
        up = _mm(hf, wf["ffn_w_up", l], mode="nn", b_l=0, out_dtype=bf16, name=f"ffnup{l}")
        act, cv = _ffn_gate_fwd(up, conv_w[l], row(w["ffn_conv_b"][l]), seq, f"ffn_gate{l}")
        cur = _mm(act, wf["ffn_w_down", l], mode="nn", b_l=0, res=cur, out_dtype=f32, name=f"ffndown{l}")
        sv.update(hf=hf, up=up, cv=cv, act=act)
        saved.append(sv)
        xs.append(cur)

    dx, g_final8, loss8 = _loss_head(cur, tgt, row(w["norm_final"]), "loss_head")

    gw = {}
    small = {"norm_final": jnp.sum(g_final8, axis=0)}
    g_mix, g_xa, g_ffn, g_cw, g_cb = [None] * depth, [None] * depth, [None] * depth, [None] * depth, [None] * depth
    dmem_n = None

    pending = []

    def wgrad(key, a, b, l, **kw):
        kw.setdefault("bk", 1024)
        gw[key, l] = _mm(a, b, mode="tn", out_dtype=bf16, out_l=0, out_layers=1, name=f"dw_{key}{l}", **kw)
        pending.append((key, l))

    def reduce_beside():
        if reducer is None or not pending:
            return None, []
        keys = list(pending)
        pending.clear()
        return reducer.begin(keys, gw), keys

    for l in reversed(range(depth)):
        sv = saved[l]
        dact = _mm(dx, wf["ffn_w_down", l], mode="nt", b_l=0, out_dtype=bf16, name=f"d_act{l}")
        wgrad("ffn_w_down", sv["act"], dx, l)
        dup, dcw8, dcb8 = _ffn_gate_bwd(dact, sv["up"], sv["cv"], conv_w[l], seq, f"ffn_gate_bwd{l}")
        g_cw[l], g_cb[l] = jnp.sum(dcw8, axis=1), jnp.sum(dcb8, axis=0)
        wgrad("ffn_w_up", sv["hf"], dup, l)
        dhf = _mm(dup, wf["ffn_w_up", l], mode="nt", b_l=0, out_dtype=f32, name=f"d_hf{l}")
        dx, g8 = _norm_bwd(dhf, sv["x2"], dx, row(w["norm_ffn"][l]), f"norm_ffn_bwd{l}")
        g_ffn[l] = jnp.sum(g8, axis=0)
        dox = _mm(dx, wf["xa_w_o", l], mode="nt", b_l=0, out_dtype=bf16, name=f"d_ox{l}")
        wgrad("xa_w_o", sv["ox"], dx, l)
        dqx, dkv = _xa_bwd(sv["qx"], kv[l], dox, seq, f"xa_bwd{l}")
        wgrad("xa_w_kv", mem_n, dkv, l, bk=mem_n.shape[0])
        dmem_n = _mm(dkv, wf["xa_w_kv", l], mode="nt", b_l=0, res=dmem_n, out_dtype=f32, name=f"d_memn{l}")
        wgrad("xa_w_q", sv["hx"], dqx, l)
        dhx = _mm(dqx, wf["xa_w_q", l], mode="nt", b_l=0, out_dtype=f32, name=f"d_hx{l}")
        dx, g8 = _norm_bwd(dhx, sv["x1"], dx, row(w["norm_xattn"][l]), f"norm_xa_bwd{l}")
        g_xa[l] = jnp.sum(g8, axis=0)
        if l % 2 == 0:
            dmix = _mm(dx, wf["ab_w_out", 0], mode="nt", b_l=0, out_dtype=f32, name=f"d_mix{l}")
            comm, keys = reduce_beside()
            dq, dk, dv, stacks = _sb_bwd(sv["qkv"], sv["ltot"], sv["first"], dmix, seq, f"sb_bwd{l}", comm=comm)
            if comm is not None:
                reducer.end(keys, gw, stacks)
            wgrad("ab_w_out", sv["mix"], dx, 0)
            du, dpw, dps8 = _pool_bwd(dmix, sv["pooled"], w["pool_w"][0], w["pool_scale"], seq, f"pool_bwd{l}")
            small["pool_w"], small["pool_scale"] = dpw[None], jnp.sum(dps8, axis=0)[None]
            dproj = jnp.concatenate([dq, dk, dv, du], axis=1)
            wgrad("ab_w_in", sv["h"], dproj, 0)
            dh = _mm(dproj, wf["ab_w_in", 0], mode="nt", b_l=0, out_dtype=f32, name=f"d_h{l}")
        else:
            dglu = _glu_bwd(dx, sv["glu"], f"glugate_bwd{l}")
            dgl = _mm(dglu, wf["ssm_w_glu", 0], mode="nt", b_l=0, out_dtype=f32, name=f"d_gelu{l}")
            dys = _gelu_bwd(dgl, sv["ys"], f"gelu_bwd{l}")
            comm, keys = reduce_beside()
            dus, db_big, dc_big, dl, dd8, stacks = _ssm_bwd(sv["us"], dys, b_big, c_big, lslab, ssm_d, seq, f"ssm_bwd{l}", comm=comm)
            if comm is not None:
                reducer.end(keys, gw, stacks)
            wgrad("ssm_w_glu", sv["gl"], dglu, 0)
            small["ssm_d"] = jnp.sum(dd8, axis=0)[None]
            half = SSM_PLANES // 2
            g_lr = (dl[:, 0:half] + dl[:, half:SUBLANES]).reshape(gs * ps, 1)
            g_li = (dl[:, SUBLANES + half:] - dl[:, SUBLANES:SUBLANES + half]).reshape(gs * ps, 1)
            g_bbr = _diag_in(db_big[:, :, :SSM_GB * ps], ps, cgrp).reshape(gs * ps, cgrp)
            g_bbi = _diag_in(db_big[:, :, SSM_GB * ps:], ps, cgrp).reshape(gs * ps, cgrp)
            d_a, d_b, d_dt, d_br, d_bi = _ssm_disc_bwd(lam_re, lam_im, dt, b_re, b_im, g_lr, g_li, g_bbr, g_bbi, "ssm_disc_bwd")
            small["ssm_lam_re"], small["ssm_lam_im"] = d_a.reshape(1, gs, ps), d_b.reshape(1, gs, ps)
            small["ssm_log_dt"] = (jnp.sum(d_dt.reshape(gs, ps), axis=1) * dt.reshape(gs, ps)[:, 0])[None]
            small["ssm_b_re"], small["ssm_b_im"] = d_br.reshape(1, gs, ps, cgrp), d_bi.reshape(1, gs, ps, cgrp)
            small["ssm_c_re"] = _diag_out(dc_big[:, :SSM_GB * ps], ps, cgrp)[None]
            small["ssm_c_im"] = -_diag_out(dc_big[:, SSM_GB * ps:], ps, cgrp)[None]
            wgrad("ssm_w_in", sv["h"], dus, 0)
            dh = _mm(dus, wf["ssm_w_in", 0], mode="nt", b_l=0, out_dtype=f32, name=f"d_h{l}")
        dx, g8 = _norm_bwd(dh, xs[l], dx, row(w["norm_mix"][l]), f"norm_mix_bwd{l}")
        g_mix[l] = jnp.sum(g8, axis=0)

    small["norm_mem"] = jnp.sum(_norm_bwd_gain_only(dmem_n, memf, "norm_mem_bwd"), axis=0)
    small["norm_mix"], small["norm_xattn"], small["norm_ffn"] = jnp.stack(g_mix), jnp.stack(g_xa), jnp.stack(g_ffn)
    small["ffn_conv_w"], small["ffn_conv_b"] = jnp.stack(g_cw), jnp.stack(g_cb)
    return loss8, dx, gw, small, pending


def _step(x, mem, loss_target, w, m, v):
    nb, seq, d = x.shape
    t_all = nb * seq
    depth = w["norm_mix"].shape[0]
    chip = 2 * lax.axis_index("x") + lax.axis_index("y")

    small_mine = _pack([w[k] for k in SMALL_SHARDED], SUBLANES)
    gathered = _all_gather([w[k].astype(bf16) for k in FIRST_MIXER] + [small_mine], [BIG_AXIS[k] for k in FIRST_MIXER] + [1],
                           "gather_first")
    wf = {(k, 0): g for k, g in zip(FIRST_MIXER, gathered[:-1])}
    per_chip = gathered[-1].reshape(N_CHIPS, -1)
    pieces = [_unpack(per_chip[q], [w[k].shape for k in SMALL_SHARDED]) for q in range(N_CHIPS)]
    ssm_d = jnp.concatenate([pc[0] for pc in pieces], axis=-1)
    conv_w = jnp.concatenate([pc[1] for pc in pieces], axis=-1)
    ff2 = conv_w.shape[-1]
    late = [(k, l) for k in BIG if k not in FIRST_MIXER for l in range(w[k].shape[0])]
    groups = {"sb_fwd0": [kl for kl in late if kl[1] == 0], "ssm_fwd1": [kl for kl in late if kl[1] > 0]}
    late_weights = {hook: (_gather_plan([w[k][l:l + 1].astype(bf16) for k, l in keys], [BIG_AXIS[k] for k, _ in keys]), keys)
                    for hook, keys in groups.items()}

    reducer = _Reducer()
    loss8, dx, gw, small, pending = _local_step(x.reshape(t_all, d), mem.reshape(-1, d), loss_target.reshape(t_all, d), w, wf,
                                                conv_w, ssm_d, seq, late_weights=late_weights, reducer=reducer)
    loss = lax.psum(0.5 * jnp.sum(loss8) / d, ("x", "y", "c"))

    small_names = SMALL_REPL + SMALL_SHARDED
    small_full_shapes = [w[k].shape for k in SMALL_REPL] + [(1, d), (depth, 3, ff2)]
    gw["small", 0] = _pack([small[k] for k in small_names], 2 * N_CHIPS * SUBLANES)
    keys = pending + [("small", 0)]
    reducer.end(keys, gw, _comm_only(reducer.begin(keys, gw), "rs_owner_last"))
    g_big = {k: jnp.concatenate([reducer.done[k, l] for l in range(w[k].shape[0])], axis=0) for k in BIG}
    small_all = _all_gather([reducer.done["small", 0]], [1], "gather_small_grads")[0]
    g_small = dict(zip(small_names, _unpack(small_all, small_full_shapes)))
    g_small["ssm_d"] = lax.dynamic_slice_in_dim(g_small["ssm_d"], chip * (d // N_CHIPS), d // N_CHIPS, axis=1)
    g_small["ffn_conv_w"] = lax.dynamic_slice_in_dim(g_small["ffn_conv_w"], chip * (ff2 // N_CHIPS), ff2 // N_CHIPS, axis=2)
    grads = {**g_big, **g_small}

    delta, new_m, new_v = {}, {}, {}
    for k in BIG:
        n_cols = w[k].shape[-1]
        two = lambda a: a.reshape(-1, n_cols)
        dl_, m_, v_ = _adamw(two(w[k]), two(grads[k]), two(m[k]), two(v[k]), f"adamw_{k}")
        delta[k], new_m[k], new_v[k] = dl_.reshape(w[k].shape), m_.reshape(w[k].shape), v_.reshape(w[k].shape)
    pk = lambda tree: _pack([tree[k] for k in small_names], 256)[0]
    small_shapes = [w[k].shape for k in small_names]
    outs = _adamw(pk(w), pk(grads), pk(m), pk(v), "adamw_small")
    for tree, buf in zip((delta, new_m, new_v), outs):
        tree.update(zip(small_names, _unpack(buf, small_shapes)))

    grad_x = dx.reshape(nb, seq, d)
    return (loss, grad_x, *[grads[k] for k in WEIGHTS], *[delta[k] for k in WEIGHTS], *[new_m[k] for k in WEIGHTS],
            *[new_v[k] for k in WEIGHTS])


def kernel(x, mem, norm_mix, norm_xattn, norm_ffn, norm_mem, norm_final, ab_w_in, pool_w, pool_scale, ab_w_out, ssm_w_in, ssm_lam_re, ssm_lam_im, ssm_log_dt, ssm_b_re, ssm_b_im, ssm_c_re, ssm_c_im, ssm_d, ssm_w_glu, xa_w_q, xa_w_kv, xa_w_o, ffn_w_up, ffn_conv_w, ffn_conv_b, ffn_w_down, loss_target, m_norm_mix, m_norm_xattn, m_norm_ffn, m_norm_mem, m_norm_final, m_ab_w_in, m_pool_w, m_pool_scale, m_ab_w_out, m_ssm_w_in, m_ssm_lam_re, m_ssm_lam_im, m_ssm_log_dt, m_ssm_b_re, m_ssm_b_im, m_ssm_c_re, m_ssm_c_im, m_ssm_d, m_ssm_w_glu, m_xa_w_q, m_xa_w_kv, m_xa_w_o, m_ffn_w_up, m_ffn_conv_w, m_ffn_conv_b, m_ffn_w_down, v_norm_mix, v_norm_xattn, v_norm_ffn, v_norm_mem, v_norm_final, v_ab_w_in, v_pool_w, v_pool_scale, v_ab_w_out, v_ssm_w_in, v_ssm_lam_re, v_ssm_lam_im, v_ssm_log_dt, v_ssm_b_re, v_ssm_b_im, v_ssm_c_re, v_ssm_c_im, v_ssm_d, v_ssm_w_glu, v_xa_w_q, v_xa_w_kv, v_xa_w_o, v_ffn_w_up, v_ffn_conv_w, v_ffn_conv_b, v_ffn_w_down):
    args = dict(locals())
    w = {k: args[k] for k in WEIGHTS}
    m = {k: args["m_" + k] for k in WEIGHTS}
    v = {k: args["v_" + k] for k in WEIGHTS}
    return _step(x, mem, loss_target, w, m, v)
```

```python
import functools
import math

import jax
import jax.numpy as jnp
from jax import lax
from jax.experimental import pallas as pl
from jax.experimental.pallas import tpu as pltpu

f32 = jnp.float32
bf16 = jnp.bfloat16
SDS = jax.ShapeDtypeStruct
MESH = pl.DeviceIdType.MESH
ANY = pl.BlockSpec(memory_space=pl.ANY)

SB_HEAD_DIM = 64
POOL_WINDOWS = (2, 4, 8, 16)
POOL_GROUP = 128
XA_HEADS = 4
SSM_GROUPS = 64
SSM_GROUP = 16
SSM_STATE = 64
EPS = 1e-6
ADAM_LR, ADAM_B1, ADAM_B2, ADAM_EPS, ADAM_WD, ADAM_STEP = 0.001, 0.9, 0.999, 1e-08, 0.01, 10

LANES = 128
SUBLANES = 8
N_CHIPS = 4
VMEM_LIMIT = 56 * 1024 * 1024

NN = ((1,), (0,))
NT = ((1,), (1,))
TN = ((0,), (0,))


def _dot(a, b, dims):
    return lax.dot_general(a, b, (dims, ((), ())), preferred_element_type=f32)


def _params(n_grid):
    return pltpu.CompilerParams(dimension_semantics=("arbitrary",) * n_grid, vmem_limit_bytes=VMEM_LIMIT)


def _sum8(x):
    r, n = x.shape
    return jnp.sum(x.reshape(r // SUBLANES, SUBLANES, n), axis=0)


def _split_bf16(x):
    hi = x.astype(bf16)
    lo = (x - hi.astype(f32)).astype(bf16)
    return hi, lo


def _sigmoid(x):
    return 1.0 / (1.0 + jnp.exp(-x))


MM_BM = (1024, 1408, 512, 256, 128)
MM_BN = (1536, 1408, 1024, 512, 256, 128)
MM_BK = (2816, 2048, 1024, 512)


def _divisor(n, cands):
    return next((c for c in cands if n % c == 0), n)


def _mm(a, b, *, mode, name, out_dtype, bm=None, bn=None, bk=None, a_l=None, b_l=None, b_n0=0, n=None,
        res=None, out_l=None, out_layers=None, out_prev=None):
    dims = {"nn": NN, "nt": NT, "tn": TN}[mode]
    a2, b2 = a.shape[-2:], b.shape[-2:]
    if mode == "nn":
        (m, k), nfull = a2, b2[1]
    elif mode == "nt":
        (m, k), nfull = a2, b2[0]
    else:
        (k, m), nfull = a2, b2[1]
    n = nfull if n is None else n
    bm = _divisor(m, MM_BM) if bm is None else min(bm, m)
    bn = _divisor(n, MM_BN) if bn is None else min(bn, n)
    if bk is None:
        bk = _divisor(k, (1024, 512)) if mode == "tn" else (k if k <= MM_BK[0] else _divisor(k, MM_BK))
    bk = min(bk, k)
    assert m % bm == 0 and n % bn == 0 and k % bk == 0 and b_n0 % bn == 0, (name, m, n, k, bm, bn, bk)
    nk, n0b = k // bk, b_n0 // bn
    a_bytes, b_bytes = m * k * a.dtype.itemsize, k * n * b.dtype.itemsize
    rows_outer = a_bytes + b_bytes * (m // bm) <= b_bytes + a_bytes * (n // bn)

    def with_layer(layer, blk, idx_fn):
        def idx(g0, g1, kk):
            i, j = (g0, g1) if rows_outer else (g1, g0)
            return idx_fn(i, j, kk) if layer is None else (layer,) + idx_fn(i, j, kk)
        return pl.BlockSpec(blk if layer is None else (None,) + blk, idx)

    if mode == "tn":
        a_spec = with_layer(a_l, (bk, bm), lambda i, j, kk: (kk, i))
    else:
        a_spec = with_layer(a_l, (bm, bk), lambda i, j, kk: (i, kk))
    if mode == "nt":
        b_spec = with_layer(b_l, (bn, bk), lambda i, j, kk: (j, kk))
    else:
        b_spec = with_layer(b_l, (bk, bn), lambda i, j, kk: (kk, j + n0b))
    o_spec = with_layer(out_l, (bm, bn), lambda i, j, kk: (i, j))
    ins, in_specs = [a, b], [a_spec, b_spec]
    if res is not None:
        ins.append(res)
        in_specs.append(with_layer(None, (bm, bn), lambda i, j, kk: (i, j)))
    aliases = {}
    if out_prev is not None:
        aliases = {len(ins): 0}
        ins.append(out_prev)
        in_specs.append(ANY)
    has_res, has_prev = res is not None, out_prev is not None

    def body(*refs):
        a_ref, b_ref = refs[0], refs[1]
        res_ref = refs[2] if has_res else None
        o_ref = refs[2 + has_res + has_prev]
        part = _dot(a_ref[...].astype(bf16), b_ref[...].astype(bf16), dims)

        def finish(r):
            if has_res:
                r = r + res_ref[...]
            o_ref[...] = r.astype(o_ref.dtype)

        if nk == 1:
            finish(part)
        else:
            acc_ref = refs[-1]
            kk = pl.program_id(2)

            @pl.when(kk == 0)
            def _():
                acc_ref[...] = part

            @pl.when(kk > 0)
            def _():
                acc_ref[...] += part

            @pl.when(kk == nk - 1)
            def _():
                finish(acc_ref[...])

    out_shape = SDS((m, n) if out_l is None else (out_layers, m, n), out_dtype)
    grid = (m // bm, n // bn, nk) if rows_outer else (n // bn, m // bm, nk)
    return pl.pallas_call(
        body, out_shape=out_shape, grid=grid, in_specs=in_specs, out_specs=o_spec,
        scratch_shapes=[] if nk == 1 else [pltpu.VMEM((bm, bn), f32)],
        input_output_aliases=aliases, name=name, compiler_params=_params(3))(*ins)


def _rowwise(fn, row_ins, full_ins, row_outs, acc_outs, *, name, br=512):
    t = row_ins[0].shape[0]
    br = next(b for b in (br, 256, 128, 64, 32, 16, 8, t) if b <= t and t % b == 0)
    nr, nf, no = len(row_ins), len(full_ins), len(row_outs)

    def body(*refs):
        rv = [r[...] for r in refs[:nr]]
        fv = [r[...] for r in refs[nr:nr + nf]]
        o_refs = refs[nr + nf:nr + nf + no]
        a_refs = refs[nr + nf + no:]
        outs, accs = fn(rv, fv)
        for o_ref, v in zip(o_refs, outs):
            o_ref[...] = v.astype(o_ref.dtype)
        if a_refs:
            i = pl.program_id(0)

            @pl.when(i == 0)
            def _():
                for a_ref, v in zip(a_refs, accs):
                    a_ref[...] = v

            @pl.when(i > 0)
            def _():
                for a_ref, v in zip(a_refs, accs):
                    a_ref[...] += v

    in_specs = [pl.BlockSpec((br, x.shape[1]), lambda i: (i, 0)) for x in row_ins]
    in_specs += [pl.BlockSpec(x.shape, lambda i, nd=x.ndim: (0,) * nd) for x in full_ins]
    out_specs = [pl.BlockSpec((br, s.shape[1]), lambda i: (i, 0)) for s in row_outs]
    out_specs += [pl.BlockSpec(s.shape, lambda i: (0, 0)) for s in acc_outs]
    res = pl.pallas_call(body, out_shape=tuple(row_outs) + tuple(acc_outs), grid=(t // br,), in_specs=in_specs,
                         out_specs=tuple(out_specs), name=name, compiler_params=_params(1))(*row_ins, *full_ins)
    return res


def _norm_fwd(x, g, name):
    def fn(rv, fv):
        (xv,), (gv,) = rv, fv
        r = lax.rsqrt(jnp.mean(xv * xv, axis=1, keepdims=True) + EPS)
        return [xv * r * gv], []
    return _rowwise(fn, [x], [g], [SDS(x.shape, bf16)], [], name=name)[0]


def _norm_bwd(dh, x, dres, g, name):
    d = x.shape[1]

    def fn(rv, fv):
        (dhv, xv, drv), (gv,) = rv, fv
        r = lax.rsqrt(jnp.mean(xv * xv, axis=1, keepdims=True) + EPS)
        xh = xv * r
        dxh = dhv * gv
        dx = drv + r * (dxh - xh * jnp.mean(dxh * xh, axis=1, keepdims=True))
        return [dx], [_sum8(dhv * xh)]
    return _rowwise(fn, [dh, x, dres], [g], [SDS(x.shape, f32)], [SDS((SUBLANES, d), f32)], name=name)


def _norm_bwd_gain_only(dh, x, name):
    d = x.shape[1]

    def fn(rv, fv):
        dhv, xv = rv
        r = lax.rsqrt(jnp.mean(xv * xv, axis=1, keepdims=True) + EPS)
        return [], [_sum8(dhv * xv * r)]
    return _rowwise(fn, [dh, x], [], [], [SDS((SUBLANES, d), f32)], name=name)[0]


def _loss_head(x, target, g, name):
    d = x.shape[1]

    def fn(rv, fv):
        (xv, tv), (gv,) = rv, fv
        r = lax.rsqrt(jnp.mean(xv * xv, axis=1, keepdims=True) + EPS)
        xh = xv * r
        err = xh * gv - tv
        dy = err * (1.0 / d)
        dxh = dy * gv
        dx = r * (dxh - xh * jnp.mean(dxh * xh, axis=1, keepdims=True))
        return [dx], [_sum8(dy * xh), _sum8(err * err)]
    return _rowwise(fn, [x, target], [g], [SDS(x.shape, f32)], [SDS((SUBLANES, d), f32), SDS((SUBLANES, d), f32)], name=name)


_GELU_C = math.sqrt(2.0 / math.pi)


def _gelu_fwd(y, name):
    def fn(rv, fv):
        (v,) = rv
        t = jnp.tanh(_GELU_C * (v + 0.044715 * v * v * v))
        return [0.5 * v * (1.0 + t)], []
    return _rowwise(fn, [y], [], [SDS(y.shape, bf16)], [], name=name)[0]


def _gelu_bwd(dg, y, name):
    def fn(rv, fv):
        dgv, v = rv
        t = jnp.tanh(_GELU_C * (v + 0.044715 * v * v * v))
        dt = (1.0 - t * t) * _GELU_C * (1.0 + 3.0 * 0.044715 * v * v)
        return [dgv * (0.5 * (1.0 + t) + 0.5 * v * dt)], []
    return _rowwise(fn, [dg, y], [], [SDS(y.shape, f32)], [], name=name)[0]


def _glu_fwd(glu, x, name):
    d = x.shape[1]

    def fn(rv, fv):
        gl, xv = rv
        return [xv + gl[:, :d] * _sigmoid(gl[:, d:])], []
    return _rowwise(fn, [glu, x], [], [SDS(x.shape, f32)], [], name=name)[0]


def _glu_bwd(dx, glu, name):
    d = dx.shape[1]

    def fn(rv, fv):
        dxv, gl = rv
        sg = _sigmoid(gl[:, d:])
        return [jnp.concatenate([dxv * sg, dxv * gl[:, :d] * sg * (1.0 - sg)], axis=1)], []
    return _rowwise(fn, [dx, glu], [], [SDS(glu.shape, bf16)], [], name=name)[0]


def _adamw(w, g, m, v, name):
    c1 = 1.0 - ADAM_B1 ** ADAM_STEP
    c2 = 1.0 - ADAM_B2 ** ADAM_STEP

    def fn(rv, fv):
        wv, gv, mv, vv = rv
        m2 = ADAM_B1 * mv + (1.0 - ADAM_B1) * gv
        v2 = ADAM_B2 * vv + (1.0 - ADAM_B2) * (gv * gv)
        delta = -ADAM_LR * ((m2 / c1) / (jnp.sqrt(v2 / c2) + ADAM_EPS) + ADAM_WD * wv)
        return [delta, m2, v2], []
    s = SDS(w.shape, f32)
    return _rowwise(fn, [w, g, m, v], [], [s, s, s], [], name=name, br=256)


SB_TQ = 128
SB_KB = 4
SB_DEAD = -110.0


def _sb_logits(qh, kb, valid):
    z = _dot(qh, kb, NT) * (SB_HEAD_DIM ** -0.5)
    sp = jnp.log(1.0 + jnp.exp(-jnp.abs(z)))
    lb = jnp.minimum(z, 0.0) - sp
    lk_raw = jnp.minimum(-z, 0.0) - sp
    return lb, lk_raw, jnp.where(valid, lk_raw, 0.0)


def _sb_heads(q, t):
    lane = lax.broadcasted_iota(jnp.int32, (t, LANES), 1)
    masks = [(lane >= hh * SB_HEAD_DIM) & (lane < (hh + 1) * SB_HEAD_DIM) for hh in range(LANES // SB_HEAD_DIM)]
    return [(m, q * jnp.where(m, 1.0, 0.0).astype(bf16)) for m in masks]


def _sb_key_minus_query(t):
    return lax.broadcasted_iota(jnp.int32, (t, t), 1) - lax.broadcasted_iota(jnp.int32, (t, t), 0)


def _tri(t, op):
    row = lax.broadcasted_iota(jnp.int32, (t, t), 0)
    col = lax.broadcasted_iota(jnp.int32, (t, t), 1)
    return jnp.where(op(row, col), 1.0, 0.0).astype(bf16)


def _dot_split(x, u):
    hi, lo = _split_bf16(x)
    return _dot(hi, u, NN) + _dot(lo, u, NN)


def _sb_block(i, g, kk, kbn, t, kmq, k_ref, v_ref):
    j = i - g * kbn - kk
    off = pl.multiple_of(jnp.maximum(j, 0) * t, t)
    limit = jnp.where(j >= 0, (i - j) * t, -2 * t)
    return off, k_ref[pl.ds(off, t), :], v_ref[pl.ds(off, t), :], kmq < limit


def _sb_fwd(qkv, seq, name, comm=None):
    t_all, w3 = qkv.shape
    w = w3 // 3
    hp, tq = w // LANES, SB_TQ
    nb, nq = t_all // seq, seq // tq
    kbn = min(SB_KB, nq)

    def body(q_ref, k_ref, v_ref, o_ref, lt_ref, first_ref):
        i = pl.program_id(2)
        heads = _sb_heads(q_ref[...], tq)
        kmq = _sb_key_minus_query(tq)
        u_after = _tri(tq, lambda r, c: r > c)
        n_it = (i + kbn) // kbn

        def alive(state):
            return (state[0] < n_it) & (state[1] > SB_DEAD)

        def step(state):
            it, carry = state[0], list(state[2:])
            blocks = [_sb_block(i, it, kk, kbn, tq, kmq, k_ref, v_ref)[1:] for kk in range(kbn)]
            chains = [(hh, qh, kb, vb, valid) for kb, vb, valid in blocks for hh, (_, qh) in enumerate(heads)]
            zs = [_dot(qh, kb, NT) for _, qh, kb, _, _ in chains]
            lbs, his, los, sums = [], [], [], []
            for z, (_, _, _, _, valid) in zip(zs, chains):
                z = z * (SB_HEAD_DIM ** -0.5)
                sp = jnp.log(1.0 + jnp.exp(-jnp.abs(z)))
                lb = jnp.minimum(z, 0.0) - sp
                lk = jnp.where(valid, lb - z, 0.0)
                hi, lo = _split_bf16(lk)
                lbs.append(lb), his.append(hi), los.append(lo), sums.append(jnp.sum(lk, axis=1, keepdims=True))
            afts = [_dot(hi, u_after, NN) + _dot(lo, u_after, NN) for hi, lo in zip(his, los)]
            wgts = []
            for (hh, _, _, _, valid), lb, aft, sm in zip(chains, lbs, afts, sums):
                wgts.append(jnp.where(valid, jnp.exp(lb + (carry[2 * hh] + aft)), 0.0).astype(bf16))
                carry[2 * hh] = carry[2 * hh] + sm
            for (hh, _, _, vb, _), wgt in zip(chains, wgts):
                carry[2 * hh + 1] = carry[2 * hh + 1] + _dot(wgt, vb, NN)
            top = jnp.max(carry[0])
            for hh in range(1, len(heads)):
                top = jnp.maximum(top, jnp.max(carry[2 * hh]))
            return (it + 1, top, *carry)

        init = (jnp.int32(0), jnp.float32(0.0)) + (jnp.zeros((tq, 1), f32), jnp.zeros((tq, LANES), f32)) * len(heads)
        fin = lax.while_loop(alive, step, init)
        out = jnp.zeros((tq, LANES), f32)
        ltot = jnp.zeros((tq, LANES), f32)
        for hh, (m, _) in enumerate(heads):
            out = out + jnp.where(m, fin[2 * hh + 3], 0.0)
            ltot = ltot + jnp.where(m, fin[2 * hh + 2], 0.0)
        o_ref[...] = out
        lt_ref[...] = ltot
        first_ref[...] = jnp.zeros((SUBLANES, LANES), f32) + fin[0].astype(f32)

    row_blk = pl.BlockSpec((tq, LANES), lambda b, p, i: (b * nq + i, p))
    (mix, ltot, first), extra = _call(
        body, ins=[qkv, qkv, qkv], out_shape=[SDS((t_all, 2 * w), f32), SDS((t_all, w), f32), SDS((nb * nq * SUBLANES, w), f32)],
        grid=(nb, hp, nq),
        in_specs=[row_blk, pl.BlockSpec((seq, LANES), lambda b, p, i: (b, hp + p)),
                  pl.BlockSpec((seq, LANES), lambda b, p, i: (b, 2 * hp + p))],
        out_specs=[row_blk, row_blk, pl.BlockSpec((SUBLANES, LANES), lambda b, p, i: (b * nq + i, p))],
        scratch_shapes=[], name=name, comm=comm)
    return mix, ltot, first, extra


def _sb_bwd(qkv, ltot, first, dmix, seq, name, comm=None):
    t_all, w3 = qkv.shape
    w = w3 // 3
    hp, tq = w // LANES, SB_TQ
    nb, nq = t_all // seq, seq // tq
    kbn = min(SB_KB, nq)

    def body(q_ref, k_ref, v_ref, lt_ref, first_ref, do_ref, dq_ref, dk_ref, dv_ref, dk_acc, dv_acc):
        i = pl.program_id(2)

        @pl.when(i == 0)
        def _():
            dk_acc[...] = jnp.zeros_like(dk_acc)
            dv_acc[...] = jnp.zeros_like(dv_acc)

        heads = _sb_heads(q_ref[...], tq)
        do = do_ref[...]
        ltv = lt_ref[...]
        dos = [jnp.where(m, do, 0.0).astype(bf16) for m, _ in heads]
        lts = [jnp.sum(jnp.where(m, ltv, 0.0), axis=1, keepdims=True) * (1.0 / SB_HEAD_DIM) for m, _ in heads]
        kmq = _sb_key_minus_query(tq)
        u_incl = _tri(tq, lambda r, c: r <= c)
        u_excl = _tri(tq, lambda r, c: r < c)
        n_it = (i + kbn) // kbn

        walked = jnp.clip(jnp.max(first_ref[...]).astype(jnp.int32), 1, n_it)

        def step(s, carry):
            carry = list(carry)
            blocks = [_sb_block(i, walked - 1 - s, kk, kbn, tq, kmq, k_ref, v_ref) for kk in reversed(range(kbn))]
            chains = [(hh, qh, kb, vb, valid) for _, kb, vb, valid in blocks for hh, (_, qh) in enumerate(heads)]
            zs = [_dot(qh, kb, NT) for _, qh, kb, _, _ in chains]
            dws = [_dot(dos[hh], vb, NT) for hh, _, _, vb, _ in chains]
            lbs, lkrs, his, los, sums = [], [], [], [], []
            for z, (_, _, _, _, valid) in zip(zs, chains):
                z = z * (SB_HEAD_DIM ** -0.5)
                sp = jnp.log(1.0 + jnp.exp(-jnp.abs(z)))
                lb = jnp.minimum(z, 0.0) - sp
                lk_raw = lb - z
                lk = jnp.where(valid, lk_raw, 0.0)
                hi, lo = _split_bf16(lk)
                lbs.append(lb), lkrs.append(lk_raw), his.append(hi), los.append(lo)
                sums.append(jnp.sum(lk, axis=1, keepdims=True))
            pins = [_dot(hi, u_incl, NN) + _dot(lo, u_incl, NN) for hi, lo in zip(his, los)]
            wbs, gs, ghis, glos, gpres = [], [], [], [], []
            for (hh, _, _, _, valid), lb, pin, sm, dw in zip(chains, lbs, pins, sums, dws):
                wgt = jnp.where(valid, jnp.exp(lb + (lts[hh] - (carry[3 * hh] + pin))), 0.0)
                carry[3 * hh] = carry[3 * hh] + sm
                g = dw * wgt
                hi, lo = _split_bf16(g)
                wbs.append(wgt.astype(bf16)), gs.append(g), ghis.append(hi), glos.append(lo)
                gpres.append(carry[3 * hh + 1])
                carry[3 * hh + 1] = carry[3 * hh + 1] + jnp.sum(g, axis=1, keepdims=True)
            gins = [_dot(hi, u_excl, NN) + _dot(lo, u_excl, NN) for hi, lo in zip(ghis, glos)]
            dzbs = []
            for (_, _, _, _, valid), lb, lk_raw, g, gpre, gin in zip(chains, lbs, lkrs, gs, gpres, gins):
                dz = jnp.where(valid, g * jnp.exp(lk_raw) - (gpre + gin) * jnp.exp(lb), 0.0) * (SB_HEAD_DIM ** -0.5)
                dzbs.append(dz.astype(bf16))
            for (hh, _, kb, _, _), dzb in zip(chains, dzbs):
                carry[3 * hh + 2] = carry[3 * hh + 2] + _dot(dzb, kb, NN)
            nh = len(heads)
            for bi, (off, _, _, _) in enumerate(blocks):
                dk_j = jnp.zeros((tq, LANES), f32)
                dv_j = jnp.zeros((tq, LANES), f32)
                for hh, (_, qh) in enumerate(heads):
                    dk_j = dk_j + _dot(dzbs[bi * nh + hh], qh, TN)
                    dv_j = dv_j + _dot(wbs[bi * nh + hh], dos[hh], TN)
                dk_acc[pl.ds(off, tq), :] += dk_j
                dv_acc[pl.ds(off, tq), :] += dv_j
            return tuple(carry)

        zero1 = jnp.zeros((tq, 1), f32)
        fin = lax.fori_loop(0, walked, step, (zero1, zero1, jnp.zeros((tq, LANES), f32)) * len(heads))
        dq_all = jnp.zeros((tq, LANES), f32)
        for hh, (m, _) in enumerate(heads):
            dq_all = dq_all + jnp.where(m, fin[3 * hh + 2], 0.0)
        dq_ref[...] = dq_all.astype(bf16)

        @pl.when(i == nq - 1)
        def _():
            dk_ref[...] = dk_acc[...].astype(bf16)
            dv_ref[...] = dv_acc[...].astype(bf16)

    row_blk = pl.BlockSpec((tq, LANES), lambda b, p, i: (b * nq + i, p))
    seq_blk = pl.BlockSpec((seq, LANES), lambda b, p, i: (b, p))
    out = SDS((t_all, w), bf16)
    (dq, dk, dv), extra = _call(
        body, ins=[qkv, qkv, qkv, ltot, first, dmix], out_shape=[out, out, out], grid=(nb, hp, nq),
        in_specs=[row_blk,
                  pl.BlockSpec((seq, LANES), lambda b, p, i: (b, hp + p)),
                  pl.BlockSpec((seq, LANES), lambda b, p, i: (b, 2 * hp + p)),
                  row_blk, pl.BlockSpec((SUBLANES, LANES), lambda b, p, i: (b * nq + i, p)), row_blk],
        out_specs=[row_blk, seq_blk, seq_blk],
        scratch_shapes=[pltpu.VMEM((seq, LANES), f32), pltpu.VMEM((seq, LANES), f32)], name=name, comm=comm)
    return dq, dk, dv, extra


POOL_CHUNK = 256
POOL_HALO = 16


def _band(rows, cols, lo, hi):
    r = lax.broadcasted_iota(jnp.int32, (rows, cols), 0)
    c = lax.broadcasted_iota(jnp.int32, (rows, cols), 1)
    d = c - r
    return jnp.where((d >= lo) & (d < hi), 1.0, 0.0).astype(bf16)


def _pool_counts(r0, rows, win):
    t = lax.broadcasted_iota(jnp.int32, (rows, 1), 0) + r0
    return jnp.minimum(t + 1, win).astype(f32)


def _pool_fwd(u, mix, pool_w, scale, seq, name):
    t_all, w = u.shape
    ng, rc = w // POOL_GROUP, min(POOL_CHUNK, seq)

    def body(u_ref, w_ref, s_ref, mix_in, p_ref, o_ref, pad):
        del mix_in
        pad[0:POOL_HALO, :] = jnp.zeros((POOL_HALO, POOL_GROUP), f32)
        for g in range(ng):
            cols = slice(g * POOL_GROUP, (g + 1) * POOL_GROUP)
            win = POOL_WINDOWS[g]
            pad[POOL_HALO:POOL_HALO + seq, :] = u_ref[:, cols]
            band = _band(rc, rc + POOL_HALO, POOL_HALO - win + 1, POOL_HALO + 1)
            wg = w_ref[g].astype(bf16)
            for r0 in range(0, seq, rc):
                ue = pad[r0:r0 + rc + POOL_HALO, :]
                hi, lo = _split_bf16(ue)
                sm = _dot(band, hi, NN) + _dot(band, lo, NN)
                pch = sm / _pool_counts(r0, rc, win) - ue[POOL_HALO:, :]
                pb = pch.astype(bf16)
                p_ref[r0:r0 + rc, cols] = pb
                o_ref[r0:r0 + rc, cols] = _dot(pb, wg, NN) * s_ref[:, cols]

    return pl.pallas_call(
        body, out_shape=(SDS((t_all, w), bf16), SDS(mix.shape, f32)), grid=(t_all // seq,),
        in_specs=[pl.BlockSpec((seq, w), lambda b: (b, 0)), pl.BlockSpec(pool_w.shape, lambda b: (0, 0, 0)),
                  pl.BlockSpec(scale.shape, lambda b: (0, 0)), ANY],
        out_specs=(pl.BlockSpec((seq, w), lambda b: (b, 0)), pl.BlockSpec((seq, w), lambda b: (b, 1))),
        scratch_shapes=[pltpu.VMEM((seq + POOL_HALO, POOL_GROUP), f32)],
        input_output_aliases={3: 1}, name=name, compiler_params=_params(1))(u, pool_w, scale, mix)


def _pool_bwd(dmix, p, pool_w, scale, seq, name):
    t_all, w = p.shape
    ng, rc = w // POOL_GROUP, min(POOL_CHUNK, seq)

    def body(dy_ref, p_ref, w_ref, s_ref, du_ref, dw_ref, ds_ref, dpn, dpr):
        b = pl.program_id(0)

        @pl.when(b == 0)
        def _():
            dw_ref[...] = jnp.zeros_like(dw_ref)
            ds_ref[...] = jnp.zeros_like(ds_ref)

        dpn[seq:seq + POOL_HALO, :] = jnp.zeros((POOL_HALO, POOL_GROUP), f32)
        for g in range(ng):
            cols = slice(g * POOL_GROUP, (g + 1) * POOL_GROUP)
            win = POOL_WINDOWS[g]
            wg = w_ref[g].astype(bf16)
            sg = s_ref[:, cols]
            dwg = jnp.zeros((POOL_GROUP, POOL_GROUP), f32)
            dsg = jnp.zeros((SUBLANES, POOL_GROUP), f32)
            for r0 in range(0, seq, rc):
                dy = dy_ref[r0:r0 + rc, cols]
                pb = p_ref[r0:r0 + rc, cols]
                dsg = dsg + _sum8(dy * _dot(pb, wg, NN))
                dyw = (dy * sg).astype(bf16)
                dwg = dwg + _dot(pb, dyw, TN)
                dp = _dot(dyw, wg, NT)
                dpr[r0:r0 + rc, :] = dp
                dpn[r0:r0 + rc, :] = dp / _pool_counts(r0, rc, win)
            dw_ref[g] += dwg
            ds_ref[:, cols] += dsg
            band = _band(rc, rc + POOL_HALO, 0, win)
            for r0 in range(0, seq, rc):
                hi, lo = _split_bf16(dpn[r0:r0 + rc + POOL_HALO, :])
                du = _dot(band, hi, NN) + _dot(band, lo, NN) - dpr[r0:r0 + rc, :]
                du_ref[r0:r0 + rc, cols] = du.astype(bf16)

    return pl.pallas_call(
        body, out_shape=(SDS((t_all, w), bf16), SDS(pool_w.shape, f32), SDS((SUBLANES, w), f32)), grid=(t_all // seq,),
        in_specs=[pl.BlockSpec((seq, w), lambda b: (b, 1)), pl.BlockSpec((seq, w), lambda b: (b, 0)),
                  pl.BlockSpec(pool_w.shape, lambda b: (0, 0, 0)), pl.BlockSpec(scale.shape, lambda b: (0, 0))],
        out_specs=(pl.BlockSpec((seq, w), lambda b: (b, 0)), pl.BlockSpec(pool_w.shape, lambda b: (0, 0, 0)),
                   pl.BlockSpec((SUBLANES, w), lambda b: (0, 0))),
        scratch_shapes=[pltpu.VMEM((seq + POOL_HALO, POOL_GROUP), f32), pltpu.VMEM((seq, POOL_GROUP), f32)],
        name=name, compiler_params=_params(1))(dmix, p, pool_w, scale)


XA_TQ = 256


def _xa_probs(qh, kh, dh):
    s = _dot(qh, kh, NT) * (dh ** -0.5)
    e = jnp.exp(s - jnp.max(s, axis=1, keepdims=True))
    return e / jnp.sum(e, axis=1, keepdims=True)


def _xa_fwd(q, kv, seq, name):
    t_all, d = q.shape
    nb = t_all // seq
    mem, dh, tq = kv.shape[0] // nb, d // XA_HEADS, min(XA_TQ, seq)
    nq = seq // tq

    def body(q_ref, kv_ref, o_ref):
        for h in range(XA_HEADS):
            cols = slice(h * dh, (h + 1) * dh)
            p = _xa_probs(q_ref[:, cols], kv_ref[:, cols], dh)
            o_ref[:, cols] = _dot(p.astype(bf16), kv_ref[:, d + h * dh:d + (h + 1) * dh], NN).astype(bf16)

    return pl.pallas_call(
        body, out_shape=SDS((t_all, d), bf16), grid=(nb, nq),
        in_specs=[pl.BlockSpec((tq, d), lambda b, i: (b * nq + i, 0)), pl.BlockSpec((mem, 2 * d), lambda b, i: (b, 0))],
        out_specs=pl.BlockSpec((tq, d), lambda b, i: (b * nq + i, 0)), name=name, compiler_params=_params(2))(q, kv)


def _xa_bwd(q, kv, do, seq, name):
    t_all, d = q.shape
    nb = t_all // seq
    mem, dh, tq = kv.shape[0] // nb, d // XA_HEADS, min(XA_TQ, seq)
    nq = seq // tq

    def body(q_ref, kv_ref, do_ref, dq_ref, dkv_ref):
        i = pl.program_id(1)

        @pl.when(i == 0)
        def _():
            dkv_ref[...] = jnp.zeros_like(dkv_ref)

        for h in range(XA_HEADS):
            cols = slice(h * dh, (h + 1) * dh)
            vcols = slice(d + h * dh, d + (h + 1) * dh)
            qh, kh, doh = q_ref[:, cols], kv_ref[:, cols], do_ref[:, cols]
            p = _xa_probs(qh, kh, dh)
            dkv_ref[:, vcols] += _dot(p.astype(bf16), doh, TN)
            dp = _dot(doh, kv_ref[:, vcols], NT)
            ds = (p * (dp - jnp.sum(dp * p, axis=1, keepdims=True)) * (dh ** -0.5)).astype(bf16)
            dq_ref[:, cols] = _dot(ds, kh, NN).astype(bf16)
            dkv_ref[:, cols] += _dot(ds, qh, TN)

    row = pl.BlockSpec((tq, d), lambda b, i: (b * nq + i, 0))
    kvs = pl.BlockSpec((mem, 2 * d), lambda b, i: (b, 0))
    return pl.pallas_call(body, out_shape=(SDS((t_all, d), bf16), SDS(kv.shape, f32)), grid=(nb, nq),
                          in_specs=[row, kvs, row], out_specs=(row, kvs), name=name, compiler_params=_params(2))(q, kv, do)


FFN_BR = 256
FFN_CHUNK = 256


def _conv3(ext, w_ref, b, cols, lo, rows):
    return (b + w_ref[2:3, cols] * ext[lo:lo + rows, :] + w_ref[1:2, cols] * ext[lo - 1:lo - 1 + rows, :]
            + w_ref[0:1, cols] * ext[lo - 2:lo - 2 + rows, :])


FFN_HALO = 16


def _ffn_gate_fwd(up, cw, cb, seq, name):
    t_all, f2 = up.shape
    ff, br, ch, hl = f2 // 2, min(FFN_BR, seq), FFN_CHUNK, FFN_HALO
    per_seq, hb = seq // br, br // hl

    def body(up_ref, halo_ref, cw_ref, cb_ref, o_ref, cv_ref, ev, eg):
        i = pl.program_id(0)
        keep = jnp.where(i % per_seq == 0, 0.0, 1.0)
        for c0 in range(0, ff, ch):
            convs = []
            for ext, off in ((ev, c0), (eg, ff + c0)):
                cols = slice(off, off + ch)
                ext[0:hl, :] = halo_ref[:, cols].astype(f32) * keep
                ext[hl:hl + br, :] = up_ref[:, cols].astype(f32)
                conv = _conv3(ext, cw_ref, cb_ref[:, cols], cols, hl, br)
                cv_ref[:, cols] = conv.astype(bf16)
                convs.append(conv)
            val, gate = convs
            o_ref[:, c0:c0 + ch] = (gate * _sigmoid(gate) * val).astype(bf16)

    return pl.pallas_call(
        body, out_shape=(SDS((t_all, ff), bf16), SDS((t_all, f2), bf16)), grid=(t_all // br,),
        in_specs=[pl.BlockSpec((br, f2), lambda i: (i, 0)),
                  pl.BlockSpec((hl, f2), lambda i: (jnp.maximum(i * hb - 1, 0), 0)),
                  pl.BlockSpec(cw.shape, lambda i: (0, 0)), pl.BlockSpec(cb.shape, lambda i: (0, 0))],
        out_specs=(pl.BlockSpec((br, ff), lambda i: (i, 0)), pl.BlockSpec((br, f2), lambda i: (i, 0))),
        scratch_shapes=[pltpu.VMEM((br + hl, ch), f32), pltpu.VMEM((br + hl, ch), f32)],
        name=name, compiler_params=_params(1))(up, up, cw, cb)


def _ffn_gate_bwd(dact, up, cv, cw, seq, name):
    t_all, f2 = up.shape
    ff, br, ch, hl = f2 // 2, min(FFN_BR, seq), FFN_CHUNK, FFN_HALO
    per_seq, hb, last = seq // br, br // hl, t_all // hl - 1
    ext_rows = br + SUBLANES

    def body(da_ref, dan_ref, cv_ref, cvn_ref, up_ref, upp_ref, cw_ref, du_ref, dcw_ref, dcb_ref, ext, e1, e2, e3, dcv, dcg):
        i = pl.program_id(0)

        @pl.when(i == 0)
        def _():
            dcw_ref[...] = jnp.zeros_like(dcw_ref)
            dcb_ref[...] = jnp.zeros_like(dcb_ref)

        keep_prev = jnp.where(i % per_seq == 0, 0.0, 1.0)
        keep_next = jnp.where((i + 1) % per_seq == 0, 0.0, 1.0)

        def with_next(scr, blk_ref, nxt_ref, cols, scale):
            scr[0:br, :] = blk_ref[:, cols].astype(f32)
            scr[br:br + hl, :] = nxt_ref[:, cols].astype(f32) * scale
            return scr[0:ext_rows, :]

        for c0 in range(0, ff, ch):
            da = with_next(e1, da_ref, dan_ref, slice(c0, c0 + ch), keep_next)
            val = with_next(e2, cv_ref, cvn_ref, slice(c0, c0 + ch), 1.0)
            gate = with_next(e3, cv_ref, cvn_ref, slice(ff + c0, ff + c0 + ch), 1.0)
            sg = _sigmoid(gate)
            dcv[...] = da * gate * sg
            dcg[...] = da * val * sg * (1.0 + gate * (1.0 - sg))
            for dc, off in ((dcv, c0), (dcg, ff + c0)):
                cols = slice(off, off + ch)
                du = (cw_ref[2:3, cols] * dc[0:br, :] + cw_ref[1:2, cols] * dc[1:br + 1, :]
                      + cw_ref[0:1, cols] * dc[2:br + 2, :])
                du_ref[:, cols] = du.astype(bf16)
                d0 = dc[0:br, :]
                dcb_ref[:, cols] += _sum8(d0)
                ext[0:hl, :] = upp_ref[:, cols].astype(f32) * keep_prev
                ext[hl:hl + br, :] = up_ref[:, cols].astype(f32)
                for tap in range(3):
                    lo = hl - (2 - tap)
                    dcw_ref[tap, :, cols] += _sum8(d0 * ext[lo:lo + br, :])

    blk = lambda n: pl.BlockSpec((br, n), lambda i: (i, 0))
    prev = lambda n: pl.BlockSpec((hl, n), lambda i: (jnp.maximum(i * hb - 1, 0), 0))
    nxt = lambda n: pl.BlockSpec((hl, n), lambda i: (jnp.minimum((i + 1) * hb, last), 0))
    return pl.pallas_call(
        body, out_shape=(SDS((t_all, f2), bf16), SDS((3, SUBLANES, f2), f32), SDS((SUBLANES, f2), f32)), grid=(t_all // br,),
        in_specs=[blk(ff), nxt(ff), blk(f2), nxt(f2), blk(f2), prev(f2), pl.BlockSpec(cw.shape, lambda i: (0, 0))],
        out_specs=(blk(f2), pl.BlockSpec((3, SUBLANES, f2), lambda i: (0, 0, 0)), pl.BlockSpec((SUBLANES, f2), lambda i: (0, 0))),
        scratch_shapes=[pltpu.VMEM((br + hl, ch), f32)] * 4 + [pltpu.VMEM((ext_rows, ch), f32)] * 2,
        name=name, compiler_params=_params(1))(dact, dact, cv, cv, up, up, cw)


SSM_GB = 8
SSM_PLANES = 8
SSM_ROWS = 256
SSM_UNROLL = 8


def _ssm_pitch(seq):
    p = seq + SUBLANES
    assert (p // SUBLANES) % 2 == 1
    return p


def _rows(base, rc):
    return pl.ds(pl.multiple_of(base + rc * SSM_ROWS, SUBLANES), SSM_ROWS)


def _ssm_project_in(u_ref, b_ref, planes, e, seq, pitch):
    def chunk(rc, _):
        uc = u_ref[_rows(e * seq, rc), :].astype(bf16)
        for j in range(SSM_PLANES):
            planes[_rows(j * pitch, rc), :] = _dot(uc, b_ref[:, j * LANES:(j + 1) * LANES], NN)
        return 0
    lax.fori_loop(0, seq // SSM_ROWS, chunk, 0)


def _ssm_rows(planes, rc, pitch):
    return jnp.concatenate([planes[_rows(j * pitch, rc), :].astype(bf16) for j in range(SSM_PLANES)], axis=1)


def _ssm_scan(planes_list, l1, l2, seq, pitch, reverse=False):
    def step(s, hs):
        hs = list(hs)
        for k in range(SSM_UNROLL):
            t = s * SSM_UNROLL + k
            t = seq - 1 - t if reverse else t
            for e, planes in enumerate(planes_list):
                hs[e] = hs[e] * l1 + pltpu.roll(hs[e], 4, 0) * l2 + planes[pl.ds(t, SUBLANES, stride=pitch), :]
                planes[pl.ds(t, SUBLANES, stride=pitch), :] = hs[e]
        return tuple(hs)
    zero = jnp.zeros((SUBLANES, LANES), f32)
    lax.fori_loop(0, seq // SSM_UNROLL, step, tuple(zero for _ in planes_list))


def _ssm_fwd(u, b_big, c_big, lslab, dskip, seq, name, comm=None):
    t_all, w = u.shape
    nb, gw, pitch = t_all // seq, SSM_GB * SSM_GROUP, _ssm_pitch(seq)
    assert gw == LANES

    def body(u_ref, b_ref, c_ref, l_ref, d_ref, y_ref, *planes):
        l1, l2 = l_ref[0:SUBLANES, :], l_ref[SUBLANES:2 * SUBLANES, :]
        for e in range(nb):
            _ssm_project_in(u_ref, b_ref, planes[e], e, seq, pitch)
        _ssm_scan(planes, l1, l2, seq, pitch)
        for e in range(nb):
            def chunk(rc, _, e=e):
                rows = _rows(e * seq, rc)
                y_ref[rows, :] = _dot(_ssm_rows(planes[e], rc, pitch), c_ref[...], NN) + d_ref[...] * u_ref[rows, :]
                return 0
            lax.fori_loop(0, seq // SSM_ROWS, chunk, 0)

    (y,), extra = _call(
        body, ins=[u, b_big, c_big, lslab, dskip], out_shape=[SDS((t_all, w), f32)], grid=(w // gw,),
        in_specs=[pl.BlockSpec((t_all, gw), lambda k: (0, k)), pl.BlockSpec((None,) + b_big.shape[1:], lambda k: (k, 0, 0)),
                  pl.BlockSpec((None,) + c_big.shape[1:], lambda k: (k, 0, 0)),
                  pl.BlockSpec((None,) + lslab.shape[1:], lambda k: (k, 0, 0)), pl.BlockSpec((1, gw), lambda k: (0, k))],
        out_specs=[pl.BlockSpec((t_all, gw), lambda k: (0, k))],
        scratch_shapes=[pltpu.VMEM((SSM_PLANES * pitch, LANES), f32) for _ in range(nb)], name=name, comm=comm)
    return y, extra


def _ssm_bwd(u, dy, b_big, c_big, lslab, dskip, seq, name, comm=None):
    t_all, w = u.shape
    nb, gw, pitch = t_all // seq, SSM_GB * SSM_GROUP, _ssm_pitch(seq)
    ns = SSM_PLANES * LANES

    def body(u_ref, dy_ref, b_ref, c_ref, l_ref, d_ref, du_ref, db_ref, dc_ref, dl_ref, dd_ref, *planes):
        hp, ap = planes[:nb], planes[nb:]
        l1, l2 = l_ref[0:SUBLANES, :], l_ref[SUBLANES:2 * SUBLANES, :]
        for e in range(nb):
            _ssm_project_in(u_ref, b_ref, hp[e], e, seq, pitch)
        _ssm_scan(hp, l1, l2, seq, pitch)
        dd_ref[...] = jnp.zeros_like(dd_ref)
        dc_ref[...] = jnp.zeros_like(dc_ref)
        db_ref[...] = jnp.zeros_like(db_ref)
        for e in range(nb):
            def chunk(rc, _, e=e):
                rows = _rows(e * seq, rc)
                dyc = dy_ref[rows, :]
                dyb = dyc.astype(bf16)
                for j in range(SSM_PLANES):
                    ap[e][_rows(j * pitch, rc), :] = _dot(dyb, c_ref[j * LANES:(j + 1) * LANES, :], NT)
                dd_ref[...] += _sum8(dyc * u_ref[rows, :])
                dc_ref[...] += _dot(_ssm_rows(hp[e], rc, pitch), dyb, TN)
                return 0
            lax.fori_loop(0, seq // SSM_ROWS, chunk, 0)

        def step(s, carry):
            carry = [list(c) for c in carry]
            for k in range(SSM_UNROLL):
                t = seq - 1 - (s * SSM_UNROLL + k)
                for e in range(nb):
                    a, s1, s2 = carry[e]
                    a = a * l1 - pltpu.roll(a, 4, 0) * l2 + ap[e][pl.ds(t, SUBLANES, stride=pitch), :]
                    ap[e][pl.ds(t, SUBLANES, stride=pitch), :] = a
                    hprev = hp[e][pl.ds(jnp.maximum(t - 1, 0), SUBLANES, stride=pitch), :] * jnp.where(t > 0, 1.0, 0.0)
                    carry[e] = [a, s1 + a * hprev, s2 + a * pltpu.roll(hprev, 4, 0)]
            return tuple(tuple(c) for c in carry)
        zero = jnp.zeros((SUBLANES, LANES), f32)
        fin = lax.fori_loop(0, seq // SSM_UNROLL, step, tuple((zero, zero, zero) for _ in range(nb)))
        dl_ref[0:SUBLANES, :] = sum(f[1] for f in fin)
        dl_ref[SUBLANES:2 * SUBLANES, :] = sum(f[2] for f in fin)

        for e in range(nb):
            def chunk2(rc, _, e=e):
                rows = _rows(e * seq, rc)
                ar = _ssm_rows(ap[e], rc, pitch)
                du_ref[rows, :] = (_dot(ar, b_ref[...], NT) + d_ref[...] * dy_ref[rows, :]).astype(bf16)
                db_ref[...] += _dot(u_ref[rows, :].astype(bf16), ar, TN)
                return 0
            lax.fori_loop(0, seq // SSM_ROWS, chunk2, 0)

    col = pl.BlockSpec((t_all, gw), lambda k: (0, k))
    per = lambda s: pl.BlockSpec((None,) + s[1:], lambda k: (k, 0, 0))
    ng = w // gw
    res, extra = _call(
        body, ins=[u, dy, b_big, c_big, lslab, dskip],
        out_shape=[SDS((t_all, w), bf16), SDS(b_big.shape, f32), SDS(c_big.shape, f32), SDS((ng, 2 * SUBLANES, LANES), f32),
                   SDS((SUBLANES, w), f32)],
        grid=(ng,),
        in_specs=[col, col, per(b_big.shape), per(c_big.shape), per(lslab.shape), pl.BlockSpec((1, gw), lambda k: (0, k))],
        out_specs=[col, per(b_big.shape), per(c_big.shape), per((ng, 2 * SUBLANES, LANES)), pl.BlockSpec((SUBLANES, gw), lambda k: (0, k))],
        scratch_shapes=[pltpu.VMEM((SSM_PLANES * pitch, LANES), f32) for _ in range(2 * nb)], name=name, comm=comm)
    return (*res, extra)


def _ssm_disc_fwd(lam_re, lam_im, dt, b_re, b_im, name):
    def body(a_ref, b_ref, dt_ref, br_ref, bi_ref, lr_ref, li_ref, cr_ref, ci_ref, bbr_ref, bbi_ref):
        a, b, dtv = a_ref[...], b_ref[...], dt_ref[...]
        mag, ang = jnp.exp(a * dtv), b * dtv
        lr, li = mag * jnp.cos(ang), mag * jnp.sin(ang)
        nr, den = lr - 1.0, a * a + b * b
        cr, ci = (nr * a + li * b) / den, (li * a - nr * b) / den
        lr_ref[...], li_ref[...], cr_ref[...], ci_ref[...] = lr, li, cr, ci
        bbr_ref[...] = cr * br_ref[...] - ci * bi_ref[...]
        bbi_ref[...] = cr * bi_ref[...] + ci * br_ref[...]
    c, m = SDS(lam_re.shape, f32), SDS(b_re.shape, f32)
    return pl.pallas_call(body, out_shape=(c, c, c, c, m, m), name=name)(lam_re, lam_im, dt, b_re, b_im)


def _ssm_disc_bwd(lam_re, lam_im, dt, b_re, b_im, g_lr, g_li, g_bbr, g_bbi, name):
    def body(a_ref, b_ref, dt_ref, br_ref, bi_ref, glr_ref, gli_ref, gbr_ref, gbi_ref, da_ref, db_ref, ddt_ref, dbr_ref, dbi_ref):
        a, b, dtv = a_ref[...], b_ref[...], dt_ref[...]
        mag, ang = jnp.exp(a * dtv), b * dtv
        cs, sn = jnp.cos(ang), jnp.sin(ang)
        lr, li = mag * cs, mag * sn
        nr, den = lr - 1.0, a * a + b * b
        cr, ci = (nr * a + li * b) / den, (li * a - nr * b) / den
        gbr, gbi, brv, biv = gbr_ref[...], gbi_ref[...], br_ref[...], bi_ref[...]
        dbr_ref[...] = cr * gbr + ci * gbi
        dbi_ref[...] = cr * gbi - ci * gbr
        dcr = jnp.sum(brv * gbr + biv * gbi, axis=1, keepdims=True)
        dci = jnp.sum(brv * gbi - biv * gbr, axis=1, keepdims=True)
        dnum_r, dnum_i = dcr / den, dci / den
        dden = -(dcr * cr + dci * ci) / den
        dnr = dnum_r * a - dnum_i * b
        dli = gli_ref[...] + dnum_r * b + dnum_i * a
        dlr = glr_ref[...] + dnr
        dmag, dang = dlr * cs + dli * sn, dli * lr - dlr * li
        dadt = dmag * mag
        da_ref[...] = dnum_r * nr + dnum_i * li + dden * 2.0 * a + dadt * dtv
        db_ref[...] = dnum_r * li - dnum_i * nr + dden * 2.0 * b + dang * dtv
        ddt_ref[...] = dadt * a + dang * b
    c, m = SDS(lam_re.shape, f32), SDS(b_re.shape, f32)
    return pl.pallas_call(body, out_shape=(c, c, c, m, m), name=name)(lam_re, lam_im, dt, b_re, b_im, g_lr, g_li, g_bbr, g_bbi)


def _place():
    x, y, c = lax.axis_index("x"), lax.axis_index("y"), lax.axis_index("c")
    return x, y, c, 2 * x + y


def _half_axis(shape, ax):
    return 0 if shape[0] == 2 else (3 - ax)


def _sub(ref, axis, start, size):
    idx = [slice(None)] * len(ref.shape)
    idx[axis] = pl.ds(start, size)
    return ref.at[tuple(idx)]


def _region(ref, full_shape, ax, slot=None, half=None):
    if slot is not None:
        n = full_shape[ax] // N_CHIPS
        ref = _sub(ref, ax, slot * n, n)
    if half is not None:
        ha = _half_axis(full_shape, ax)
        n = full_shape[ha] // 2
        ref = _sub(ref, ha, half * n, n)
    return ref


def _halved(shape, axis):
    return tuple(s // 2 if a == axis else s for a, s in enumerate(shape))


class _Comm:
    def __init__(self, ins, out_shapes, aliases, scratch, start, finish):
        self.ins, self.out_shapes, self.aliases, self.scratch, self.start, self.finish = ins, out_shapes, aliases, scratch, start, finish


def _call(body, *, ins, in_specs, out_shape, out_specs, grid, scratch_shapes, name, comm=None):
    if comm is None:
        res = pl.pallas_call(body, out_shape=tuple(out_shape), grid=grid, in_specs=list(in_specs), out_specs=tuple(out_specs),
                             scratch_shapes=list(scratch_shapes), name=name, compiler_params=_params(len(grid)))(*ins)
        return list(res), []
    n_in, n_out, n_scr, c_in, c_out = len(ins), len(out_shape), len(scratch_shapes), len(comm.ins), len(comm.out_shapes)

    def fused(*refs):
        pos = [n_in, n_in + c_in, n_in + c_in + n_out, n_in + c_in + n_out + c_out, n_in + c_in + n_out + c_out + n_scr]
        in_refs, cin, out_refs, cout, scr, cscr = (refs[:pos[0]], refs[pos[0]:pos[1]], refs[pos[1]:pos[2]], refs[pos[2]:pos[3]],
                                                   refs[pos[3]:pos[4]], refs[pos[4]:])
        ids = [pl.program_id(a) for a in range(len(grid))]
        first, last = ids[0] == 0, ids[0] == grid[0] - 1
        for a in range(1, len(grid)):
            first, last = first & (ids[a] == 0), last & (ids[a] == grid[a] - 1)

        @pl.when(first)
        def _():
            comm.start(cin, cout, cscr)

        body(*in_refs, *out_refs, *scr)

        @pl.when(last)
        def _():
            comm.finish(cin, cout, cscr)

    res = pl.pallas_call(
        fused, out_shape=tuple(out_shape) + tuple(comm.out_shapes), grid=grid, in_specs=list(in_specs) + [ANY] * c_in,
        out_specs=tuple(out_specs) + tuple([ANY] * c_out), scratch_shapes=list(scratch_shapes) + list(comm.scratch),
        input_output_aliases={n_in + i: n_out + o for i, o in comm.aliases}, name=name, compiler_params=_params(len(grid)))(*ins, *comm.ins)
    return list(res[:n_out]), list(res[n_out:])


def _comm_only(comm, name):
    c_in, c_out = len(comm.ins), len(comm.out_shapes)

    def body(*refs):
        cin, cout, cscr = refs[:c_in], refs[c_in:c_in + c_out], refs[c_in + c_out:]
        comm.start(cin, cout, cscr)
        comm.finish(cin, cout, cscr)

    return pl.pallas_call(body, out_shape=tuple(comm.out_shapes), in_specs=[ANY] * c_in, out_specs=tuple([ANY] * c_out),
                          scratch_shapes=list(comm.scratch), input_output_aliases=dict(comm.aliases), name=name)(*comm.ins)


def _gather_plan(shards, axes):
    n = len(shards)
    fulls = [tuple(s * N_CHIPS if a == ax else s for a, s in enumerate(sh.shape)) for sh, ax in zip(shards, axes)]
    own = 6

    def copies(src, dst, scr):
        send_sems, recv_sems = scr
        x, y, c, p = _place()
        chips = [(1 - x, y), (x, 1 - y), (1 - x, 1 - y)]
        slots = [2 * cx + cy for cx, cy in chips]

        def copy(a, k, slot, half, to, from_shard=False):
            where = _region(dst[a], fulls[a], axes[a], slot, half)
            source = where
            if from_shard:
                ha = _half_axis(fulls[a], axes[a])
                hn = fulls[a][ha] // 2
                source = _sub(src[a], ha, half * hn, hn)
            return pltpu.make_async_remote_copy(src_ref=source, dst_ref=where, send_sem=send_sems.at[a, k],
                                                recv_sem=recv_sems.at[a, k], device_id=to, device_id_type=MESH)

        parts = range(n)
        mine = [pltpu.make_async_remote_copy(src_ref=src[a], dst_ref=_region(dst[a], fulls[a], axes[a], p),
                                             send_sem=send_sems.at[a, own], recv_sem=recv_sems.at[a, own],
                                             device_id=(x, y, 1 - c), device_id_type=MESH) for a in parts]
        first = [copy(a, j, p, c, (*chips[j], c), True) for a in parts for j in range(3)]
        landed = [copy(a, j, slots[j], c, (x, y, c)) for a in parts for j in range(3)]
        passed = [copy(a, 3 + j, slots[j], c, (x, y, 1 - c)) for a in parts for j in range(3)]
        handed = [copy(a, 3 + j, slots[j], 1 - c, (x, y, c)) for a in parts for j in range(3)]
        return mine, first, landed, passed, handed

    def start(src, dst, scr):
        mine, first, _, _, _ = copies(src, dst, scr)
        for cp in first + mine:
            cp.start()

    def finish(src, dst, scr):
        mine, first, landed, passed, handed = copies(src, dst, scr)
        for arrived, fwd in zip(landed, passed):
            arrived.wait_recv()
            fwd.start()
        for cp in handed + mine:
            cp.wait_recv()
        for cp in first + passed + mine:
            cp.wait_send()

    return _Comm(list(shards), [SDS(f, s.dtype) for f, s in zip(fulls, shards)], [],
                 [pltpu.SemaphoreType.DMA((n, 7)), pltpu.SemaphoreType.DMA((n, 7))], start, finish)


def _all_gather(shards, axes, name):
    return _comm_only(_gather_plan(shards, axes), name)


def _swap_halves(grads, axes, name):
    n = len(grads)
    shapes = [g.shape for g in grads]

    def body(*refs):
        src, dst = refs[:n], refs[n:2 * n]
        send_sems, recv_sems = refs[2 * n:]
        x, y, c, _ = _place()
        cps = [pltpu.make_async_remote_copy(src_ref=_region(src[a], shapes[a], axes[a], None, 1 - c), dst_ref=dst[a],
                                            send_sem=send_sems.at[a], recv_sem=recv_sems.at[a],
                                            device_id=(x, y, 1 - c), device_id_type=MESH) for a in range(n)]
        for cp in cps:
            cp.start()
        for cp in cps:
            cp.wait()

    outs = tuple(SDS(_halved(s, _half_axis(s, ax)), g.dtype) for s, ax, g in zip(shapes, axes, grads))
    return pl.pallas_call(body, out_shape=outs, in_specs=[ANY] * n, out_specs=tuple([ANY] * n),
                          scratch_shapes=[pltpu.SemaphoreType.DMA((n,)), pltpu.SemaphoreType.DMA((n,))], name=name)(*grads)


def _row_block(rows, row_bytes, limit=3 << 20):
    for b in (1024, 512, 256, 128, 64, 32, 16, 8):
        if rows % b == 0 and b * row_bytes <= limit:
            return b
    return rows


def _add_own_half(g, other, ax, cidx, name):
    _, kp, np_ = other.shape
    ha = _half_axis(g.shape, ax)
    ks, ns = (kp // N_CHIPS, np_) if ax == 1 else (kp, np_ // N_CHIPS)
    bk = _row_block(ks, ns * 4)
    nkb = ks // bk

    def g_map(q, i, cref):
        c = cref[0]
        if ax == 1:
            return (c, q * nkb + i, 0) if ha == 0 else (0, q * nkb + i, c)
        return (c, i, q) if ha == 0 else (0, c * nkb + i, q)

    def o_map(q, i, cref):
        return (0, q * nkb + i, 0) if ax == 1 else (0, i, q)

    def body(c_ref, g_ref, o_ref, send_ref, land_ref):
        del c_ref
        s = (g_ref[...].astype(f32) + o_ref[...].astype(f32)).astype(send_ref.dtype)
        send_ref[...] = s
        land_ref[...] = s

    out = pl.BlockSpec((None, bk, ns), lambda q, i, cref: (q, i, 0))
    grid_spec = pltpu.PrefetchScalarGridSpec(
        num_scalar_prefetch=1, grid=(N_CHIPS, nkb),
        in_specs=[pl.BlockSpec((None, bk, ns), g_map), pl.BlockSpec((None, bk, ns), o_map)], out_specs=(out, out))
    shape = SDS((N_CHIPS, ks, ns), g.dtype)
    return pl.pallas_call(body, out_shape=(shape, shape), grid_spec=grid_spec, name=name, compiler_params=_params(2))(cidx, g, other)


def _owner_plan(sends, lands):
    n = len(sends)

    def copies(cin, dst, scr):
        src = cin[:n]
        send_sems, recv_sems = scr
        x, y, c, p = _place()
        chips = [(1 - x, y), (x, 1 - y), (1 - x, 1 - y)]
        slots = [2 * cx + cy for cx, cy in chips]
        out = [pltpu.make_async_remote_copy(src_ref=src[a].at[slots[j]], dst_ref=dst[a].at[p], send_sem=send_sems.at[a, j],
                                            recv_sem=recv_sems.at[a, j], device_id=(*chips[j], c), device_id_type=MESH)
               for a in range(n) for j in range(3)]
        back = [pltpu.make_async_remote_copy(src_ref=src[a].at[p], dst_ref=dst[a].at[slots[j]], send_sem=send_sems.at[a, j],
                                             recv_sem=recv_sems.at[a, j], device_id=(x, y, c), device_id_type=MESH)
                for a in range(n) for j in range(3)]
        return out, back

    def start(cin, dst, scr):
        for cp in copies(cin, dst, scr)[0]:
            cp.start()

    def finish(cin, dst, scr):
        out, back = copies(cin, dst, scr)
        for cp in back:
            cp.wait_recv()
        for cp in out:
            cp.wait_send()

    return _Comm(list(sends) + list(lands), [SDS(l.shape, l.dtype) for l in lands], [(n + a, a) for a in range(n)],
                 [pltpu.SemaphoreType.DMA((n, 3)), pltpu.SemaphoreType.DMA((n, 3))], start, finish)


def _sum_chips(stack, shard_shape, ax, cidx, name):
    _, ks, ns = stack.shape
    ha = _half_axis(shard_shape, ax)
    bk = _row_block(ks, ns * 4 * N_CHIPS)
    nkb = ks // bk

    def o_map(i, cref):
        c = cref[0]
        return (c, i, 0) if ha == 0 else ((0, c * nkb + i, 0) if ha == 1 else (0, i, c))

    def body(c_ref, s_ref, o_ref):
        del c_ref
        acc = s_ref[0].astype(f32)
        for q in range(1, N_CHIPS):
            acc = acc + s_ref[q].astype(f32)
        o_ref[...] = acc

    grid_spec = pltpu.PrefetchScalarGridSpec(
        num_scalar_prefetch=1, grid=(nkb,), in_specs=[pl.BlockSpec((N_CHIPS, bk, ns), lambda i, cref: (0, i, 0))],
        out_specs=pl.BlockSpec((None, bk, ns), o_map))
    return pl.pallas_call(body, out_shape=SDS(shard_shape, f32), grid_spec=grid_spec, name=name, compiler_params=_params(1))(cidx, stack)


def _join_halves(slices, axes, name):
    n = len(slices)

    def body(*refs):
        dst = refs[n:2 * n]
        send_sems, recv_sems = refs[2 * n:]
        x, y, c, _ = _place()

        def half(a, h):
            ha = _half_axis(slices[a].shape, axes[a])
            hn = slices[a].shape[ha] // 2
            return _sub(dst[a], ha, h * hn, hn)

        cps = [pltpu.make_async_remote_copy(src_ref=half(a, c), dst_ref=half(a, c), send_sem=send_sems.at[a], recv_sem=recv_sems.at[a],
                                            device_id=(x, y, 1 - c), device_id_type=MESH) for a in range(n)]
        for cp in cps:
            cp.start()
        for a in range(n):
            pltpu.make_async_remote_copy(src_ref=half(a, c), dst_ref=half(a, 1 - c), send_sem=send_sems.at[a], recv_sem=recv_sems.at[a],
                                         device_id=(x, y, c), device_id_type=MESH).wait_recv()
        for cp in cps:
            cp.wait_send()

    return pl.pallas_call(
        body, out_shape=tuple(SDS(s.shape, s.dtype) for s in slices), in_specs=[ANY] * n, out_specs=tuple([ANY] * n),
        scratch_shapes=[pltpu.SemaphoreType.DMA((n,)), pltpu.SemaphoreType.DMA((n,))],
        input_output_aliases={a: a for a in range(n)}, name=name)(*slices)


def _core_index():
    return jnp.reshape(lax.axis_index("c"), (1,)).astype(jnp.int32)


def _reduce_begin(grads, axes, tag):
    cidx = _core_index()
    others = _swap_halves(grads, axes, f"rs_swap_{tag}")
    pairs = [_add_own_half(g, o, ax, cidx, f"rs_add_{tag}_{a}") for a, (g, o, ax) in enumerate(zip(grads, others, axes))]
    return _owner_plan([s for s, _ in pairs], [l for _, l in pairs])


def _reduce_end(stacks, shapes, axes, tag):
    cidx = _core_index()
    shard_shapes = [tuple(s // N_CHIPS if i == ax else s for i, s in enumerate(sh)) for sh, ax in zip(shapes, axes)]
    slices = [_sum_chips(s, sh, ax, cidx, f"rs_sum_{tag}_{a}") for a, (s, sh, ax) in enumerate(zip(stacks, shard_shapes, axes))]
    return _join_halves(slices, axes, f"rs_join_{tag}")


def _reduce_scatter(grads, axes, tag):
    stacks = _comm_only(_reduce_begin(grads, axes, tag), f"rs_owner_{tag}")
    return _reduce_end(stacks, [g.shape for g in grads], axes, tag)


SMALL_COLS = 256


def _pack(arrays, rows_multiple):
    flat = jnp.concatenate([a.reshape(-1).astype(f32) for a in arrays])
    rows = -(-flat.shape[0] // SMALL_COLS)
    rows = -(-rows // rows_multiple) * rows_multiple
    flat = jnp.pad(flat, (0, rows * SMALL_COLS - flat.shape[0]))
    return flat.reshape(1, rows, SMALL_COLS)


def _unpack(buf, shapes):
    flat, out, off = buf.reshape(-1), [], 0
    for s in shapes:
        n = math.prod(s)
        out.append(flat[off:off + n].reshape(s))
        off += n
    return out


def _block_diag_in(bb):
    g, p, c = bb.shape
    k = g // SSM_GB
    eye = jnp.eye(SSM_GB, dtype=bb.dtype)
    return jnp.einsum("kgpc,gh->kgchp", bb.reshape(k, SSM_GB, p, c), eye).reshape(k, SSM_GB * c, SSM_GB * p)


def _block_diag_out(cc):
    g, c, p = cc.shape
    k = g // SSM_GB
    eye = jnp.eye(SSM_GB, dtype=cc.dtype)
    return jnp.einsum("kgcp,gh->kgphc", cc.reshape(k, SSM_GB, c, p), eye).reshape(k, SSM_GB * p, SSM_GB * c)


def _diag_in(db, p, c):
    k = db.shape[0]
    return jnp.einsum("kgcgp->kgpc", db.reshape(k, SSM_GB, c, SSM_GB, p)).reshape(k * SSM_GB, p, c)


def _diag_out(dc, p, c):
    k = dc.shape[0]
    return jnp.einsum("kgpgc->kgcp", dc.reshape(k, SSM_GB, p, SSM_GB, c)).reshape(k * SSM_GB, c, p)


def _state_slab(v):
    g, p = v.shape
    return v.reshape(g // SSM_GB, SSM_GB * p // LANES, LANES)


BIG = ("ab_w_in", "ab_w_out", "ssm_w_in", "ssm_w_glu", "xa_w_q", "xa_w_kv", "xa_w_o", "ffn_w_up", "ffn_w_down")
BIG_AXIS = dict(ab_w_in=2, ab_w_out=1, ssm_w_in=1, ssm_w_glu=2, xa_w_q=1, xa_w_kv=2, xa_w_o=1, ffn_w_up=2, ffn_w_down=1)
SMALL_REPL = ("norm_mix", "norm_xattn", "norm_ffn", "norm_mem", "norm_final", "pool_w", "pool_scale", "ssm_lam_re", "ssm_lam_im",
              "ssm_log_dt", "ssm_b_re", "ssm_b_im", "ssm_c_re", "ssm_c_im", "ffn_conv_b")
SMALL_SHARDED = ("ssm_d", "ffn_conv_w")
FIRST_MIXER = ("ab_w_in", "ab_w_out")
WEIGHTS = ("norm_mix", "norm_xattn", "norm_ffn", "norm_mem", "norm_final", "ab_w_in", "pool_w", "pool_scale", "ab_w_out", "ssm_w_in",
           "ssm_lam_re", "ssm_lam_im", "ssm_log_dt", "ssm_b_re", "ssm_b_im", "ssm_c_re", "ssm_c_im", "ssm_d", "ssm_w_glu", "xa_w_q",
           "xa_w_kv", "xa_w_o", "ffn_w_up", "ffn_conv_w", "ffn_conv_b", "ffn_w_down")


class _Reducer:
    def __init__(self):
        self.done, self.groups = {}, 0

    def begin(self, keys, gw):
        self.groups += 1
        return _reduce_begin([gw[k] for k in keys], [BIG_AXIS.get(k[0], 1) for k in keys], f"g{self.groups}")

    def end(self, keys, gw, stacks):
        slices = _reduce_end(stacks, [gw[k].shape for k in keys], [BIG_AXIS.get(k[0], 1) for k in keys], f"g{self.groups}")
        self.done.update(zip(keys, slices))


def _local_step(xf, memf, tgt, w, wf, conv_w, ssm_d, seq, late_weights=None, reducer=None):
    d = xf.shape[1]
    depth = w["norm_mix"].shape[0]
    wf = dict(wf)
    late_weights = late_weights or {}
    sbw = wf["ab_w_in", 0].shape[2] // 4
    row = lambda a: a.reshape(1, -1)

    gs, ps = w["ssm_lam_re"].shape[1:]
    col = lambda a: a.reshape(gs * ps, 1)
    lam_re, lam_im = col(w["ssm_lam_re"][0]), col(w["ssm_lam_im"][0])
    dt = col(jnp.broadcast_to(jnp.exp(w["ssm_log_dt"][0])[:, None], (gs, ps)))
    b_re, b_im = w["ssm_b_re"][0].reshape(gs * ps, -1), w["ssm_b_im"][0].reshape(gs * ps, -1)
    lb_re, lb_im, _, _, bb_re, bb_im = _ssm_disc_fwd(lam_re, lam_im, dt, b_re, b_im, "ssm_disc")
    cgrp = b_re.shape[1]
    b_big = jnp.concatenate([_block_diag_in(bb_re.reshape(gs, ps, cgrp)), _block_diag_in(bb_im.reshape(gs, ps, cgrp))], axis=2).astype(bf16)
    c_big = jnp.concatenate([_block_diag_out(w["ssm_c_re"][0]), -_block_diag_out(w["ssm_c_im"][0])], axis=1).astype(bf16)
    lr_s, li_s = _state_slab(lb_re.reshape(gs, ps)), _state_slab(lb_im.reshape(gs, ps))
    lslab = jnp.concatenate([lr_s, lr_s, -li_s, li_s], axis=1)

    mem_n = _norm_fwd(memf, row(w["norm_mem"]), "norm_mem")
    kv = [None] * depth
    xs, saved = [xf], []
    cur = xf
    for l in range(depth):
        sv = {}
        h = _norm_fwd(cur, row(w["norm_mix"][l]), f"norm_mix{l}")
        sv["h"] = h
        if l % 2 == 0:
            qkv = _mm(h, wf["ab_w_in", 0], mode="nn", b_l=0, n=3 * sbw, out_dtype=bf16, name=f"qkv{l}")
            u = _mm(h, wf["ab_w_in", 0], mode="nn", b_l=0, b_n0=3 * sbw, n=sbw, out_dtype=f32, name=f"poolin{l}")
            plan, names = late_weights.get(f"sb_fwd{l}", (None, ()))
            mix, ltot, first, late = _sb_fwd(qkv, seq, f"sb_fwd{l}", comm=plan)
            wf.update(zip(names, late))
            pooled, mix = _pool_fwd(u, mix, w["pool_w"][0], w["pool_scale"], seq, f"pool_fwd{l}")
            sv.update(qkv=qkv, mix=mix, ltot=ltot, first=first, pooled=pooled)
            cur = _mm(mix, wf["ab_w_out", 0], mode="nn", b_l=0, res=cur, out_dtype=f32, name=f"mixout{l}")
        else:
            us = _mm(h, wf["ssm_w_in", 0], mode="nn", b_l=0, out_dtype=f32, name=f"ssmin{l}")
            plan, names = late_weights.get(f"ssm_fwd{l}", (None, ()))
            ys, late = _ssm_fwd(us, b_big, c_big, lslab, ssm_d, seq, f"ssm_fwd{l}", comm=plan)
            wf.update(zip(names, late))
            gl = _gelu_fwd(ys, f"gelu{l}")
            glu = _mm(gl, wf["ssm_w_glu", 0], mode="nn", b_l=0, out_dtype=f32, name=f"glu{l}")
            sv.update(us=us, ys=ys, gl=gl, glu=glu)
            cur = _glu_fwd(glu, cur, f"glugate{l}")
        sv["x1"] = cur
        kv[l] = _mm(mem_n, wf["xa_w_kv", l], mode="nn", b_l=0, out_dtype=bf16, name=f"kv{l}")
        hx = _norm_fwd(cur, row(w["norm_xattn"][l]), f"norm_xa{l}")
        qx = _mm(hx, wf["xa_w_q", l], mode="nn", b_l=0, out_dtype=bf16, name=f"xaq{l}")
        ox = _xa_fwd(qx, kv[l], seq, f"xa_fwd{l}")
        cur = _mm(ox, wf["xa_w_o", l], mode="nn", b_l=0, res=cur, out_dtype=f32, name=f"xao{l}")
        sv.update(hx=hx, qx=qx, ox=ox, x2=cur)
        hf = _norm_fwd(cur, row(w["norm_ffn"][l]), f"norm_ffn{l}")
        up = _mm(hf, wf["ffn_w_up", l], mode="nn", b_l=0, out_dtype=bf16, name=f"ffnup{l}")
        act, cv = _ffn_gate_fwd(up, conv_w[l], row(w["ffn_conv_b"][l]), seq, f"ffn_gate{l}")
        cur = _mm(act, wf["ffn_w_down", l], mode="nn", b_l=0, res=cur, out_dtype=f32, name=f"ffndown{l}")
        sv.update(hf=hf, up=up, cv=cv, act=act)
        saved.append(sv)
        xs.append(cur)

    dx, g_final8, loss8 = _loss_head(cur, tgt, row(w["norm_final"]), "loss_head")

    gw = {}
    small = {"norm_final": jnp.sum(g_final8, axis=0)}
    g_mix, g_xa, g_ffn, g_cw, g_cb = [None] * depth, [None] * depth, [None] * depth, [None] * depth, [None] * depth
    dmem_n = None

    pending = []

    def wgrad(key, a, b, l, **kw):
        kw.setdefault("bk", 1024)
        gw[key, l] = _mm(a, b, mode="tn", out_dtype=bf16, out_l=0, out_layers=1, name=f"dw_{key}{l}", **kw)
        pending.append((key, l))

    def reduce_beside():
        if reducer is None or not pending:
            return None, []
        keys = list(pending)
        pending.clear()
        return reducer.begin(keys, gw), keys

    for l in reversed(range(depth)):
        sv = saved[l]
        dact = _mm(dx, wf["ffn_w_down", l], mode="nt", b_l=0, out_dtype=bf16, name=f"d_act{l}")
        wgrad("ffn_w_down", sv["act"], dx, l)
        dup, dcw8, dcb8 = _ffn_gate_bwd(dact, sv["up"], sv["cv"], conv_w[l], seq, f"ffn_gate_bwd{l}")
        g_cw[l], g_cb[l] = jnp.sum(dcw8, axis=1), jnp.sum(dcb8, axis=0)
        wgrad("ffn_w_up", sv["hf"], dup, l)
        dhf = _mm(dup, wf["ffn_w_up", l], mode="nt", b_l=0, out_dtype=f32, name=f"d_hf{l}")
        dx, g8 = _norm_bwd(dhf, sv["x2"], dx, row(w["norm_ffn"][l]), f"norm_ffn_bwd{l}")
        g_ffn[l] = jnp.sum(g8, axis=0)
        dox = _mm(dx, wf["xa_w_o", l], mode="nt", b_l=0, out_dtype=bf16, name=f"d_ox{l}")
        wgrad("xa_w_o", sv["ox"], dx, l)
        dqx, dkv = _xa_bwd(sv["qx"], kv[l], dox, seq, f"xa_bwd{l}")
        wgrad("xa_w_kv", mem_n, dkv, l, bk=mem_n.shape[0])
        dmem_n = _mm(dkv, wf["xa_w_kv", l], mode="nt", b_l=0, res=dmem_n, out_dtype=f32, name=f"d_memn{l}")
        wgrad("xa_w_q", sv["hx"], dqx, l)
        dhx = _mm(dqx, wf["xa_w_q", l], mode="nt", b_l=0, out_dtype=f32, name=f"d_hx{l}")
        dx, g8 = _norm_bwd(dhx, sv["x1"], dx, row(w["norm_xattn"][l]), f"norm_xa_bwd{l}")
        g_xa[l] = jnp.sum(g8, axis=0)
        if l % 2 == 0:
            dmix = _mm(dx, wf["ab_w_out", 0], mode="nt", b_l=0, out_dtype=f32, name=f"d_mix{l}")
            comm, keys = reduce_beside()
            dq, dk, dv, stacks = _sb_bwd(sv["qkv"], sv["ltot"], sv["first"], dmix, seq, f"sb_bwd{l}", comm=comm)
            if comm is not None:
                reducer.end(keys, gw, stacks)
            wgrad("ab_w_out", sv["mix"], dx, 0)
            du, dpw, dps8 = _pool_bwd(dmix, sv["pooled"], w["pool_w"][0], w["pool_scale"], seq, f"pool_bwd{l}")
            small["pool_w"], small["pool_scale"] = dpw[None], jnp.sum(dps8, axis=0)[None]
            dproj = jnp.concatenate([dq, dk, dv, du], axis=1)
            wgrad("ab_w_in", sv["h"], dproj, 0)
            dh = _mm(dproj, wf["ab_w_in", 0], mode="nt", b_l=0, out_dtype=f32, name=f"d_h{l}")
        else:
            dglu = _glu_bwd(dx, sv["glu"], f"glugate_bwd{l}")
            dgl = _mm(dglu, wf["ssm_w_glu", 0], mode="nt", b_l=0, out_dtype=f32, name=f"d_gelu{l}")
            dys = _gelu_bwd(dgl, sv["ys"], f"gelu_bwd{l}")
            comm, keys = reduce_beside()
            dus, db_big, dc_big, dl, dd8, stacks = _ssm_bwd(sv["us"], dys, b_big, c_big, lslab, ssm_d, seq, f"ssm_bwd{l}", comm=comm)
            if comm is not None:
                reducer.end(keys, gw, stacks)
            wgrad("ssm_w_glu", sv["gl"], dglu, 0)
            small["ssm_d"] = jnp.sum(dd8, axis=0)[None]
            half = SSM_PLANES // 2
            g_lr = (dl[:, 0:half] + dl[:, half:SUBLANES]).reshape(gs * ps, 1)
            g_li = (dl[:, SUBLANES + half:] - dl[:, SUBLANES:SUBLANES + half]).reshape(gs * ps, 1)
            g_bbr = _diag_in(db_big[:, :, :SSM_GB * ps], ps, cgrp).reshape(gs * ps, cgrp)
            g_bbi = _diag_in(db_big[:, :, SSM_GB * ps:], ps, cgrp).reshape(gs * ps, cgrp)
            d_a, d_b, d_dt, d_br, d_bi = _ssm_disc_bwd(lam_re, lam_im, dt, b_re, b_im, g_lr, g_li, g_bbr, g_bbi, "ssm_disc_bwd")
            small["ssm_lam_re"], small["ssm_lam_im"] = d_a.reshape(1, gs, ps), d_b.reshape(1, gs, ps)
            small["ssm_log_dt"] = (jnp.sum(d_dt.reshape(gs, ps), axis=1) * dt.reshape(gs, ps)[:, 0])[None]
            small["ssm_b_re"], small["ssm_b_im"] = d_br.reshape(1, gs, ps, cgrp), d_bi.reshape(1, gs, ps, cgrp)
            small["ssm_c_re"] = _diag_out(dc_big[:, :SSM_GB * ps], ps, cgrp)[None]
            small["ssm_c_im"] = -_diag_out(dc_big[:, SSM_GB * ps:], ps, cgrp)[None]
            wgrad("ssm_w_in", sv["h"], dus, 0)
            dh = _mm(dus, wf["ssm_w_in", 0], mode="nt", b_l=0, out_dtype=f32, name=f"d_h{l}")
        dx, g8 = _norm_bwd(dh, xs[l], dx, row(w["norm_mix"][l]), f"norm_mix_bwd{l}")
        g_mix[l] = jnp.sum(g8, axis=0)

    small["norm_mem"] = jnp.sum(_norm_bwd_gain_only(dmem_n, memf, "norm_mem_bwd"), axis=0)
    small["norm_mix"], small["norm_xattn"], small["norm_ffn"] = jnp.stack(g_mix), jnp.stack(g_xa), jnp.stack(g_ffn)
    small["ffn_conv_w"], small["ffn_conv_b"] = jnp.stack(g_cw), jnp.stack(g_cb)
    return loss8, dx, gw, small, pending


def _step(x, mem, loss_target, w, m, v):
    nb, seq, d = x.shape
    t_all = nb * seq
    depth = w["norm_mix"].shape[0]
    chip = 2 * lax.axis_index("x") + lax.axis_index("y")

    small_mine = _pack([w[k] for k in SMALL_SHARDED], SUBLANES)
    gathered = _all_gather([w[k].astype(bf16) for k in FIRST_MIXER] + [small_mine], [BIG_AXIS[k] for k in FIRST_MIXER] + [1],
                           "gather_first")
    wf = {(k, 0): g for k, g in zip(FIRST_MIXER, gathered[:-1])}
    per_chip = gathered[-1].reshape(N_CHIPS, -1)
    pieces = [_unpack(per_chip[q], [w[k].shape for k in SMALL_SHARDED]) for q in range(N_CHIPS)]
    ssm_d = jnp.concatenate([pc[0] for pc in pieces], axis=-1)
    conv_w = jnp.concatenate([pc[1] for pc in pieces], axis=-1)
    ff2 = conv_w.shape[-1]
    late = [(k, l) for k in BIG if k not in FIRST_MIXER for l in range(w[k].shape[0])]
    groups = {"sb_fwd0": [kl for kl in late if kl[1] == 0], "ssm_fwd1": [kl for kl in late if kl[1] > 0]}
    late_weights = {hook: (_gather_plan([w[k][l:l + 1].astype(bf16) for k, l in keys], [BIG_AXIS[k] for k, _ in keys]), keys)
                    for hook, keys in groups.items()}

    reducer = _Reducer()
    loss8, dx, gw, small, pending = _local_step(x.reshape(t_all, d), mem.reshape(-1, d), loss_target.reshape(t_all, d), w, wf,
                                                conv_w, ssm_d, seq, late_weights=late_weights, reducer=reducer)
    loss = lax.psum(0.5 * jnp.sum(loss8) / d, ("x", "y", "c"))

    small_names = SMALL_REPL + SMALL_SHARDED
    small_full_shapes = [w[k].shape for k in SMALL_REPL] + [(1, d), (depth, 3, ff2)]
    gw["small", 0] = _pack([small[k] for k in small_names], 2 * N_CHIPS * SUBLANES)
    keys = pending + [("small", 0)]
    reducer.end(keys, gw, _comm_only(reducer.begin(keys, gw), "rs_owner_last"))
    g_big = {k: jnp.concatenate([reducer.done[k, l] for l in range(w[k].shape[0])], axis=0) for k in BIG}
    small_all = _all_gather([reducer.done["small", 0]], [1], "gather_small_grads")[0]
    g_small = dict(zip(small_names, _unpack(small_all, small_full_shapes)))
    g_small["ssm_d"] = lax.dynamic_slice_in_dim(g_small["ssm_d"], chip * (d // N_CHIPS), d // N_CHIPS, axis=1)
    g_small["ffn_conv_w"] = lax.dynamic_slice_in_dim(g_small["ffn_conv_w"], chip * (ff2 // N_CHIPS), ff2 // N_CHIPS, axis=2)
    grads = {**g_big, **g_small}

    delta, new_m, new_v = {}, {}, {}
    for k in BIG:
        n_cols = w[k].shape[-1]
        two = lambda a: a.reshape(-1, n_cols)
        dl_, m_, v_ = _adamw(two(w[k]), two(grads[k]), two(m[k]), two(v[k]), f"adamw_{k}")
        delta[k], new_m[k], new_v[k] = dl_.reshape(w[k].shape), m_.reshape(w[k].shape), v_.reshape(w[k].shape)
    pk = lambda tree: _pack([tree[k] for k in small_names], 256)[0]
    small_shapes = [w[k].shape for k in small_names]
    outs = _adamw(pk(w), pk(grads), pk(m), pk(v), "adamw_small")
    for tree, buf in zip((delta, new_m, new_v), outs):
        tree.update(zip(small_names, _unpack(buf, small_shapes)))

    grad_x = dx.reshape(nb, seq, d)
    return (loss, grad_x, *[grads[k] for k in WEIGHTS], *[delta[k] for k in WEIGHTS], *[new_m[k] for k in WEIGHTS],
            *[new_v[k] for k in WEIGHTS])


def kernel(x, mem, norm_mix, norm_xattn, norm_ffn, norm_mem, norm_final, ab_w_in, pool_w, pool_scale, ab_w_out, ssm_w_in, ssm_lam_re, ssm_lam_im, ssm_log_dt, ssm_b_re, ssm_b_im, ssm_c_re, ssm_c_im, ssm_d, ssm_w_glu, xa_w_q, xa_w_kv, xa_w_o, ffn_w_up, ffn_conv_w, ffn_conv_b, ffn_w_down, loss_target, m_norm_mix, m_norm_xattn, m_norm_ffn, m_norm_mem, m_norm_final, m_ab_w_in, m_pool_w, m_pool_scale, m_ab_w_out, m_ssm_w_in, m_ssm_lam_re, m_ssm_lam_im, m_ssm_log_dt, m_ssm_b_re, m_ssm_b_im, m_ssm_c_re, m_ssm_c_im, m_ssm_d, m_ssm_w_glu, m_xa_w_q, m_xa_w_kv, m_xa_w_o, m_ffn_w_up, m_ffn_conv_w, m_ffn_conv_b, m_ffn_w_down, v_norm_mix, v_norm_xattn, v_norm_ffn, v_norm_mem, v_norm_final, v_ab_w_in, v_pool_w, v_pool_scale, v_ab_w_out, v_ssm_w_in, v_ssm_lam_re, v_ssm_lam_im, v_ssm_log_dt, v_ssm_b_re, v_ssm_b_im, v_ssm_c_re, v_ssm_c_im, v_ssm_d, v_ssm_w_glu, v_xa_w_q, v_xa_w_kv, v_xa_w_o, v_ffn_w_up, v_ffn_conv_w, v_ffn_conv_b, v_ffn_w_down):
    args = dict(locals())
    w = {k: args[k] for k in WEIGHTS}
    m = {k: args["m_" + k] for k in WEIGHTS}
    v = {k: args["v_" + k] for k in WEIGHTS}
    return _step(x, mem, loss_target, w, m, v)
```

```python
import functools
import math

import jax
import jax.numpy as jnp
from jax import lax
from jax.experimental import pallas as pl
from jax.experimental.pallas import tpu as pltpu

f32 = jnp.float32
bf16 = jnp.bfloat16
SDS = jax.ShapeDtypeStruct
MESH = pl.DeviceIdType.MESH
ANY = pl.BlockSpec(memory_space=pl.ANY)

SB_HEAD_DIM = 64
POOL_WINDOWS = (2, 4, 8, 16)
POOL_GROUP = 128
XA_HEADS = 4
SSM_GROUPS = 64
SSM_GROUP = 16
SSM_STATE = 64
EPS = 1e-6
ADAM_LR, ADAM_B1, ADAM_B2, ADAM_EPS, ADAM_WD, ADAM_STEP = 0.001, 0.9, 0.999, 1e-08, 0.01, 10

LANES = 128
SUBLANES = 8
N_CHIPS = 4
VMEM_LIMIT = 56 * 1024 * 1024

NN = ((1,), (0,))
NT = ((1,), (1,))
TN = ((0,), (0,))


def _dot(a, b, dims):
    return lax.dot_general(a, b, (dims, ((), ())), preferred_element_type=f32)


def _params(n_grid):
    return pltpu.CompilerParams(dimension_semantics=("arbitrary",) * n_grid, vmem_limit_bytes=VMEM_LIMIT)


def _sum8(x):
    r, n = x.shape
    return jnp.sum(x.reshape(r // SUBLANES, SUBLANES, n), axis=0)


def _split_bf16(x):
    hi = x.astype(bf16)
    lo = (x - hi.astype(f32)).astype(bf16)
    return hi, lo


def _sigmoid(x):
    return 1.0 / (1.0 + jnp.exp(-x))


MM_BM = (1024, 1408, 512, 256, 128)
MM_BN = (1536, 1408, 1024, 512, 256, 128)
MM_BK = (2816, 2048, 1024, 512)


def _divisor(n, cands):
    return next((c for c in cands if n % c == 0), n)


def _mm(a, b, *, mode, name, out_dtype, bm=None, bn=None, bk=None, a_l=None, b_l=None, b_n0=0, n=None,
        res=None, out_l=None, out_layers=None, out_prev=None, norm=None):
    dims = {"nn": NN, "nt": NT, "tn": TN}[mode]
    a2, b2 = a.shape[-2:], b.shape[-2:]
    if mode == "nn":
        (m, k), nfull = a2, b2[1]
    elif mode == "nt":
        (m, k), nfull = a2, b2[0]
    else:
        (k, m), nfull = a2, b2[1]
    n = nfull if n is None else n
    if bm is None and norm is not None:
        bm = 512
    bm = _divisor(m, MM_BM) if bm is None else min(bm, m)
    bn = _divisor(n, MM_BN) if bn is None else min(bn, n)
    if bk is None:
        bk = _divisor(k, (1024, 512)) if mode == "tn" else (k if k <= MM_BK[0] else _divisor(k, MM_BK))
    bk = min(bk, k)
    assert m % bm == 0 and n % bn == 0 and k % bk == 0 and b_n0 % bn == 0, (name, m, n, k, bm, bn, bk)
    nk, n0b = k // bk, b_n0 // bn
    a_bytes, b_bytes = m * k * a.dtype.itemsize, k * n * b.dtype.itemsize
    rows_outer = a_bytes + b_bytes * (m // bm) <= b_bytes + a_bytes * (n // bn)

    def with_layer(layer, blk, idx_fn):
        def idx(g0, g1, kk):
            i, j = (g0, g1) if rows_outer else (g1, g0)
            return idx_fn(i, j, kk) if layer is None else (layer,) + idx_fn(i, j, kk)
        return pl.BlockSpec(blk if layer is None else (None,) + blk, idx)

    if mode == "tn":
        a_spec = with_layer(a_l, (bk, bm), lambda i, j, kk: (kk, i))
    else:
        a_spec = with_layer(a_l, (bm, bk), lambda i, j, kk: (i, kk))
    if mode == "nt":
        b_spec = with_layer(b_l, (bn, bk), lambda i, j, kk: (j, kk))
    else:
        b_spec = with_layer(b_l, (bk, bn), lambda i, j, kk: (kk, j + n0b))
    o_spec = with_layer(out_l, (bm, bn), lambda i, j, kk: (i, j))
    ins, in_specs = [a, b], [a_spec, b_spec]
    row_blk = with_layer(None, (bm, bn), lambda i, j, kk: (i, j))
    if res is not None:
        ins.append(res)
        in_specs.append(row_blk)
    n_norm_in = 0
    if norm is not None:
        assert bn == n and out_l is None and out_prev is None, name
        extra = list(norm[1:])
        n_norm_in = len(extra)
        ins += extra
        in_specs += [row_blk] * (n_norm_in - 1) + [pl.BlockSpec((1, n), lambda g0, g1, kk: (0, 0))]
    aliases = {}
    if out_prev is not None:
        aliases = {len(ins): 0}
        ins.append(out_prev)
        in_specs.append(ANY)
    has_res, has_prev = res is not None, out_prev is not None

    def body(*refs):
        a_ref, b_ref = refs[0], refs[1]
        res_ref = refs[2] if has_res else None
        norm_refs = refs[2 + has_res:2 + has_res + n_norm_in]
        o_ref = refs[2 + has_res + n_norm_in + has_prev]
        row_block = pl.program_id(0 if rows_outer else 1)
        part = _dot(a_ref[...].astype(bf16), b_ref[...].astype(bf16), dims)

        def finish(r):
            if has_res:
                r = r + res_ref[...]
            if norm is None:
                o_ref[...] = r.astype(o_ref.dtype)
            elif norm[0] == "fwd":
                h_ref = refs[3 + has_res + n_norm_in + has_prev]
                o_ref[...] = r
                rs = lax.rsqrt(jnp.mean(r * r, axis=1, keepdims=True) + EPS)
                h_ref[...] = (r * rs * norm_refs[0][...]).astype(bf16)
            else:
                x_ref, dres_ref, g_ref = norm_refs
                dg_ref = refs[3 + has_res + n_norm_in + has_prev]
                xv = x_ref[...]
                rs = lax.rsqrt(jnp.mean(xv * xv, axis=1, keepdims=True) + EPS)
                xh = xv * rs
                dxh = r * g_ref[...]
                o_ref[...] = dres_ref[...] + rs * (dxh - xh * jnp.mean(dxh * xh, axis=1, keepdims=True))
                dg = _sum8(r * xh)

                @pl.when(row_block == 0)
                def _():
                    dg_ref[...] = dg

                @pl.when(row_block > 0)
                def _():
                    dg_ref[...] += dg

        if nk == 1:
            finish(part)
        else:
            acc_ref = refs[-1]
            kk = pl.program_id(2)

            @pl.when(kk == 0)
            def _():
                acc_ref[...] = part

            @pl.when(kk > 0)
            def _():
                acc_ref[...] += part

            @pl.when(kk == nk - 1)
            def _():
                finish(acc_ref[...])

    out_shape = SDS((m, n) if out_l is None else (out_layers, m, n), out_dtype)
    grid = (m // bm, n // bn, nk) if rows_outer else (n // bn, m // bm, nk)
    if norm is not None:
        if norm[0] == "fwd":
            out_shape, o_spec = (out_shape, SDS((m, n), bf16)), (o_spec, row_blk)
        else:
            out_shape = (out_shape, SDS((SUBLANES, n), f32))
            o_spec = (o_spec, pl.BlockSpec((SUBLANES, n), lambda g0, g1, kk: (0, 0)))
    return pl.pallas_call(
        body, out_shape=out_shape, grid=grid, in_specs=in_specs, out_specs=o_spec,
        scratch_shapes=[] if nk == 1 else [pltpu.VMEM((bm, bn), f32)],
        input_output_aliases=aliases, name=name, compiler_params=_params(3))(*ins)


def _rowwise(fn, row_ins, full_ins, row_outs, acc_outs, *, name, br=512):
    t = row_ins[0].shape[0]
    br = next(b for b in (br, 256, 128, 64, 32, 16, 8, t) if b <= t and t % b == 0)
    nr, nf, no = len(row_ins), len(full_ins), len(row_outs)

    def body(*refs):
        rv = [r[...] for r in refs[:nr]]
        fv = [r[...] for r in refs[nr:nr + nf]]
        o_refs = refs[nr + nf:nr + nf + no]
        a_refs = refs[nr + nf + no:]
        outs, accs = fn(rv, fv)
        for o_ref, v in zip(o_refs, outs):
            o_ref[...] = v.astype(o_ref.dtype)
        if a_refs:
            i = pl.program_id(0)

            @pl.when(i == 0)
            def _():
                for a_ref, v in zip(a_refs, accs):
                    a_ref[...] = v

            @pl.when(i > 0)
            def _():
                for a_ref, v in zip(a_refs, accs):
                    a_ref[...] += v

    in_specs = [pl.BlockSpec((br, x.shape[1]), lambda i: (i, 0)) for x in row_ins]
    in_specs += [pl.BlockSpec(x.shape, lambda i, nd=x.ndim: (0,) * nd) for x in full_ins]
    out_specs = [pl.BlockSpec((br, s.shape[1]), lambda i: (i, 0)) for s in row_outs]
    out_specs += [pl.BlockSpec(s.shape, lambda i: (0, 0)) for s in acc_outs]
    res = pl.pallas_call(body, out_shape=tuple(row_outs) + tuple(acc_outs), grid=(t // br,), in_specs=in_specs,
                         out_specs=tuple(out_specs), name=name, compiler_params=_params(1))(*row_ins, *full_ins)
    return res


def _norm_fwd(x, g, name):
    def fn(rv, fv):
        (xv,), (gv,) = rv, fv
        r = lax.rsqrt(jnp.mean(xv * xv, axis=1, keepdims=True) + EPS)
        return [xv * r * gv], []
    return _rowwise(fn, [x], [g], [SDS(x.shape, bf16)], [], name=name)[0]


def _norm_bwd(dh, x, dres, g, name):
    d = x.shape[1]

    def fn(rv, fv):
        (dhv, xv, drv), (gv,) = rv, fv
        r = lax.rsqrt(jnp.mean(xv * xv, axis=1, keepdims=True) + EPS)
        xh = xv * r
        dxh = dhv * gv
        dx = drv + r * (dxh - xh * jnp.mean(dxh * xh, axis=1, keepdims=True))
        return [dx], [_sum8(dhv * xh)]
    return _rowwise(fn, [dh, x, dres], [g], [SDS(x.shape, f32)], [SDS((SUBLANES, d), f32)], name=name)


def _norm_bwd_gain_only(dh, x, name):
    d = x.shape[1]

    def fn(rv, fv):
        dhv, xv = rv
        r = lax.rsqrt(jnp.mean(xv * xv, axis=1, keepdims=True) + EPS)
        return [], [_sum8(dhv * xv * r)]
    return _rowwise(fn, [dh, x], [], [], [SDS((SUBLANES, d), f32)], name=name)[0]


def _loss_head(x, target, g, name):
    d = x.shape[1]

    def fn(rv, fv):
        (xv, tv), (gv,) = rv, fv
        r = lax.rsqrt(jnp.mean(xv * xv, axis=1, keepdims=True) + EPS)
        xh = xv * r
        err = xh * gv - tv
        dy = err * (1.0 / d)
        dxh = dy * gv
        dx = r * (dxh - xh * jnp.mean(dxh * xh, axis=1, keepdims=True))
        return [dx], [_sum8(dy * xh), _sum8(err * err)]
    return _rowwise(fn, [x, target], [g], [SDS(x.shape, f32)], [SDS((SUBLANES, d), f32), SDS((SUBLANES, d), f32)], name=name)


_GELU_C = math.sqrt(2.0 / math.pi)


def _gelu_fwd(y, name):
    def fn(rv, fv):
        (v,) = rv
        t = jnp.tanh(_GELU_C * (v + 0.044715 * v * v * v))
        return [0.5 * v * (1.0 + t)], []
    return _rowwise(fn, [y], [], [SDS(y.shape, bf16)], [], name=name)[0]


def _gelu_bwd(dg, y, name):
    def fn(rv, fv):
        dgv, v = rv
        t = jnp.tanh(_GELU_C * (v + 0.044715 * v * v * v))
        dt = (1.0 - t * t) * _GELU_C * (1.0 + 3.0 * 0.044715 * v * v)
        return [dgv * (0.5 * (1.0 + t) + 0.5 * v * dt)], []
    return _rowwise(fn, [dg, y], [], [SDS(y.shape, f32)], [], name=name)[0]


def _glu_fwd(glu, x, g, name):
    d = x.shape[1]

    def fn(rv, fv):
        (gl, xv), (gv,) = rv, fv
        y = xv + gl[:, :d] * _sigmoid(gl[:, d:])
        r = lax.rsqrt(jnp.mean(y * y, axis=1, keepdims=True) + EPS)
        return [y, y * r * gv], []
    return _rowwise(fn, [glu, x], [g], [SDS(x.shape, f32), SDS(x.shape, bf16)], [], name=name)


def _glu_bwd(dx, glu, name):
    d = dx.shape[1]

    def fn(rv, fv):
        dxv, gl = rv
        sg = _sigmoid(gl[:, d:])
        return [jnp.concatenate([dxv * sg, dxv * gl[:, :d] * sg * (1.0 - sg)], axis=1)], []
    return _rowwise(fn, [dx, glu], [], [SDS(glu.shape, bf16)], [], name=name)[0]


def _adamw(w, g, m, v, name):
    c1 = 1.0 - ADAM_B1 ** ADAM_STEP
    c2 = 1.0 - ADAM_B2 ** ADAM_STEP

    def fn(rv, fv):
        wv, gv, mv, vv = rv
        m2 = ADAM_B1 * mv + (1.0 - ADAM_B1) * gv
        v2 = ADAM_B2 * vv + (1.0 - ADAM_B2) * (gv * gv)
        delta = -ADAM_LR * ((m2 / c1) / (jnp.sqrt(v2 / c2) + ADAM_EPS) + ADAM_WD * wv)
        return [delta, m2, v2], []
    s = SDS(w.shape, f32)
    return _rowwise(fn, [w, g, m, v], [], [s, s, s], [], name=name, br=256)


SB_TQ = 128
SB_KB = 4
SB_DEAD = -110.0


def _sb_logits(qh, kb, valid):
    z = _dot(qh, kb, NT) * (SB_HEAD_DIM ** -0.5)
    sp = jnp.log(1.0 + jnp.exp(-jnp.abs(z)))
    lb = jnp.minimum(z, 0.0) - sp
    lk_raw = jnp.minimum(-z, 0.0) - sp
    return lb, lk_raw, jnp.where(valid, lk_raw, 0.0)


def _sb_heads(q, t):
    lane = lax.broadcasted_iota(jnp.int32, (t, LANES), 1)
    masks = [(lane >= hh * SB_HEAD_DIM) & (lane < (hh + 1) * SB_HEAD_DIM) for hh in range(LANES // SB_HEAD_DIM)]
    return [(m, q * jnp.where(m, 1.0, 0.0).astype(bf16)) for m in masks]


def _sb_key_minus_query(t):
    return lax.broadcasted_iota(jnp.int32, (t, t), 1) - lax.broadcasted_iota(jnp.int32, (t, t), 0)


def _tri(t, op):
    row = lax.broadcasted_iota(jnp.int32, (t, t), 0)
    col = lax.broadcasted_iota(jnp.int32, (t, t), 1)
    return jnp.where(op(row, col), 1.0, 0.0).astype(bf16)


def _dot_split(x, u):
    hi, lo = _split_bf16(x)
    return _dot(hi, u, NN) + _dot(lo, u, NN)


def _sb_block(i, g, kk, kbn, t, kmq, k_ref, v_ref):
    j = i - g * kbn - kk
    off = pl.multiple_of(jnp.maximum(j, 0) * t, t)
    limit = jnp.where(j >= 0, (i - j) * t, -2 * t)
    return off, k_ref[pl.ds(off, t), :], v_ref[pl.ds(off, t), :], kmq < limit


def _sb_fwd(qkv, seq, name, comm=None):
    t_all, w3 = qkv.shape
    w = w3 // 3
    hp, tq = w // LANES, SB_TQ
    nb, nq = t_all // seq, seq // tq
    kbn = min(SB_KB, nq)

    def body(q_ref, k_ref, v_ref, o_ref, lt_ref, first_ref):
        i = pl.program_id(2)
        heads = _sb_heads(q_ref[...], tq)
        kmq = _sb_key_minus_query(tq)
        u_after = _tri(tq, lambda r, c: r > c)
        n_it = (i + kbn) // kbn

        def alive(state):
            return (state[0] < n_it) & (state[1] > SB_DEAD)

        def step(state):
            it, carry = state[0], list(state[2:])
            blocks = [_sb_block(i, it, kk, kbn, tq, kmq, k_ref, v_ref)[1:] for kk in range(kbn)]
            chains = [(hh, qh, kb, vb, valid) for kb, vb, valid in blocks for hh, (_, qh) in enumerate(heads)]
            zs = [_dot(qh, kb, NT) for _, qh, kb, _, _ in chains]
            lbs, his, los, sums = [], [], [], []
            for z, (_, _, _, _, valid) in zip(zs, chains):
                z = z * (SB_HEAD_DIM ** -0.5)
                sp = jnp.log(1.0 + jnp.exp(-jnp.abs(z)))
                lb = jnp.minimum(z, 0.0) - sp
                lk = jnp.where(valid, lb - z, 0.0)
                hi, lo = _split_bf16(lk)
                lbs.append(lb), his.append(hi), los.append(lo), sums.append(jnp.sum(lk, axis=1, keepdims=True))
            afts = [_dot(hi, u_after, NN) + _dot(lo, u_after, NN) for hi, lo in zip(his, los)]
            wgts = []
            for (hh, _, _, _, valid), lb, aft, sm in zip(chains, lbs, afts, sums):
                wgts.append(jnp.where(valid, jnp.exp(lb + (carry[2 * hh] + aft)), 0.0).astype(bf16))
                carry[2 * hh] = carry[2 * hh] + sm
            for (hh, _, _, vb, _), wgt in zip(chains, wgts):
                carry[2 * hh + 1] = carry[2 * hh + 1] + _dot(wgt, vb, NN)
            top = jnp.max(carry[0])
            for hh in range(1, len(heads)):
                top = jnp.maximum(top, jnp.max(carry[2 * hh]))
            return (it + 1, top, *carry)

        init = (jnp.int32(0), jnp.float32(0.0)) + (jnp.zeros((tq, 1), f32), jnp.zeros((tq, LANES), f32)) * len(heads)
        fin = lax.while_loop(alive, step, init)
        out = jnp.zeros((tq, LANES), f32)
        ltot = jnp.zeros((tq, LANES), f32)
        for hh, (m, _) in enumerate(heads):
            out = out + jnp.where(m, fin[2 * hh + 3], 0.0)
            ltot = ltot + jnp.where(m, fin[2 * hh + 2], 0.0)
        o_ref[...] = out
        lt_ref[...] = ltot
        first_ref[...] = jnp.zeros((SUBLANES, LANES), f32) + fin[0].astype(f32)

    row_blk = pl.BlockSpec((tq, LANES), lambda b, p, i: (b * nq + i, p))
    (mix, ltot, first), extra = _call(
        body, ins=[qkv, qkv, qkv], out_shape=[SDS((t_all, 2 * w), f32), SDS((t_all, w), f32), SDS((nb * nq * SUBLANES, w), f32)],
        grid=(nb, hp, nq),
        in_specs=[row_blk, pl.BlockSpec((seq, LANES), lambda b, p, i: (b, hp + p)),
                  pl.BlockSpec((seq, LANES), lambda b, p, i: (b, 2 * hp + p))],
        out_specs=[row_blk, row_blk, pl.BlockSpec((SUBLANES, LANES), lambda b, p, i: (b * nq + i, p))],
        scratch_shapes=[], name=name, comm=comm)
    return mix, ltot, first, extra


def _sb_bwd(qkv, ltot, first, dmix, seq, name, comm=None):
    t_all, w3 = qkv.shape
    w = w3 // 3
    hp, tq = w // LANES, SB_TQ
    nb, nq = t_all // seq, seq // tq
    kbn = min(SB_KB, nq)

    def body(q_ref, k_ref, v_ref, lt_ref, first_ref, do_ref, dq_ref, dk_ref, dv_ref, dk_acc, dv_acc):
        i = pl.program_id(2)

        @pl.when(i == 0)
        def _():
            dk_acc[...] = jnp.zeros_like(dk_acc)
            dv_acc[...] = jnp.zeros_like(dv_acc)

        heads = _sb_heads(q_ref[...], tq)
        do = do_ref[...]
        ltv = lt_ref[...]
        dos = [jnp.where(m, do, 0.0).astype(bf16) for m, _ in heads]
        lts = [jnp.sum(jnp.where(m, ltv, 0.0), axis=1, keepdims=True) * (1.0 / SB_HEAD_DIM) for m, _ in heads]
        kmq = _sb_key_minus_query(tq)
        u_incl = _tri(tq, lambda r, c: r <= c)
        u_excl = _tri(tq, lambda r, c: r < c)
        n_it = (i + kbn) // kbn

        walked = jnp.clip(jnp.max(first_ref[...]).astype(jnp.int32), 1, n_it)

        def step(s, carry):
            carry = list(carry)
            blocks = [_sb_block(i, walked - 1 - s, kk, kbn, tq, kmq, k_ref, v_ref) for kk in reversed(range(kbn))]
            chains = [(hh, qh, kb, vb, valid) for _, kb, vb, valid in blocks for hh, (_, qh) in enumerate(heads)]
            zs = [_dot(qh, kb, NT) for _, qh, kb, _, _ in chains]
            dws = [_dot(dos[hh], vb, NT) for hh, _, _, vb, _ in chains]
            lbs, lkrs, his, los, sums = [], [], [], [], []
            for z, (_, _, _, _, valid) in zip(zs, chains):
                z = z * (SB_HEAD_DIM ** -0.5)
                sp = jnp.log(1.0 + jnp.exp(-jnp.abs(z)))
                lb = jnp.minimum(z, 0.0) - sp
                lk_raw = lb - z
                lk = jnp.where(valid, lk_raw, 0.0)
                hi, lo = _split_bf16(lk)
                lbs.append(lb), lkrs.append(lk_raw), his.append(hi), los.append(lo)
                sums.append(jnp.sum(lk, axis=1, keepdims=True))
            pins = [_dot(hi, u_incl, NN) + _dot(lo, u_incl, NN) for hi, lo in zip(his, los)]
            wbs, gs, ghis, glos, gpres = [], [], [], [], []
            for (hh, _, _, _, valid), lb, pin, sm, dw in zip(chains, lbs, pins, sums, dws):
                wgt = jnp.where(valid, jnp.exp(lb + (lts[hh] - (carry[3 * hh] + pin))), 0.0)
                carry[3 * hh] = carry[3 * hh] + sm
                g = dw * wgt
                hi, lo = _split_bf16(g)
                wbs.append(wgt.astype(bf16)), gs.append(g), ghis.append(hi), glos.append(lo)
                gpres.append(carry[3 * hh + 1])
                carry[3 * hh + 1] = carry[3 * hh + 1] + jnp.sum(g, axis=1, keepdims=True)
            gins = [_dot(hi, u_excl, NN) + _dot(lo, u_excl, NN) for hi, lo in zip(ghis, glos)]
            dzbs = []
            for (_, _, _, _, valid), lb, lk_raw, g, gpre, gin in zip(chains, lbs, lkrs, gs, gpres, gins):
                dz = jnp.where(valid, g * jnp.exp(lk_raw) - (gpre + gin) * jnp.exp(lb), 0.0) * (SB_HEAD_DIM ** -0.5)
                dzbs.append(dz.astype(bf16))
            for (hh, _, kb, _, _), dzb in zip(chains, dzbs):
                carry[3 * hh + 2] = carry[3 * hh + 2] + _dot(dzb, kb, NN)
            nh = len(heads)
            for bi, (off, _, _, _) in enumerate(blocks):
                dk_j = jnp.zeros((tq, LANES), f32)
                dv_j = jnp.zeros((tq, LANES), f32)
                for hh, (_, qh) in enumerate(heads):
                    dk_j = dk_j + _dot(dzbs[bi * nh + hh], qh, TN)
                    dv_j = dv_j + _dot(wbs[bi * nh + hh], dos[hh], TN)
                dk_acc[pl.ds(off, tq), :] += dk_j
                dv_acc[pl.ds(off, tq), :] += dv_j
            return tuple(carry)

        zero1 = jnp.zeros((tq, 1), f32)
        fin = lax.fori_loop(0, walked, step, (zero1, zero1, jnp.zeros((tq, LANES), f32)) * len(heads))
        dq_all = jnp.zeros((tq, LANES), f32)
        for hh, (m, _) in enumerate(heads):
            dq_all = dq_all + jnp.where(m, fin[3 * hh + 2], 0.0)
        dq_ref[...] = dq_all.astype(bf16)

        @pl.when(i == nq - 1)
        def _():
            dk_ref[...] = dk_acc[...].astype(bf16)
            dv_ref[...] = dv_acc[...].astype(bf16)

    row_blk = pl.BlockSpec((tq, LANES), lambda b, p, i: (b * nq + i, p))
    seq_blk = pl.BlockSpec((seq, LANES), lambda b, p, i: (b, p))
    out = SDS((t_all, w), bf16)
    (dq, dk, dv), extra = _call(
        body, ins=[qkv, qkv, qkv, ltot, first, dmix], out_shape=[out, out, out], grid=(nb, hp, nq),
        in_specs=[row_blk,
                  pl.BlockSpec((seq, LANES), lambda b, p, i: (b, hp + p)),
                  pl.BlockSpec((seq, LANES), lambda b, p, i: (b, 2 * hp + p)),
                  row_blk, pl.BlockSpec((SUBLANES, LANES), lambda b, p, i: (b * nq + i, p)), row_blk],
        out_specs=[row_blk, seq_blk, seq_blk],
        scratch_shapes=[pltpu.VMEM((seq, LANES), f32), pltpu.VMEM((seq, LANES), f32)], name=name, comm=comm)
    return dq, dk, dv, extra


POOL_CHUNK = 256
POOL_HALO = 16


def _band(rows, cols, lo, hi):
    r = lax.broadcasted_iota(jnp.int32, (rows, cols), 0)
    c = lax.broadcasted_iota(jnp.int32, (rows, cols), 1)
    d = c - r
    return jnp.where((d >= lo) & (d < hi), 1.0, 0.0).astype(bf16)


def _pool_counts(r0, rows, win):
    t = lax.broadcasted_iota(jnp.int32, (rows, 1), 0) + r0
    return jnp.minimum(t + 1, win).astype(f32)


def _pool_fwd(u, mix, pool_w, scale, seq, name):
    t_all, w = u.shape
    ng, rc = w // POOL_GROUP, min(POOL_CHUNK, seq)

    def body(u_ref, w_ref, s_ref, mix_in, p_ref, o_ref, pad):
        del mix_in
        pad[0:POOL_HALO, :] = jnp.zeros((POOL_HALO, POOL_GROUP), f32)
        for g in range(ng):
            cols = slice(g * POOL_GROUP, (g + 1) * POOL_GROUP)
            win = POOL_WINDOWS[g]
            pad[POOL_HALO:POOL_HALO + seq, :] = u_ref[:, cols]
            band = _band(rc, rc + POOL_HALO, POOL_HALO - win + 1, POOL_HALO + 1)
            wg = w_ref[g].astype(bf16)
            for r0 in range(0, seq, rc):
                ue = pad[r0:r0 + rc + POOL_HALO, :]
                hi, lo = _split_bf16(ue)
                sm = _dot(band, hi, NN) + _dot(band, lo, NN)
                pch = sm / _pool_counts(r0, rc, win) - ue[POOL_HALO:, :]
                pb = pch.astype(bf16)
                p_ref[r0:r0 + rc, cols] = pb
                o_ref[r0:r0 + rc, cols] = _dot(pb, wg, NN) * s_ref[:, cols]

    return pl.pallas_call(
        body, out_shape=(SDS((t_all, w), bf16), SDS(mix.shape, f32)), grid=(t_all // seq,),
        in_specs=[pl.BlockSpec((seq, w), lambda b: (b, 0)), pl.BlockSpec(pool_w.shape, lambda b: (0, 0, 0)),
                  pl.BlockSpec(scale.shape, lambda b: (0, 0)), ANY],
        out_specs=(pl.BlockSpec((seq, w), lambda b: (b, 0)), pl.BlockSpec((seq, w), lambda b: (b, 1))),
        scratch_shapes=[pltpu.VMEM((seq + POOL_HALO, POOL_GROUP), f32)],
        input_output_aliases={3: 1}, name=name, compiler_params=_params(1))(u, pool_w, scale, mix)


def _pool_bwd(dmix, p, pool_w, scale, seq, name):
    t_all, w = p.shape
    ng, rc = w // POOL_GROUP, min(POOL_CHUNK, seq)

    def body(dy_ref, p_ref, w_ref, s_ref, du_ref, dw_ref, ds_ref, dpn, dpr):
        b = pl.program_id(0)

        @pl.when(b == 0)
        def _():
            dw_ref[...] = jnp.zeros_like(dw_ref)
            ds_ref[...] = jnp.zeros_like(ds_ref)

        dpn[seq:seq + POOL_HALO, :] = jnp.zeros((POOL_HALO, POOL_GROUP), f32)
        for g in range(ng):
            cols = slice(g * POOL_GROUP, (g + 1) * POOL_GROUP)
            win = POOL_WINDOWS[g]
            wg = w_ref[g].astype(bf16)
            sg = s_ref[:, cols]
            dwg = jnp.zeros((POOL_GROUP, POOL_GROUP), f32)
            dsg = jnp.zeros((SUBLANES, POOL_GROUP), f32)
            for r0 in range(0, seq, rc):
                dy = dy_ref[r0:r0 + rc, cols]
                pb = p_ref[r0:r0 + rc, cols]
                dsg = dsg + _sum8(dy * _dot(pb, wg, NN))
                dyw = (dy * sg).astype(bf16)
                dwg = dwg + _dot(pb, dyw, TN)
                dp = _dot(dyw, wg, NT)
                dpr[r0:r0 + rc, :] = dp
                dpn[r0:r0 + rc, :] = dp / _pool_counts(r0, rc, win)
            dw_ref[g] += dwg
            ds_ref[:, cols] += dsg
            band = _band(rc, rc + POOL_HALO, 0, win)
            for r0 in range(0, seq, rc):
                hi, lo = _split_bf16(dpn[r0:r0 + rc + POOL_HALO, :])
                du = _dot(band, hi, NN) + _dot(band, lo, NN) - dpr[r0:r0 + rc, :]
                du_ref[r0:r0 + rc, cols] = du.astype(bf16)

    return pl.pallas_call(
        body, out_shape=(SDS((t_all, w), bf16), SDS(pool_w.shape, f32), SDS((SUBLANES, w), f32)), grid=(t_all // seq,),
        in_specs=[pl.BlockSpec((seq, w), lambda b: (b, 1)), pl.BlockSpec((seq, w), lambda b: (b, 0)),
                  pl.BlockSpec(pool_w.shape, lambda b: (0, 0, 0)), pl.BlockSpec(scale.shape, lambda b: (0, 0))],
        out_specs=(pl.BlockSpec((seq, w), lambda b: (b, 0)), pl.BlockSpec(pool_w.shape, lambda b: (0, 0, 0)),
                   pl.BlockSpec((SUBLANES, w), lambda b: (0, 0))),
        scratch_shapes=[pltpu.VMEM((seq + POOL_HALO, POOL_GROUP), f32), pltpu.VMEM((seq, POOL_GROUP), f32)],
        name=name, compiler_params=_params(1))(dmix, p, pool_w, scale)


XA_TQ = 256


def _xa_probs(qh, kh, dh):
    s = _dot(qh, kh, NT) * (dh ** -0.5)
    e = jnp.exp(s - jnp.max(s, axis=1, keepdims=True))
    return e / jnp.sum(e, axis=1, keepdims=True)


def _xa_fwd(q, kv, seq, name):
    t_all, d = q.shape
    nb = t_all // seq
    mem, dh, tq = kv.shape[0] // nb, d // XA_HEADS, min(XA_TQ, seq)
    nq = seq // tq

    def body(q_ref, kv_ref, o_ref):
        for h in range(XA_HEADS):
            cols = slice(h * dh, (h + 1) * dh)
            p = _xa_probs(q_ref[:, cols], kv_ref[:, cols], dh)
            o_ref[:, cols] = _dot(p.astype(bf16), kv_ref[:, d + h * dh:d + (h + 1) * dh], NN).astype(bf16)

    return pl.pallas_call(
        body, out_shape=SDS((t_all, d), bf16), grid=(nb, nq),
        in_specs=[pl.BlockSpec((tq, d), lambda b, i: (b * nq + i, 0)), pl.BlockSpec((mem, 2 * d), lambda b, i: (b, 0))],
        out_specs=pl.BlockSpec((tq, d), lambda b, i: (b * nq + i, 0)), name=name, compiler_params=_params(2))(q, kv)


def _xa_bwd(q, kv, do, seq, name):
    t_all, d = q.shape
    nb = t_all // seq
    mem, dh, tq = kv.shape[0] // nb, d // XA_HEADS, min(XA_TQ, seq)
    nq = seq // tq

    def body(q_ref, kv_ref, do_ref, dq_ref, dkv_ref):
        i = pl.program_id(1)

        @pl.when(i == 0)
        def _():
            dkv_ref[...] = jnp.zeros_like(dkv_ref)

        for h in range(XA_HEADS):
            cols = slice(h * dh, (h + 1) * dh)
            vcols = slice(d + h * dh, d + (h + 1) * dh)
            qh, kh, doh = q_ref[:, cols], kv_ref[:, cols], do_ref[:, cols]
            p = _xa_probs(qh, kh, dh)
            dkv_ref[:, vcols] += _dot(p.astype(bf16), doh, TN)
            dp = _dot(doh, kv_ref[:, vcols], NT)
            ds = (p * (dp - jnp.sum(dp * p, axis=1, keepdims=True)) * (dh ** -0.5)).astype(bf16)
            dq_ref[:, cols] = _dot(ds, kh, NN).astype(bf16)
            dkv_ref[:, cols] += _dot(ds, qh, TN)

    row = pl.BlockSpec((tq, d), lambda b, i: (b * nq + i, 0))
    kvs = pl.BlockSpec((mem, 2 * d), lambda b, i: (b, 0))
    return pl.pallas_call(body, out_shape=(SDS((t_all, d), bf16), SDS(kv.shape, f32)), grid=(nb, nq),
                          in_specs=[row, kvs, row], out_specs=(row, kvs), name=name, compiler_params=_params(2))(q, kv, do)


FFN_BR = 256
FFN_CHUNK = 256


def _conv3(ext, w_ref, b, cols, lo, rows):
    return (b + w_ref[2:3, cols] * ext[lo:lo + rows, :] + w_ref[1:2, cols] * ext[lo - 1:lo - 1 + rows, :]
            + w_ref[0:1, cols] * ext[lo - 2:lo - 2 + rows, :])


FFN_HALO = 16


def _ffn_gate_fwd(up, cw, cb, seq, name):
    t_all, f2 = up.shape
    ff, br, ch, hl = f2 // 2, min(FFN_BR, seq), FFN_CHUNK, FFN_HALO
    per_seq, hb = seq // br, br // hl

    def body(up_ref, halo_ref, cw_ref, cb_ref, o_ref, cv_ref, ev, eg):
        i = pl.program_id(0)
        keep = jnp.where(i % per_seq == 0, 0.0, 1.0)
        for c0 in range(0, ff, ch):
            convs = []
            for ext, off in ((ev, c0), (eg, ff + c0)):
                cols = slice(off, off + ch)
                ext[0:hl, :] = halo_ref[:, cols].astype(f32) * keep
                ext[hl:hl + br, :] = up_ref[:, cols].astype(f32)
                conv = _conv3(ext, cw_ref, cb_ref[:, cols], cols, hl, br)
                cv_ref[:, cols] = conv.astype(bf16)
                convs.append(conv)
            val, gate = convs
            o_ref[:, c0:c0 + ch] = (gate * _sigmoid(gate) * val).astype(bf16)

    return pl.pallas_call(
        body, out_shape=(SDS((t_all, ff), bf16), SDS((t_all, f2), bf16)), grid=(t_all // br,),
        in_specs=[pl.BlockSpec((br, f2), lambda i: (i, 0)),
                  pl.BlockSpec((hl, f2), lambda i: (jnp.maximum(i * hb - 1, 0), 0)),
                  pl.BlockSpec(cw.shape, lambda i: (0, 0)), pl.BlockSpec(cb.shape, lambda i: (0, 0))],
        out_specs=(pl.BlockSpec((br, ff), lambda i: (i, 0)), pl.BlockSpec((br, f2), lambda i: (i, 0))),
        scratch_shapes=[pltpu.VMEM((br + hl, ch), f32), pltpu.VMEM((br + hl, ch), f32)],
        name=name, compiler_params=_params(1))(up, up, cw, cb)


def _ffn_gate_bwd(dact, up, cv, cw, seq, name):
    t_all, f2 = up.shape
    ff, br, ch, hl = f2 // 2, min(FFN_BR, seq), FFN_CHUNK, FFN_HALO
    per_seq, hb, last = seq // br, br // hl, t_all // hl - 1
    ext_rows = br + SUBLANES

    def body(da_ref, dan_ref, cv_ref, cvn_ref, up_ref, upp_ref, cw_ref, du_ref, dcw_ref, dcb_ref, ext, e1, e2, e3, dcv, dcg):
        i = pl.program_id(0)

        @pl.when(i == 0)
        def _():
            dcw_ref[...] = jnp.zeros_like(dcw_ref)
            dcb_ref[...] = jnp.zeros_like(dcb_ref)

        keep_prev = jnp.where(i % per_seq == 0, 0.0, 1.0)
        keep_next = jnp.where((i + 1) % per_seq == 0, 0.0, 1.0)

        def with_next(scr, blk_ref, nxt_ref, cols, scale):
            scr[0:br, :] = blk_ref[:, cols].astype(f32)
            scr[br:br + hl, :] = nxt_ref[:, cols].astype(f32) * scale
            return scr[0:ext_rows, :]

        for c0 in range(0, ff, ch):
            da = with_next(e1, da_ref, dan_ref, slice(c0, c0 + ch), keep_next)
            val = with_next(e2, cv_ref, cvn_ref, slice(c0, c0 + ch), 1.0)
            gate = with_next(e3, cv_ref, cvn_ref, slice(ff + c0, ff + c0 + ch), 1.0)
            sg = _sigmoid(gate)
            dcv[...] = da * gate * sg
            dcg[...] = da * val * sg * (1.0 + gate * (1.0 - sg))
            for dc, off in ((dcv, c0), (dcg, ff + c0)):
                cols = slice(off, off + ch)
                du = (cw_ref[2:3, cols] * dc[0:br, :] + cw_ref[1:2, cols] * dc[1:br + 1, :]
                      + cw_ref[0:1, cols] * dc[2:br + 2, :])
                du_ref[:, cols] = du.astype(bf16)
                d0 = dc[0:br, :]
                dcb_ref[:, cols] += _sum8(d0)
                ext[0:hl, :] = upp_ref[:, cols].astype(f32) * keep_prev
                ext[hl:hl + br, :] = up_ref[:, cols].astype(f32)
                for tap in range(3):
                    lo = hl - (2 - tap)
                    dcw_ref[tap, :, cols] += _sum8(d0 * ext[lo:lo + br, :])

    blk = lambda n: pl.BlockSpec((br, n), lambda i: (i, 0))
    prev = lambda n: pl.BlockSpec((hl, n), lambda i: (jnp.maximum(i * hb - 1, 0), 0))
    nxt = lambda n: pl.BlockSpec((hl, n), lambda i: (jnp.minimum((i + 1) * hb, last), 0))
    return pl.pallas_call(
        body, out_shape=(SDS((t_all, f2), bf16), SDS((3, SUBLANES, f2), f32), SDS((SUBLANES, f2), f32)), grid=(t_all // br,),
        in_specs=[blk(ff), nxt(ff), blk(f2), nxt(f2), blk(f2), prev(f2), pl.BlockSpec(cw.shape, lambda i: (0, 0))],
        out_specs=(blk(f2), pl.BlockSpec((3, SUBLANES, f2), lambda i: (0, 0, 0)), pl.BlockSpec((SUBLANES, f2), lambda i: (0, 0))),
        scratch_shapes=[pltpu.VMEM((br + hl, ch), f32)] * 4 + [pltpu.VMEM((ext_rows, ch), f32)] * 2,
        name=name, compiler_params=_params(1))(dact, dact, cv, cv, up, up, cw)


SSM_GB = 8
SSM_PLANES = 8
SSM_ROWS = 256
SSM_UNROLL = 8


def _ssm_pitch(seq):
    p = seq + SUBLANES
    assert (p // SUBLANES) % 2 == 1
    return p


def _rows(base, rc):
    return pl.ds(pl.multiple_of(base + rc * SSM_ROWS, SUBLANES), SSM_ROWS)


def _ssm_project_in(u_ref, b_ref, planes, e, seq, pitch):
    def chunk(rc, _):
        uc = u_ref[_rows(e * seq, rc), :].astype(bf16)
        for j in range(SSM_PLANES):
            planes[_rows(j * pitch, rc), :] = _dot(uc, b_ref[:, j * LANES:(j + 1) * LANES], NN)
        return 0
    lax.fori_loop(0, seq // SSM_ROWS, chunk, 0)


def _ssm_rows(planes, rc, pitch):
    return jnp.concatenate([planes[_rows(j * pitch, rc), :].astype(bf16) for j in range(SSM_PLANES)], axis=1)


def _ssm_scan(planes_list, l1, l2, seq, pitch, reverse=False):
    def step(s, hs):
        hs = list(hs)
        for k in range(SSM_UNROLL):
            t = s * SSM_UNROLL + k
            t = seq - 1 - t if reverse else t
            for e, planes in enumerate(planes_list):
                hs[e] = hs[e] * l1 + pltpu.roll(hs[e], 4, 0) * l2 + planes[pl.ds(t, SUBLANES, stride=pitch), :]
                planes[pl.ds(t, SUBLANES, stride=pitch), :] = hs[e]
        return tuple(hs)
    zero = jnp.zeros((SUBLANES, LANES), f32)
    lax.fori_loop(0, seq // SSM_UNROLL, step, tuple(zero for _ in planes_list))


def _ssm_fwd(u, b_big, c_big, lslab, dskip, seq, name, comm=None):
    t_all, w = u.shape
    nb, gw, pitch = t_all // seq, SSM_GB * SSM_GROUP, _ssm_pitch(seq)
    assert gw == LANES

    def body(u_ref, b_ref, c_ref, l_ref, d_ref, y_ref, *planes):
        l1, l2 = l_ref[0:SUBLANES, :], l_ref[SUBLANES:2 * SUBLANES, :]
        for e in range(nb):
            _ssm_project_in(u_ref, b_ref, planes[e], e, seq, pitch)
        _ssm_scan(planes, l1, l2, seq, pitch)
        for e in range(nb):
            def chunk(rc, _, e=e):
                rows = _rows(e * seq, rc)
                y_ref[rows, :] = _dot(_ssm_rows(planes[e], rc, pitch), c_ref[...], NN) + d_ref[...] * u_ref[rows, :]
                return 0
            lax.fori_loop(0, seq // SSM_ROWS, chunk, 0)

    (y,), extra = _call(
        body, ins=[u, b_big, c_big, lslab, dskip], out_shape=[SDS((t_all, w), f32)], grid=(w // gw,),
        in_specs=[pl.BlockSpec((t_all, gw), lambda k: (0, k)), pl.BlockSpec((None,) + b_big.shape[1:], lambda k: (k, 0, 0)),
                  pl.BlockSpec((None,) + c_big.shape[1:], lambda k: (k, 0, 0)),
                  pl.BlockSpec((None,) + lslab.shape[1:], lambda k: (k, 0, 0)), pl.BlockSpec((1, gw), lambda k: (0, k))],
        out_specs=[pl.BlockSpec((t_all, gw), lambda k: (0, k))],
        scratch_shapes=[pltpu.VMEM((SSM_PLANES * pitch, LANES), f32) for _ in range(nb)], name=name, comm=comm)
    return y, extra


def _ssm_bwd(u, dy, b_big, c_big, lslab, dskip, seq, name, comm=None):
    t_all, w = u.shape
    nb, gw, pitch = t_all // seq, SSM_GB * SSM_GROUP, _ssm_pitch(seq)
    ns = SSM_PLANES * LANES

    def body(u_ref, dy_ref, b_ref, c_ref, l_ref, d_ref, du_ref, db_ref, dc_ref, dl_ref, dd_ref, *planes):
        hp, ap = planes[:nb], planes[nb:]
        l1, l2 = l_ref[0:SUBLANES, :], l_ref[SUBLANES:2 * SUBLANES, :]
        for e in range(nb):
            _ssm_project_in(u_ref, b_ref, hp[e], e, seq, pitch)
        _ssm_scan(hp, l1, l2, seq, pitch)
        dd_ref[...] = jnp.zeros_like(dd_ref)
        dc_ref[...] = jnp.zeros_like(dc_ref)
        db_ref[...] = jnp.zeros_like(db_ref)
        for e in range(nb):
            def chunk(rc, _, e=e):
                rows = _rows(e * seq, rc)
                dyc = dy_ref[rows, :]
                dyb = dyc.astype(bf16)
                for j in range(SSM_PLANES):
                    ap[e][_rows(j * pitch, rc), :] = _dot(dyb, c_ref[j * LANES:(j + 1) * LANES, :], NT)
                dd_ref[...] += _sum8(dyc * u_ref[rows, :])
                dc_ref[...] += _dot(_ssm_rows(hp[e], rc, pitch), dyb, TN)
                return 0
            lax.fori_loop(0, seq // SSM_ROWS, chunk, 0)

        def step(s, carry):
            carry = [list(c) for c in carry]
            for k in range(SSM_UNROLL):
                t = seq - 1 - (s * SSM_UNROLL + k)
                for e in range(nb):
                    a, s1, s2 = carry[e]
                    a = a * l1 - pltpu.roll(a, 4, 0) * l2 + ap[e][pl.ds(t, SUBLANES, stride=pitch), :]
                    ap[e][pl.ds(t, SUBLANES, stride=pitch), :] = a
                    hprev = hp[e][pl.ds(jnp.maximum(t - 1, 0), SUBLANES, stride=pitch), :] * jnp.where(t > 0, 1.0, 0.0)
                    carry[e] = [a, s1 + a * hprev, s2 + a * pltpu.roll(hprev, 4, 0)]
            return tuple(tuple(c) for c in carry)
        zero = jnp.zeros((SUBLANES, LANES), f32)
        fin = lax.fori_loop(0, seq // SSM_UNROLL, step, tuple((zero, zero, zero) for _ in range(nb)))
        dl_ref[0:SUBLANES, :] = sum(f[1] for f in fin)
        dl_ref[SUBLANES:2 * SUBLANES, :] = sum(f[2] for f in fin)

        for e in range(nb):
            def chunk2(rc, _, e=e):
                rows = _rows(e * seq, rc)
                ar = _ssm_rows(ap[e], rc, pitch)
                du_ref[rows, :] = (_dot(ar, b_ref[...], NT) + d_ref[...] * dy_ref[rows, :]).astype(bf16)
                db_ref[...] += _dot(u_ref[rows, :].astype(bf16), ar, TN)
                return 0
            lax.fori_loop(0, seq // SSM_ROWS, chunk2, 0)

    col = pl.BlockSpec((t_all, gw), lambda k: (0, k))
    per = lambda s: pl.BlockSpec((None,) + s[1:], lambda k: (k, 0, 0))
    ng = w // gw
    res, extra = _call(
        body, ins=[u, dy, b_big, c_big, lslab, dskip],
        out_shape=[SDS((t_all, w), bf16), SDS(b_big.shape, f32), SDS(c_big.shape, f32), SDS((ng, 2 * SUBLANES, LANES), f32),
                   SDS((SUBLANES, w), f32)],
        grid=(ng,),
        in_specs=[col, col, per(b_big.shape), per(c_big.shape), per(lslab.shape), pl.BlockSpec((1, gw), lambda k: (0, k))],
        out_specs=[col, per(b_big.shape), per(c_big.shape), per((ng, 2 * SUBLANES, LANES)), pl.BlockSpec((SUBLANES, gw), lambda k: (0, k))],
        scratch_shapes=[pltpu.VMEM((SSM_PLANES * pitch, LANES), f32) for _ in range(2 * nb)], name=name, comm=comm)
    return (*res, extra)


def _ssm_disc_fwd(lam_re, lam_im, dt, b_re, b_im, name):
    def body(a_ref, b_ref, dt_ref, br_ref, bi_ref, lr_ref, li_ref, cr_ref, ci_ref, bbr_ref, bbi_ref):
        a, b, dtv = a_ref[...], b_ref[...], dt_ref[...]
        mag, ang = jnp.exp(a * dtv), b * dtv
        lr, li = mag * jnp.cos(ang), mag * jnp.sin(ang)
        nr, den = lr - 1.0, a * a + b * b
        cr, ci = (nr * a + li * b) / den, (li * a - nr * b) / den
        lr_ref[...], li_ref[...], cr_ref[...], ci_ref[...] = lr, li, cr, ci
        bbr_ref[...] = cr * br_ref[...] - ci * bi_ref[...]
        bbi_ref[...] = cr * bi_ref[...] + ci * br_ref[...]
    c, m = SDS(lam_re.shape, f32), SDS(b_re.shape, f32)
    return pl.pallas_call(body, out_shape=(c, c, c, c, m, m), name=name)(lam_re, lam_im, dt, b_re, b_im)


def _ssm_disc_bwd(lam_re, lam_im, dt, b_re, b_im, g_lr, g_li, g_bbr, g_bbi, name):
    def body(a_ref, b_ref, dt_ref, br_ref, bi_ref, glr_ref, gli_ref, gbr_ref, gbi_ref, da_ref, db_ref, ddt_ref, dbr_ref, dbi_ref):
        a, b, dtv = a_ref[...], b_ref[...], dt_ref[...]
        mag, ang = jnp.exp(a * dtv), b * dtv
        cs, sn = jnp.cos(ang), jnp.sin(ang)
        lr, li = mag * cs, mag * sn
        nr, den = lr - 1.0, a * a + b * b
        cr, ci = (nr * a + li * b) / den, (li * a - nr * b) / den
        gbr, gbi, brv, biv = gbr_ref[...], gbi_ref[...], br_ref[...], bi_ref[...]
        dbr_ref[...] = cr * gbr + ci * gbi
        dbi_ref[...] = cr * gbi - ci * gbr
        dcr = jnp.sum(brv * gbr + biv * gbi, axis=1, keepdims=True)
        dci = jnp.sum(brv * gbi - biv * gbr, axis=1, keepdims=True)
        dnum_r, dnum_i = dcr / den, dci / den
        dden = -(dcr * cr + dci * ci) / den
        dnr = dnum_r * a - dnum_i * b
        dli = gli_ref[...] + dnum_r * b + dnum_i * a
        dlr = glr_ref[...] + dnr
        dmag, dang = dlr * cs + dli * sn, dli * lr - dlr * li
        dadt = dmag * mag
        da_ref[...] = dnum_r * nr + dnum_i * li + dden * 2.0 * a + dadt * dtv
        db_ref[...] = dnum_r * li - dnum_i * nr + dden * 2.0 * b + dang * dtv
        ddt_ref[...] = dadt * a + dang * b
    c, m = SDS(lam_re.shape, f32), SDS(b_re.shape, f32)
    return pl.pallas_call(body, out_shape=(c, c, c, m, m), name=name)(lam_re, lam_im, dt, b_re, b_im, g_lr, g_li, g_bbr, g_bbi)


def _place():
    x, y, c = lax.axis_index("x"), lax.axis_index("y"), lax.axis_index("c")
    return x, y, c, 2 * x + y


def _half_axis(shape, ax):
    return 0 if shape[0] == 2 else (3 - ax)


def _sub(ref, axis, start, size):
    idx = [slice(None)] * len(ref.shape)
    idx[axis] = pl.ds(start, size)
    return ref.at[tuple(idx)]


def _region(ref, full_shape, ax, slot=None, half=None):
    if slot is not None:
        n = full_shape[ax] // N_CHIPS
        ref = _sub(ref, ax, slot * n, n)
    if half is not None:
        ha = _half_axis(full_shape, ax)
        n = full_shape[ha] // 2
        ref = _sub(ref, ha, half * n, n)
    return ref


def _halved(shape, axis):
    return tuple(s // 2 if a == axis else s for a, s in enumerate(shape))


class _Comm:
    def __init__(self, ins, out_shapes, aliases, scratch, start, finish):
        self.ins, self.out_shapes, self.aliases, self.scratch, self.start, self.finish = ins, out_shapes, aliases, scratch, start, finish


def _call(body, *, ins, in_specs, out_shape, out_specs, grid, scratch_shapes, name, comm=None):
    if comm is None:
        res = pl.pallas_call(body, out_shape=tuple(out_shape), grid=grid, in_specs=list(in_specs), out_specs=tuple(out_specs),
                             scratch_shapes=list(scratch_shapes), name=name, compiler_params=_params(len(grid)))(*ins)
        return list(res), []
    n_in, n_out, n_scr, c_in, c_out = len(ins), len(out_shape), len(scratch_shapes), len(comm.ins), len(comm.out_shapes)

    def fused(*refs):
        pos = [n_in, n_in + c_in, n_in + c_in + n_out, n_in + c_in + n_out + c_out, n_in + c_in + n_out + c_out + n_scr]
        in_refs, cin, out_refs, cout, scr, cscr = (refs[:pos[0]], refs[pos[0]:pos[1]], refs[pos[1]:pos[2]], refs[pos[2]:pos[3]],
                                                   refs[pos[3]:pos[4]], refs[pos[4]:])
        ids = [pl.program_id(a) for a in range(len(grid))]
        first, last = ids[0] == 0, ids[0] == grid[0] - 1
        for a in range(1, len(grid)):
            first, last = first & (ids[a] == 0), last & (ids[a] == grid[a] - 1)

        @pl.when(first)
        def _():
            comm.start(cin, cout, cscr)

        body(*in_refs, *out_refs, *scr)

        @pl.when(last)
        def _():
            comm.finish(cin, cout, cscr)

    res = pl.pallas_call(
        fused, out_shape=tuple(out_shape) + tuple(comm.out_shapes), grid=grid, in_specs=list(in_specs) + [ANY] * c_in,
        out_specs=tuple(out_specs) + tuple([ANY] * c_out), scratch_shapes=list(scratch_shapes) + list(comm.scratch),
        input_output_aliases={n_in + i: n_out + o for i, o in comm.aliases}, name=name, compiler_params=_params(len(grid)))(*ins, *comm.ins)
    return list(res[:n_out]), list(res[n_out:])


def _comm_only(comm, name):
    c_in, c_out = len(comm.ins), len(comm.out_shapes)

    def body(*refs):
        cin, cout, cscr = refs[:c_in], refs[c_in:c_in + c_out], refs[c_in + c_out:]
        comm.start(cin, cout, cscr)
        comm.finish(cin, cout, cscr)

    return pl.pallas_call(body, out_shape=tuple(comm.out_shapes), in_specs=[ANY] * c_in, out_specs=tuple([ANY] * c_out),
                          scratch_shapes=list(comm.scratch), input_output_aliases=dict(comm.aliases), name=name)(*comm.ins)


def _gather_plan(shards, axes):
    n = len(shards)
    fulls = [tuple(s * N_CHIPS if a == ax else s for a, s in enumerate(sh.shape)) for sh, ax in zip(shards, axes)]
    own = 6

    def copies(src, dst, scr):
        send_sems, recv_sems = scr
        x, y, c, p = _place()
        chips = [(1 - x, y), (x, 1 - y), (1 - x, 1 - y)]
        slots = [2 * cx + cy for cx, cy in chips]

        def copy(a, k, slot, half, to, from_shard=False):
            where = _region(dst[a], fulls[a], axes[a], slot, half)
            source = where
            if from_shard:
                ha = _half_axis(fulls[a], axes[a])
                hn = fulls[a][ha] // 2
                source = _sub(src[a], ha, half * hn, hn)
            return pltpu.make_async_remote_copy(src_ref=source, dst_ref=where, send_sem=send_sems.at[a, k],
                                                recv_sem=recv_sems.at[a, k], device_id=to, device_id_type=MESH)

        parts = range(n)
        mine = [pltpu.make_async_remote_copy(src_ref=src[a], dst_ref=_region(dst[a], fulls[a], axes[a], p),
                                             send_sem=send_sems.at[a, own], recv_sem=recv_sems.at[a, own],
                                             device_id=(x, y, 1 - c), device_id_type=MESH) for a in parts]
        first = [copy(a, j, p, c, (*chips[j], c), True) for a in parts for j in range(3)]
        landed = [copy(a, j, slots[j], c, (x, y, c)) for a in parts for j in range(3)]
        passed = [copy(a, 3 + j, slots[j], c, (x, y, 1 - c)) for a in parts for j in range(3)]
        handed = [copy(a, 3 + j, slots[j], 1 - c, (x, y, c)) for a in parts for j in range(3)]
        return mine, first, landed, passed, handed

    def start(src, dst, scr):
        mine, first, _, _, _ = copies(src, dst, scr)
        for cp in first + mine:
            cp.start()

    def finish(src, dst, scr):
        mine, first, landed, passed, handed = copies(src, dst, scr)
        for arrived, fwd in zip(landed, passed):
            arrived.wait_recv()
            fwd.start()
        for cp in handed + mine:
            cp.wait_recv()
        for cp in first + passed + mine:
            cp.wait_send()

    return _Comm(list(shards), [SDS(f, s.dtype) for f, s in zip(fulls, shards)], [],
                 [pltpu.SemaphoreType.DMA((n, 7)), pltpu.SemaphoreType.DMA((n, 7))], start, finish)


def _all_gather(shards, axes, name):
    return _comm_only(_gather_plan(shards, axes), name)


def _swap_halves(grads, axes, name):
    n = len(grads)
    shapes = [g.shape for g in grads]

    def body(*refs):
        src, dst = refs[:n], refs[n:2 * n]
        send_sems, recv_sems = refs[2 * n:]
        x, y, c, _ = _place()
        cps = [pltpu.make_async_remote_copy(src_ref=_region(src[a], shapes[a], axes[a], None, 1 - c), dst_ref=dst[a],
                                            send_sem=send_sems.at[a], recv_sem=recv_sems.at[a],
                                            device_id=(x, y, 1 - c), device_id_type=MESH) for a in range(n)]
        for cp in cps:
            cp.start()
        for cp in cps:
            cp.wait()

    outs = tuple(SDS(_halved(s, _half_axis(s, ax)), g.dtype) for s, ax, g in zip(shapes, axes, grads))
    return pl.pallas_call(body, out_shape=outs, in_specs=[ANY] * n, out_specs=tuple([ANY] * n),
                          scratch_shapes=[pltpu.SemaphoreType.DMA((n,)), pltpu.SemaphoreType.DMA((n,))], name=name)(*grads)


def _row_block(rows, row_bytes, limit=3 << 20):
    for b in (1024, 512, 256, 128, 64, 32, 16, 8):
        if rows % b == 0 and b * row_bytes <= limit:
            return b
    return rows


def _add_own_half(g, other, ax, cidx, name):
    _, kp, np_ = other.shape
    ha = _half_axis(g.shape, ax)
    ks, ns = (kp // N_CHIPS, np_) if ax == 1 else (kp, np_ // N_CHIPS)
    bk = _row_block(ks, ns * 4)
    nkb = ks // bk

    def g_map(q, i, cref):
        c = cref[0]
        if ax == 1:
            return (c, q * nkb + i, 0) if ha == 0 else (0, q * nkb + i, c)
        return (c, i, q) if ha == 0 else (0, c * nkb + i, q)

    def o_map(q, i, cref):
        return (0, q * nkb + i, 0) if ax == 1 else (0, i, q)

    def body(c_ref, g_ref, o_ref, send_ref, land_ref):
        del c_ref
        s = (g_ref[...].astype(f32) + o_ref[...].astype(f32)).astype(send_ref.dtype)
        send_ref[...] = s
        land_ref[...] = s

    out = pl.BlockSpec((None, bk, ns), lambda q, i, cref: (q, i, 0))
    grid_spec = pltpu.PrefetchScalarGridSpec(
        num_scalar_prefetch=1, grid=(N_CHIPS, nkb),
        in_specs=[pl.BlockSpec((None, bk, ns), g_map), pl.BlockSpec((None, bk, ns), o_map)], out_specs=(out, out))
    shape = SDS((N_CHIPS, ks, ns), g.dtype)
    return pl.pallas_call(body, out_shape=(shape, shape), grid_spec=grid_spec, name=name, compiler_params=_params(2))(cidx, g, other)


def _owner_plan(sends, lands):
    n = len(sends)

    def copies(cin, dst, scr):
        src = cin[:n]
        send_sems, recv_sems = scr
        x, y, c, p = _place()
        chips = [(1 - x, y), (x, 1 - y), (1 - x, 1 - y)]
        slots = [2 * cx + cy for cx, cy in chips]
        out = [pltpu.make_async_remote_copy(src_ref=src[a].at[slots[j]], dst_ref=dst[a].at[p], send_sem=send_sems.at[a, j],
                                            recv_sem=recv_sems.at[a, j], device_id=(*chips[j], c), device_id_type=MESH)
               for a in range(n) for j in range(3)]
        back = [pltpu.make_async_remote_copy(src_ref=src[a].at[p], dst_ref=dst[a].at[slots[j]], send_sem=send_sems.at[a, j],
                                             recv_sem=recv_sems.at[a, j], device_id=(x, y, c), device_id_type=MESH)
                for a in range(n) for j in range(3)]
        return out, back

    def start(cin, dst, scr):
        for cp in copies(cin, dst, scr)[0]:
            cp.start()

    def finish(cin, dst, scr):
        out, back = copies(cin, dst, scr)
        for cp in back:
            cp.wait_recv()
        for cp in out:
            cp.wait_send()

    return _Comm(list(sends) + list(lands), [SDS(l.shape, l.dtype) for l in lands], [(n + a, a) for a in range(n)],
                 [pltpu.SemaphoreType.DMA((n, 3)), pltpu.SemaphoreType.DMA((n, 3))], start, finish)


def _sum_chips(stack, shard_shape, ax, cidx, name):
    _, ks, ns = stack.shape
    ha = _half_axis(shard_shape, ax)
    bk = _row_block(ks, ns * 4 * N_CHIPS)
    nkb = ks // bk

    def o_map(i, cref):
        c = cref[0]
        return (c, i, 0) if ha == 0 else ((0, c * nkb + i, 0) if ha == 1 else (0, i, c))

    def body(c_ref, s_ref, o_ref):
        del c_ref
        acc = s_ref[0].astype(f32)
        for q in range(1, N_CHIPS):
            acc = acc + s_ref[q].astype(f32)
        o_ref[...] = acc

    grid_spec = pltpu.PrefetchScalarGridSpec(
        num_scalar_prefetch=1, grid=(nkb,), in_specs=[pl.BlockSpec((N_CHIPS, bk, ns), lambda i, cref: (0, i, 0))],
        out_specs=pl.BlockSpec((None, bk, ns), o_map))
    return pl.pallas_call(body, out_shape=SDS(shard_shape, f32), grid_spec=grid_spec, name=name, compiler_params=_params(1))(cidx, stack)


def _join_halves(slices, axes, name):
    n = len(slices)

    def body(*refs):
        dst = refs[n:2 * n]
        send_sems, recv_sems = refs[2 * n:]
        x, y, c, _ = _place()

        def half(a, h):
            ha = _half_axis(slices[a].shape, axes[a])
            hn = slices[a].shape[ha] // 2
            return _sub(dst[a], ha, h * hn, hn)

        cps = [pltpu.make_async_remote_copy(src_ref=half(a, c), dst_ref=half(a, c), send_sem=send_sems.at[a], recv_sem=recv_sems.at[a],
                                            device_id=(x, y, 1 - c), device_id_type=MESH) for a in range(n)]
        for cp in cps:
            cp.start()
        for a in range(n):
            pltpu.make_async_remote_copy(src_ref=half(a, c), dst_ref=half(a, 1 - c), send_sem=send_sems.at[a], recv_sem=recv_sems.at[a],
                                         device_id=(x, y, c), device_id_type=MESH).wait_recv()
        for cp in cps:
            cp.wait_send()

    return pl.pallas_call(
        body, out_shape=tuple(SDS(s.shape, s.dtype) for s in slices), in_specs=[ANY] * n, out_specs=tuple([ANY] * n),
        scratch_shapes=[pltpu.SemaphoreType.DMA((n,)), pltpu.SemaphoreType.DMA((n,))],
        input_output_aliases={a: a for a in range(n)}, name=name)(*slices)


def _core_index():
    return jnp.reshape(lax.axis_index("c"), (1,)).astype(jnp.int32)


def _reduce_begin(grads, axes, tag):
    cidx = _core_index()
    others = _swap_halves(grads, axes, f"rs_swap_{tag}")
    pairs = [_add_own_half(g, o, ax, cidx, f"rs_add_{tag}_{a}") for a, (g, o, ax) in enumerate(zip(grads, others, axes))]
    return _owner_plan([s for s, _ in pairs], [l for _, l in pairs])


def _reduce_end(stacks, shapes, axes, tag):
    cidx = _core_index()
    shard_shapes = [tuple(s // N_CHIPS if i == ax else s for i, s in enumerate(sh)) for sh, ax in zip(shapes, axes)]
    slices = [_sum_chips(s, sh, ax, cidx, f"rs_sum_{tag}_{a}") for a, (s, sh, ax) in enumerate(zip(stacks, shard_shapes, axes))]
    return _join_halves(slices, axes, f"rs_join_{tag}")


def _reduce_scatter(grads, axes, tag):
    stacks = _comm_only(_reduce_begin(grads, axes, tag), f"rs_owner_{tag}")
    return _reduce_end(stacks, [g.shape for g in grads], axes, tag)


SMALL_COLS = 256


def _pack(arrays, rows_multiple):
    flat = jnp.concatenate([a.reshape(-1).astype(f32) for a in arrays])
    rows = -(-flat.shape[0] // SMALL_COLS)
    rows = -(-rows // rows_multiple) * rows_multiple
    flat = jnp.pad(flat, (0, rows * SMALL_COLS - flat.shape[0]))
    return flat.reshape(1, rows, SMALL_COLS)


def _unpack(buf, shapes):
    flat, out, off = buf.reshape(-1), [], 0
    for s in shapes:
        n = math.prod(s)
        out.append(flat[off:off + n].reshape(s))
        off += n
    return out


def _block_diag_in(bb):
    g, p, c = bb.shape
    k = g // SSM_GB
    eye = jnp.eye(SSM_GB, dtype=bb.dtype)
    return jnp.einsum("kgpc,gh->kgchp", bb.reshape(k, SSM_GB, p, c), eye).reshape(k, SSM_GB * c, SSM_GB * p)


def _block_diag_out(cc):
    g, c, p = cc.shape
    k = g // SSM_GB
    eye = jnp.eye(SSM_GB, dtype=cc.dtype)
    return jnp.einsum("kgcp,gh->kgphc", cc.reshape(k, SSM_GB, c, p), eye).reshape(k, SSM_GB * p, SSM_GB * c)


def _diag_in(db, p, c):
    k = db.shape[0]
    return jnp.einsum("kgcgp->kgpc", db.reshape(k, SSM_GB, c, SSM_GB, p)).reshape(k * SSM_GB, p, c)


def _diag_out(dc, p, c):
    k = dc.shape[0]
    return jnp.einsum("kgpgc->kgcp", dc.reshape(k, SSM_GB, p, SSM_GB, c)).reshape(k * SSM_GB, c, p)


def _state_slab(v):
    g, p = v.shape
    return v.reshape(g // SSM_GB, SSM_GB * p // LANES, LANES)


BIG = ("ab_w_in", "ab_w_out", "ssm_w_in", "ssm_w_glu", "xa_w_q", "xa_w_kv", "xa_w_o", "ffn_w_up", "ffn_w_down")
BIG_AXIS = dict(ab_w_in=2, ab_w_out=1, ssm_w_in=1, ssm_w_glu=2, xa_w_q=1, xa_w_kv=2, xa_w_o=1, ffn_w_up=2, ffn_w_down=1)
SMALL_REPL = ("norm_mix", "norm_xattn", "norm_ffn", "norm_mem", "norm_final", "pool_w", "pool_scale", "ssm_lam_re", "ssm_lam_im",
              "ssm_log_dt", "ssm_b_re", "ssm_b_im", "ssm_c_re", "ssm_c_im", "ffn_conv_b")
SMALL_SHARDED = ("ssm_d", "ffn_conv_w")
FIRST_MIXER = ("ab_w_in", "ab_w_out")
WEIGHTS = ("norm_mix", "norm_xattn", "norm_ffn", "norm_mem", "norm_final", "ab_w_in", "pool_w", "pool_scale", "ab_w_out", "ssm_w_in",
           "ssm_lam_re", "ssm_lam_im", "ssm_log_dt", "ssm_b_re", "ssm_b_im", "ssm_c_re", "ssm_c_im", "ssm_d", "ssm_w_glu", "xa_w_q",
           "xa_w_kv", "xa_w_o", "ffn_w_up", "ffn_conv_w", "ffn_conv_b", "ffn_w_down")


class _Reducer:
    def __init__(self):
        self.done, self.groups = {}, 0

    def begin(self, keys, gw):
        self.groups += 1
        return _reduce_begin([gw[k] for k in keys], [BIG_AXIS.get(k[0], 1) for k in keys], f"g{self.groups}")

    def end(self, keys, gw, stacks):
        slices = _reduce_end(stacks, [gw[k].shape for k in keys], [BIG_AXIS.get(k[0], 1) for k in keys], f"g{self.groups}")
        self.done.update(zip(keys, slices))


def _local_step(xf, memf, tgt, w, wf, conv_w, ssm_d, seq, late_weights=None, reducer=None):
    d = xf.shape[1]
    depth = w["norm_mix"].shape[0]
    wf = dict(wf)
    late_weights = late_weights or {}
    sbw = wf["ab_w_in", 0].shape[2] // 4
    row = lambda a: a.reshape(1, -1)

    gs, ps = w["ssm_lam_re"].shape[1:]
    col = lambda a: a.reshape(gs * ps, 1)
    lam_re, lam_im = col(w["ssm_lam_re"][0]), col(w["ssm_lam_im"][0])
    dt = col(jnp.broadcast_to(jnp.exp(w["ssm_log_dt"][0])[:, None], (gs, ps)))
    b_re, b_im = w["ssm_b_re"][0].reshape(gs * ps, -1), w["ssm_b_im"][0].reshape(gs * ps, -1)
    lb_re, lb_im, _, _, bb_re, bb_im = _ssm_disc_fwd(lam_re, lam_im, dt, b_re, b_im, "ssm_disc")
    cgrp = b_re.shape[1]
    b_big = jnp.concatenate([_block_diag_in(bb_re.reshape(gs, ps, cgrp)), _block_diag_in(bb_im.reshape(gs, ps, cgrp))], axis=2).astype(bf16)
    c_big = jnp.concatenate([_block_diag_out(w["ssm_c_re"][0]), -_block_diag_out(w["ssm_c_im"][0])], axis=1).astype(bf16)
    lr_s, li_s = _state_slab(lb_re.reshape(gs, ps)), _state_slab(lb_im.reshape(gs, ps))
    lslab = jnp.concatenate([lr_s, lr_s, -li_s, li_s], axis=1)

    mem_n = _norm_fwd(memf, row(w["norm_mem"]), "norm_mem")
    kv = [None] * depth
    xs, saved = [xf], []
    cur = xf
    h_next = _norm_fwd(cur, row(w["norm_mix"][0]), "norm_mix0")
    for l in range(depth):
        sv = {}
        h = h_next
        sv["h"] = h
        if l % 2 == 0:
            qkv = _mm(h, wf["ab_w_in", 0], mode="nn", b_l=0, n=3 * sbw, out_dtype=bf16, name=f"qkv{l}")
            u = _mm(h, wf["ab_w_in", 0], mode="nn", b_l=0, b_n0=3 * sbw, n=sbw, out_dtype=f32, name=f"poolin{l}")
            plan, names = late_weights.get(f"sb_fwd{l}", (None, ()))
            mix, ltot, first, late = _sb_fwd(qkv, seq, f"sb_fwd{l}", comm=plan)
            wf.update(zip(names, late))
            pooled, mix = _pool_fwd(u, mix, w["pool_w"][0], w["pool_scale"], seq, f"pool_fwd{l}")
            sv.update(qkv=qkv, mix=mix, ltot=ltot, first=first, pooled=pooled)
            cur, hx = _mm(mix, wf["ab_w_out", 0], mode="nn", b_l=0, res=cur, out_dtype=f32, name=f"mixout{l}",
                          norm=("fwd", row(w["norm_xattn"][l])))
        else:
            us = _mm(h, wf["ssm_w_in", 0], mode="nn", b_l=0, out_dtype=f32, name=f"ssmin{l}")
            plan, names = late_weights.get(f"ssm_fwd{l}", (None, ()))
            ys, late = _ssm_fwd(us, b_big, c_big, lslab, ssm_d, seq, f"ssm_fwd{l}", comm=plan)
            wf.update(zip(names, late))
            gl = _gelu_fwd(ys, f"gelu{l}")
            glu = _mm(gl, wf["ssm_w_glu", 0], mode="nn", b_l=0, out_dtype=f32, name=f"glu{l}")
            sv.update(us=us, ys=ys, gl=gl, glu=glu)
            cur, hx = _glu_fwd(glu, cur, row(w["norm_xattn"][l]), f"glugate{l}")
        sv["x1"] = cur
        kv[l] = _mm(mem_n, wf["xa_w_kv", l], mode="nn", b_l=0, out_dtype=bf16, name=f"kv{l}")
        qx = _mm(hx, wf["xa_w_q", l], mode="nn", b_l=0, out_dtype=bf16, name=f"xaq{l}")
        ox = _xa_fwd(qx, kv[l], seq, f"xa_fwd{l}")
        cur, hf = _mm(ox, wf["xa_w_o", l], mode="nn", b_l=0, res=cur, out_dtype=f32, name=f"xao{l}",
                      norm=("fwd", row(w["norm_ffn"][l])))
        sv.update(hx=hx, qx=qx, ox=ox, x2=cur)
        up = _mm(hf, wf["ffn_w_up", l], mode="nn", b_l=0, out_dtype=bf16, name=f"ffnup{l}")
        act, cv = _ffn_gate_fwd(up, conv_w[l], row(w["ffn_conv_b"][l]), seq, f"ffn_gate{l}")
        if l + 1 < depth:
            cur, h_next = _mm(act, wf["ffn_w_down", l], mode="nn", b_l=0, res=cur, out_dtype=f32, name=f"ffndown{l}",
                              norm=("fwd", row(w["norm_mix"][l + 1])))
        else:
            cur = _mm(act, wf["ffn_w_down", l], mode="nn", b_l=0, res=cur, out_dtype=f32, name=f"ffndown{l}")
        sv.update(hf=hf, up=up, cv=cv, act=act)
        saved.append(sv)
        xs.append(cur)

    dx, g_final8, loss8 = _loss_head(cur, tgt, row(w["norm_final"]), "loss_head")

    gw = {}
    small = {"norm_final": jnp.sum(g_final8, axis=0)}
    g_mix, g_xa, g_ffn, g_cw, g_cb = [None] * depth, [None] * depth, [None] * depth, [None] * depth, [None] * depth
    dmem_n = None

    pending = []

    def wgrad(key, a, b, l, **kw):
        kw.setdefault("bk", 1024)
        gw[key, l] = _mm(a, b, mode="tn", out_dtype=bf16, out_l=0, out_layers=1, name=f"dw_{key}{l}", **kw)
        pending.append((key, l))

    def reduce_beside():
        if reducer is None or not pending:
            return None, []
        keys = list(pending)
        pending.clear()
        return reducer.begin(keys, gw), keys

    for l in reversed(range(depth)):
        sv = saved[l]
        dact = _mm(dx, wf["ffn_w_down", l], mode="nt", b_l=0, out_dtype=bf16, name=f"d_act{l}")
        wgrad("ffn_w_down", sv["act"], dx, l)
        dup, dcw8, dcb8 = _ffn_gate_bwd(dact, sv["up"], sv["cv"], conv_w[l], seq, f"ffn_gate_bwd{l}")
        g_cw[l], g_cb[l] = jnp.sum(dcw8, axis=1), jnp.sum(dcb8, axis=0)
        wgrad("ffn_w_up", sv["hf"], dup, l)
        dx, g8 = _mm(dup, wf["ffn_w_up", l], mode="nt", b_l=0, out_dtype=f32, name=f"d_hf{l}",
                     norm=("bwd", sv["x2"], dx, row(w["norm_ffn"][l])))
        g_ffn[l] = jnp.sum(g8, axis=0)
        dox = _mm(dx, wf["xa_w_o", l], mode="nt", b_l=0, out_dtype=bf16, name=f"d_ox{l}")
        wgrad("xa_w_o", sv["ox"], dx, l)
        dqx, dkv = _xa_bwd(sv["qx"], kv[l], dox, seq, f"xa_bwd{l}")
        wgrad("xa_w_kv", mem_n, dkv, l, bk=mem_n.shape[0])
        dmem_n = _mm(dkv, wf["xa_w_kv", l], mode="nt", b_l=0, res=dmem_n, out_dtype=f32, name=f"d_memn{l}")
        wgrad("xa_w_q", sv["hx"], dqx, l)
        dx, g8 = _mm(dqx, wf["xa_w_q", l], mode="nt", b_l=0, out_dtype=f32, name=f"d_hx{l}",
                     norm=("bwd", sv["x1"], dx, row(w["norm_xattn"][l])))
        g_xa[l] = jnp.sum(g8, axis=0)
        if l % 2 == 0:
            dmix = _mm(dx, wf["ab_w_out", 0], mode="nt", b_l=0, out_dtype=f32, name=f"d_mix{l}")
            comm, keys = reduce_beside()
            dq, dk, dv, stacks = _sb_bwd(sv["qkv"], sv["ltot"], sv["first"], dmix, seq, f"sb_bwd{l}", comm=comm)
            if comm is not None:
                reducer.end(keys, gw, stacks)
            wgrad("ab_w_out", sv["mix"], dx, 0)
            du, dpw, dps8 = _pool_bwd(dmix, sv["pooled"], w["pool_w"][0], w["pool_scale"], seq, f"pool_bwd{l}")
            small["pool_w"], small["pool_scale"] = dpw[None], jnp.sum(dps8, axis=0)[None]
            dproj = jnp.concatenate([dq, dk, dv, du], axis=1)
            wgrad("ab_w_in", sv["h"], dproj, 0)
            dx, g8 = _mm(dproj, wf["ab_w_in", 0], mode="nt", b_l=0, out_dtype=f32, name=f"d_h{l}",
                         norm=("bwd", xs[l], dx, row(w["norm_mix"][l])))
        else:
            dglu = _glu_bwd(dx, sv["glu"], f"glugate_bwd{l}")
            dgl = _mm(dglu, wf["ssm_w_glu", 0], mode="nt", b_l=0, out_dtype=f32, name=f"d_gelu{l}")
            dys = _gelu_bwd(dgl, sv["ys"], f"gelu_bwd{l}")
            comm, keys = reduce_beside()
            dus, db_big, dc_big, dl, dd8, stacks = _ssm_bwd(sv["us"], dys, b_big, c_big, lslab, ssm_d, seq, f"ssm_bwd{l}", comm=comm)
            if comm is not None:
                reducer.end(keys, gw, stacks)
            wgrad("ssm_w_glu", sv["gl"], dglu, 0)
            small["ssm_d"] = jnp.sum(dd8, axis=0)[None]
            half = SSM_PLANES // 2
            g_lr = (dl[:, 0:half] + dl[:, half:SUBLANES]).reshape(gs * ps, 1)
            g_li = (dl[:, SUBLANES + half:] - dl[:, SUBLANES:SUBLANES + half]).reshape(gs * ps, 1)
            g_bbr = _diag_in(db_big[:, :, :SSM_GB * ps], ps, cgrp).reshape(gs * ps, cgrp)
            g_bbi = _diag_in(db_big[:, :, SSM_GB * ps:], ps, cgrp).reshape(gs * ps, cgrp)
            d_a, d_b, d_dt, d_br, d_bi = _ssm_disc_bwd(lam_re, lam_im, dt, b_re, b_im, g_lr, g_li, g_bbr, g_bbi, "ssm_disc_bwd")
            small["ssm_lam_re"], small["ssm_lam_im"] = d_a.reshape(1, gs, ps), d_b.reshape(1, gs, ps)
            small["ssm_log_dt"] = (jnp.sum(d_dt.reshape(gs, ps), axis=1) * dt.reshape(gs, ps)[:, 0])[None]
            small["ssm_b_re"], small["ssm_b_im"] = d_br.reshape(1, gs, ps, cgrp), d_bi.reshape(1, gs, ps, cgrp)
            small["ssm_c_re"] = _diag_out(dc_big[:, :SSM_GB * ps], ps, cgrp)[None]
            small["ssm_c_im"] = -_diag_out(dc_big[:, SSM_GB * ps:], ps, cgrp)[None]
            wgrad("ssm_w_in", sv["h"], dus, 0)
            dx, g8 = _mm(dus, wf["ssm_w_in", 0], mode="nt", b_l=0, out_dtype=f32, name=f"d_h{l}",
                         norm=("bwd", xs[l], dx, row(w["norm_mix"][l])))
        g_mix[l] = jnp.sum(g8, axis=0)

    small["norm_mem"] = jnp.sum(_norm_bwd_gain_only(dmem_n, memf, "norm_mem_bwd"), axis=0)
    small["norm_mix"], small["norm_xattn"], small["norm_ffn"] = jnp.stack(g_mix), jnp.stack(g_xa), jnp.stack(g_ffn)
    small["ffn_conv_w"], small["ffn_conv_b"] = jnp.stack(g_cw), jnp.stack(g_cb)
    return loss8, dx, gw, small, pending


def _step(x, mem, loss_target, w, m, v):
    nb, seq, d = x.shape
    t_all = nb * seq
    depth = w["norm_mix"].shape[0]
    chip = 2 * lax.axis_index("x") + lax.axis_index("y")

    small_mine = _pack([w[k] for k in SMALL_SHARDED], SUBLANES)
    gathered = _all_gather([w[k].astype(bf16) for k in FIRST_MIXER] + [small_mine], [BIG_AXIS[k] for k in FIRST_MIXER] + [1],
                           "gather_first")
    wf = {(k, 0): g for k, g in zip(FIRST_MIXER, gathered[:-1])}
    per_chip = gathered[-1].reshape(N_CHIPS, -1)
    pieces = [_unpack(per_chip[q], [w[k].shape for k in SMALL_SHARDED]) for q in range(N_CHIPS)]
    ssm_d = jnp.concatenate([pc[0] for pc in pieces], axis=-1)
    conv_w = jnp.concatenate([pc[1] for pc in pieces], axis=-1)
    ff2 = conv_w.shape[-1]
    late = [(k, l) for k in BIG if k not in FIRST_MIXER for l in range(w[k].shape[0])]
    groups = {"sb_fwd0": [kl for kl in late if kl[1] == 0], "ssm_fwd1": [kl for kl in late if kl[1] > 0]}
    late_weights = {hook: (_gather_plan([w[k][l:l + 1].astype(bf16) for k, l in keys], [BIG_AXIS[k] for k, _ in keys]), keys)
                    for hook, keys in groups.items()}

    reducer = _Reducer()
    loss8, dx, gw, small, pending = _local_step(x.reshape(t_all, d), mem.reshape(-1, d), loss_target.reshape(t_all, d), w, wf,
                                                conv_w, ssm_d, seq, late_weights=late_weights, reducer=reducer)
    loss = lax.psum(0.5 * jnp.sum(loss8) / d, ("x", "y", "c"))

    small_names = SMALL_REPL + SMALL_SHARDED
    small_full_shapes = [w[k].shape for k in SMALL_REPL] + [(1, d), (depth, 3, ff2)]
    gw["small", 0] = _pack([small[k] for k in small_names], 2 * N_CHIPS * SUBLANES)
    keys = pending + [("small", 0)]
    reducer.end(keys, gw, _comm_only(reducer.begin(keys, gw), "rs_owner_last"))
    g_big = {k: jnp.concatenate([reducer.done[k, l] for l in range(w[k].shape[0])], axis=0) for k in BIG}
    small_all = _all_gather([reducer.done["small", 0]], [1], "gather_small_grads")[0]
    g_small = dict(zip(small_names, _unpack(small_all, small_full_shapes)))
    g_small["ssm_d"] = lax.dynamic_slice_in_dim(g_small["ssm_d"], chip * (d // N_CHIPS), d // N_CHIPS, axis=1)
    g_small["ffn_conv_w"] = lax.dynamic_slice_in_dim(g_small["ffn_conv_w"], chip * (ff2 // N_CHIPS), ff2 // N_CHIPS, axis=2)
    grads = {**g_big, **g_small}

    delta, new_m, new_v = {}, {}, {}
    for k in BIG:
        n_cols = w[k].shape[-1]
        two = lambda a: a.reshape(-1, n_cols)
        dl_, m_, v_ = _adamw(two(w[k]), two(grads[k]), two(m[k]), two(v[k]), f"adamw_{k}")
        delta[k], new_m[k], new_v[k] = dl_.reshape(w[k].shape), m_.reshape(w[k].shape), v_.reshape(w[k].shape)
    pk = lambda tree: _pack([tree[k] for k in small_names], 256)[0]
    small_shapes = [w[k].shape for k in small_names]
    outs = _adamw(pk(w), pk(grads), pk(m), pk(v), "adamw_small")
    for tree, buf in zip((delta, new_m, new_v), outs):
        tree.update(zip(small_names, _unpack(buf, small_shapes)))

    grad_x = dx.reshape(nb, seq, d)
    return (loss, grad_x, *[grads[k] for k in WEIGHTS], *[delta[k] for k in WEIGHTS], *[new_m[k] for k in WEIGHTS],
            *[new_v[k] for k in WEIGHTS])


def kernel(x, mem, norm_mix, norm_xattn, norm_ffn, norm_mem, norm_final, ab_w_in, pool_w, pool_scale, ab_w_out, ssm_w_in, ssm_lam_re, ssm_lam_im, ssm_log_dt, ssm_b_re, ssm_b_im, ssm_c_re, ssm_c_im, ssm_d, ssm_w_glu, xa_w_q, xa_w_kv, xa_w_o, ffn_w_up, ffn_conv_w, ffn_conv_b, ffn_w_down, loss_target, m_norm_mix, m_norm_xattn, m_norm_ffn, m_norm_mem, m_norm_final, m_ab_w_in, m_pool_w, m_pool_scale, m_ab_w_out, m_ssm_w_in, m_ssm_lam_re, m_ssm_lam_im, m_ssm_log_dt, m_ssm_b_re, m_ssm_b_im, m_ssm_c_re, m_ssm_c_im, m_ssm_d, m_ssm_w_glu, m_xa_w_q, m_xa_w_kv, m_xa_w_o, m_ffn_w_up, m_ffn_conv_w, m_ffn_conv_b, m_ffn_w_down, v_norm_mix, v_norm_xattn, v_norm_ffn, v_norm_mem, v_norm_final, v_ab_w_in, v_pool_w, v_pool_scale, v_ab_w_out, v_ssm_w_in, v_ssm_lam_re, v_ssm_lam_im, v_ssm_log_dt, v_ssm_b_re, v_ssm_b_im, v_ssm_c_re, v_ssm_c_im, v_ssm_d, v_ssm_w_glu, v_xa_w_q, v_xa_w_kv, v_xa_w_o, v_ffn_w_up, v_ffn_conv_w, v_ffn_conv_b, v_ffn_w_down):
    args = dict(locals())
    w = {k: args[k] for k in WEIGHTS}
    m = {k: args["m_" + k] for k in WEIGHTS}
    v = {k: args["v_" + k] for k in WEIGHTS}
    return _step(x, mem, loss_target, w, m, v)
```

```python
import math

import jax
import jax.numpy as jnp
from jax import lax
from jax.experimental import pallas as pl
from jax.experimental.pallas import tpu as pltpu

f32 = jnp.float32
bf16 = jnp.bfloat16
SDS = jax.ShapeDtypeStruct
MESH = pl.DeviceIdType.MESH
ANY = pl.BlockSpec(memory_space=pl.ANY)

SB_HEAD_DIM = 64
POOL_WINDOWS = (2, 4, 8, 16)
POOL_GROUP = 128
XA_HEADS = 4
SSM_GROUPS = 64
SSM_GROUP = 16
SSM_STATE = 64
EPS = 1e-6
ADAM_LR, ADAM_B1, ADAM_B2, ADAM_EPS, ADAM_WD, ADAM_STEP = 0.001, 0.9, 0.999, 1e-08, 0.01, 10

LANES = 128
SUBLANES = 8
N_CHIPS = 4
VMEM_LIMIT = 56 * 1024 * 1024

NN = ((1,), (0,))
NT = ((1,), (1,))
TN = ((0,), (0,))


def _dot(a, b, dims):
    return lax.dot_general(a, b, (dims, ((), ())), preferred_element_type=f32)


def _params(n_grid):
    return pltpu.CompilerParams(dimension_semantics=("arbitrary",) * n_grid, vmem_limit_bytes=VMEM_LIMIT)


def _sum8(x):
    r, n = x.shape
    return jnp.sum(x.reshape(r // SUBLANES, SUBLANES, n), axis=0)


def _split_bf16(x):
    hi = x.astype(bf16)
    lo = (x - hi.astype(f32)).astype(bf16)
    return hi, lo


def _sigmoid(x):
    return 1.0 / (1.0 + jnp.exp(-x))


MM_BM = (1024, 1408, 512, 256, 128)
MM_BN = (1536, 1408, 1024, 512, 256, 128)
MM_BK = (2816, 2048, 1024, 512)


def _divisor(n, cands):
    return next((c for c in cands if n % c == 0), n)


def _mm(a, b, *, mode, name, out_dtype, bm=None, bn=None, bk=None, a_l=None, b_l=None, b_n0=0, n=None,
        res=None, out_l=None, out_layers=None, out_prev=None, norm=None):
    dims = {"nn": NN, "nt": NT, "tn": TN}[mode]
    a2, b2 = a.shape[-2:], b.shape[-2:]
    if mode == "nn":
        (m, k), nfull = a2, b2[1]
    elif mode == "nt":
        (m, k), nfull = a2, b2[0]
    else:
        (k, m), nfull = a2, b2[1]
    n = nfull if n is None else n
    if bm is None and norm is not None:
        bm = 512
    bm = _divisor(m, MM_BM) if bm is None else min(bm, m)
    bn = _divisor(n, MM_BN) if bn is None else min(bn, n)
    if bk is None:
        bk = _divisor(k, (1024, 512)) if mode == "tn" else (k if k <= MM_BK[0] else _divisor(k, MM_BK))
    bk = min(bk, k)
    assert m % bm == 0 and n % bn == 0 and k % bk == 0 and b_n0 % bn == 0, (name, m, n, k, bm, bn, bk)
    nk, n0b = k // bk, b_n0 // bn
    a_bytes, b_bytes = m * k * a.dtype.itemsize, k * n * b.dtype.itemsize
    rows_outer = a_bytes + b_bytes * (m // bm) <= b_bytes + a_bytes * (n // bn)

    def with_layer(layer, blk, idx_fn):
        def idx(g0, g1, kk):
            i, j = (g0, g1) if rows_outer else (g1, g0)
            return idx_fn(i, j, kk) if layer is None else (layer,) + idx_fn(i, j, kk)
        return pl.BlockSpec(blk if layer is None else (None,) + blk, idx)

    if mode == "tn":
        a_spec = with_layer(a_l, (bk, bm), lambda i, j, kk: (kk, i))
    else:
        a_spec = with_layer(a_l, (bm, bk), lambda i, j, kk: (i, kk))
    if mode == "nt":
        b_spec = with_layer(b_l, (bn, bk), lambda i, j, kk: (j, kk))
    else:
        b_spec = with_layer(b_l, (bk, bn), lambda i, j, kk: (kk, j + n0b))
    o_spec = with_layer(out_l, (bm, bn), lambda i, j, kk: (i, j))
    ins, in_specs = [a, b], [a_spec, b_spec]
    row_blk = with_layer(None, (bm, bn), lambda i, j, kk: (i, j))
    if res is not None:
        ins.append(res)
        in_specs.append(row_blk)
    n_norm_in = 0
    if norm is not None:
        assert bn == n and out_l is None and out_prev is None, name
        extra = list(norm[1:])
        n_norm_in = len(extra)
        ins += extra
        in_specs += [row_blk] * (n_norm_in - 1) + [pl.BlockSpec((1, n), lambda g0, g1, kk: (0, 0))]
    aliases = {}
    if out_prev is not None:
        aliases = {len(ins): 0}
        ins.append(out_prev)
        in_specs.append(ANY)
    has_res, has_prev = res is not None, out_prev is not None

    def body(*refs):
        a_ref, b_ref = refs[0], refs[1]
        res_ref = refs[2] if has_res else None
        norm_refs = refs[2 + has_res:2 + has_res + n_norm_in]
        o_ref = refs[2 + has_res + n_norm_in + has_prev]
        row_block = pl.program_id(0 if rows_outer else 1)
        part = _dot(a_ref[...].astype(bf16), b_ref[...].astype(bf16), dims)

        def finish(r):
            if has_res:
                r = r + res_ref[...]
            if norm is None:
                o_ref[...] = r.astype(o_ref.dtype)
            elif norm[0] == "fwd":
                h_ref = refs[3 + has_res + n_norm_in + has_prev]
                o_ref[...] = r
                rs = lax.rsqrt(jnp.mean(r * r, axis=1, keepdims=True) + EPS)
                h_ref[...] = (r * rs * norm_refs[0][...]).astype(bf16)
            else:
                x_ref, dres_ref, g_ref = norm_refs
                dg_ref = refs[3 + has_res + n_norm_in + has_prev]
                xv = x_ref[...]
                rs = lax.rsqrt(jnp.mean(xv * xv, axis=1, keepdims=True) + EPS)
                xh = xv * rs
                dxh = r * g_ref[...]
                o_ref[...] = dres_ref[...] + rs * (dxh - xh * jnp.mean(dxh * xh, axis=1, keepdims=True))
                dg = _sum8(r * xh)

                @pl.when(row_block == 0)
                def _():
                    dg_ref[...] = dg

                @pl.when(row_block > 0)
                def _():
                    dg_ref[...] += dg

        if nk == 1:
            finish(part)
        else:
            acc_ref = refs[-1]
            kk = pl.program_id(2)

            @pl.when(kk == 0)
            def _():
                acc_ref[...] = part

            @pl.when(kk > 0)
            def _():
                acc_ref[...] += part

            @pl.when(kk == nk - 1)
            def _():
                finish(acc_ref[...])

    out_shape = SDS((m, n) if out_l is None else (out_layers, m, n), out_dtype)
    grid = (m // bm, n // bn, nk) if rows_outer else (n // bn, m // bm, nk)
    if norm is not None:
        if norm[0] == "fwd":
            out_shape, o_spec = (out_shape, SDS((m, n), bf16)), (o_spec, row_blk)
        else:
            out_shape = (out_shape, SDS((SUBLANES, n), f32))
            o_spec = (o_spec, pl.BlockSpec((SUBLANES, n), lambda g0, g1, kk: (0, 0)))
    return pl.pallas_call(
        body, out_shape=out_shape, grid=grid, in_specs=in_specs, out_specs=o_spec,
        scratch_shapes=[] if nk == 1 else [pltpu.VMEM((bm, bn), f32)],
        input_output_aliases=aliases, name=name, compiler_params=_params(3))(*ins)


def _rowwise(fn, row_ins, full_ins, row_outs, acc_outs, *, name, br=512):
    t = row_ins[0].shape[0]
    br = next(b for b in (br, 256, 128, 64, 32, 16, 8, t) if b <= t and t % b == 0)
    nr, nf, no = len(row_ins), len(full_ins), len(row_outs)

    def body(*refs):
        rv = [r[...] for r in refs[:nr]]
        fv = [r[...] for r in refs[nr:nr + nf]]
        o_refs = refs[nr + nf:nr + nf + no]
        a_refs = refs[nr + nf + no:]
        outs, accs = fn(rv, fv)
        for o_ref, v in zip(o_refs, outs):
            o_ref[...] = v.astype(o_ref.dtype)
        if a_refs:
            i = pl.program_id(0)

            @pl.when(i == 0)
            def _():
                for a_ref, v in zip(a_refs, accs):
                    a_ref[...] = v

            @pl.when(i > 0)
            def _():
                for a_ref, v in zip(a_refs, accs):
                    a_ref[...] += v

    in_specs = [pl.BlockSpec((br, x.shape[1]), lambda i: (i, 0)) for x in row_ins]
    in_specs += [pl.BlockSpec(x.shape, lambda i, nd=x.ndim: (0,) * nd) for x in full_ins]
    out_specs = [pl.BlockSpec((br, s.shape[1]), lambda i: (i, 0)) for s in row_outs]
    out_specs += [pl.BlockSpec(s.shape, lambda i: (0, 0)) for s in acc_outs]
    res = pl.pallas_call(body, out_shape=tuple(row_outs) + tuple(acc_outs), grid=(t // br,), in_specs=in_specs,
                         out_specs=tuple(out_specs), name=name, compiler_params=_params(1))(*row_ins, *full_ins)
    return res


def _norm_fwd(x, g, name):
    def fn(rv, fv):
        (xv,), (gv,) = rv, fv
        r = lax.rsqrt(jnp.mean(xv * xv, axis=1, keepdims=True) + EPS)
        return [xv * r * gv], []
    return _rowwise(fn, [x], [g], [SDS(x.shape, bf16)], [], name=name)[0]


def _norm_bwd_gain_only(dh, x, name):
    d = x.shape[1]

    def fn(rv, fv):
        dhv, xv = rv
        r = lax.rsqrt(jnp.mean(xv * xv, axis=1, keepdims=True) + EPS)
        return [], [_sum8(dhv * xv * r)]
    return _rowwise(fn, [dh, x], [], [], [SDS((SUBLANES, d), f32)], name=name)[0]


def _loss_head(x, target, g, name):
    d = x.shape[1]

    def fn(rv, fv):
        (xv, tv), (gv,) = rv, fv
        r = lax.rsqrt(jnp.mean(xv * xv, axis=1, keepdims=True) + EPS)
        xh = xv * r
        err = xh * gv - tv
        dy = err * (1.0 / d)
        dxh = dy * gv
        dx = r * (dxh - xh * jnp.mean(dxh * xh, axis=1, keepdims=True))
        return [dx], [_sum8(dy * xh), _sum8(err * err)]
    return _rowwise(fn, [x, target], [g], [SDS(x.shape, f32)], [SDS((SUBLANES, d), f32), SDS((SUBLANES, d), f32)], name=name)


_GELU_C = math.sqrt(2.0 / math.pi)


def _gelu_fwd(y, name):
    def fn(rv, fv):
        (v,) = rv
        t = jnp.tanh(_GELU_C * (v + 0.044715 * v * v * v))
        return [0.5 * v * (1.0 + t)], []
    return _rowwise(fn, [y], [], [SDS(y.shape, bf16)], [], name=name)[0]


def _gelu_bwd(dg, y, name):
    def fn(rv, fv):
        dgv, v = rv
        t = jnp.tanh(_GELU_C * (v + 0.044715 * v * v * v))
        dt = (1.0 - t * t) * _GELU_C * (1.0 + 3.0 * 0.044715 * v * v)
        return [dgv * (0.5 * (1.0 + t) + 0.5 * v * dt)], []
    return _rowwise(fn, [dg, y], [], [SDS(y.shape, f32)], [], name=name)[0]


def _glu_fwd(glu, x, g, name):
    d = x.shape[1]

    def fn(rv, fv):
        (gl, xv), (gv,) = rv, fv
        y = xv + gl[:, :d] * _sigmoid(gl[:, d:])
        r = lax.rsqrt(jnp.mean(y * y, axis=1, keepdims=True) + EPS)
        return [y, y * r * gv], []
    return _rowwise(fn, [glu, x], [g], [SDS(x.shape, f32), SDS(x.shape, bf16)], [], name=name)


def _glu_bwd(dx, glu, name):
    d = dx.shape[1]

    def fn(rv, fv):
        dxv, gl = rv
        sg = _sigmoid(gl[:, d:])
        return [jnp.concatenate([dxv * sg, dxv * gl[:, :d] * sg * (1.0 - sg)], axis=1)], []
    return _rowwise(fn, [dx, glu], [], [SDS(glu.shape, bf16)], [], name=name)[0]


def _adamw(w, g, m, v, name):
    c1 = 1.0 - ADAM_B1 ** ADAM_STEP
    c2 = 1.0 - ADAM_B2 ** ADAM_STEP

    def fn(rv, fv):
        wv, gv, mv, vv = rv
        m2 = ADAM_B1 * mv + (1.0 - ADAM_B1) * gv
        v2 = ADAM_B2 * vv + (1.0 - ADAM_B2) * (gv * gv)
        delta = -ADAM_LR * ((m2 / c1) / (jnp.sqrt(v2 / c2) + ADAM_EPS) + ADAM_WD * wv)
        return [delta, m2, v2], []
    s = SDS(w.shape, f32)
    return _rowwise(fn, [w, g, m, v], [], [s, s, s], [], name=name, br=256)


SB_TQ = 128
SB_KB = 4
SB_DEAD = -110.0


def _sb_heads(q, t):
    lane = lax.broadcasted_iota(jnp.int32, (t, LANES), 1)
    masks = [(lane >= hh * SB_HEAD_DIM) & (lane < (hh + 1) * SB_HEAD_DIM) for hh in range(LANES // SB_HEAD_DIM)]
    return [(m, q * jnp.where(m, 1.0, 0.0).astype(bf16)) for m in masks]


def _sb_key_minus_query(t):
    return lax.broadcasted_iota(jnp.int32, (t, t), 1) - lax.broadcasted_iota(jnp.int32, (t, t), 0)


def _tri(t, op):
    row = lax.broadcasted_iota(jnp.int32, (t, t), 0)
    col = lax.broadcasted_iota(jnp.int32, (t, t), 1)
    return jnp.where(op(row, col), 1.0, 0.0).astype(bf16)


def _sb_block(i, g, kk, kbn, t, kmq, k_ref, v_ref):
    j = i - g * kbn - kk
    off = pl.multiple_of(jnp.maximum(j, 0) * t, t)
    limit = jnp.where(j >= 0, (i - j) * t, -2 * t)
    return off, k_ref[pl.ds(off, t), :], v_ref[pl.ds(off, t), :], kmq < limit


def _sb_fwd(qkv, seq, name, comm=None):
    t_all, w3 = qkv.shape
    w = w3 // 3
    hp, tq = w // LANES, SB_TQ
    nb, nq = t_all // seq, seq // tq
    kbn = min(SB_KB, nq)

    def body(q_ref, k_ref, v_ref, o_ref, lt_ref, first_ref):
        i = pl.program_id(2)
        heads = _sb_heads(q_ref[...], tq)
        kmq = _sb_key_minus_query(tq)
        u_after = _tri(tq, lambda r, c: r > c)
        n_it = (i + kbn) // kbn

        def alive(state):
            return (state[0] < n_it) & (state[1] > SB_DEAD)

        def step(state):
            it, carry = state[0], list(state[2:])
            blocks = [_sb_block(i, it, kk, kbn, tq, kmq, k_ref, v_ref)[1:] for kk in range(kbn)]
            chains = [(hh, qh, kb, vb, valid) for kb, vb, valid in blocks for hh, (_, qh) in enumerate(heads)]
            zs = [_dot(qh, kb, NT) for _, qh, kb, _, _ in chains]
            lbs, his, los, sums = [], [], [], []
            for z, (_, _, _, _, valid) in zip(zs, chains):
                z = z * (SB_HEAD_DIM ** -0.5)
                sp = jnp.log(1.0 + jnp.exp(-jnp.abs(z)))
                lb = jnp.minimum(z, 0.0) - sp
                lk = jnp.where(valid, lb - z, 0.0)
                hi, lo = _split_bf16(lk)
                lbs.append(lb), his.append(hi), los.append(lo), sums.append(jnp.sum(lk, axis=1, keepdims=True))
            afts = [_dot(hi, u_after, NN) + _dot(lo, u_after, NN) for hi, lo in zip(his, los)]
            wgts = []
            for (hh, _, _, _, valid), lb, aft, sm in zip(chains, lbs, afts, sums):
                wgts.append(jnp.where(valid, jnp.exp(lb + (carry[2 * hh] + aft)), 0.0).astype(bf16))
                carry[2 * hh] = carry[2 * hh] + sm
            for (hh, _, _, vb, _), wgt in zip(chains, wgts):
                carry[2 * hh + 1] = carry[2 * hh + 1] + _dot(wgt, vb, NN)
            top = jnp.max(carry[0])
            for hh in range(1, len(heads)):
                top = jnp.maximum(top, jnp.max(carry[2 * hh]))
            return (it + 1, top, *carry)

        init = (jnp.int32(0), jnp.float32(0.0)) + (jnp.zeros((tq, 1), f32), jnp.zeros((tq, LANES), f32)) * len(heads)
        fin = lax.while_loop(alive, step, init)
        out = jnp.zeros((tq, LANES), f32)
        ltot = jnp.zeros((tq, LANES), f32)
        for hh, (m, _) in enumerate(heads):
            out = out + jnp.where(m, fin[2 * hh + 3], 0.0)
            ltot = ltot + jnp.where(m, fin[2 * hh + 2], 0.0)
        o_ref[...] = out
        lt_ref[...] = ltot
        first_ref[...] = jnp.zeros((SUBLANES, LANES), f32) + fin[0].astype(f32)

    row_blk = pl.BlockSpec((tq, LANES), lambda b, p, i: (b * nq + i, p))
    (mix, ltot, first), extra = _call(
        body, ins=[qkv, qkv, qkv], out_shape=[SDS((t_all, 2 * w), f32), SDS((t_all, w), f32), SDS((nb * nq * SUBLANES, w), f32)],
        grid=(nb, hp, nq),
        in_specs=[row_blk, pl.BlockSpec((seq, LANES), lambda b, p, i: (b, hp + p)),
                  pl.BlockSpec((seq, LANES), lambda b, p, i: (b, 2 * hp + p))],
        out_specs=[row_blk, row_blk, pl.BlockSpec((SUBLANES, LANES), lambda b, p, i: (b * nq + i, p))],
        scratch_shapes=[], name=name, comm=comm)
    return mix, ltot, first, extra


def _sb_bwd(qkv, ltot, first, dmix, seq, name, comm=None):
    t_all, w3 = qkv.shape
    w = w3 // 3
    hp, tq = w // LANES, SB_TQ
    nb, nq = t_all // seq, seq // tq
    kbn = min(SB_KB, nq)

    def body(q_ref, k_ref, v_ref, lt_ref, first_ref, do_ref, dq_ref, dk_ref, dv_ref, dk_acc, dv_acc):
        i = pl.program_id(2)

        @pl.when(i == 0)
        def _():
            dk_acc[...] = jnp.zeros_like(dk_acc)
            dv_acc[...] = jnp.zeros_like(dv_acc)

        heads = _sb_heads(q_ref[...], tq)
        do = do_ref[...]
        ltv = lt_ref[...]
        dos = [jnp.where(m, do, 0.0).astype(bf16) for m, _ in heads]
        lts = [jnp.sum(jnp.where(m, ltv, 0.0), axis=1, keepdims=True) * (1.0 / SB_HEAD_DIM) for m, _ in heads]
        kmq = _sb_key_minus_query(tq)
        u_incl = _tri(tq, lambda r, c: r <= c)
        u_excl = _tri(tq, lambda r, c: r < c)
        n_it = (i + kbn) // kbn

        walked = jnp.clip(jnp.max(first_ref[...]).astype(jnp.int32), 1, n_it)

        def step(s, carry):
            carry = list(carry)
            blocks = [_sb_block(i, walked - 1 - s, kk, kbn, tq, kmq, k_ref, v_ref) for kk in reversed(range(kbn))]
            chains = [(hh, qh, kb, vb, valid) for _, kb, vb, valid in blocks for hh, (_, qh) in enumerate(heads)]
            zs = [_dot(qh, kb, NT) for _, qh, kb, _, _ in chains]
            dws = [_dot(dos[hh], vb, NT) for hh, _, _, vb, _ in chains]
            lbs, lkrs, his, los, sums = [], [], [], [], []
            for z, (_, _, _, _, valid) in zip(zs, chains):
                z = z * (SB_HEAD_DIM ** -0.5)
                sp = jnp.log(1.0 + jnp.exp(-jnp.abs(z)))
                lb = jnp.minimum(z, 0.0) - sp
                lk_raw = lb - z
                lk = jnp.where(valid, lk_raw, 0.0)
                hi, lo = _split_bf16(lk)
                lbs.append(lb), lkrs.append(lk_raw), his.append(hi), los.append(lo)
                sums.append(jnp.sum(lk, axis=1, keepdims=True))
            pins = [_dot(hi, u_incl, NN) + _dot(lo, u_incl, NN) for hi, lo in zip(his, los)]
            wbs, gs, ghis, glos, gpres = [], [], [], [], []
            for (hh, _, _, _, valid), lb, pin, sm, dw in zip(chains, lbs, pins, sums, dws):
                wgt = jnp.where(valid, jnp.exp(lb + (lts[hh] - (carry[3 * hh] + pin))), 0.0)
                carry[3 * hh] = carry[3 * hh] + sm
                g = dw * wgt
                hi, lo = _split_bf16(g)
                wbs.append(wgt.astype(bf16)), gs.append(g), ghis.append(hi), glos.append(lo)
                gpres.append(carry[3 * hh + 1])
                carry[3 * hh + 1] = carry[3 * hh + 1] + jnp.sum(g, axis=1, keepdims=True)
            gins = [_dot(hi, u_excl, NN) + _dot(lo, u_excl, NN) for hi, lo in zip(ghis, glos)]
            dzbs = []
            for (_, _, _, _, valid), lb, lk_raw, g, gpre, gin in zip(chains, lbs, lkrs, gs, gpres, gins):
                dz = jnp.where(valid, g * jnp.exp(lk_raw) - (gpre + gin) * jnp.exp(lb), 0.0) * (SB_HEAD_DIM ** -0.5)
                dzbs.append(dz.astype(bf16))
            for (hh, _, kb, _, _), dzb in zip(chains, dzbs):
                carry[3 * hh + 2] = carry[3 * hh + 2] + _dot(dzb, kb, NN)
            nh = len(heads)
            for bi, (off, _, _, _) in enumerate(blocks):
                dk_j = jnp.zeros((tq, LANES), f32)
                dv_j = jnp.zeros((tq, LANES), f32)
                for hh, (_, qh) in enumerate(heads):
                    dk_j = dk_j + _dot(dzbs[bi * nh + hh], qh, TN)
                    dv_j = dv_j + _dot(wbs[bi * nh + hh], dos[hh], TN)
                dk_acc[pl.ds(off, tq), :] += dk_j
                dv_acc[pl.ds(off, tq), :] += dv_j
            return tuple(carry)

        zero1 = jnp.zeros((tq, 1), f32)
        fin = lax.fori_loop(0, walked, step, (zero1, zero1, jnp.zeros((tq, LANES), f32)) * len(heads))
        dq_all = jnp.zeros((tq, LANES), f32)
        for hh, (m, _) in enumerate(heads):
            dq_all = dq_all + jnp.where(m, fin[3 * hh + 2], 0.0)
        dq_ref[...] = dq_all.astype(bf16)

        @pl.when(i == nq - 1)
        def _():
            dk_ref[...] = dk_acc[...].astype(bf16)
            dv_ref[...] = dv_acc[...].astype(bf16)

    row_blk = pl.BlockSpec((tq, LANES), lambda b, p, i: (b * nq + i, p))
    seq_blk = pl.BlockSpec((seq, LANES), lambda b, p, i: (b, p))
    out = SDS((t_all, w), bf16)
    (dq, dk, dv), extra = _call(
        body, ins=[qkv, qkv, qkv, ltot, first, dmix], out_shape=[out, out, out], grid=(nb, hp, nq),
        in_specs=[row_blk,
                  pl.BlockSpec((seq, LANES), lambda b, p, i: (b, hp + p)),
                  pl.BlockSpec((seq, LANES), lambda b, p, i: (b, 2 * hp + p)),
                  row_blk, pl.BlockSpec((SUBLANES, LANES), lambda b, p, i: (b * nq + i, p)), row_blk],
        out_specs=[row_blk, seq_blk, seq_blk],
        scratch_shapes=[pltpu.VMEM((seq, LANES), f32), pltpu.VMEM((seq, LANES), f32)], name=name, comm=comm)
    return dq, dk, dv, extra


POOL_CHUNK = 256
POOL_HALO = 16


def _band(rows, cols, lo, hi):
    r = lax.broadcasted_iota(jnp.int32, (rows, cols), 0)
    c = lax.broadcasted_iota(jnp.int32, (rows, cols), 1)
    d = c - r
    return jnp.where((d >= lo) & (d < hi), 1.0, 0.0).astype(bf16)


def _pool_counts(r0, rows, win):
    t = lax.broadcasted_iota(jnp.int32, (rows, 1), 0) + r0
    return jnp.minimum(t + 1, win).astype(f32)


def _pool_fwd(u, mix, pool_w, scale, seq, name):
    t_all, w = u.shape
    ng, rc = w // POOL_GROUP, min(POOL_CHUNK, seq)

    def body(u_ref, w_ref, s_ref, mix_in, p_ref, o_ref, pad):
        del mix_in
        pad[0:POOL_HALO, :] = jnp.zeros((POOL_HALO, POOL_GROUP), f32)
        for g in range(ng):
            cols = slice(g * POOL_GROUP, (g + 1) * POOL_GROUP)
            win = POOL_WINDOWS[g]
            pad[POOL_HALO:POOL_HALO + seq, :] = u_ref[:, cols]
            band = _band(rc, rc + POOL_HALO, POOL_HALO - win + 1, POOL_HALO + 1)
            wg = w_ref[g].astype(bf16)
            for r0 in range(0, seq, rc):
                ue = pad[r0:r0 + rc + POOL_HALO, :]
                hi, lo = _split_bf16(ue)
                sm = _dot(band, hi, NN) + _dot(band, lo, NN)
                pch = sm / _pool_counts(r0, rc, win) - ue[POOL_HALO:, :]
                pb = pch.astype(bf16)
                p_ref[r0:r0 + rc, cols] = pb
                o_ref[r0:r0 + rc, cols] = _dot(pb, wg, NN) * s_ref[:, cols]

    return pl.pallas_call(
        body, out_shape=(SDS((t_all, w), bf16), SDS(mix.shape, f32)), grid=(t_all // seq,),
        in_specs=[pl.BlockSpec((seq, w), lambda b: (b, 0)), pl.BlockSpec(pool_w.shape, lambda b: (0, 0, 0)),
                  pl.BlockSpec(scale.shape, lambda b: (0, 0)), ANY],
        out_specs=(pl.BlockSpec((seq, w), lambda b: (b, 0)), pl.BlockSpec((seq, w), lambda b: (b, 1))),
        scratch_shapes=[pltpu.VMEM((seq + POOL_HALO, POOL_GROUP), f32)],
        input_output_aliases={3: 1}, name=name, compiler_params=_params(1))(u, pool_w, scale, mix)


def _pool_bwd(dmix, p, pool_w, scale, seq, name):
    t_all, w = p.shape
    ng, rc = w // POOL_GROUP, min(POOL_CHUNK, seq)

    def body(dy_ref, p_ref, w_ref, s_ref, du_ref, dw_ref, ds_ref, dpn, dpr):
        b = pl.program_id(0)

        @pl.when(b == 0)
        def _():
            dw_ref[...] = jnp.zeros_like(dw_ref)
            ds_ref[...] = jnp.zeros_like(ds_ref)

        dpn[seq:seq + POOL_HALO, :] = jnp.zeros((POOL_HALO, POOL_GROUP), f32)
        for g in range(ng):
            cols = slice(g * POOL_GROUP, (g + 1) * POOL_GROUP)
            win = POOL_WINDOWS[g]
            wg = w_ref[g].astype(bf16)
            sg = s_ref[:, cols]
            dwg = jnp.zeros((POOL_GROUP, POOL_GROUP), f32)
            dsg = jnp.zeros((SUBLANES, POOL_GROUP), f32)
            for r0 in range(0, seq, rc):
                dy = dy_ref[r0:r0 + rc, cols]
                pb = p_ref[r0:r0 + rc, cols]
                dsg = dsg + _sum8(dy * _dot(pb, wg, NN))
                dyw = (dy * sg).astype(bf16)
                dwg = dwg + _dot(pb, dyw, TN)
                dp = _dot(dyw, wg, NT)
                dpr[r0:r0 + rc, :] = dp
                dpn[r0:r0 + rc, :] = dp / _pool_counts(r0, rc, win)
            dw_ref[g] += dwg
            ds_ref[:, cols] += dsg
            band = _band(rc, rc + POOL_HALO, 0, win)
            for r0 in range(0, seq, rc):
                hi, lo = _split_bf16(dpn[r0:r0 + rc + POOL_HALO, :])
                du = _dot(band, hi, NN) + _dot(band, lo, NN) - dpr[r0:r0 + rc, :]
                du_ref[r0:r0 + rc, cols] = du.astype(bf16)

    return pl.pallas_call(
        body, out_shape=(SDS((t_all, w), bf16), SDS(pool_w.shape, f32), SDS((SUBLANES, w), f32)), grid=(t_all // seq,),
        in_specs=[pl.BlockSpec((seq, w), lambda b: (b, 1)), pl.BlockSpec((seq, w), lambda b: (b, 0)),
                  pl.BlockSpec(pool_w.shape, lambda b: (0, 0, 0)), pl.BlockSpec(scale.shape, lambda b: (0, 0))],
        out_specs=(pl.BlockSpec((seq, w), lambda b: (b, 0)), pl.BlockSpec(pool_w.shape, lambda b: (0, 0, 0)),
                   pl.BlockSpec((SUBLANES, w), lambda b: (0, 0))),
        scratch_shapes=[pltpu.VMEM((seq + POOL_HALO, POOL_GROUP), f32), pltpu.VMEM((seq, POOL_GROUP), f32)],
        name=name, compiler_params=_params(1))(dmix, p, pool_w, scale)


XA_TQ = 512


def _xa_probs(qh, kh, dh):
    s = _dot(qh, kh, NT) * (dh ** -0.5)
    e = jnp.exp(s - jnp.max(s, axis=1, keepdims=True))
    return e / jnp.sum(e, axis=1, keepdims=True)


def _xa_fwd(q, kv, seq, name):
    t_all, d = q.shape
    nb = t_all // seq
    mem, dh, tq = kv.shape[0] // nb, d // XA_HEADS, min(XA_TQ, seq)
    nq = seq // tq

    def body(q_ref, kv_ref, o_ref):
        for h in range(XA_HEADS):
            cols = slice(h * dh, (h + 1) * dh)
            p = _xa_probs(q_ref[:, cols], kv_ref[:, cols], dh)
            o_ref[:, cols] = _dot(p.astype(bf16), kv_ref[:, d + h * dh:d + (h + 1) * dh], NN).astype(bf16)

    return pl.pallas_call(
        body, out_shape=SDS((t_all, d), bf16), grid=(nb, nq),
        in_specs=[pl.BlockSpec((tq, d), lambda b, i: (b * nq + i, 0)), pl.BlockSpec((mem, 2 * d), lambda b, i: (b, 0))],
        out_specs=pl.BlockSpec((tq, d), lambda b, i: (b * nq + i, 0)), name=name, compiler_params=_params(2))(q, kv)


def _xa_bwd(q, kv, do, seq, name):
    t_all, d = q.shape
    nb = t_all // seq
    mem, dh, tq = kv.shape[0] // nb, d // XA_HEADS, min(XA_TQ, seq)
    nq = seq // tq

    def body(q_ref, kv_ref, do_ref, dq_ref, dkv_ref):
        i = pl.program_id(1)

        @pl.when(i == 0)
        def _():
            dkv_ref[...] = jnp.zeros_like(dkv_ref)

        for h in range(XA_HEADS):
            cols = slice(h * dh, (h + 1) * dh)
            vcols = slice(d + h * dh, d + (h + 1) * dh)
            qh, kh, doh = q_ref[:, cols], kv_ref[:, cols], do_ref[:, cols]
            p = _xa_probs(qh, kh, dh)
            dkv_ref[:, vcols] += _dot(p.astype(bf16), doh, TN)
            dp = _dot(doh, kv_ref[:, vcols], NT)
            ds = (p * (dp - jnp.sum(dp * p, axis=1, keepdims=True)) * (dh ** -0.5)).astype(bf16)
            dq_ref[:, cols] = _dot(ds, kh, NN).astype(bf16)
            dkv_ref[:, cols] += _dot(ds, qh, TN)

    row = pl.BlockSpec((tq, d), lambda b, i: (b * nq + i, 0))
    kvs = pl.BlockSpec((mem, 2 * d), lambda b, i: (b, 0))
    return pl.pallas_call(body, out_shape=(SDS((t_all, d), bf16), SDS(kv.shape, f32)), grid=(nb, nq),
                          in_specs=[row, kvs, row], out_specs=(row, kvs), name=name, compiler_params=_params(2))(q, kv, do)


FFN_BR = 256
FFN_CHUNK = 256


def _conv3(ext, w_ref, b, cols, lo, rows):
    return (b + w_ref[2:3, cols] * ext[lo:lo + rows, :] + w_ref[1:2, cols] * ext[lo - 1:lo - 1 + rows, :]
            + w_ref[0:1, cols] * ext[lo - 2:lo - 2 + rows, :])


FFN_HALO = 16


def _ffn_gate_fwd(up, cw, cb, seq, name):
    t_all, f2 = up.shape
    ff, br, ch, hl = f2 // 2, min(FFN_BR, seq), FFN_CHUNK, FFN_HALO
    per_seq, hb = seq // br, br // hl

    def body(up_ref, halo_ref, cw_ref, cb_ref, o_ref, cv_ref, ev, eg):
        i = pl.program_id(0)
        keep = jnp.where(i % per_seq == 0, 0.0, 1.0)
        for c0 in range(0, ff, ch):
            convs = []
            for ext, off in ((ev, c0), (eg, ff + c0)):
                cols = slice(off, off + ch)
                ext[0:hl, :] = halo_ref[:, cols].astype(f32) * keep
                ext[hl:hl + br, :] = up_ref[:, cols].astype(f32)
                conv = _conv3(ext, cw_ref, cb_ref[:, cols], cols, hl, br)
                cv_ref[:, cols] = conv.astype(bf16)
                convs.append(conv)
            val, gate = convs
            o_ref[:, c0:c0 + ch] = (gate * _sigmoid(gate) * val).astype(bf16)

    return pl.pallas_call(
        body, out_shape=(SDS((t_all, ff), bf16), SDS((t_all, f2), bf16)), grid=(t_all // br,),
        in_specs=[pl.BlockSpec((br, f2), lambda i: (i, 0)),
                  pl.BlockSpec((hl, f2), lambda i: (jnp.maximum(i * hb - 1, 0), 0)),
                  pl.BlockSpec(cw.shape, lambda i: (0, 0)), pl.BlockSpec(cb.shape, lambda i: (0, 0))],
        out_specs=(pl.BlockSpec((br, ff), lambda i: (i, 0)), pl.BlockSpec((br, f2), lambda i: (i, 0))),
        scratch_shapes=[pltpu.VMEM((br + hl, ch), f32), pltpu.VMEM((br + hl, ch), f32)],
        name=name, compiler_params=_params(1))(up, up, cw, cb)


def _ffn_gate_bwd(dact, up, cv, cw, seq, name):
    t_all, f2 = up.shape
    ff, br, ch, hl = f2 // 2, min(FFN_BR, seq), FFN_CHUNK, FFN_HALO
    per_seq, hb, last = seq // br, br // hl, t_all // hl - 1
    ext_rows = br + SUBLANES

    def body(da_ref, dan_ref, cv_ref, cvn_ref, up_ref, upp_ref, cw_ref, du_ref, dcw_ref, dcb_ref, ext, e1, e2, e3, dcv, dcg):
        i = pl.program_id(0)

        @pl.when(i == 0)
        def _():
            dcw_ref[...] = jnp.zeros_like(dcw_ref)
            dcb_ref[...] = jnp.zeros_like(dcb_ref)

        keep_prev = jnp.where(i % per_seq == 0, 0.0, 1.0)
        keep_next = jnp.where((i + 1) % per_seq == 0, 0.0, 1.0)

        def with_next(scr, blk_ref, nxt_ref, cols, scale):
            scr[0:br, :] = blk_ref[:, cols].astype(f32)
            scr[br:br + hl, :] = nxt_ref[:, cols].astype(f32) * scale
            return scr[0:ext_rows, :]

        for c0 in range(0, ff, ch):
            da = with_next(e1, da_ref, dan_ref, slice(c0, c0 + ch), keep_next)
            val = with_next(e2, cv_ref, cvn_ref, slice(c0, c0 + ch), 1.0)
            gate = with_next(e3, cv_ref, cvn_ref, slice(ff + c0, ff + c0 + ch), 1.0)
            sg = _sigmoid(gate)
            dcv[...] = da * gate * sg
            dcg[...] = da * val * sg * (1.0 + gate * (1.0 - sg))
            for dc, off in ((dcv, c0), (dcg, ff + c0)):
                cols = slice(off, off + ch)
                du = (cw_ref[2:3, cols] * dc[0:br, :] + cw_ref[1:2, cols] * dc[1:br + 1, :]
                      + cw_ref[0:1, cols] * dc[2:br + 2, :])
                du_ref[:, cols] = du.astype(bf16)
                d0 = dc[0:br, :]
                dcb_ref[:, cols] += _sum8(d0)
                ext[0:hl, :] = upp_ref[:, cols].astype(f32) * keep_prev
                ext[hl:hl + br, :] = up_ref[:, cols].astype(f32)
                for tap in range(3):
                    lo = hl - (2 - tap)
                    dcw_ref[tap, :, cols] += _sum8(d0 * ext[lo:lo + br, :])

    blk = lambda n: pl.BlockSpec((br, n), lambda i: (i, 0))
    prev = lambda n: pl.BlockSpec((hl, n), lambda i: (jnp.maximum(i * hb - 1, 0), 0))
    nxt = lambda n: pl.BlockSpec((hl, n), lambda i: (jnp.minimum((i + 1) * hb, last), 0))
    return pl.pallas_call(
        body, out_shape=(SDS((t_all, f2), bf16), SDS((3, SUBLANES, f2), f32), SDS((SUBLANES, f2), f32)), grid=(t_all // br,),
        in_specs=[blk(ff), nxt(ff), blk(f2), nxt(f2), blk(f2), prev(f2), pl.BlockSpec(cw.shape, lambda i: (0, 0))],
        out_specs=(blk(f2), pl.BlockSpec((3, SUBLANES, f2), lambda i: (0, 0, 0)), pl.BlockSpec((SUBLANES, f2), lambda i: (0, 0))),
        scratch_shapes=[pltpu.VMEM((br + hl, ch), f32)] * 4 + [pltpu.VMEM((ext_rows, ch), f32)] * 2,
        name=name, compiler_params=_params(1))(dact, dact, cv, cv, up, up, cw)


SSM_GB = 8
SSM_PLANES = 8
SSM_ROWS = 256
SSM_UNROLL = 8


def _ssm_pitch(seq):
    p = seq + SUBLANES
    assert (p // SUBLANES) % 2 == 1
    return p


def _rows(base, rc):
    return pl.ds(pl.multiple_of(base + rc * SSM_ROWS, SUBLANES), SSM_ROWS)


def _ssm_project_in(u_ref, b_ref, planes, e, seq, pitch):
    def chunk(rc, _):
        uc = u_ref[_rows(e * seq, rc), :].astype(bf16)
        for j in range(SSM_PLANES):
            planes[_rows(j * pitch, rc), :] = _dot(uc, b_ref[:, j * LANES:(j + 1) * LANES], NN)
        return 0
    lax.fori_loop(0, seq // SSM_ROWS, chunk, 0)


def _ssm_rows(planes, rc, pitch):
    return jnp.concatenate([planes[_rows(j * pitch, rc), :].astype(bf16) for j in range(SSM_PLANES)], axis=1)


def _ssm_scan(planes_list, l1, l2, seq, pitch, reverse=False):
    def step(s, hs):
        hs = list(hs)
        for k in range(SSM_UNROLL):
            t = s * SSM_UNROLL + k
            t = seq - 1 - t if reverse else t
            for e, planes in enumerate(planes_list):
                hs[e] = hs[e] * l1 + pltpu.roll(hs[e], 4, 0) * l2 + planes[pl.ds(t, SUBLANES, stride=pitch), :]
                planes[pl.ds(t, SUBLANES, stride=pitch), :] = hs[e]
        return tuple(hs)
    zero = jnp.zeros((SUBLANES, LANES), f32)
    lax.fori_loop(0, seq // SSM_UNROLL, step, tuple(zero for _ in planes_list))


def _ssm_fwd(u, b_big, c_big, lslab, dskip, seq, name, comm=None):
    t_all, w = u.shape
    nb, gw, pitch = t_all // seq, SSM_GB * SSM_GROUP, _ssm_pitch(seq)
    assert gw == LANES

    def body(u_ref, b_ref, c_ref, l_ref, d_ref, y_ref, *planes):
        l1, l2 = l_ref[0:SUBLANES, :], l_ref[SUBLANES:2 * SUBLANES, :]
        for e in range(nb):
            _ssm_project_in(u_ref, b_ref, planes[e], e, seq, pitch)
        _ssm_scan(planes, l1, l2, seq, pitch)
        for e in range(nb):
            def chunk(rc, _, e=e):
                rows = _rows(e * seq, rc)
                y_ref[rows, :] = _dot(_ssm_rows(planes[e], rc, pitch), c_ref[...], NN) + d_ref[...] * u_ref[rows, :]
                return 0
            lax.fori_loop(0, seq // SSM_ROWS, chunk, 0)

    (y,), extra = _call(
        body, ins=[u, b_big, c_big, lslab, dskip], out_shape=[SDS((t_all, w), f32)], grid=(w // gw,),
        in_specs=[pl.BlockSpec((t_all, gw), lambda k: (0, k)), pl.BlockSpec((None,) + b_big.shape[1:], lambda k: (k, 0, 0)),
                  pl.BlockSpec((None,) + c_big.shape[1:], lambda k: (k, 0, 0)),
                  pl.BlockSpec((None,) + lslab.shape[1:], lambda k: (k, 0, 0)), pl.BlockSpec((1, gw), lambda k: (0, k))],
        out_specs=[pl.BlockSpec((t_all, gw), lambda k: (0, k))],
        scratch_shapes=[pltpu.VMEM((SSM_PLANES * pitch, LANES), f32) for _ in range(nb)], name=name, comm=comm)
    return y, extra


def _ssm_bwd(u, dy, b_big, c_big, lslab, dskip, seq, name, comm=None):
    t_all, w = u.shape
    nb, gw, pitch = t_all // seq, SSM_GB * SSM_GROUP, _ssm_pitch(seq)

    def body(u_ref, dy_ref, b_ref, c_ref, l_ref, d_ref, du_ref, db_ref, dc_ref, dl_ref, dd_ref, *planes):
        hp, ap = planes[:nb], planes[nb:]
        l1, l2 = l_ref[0:SUBLANES, :], l_ref[SUBLANES:2 * SUBLANES, :]
        for e in range(nb):
            _ssm_project_in(u_ref, b_ref, hp[e], e, seq, pitch)
        _ssm_scan(hp, l1, l2, seq, pitch)
        dd_ref[...] = jnp.zeros_like(dd_ref)
        dc_ref[...] = jnp.zeros_like(dc_ref)
        db_ref[...] = jnp.zeros_like(db_ref)
        for e in range(nb):
            def chunk(rc, _, e=e):
                rows = _rows(e * seq, rc)
                dyc = dy_ref[rows, :]
                dyb = dyc.astype(bf16)
                for j in range(SSM_PLANES):
                    ap[e][_rows(j * pitch, rc), :] = _dot(dyb, c_ref[j * LANES:(j + 1) * LANES, :], NT)
                dd_ref[...] += _sum8(dyc * u_ref[rows, :])
                dc_ref[...] += _dot(_ssm_rows(hp[e], rc, pitch), dyb, TN)
                return 0
            lax.fori_loop(0, seq // SSM_ROWS, chunk, 0)

        def step(s, carry):
            carry = [list(c) for c in carry]
            for k in range(SSM_UNROLL):
                t = seq - 1 - (s * SSM_UNROLL + k)
                for e in range(nb):
                    a, s1, s2 = carry[e]
                    a = a * l1 - pltpu.roll(a, 4, 0) * l2 + ap[e][pl.ds(t, SUBLANES, stride=pitch), :]
                    ap[e][pl.ds(t, SUBLANES, stride=pitch), :] = a
                    hprev = hp[e][pl.ds(jnp.maximum(t - 1, 0), SUBLANES, stride=pitch), :] * jnp.where(t > 0, 1.0, 0.0)
                    carry[e] = [a, s1 + a * hprev, s2 + a * pltpu.roll(hprev, 4, 0)]
            return tuple(tuple(c) for c in carry)
        zero = jnp.zeros((SUBLANES, LANES), f32)
        fin = lax.fori_loop(0, seq // SSM_UNROLL, step, tuple((zero, zero, zero) for _ in range(nb)))
        dl_ref[0:SUBLANES, :] = sum(f[1] for f in fin)
        dl_ref[SUBLANES:2 * SUBLANES, :] = sum(f[2] for f in fin)

        for e in range(nb):
            def chunk2(rc, _, e=e):
                rows = _rows(e * seq, rc)
                ar = _ssm_rows(ap[e], rc, pitch)
                du_ref[rows, :] = (_dot(ar, b_ref[...], NT) + d_ref[...] * dy_ref[rows, :]).astype(bf16)
                db_ref[...] += _dot(u_ref[rows, :].astype(bf16), ar, TN)
                return 0
            lax.fori_loop(0, seq // SSM_ROWS, chunk2, 0)

    col = pl.BlockSpec((t_all, gw), lambda k: (0, k))
    per = lambda s: pl.BlockSpec((None,) + s[1:], lambda k: (k, 0, 0))
    ng = w // gw
    res, extra = _call(
        body, ins=[u, dy, b_big, c_big, lslab, dskip],
        out_shape=[SDS((t_all, w), bf16), SDS(b_big.shape, f32), SDS(c_big.shape, f32), SDS((ng, 2 * SUBLANES, LANES), f32),
                   SDS((SUBLANES, w), f32)],
        grid=(ng,),
        in_specs=[col, col, per(b_big.shape), per(c_big.shape), per(lslab.shape), pl.BlockSpec((1, gw), lambda k: (0, k))],
        out_specs=[col, per(b_big.shape), per(c_big.shape), per((ng, 2 * SUBLANES, LANES)), pl.BlockSpec((SUBLANES, gw), lambda k: (0, k))],
        scratch_shapes=[pltpu.VMEM((SSM_PLANES * pitch, LANES), f32) for _ in range(2 * nb)], name=name, comm=comm)
    return (*res, extra)


def _ssm_disc_fwd(lam_re, lam_im, dt, b_re, b_im, name):
    def body(a_ref, b_ref, dt_ref, br_ref, bi_ref, lr_ref, li_ref, cr_ref, ci_ref, bbr_ref, bbi_ref):
        a, b, dtv = a_ref[...], b_ref[...], dt_ref[...]
        mag, ang = jnp.exp(a * dtv), b * dtv
        lr, li = mag * jnp.cos(ang), mag * jnp.sin(ang)
        nr, den = lr - 1.0, a * a + b * b
        cr, ci = (nr * a + li * b) / den, (li * a - nr * b) / den
        lr_ref[...], li_ref[...], cr_ref[...], ci_ref[...] = lr, li, cr, ci
        bbr_ref[...] = cr * br_ref[...] - ci * bi_ref[...]
        bbi_ref[...] = cr * bi_ref[...] + ci * br_ref[...]
    c, m = SDS(lam_re.shape, f32), SDS(b_re.shape, f32)
    return pl.pallas_call(body, out_shape=(c, c, c, c, m, m), name=name)(lam_re, lam_im, dt, b_re, b_im)


def _ssm_disc_bwd(lam_re, lam_im, dt, b_re, b_im, g_lr, g_li, g_bbr, g_bbi, name):
    def body(a_ref, b_ref, dt_ref, br_ref, bi_ref, glr_ref, gli_ref, gbr_ref, gbi_ref, da_ref, db_ref, ddt_ref, dbr_ref, dbi_ref):
        a, b, dtv = a_ref[...], b_ref[...], dt_ref[...]
        mag, ang = jnp.exp(a * dtv), b * dtv
        cs, sn = jnp.cos(ang), jnp.sin(ang)
        lr, li = mag * cs, mag * sn
        nr, den = lr - 1.0, a * a + b * b
        cr, ci = (nr * a + li * b) / den, (li * a - nr * b) / den
        gbr, gbi, brv, biv = gbr_ref[...], gbi_ref[...], br_ref[...], bi_ref[...]
        dbr_ref[...] = cr * gbr + ci * gbi
        dbi_ref[...] = cr * gbi - ci * gbr
        dcr = jnp.sum(brv * gbr + biv * gbi, axis=1, keepdims=True)
        dci = jnp.sum(brv * gbi - biv * gbr, axis=1, keepdims=True)
        dnum_r, dnum_i = dcr / den, dci / den
        dden = -(dcr * cr + dci * ci) / den
        dnr = dnum_r * a - dnum_i * b
        dli = gli_ref[...] + dnum_r * b + dnum_i * a
        dlr = glr_ref[...] + dnr
        dmag, dang = dlr * cs + dli * sn, dli * lr - dlr * li
        dadt = dmag * mag
        da_ref[...] = dnum_r * nr + dnum_i * li + dden * 2.0 * a + dadt * dtv
        db_ref[...] = dnum_r * li - dnum_i * nr + dden * 2.0 * b + dang * dtv
        ddt_ref[...] = dadt * a + dang * b
    c, m = SDS(lam_re.shape, f32), SDS(b_re.shape, f32)
    return pl.pallas_call(body, out_shape=(c, c, c, m, m), name=name)(lam_re, lam_im, dt, b_re, b_im, g_lr, g_li, g_bbr, g_bbi)


def _place():
    x, y, c = lax.axis_index("x"), lax.axis_index("y"), lax.axis_index("c")
    return x, y, c, 2 * x + y


def _half_axis(shape, ax):
    return 0 if shape[0] == 2 else (3 - ax)


def _sub(ref, axis, start, size):
    idx = [slice(None)] * len(ref.shape)
    idx[axis] = pl.ds(start, size)
    return ref.at[tuple(idx)]


def _region(ref, full_shape, ax, slot=None, half=None):
    if slot is not None:
        n = full_shape[ax] // N_CHIPS
        ref = _sub(ref, ax, slot * n, n)
    if half is not None:
        ha = _half_axis(full_shape, ax)
        n = full_shape[ha] // 2
        ref = _sub(ref, ha, half * n, n)
    return ref


def _halved(shape, axis):
    return tuple(s // 2 if a == axis else s for a, s in enumerate(shape))


class _Comm:
    def __init__(self, ins, out_shapes, aliases, scratch, start, finish):
        self.ins, self.out_shapes, self.aliases, self.scratch, self.start, self.finish = ins, out_shapes, aliases, scratch, start, finish


def _call(body, *, ins, in_specs, out_shape, out_specs, grid, scratch_shapes, name, comm=None):
    if comm is None:
        res = pl.pallas_call(body, out_shape=tuple(out_shape), grid=grid, in_specs=list(in_specs), out_specs=tuple(out_specs),
                             scratch_shapes=list(scratch_shapes), name=name, compiler_params=_params(len(grid)))(*ins)
        return list(res), []
    n_in, n_out, n_scr, c_in, c_out = len(ins), len(out_shape), len(scratch_shapes), len(comm.ins), len(comm.out_shapes)

    def fused(*refs):
        pos = [n_in, n_in + c_in, n_in + c_in + n_out, n_in + c_in + n_out + c_out, n_in + c_in + n_out + c_out + n_scr]
        in_refs, cin, out_refs, cout, scr, cscr = (refs[:pos[0]], refs[pos[0]:pos[1]], refs[pos[1]:pos[2]], refs[pos[2]:pos[3]],
                                                   refs[pos[3]:pos[4]], refs[pos[4]:])
        ids = [pl.program_id(a) for a in range(len(grid))]
        first, last = ids[0] == 0, ids[0] == grid[0] - 1
        for a in range(1, len(grid)):
            first, last = first & (ids[a] == 0), last & (ids[a] == grid[a] - 1)

        @pl.when(first)
        def _():
            comm.start(cin, cout, cscr)

        body(*in_refs, *out_refs, *scr)

        @pl.when(last)
        def _():
            comm.finish(cin, cout, cscr)

    res = pl.pallas_call(
        fused, out_shape=tuple(out_shape) + tuple(comm.out_shapes), grid=grid, in_specs=list(in_specs) + [ANY] * c_in,
        out_specs=tuple(out_specs) + tuple([ANY] * c_out), scratch_shapes=list(scratch_shapes) + list(comm.scratch),
        input_output_aliases={n_in + i: n_out + o for i, o in comm.aliases}, name=name, compiler_params=_params(len(grid)))(*ins, *comm.ins)
    return list(res[:n_out]), list(res[n_out:])


def _comm_only(comm, name):
    c_in, c_out = len(comm.ins), len(comm.out_shapes)

    def body(*refs):
        cin, cout, cscr = refs[:c_in], refs[c_in:c_in + c_out], refs[c_in + c_out:]
        comm.start(cin, cout, cscr)
        comm.finish(cin, cout, cscr)

    return pl.pallas_call(body, out_shape=tuple(comm.out_shapes), in_specs=[ANY] * c_in, out_specs=tuple([ANY] * c_out),
                          scratch_shapes=list(comm.scratch), input_output_aliases=dict(comm.aliases), name=name)(*comm.ins)


def _gather_plan(shards, axes):
    n = len(shards)
    fulls = [tuple(s * N_CHIPS if a == ax else s for a, s in enumerate(sh.shape)) for sh, ax in zip(shards, axes)]
    own = 6

    def copies(src, dst, scr, sends_only=False):
        send_sems, recv_sems = scr
        x, y, c, p = _place()
        chips = [(1 - x, y), (x, 1 - y), (1 - x, 1 - y)]
        slots = [2 * cx + cy for cx, cy in chips]

        def copy(a, k, slot, half, to, from_shard=False):
            where = _region(dst[a], fulls[a], axes[a], slot, half)
            source = where
            if from_shard:
                ha = _half_axis(fulls[a], axes[a])
                hn = fulls[a][ha] // 2
                source = _sub(src[a], ha, half * hn, hn)
            return pltpu.make_async_remote_copy(src_ref=source, dst_ref=where, send_sem=send_sems.at[a, k],
                                                recv_sem=recv_sems.at[a, k], device_id=to, device_id_type=MESH)

        parts = range(n)
        mine = [pltpu.make_async_remote_copy(src_ref=src[a], dst_ref=_region(dst[a], fulls[a], axes[a], p),
                                             send_sem=send_sems.at[a, own], recv_sem=recv_sems.at[a, own],
                                             device_id=(x, y, 1 - c), device_id_type=MESH) for a in parts]
        first = [copy(a, j, p, c, (*chips[j], c), True) for a in parts for j in range(3)]
        if sends_only:
            return mine, first
        landed = [copy(a, j, slots[j], c, (x, y, c)) for a in parts for j in range(3)]
        passed = [copy(a, 3 + j, slots[j], c, (x, y, 1 - c)) for a in parts for j in range(3)]
        handed = [copy(a, 3 + j, slots[j], 1 - c, (x, y, c)) for a in parts for j in range(3)]
        return mine, first, landed, passed, handed

    def start(src, dst, scr):
        mine, first = copies(src, dst, scr, sends_only=True)
        for cp in first + mine:
            cp.start()

    def finish(src, dst, scr):
        mine, first, landed, passed, handed = copies(src, dst, scr)
        for arrived, fwd in zip(landed, passed):
            arrived.wait_recv()
            fwd.start()
        for cp in handed + mine:
            cp.wait_recv()
        for cp in first + passed + mine:
            cp.wait_send()

    return _Comm(list(shards), [SDS(f, s.dtype) for f, s in zip(fulls, shards)], [],
                 [pltpu.SemaphoreType.DMA((n, 7)), pltpu.SemaphoreType.DMA((n, 7))], start, finish)


def _all_gather(shards, axes, name):
    return _comm_only(_gather_plan(shards, axes), name)


def _swap_halves(grads, axes, name):
    n = len(grads)
    shapes = [g.shape for g in grads]

    def body(*refs):
        src, dst = refs[:n], refs[n:2 * n]
        send_sems, recv_sems = refs[2 * n:]
        x, y, c, _ = _place()
        cps = [pltpu.make_async_remote_copy(src_ref=_region(src[a], shapes[a], axes[a], None, 1 - c), dst_ref=dst[a],
                                            send_sem=send_sems.at[a], recv_sem=recv_sems.at[a],
                                            device_id=(x, y, 1 - c), device_id_type=MESH) for a in range(n)]
        for cp in cps:
            cp.start()
        for cp in cps:
            cp.wait()

    outs = tuple(SDS(_halved(s, _half_axis(s, ax)), g.dtype) for s, ax, g in zip(shapes, axes, grads))
    return pl.pallas_call(body, out_shape=outs, in_specs=[ANY] * n, out_specs=tuple([ANY] * n),
                          scratch_shapes=[pltpu.SemaphoreType.DMA((n,)), pltpu.SemaphoreType.DMA((n,))], name=name)(*grads)


def _row_block(rows, row_bytes, limit=3 << 20):
    for b in (1024, 512, 256, 128, 64, 32, 16, 8):
        if rows % b == 0 and b * row_bytes <= limit:
            return b
    return rows


def _add_own_half(g, other, ax, cidx, name):
    _, kp, np_ = other.shape
    ha = _half_axis(g.shape, ax)
    ks, ns = (kp // N_CHIPS, np_) if ax == 1 else (kp, np_ // N_CHIPS)
    bk = _row_block(ks, ns * 4)
    nkb = ks // bk

    def g_map(q, i, cref):
        c = cref[0]
        if ax == 1:
            return (c, q * nkb + i, 0) if ha == 0 else (0, q * nkb + i, c)
        return (c, i, q) if ha == 0 else (0, c * nkb + i, q)

    def o_map(q, i, cref):
        return (0, q * nkb + i, 0) if ax == 1 else (0, i, q)

    def body(c_ref, g_ref, o_ref, send_ref, land_ref):
        del c_ref
        s = (g_ref[...].astype(f32) + o_ref[...].astype(f32)).astype(send_ref.dtype)
        send_ref[...] = s
        land_ref[...] = s

    out = pl.BlockSpec((None, bk, ns), lambda q, i, cref: (q, i, 0))
    grid_spec = pltpu.PrefetchScalarGridSpec(
        num_scalar_prefetch=1, grid=(N_CHIPS, nkb),
        in_specs=[pl.BlockSpec((None, bk, ns), g_map), pl.BlockSpec((None, bk, ns), o_map)], out_specs=(out, out))
    shape = SDS((N_CHIPS, ks, ns), g.dtype)
    return pl.pallas_call(body, out_shape=(shape, shape), grid_spec=grid_spec, name=name, compiler_params=_params(2))(cidx, g, other)


def _owner_plan(sends, lands):
    n = len(sends)

    def copies(cin, dst, scr, arrivals):
        src = cin[:n]
        send_sems, recv_sems = scr
        x, y, c, p = _place()
        chips = [(1 - x, y), (x, 1 - y), (1 - x, 1 - y)]
        slots = [2 * cx + cy for cx, cy in chips]
        if arrivals:
            return [pltpu.make_async_remote_copy(src_ref=src[a].at[p], dst_ref=dst[a].at[slots[j]], send_sem=send_sems.at[a, j],
                                                 recv_sem=recv_sems.at[a, j], device_id=(x, y, c), device_id_type=MESH)
                    for a in range(n) for j in range(3)]
        return [pltpu.make_async_remote_copy(src_ref=src[a].at[slots[j]], dst_ref=dst[a].at[p], send_sem=send_sems.at[a, j],
                                             recv_sem=recv_sems.at[a, j], device_id=(*chips[j], c), device_id_type=MESH)
                for a in range(n) for j in range(3)]

    def start(cin, dst, scr):
        for cp in copies(cin, dst, scr, False):
            cp.start()

    def finish(cin, dst, scr):
        for cp in copies(cin, dst, scr, True):
            cp.wait_recv()
        for cp in copies(cin, dst, scr, False):
            cp.wait_send()

    return _Comm(list(sends) + list(lands), [SDS(l.shape, l.dtype) for l in lands], [(n + a, a) for a in range(n)],
                 [pltpu.SemaphoreType.DMA((n, 3)), pltpu.SemaphoreType.DMA((n, 3))], start, finish)


def _sum_chips(stack, shard_shape, ax, cidx, name):
    _, ks, ns = stack.shape
    ha = _half_axis(shard_shape, ax)
    bk = _row_block(ks, ns * 4 * N_CHIPS)
    nkb = ks // bk

    def o_map(i, cref):
        c = cref[0]
        return (c, i, 0) if ha == 0 else ((0, c * nkb + i, 0) if ha == 1 else (0, i, c))

    def body(c_ref, s_ref, o_ref):
        del c_ref
        acc = s_ref[0].astype(f32)
        for q in range(1, N_CHIPS):
            acc = acc + s_ref[q].astype(f32)
        o_ref[...] = acc

    grid_spec = pltpu.PrefetchScalarGridSpec(
        num_scalar_prefetch=1, grid=(nkb,), in_specs=[pl.BlockSpec((N_CHIPS, bk, ns), lambda i, cref: (0, i, 0))],
        out_specs=pl.BlockSpec((None, bk, ns), o_map))
    return pl.pallas_call(body, out_shape=SDS(shard_shape, f32), grid_spec=grid_spec, name=name, compiler_params=_params(1))(cidx, stack)


def _join_halves(slices, axes, name):
    n = len(slices)

    def body(*refs):
        dst = refs[n:2 * n]
        send_sems, recv_sems = refs[2 * n:]
        x, y, c, _ = _place()

        def half(a, h):
            ha = _half_axis(slices[a].shape, axes[a])
            hn = slices[a].shape[ha] // 2
            return _sub(dst[a], ha, h * hn, hn)

        cps = [pltpu.make_async_remote_copy(src_ref=half(a, c), dst_ref=half(a, c), send_sem=send_sems.at[a], recv_sem=recv_sems.at[a],
                                            device_id=(x, y, 1 - c), device_id_type=MESH) for a in range(n)]
        for cp in cps:
            cp.start()
        for a in range(n):
            pltpu.make_async_remote_copy(src_ref=half(a, c), dst_ref=half(a, 1 - c), send_sem=send_sems.at[a], recv_sem=recv_sems.at[a],
                                         device_id=(x, y, c), device_id_type=MESH).wait_recv()
        for cp in cps:
            cp.wait_send()

    return pl.pallas_call(
        body, out_shape=tuple(SDS(s.shape, s.dtype) for s in slices), in_specs=[ANY] * n, out_specs=tuple([ANY] * n),
        scratch_shapes=[pltpu.SemaphoreType.DMA((n,)), pltpu.SemaphoreType.DMA((n,))],
        input_output_aliases={a: a for a in range(n)}, name=name)(*slices)


def _core_index():
    return jnp.reshape(lax.axis_index("c"), (1,)).astype(jnp.int32)


def _reduce_begin(grads, axes, tag):
    cidx = _core_index()
    others = _swap_halves(grads, axes, f"rs_swap_{tag}")
    pairs = [_add_own_half(g, o, ax, cidx, f"rs_add_{tag}_{a}") for a, (g, o, ax) in enumerate(zip(grads, others, axes))]
    return _owner_plan([s for s, _ in pairs], [l for _, l in pairs])


def _reduce_end(stacks, shapes, axes, tag):
    cidx = _core_index()
    shard_shapes = [tuple(s // N_CHIPS if i == ax else s for i, s in enumerate(sh)) for sh, ax in zip(shapes, axes)]
    slices = [_sum_chips(s, sh, ax, cidx, f"rs_sum_{tag}_{a}") for a, (s, sh, ax) in enumerate(zip(stacks, shard_shapes, axes))]
    return _join_halves(slices, axes, f"rs_join_{tag}")


SMALL_COLS = 256


def _pack(arrays, rows_multiple):
    flat = jnp.concatenate([a.reshape(-1).astype(f32) for a in arrays])
    rows = -(-flat.shape[0] // SMALL_COLS)
    rows = -(-rows // rows_multiple) * rows_multiple
    flat = jnp.pad(flat, (0, rows * SMALL_COLS - flat.shape[0]))
    return flat.reshape(1, rows, SMALL_COLS)


def _unpack(buf, shapes):
    flat, out, off = buf.reshape(-1), [], 0
    for s in shapes:
        n = math.prod(s)
        out.append(flat[off:off + n].reshape(s))
        off += n
    return out


def _block_diag_in(bb):
    g, p, c = bb.shape
    k = g // SSM_GB
    eye = jnp.eye(SSM_GB, dtype=bb.dtype)
    return jnp.einsum("kgpc,gh->kgchp", bb.reshape(k, SSM_GB, p, c), eye).reshape(k, SSM_GB * c, SSM_GB * p)


def _block_diag_out(cc):
    g, c, p = cc.shape
    k = g // SSM_GB
    eye = jnp.eye(SSM_GB, dtype=cc.dtype)
    return jnp.einsum("kgcp,gh->kgphc", cc.reshape(k, SSM_GB, c, p), eye).reshape(k, SSM_GB * p, SSM_GB * c)


def _diag_in(db, p, c):
    k = db.shape[0]
    return jnp.einsum("kgcgp->kgpc", db.reshape(k, SSM_GB, c, SSM_GB, p)).reshape(k * SSM_GB, p, c)


def _diag_out(dc, p, c):
    k = dc.shape[0]
    return jnp.einsum("kgpgc->kgcp", dc.reshape(k, SSM_GB, p, SSM_GB, c)).reshape(k * SSM_GB, c, p)


def _state_slab(v):
    g, p = v.shape
    return v.reshape(g // SSM_GB, SSM_GB * p // LANES, LANES)


BIG = ("ab_w_in", "ab_w_out", "ssm_w_in", "ssm_w_glu", "xa_w_q", "xa_w_kv", "xa_w_o", "ffn_w_up", "ffn_w_down")
BIG_AXIS = dict(ab_w_in=2, ab_w_out=1, ssm_w_in=1, ssm_w_glu=2, xa_w_q=1, xa_w_kv=2, xa_w_o=1, ffn_w_up=2, ffn_w_down=1)
SMALL_REPL = ("norm_mix", "norm_xattn", "norm_ffn", "norm_mem", "norm_final", "pool_w", "pool_scale", "ssm_lam_re", "ssm_lam_im",
              "ssm_log_dt", "ssm_b_re", "ssm_b_im", "ssm_c_re", "ssm_c_im", "ffn_conv_b")
SMALL_SHARDED = ("ssm_d", "ffn_conv_w")
FIRST_MIXER = ("ab_w_in", "ab_w_out")
WEIGHTS = ("norm_mix", "norm_xattn", "norm_ffn", "norm_mem", "norm_final", "ab_w_in", "pool_w", "pool_scale", "ab_w_out", "ssm_w_in",
           "ssm_lam_re", "ssm_lam_im", "ssm_log_dt", "ssm_b_re", "ssm_b_im", "ssm_c_re", "ssm_c_im", "ssm_d", "ssm_w_glu", "xa_w_q",
           "xa_w_kv", "xa_w_o", "ffn_w_up", "ffn_conv_w", "ffn_conv_b", "ffn_w_down")


class _Reducer:
    def __init__(self):
        self.done, self.groups = {}, 0

    def begin(self, keys, gw):
        self.groups += 1
        return _reduce_begin([gw[k] for k in keys], [BIG_AXIS.get(k[0], 1) for k in keys], f"g{self.groups}")

    def end(self, keys, gw, stacks):
        slices = _reduce_end(stacks, [gw[k].shape for k in keys], [BIG_AXIS.get(k[0], 1) for k in keys], f"g{self.groups}")
        self.done.update(zip(keys, slices))


def _local_step(xf, memf, tgt, w, wf, conv_w, ssm_d, seq, late_weights=None, reducer=None):
    d = xf.shape[1]
    depth = w["norm_mix"].shape[0]
    wf = dict(wf)
    late_weights = late_weights or {}
    sbw = wf["ab_w_in", 0].shape[2] // 4
    row = lambda a: a.reshape(1, -1)

    gs, ps = w["ssm_lam_re"].shape[1:]
    col = lambda a: a.reshape(gs * ps, 1)
    lam_re, lam_im = col(w["ssm_lam_re"][0]), col(w["ssm_lam_im"][0])
    dt = col(jnp.broadcast_to(jnp.exp(w["ssm_log_dt"][0])[:, None], (gs, ps)))
    b_re, b_im = w["ssm_b_re"][0].reshape(gs * ps, -1), w["ssm_b_im"][0].reshape(gs * ps, -1)
    lb_re, lb_im, _, _, bb_re, bb_im = _ssm_disc_fwd(lam_re, lam_im, dt, b_re, b_im, "ssm_disc")
    cgrp = b_re.shape[1]
    b_big = jnp.concatenate([_block_diag_in(bb_re.reshape(gs, ps, cgrp)), _block_diag_in(bb_im.reshape(gs, ps, cgrp))], axis=2).astype(bf16)
    c_big = jnp.concatenate([_block_diag_out(w["ssm_c_re"][0]), -_block_diag_out(w["ssm_c_im"][0])], axis=1).astype(bf16)
    lr_s, li_s = _state_slab(lb_re.reshape(gs, ps)), _state_slab(lb_im.reshape(gs, ps))
    lslab = jnp.concatenate([lr_s, lr_s, -li_s, li_s], axis=1)

    mem_n = _norm_fwd(memf, row(w["norm_mem"]), "norm_mem")
    kv = [None] * depth
    xs, saved = [xf], []
    cur = xf
    h_next = _norm_fwd(cur, row(w["norm_mix"][0]), "norm_mix0")
    for l in range(depth):
        sv = {}
        h = h_next
        sv["h"] = h
        if l % 2 == 0:
            qkv = _mm(h, wf["ab_w_in", 0], mode="nn", b_l=0, n=3 * sbw, out_dtype=bf16, name=f"qkv{l}")
            u = _mm(h, wf["ab_w_in", 0], mode="nn", b_l=0, b_n0=3 * sbw, n=sbw, out_dtype=f32, name=f"poolin{l}")
            plan, names = late_weights.get(f"sb_fwd{l}", (None, ()))
            mix, ltot, first, late = _sb_fwd(qkv, seq, f"sb_fwd{l}", comm=plan)
            wf.update(zip(names, late))
            pooled, mix = _pool_fwd(u, mix, w["pool_w"][0], w["pool_scale"], seq, f"pool_fwd{l}")
            sv.update(qkv=qkv, mix=mix, ltot=ltot, first=first, pooled=pooled)
            cur, hx = _mm(mix, wf["ab_w_out", 0], mode="nn", b_l=0, res=cur, out_dtype=f32, name=f"mixout{l}",
                          norm=("fwd", row(w["norm_xattn"][l])))
        else:
            us = _mm(h, wf["ssm_w_in", 0], mode="nn", b_l=0, out_dtype=f32, name=f"ssmin{l}")
            plan, names = late_weights.get(f"ssm_fwd{l}", (None, ()))
            ys, late = _ssm_fwd(us, b_big, c_big, lslab, ssm_d, seq, f"ssm_fwd{l}", comm=plan)
            wf.update(zip(names, late))
            gl = _gelu_fwd(ys, f"gelu{l}")
            glu = _mm(gl, wf["ssm_w_glu", 0], mode="nn", b_l=0, out_dtype=f32, name=f"glu{l}")
            sv.update(us=us, ys=ys, gl=gl, glu=glu)
            cur, hx = _glu_fwd(glu, cur, row(w["norm_xattn"][l]), f"glugate{l}")
        sv["x1"] = cur
        kv[l] = _mm(mem_n, wf["xa_w_kv", l], mode="nn", b_l=0, out_dtype=bf16, name=f"kv{l}")
        qx = _mm(hx, wf["xa_w_q", l], mode="nn", b_l=0, out_dtype=bf16, name=f"xaq{l}")
        ox = _xa_fwd(qx, kv[l], seq, f"xa_fwd{l}")
        cur, hf = _mm(ox, wf["xa_w_o", l], mode="nn", b_l=0, res=cur, out_dtype=f32, name=f"xao{l}",
                      norm=("fwd", row(w["norm_ffn"][l])))
        sv.update(hx=hx, qx=qx, ox=ox, x2=cur)
        up = _mm(hf, wf["ffn_w_up", l], mode="nn", b_l=0, out_dtype=bf16, name=f"ffnup{l}")
        act, cv = _ffn_gate_fwd(up, conv_w[l], row(w["ffn_conv_b"][l]), seq, f"ffn_gate{l}")
        if l + 1 < depth:
            cur, h_next = _mm(act, wf["ffn_w_down", l], mode="nn", b_l=0, res=cur, out_dtype=f32, name=f"ffndown{l}",
                              norm=("fwd", row(w["norm_mix"][l + 1])))
        else:
            cur = _mm(act, wf["ffn_w_down", l], mode="nn", b_l=0, res=cur, out_dtype=f32, name=f"ffndown{l}")
        sv.update(hf=hf, up=up, cv=cv, act=act)
        saved.append(sv)
        xs.append(cur)

    dx, g_final8, loss8 = _loss_head(cur, tgt, row(w["norm_final"]), "loss_head")

    gw = {}
    small = {"norm_final": jnp.sum(g_final8, axis=0)}
    g_mix, g_xa, g_ffn, g_cw, g_cb = [None] * depth, [None] * depth, [None] * depth, [None] * depth, [None] * depth
    dmem_n = None

    pending = []

    def wgrad(key, a, b, l, **kw):
        kw.setdefault("bk", 1024)
        gw[key, l] = _mm(a, b, mode="tn", out_dtype=bf16, out_l=0, out_layers=1, name=f"dw_{key}{l}", **kw)
        pending.append((key, l))

    def reduce_beside():
        if reducer is None or not pending:
            return None, []
        keys = list(pending)
        pending.clear()
        return reducer.begin(keys, gw), keys

    for l in reversed(range(depth)):
        sv = saved[l]
        dact = _mm(dx, wf["ffn_w_down", l], mode="nt", b_l=0, out_dtype=bf16, name=f"d_act{l}")
        wgrad("ffn_w_down", sv["act"], dx, l)
        dup, dcw8, dcb8 = _ffn_gate_bwd(dact, sv["up"], sv["cv"], conv_w[l], seq, f"ffn_gate_bwd{l}")
        g_cw[l], g_cb[l] = jnp.sum(dcw8, axis=1), jnp.sum(dcb8, axis=0)
        wgrad("ffn_w_up", sv["hf"], dup, l)
        dx, g8 = _mm(dup, wf["ffn_w_up", l], mode="nt", b_l=0, out_dtype=f32, name=f"d_hf{l}",
                     norm=("bwd", sv["x2"], dx, row(w["norm_ffn"][l])))
        g_ffn[l] = jnp.sum(g8, axis=0)
        dox = _mm(dx, wf["xa_w_o", l], mode="nt", b_l=0, out_dtype=bf16, name=f"d_ox{l}")
        wgrad("xa_w_o", sv["ox"], dx, l)
        dqx, dkv = _xa_bwd(sv["qx"], kv[l], dox, seq, f"xa_bwd{l}")
        wgrad("xa_w_kv", mem_n, dkv, l, bk=mem_n.shape[0])
        dmem_n = _mm(dkv, wf["xa_w_kv", l], mode="nt", b_l=0, res=dmem_n, out_dtype=f32, name=f"d_memn{l}")
        wgrad("xa_w_q", sv["hx"], dqx, l)
        dx, g8 = _mm(dqx, wf["xa_w_q", l], mode="nt", b_l=0, out_dtype=f32, name=f"d_hx{l}",
                     norm=("bwd", sv["x1"], dx, row(w["norm_xattn"][l])))
        g_xa[l] = jnp.sum(g8, axis=0)
        if l % 2 == 0:
            dmix = _mm(dx, wf["ab_w_out", 0], mode="nt", b_l=0, out_dtype=f32, name=f"d_mix{l}")
            comm, keys = reduce_beside()
            dq, dk, dv, stacks = _sb_bwd(sv["qkv"], sv["ltot"], sv["first"], dmix, seq, f"sb_bwd{l}", comm=comm)
            if comm is not None:
                reducer.end(keys, gw, stacks)
            wgrad("ab_w_out", sv["mix"], dx, 0)
            du, dpw, dps8 = _pool_bwd(dmix, sv["pooled"], w["pool_w"][0], w["pool_scale"], seq, f"pool_bwd{l}")
            small["pool_w"], small["pool_scale"] = dpw[None], jnp.sum(dps8, axis=0)[None]
            dproj = jnp.concatenate([dq, dk, dv, du], axis=1)
            wgrad("ab_w_in", sv["h"], dproj, 0)
            dx, g8 = _mm(dproj, wf["ab_w_in", 0], mode="nt", b_l=0, out_dtype=f32, name=f"d_h{l}",
                         norm=("bwd", xs[l], dx, row(w["norm_mix"][l])))
        else:
            dglu = _glu_bwd(dx, sv["glu"], f"glugate_bwd{l}")
            dgl = _mm(dglu, wf["ssm_w_glu", 0], mode="nt", b_l=0, out_dtype=f32, name=f"d_gelu{l}")
            dys = _gelu_bwd(dgl, sv["ys"], f"gelu_bwd{l}")
            comm, keys = reduce_beside()
            dus, db_big, dc_big, dl, dd8, stacks = _ssm_bwd(sv["us"], dys, b_big, c_big, lslab, ssm_d, seq, f"ssm_bwd{l}", comm=comm)
            if comm is not None:
                reducer.end(keys, gw, stacks)
            wgrad("ssm_w_glu", sv["gl"], dglu, 0)
            small["ssm_d"] = jnp.sum(dd8, axis=0)[None]
            half = SSM_PLANES // 2
            g_lr = (dl[:, 0:half] + dl[:, half:SUBLANES]).reshape(gs * ps, 1)
            g_li = (dl[:, SUBLANES + half:] - dl[:, SUBLANES:SUBLANES + half]).reshape(gs * ps, 1)
            g_bbr = _diag_in(db_big[:, :, :SSM_GB * ps], ps, cgrp).reshape(gs * ps, cgrp)
            g_bbi = _diag_in(db_big[:, :, SSM_GB * ps:], ps, cgrp).reshape(gs * ps, cgrp)
            d_a, d_b, d_dt, d_br, d_bi = _ssm_disc_bwd(lam_re, lam_im, dt, b_re, b_im, g_lr, g_li, g_bbr, g_bbi, "ssm_disc_bwd")
            small["ssm_lam_re"], small["ssm_lam_im"] = d_a.reshape(1, gs, ps), d_b.reshape(1, gs, ps)
            small["ssm_log_dt"] = (jnp.sum(d_dt.reshape(gs, ps), axis=1) * dt.reshape(gs, ps)[:, 0])[None]
            small["ssm_b_re"], small["ssm_b_im"] = d_br.reshape(1, gs, ps, cgrp), d_bi.reshape(1, gs, ps, cgrp)
            small["ssm_c_re"] = _diag_out(dc_big[:, :SSM_GB * ps], ps, cgrp)[None]
            small["ssm_c_im"] = -_diag_out(dc_big[:, SSM_GB * ps:], ps, cgrp)[None]
            wgrad("ssm_w_in", sv["h"], dus, 0)
            dx, g8 = _mm(dus, wf["ssm_w_in", 0], mode="nt", b_l=0, out_dtype=f32, name=f"d_h{l}",
                         norm=("bwd", xs[l], dx, row(w["norm_mix"][l])))
        g_mix[l] = jnp.sum(g8, axis=0)

    small["norm_mem"] = jnp.sum(_norm_bwd_gain_only(dmem_n, memf, "norm_mem_bwd"), axis=0)
    small["norm_mix"], small["norm_xattn"], small["norm_ffn"] = jnp.stack(g_mix), jnp.stack(g_xa), jnp.stack(g_ffn)
    small["ffn_conv_w"], small["ffn_conv_b"] = jnp.stack(g_cw), jnp.stack(g_cb)
    return loss8, dx, gw, small, pending


def _step(x, mem, loss_target, w, m, v):
    nb, seq, d = x.shape
    t_all = nb * seq
    depth = w["norm_mix"].shape[0]
    chip = 2 * lax.axis_index("x") + lax.axis_index("y")

    small_mine = _pack([w[k] for k in SMALL_SHARDED], SUBLANES)
    gathered = _all_gather([w[k].astype(bf16) for k in FIRST_MIXER] + [small_mine], [BIG_AXIS[k] for k in FIRST_MIXER] + [1],
                           "gather_first")
    wf = {(k, 0): g for k, g in zip(FIRST_MIXER, gathered[:-1])}
    per_chip = gathered[-1].reshape(N_CHIPS, -1)
    pieces = [_unpack(per_chip[q], [w[k].shape for k in SMALL_SHARDED]) for q in range(N_CHIPS)]
    ssm_d = jnp.concatenate([pc[0] for pc in pieces], axis=-1)
    conv_w = jnp.concatenate([pc[1] for pc in pieces], axis=-1)
    ff2 = conv_w.shape[-1]
    late = [(k, l) for k in BIG if k not in FIRST_MIXER for l in range(w[k].shape[0])]
    groups = {"sb_fwd0": [kl for kl in late if kl[1] == 0], "ssm_fwd1": [kl for kl in late if kl[1] > 0]}
    late_weights = {hook: (_gather_plan([w[k][l:l + 1].astype(bf16) for k, l in keys], [BIG_AXIS[k] for k, _ in keys]), keys)
                    for hook, keys in groups.items()}

    reducer = _Reducer()
    loss8, dx, gw, small, pending = _local_step(x.reshape(t_all, d), mem.reshape(-1, d), loss_target.reshape(t_all, d), w, wf,
                                                conv_w, ssm_d, seq, late_weights=late_weights, reducer=reducer)
    loss = lax.psum(0.5 * jnp.sum(loss8) / d, ("x", "y", "c"))

    small_names = SMALL_REPL + SMALL_SHARDED
    small_full_shapes = [w[k].shape for k in SMALL_REPL] + [(1, d), (depth, 3, ff2)]
    gw["small", 0] = _pack([small[k] for k in small_names], 2 * N_CHIPS * SUBLANES)
    keys = pending + [("small", 0)]
    reducer.end(keys, gw, _comm_only(reducer.begin(keys, gw), "rs_owner_last"))
    g_big = {k: jnp.concatenate([reducer.done[k, l] for l in range(w[k].shape[0])], axis=0) for k in BIG}
    small_all = _all_gather([reducer.done["small", 0]], [1], "gather_small_grads")[0]
    g_small = dict(zip(small_names, _unpack(small_all, small_full_shapes)))
    g_small["ssm_d"] = lax.dynamic_slice_in_dim(g_small["ssm_d"], chip * (d // N_CHIPS), d // N_CHIPS, axis=1)
    g_small["ffn_conv_w"] = lax.dynamic_slice_in_dim(g_small["ffn_conv_w"], chip * (ff2 // N_CHIPS), ff2 // N_CHIPS, axis=2)
    grads = {**g_big, **g_small}

    delta, new_m, new_v = {}, {}, {}
    for k in BIG:
        n_cols = w[k].shape[-1]
        two = lambda a: a.reshape(-1, n_cols)
        dl_, m_, v_ = _adamw(two(w[k]), two(grads[k]), two(m[k]), two(v[k]), f"adamw_{k}")
        delta[k], new_m[k], new_v[k] = dl_.reshape(w[k].shape), m_.reshape(w[k].shape), v_.reshape(w[k].shape)
    pk = lambda tree: _pack([tree[k] for k in small_names], 256)[0]
    small_shapes = [w[k].shape for k in small_names]
    outs = _adamw(pk(w), pk(grads), pk(m), pk(v), "adamw_small")
    for tree, buf in zip((delta, new_m, new_v), outs):
        tree.update(zip(small_names, _unpack(buf, small_shapes)))

    grad_x = dx.reshape(nb, seq, d)
    return (loss, grad_x, *[grads[k] for k in WEIGHTS], *[delta[k] for k in WEIGHTS], *[new_m[k] for k in WEIGHTS],
            *[new_v[k] for k in WEIGHTS])


def kernel(x, mem, norm_mix, norm_xattn, norm_ffn, norm_mem, norm_final, ab_w_in, pool_w, pool_scale, ab_w_out, ssm_w_in, ssm_lam_re, ssm_lam_im, ssm_log_dt, ssm_b_re, ssm_b_im, ssm_c_re, ssm_c_im, ssm_d, ssm_w_glu, xa_w_q, xa_w_kv, xa_w_o, ffn_w_up, ffn_conv_w, ffn_conv_b, ffn_w_down, loss_target, m_norm_mix, m_norm_xattn, m_norm_ffn, m_norm_mem, m_norm_final, m_ab_w_in, m_pool_w, m_pool_scale, m_ab_w_out, m_ssm_w_in, m_ssm_lam_re, m_ssm_lam_im, m_ssm_log_dt, m_ssm_b_re, m_ssm_b_im, m_ssm_c_re, m_ssm_c_im, m_ssm_d, m_ssm_w_glu, m_xa_w_q, m_xa_w_kv, m_xa_w_o, m_ffn_w_up, m_ffn_conv_w, m_ffn_conv_b, m_ffn_w_down, v_norm_mix, v_norm_xattn, v_norm_ffn, v_norm_mem, v_norm_final, v_ab_w_in, v_pool_w, v_pool_scale, v_ab_w_out, v_ssm_w_in, v_ssm_lam_re, v_ssm_lam_im, v_ssm_log_dt, v_ssm_b_re, v_ssm_b_im, v_ssm_c_re, v_ssm_c_im, v_ssm_d, v_ssm_w_glu, v_xa_w_q, v_xa_w_kv, v_xa_w_o, v_ffn_w_up, v_ffn_conv_w, v_ffn_conv_b, v_ffn_w_down):
    args = dict(locals())
    w = {k: args[k] for k in WEIGHTS}
    m = {k: args["m_" + k] for k in WEIGHTS}
    v = {k: args["v_" + k] for k in WEIGHTS}
    return _step(x, mem, loss_target, w, m, v)
```

```python
import math

import jax
import jax.numpy as jnp
from jax import lax
from jax.experimental import pallas as pl
from jax.experimental.pallas import tpu as pltpu

f32 = jnp.float32
bf16 = jnp.bfloat16
SDS = jax.ShapeDtypeStruct
MESH = pl.DeviceIdType.MESH
ANY = pl.BlockSpec(memory_space=pl.ANY)

SB_HEAD_DIM = 64
POOL_WINDOWS = (2, 4, 8, 16)
POOL_GROUP = 128
XA_HEADS = 4
SSM_GROUPS = 64
SSM_GROUP = 16
SSM_STATE = 64
EPS = 1e-6
ADAM_LR, ADAM_B1, ADAM_B2, ADAM_EPS, ADAM_WD, ADAM_STEP = 0.001, 0.9, 0.999, 1e-08, 0.01, 10

LANES = 128
SUBLANES = 8
N_CHIPS = 4
VMEM_LIMIT = 56 * 1024 * 1024

NN = ((1,), (0,))
NT = ((1,), (1,))
TN = ((0,), (0,))


def _dot(a, b, dims):
    return lax.dot_general(a, b, (dims, ((), ())), preferred_element_type=f32)


def _params(n_grid):
    return pltpu.CompilerParams(dimension_semantics=("arbitrary",) * n_grid, vmem_limit_bytes=VMEM_LIMIT)


def _sum8(x):
    r, n = x.shape
    return jnp.sum(x.reshape(r // SUBLANES, SUBLANES, n), axis=0)


def _split_bf16(x):
    hi = x.astype(bf16)
    lo = (x - hi.astype(f32)).astype(bf16)
    return hi, lo


def _sigmoid(x):
    return 1.0 / (1.0 + jnp.exp(-x))


MM_BM = (1024, 1408, 512, 256, 128)
MM_BN = (1536, 1408, 1024, 512, 256, 128)
MM_BK = (2816, 2048, 1024, 512)


def _divisor(n, cands):
    return next((c for c in cands if n % c == 0), n)


def _mm(a, b, *, mode, name, out_dtype, bm=None, bn=None, bk=None, a_l=None, b_l=None, b_n0=0, n=None,
        res=None, out_l=None, out_layers=None, out_prev=None, norm=None):
    dims = {"nn": NN, "nt": NT, "tn": TN}[mode]
    a2, b2 = a.shape[-2:], b.shape[-2:]
    if mode == "nn":
        (m, k), nfull = a2, b2[1]
    elif mode == "nt":
        (m, k), nfull = a2, b2[0]
    else:
        (k, m), nfull = a2, b2[1]
    n = nfull if n is None else n
    if bm is None and norm is not None:
        bm = 512
    bm = _divisor(m, MM_BM) if bm is None else min(bm, m)
    bn = _divisor(n, MM_BN) if bn is None else min(bn, n)
    if bk is None:
        bk = _divisor(k, (1024, 512)) if mode == "tn" else (k if k <= MM_BK[0] else _divisor(k, MM_BK))
    bk = min(bk, k)
    assert m % bm == 0 and n % bn == 0 and k % bk == 0 and b_n0 % bn == 0, (name, m, n, k, bm, bn, bk)
    nk, n0b = k // bk, b_n0 // bn
    a_bytes, b_bytes = m * k * a.dtype.itemsize, k * n * b.dtype.itemsize
    rows_outer = a_bytes + b_bytes * (m // bm) <= b_bytes + a_bytes * (n // bn)

    def with_layer(layer, blk, idx_fn):
        def idx(g0, g1, kk):
            i, j = (g0, g1) if rows_outer else (g1, g0)
            return idx_fn(i, j, kk) if layer is None else (layer,) + idx_fn(i, j, kk)
        return pl.BlockSpec(blk if layer is None else (None,) + blk, idx)

    if mode == "tn":
        a_spec = with_layer(a_l, (bk, bm), lambda i, j, kk: (kk, i))
    else:
        a_spec = with_layer(a_l, (bm, bk), lambda i, j, kk: (i, kk))
    if mode == "nt":
        b_spec = with_layer(b_l, (bn, bk), lambda i, j, kk: (j, kk))
    else:
        b_spec = with_layer(b_l, (bk, bn), lambda i, j, kk: (kk, j + n0b))
    o_spec = with_layer(out_l, (bm, bn), lambda i, j, kk: (i, j))
    ins, in_specs = [a, b], [a_spec, b_spec]
    row_blk = with_layer(None, (bm, bn), lambda i, j, kk: (i, j))
    if res is not None:
        ins.append(res)
        in_specs.append(row_blk)
    n_norm_in = 0
    if norm is not None:
        assert bn == n and out_l is None and out_prev is None, name
        extra = list(norm[1:])
        n_norm_in = len(extra)
        ins += extra
        in_specs += [row_blk] * (n_norm_in - 1) + [pl.BlockSpec((1, n), lambda g0, g1, kk: (0, 0))]
    aliases = {}
    if out_prev is not None:
        aliases = {len(ins): 0}
        ins.append(out_prev)
        in_specs.append(ANY)
    has_res, has_prev = res is not None, out_prev is not None

    def body(*refs):
        a_ref, b_ref = refs[0], refs[1]
        res_ref = refs[2] if has_res else None
        norm_refs = refs[2 + has_res:2 + has_res + n_norm_in]
        o_ref = refs[2 + has_res + n_norm_in + has_prev]
        row_block = pl.program_id(0 if rows_outer else 1)
        part = _dot(a_ref[...].astype(bf16), b_ref[...].astype(bf16), dims)

        def finish(r):
            if has_res:
                r = r + res_ref[...]
            if norm is None:
                o_ref[...] = r.astype(o_ref.dtype)
            elif norm[0] == "fwd":
                h_ref = refs[3 + has_res + n_norm_in + has_prev]
                o_ref[...] = r
                rs = lax.rsqrt(jnp.mean(r * r, axis=1, keepdims=True) + EPS)
                h_ref[...] = (r * rs * norm_refs[0][...]).astype(bf16)
            else:
                x_ref, dres_ref, g_ref = norm_refs
                dg_ref = refs[3 + has_res + n_norm_in + has_prev]
                xv = x_ref[...]
                rs = lax.rsqrt(jnp.mean(xv * xv, axis=1, keepdims=True) + EPS)
                xh = xv * rs
                dxh = r * g_ref[...]
                o_ref[...] = dres_ref[...] + rs * (dxh - xh * jnp.mean(dxh * xh, axis=1, keepdims=True))
                dg = _sum8(r * xh)

                @pl.when(row_block == 0)
                def _():
                    dg_ref[...] = dg

                @pl.when(row_block > 0)
                def _():
                    dg_ref[...] += dg

        if nk == 1:
            finish(part)
        else:
            acc_ref = refs[-1]
            kk = pl.program_id(2)

            @pl.when(kk == 0)
            def _():
                acc_ref[...] = part

            @pl.when(kk > 0)
            def _():
                acc_ref[...] += part

            @pl.when(kk == nk - 1)
            def _():
                finish(acc_ref[...])

    out_shape = SDS((m, n) if out_l is None else (out_layers, m, n), out_dtype)
    grid = (m // bm, n // bn, nk) if rows_outer else (n // bn, m // bm, nk)
    if norm is not None:
        if norm[0] == "fwd":
            out_shape, o_spec = (out_shape, SDS((m, n), bf16)), (o_spec, row_blk)
        else:
            out_shape = (out_shape, SDS((SUBLANES, n), f32))
            o_spec = (o_spec, pl.BlockSpec((SUBLANES, n), lambda g0, g1, kk: (0, 0)))
    return pl.pallas_call(
        body, out_shape=out_shape, grid=grid, in_specs=in_specs, out_specs=o_spec,
        scratch_shapes=[] if nk == 1 else [pltpu.VMEM((bm, bn), f32)],
        input_output_aliases=aliases, name=name, compiler_params=_params(3))(*ins)


def _rowwise(fn, row_ins, full_ins, row_outs, acc_outs, *, name, br=512):
    t = row_ins[0].shape[0]
    br = next(b for b in (br, 256, 128, 64, 32, 16, 8, t) if b <= t and t % b == 0)
    nr, nf, no = len(row_ins), len(full_ins), len(row_outs)

    def body(*refs):
        rv = [r[...] for r in refs[:nr]]
        fv = [r[...] for r in refs[nr:nr + nf]]
        o_refs = refs[nr + nf:nr + nf + no]
        a_refs = refs[nr + nf + no:]
        outs, accs = fn(rv, fv)
        for o_ref, v in zip(o_refs, outs):
            o_ref[...] = v.astype(o_ref.dtype)
        if a_refs:
            i = pl.program_id(0)

            @pl.when(i == 0)
            def _():
                for a_ref, v in zip(a_refs, accs):
                    a_ref[...] = v

            @pl.when(i > 0)
            def _():
                for a_ref, v in zip(a_refs, accs):
                    a_ref[...] += v

    in_specs = [pl.BlockSpec((br, x.shape[1]), lambda i: (i, 0)) for x in row_ins]
    in_specs += [pl.BlockSpec(x.shape, lambda i, nd=x.ndim: (0,) * nd) for x in full_ins]
    out_specs = [pl.BlockSpec((br, s.shape[1]), lambda i: (i, 0)) for s in row_outs]
    out_specs += [pl.BlockSpec(s.shape, lambda i: (0, 0)) for s in acc_outs]
    res = pl.pallas_call(body, out_shape=tuple(row_outs) + tuple(acc_outs), grid=(t // br,), in_specs=in_specs,
                         out_specs=tuple(out_specs), name=name, compiler_params=_params(1))(*row_ins, *full_ins)
    return res


def _norm_fwd(x, g, name):
    def fn(rv, fv):
        (xv,), (gv,) = rv, fv
        r = lax.rsqrt(jnp.mean(xv * xv, axis=1, keepdims=True) + EPS)
        return [xv * r * gv], []
    return _rowwise(fn, [x], [g], [SDS(x.shape, bf16)], [], name=name)[0]


def _norm_bwd_gain_only(dh, x, name):
    d = x.shape[1]

    def fn(rv, fv):
        dhv, xv = rv
        r = lax.rsqrt(jnp.mean(xv * xv, axis=1, keepdims=True) + EPS)
        return [], [_sum8(dhv * xv * r)]
    return _rowwise(fn, [dh, x], [], [], [SDS((SUBLANES, d), f32)], name=name)[0]


def _loss_head(x, target, g, name):
    d = x.shape[1]

    def fn(rv, fv):
        (xv, tv), (gv,) = rv, fv
        r = lax.rsqrt(jnp.mean(xv * xv, axis=1, keepdims=True) + EPS)
        xh = xv * r
        err = xh * gv - tv
        dy = err * (1.0 / d)
        dxh = dy * gv
        dx = r * (dxh - xh * jnp.mean(dxh * xh, axis=1, keepdims=True))
        return [dx], [_sum8(dy * xh), _sum8(err * err)]
    return _rowwise(fn, [x, target], [g], [SDS(x.shape, f32)], [SDS((SUBLANES, d), f32), SDS((SUBLANES, d), f32)], name=name)


_GELU_C = math.sqrt(2.0 / math.pi)


def _gelu_fwd(y, name):
    def fn(rv, fv):
        (v,) = rv
        t = jnp.tanh(_GELU_C * (v + 0.044715 * v * v * v))
        return [0.5 * v * (1.0 + t)], []
    return _rowwise(fn, [y], [], [SDS(y.shape, bf16)], [], name=name)[0]


def _gelu_bwd(dg, y, name):
    def fn(rv, fv):
        dgv, v = rv
        t = jnp.tanh(_GELU_C * (v + 0.044715 * v * v * v))
        dt = (1.0 - t * t) * _GELU_C * (1.0 + 3.0 * 0.044715 * v * v)
        return [dgv * (0.5 * (1.0 + t) + 0.5 * v * dt)], []
    return _rowwise(fn, [dg, y], [], [SDS(y.shape, f32)], [], name=name)[0]


def _glu_fwd(glu, x, g, name):
    d = x.shape[1]

    def fn(rv, fv):
        (gl, xv), (gv,) = rv, fv
        y = xv + gl[:, :d] * _sigmoid(gl[:, d:])
        r = lax.rsqrt(jnp.mean(y * y, axis=1, keepdims=True) + EPS)
        return [y, y * r * gv], []
    return _rowwise(fn, [glu, x], [g], [SDS(x.shape, f32), SDS(x.shape, bf16)], [], name=name)


def _glu_bwd(dx, glu, name):
    d = dx.shape[1]

    def fn(rv, fv):
        dxv, gl = rv
        sg = _sigmoid(gl[:, d:])
        return [jnp.concatenate([dxv * sg, dxv * gl[:, :d] * sg * (1.0 - sg)], axis=1)], []
    return _rowwise(fn, [dx, glu], [], [SDS(glu.shape, bf16)], [], name=name)[0]


def _adamw(w, g, m, v, name):
    c1 = 1.0 - ADAM_B1 ** ADAM_STEP
    c2 = 1.0 - ADAM_B2 ** ADAM_STEP

    def fn(rv, fv):
        wv, gv, mv, vv = rv
        m2 = ADAM_B1 * mv + (1.0 - ADAM_B1) * gv
        v2 = ADAM_B2 * vv + (1.0 - ADAM_B2) * (gv * gv)
        delta = -ADAM_LR * ((m2 / c1) / (jnp.sqrt(v2 / c2) + ADAM_EPS) + ADAM_WD * wv)
        return [delta, m2, v2], []
    s = SDS(w.shape, f32)
    return _rowwise(fn, [w, g, m, v], [], [s, s, s], [], name=name, br=256)


SB_TQ = 128
SB_KB = 4
SB_DEAD = -110.0


def _sb_heads(q, t):
    lane = lax.broadcasted_iota(jnp.int32, (t, LANES), 1)
    masks = [(lane >= hh * SB_HEAD_DIM) & (lane < (hh + 1) * SB_HEAD_DIM) for hh in range(LANES // SB_HEAD_DIM)]
    return [(m, q * jnp.where(m, 1.0, 0.0).astype(bf16)) for m in masks]


def _sb_key_minus_query(t):
    return lax.broadcasted_iota(jnp.int32, (t, t), 1) - lax.broadcasted_iota(jnp.int32, (t, t), 0)


def _tri(t, op):
    row = lax.broadcasted_iota(jnp.int32, (t, t), 0)
    col = lax.broadcasted_iota(jnp.int32, (t, t), 1)
    return jnp.where(op(row, col), 1.0, 0.0).astype(bf16)


def _sb_block(i, g, kk, kbn, t, kmq, k_ref, v_ref):
    j = i - g * kbn - kk
    off = pl.multiple_of(jnp.maximum(j, 0) * t, t)
    limit = jnp.where(j >= 0, (i - j) * t, -2 * t)
    return off, k_ref[pl.ds(off, t), :], v_ref[pl.ds(off, t), :], kmq < limit


def _sb_fwd(qkv, seq, name, comm=None):
    t_all, w3 = qkv.shape
    w = w3 // 3
    hp, tq = w // LANES, SB_TQ
    nb, nq = t_all // seq, seq // tq
    kbn = min(SB_KB, nq)

    def body(q_ref, k_ref, v_ref, o_ref, lt_ref, first_ref):
        i = pl.program_id(2)
        heads = _sb_heads(q_ref[...], tq)
        kmq = _sb_key_minus_query(tq)
        u_after = _tri(tq, lambda r, c: r > c)
        n_it = (i + kbn) // kbn

        def alive(state):
            return (state[0] < n_it) & (state[1] > SB_DEAD)

        def step(state):
            it, carry = state[0], list(state[2:])
            blocks = [_sb_block(i, it, kk, kbn, tq, kmq, k_ref, v_ref)[1:] for kk in range(kbn)]
            chains = [(hh, qh, kb, vb, valid) for kb, vb, valid in blocks for hh, (_, qh) in enumerate(heads)]
            zs = [_dot(qh, kb, NT) for _, qh, kb, _, _ in chains]
            lbs, his, los, sums = [], [], [], []
            for z, (_, _, _, _, valid) in zip(zs, chains):
                z = z * (SB_HEAD_DIM ** -0.5)
                sp = jnp.log(1.0 + jnp.exp(-jnp.abs(z)))
                lb = jnp.minimum(z, 0.0) - sp
                lk = jnp.where(valid, lb - z, 0.0)
                hi, lo = _split_bf16(lk)
                lbs.append(lb), his.append(hi), los.append(lo), sums.append(jnp.sum(lk, axis=1, keepdims=True))
            afts = [_dot(hi, u_after, NN) + _dot(lo, u_after, NN) for hi, lo in zip(his, los)]
            wgts = []
            for (hh, _, _, _, valid), lb, aft, sm in zip(chains, lbs, afts, sums):
                wgts.append(jnp.where(valid, jnp.exp(lb + (carry[2 * hh] + aft)), 0.0).astype(bf16))
                carry[2 * hh] = carry[2 * hh] + sm
            for (hh, _, _, vb, _), wgt in zip(chains, wgts):
                carry[2 * hh + 1] = carry[2 * hh + 1] + _dot(wgt, vb, NN)
            top = jnp.max(carry[0])
            for hh in range(1, len(heads)):
                top = jnp.maximum(top, jnp.max(carry[2 * hh]))
            return (it + 1, top, *carry)

        init = (jnp.int32(0), jnp.float32(0.0)) + (jnp.zeros((tq, 1), f32), jnp.zeros((tq, LANES), f32)) * len(heads)
        fin = lax.while_loop(alive, step, init)
        out = jnp.zeros((tq, LANES), f32)
        ltot = jnp.zeros((tq, LANES), f32)
        for hh, (m, _) in enumerate(heads):
            out = out + jnp.where(m, fin[2 * hh + 3], 0.0)
            ltot = ltot + jnp.where(m, fin[2 * hh + 2], 0.0)
        o_ref[...] = out
        lt_ref[...] = ltot
        first_ref[...] = jnp.zeros((SUBLANES, LANES), f32) + fin[0].astype(f32)

    row_blk = pl.BlockSpec((tq, LANES), lambda b, p, i: (b * nq + i, p))
    (mix, ltot, first), extra = _call(
        body, ins=[qkv, qkv, qkv], out_shape=[SDS((t_all, 2 * w), f32), SDS((t_all, w), f32), SDS((nb * nq * SUBLANES, w), f32)],
        grid=(nb, hp, nq),
        in_specs=[row_blk, pl.BlockSpec((seq, LANES), lambda b, p, i: (b, hp + p)),
                  pl.BlockSpec((seq, LANES), lambda b, p, i: (b, 2 * hp + p))],
        out_specs=[row_blk, row_blk, pl.BlockSpec((SUBLANES, LANES), lambda b, p, i: (b * nq + i, p))],
        scratch_shapes=[], name=name, comm=comm)
    return mix, ltot, first, extra


def _sb_bwd(qkv, ltot, first, dmix, seq, name, comm=None):
    t_all, w3 = qkv.shape
    w = w3 // 3
    hp, tq = w // LANES, SB_TQ
    nb, nq = t_all // seq, seq // tq
    kbn = min(SB_KB, nq)

    def body(q_ref, k_ref, v_ref, lt_ref, first_ref, do_ref, dq_ref, dk_ref, dv_ref, dk_acc, dv_acc):
        i = pl.program_id(2)

        @pl.when(i == 0)
        def _():
            dk_acc[...] = jnp.zeros_like(dk_acc)
            dv_acc[...] = jnp.zeros_like(dv_acc)

        heads = _sb_heads(q_ref[...], tq)
        do = do_ref[...]
        ltv = lt_ref[...]
        dos = [jnp.where(m, do, 0.0).astype(bf16) for m, _ in heads]
        lts = [jnp.sum(jnp.where(m, ltv, 0.0), axis=1, keepdims=True) * (1.0 / SB_HEAD_DIM) for m, _ in heads]
        kmq = _sb_key_minus_query(tq)
        u_incl = _tri(tq, lambda r, c: r <= c)
        u_excl = _tri(tq, lambda r, c: r < c)
        n_it = (i + kbn) // kbn

        walked = jnp.clip(jnp.max(first_ref[...]).astype(jnp.int32), 1, n_it)

        def step(s, carry):
            carry = list(carry)
            blocks = [_sb_block(i, walked - 1 - s, kk, kbn, tq, kmq, k_ref, v_ref) for kk in reversed(range(kbn))]
            chains = [(hh, qh, kb, vb, valid) for _, kb, vb, valid in blocks for hh, (_, qh) in enumerate(heads)]
            zs = [_dot(qh, kb, NT) for _, qh, kb, _, _ in chains]
            dws = [_dot(dos[hh], vb, NT) for hh, _, _, vb, _ in chains]
            lbs, lkrs, his, los, sums = [], [], [], [], []
            for z, (_, _, _, _, valid) in zip(zs, chains):
                z = z * (SB_HEAD_DIM ** -0.5)
                sp = jnp.log(1.0 + jnp.exp(-jnp.abs(z)))
                lb = jnp.minimum(z, 0.0) - sp
                lk_raw = lb - z
                lk = jnp.where(valid, lk_raw, 0.0)
                hi, lo = _split_bf16(lk)
                lbs.append(lb), lkrs.append(lk_raw), his.append(hi), los.append(lo)
                sums.append(jnp.sum(lk, axis=1, keepdims=True))
            pins = [_dot(hi, u_incl, NN) + _dot(lo, u_incl, NN) for hi, lo in zip(his, los)]
            wbs, gs, ghis, glos, gpres = [], [], [], [], []
            for (hh, _, _, _, valid), lb, pin, sm, dw in zip(chains, lbs, pins, sums, dws):
                wgt = jnp.where(valid, jnp.exp(lb + (lts[hh] - (carry[3 * hh] + pin))), 0.0)
                carry[3 * hh] = carry[3 * hh] + sm
                g = dw * wgt
                hi, lo = _split_bf16(g)
                wbs.append(wgt.astype(bf16)), gs.append(g), ghis.append(hi), glos.append(lo)
                gpres.append(carry[3 * hh + 1])
                carry[3 * hh + 1] = carry[3 * hh + 1] + jnp.sum(g, axis=1, keepdims=True)
            gins = [_dot(hi, u_excl, NN) + _dot(lo, u_excl, NN) for hi, lo in zip(ghis, glos)]
            dzbs = []
            for (_, _, _, _, valid), lb, lk_raw, g, gpre, gin in zip(chains, lbs, lkrs, gs, gpres, gins):
                dz = jnp.where(valid, g * jnp.exp(lk_raw) - (gpre + gin) * jnp.exp(lb), 0.0) * (SB_HEAD_DIM ** -0.5)
                dzbs.append(dz.astype(bf16))
            for (hh, _, kb, _, _), dzb in zip(chains, dzbs):
                carry[3 * hh + 2] = carry[3 * hh + 2] + _dot(dzb, kb, NN)
            nh = len(heads)
            for bi, (off, _, _, _) in enumerate(blocks):
                dk_j = jnp.zeros((tq, LANES), f32)
                dv_j = jnp.zeros((tq, LANES), f32)
                for hh, (_, qh) in enumerate(heads):
                    dk_j = dk_j + _dot(dzbs[bi * nh + hh], qh, TN)
                    dv_j = dv_j + _dot(wbs[bi * nh + hh], dos[hh], TN)
                dk_acc[pl.ds(off, tq), :] += dk_j
                dv_acc[pl.ds(off, tq), :] += dv_j
            return tuple(carry)

        zero1 = jnp.zeros((tq, 1), f32)
        fin = lax.fori_loop(0, walked, step, (zero1, zero1, jnp.zeros((tq, LANES), f32)) * len(heads))
        dq_all = jnp.zeros((tq, LANES), f32)
        for hh, (m, _) in enumerate(heads):
            dq_all = dq_all + jnp.where(m, fin[3 * hh + 2], 0.0)
        dq_ref[...] = dq_all.astype(bf16)

        @pl.when(i == nq - 1)
        def _():
            dk_ref[...] = dk_acc[...].astype(bf16)
            dv_ref[...] = dv_acc[...].astype(bf16)

    row_blk = pl.BlockSpec((tq, LANES), lambda b, p, i: (b * nq + i, p))
    seq_blk = pl.BlockSpec((seq, LANES), lambda b, p, i: (b, p))
    out = SDS((t_all, w), bf16)
    (dq, dk, dv), extra = _call(
        body, ins=[qkv, qkv, qkv, ltot, first, dmix], out_shape=[out, out, out], grid=(nb, hp, nq),
        in_specs=[row_blk,
                  pl.BlockSpec((seq, LANES), lambda b, p, i: (b, hp + p)),
                  pl.BlockSpec((seq, LANES), lambda b, p, i: (b, 2 * hp + p)),
                  row_blk, pl.BlockSpec((SUBLANES, LANES), lambda b, p, i: (b * nq + i, p)), row_blk],
        out_specs=[row_blk, seq_blk, seq_blk],
        scratch_shapes=[pltpu.VMEM((seq, LANES), f32), pltpu.VMEM((seq, LANES), f32)], name=name, comm=comm)
    return dq, dk, dv, extra


POOL_CHUNK = 256
POOL_HALO = 16


def _band(rows, cols, lo, hi):
    r = lax.broadcasted_iota(jnp.int32, (rows, cols), 0)
    c = lax.broadcasted_iota(jnp.int32, (rows, cols), 1)
    d = c - r
    return jnp.where((d >= lo) & (d < hi), 1.0, 0.0).astype(bf16)


def _pool_counts(r0, rows, win):
    t = lax.broadcasted_iota(jnp.int32, (rows, 1), 0) + r0
    return jnp.minimum(t + 1, win).astype(f32)


def _pool_fwd(u, mix, pool_w, scale, seq, name):
    t_all, w = u.shape
    ng, rc = w // POOL_GROUP, min(POOL_CHUNK, seq)

    def body(u_ref, w_ref, s_ref, mix_in, p_ref, o_ref, pad):
        del mix_in
        pad[0:POOL_HALO, :] = jnp.zeros((POOL_HALO, POOL_GROUP), f32)
        for g in range(ng):
            cols = slice(g * POOL_GROUP, (g + 1) * POOL_GROUP)
            win = POOL_WINDOWS[g]
            pad[POOL_HALO:POOL_HALO + seq, :] = u_ref[:, cols]
            band = _band(rc, rc + POOL_HALO, POOL_HALO - win + 1, POOL_HALO + 1)
            wg = w_ref[g].astype(bf16)
            for r0 in range(0, seq, rc):
                ue = pad[r0:r0 + rc + POOL_HALO, :]
                hi, lo = _split_bf16(ue)
                sm = _dot(band, hi, NN) + _dot(band, lo, NN)
                pch = sm / _pool_counts(r0, rc, win) - ue[POOL_HALO:, :]
                pb = pch.astype(bf16)
                p_ref[r0:r0 + rc, cols] = pb
                o_ref[r0:r0 + rc, cols] = _dot(pb, wg, NN) * s_ref[:, cols]

    return pl.pallas_call(
        body, out_shape=(SDS((t_all, w), bf16), SDS(mix.shape, f32)), grid=(t_all // seq,),
        in_specs=[pl.BlockSpec((seq, w), lambda b: (b, 0)), pl.BlockSpec(pool_w.shape, lambda b: (0, 0, 0)),
                  pl.BlockSpec(scale.shape, lambda b: (0, 0)), ANY],
        out_specs=(pl.BlockSpec((seq, w), lambda b: (b, 0)), pl.BlockSpec((seq, w), lambda b: (b, 1))),
        scratch_shapes=[pltpu.VMEM((seq + POOL_HALO, POOL_GROUP), f32)],
        input_output_aliases={3: 1}, name=name, compiler_params=_params(1))(u, pool_w, scale, mix)


def _pool_bwd(dmix, p, pool_w, scale, seq, name):
    t_all, w = p.shape
    ng, rc = w // POOL_GROUP, min(POOL_CHUNK, seq)

    def body(dy_ref, p_ref, w_ref, s_ref, du_ref, dw_ref, ds_ref, dpn, dpr):
        b = pl.program_id(0)

        @pl.when(b == 0)
        def _():
            dw_ref[...] = jnp.zeros_like(dw_ref)
            ds_ref[...] = jnp.zeros_like(ds_ref)

        dpn[seq:seq + POOL_HALO, :] = jnp.zeros((POOL_HALO, POOL_GROUP), f32)
        for g in range(ng):
            cols = slice(g * POOL_GROUP, (g + 1) * POOL_GROUP)
            win = POOL_WINDOWS[g]
            wg = w_ref[g].astype(bf16)
            sg = s_ref[:, cols]
            dwg = jnp.zeros((POOL_GROUP, POOL_GROUP), f32)
            dsg = jnp.zeros((SUBLANES, POOL_GROUP), f32)
            for r0 in range(0, seq, rc):
                dy = dy_ref[r0:r0 + rc, cols]
                pb = p_ref[r0:r0 + rc, cols]
                dsg = dsg + _sum8(dy * _dot(pb, wg, NN))
                dyw = (dy * sg).astype(bf16)
                dwg = dwg + _dot(pb, dyw, TN)
                dp = _dot(dyw, wg, NT)
                dpr[r0:r0 + rc, :] = dp
                dpn[r0:r0 + rc, :] = dp / _pool_counts(r0, rc, win)
            dw_ref[g] += dwg
            ds_ref[:, cols] += dsg
            band = _band(rc, rc + POOL_HALO, 0, win)
            for r0 in range(0, seq, rc):
                hi, lo = _split_bf16(dpn[r0:r0 + rc + POOL_HALO, :])
                du = _dot(band, hi, NN) + _dot(band, lo, NN) - dpr[r0:r0 + rc, :]
                du_ref[r0:r0 + rc, cols] = du.astype(bf16)

    return pl.pallas_call(
        body, out_shape=(SDS((t_all, w), bf16), SDS(pool_w.shape, f32), SDS((SUBLANES, w), f32)), grid=(t_all // seq,),
        in_specs=[pl.BlockSpec((seq, w), lambda b: (b, 1)), pl.BlockSpec((seq, w), lambda b: (b, 0)),
                  pl.BlockSpec(pool_w.shape, lambda b: (0, 0, 0)), pl.BlockSpec(scale.shape, lambda b: (0, 0))],
        out_specs=(pl.BlockSpec((seq, w), lambda b: (b, 0)), pl.BlockSpec(pool_w.shape, lambda b: (0, 0, 0)),
                   pl.BlockSpec((SUBLANES, w), lambda b: (0, 0))),
        scratch_shapes=[pltpu.VMEM((seq + POOL_HALO, POOL_GROUP), f32), pltpu.VMEM((seq, POOL_GROUP), f32)],
        name=name, compiler_params=_params(1))(dmix, p, pool_w, scale)


XA_TQ = 1024


def _xa_probs(qh, kh, dh):
    s = _dot(qh, kh, NT) * (dh ** -0.5)
    e = jnp.exp(s - jnp.max(s, axis=1, keepdims=True))
    return e / jnp.sum(e, axis=1, keepdims=True)


def _xa_fwd(q, kv, seq, name):
    t_all, d = q.shape
    nb = t_all // seq
    mem, dh, tq = kv.shape[0] // nb, d // XA_HEADS, min(XA_TQ, seq)
    nq = seq // tq

    def body(q_ref, kv_ref, o_ref):
        for h in range(XA_HEADS):
            cols = slice(h * dh, (h + 1) * dh)
            p = _xa_probs(q_ref[:, cols], kv_ref[:, cols], dh)
            o_ref[:, cols] = _dot(p.astype(bf16), kv_ref[:, d + h * dh:d + (h + 1) * dh], NN).astype(bf16)

    return pl.pallas_call(
        body, out_shape=SDS((t_all, d), bf16), grid=(nb, nq),
        in_specs=[pl.BlockSpec((tq, d), lambda b, i: (b * nq + i, 0)), pl.BlockSpec((mem, 2 * d), lambda b, i: (b, 0))],
        out_specs=pl.BlockSpec((tq, d), lambda b, i: (b * nq + i, 0)), name=name, compiler_params=_params(2))(q, kv)


def _xa_bwd(q, kv, do, seq, name):
    t_all, d = q.shape
    nb = t_all // seq
    mem, dh, tq = kv.shape[0] // nb, d // XA_HEADS, min(XA_TQ, seq)
    nq = seq // tq

    def body(q_ref, kv_ref, do_ref, dq_ref, dkv_ref):
        i = pl.program_id(1)

        @pl.when(i == 0)
        def _():
            dkv_ref[...] = jnp.zeros_like(dkv_ref)

        for h in range(XA_HEADS):
            cols = slice(h * dh, (h + 1) * dh)
            vcols = slice(d + h * dh, d + (h + 1) * dh)
            qh, kh, doh = q_ref[:, cols], kv_ref[:, cols], do_ref[:, cols]
            p = _xa_probs(qh, kh, dh)
            dkv_ref[:, vcols] += _dot(p.astype(bf16), doh, TN)
            dp = _dot(doh, kv_ref[:, vcols], NT)
            ds = (p * (dp - jnp.sum(dp * p, axis=1, keepdims=True)) * (dh ** -0.5)).astype(bf16)
            dq_ref[:, cols] = _dot(ds, kh, NN).astype(bf16)
            dkv_ref[:, cols] += _dot(ds, qh, TN)

    row = pl.BlockSpec((tq, d), lambda b, i: (b * nq + i, 0))
    kvs = pl.BlockSpec((mem, 2 * d), lambda b, i: (b, 0))
    return pl.pallas_call(body, out_shape=(SDS((t_all, d), bf16), SDS(kv.shape, f32)), grid=(nb, nq),
                          in_specs=[row, kvs, row], out_specs=(row, kvs), name=name, compiler_params=_params(2))(q, kv, do)


FFN_BR = 256
FFN_CHUNK = 256


def _conv3(ext, w_ref, b, cols, lo, rows):
    return (b + w_ref[2:3, cols] * ext[lo:lo + rows, :] + w_ref[1:2, cols] * ext[lo - 1:lo - 1 + rows, :]
            + w_ref[0:1, cols] * ext[lo - 2:lo - 2 + rows, :])


FFN_HALO = 16


def _ffn_gate_fwd(up, cw, cb, seq, name):
    t_all, f2 = up.shape
    ff, br, ch, hl = f2 // 2, min(FFN_BR, seq), FFN_CHUNK, FFN_HALO
    per_seq, hb = seq // br, br // hl

    def body(up_ref, halo_ref, cw_ref, cb_ref, o_ref, cv_ref, ev, eg):
        i = pl.program_id(0)
        keep = jnp.where(i % per_seq == 0, 0.0, 1.0)
        for c0 in range(0, ff, ch):
            convs = []
            for ext, off in ((ev, c0), (eg, ff + c0)):
                cols = slice(off, off + ch)
                ext[0:hl, :] = halo_ref[:, cols].astype(f32) * keep
                ext[hl:hl + br, :] = up_ref[:, cols].astype(f32)
                conv = _conv3(ext, cw_ref, cb_ref[:, cols], cols, hl, br)
                cv_ref[:, cols] = conv.astype(bf16)
                convs.append(conv)
            val, gate = convs
            o_ref[:, c0:c0 + ch] = (gate * _sigmoid(gate) * val).astype(bf16)

    return pl.pallas_call(
        body, out_shape=(SDS((t_all, ff), bf16), SDS((t_all, f2), bf16)), grid=(t_all // br,),
        in_specs=[pl.BlockSpec((br, f2), lambda i: (i, 0)),
                  pl.BlockSpec((hl, f2), lambda i: (jnp.maximum(i * hb - 1, 0), 0)),
                  pl.BlockSpec(cw.shape, lambda i: (0, 0)), pl.BlockSpec(cb.shape, lambda i: (0, 0))],
        out_specs=(pl.BlockSpec((br, ff), lambda i: (i, 0)), pl.BlockSpec((br, f2), lambda i: (i, 0))),
        scratch_shapes=[pltpu.VMEM((br + hl, ch), f32), pltpu.VMEM((br + hl, ch), f32)],
        name=name, compiler_params=_params(1))(up, up, cw, cb)


def _ffn_gate_bwd(dact, up, cv, cw, seq, name):
    t_all, f2 = up.shape
    ff, br, ch, hl = f2 // 2, min(FFN_BR, seq), FFN_CHUNK, FFN_HALO
    per_seq, hb, last = seq // br, br // hl, t_all // hl - 1
    ext_rows = br + SUBLANES

    def body(da_ref, dan_ref, cv_ref, cvn_ref, up_ref, upp_ref, cw_ref, du_ref, dcw_ref, dcb_ref, ext, e1, e2, e3, dcv, dcg):
        i = pl.program_id(0)

        @pl.when(i == 0)
        def _():
            dcw_ref[...] = jnp.zeros_like(dcw_ref)
            dcb_ref[...] = jnp.zeros_like(dcb_ref)

        keep_prev = jnp.where(i % per_seq == 0, 0.0, 1.0)
        keep_next = jnp.where((i + 1) % per_seq == 0, 0.0, 1.0)

        def with_next(scr, blk_ref, nxt_ref, cols, scale):
            scr[0:br, :] = blk_ref[:, cols].astype(f32)
            scr[br:br + hl, :] = nxt_ref[:, cols].astype(f32) * scale
            return scr[0:ext_rows, :]

        for c0 in range(0, ff, ch):
            da = with_next(e1, da_ref, dan_ref, slice(c0, c0 + ch), keep_next)
            val = with_next(e2, cv_ref, cvn_ref, slice(c0, c0 + ch), 1.0)
            gate = with_next(e3, cv_ref, cvn_ref, slice(ff + c0, ff + c0 + ch), 1.0)
            sg = _sigmoid(gate)
            dcv[...] = da * gate * sg
            dcg[...] = da * val * sg * (1.0 + gate * (1.0 - sg))
            for dc, off in ((dcv, c0), (dcg, ff + c0)):
                cols = slice(off, off + ch)
                du = (cw_ref[2:3, cols] * dc[0:br, :] + cw_ref[1:2, cols] * dc[1:br + 1, :]
                      + cw_ref[0:1, cols] * dc[2:br + 2, :])
                du_ref[:, cols] = du.astype(bf16)
                d0 = dc[0:br, :]
                dcb_ref[:, cols] += _sum8(d0)
                ext[0:hl, :] = upp_ref[:, cols].astype(f32) * keep_prev
                ext[hl:hl + br, :] = up_ref[:, cols].astype(f32)
                for tap in range(3):
                    lo = hl - (2 - tap)
                    dcw_ref[tap, :, cols] += _sum8(d0 * ext[lo:lo + br, :])

    blk = lambda n: pl.BlockSpec((br, n), lambda i: (i, 0))
    prev = lambda n: pl.BlockSpec((hl, n), lambda i: (jnp.maximum(i * hb - 1, 0), 0))
    nxt = lambda n: pl.BlockSpec((hl, n), lambda i: (jnp.minimum((i + 1) * hb, last), 0))
    return pl.pallas_call(
        body, out_shape=(SDS((t_all, f2), bf16), SDS((3, SUBLANES, f2), f32), SDS((SUBLANES, f2), f32)), grid=(t_all // br,),
        in_specs=[blk(ff), nxt(ff), blk(f2), nxt(f2), blk(f2), prev(f2), pl.BlockSpec(cw.shape, lambda i: (0, 0))],
        out_specs=(blk(f2), pl.BlockSpec((3, SUBLANES, f2), lambda i: (0, 0, 0)), pl.BlockSpec((SUBLANES, f2), lambda i: (0, 0))),
        scratch_shapes=[pltpu.VMEM((br + hl, ch), f32)] * 4 + [pltpu.VMEM((ext_rows, ch), f32)] * 2,
        name=name, compiler_params=_params(1))(dact, dact, cv, cv, up, up, cw)


SSM_GB = 8
SSM_PLANES = 8
SSM_ROWS = 256
SSM_UNROLL = 8


def _ssm_pitch(seq):
    p = seq + SUBLANES
    assert (p // SUBLANES) % 2 == 1
    return p


def _rows(base, rc):
    return pl.ds(pl.multiple_of(base + rc * SSM_ROWS, SUBLANES), SSM_ROWS)


def _ssm_project_in(u_ref, b_ref, planes, e, seq, pitch):
    def chunk(rc, _):
        uc = u_ref[_rows(e * seq, rc), :].astype(bf16)
        for j in range(SSM_PLANES):
            planes[_rows(j * pitch, rc), :] = _dot(uc, b_ref[:, j * LANES:(j + 1) * LANES], NN)
        return 0
    lax.fori_loop(0, seq // SSM_ROWS, chunk, 0)


def _ssm_rows(planes, rc, pitch):
    return jnp.concatenate([planes[_rows(j * pitch, rc), :].astype(bf16) for j in range(SSM_PLANES)], axis=1)


def _ssm_scan(planes_list, l1, l2, seq, pitch, reverse=False):
    def step(s, hs):
        hs = list(hs)
        for k in range(SSM_UNROLL):
            t = s * SSM_UNROLL + k
            t = seq - 1 - t if reverse else t
            for e, planes in enumerate(planes_list):
                hs[e] = hs[e] * l1 + pltpu.roll(hs[e], 4, 0) * l2 + planes[pl.ds(t, SUBLANES, stride=pitch), :]
                planes[pl.ds(t, SUBLANES, stride=pitch), :] = hs[e]
        return tuple(hs)
    zero = jnp.zeros((SUBLANES, LANES), f32)
    lax.fori_loop(0, seq // SSM_UNROLL, step, tuple(zero for _ in planes_list))


def _ssm_fwd(u, b_big, c_big, lslab, dskip, seq, name, comm=None):
    t_all, w = u.shape
    nb, gw, pitch = t_all // seq, SSM_GB * SSM_GROUP, _ssm_pitch(seq)
    assert gw == LANES

    def body(u_ref, b_ref, c_ref, l_ref, d_ref, y_ref, *planes):
        l1, l2 = l_ref[0:SUBLANES, :], l_ref[SUBLANES:2 * SUBLANES, :]
        for e in range(nb):
            _ssm_project_in(u_ref, b_ref, planes[e], e, seq, pitch)
        _ssm_scan(planes, l1, l2, seq, pitch)
        for e in range(nb):
            def chunk(rc, _, e=e):
                rows = _rows(e * seq, rc)
                y_ref[rows, :] = _dot(_ssm_rows(planes[e], rc, pitch), c_ref[...], NN) + d_ref[...] * u_ref[rows, :]
                return 0
            lax.fori_loop(0, seq // SSM_ROWS, chunk, 0)

    (y,), extra = _call(
        body, ins=[u, b_big, c_big, lslab, dskip], out_shape=[SDS((t_all, w), f32)], grid=(w // gw,),
        in_specs=[pl.BlockSpec((t_all, gw), lambda k: (0, k)), pl.BlockSpec((None,) + b_big.shape[1:], lambda k: (k, 0, 0)),
                  pl.BlockSpec((None,) + c_big.shape[1:], lambda k: (k, 0, 0)),
                  pl.BlockSpec((None,) + lslab.shape[1:], lambda k: (k, 0, 0)), pl.BlockSpec((1, gw), lambda k: (0, k))],
        out_specs=[pl.BlockSpec((t_all, gw), lambda k: (0, k))],
        scratch_shapes=[pltpu.VMEM((SSM_PLANES * pitch, LANES), f32) for _ in range(nb)], name=name, comm=comm)
    return y, extra


def _ssm_bwd(u, dy, b_big, c_big, lslab, dskip, seq, name, comm=None):
    t_all, w = u.shape
    nb, gw, pitch = t_all // seq, SSM_GB * SSM_GROUP, _ssm_pitch(seq)

    def body(u_ref, dy_ref, b_ref, c_ref, l_ref, d_ref, du_ref, db_ref, dc_ref, dl_ref, dd_ref, *planes):
        hp, ap = planes[:nb], planes[nb:]
        l1, l2 = l_ref[0:SUBLANES, :], l_ref[SUBLANES:2 * SUBLANES, :]
        for e in range(nb):
            _ssm_project_in(u_ref, b_ref, hp[e], e, seq, pitch)
        _ssm_scan(hp, l1, l2, seq, pitch)
        dd_ref[...] = jnp.zeros_like(dd_ref)
        dc_ref[...] = jnp.zeros_like(dc_ref)
        db_ref[...] = jnp.zeros_like(db_ref)
        for e in range(nb):
            def chunk(rc, _, e=e):
                rows = _rows(e * seq, rc)
                dyc = dy_ref[rows, :]
                dyb = dyc.astype(bf16)
                for j in range(SSM_PLANES):
                    ap[e][_rows(j * pitch, rc), :] = _dot(dyb, c_ref[j * LANES:(j + 1) * LANES, :], NT)
                dd_ref[...] += _sum8(dyc * u_ref[rows, :])
                dc_ref[...] += _dot(_ssm_rows(hp[e], rc, pitch), dyb, TN)
                return 0
            lax.fori_loop(0, seq // SSM_ROWS, chunk, 0)

        def step(s, carry):
            carry = [list(c) for c in carry]
            for k in range(SSM_UNROLL):
                t = seq - 1 - (s * SSM_UNROLL + k)
                for e in range(nb):
                    a, s1, s2 = carry[e]
                    a = a * l1 - pltpu.roll(a, 4, 0) * l2 + ap[e][pl.ds(t, SUBLANES, stride=pitch), :]
                    ap[e][pl.ds(t, SUBLANES, stride=pitch), :] = a
                    hprev = hp[e][pl.ds(jnp.maximum(t - 1, 0), SUBLANES, stride=pitch), :] * jnp.where(t > 0, 1.0, 0.0)
                    carry[e] = [a, s1 + a * hprev, s2 + a * pltpu.roll(hprev, 4, 0)]
            return tuple(tuple(c) for c in carry)
        zero = jnp.zeros((SUBLANES, LANES), f32)
        fin = lax.fori_loop(0, seq // SSM_UNROLL, step, tuple((zero, zero, zero) for _ in range(nb)))
        dl_ref[0:SUBLANES, :] = sum(f[1] for f in fin)
        dl_ref[SUBLANES:2 * SUBLANES, :] = sum(f[2] for f in fin)

        for e in range(nb):
            def chunk2(rc, _, e=e):
                rows = _rows(e * seq, rc)
                ar = _ssm_rows(ap[e], rc, pitch)
                du_ref[rows, :] = (_dot(ar, b_ref[...], NT) + d_ref[...] * dy_ref[rows, :]).astype(bf16)
                db_ref[...] += _dot(u_ref[rows, :].astype(bf16), ar, TN)
                return 0
            lax.fori_loop(0, seq // SSM_ROWS, chunk2, 0)

    col = pl.BlockSpec((t_all, gw), lambda k: (0, k))
    per = lambda s: pl.BlockSpec((None,) + s[1:], lambda k: (k, 0, 0))
    ng = w // gw
    res, extra = _call(
        body, ins=[u, dy, b_big, c_big, lslab, dskip],
        out_shape=[SDS((t_all, w), bf16), SDS(b_big.shape, f32), SDS(c_big.shape, f32), SDS((ng, 2 * SUBLANES, LANES), f32),
                   SDS((SUBLANES, w), f32)],
        grid=(ng,),
        in_specs=[col, col, per(b_big.shape), per(c_big.shape), per(lslab.shape), pl.BlockSpec((1, gw), lambda k: (0, k))],
        out_specs=[col, per(b_big.shape), per(c_big.shape), per((ng, 2 * SUBLANES, LANES)), pl.BlockSpec((SUBLANES, gw), lambda k: (0, k))],
        scratch_shapes=[pltpu.VMEM((SSM_PLANES * pitch, LANES), f32) for _ in range(2 * nb)], name=name, comm=comm)
    return (*res, extra)


def _dense(col):
    return col.reshape(-1, LANES)


def _zoh(a, b, dtv):
    mag, ang = jnp.exp(a * dtv), b * dtv
    cs, sn = jnp.cos(ang), jnp.sin(ang)
    lr, li = mag * cs, mag * sn
    nr, den = lr - 1.0, a * a + b * b
    return lr, li, (nr * a + li * b) / den, (li * a - nr * b) / den, mag, cs, sn, nr, den


def _ssm_disc_fwd(lam_re, lam_im, dt, b_re, b_im, name):
    def states(a_ref, b_ref, dt_ref, lr_ref, li_ref, cr_ref, ci_ref):
        lr_ref[...], li_ref[...], cr_ref[...], ci_ref[...] = _zoh(a_ref[...], b_ref[...], dt_ref[...])[:4]

    def maps(cr_ref, ci_ref, br_ref, bi_ref, bbr_ref, bbi_ref):
        cr, ci = cr_ref[...], ci_ref[...]
        bbr_ref[...] = cr * br_ref[...] - ci * bi_ref[...]
        bbi_ref[...] = cr * bi_ref[...] + ci * br_ref[...]

    c, m = SDS(_dense(lam_re).shape, f32), SDS(b_re.shape, f32)
    lr, li, cr, ci = pl.pallas_call(states, out_shape=(c, c, c, c), name=name)(_dense(lam_re), _dense(lam_im), _dense(dt))
    cr, ci = cr.reshape(lam_re.shape), ci.reshape(lam_re.shape)
    bbr, bbi = pl.pallas_call(maps, out_shape=(m, m), name=name + "_maps")(cr, ci, b_re, b_im)
    return lr.reshape(lam_re.shape), li.reshape(lam_re.shape), cr, ci, bbr, bbi


def _ssm_disc_bwd(lam_re, lam_im, dt, coef_re, coef_im, b_re, b_im, g_lr, g_li, g_bbr, g_bbi, name):
    def maps(cr_ref, ci_ref, br_ref, bi_ref, gbr_ref, gbi_ref, dbr_ref, dbi_ref, dcr_ref, dci_ref):
        cr, ci = cr_ref[...], ci_ref[...]
        gbr, gbi, brv, biv = gbr_ref[...], gbi_ref[...], br_ref[...], bi_ref[...]
        dbr_ref[...] = cr * gbr + ci * gbi
        dbi_ref[...] = cr * gbi - ci * gbr
        dcr_ref[...] = jnp.sum(brv * gbr + biv * gbi, axis=1, keepdims=True)
        dci_ref[...] = jnp.sum(brv * gbi - biv * gbr, axis=1, keepdims=True)

    def states(a_ref, b_ref, dt_ref, glr_ref, gli_ref, dcr_ref, dci_ref, da_ref, db_ref, ddt_ref):
        a, b, dtv = a_ref[...], b_ref[...], dt_ref[...]
        lr, li, cr, ci, mag, cs, sn, nr, den = _zoh(a, b, dtv)
        dcr, dci = dcr_ref[...], dci_ref[...]
        dnum_r, dnum_i = dcr / den, dci / den
        dden = -(dcr * cr + dci * ci) / den
        dnr = dnum_r * a - dnum_i * b
        dli = gli_ref[...] + dnum_r * b + dnum_i * a
        dlr = glr_ref[...] + dnr
        dmag, dang = dlr * cs + dli * sn, dli * lr - dlr * li
        dadt = dmag * mag
        da_ref[...] = dnum_r * nr + dnum_i * li + dden * 2.0 * a + dadt * dtv
        db_ref[...] = dnum_r * li - dnum_i * nr + dden * 2.0 * b + dang * dtv
        ddt_ref[...] = dadt * a + dang * b

    col, m = SDS(lam_re.shape, f32), SDS(b_re.shape, f32)
    d_br, d_bi, dcr, dci = pl.pallas_call(maps, out_shape=(m, m, col, col), name=name + "_maps")(coef_re, coef_im, b_re, b_im, g_bbr, g_bbi)
    c = SDS(_dense(lam_re).shape, f32)
    d_a, d_b, d_dt = pl.pallas_call(states, out_shape=(c, c, c), name=name)(
        _dense(lam_re), _dense(lam_im), _dense(dt), _dense(g_lr), _dense(g_li), _dense(dcr), _dense(dci))
    return d_a.reshape(lam_re.shape), d_b.reshape(lam_re.shape), d_dt.reshape(lam_re.shape), d_br, d_bi


def _place():
    x, y, c = lax.axis_index("x"), lax.axis_index("y"), lax.axis_index("c")
    return x, y, c, 2 * x + y


def _half_axis(shape, ax):
    return 0 if shape[0] == 2 else (3 - ax)


def _sub(ref, axis, start, size):
    idx = [slice(None)] * len(ref.shape)
    idx[axis] = pl.ds(start, size)
    return ref.at[tuple(idx)]


def _region(ref, full_shape, ax, slot=None, half=None):
    if slot is not None:
        n = full_shape[ax] // N_CHIPS
        ref = _sub(ref, ax, slot * n, n)
    if half is not None:
        ha = _half_axis(full_shape, ax)
        n = full_shape[ha] // 2
        ref = _sub(ref, ha, half * n, n)
    return ref


def _halved(shape, axis):
    return tuple(s // 2 if a == axis else s for a, s in enumerate(shape))


class _Comm:
    def __init__(self, ins, out_shapes, aliases, scratch, start, finish):
        self.ins, self.out_shapes, self.aliases, self.scratch, self.start, self.finish = ins, out_shapes, aliases, scratch, start, finish


def _call(body, *, ins, in_specs, out_shape, out_specs, grid, scratch_shapes, name, comm=None):
    if comm is None:
        res = pl.pallas_call(body, out_shape=tuple(out_shape), grid=grid, in_specs=list(in_specs), out_specs=tuple(out_specs),
                             scratch_shapes=list(scratch_shapes), name=name, compiler_params=_params(len(grid)))(*ins)
        return list(res), []
    n_in, n_out, n_scr, c_in, c_out = len(ins), len(out_shape), len(scratch_shapes), len(comm.ins), len(comm.out_shapes)

    def fused(*refs):
        pos = [n_in, n_in + c_in, n_in + c_in + n_out, n_in + c_in + n_out + c_out, n_in + c_in + n_out + c_out + n_scr]
        in_refs, cin, out_refs, cout, scr, cscr = (refs[:pos[0]], refs[pos[0]:pos[1]], refs[pos[1]:pos[2]], refs[pos[2]:pos[3]],
                                                   refs[pos[3]:pos[4]], refs[pos[4]:])
        ids = [pl.program_id(a) for a in range(len(grid))]
        first, last = ids[0] == 0, ids[0] == grid[0] - 1
        for a in range(1, len(grid)):
            first, last = first & (ids[a] == 0), last & (ids[a] == grid[a] - 1)

        @pl.when(first)
        def _():
            comm.start(cin, cout, cscr)

        body(*in_refs, *out_refs, *scr)

        @pl.when(last)
        def _():
            comm.finish(cin, cout, cscr)

    res = pl.pallas_call(
        fused, out_shape=tuple(out_shape) + tuple(comm.out_shapes), grid=grid, in_specs=list(in_specs) + [ANY] * c_in,
        out_specs=tuple(out_specs) + tuple([ANY] * c_out), scratch_shapes=list(scratch_shapes) + list(comm.scratch),
        input_output_aliases={n_in + i: n_out + o for i, o in comm.aliases}, name=name, compiler_params=_params(len(grid)))(*ins, *comm.ins)
    return list(res[:n_out]), list(res[n_out:])


def _comm_only(comm, name):
    c_in, c_out = len(comm.ins), len(comm.out_shapes)

    def body(*refs):
        cin, cout, cscr = refs[:c_in], refs[c_in:c_in + c_out], refs[c_in + c_out:]
        comm.start(cin, cout, cscr)
        comm.finish(cin, cout, cscr)

    return pl.pallas_call(body, out_shape=tuple(comm.out_shapes), in_specs=[ANY] * c_in, out_specs=tuple([ANY] * c_out),
                          scratch_shapes=list(comm.scratch), input_output_aliases=dict(comm.aliases), name=name)(*comm.ins)


def _gather_plan(shards, axes):
    n = len(shards)
    fulls = [tuple(s * N_CHIPS if a == ax else s for a, s in enumerate(sh.shape)) for sh, ax in zip(shards, axes)]
    own = 6

    def copies(src, dst, scr, sends_only=False):
        send_sems, recv_sems = scr
        x, y, c, p = _place()
        chips = [(1 - x, y), (x, 1 - y), (1 - x, 1 - y)]
        slots = [2 * cx + cy for cx, cy in chips]

        def copy(a, k, slot, half, to, from_shard=False):
            where = _region(dst[a], fulls[a], axes[a], slot, half)
            source = where
            if from_shard:
                ha = _half_axis(fulls[a], axes[a])
                hn = fulls[a][ha] // 2
                source = _sub(src[a], ha, half * hn, hn)
            return pltpu.make_async_remote_copy(src_ref=source, dst_ref=where, send_sem=send_sems.at[a, k],
                                                recv_sem=recv_sems.at[a, k], device_id=to, device_id_type=MESH)

        parts = range(n)
        mine = [pltpu.make_async_remote_copy(src_ref=src[a], dst_ref=_region(dst[a], fulls[a], axes[a], p),
                                             send_sem=send_sems.at[a, own], recv_sem=recv_sems.at[a, own],
                                             device_id=(x, y, 1 - c), device_id_type=MESH) for a in parts]
        first = [copy(a, j, p, c, (*chips[j], c), True) for a in parts for j in range(3)]
        if sends_only:
            return mine, first
        landed = [copy(a, j, slots[j], c, (x, y, c)) for a in parts for j in range(3)]
        passed = [copy(a, 3 + j, slots[j], c, (x, y, 1 - c)) for a in parts for j in range(3)]
        handed = [copy(a, 3 + j, slots[j], 1 - c, (x, y, c)) for a in parts for j in range(3)]
        return mine, first, landed, passed, handed

    def start(src, dst, scr):
        mine, first = copies(src, dst, scr, sends_only=True)
        for cp in first + mine:
            cp.start()

    def finish(src, dst, scr):
        mine, first, landed, passed, handed = copies(src, dst, scr)
        for arrived, fwd in zip(landed, passed):
            arrived.wait_recv()
            fwd.start()
        for cp in handed + mine:
            cp.wait_recv()
        for cp in first + passed + mine:
            cp.wait_send()

    return _Comm(list(shards), [SDS(f, s.dtype) for f, s in zip(fulls, shards)], [],
                 [pltpu.SemaphoreType.DMA((n, 7)), pltpu.SemaphoreType.DMA((n, 7))], start, finish)


def _all_gather(shards, axes, name):
    return _comm_only(_gather_plan(shards, axes), name)


def _swap_halves(grads, axes, name):
    n = len(grads)
    shapes = [g.shape for g in grads]

    def body(*refs):
        src, dst = refs[:n], refs[n:2 * n]
        send_sems, recv_sems = refs[2 * n:]
        x, y, c, _ = _place()
        cps = [pltpu.make_async_remote_copy(src_ref=_region(src[a], shapes[a], axes[a], None, 1 - c), dst_ref=dst[a],
                                            send_sem=send_sems.at[a], recv_sem=recv_sems.at[a],
                                            device_id=(x, y, 1 - c), device_id_type=MESH) for a in range(n)]
        for cp in cps:
            cp.start()
        for cp in cps:
            cp.wait()

    outs = tuple(SDS(_halved(s, _half_axis(s, ax)), g.dtype) for s, ax, g in zip(shapes, axes, grads))
    return pl.pallas_call(body, out_shape=outs, in_specs=[ANY] * n, out_specs=tuple([ANY] * n),
                          scratch_shapes=[pltpu.SemaphoreType.DMA((n,)), pltpu.SemaphoreType.DMA((n,))], name=name)(*grads)


def _row_block(rows, row_bytes, limit=3 << 20):
    for b in (1024, 512, 256, 128, 64, 32, 16, 8):
        if rows % b == 0 and b * row_bytes <= limit:
            return b
    return rows


def _add_own_half(g, other, ax, cidx, name):
    _, kp, np_ = other.shape
    ha = _half_axis(g.shape, ax)
    ks, ns = (kp // N_CHIPS, np_) if ax == 1 else (kp, np_ // N_CHIPS)
    bk = _row_block(ks, ns * 4)
    nkb = ks // bk

    def g_map(q, i, cref):
        c = cref[0]
        if ax == 1:
            return (c, q * nkb + i, 0) if ha == 0 else (0, q * nkb + i, c)
        return (c, i, q) if ha == 0 else (0, c * nkb + i, q)

    def o_map(q, i, cref):
        return (0, q * nkb + i, 0) if ax == 1 else (0, i, q)

    def body(c_ref, g_ref, o_ref, send_ref, land_ref):
        del c_ref
        s = (g_ref[...].astype(f32) + o_ref[...].astype(f32)).astype(send_ref.dtype)
        send_ref[...] = s
        land_ref[...] = s

    out = pl.BlockSpec((None, bk, ns), lambda q, i, cref: (q, i, 0))
    grid_spec = pltpu.PrefetchScalarGridSpec(
        num_scalar_prefetch=1, grid=(N_CHIPS, nkb),
        in_specs=[pl.BlockSpec((None, bk, ns), g_map), pl.BlockSpec((None, bk, ns), o_map)], out_specs=(out, out))
    shape = SDS((N_CHIPS, ks, ns), g.dtype)
    return pl.pallas_call(body, out_shape=(shape, shape), grid_spec=grid_spec, name=name, compiler_params=_params(2))(cidx, g, other)


def _owner_plan(sends, lands):
    n = len(sends)

    def copies(cin, dst, scr, arrivals):
        src = cin[:n]
        send_sems, recv_sems = scr
        x, y, c, p = _place()
        chips = [(1 - x, y), (x, 1 - y), (1 - x, 1 - y)]
        slots = [2 * cx + cy for cx, cy in chips]
        if arrivals:
            return [pltpu.make_async_remote_copy(src_ref=src[a].at[p], dst_ref=dst[a].at[slots[j]], send_sem=send_sems.at[a, j],
                                                 recv_sem=recv_sems.at[a, j], device_id=(x, y, c), device_id_type=MESH)
                    for a in range(n) for j in range(3)]
        return [pltpu.make_async_remote_copy(src_ref=src[a].at[slots[j]], dst_ref=dst[a].at[p], send_sem=send_sems.at[a, j],
                                             recv_sem=recv_sems.at[a, j], device_id=(*chips[j], c), device_id_type=MESH)
                for a in range(n) for j in range(3)]

    def start(cin, dst, scr):
        for cp in copies(cin, dst, scr, False):
            cp.start()

    def finish(cin, dst, scr):
        for cp in copies(cin, dst, scr, True):
            cp.wait_recv()
        for cp in copies(cin, dst, scr, False):
            cp.wait_send()

    return _Comm(list(sends) + list(lands), [SDS(l.shape, l.dtype) for l in lands], [(n + a, a) for a in range(n)],
                 [pltpu.SemaphoreType.DMA((n, 3)), pltpu.SemaphoreType.DMA((n, 3))], start, finish)


def _sum_chips(stack, shard_shape, ax, cidx, name):
    _, ks, ns = stack.shape
    ha = _half_axis(shard_shape, ax)
    bk = _row_block(ks, ns * 4 * N_CHIPS)
    nkb = ks // bk

    def o_map(i, cref):
        c = cref[0]
        return (c, i, 0) if ha == 0 else ((0, c * nkb + i, 0) if ha == 1 else (0, i, c))

    def body(c_ref, s_ref, o_ref):
        del c_ref
        acc = s_ref[0].astype(f32)
        for q in range(1, N_CHIPS):
            acc = acc + s_ref[q].astype(f32)
        o_ref[...] = acc

    grid_spec = pltpu.PrefetchScalarGridSpec(
        num_scalar_prefetch=1, grid=(nkb,), in_specs=[pl.BlockSpec((N_CHIPS, bk, ns), lambda i, cref: (0, i, 0))],
        out_specs=pl.BlockSpec((None, bk, ns), o_map))
    return pl.pallas_call(body, out_shape=SDS(shard_shape, f32), grid_spec=grid_spec, name=name, compiler_params=_params(1))(cidx, stack)


def _join_halves(slices, axes, name):
    n = len(slices)

    def body(*refs):
        dst = refs[n:2 * n]
        send_sems, recv_sems = refs[2 * n:]
        x, y, c, _ = _place()

        def half(a, h):
            ha = _half_axis(slices[a].shape, axes[a])
            hn = slices[a].shape[ha] // 2
            return _sub(dst[a], ha, h * hn, hn)

        cps = [pltpu.make_async_remote_copy(src_ref=half(a, c), dst_ref=half(a, c), send_sem=send_sems.at[a], recv_sem=recv_sems.at[a],
                                            device_id=(x, y, 1 - c), device_id_type=MESH) for a in range(n)]
        for cp in cps:
            cp.start()
        for a in range(n):
            pltpu.make_async_remote_copy(src_ref=half(a, c), dst_ref=half(a, 1 - c), send_sem=send_sems.at[a], recv_sem=recv_sems.at[a],
                                         device_id=(x, y, c), device_id_type=MESH).wait_recv()
        for cp in cps:
            cp.wait_send()

    return pl.pallas_call(
        body, out_shape=tuple(SDS(s.shape, s.dtype) for s in slices), in_specs=[ANY] * n, out_specs=tuple([ANY] * n),
        scratch_shapes=[pltpu.SemaphoreType.DMA((n,)), pltpu.SemaphoreType.DMA((n,))],
        input_output_aliases={a: a for a in range(n)}, name=name)(*slices)


def _core_index():
    return jnp.reshape(lax.axis_index("c"), (1,)).astype(jnp.int32)


def _reduce_begin(grads, axes, tag):
    cidx = _core_index()
    others = _swap_halves(grads, axes, f"rs_swap_{tag}")
    pairs = [_add_own_half(g, o, ax, cidx, f"rs_add_{tag}_{a}") for a, (g, o, ax) in enumerate(zip(grads, others, axes))]
    return _owner_plan([s for s, _ in pairs], [l for _, l in pairs])


def _reduce_end(stacks, shapes, axes, tag):
    cidx = _core_index()
    shard_shapes = [tuple(s // N_CHIPS if i == ax else s for i, s in enumerate(sh)) for sh, ax in zip(shapes, axes)]
    slices = [_sum_chips(s, sh, ax, cidx, f"rs_sum_{tag}_{a}") for a, (s, sh, ax) in enumerate(zip(stacks, shard_shapes, axes))]
    return _join_halves(slices, axes, f"rs_join_{tag}")


SMALL_COLS = 256


def _pack(arrays, rows_multiple):
    flat = jnp.concatenate([a.reshape(-1).astype(f32) for a in arrays])
    rows = -(-flat.shape[0] // SMALL_COLS)
    rows = -(-rows // rows_multiple) * rows_multiple
    flat = jnp.pad(flat, (0, rows * SMALL_COLS - flat.shape[0]))
    return flat.reshape(1, rows, SMALL_COLS)


def _unpack(buf, shapes):
    flat, out, off = buf.reshape(-1), [], 0
    for s in shapes:
        n = math.prod(s)
        out.append(flat[off:off + n].reshape(s))
        off += n
    return out


def _block_diag_in(bb):
    g, p, c = bb.shape
    k = g // SSM_GB
    eye = jnp.eye(SSM_GB, dtype=bb.dtype)
    return jnp.einsum("kgpc,gh->kgchp", bb.reshape(k, SSM_GB, p, c), eye).reshape(k, SSM_GB * c, SSM_GB * p)


def _block_diag_out(cc):
    g, c, p = cc.shape
    k = g // SSM_GB
    eye = jnp.eye(SSM_GB, dtype=cc.dtype)
    return jnp.einsum("kgcp,gh->kgphc", cc.reshape(k, SSM_GB, c, p), eye).reshape(k, SSM_GB * p, SSM_GB * c)


def _diag_in(db, p, c):
    k = db.shape[0]
    return jnp.einsum("kgcgp->kgpc", db.reshape(k, SSM_GB, c, SSM_GB, p)).reshape(k * SSM_GB, p, c)


def _diag_out(dc, p, c):
    k = dc.shape[0]
    return jnp.einsum("kgpgc->kgcp", dc.reshape(k, SSM_GB, p, SSM_GB, c)).reshape(k * SSM_GB, c, p)


def _state_slab(v):
    g, p = v.shape
    return v.reshape(g // SSM_GB, SSM_GB * p // LANES, LANES)


BIG = ("ab_w_in", "ab_w_out", "ssm_w_in", "ssm_w_glu", "xa_w_q", "xa_w_kv", "xa_w_o", "ffn_w_up", "ffn_w_down")
BIG_AXIS = dict(ab_w_in=2, ab_w_out=1, ssm_w_in=1, ssm_w_glu=2, xa_w_q=1, xa_w_kv=2, xa_w_o=1, ffn_w_up=2, ffn_w_down=1)
SMALL_REPL = ("norm_mix", "norm_xattn", "norm_ffn", "norm_mem", "norm_final", "pool_w", "pool_scale", "ssm_lam_re", "ssm_lam_im",
              "ssm_log_dt", "ssm_b_re", "ssm_b_im", "ssm_c_re", "ssm_c_im", "ffn_conv_b")
SMALL_SHARDED = ("ssm_d", "ffn_conv_w")
FIRST_MIXER = ("ab_w_in", "ab_w_out")
WEIGHTS = ("norm_mix", "norm_xattn", "norm_ffn", "norm_mem", "norm_final", "ab_w_in", "pool_w", "pool_scale", "ab_w_out", "ssm_w_in",
           "ssm_lam_re", "ssm_lam_im", "ssm_log_dt", "ssm_b_re", "ssm_b_im", "ssm_c_re", "ssm_c_im", "ssm_d", "ssm_w_glu", "xa_w_q",
           "xa_w_kv", "xa_w_o", "ffn_w_up", "ffn_conv_w", "ffn_conv_b", "ffn_w_down")


class _Reducer:
    def __init__(self):
        self.done, self.groups = {}, 0

    def begin(self, keys, gw):
        self.groups += 1
        return _reduce_begin([gw[k] for k in keys], [BIG_AXIS.get(k[0], 1) for k in keys], f"g{self.groups}")

    def end(self, keys, gw, stacks):
        slices = _reduce_end(stacks, [gw[k].shape for k in keys], [BIG_AXIS.get(k[0], 1) for k in keys], f"g{self.groups}")
        self.done.update(zip(keys, slices))


def _local_step(xf, memf, tgt, w, wf, conv_w, ssm_d, seq, late_weights=None, reducer=None):
    d = xf.shape[1]
    depth = w["norm_mix"].shape[0]
    wf = dict(wf)
    late_weights = late_weights or {}
    sbw = wf["ab_w_in", 0].shape[2] // 4
    row = lambda a: a.reshape(1, -1)

    gs, ps = w["ssm_lam_re"].shape[1:]
    col = lambda a: a.reshape(gs * ps, 1)
    lam_re, lam_im = col(w["ssm_lam_re"][0]), col(w["ssm_lam_im"][0])
    dt = col(jnp.broadcast_to(jnp.exp(w["ssm_log_dt"][0])[:, None], (gs, ps)))
    b_re, b_im = w["ssm_b_re"][0].reshape(gs * ps, -1), w["ssm_b_im"][0].reshape(gs * ps, -1)
    lb_re, lb_im, coef_re, coef_im, bb_re, bb_im = _ssm_disc_fwd(lam_re, lam_im, dt, b_re, b_im, "ssm_disc")
    cgrp = b_re.shape[1]
    b_big = jnp.concatenate([_block_diag_in(bb_re.reshape(gs, ps, cgrp)), _block_diag_in(bb_im.reshape(gs, ps, cgrp))], axis=2).astype(bf16)
    c_big = jnp.concatenate([_block_diag_out(w["ssm_c_re"][0]), -_block_diag_out(w["ssm_c_im"][0])], axis=1).astype(bf16)
    lr_s, li_s = _state_slab(lb_re.reshape(gs, ps)), _state_slab(lb_im.reshape(gs, ps))
    lslab = jnp.concatenate([lr_s, lr_s, -li_s, li_s], axis=1)

    mem_n = _norm_fwd(memf, row(w["norm_mem"]), "norm_mem")
    kv = [None] * depth
    xs, saved = [xf], []
    cur = xf
    h_next = _norm_fwd(cur, row(w["norm_mix"][0]), "norm_mix0")
    for l in range(depth):
        sv = {}
        h = h_next
        sv["h"] = h
        if l % 2 == 0:
            qkv = _mm(h, wf["ab_w_in", 0], mode="nn", b_l=0, n=3 * sbw, out_dtype=bf16, name=f"qkv{l}")
            u = _mm(h, wf["ab_w_in", 0], mode="nn", b_l=0, b_n0=3 * sbw, n=sbw, out_dtype=f32, name=f"poolin{l}")
            plan, names = late_weights.get(f"sb_fwd{l}", (None, ()))
            mix, ltot, first, late = _sb_fwd(qkv, seq, f"sb_fwd{l}", comm=plan)
            wf.update(zip(names, late))
            pooled, mix = _pool_fwd(u, mix, w["pool_w"][0], w["pool_scale"], seq, f"pool_fwd{l}")
            sv.update(qkv=qkv, mix=mix, ltot=ltot, first=first, pooled=pooled)
            cur, hx = _mm(mix, wf["ab_w_out", 0], mode="nn", b_l=0, res=cur, out_dtype=f32, name=f"mixout{l}",
                          norm=("fwd", row(w["norm_xattn"][l])))
        else:
            us = _mm(h, wf["ssm_w_in", 0], mode="nn", b_l=0, out_dtype=f32, name=f"ssmin{l}")
            plan, names = late_weights.get(f"ssm_fwd{l}", (None, ()))
            ys, late = _ssm_fwd(us, b_big, c_big, lslab, ssm_d, seq, f"ssm_fwd{l}", comm=plan)
            wf.update(zip(names, late))
            gl = _gelu_fwd(ys, f"gelu{l}")
            glu = _mm(gl, wf["ssm_w_glu", 0], mode="nn", b_l=0, out_dtype=f32, name=f"glu{l}")
            sv.update(us=us, ys=ys, gl=gl, glu=glu)
            cur, hx = _glu_fwd(glu, cur, row(w["norm_xattn"][l]), f"glugate{l}")
        sv["x1"] = cur
        kv[l] = _mm(mem_n, wf["xa_w_kv", l], mode="nn", b_l=0, out_dtype=bf16, name=f"kv{l}")
        qx = _mm(hx, wf["xa_w_q", l], mode="nn", b_l=0, out_dtype=bf16, name=f"xaq{l}")
        ox = _xa_fwd(qx, kv[l], seq, f"xa_fwd{l}")
        cur, hf = _mm(ox, wf["xa_w_o", l], mode="nn", b_l=0, res=cur, out_dtype=f32, name=f"xao{l}",
                      norm=("fwd", row(w["norm_ffn"][l])))
        sv.update(hx=hx, qx=qx, ox=ox, x2=cur)
        up = _mm(hf, wf["ffn_w_up", l], mode="nn", b_l=0, out_dtype=bf16, name=f"ffnup{l}")
        act, cv = _ffn_gate_fwd(up, conv_w[l], row(w["ffn_conv_b"][l]), seq, f"ffn_gate{l}")
        if l + 1 < depth:
            cur, h_next = _mm(act, wf["ffn_w_down", l], mode="nn", b_l=0, res=cur, out_dtype=f32, name=f"ffndown{l}",
                              norm=("fwd", row(w["norm_mix"][l + 1])))
        else:
            cur = _mm(act, wf["ffn_w_down", l], mode="nn", b_l=0, res=cur, out_dtype=f32, name=f"ffndown{l}")
        sv.update(hf=hf, up=up, cv=cv, act=act)
        saved.append(sv)
        xs.append(cur)

    dx, g_final8, loss8 = _loss_head(cur, tgt, row(w["norm_final"]), "loss_head")

    gw = {}
    small = {"norm_final": jnp.sum(g_final8, axis=0)}
    g_mix, g_xa, g_ffn, g_cw, g_cb = [None] * depth, [None] * depth, [None] * depth, [None] * depth, [None] * depth
    dmem_n = None

    pending = []

    def wgrad(key, a, b, l, **kw):
        kw.setdefault("bk", 1024)
        gw[key, l] = _mm(a, b, mode="tn", out_dtype=bf16, out_l=0, out_layers=1, name=f"dw_{key}{l}", **kw)
        pending.append((key, l))

    def reduce_beside():
        if reducer is None or not pending:
            return None, []
        keys = list(pending)
        pending.clear()
        return reducer.begin(keys, gw), keys

    for l in reversed(range(depth)):
        sv = saved[l]
        dact = _mm(dx, wf["ffn_w_down", l], mode="nt", b_l=0, out_dtype=bf16, name=f"d_act{l}")
        wgrad("ffn_w_down", sv["act"], dx, l)
        dup, dcw8, dcb8 = _ffn_gate_bwd(dact, sv["up"], sv["cv"], conv_w[l], seq, f"ffn_gate_bwd{l}")
        g_cw[l], g_cb[l] = jnp.sum(dcw8, axis=1), jnp.sum(dcb8, axis=0)
        wgrad("ffn_w_up", sv["hf"], dup, l)
        dx, g8 = _mm(dup, wf["ffn_w_up", l], mode="nt", b_l=0, out_dtype=f32, name=f"d_hf{l}",
                     norm=("bwd", sv["x2"], dx, row(w["norm_ffn"][l])))
        g_ffn[l] = jnp.sum(g8, axis=0)
        dox = _mm(dx, wf["xa_w_o", l], mode="nt", b_l=0, out_dtype=bf16, name=f"d_ox{l}")
        wgrad("xa_w_o", sv["ox"], dx, l)
        dqx, dkv = _xa_bwd(sv["qx"], kv[l], dox, seq, f"xa_bwd{l}")
        wgrad("xa_w_kv", mem_n, dkv, l, bk=mem_n.shape[0])
        dmem_n = _mm(dkv, wf["xa_w_kv", l], mode="nt", b_l=0, res=dmem_n, out_dtype=f32, name=f"d_memn{l}")
        wgrad("xa_w_q", sv["hx"], dqx, l)
        dx, g8 = _mm(dqx, wf["xa_w_q", l], mode="nt", b_l=0, out_dtype=f32, name=f"d_hx{l}",
                     norm=("bwd", sv["x1"], dx, row(w["norm_xattn"][l])))
        g_xa[l] = jnp.sum(g8, axis=0)
        if l % 2 == 0:
            dmix = _mm(dx, wf["ab_w_out", 0], mode="nt", b_l=0, out_dtype=f32, name=f"d_mix{l}")
            comm, keys = reduce_beside()
            dq, dk, dv, stacks = _sb_bwd(sv["qkv"], sv["ltot"], sv["first"], dmix, seq, f"sb_bwd{l}", comm=comm)
            if comm is not None:
                reducer.end(keys, gw, stacks)
            wgrad("ab_w_out", sv["mix"], dx, 0)
            du, dpw, dps8 = _pool_bwd(dmix, sv["pooled"], w["pool_w"][0], w["pool_scale"], seq, f"pool_bwd{l}")
            small["pool_w"], small["pool_scale"] = dpw[None], jnp.sum(dps8, axis=0)[None]
            dproj = jnp.concatenate([dq, dk, dv, du], axis=1)
            wgrad("ab_w_in", sv["h"], dproj, 0)
            dx, g8 = _mm(dproj, wf["ab_w_in", 0], mode="nt", b_l=0, out_dtype=f32, name=f"d_h{l}",
                         norm=("bwd", xs[l], dx, row(w["norm_mix"][l])))
        else:
            dglu = _glu_bwd(dx, sv["glu"], f"glugate_bwd{l}")
            dgl = _mm(dglu, wf["ssm_w_glu", 0], mode="nt", b_l=0, out_dtype=f32, name=f"d_gelu{l}")
            dys = _gelu_bwd(dgl, sv["ys"], f"gelu_bwd{l}")
            comm, keys = reduce_beside()
            dus, db_big, dc_big, dl, dd8, stacks = _ssm_bwd(sv["us"], dys, b_big, c_big, lslab, ssm_d, seq, f"ssm_bwd{l}", comm=comm)
            if comm is not None:
                reducer.end(keys, gw, stacks)
            wgrad("ssm_w_glu", sv["gl"], dglu, 0)
            small["ssm_d"] = jnp.sum(dd8, axis=0)[None]
            half = SSM_PLANES // 2
            g_lr = (dl[:, 0:half] + dl[:, half:SUBLANES]).reshape(gs * ps, 1)
            g_li = (dl[:, SUBLANES + half:] - dl[:, SUBLANES:SUBLANES + half]).reshape(gs * ps, 1)
            g_bbr = _diag_in(db_big[:, :, :SSM_GB * ps], ps, cgrp).reshape(gs * ps, cgrp)
            g_bbi = _diag_in(db_big[:, :, SSM_GB * ps:], ps, cgrp).reshape(gs * ps, cgrp)
            d_a, d_b, d_dt, d_br, d_bi = _ssm_disc_bwd(lam_re, lam_im, dt, coef_re, coef_im, b_re, b_im, g_lr, g_li, g_bbr, g_bbi,
                                                       "ssm_disc_bwd")
            small["ssm_lam_re"], small["ssm_lam_im"] = d_a.reshape(1, gs, ps), d_b.reshape(1, gs, ps)
            small["ssm_log_dt"] = (jnp.sum(d_dt.reshape(gs, ps), axis=1) * dt.reshape(gs, ps)[:, 0])[None]
            small["ssm_b_re"], small["ssm_b_im"] = d_br.reshape(1, gs, ps, cgrp), d_bi.reshape(1, gs, ps, cgrp)
            small["ssm_c_re"] = _diag_out(dc_big[:, :SSM_GB * ps], ps, cgrp)[None]
            small["ssm_c_im"] = -_diag_out(dc_big[:, SSM_GB * ps:], ps, cgrp)[None]
            wgrad("ssm_w_in", sv["h"], dus, 0)
            dx, g8 = _mm(dus, wf["ssm_w_in", 0], mode="nt", b_l=0, out_dtype=f32, name=f"d_h{l}",
                         norm=("bwd", xs[l], dx, row(w["norm_mix"][l])))
        g_mix[l] = jnp.sum(g8, axis=0)

    small["norm_mem"] = jnp.sum(_norm_bwd_gain_only(dmem_n, memf, "norm_mem_bwd"), axis=0)
    small["norm_mix"], small["norm_xattn"], small["norm_ffn"] = jnp.stack(g_mix), jnp.stack(g_xa), jnp.stack(g_ffn)
    small["ffn_conv_w"], small["ffn_conv_b"] = jnp.stack(g_cw), jnp.stack(g_cb)
    return loss8, dx, gw, small, pending


def _step(x, mem, loss_target, w, m, v):
    nb, seq, d = x.shape
    t_all = nb * seq
    depth = w["norm_mix"].shape[0]
    chip = 2 * lax.axis_index("x") + lax.axis_index("y")

    small_mine = _pack([w[k] for k in SMALL_SHARDED], SUBLANES)
    gathered = _all_gather([w[k].astype(bf16) for k in FIRST_MIXER] + [small_mine], [BIG_AXIS[k] for k in FIRST_MIXER] + [1],
                           "gather_first")
    wf = {(k, 0): g for k, g in zip(FIRST_MIXER, gathered[:-1])}
    per_chip = gathered[-1].reshape(N_CHIPS, -1)
    pieces = [_unpack(per_chip[q], [w[k].shape for k in SMALL_SHARDED]) for q in range(N_CHIPS)]
    ssm_d = jnp.concatenate([pc[0] for pc in pieces], axis=-1)
    conv_w = jnp.concatenate([pc[1] for pc in pieces], axis=-1)
    ff2 = conv_w.shape[-1]
    late = [(k, l) for k in BIG if k not in FIRST_MIXER for l in range(w[k].shape[0])]
    groups = {"sb_fwd0": [kl for kl in late if kl[1] == 0], "ssm_fwd1": [kl for kl in late if kl[1] > 0]}
    late_weights = {hook: (_gather_plan([w[k][l:l + 1].astype(bf16) for k, l in keys], [BIG_AXIS[k] for k, _ in keys]), keys)
                    for hook, keys in groups.items()}

    reducer = _Reducer()
    loss8, dx, gw, small, pending = _local_step(x.reshape(t_all, d), mem.reshape(-1, d), loss_target.reshape(t_all, d), w, wf,
                                                conv_w, ssm_d, seq, late_weights=late_weights, reducer=reducer)
    loss = lax.psum(0.5 * jnp.sum(loss8) / d, ("x", "y", "c"))

    small_names = SMALL_REPL + SMALL_SHARDED
    small_full_shapes = [w[k].shape for k in SMALL_REPL] + [(1, d), (depth, 3, ff2)]
    gw["small", 0] = _pack([small[k] for k in small_names], 2 * N_CHIPS * SUBLANES)
    keys = pending + [("small", 0)]
    reducer.end(keys, gw, _comm_only(reducer.begin(keys, gw), "rs_owner_last"))
    g_big = {k: jnp.concatenate([reducer.done[k, l] for l in range(w[k].shape[0])], axis=0) for k in BIG}
    small_all = _all_gather([reducer.done["small", 0]], [1], "gather_small_grads")[0]
    g_small = dict(zip(small_names, _unpack(small_all, small_full_shapes)))
    g_small["ssm_d"] = lax.dynamic_slice_in_dim(g_small["ssm_d"], chip * (d // N_CHIPS), d // N_CHIPS, axis=1)
    g_small["ffn_conv_w"] = lax.dynamic_slice_in_dim(g_small["ffn_conv_w"], chip * (ff2 // N_CHIPS), ff2 // N_CHIPS, axis=2)
    grads = {**g_big, **g_small}

    delta, new_m, new_v = {}, {}, {}
    for k in BIG:
        n_cols = w[k].shape[-1]
        two = lambda a: a.reshape(-1, n_cols)
        dl_, m_, v_ = _adamw(two(w[k]), two(grads[k]), two(m[k]), two(v[k]), f"adamw_{k}")
        delta[k], new_m[k], new_v[k] = dl_.reshape(w[k].shape), m_.reshape(w[k].shape), v_.reshape(w[k].shape)
    pk = lambda tree: _pack([tree[k] for k in small_names], 256)[0]
    small_shapes = [w[k].shape for k in small_names]
    outs = _adamw(pk(w), pk(grads), pk(m), pk(v), "adamw_small")
    for tree, buf in zip((delta, new_m, new_v), outs):
        tree.update(zip(small_names, _unpack(buf, small_shapes)))

    grad_x = dx.reshape(nb, seq, d)
    return (loss, grad_x, *[grads[k] for k in WEIGHTS], *[delta[k] for k in WEIGHTS], *[new_m[k] for k in WEIGHTS],
            *[new_v[k] for k in WEIGHTS])


def kernel(x, mem, norm_mix, norm_xattn, norm_ffn, norm_mem, norm_final, ab_w_in, pool_w, pool_scale, ab_w_out, ssm_w_in, ssm_lam_re, ssm_lam_im, ssm_log_dt, ssm_b_re, ssm_b_im, ssm_c_re, ssm_c_im, ssm_d, ssm_w_glu, xa_w_q, xa_w_kv, xa_w_o, ffn_w_up, ffn_conv_w, ffn_conv_b, ffn_w_down, loss_target, m_norm_mix, m_norm_xattn, m_norm_ffn, m_norm_mem, m_norm_final, m_ab_w_in, m_pool_w, m_pool_scale, m_ab_w_out, m_ssm_w_in, m_ssm_lam_re, m_ssm_lam_im, m_ssm_log_dt, m_ssm_b_re, m_ssm_b_im, m_ssm_c_re, m_ssm_c_im, m_ssm_d, m_ssm_w_glu, m_xa_w_q, m_xa_w_kv, m_xa_w_o, m_ffn_w_up, m_ffn_conv_w, m_ffn_conv_b, m_ffn_w_down, v_norm_mix, v_norm_xattn, v_norm_ffn, v_norm_mem, v_norm_final, v_ab_w_in, v_pool_w, v_pool_scale, v_ab_w_out, v_ssm_w_in, v_ssm_lam_re, v_ssm_lam_im, v_ssm_log_dt, v_ssm_b_re, v_ssm_b_im, v_ssm_c_re, v_ssm_c_im, v_ssm_d, v_ssm_w_glu, v_xa_w_q, v_xa_w_kv, v_xa_w_o, v_ffn_w_up, v_ffn_conv_w, v_ffn_conv_b, v_ffn_w_down):
    args = dict(locals())
    w = {k: args[k] for k in WEIGHTS}
    m = {k: args["m_" + k] for k in WEIGHTS}
    v = {k: args["v_" + k] for k in WEIGHTS}
    return _step(x, mem, loss_target, w, m, v)
```

```python
import math

import jax
import jax.numpy as jnp
from jax import lax
from jax.experimental import pallas as pl
from jax.experimental.pallas import tpu as pltpu

f32 = jnp.float32
bf16 = jnp.bfloat16
SDS = jax.ShapeDtypeStruct
MESH = pl.DeviceIdType.MESH
ANY = pl.BlockSpec(memory_space=pl.ANY)

SB_HEAD_DIM = 64
POOL_WINDOWS = (2, 4, 8, 16)
POOL_GROUP = 128
XA_HEADS = 4
SSM_GROUPS = 64
SSM_GROUP = 16
SSM_STATE = 64
EPS = 1e-6
ADAM_LR, ADAM_B1, ADAM_B2, ADAM_EPS, ADAM_WD, ADAM_STEP = 0.001, 0.9, 0.999, 1e-08, 0.01, 10

LANES = 128
SUBLANES = 8
N_CHIPS = 4
VMEM_LIMIT = 56 * 1024 * 1024

NN = ((1,), (0,))
NT = ((1,), (1,))
TN = ((0,), (0,))


def _dot(a, b, dims):
    return lax.dot_general(a, b, (dims, ((), ())), preferred_element_type=f32)


def _params(n_grid):
    return pltpu.CompilerParams(dimension_semantics=("arbitrary",) * n_grid, vmem_limit_bytes=VMEM_LIMIT)


def _sum8(x):
    r, n = x.shape
    return jnp.sum(x.reshape(r // SUBLANES, SUBLANES, n), axis=0)


def _split_bf16(x):
    hi = x.astype(bf16)
    lo = (x - hi.astype(f32)).astype(bf16)
    return hi, lo


def _sigmoid(x):
    return 1.0 / (1.0 + jnp.exp(-x))


MM_BM = (1024, 1408, 512, 256, 128)
MM_BN = (1536, 1408, 1024, 512, 256, 128)
MM_BK = (2816, 2048, 1024, 512)


def _divisor(n, cands):
    return next((c for c in cands if n % c == 0), n)


def _mm(a, b, *, mode, name, out_dtype, bm=None, bn=None, bk=None, a_l=None, b_l=None, b_n0=0, n=None,
        res=None, out_l=None, out_layers=None, out_prev=None, norm=None):
    dims = {"nn": NN, "nt": NT, "tn": TN}[mode]
    a2, b2 = a.shape[-2:], b.shape[-2:]
    if mode == "nn":
        (m, k), nfull = a2, b2[1]
    elif mode == "nt":
        (m, k), nfull = a2, b2[0]
    else:
        (k, m), nfull = a2, b2[1]
    n = nfull if n is None else n
    if bm is None and norm is not None:
        bm = 512
    bm = _divisor(m, MM_BM) if bm is None else min(bm, m)
    bn = _divisor(n, MM_BN) if bn is None else min(bn, n)
    if bk is None:
        bk = _divisor(k, (1024, 512)) if mode == "tn" else (k if k <= MM_BK[0] else _divisor(k, MM_BK))
    bk = min(bk, k)
    assert m % bm == 0 and n % bn == 0 and k % bk == 0 and b_n0 % bn == 0, (name, m, n, k, bm, bn, bk)
    nk, n0b = k // bk, b_n0 // bn
    a_bytes, b_bytes = m * k * a.dtype.itemsize, k * n * b.dtype.itemsize
    rows_outer = a_bytes + b_bytes * (m // bm) <= b_bytes + a_bytes * (n // bn)

    def with_layer(layer, blk, idx_fn):
        def idx(g0, g1, kk):
            i, j = (g0, g1) if rows_outer else (g1, g0)
            return idx_fn(i, j, kk) if layer is None else (layer,) + idx_fn(i, j, kk)
        return pl.BlockSpec(blk if layer is None else (None,) + blk, idx)

    if mode == "tn":
        a_spec = with_layer(a_l, (bk, bm), lambda i, j, kk: (kk, i))
    else:
        a_spec = with_layer(a_l, (bm, bk), lambda i, j, kk: (i, kk))
    if mode == "nt":
        b_spec = with_layer(b_l, (bn, bk), lambda i, j, kk: (j, kk))
    else:
        b_spec = with_layer(b_l, (bk, bn), lambda i, j, kk: (kk, j + n0b))
    o_spec = with_layer(out_l, (bm, bn), lambda i, j, kk: (i, j))
    ins, in_specs = [a, b], [a_spec, b_spec]
    row_blk = with_layer(None, (bm, bn), lambda i, j, kk: (i, j))
    if res is not None:
        ins.append(res)
        in_specs.append(row_blk)
    n_norm_in = 0
    if norm is not None:
        assert bn == n and out_l is None and out_prev is None, name
        extra = list(norm[1:])
        n_norm_in = len(extra)
        ins += extra
        in_specs += [row_blk] * (n_norm_in - 1) + [pl.BlockSpec((1, n), lambda g0, g1, kk: (0, 0))]
    aliases = {}
    if out_prev is not None:
        aliases = {len(ins): 0}
        ins.append(out_prev)
        in_specs.append(ANY)
    has_res, has_prev = res is not None, out_prev is not None

    def body(*refs):
        a_ref, b_ref = refs[0], refs[1]
        res_ref = refs[2] if has_res else None
        norm_refs = refs[2 + has_res:2 + has_res + n_norm_in]
        o_ref = refs[2 + has_res + n_norm_in + has_prev]
        row_block = pl.program_id(0 if rows_outer else 1)
        part = _dot(a_ref[...].astype(bf16), b_ref[...].astype(bf16), dims)

        def finish(r):
            if has_res:
                r = r + res_ref[...]
            if norm is None:
                o_ref[...] = r.astype(o_ref.dtype)
            elif norm[0] == "fwd":
                h_ref = refs[3 + has_res + n_norm_in + has_prev]
                o_ref[...] = r
                rs = lax.rsqrt(jnp.mean(r * r, axis=1, keepdims=True) + EPS)
                h_ref[...] = (r * rs * norm_refs[0][...]).astype(bf16)
            else:
                x_ref, dres_ref, g_ref = norm_refs
                dg_ref = refs[3 + has_res + n_norm_in + has_prev]
                xv = x_ref[...]
                rs = lax.rsqrt(jnp.mean(xv * xv, axis=1, keepdims=True) + EPS)
                xh = xv * rs
                dxh = r * g_ref[...]
                o_ref[...] = dres_ref[...] + rs * (dxh - xh * jnp.mean(dxh * xh, axis=1, keepdims=True))
                dg = _sum8(r * xh)

                @pl.when(row_block == 0)
                def _():
                    dg_ref[...] = dg

                @pl.when(row_block > 0)
                def _():
                    dg_ref[...] += dg

        if nk == 1:
            finish(part)
        else:
            acc_ref = refs[-1]
            kk = pl.program_id(2)

            @pl.when(kk == 0)
            def _():
                acc_ref[...] = part

            @pl.when(kk > 0)
            def _():
                acc_ref[...] += part

            @pl.when(kk == nk - 1)
            def _():
                finish(acc_ref[...])

    out_shape = SDS((m, n) if out_l is None else (out_layers, m, n), out_dtype)
    grid = (m // bm, n // bn, nk) if rows_outer else (n // bn, m // bm, nk)
    if norm is not None:
        if norm[0] == "fwd":
            out_shape, o_spec = (out_shape, SDS((m, n), bf16)), (o_spec, row_blk)
        else:
            out_shape = (out_shape, SDS((SUBLANES, n), f32))
            o_spec = (o_spec, pl.BlockSpec((SUBLANES, n), lambda g0, g1, kk: (0, 0)))
    return pl.pallas_call(
        body, out_shape=out_shape, grid=grid, in_specs=in_specs, out_specs=o_spec,
        scratch_shapes=[] if nk == 1 else [pltpu.VMEM((bm, bn), f32)],
        input_output_aliases=aliases, name=name, compiler_params=_params(3))(*ins)


def _rowwise(fn, row_ins, full_ins, row_outs, acc_outs, *, name, br=512):
    t = row_ins[0].shape[0]
    br = next(b for b in (br, 256, 128, 64, 32, 16, 8, t) if b <= t and t % b == 0)
    nr, nf, no = len(row_ins), len(full_ins), len(row_outs)

    def body(*refs):
        rv = [r[...] for r in refs[:nr]]
        fv = [r[...] for r in refs[nr:nr + nf]]
        o_refs = refs[nr + nf:nr + nf + no]
        a_refs = refs[nr + nf + no:]
        outs, accs = fn(rv, fv)
        for o_ref, v in zip(o_refs, outs):
            o_ref[...] = v.astype(o_ref.dtype)
        if a_refs:
            i = pl.program_id(0)

            @pl.when(i == 0)
            def _():
                for a_ref, v in zip(a_refs, accs):
                    a_ref[...] = v

            @pl.when(i > 0)
            def _():
                for a_ref, v in zip(a_refs, accs):
                    a_ref[...] += v

    in_specs = [pl.BlockSpec((br, x.shape[1]), lambda i: (i, 0)) for x in row_ins]
    in_specs += [pl.BlockSpec(x.shape, lambda i, nd=x.ndim: (0,) * nd) for x in full_ins]
    out_specs = [pl.BlockSpec((br, s.shape[1]), lambda i: (i, 0)) for s in row_outs]
    out_specs += [pl.BlockSpec(s.shape, lambda i: (0, 0)) for s in acc_outs]
    res = pl.pallas_call(body, out_shape=tuple(row_outs) + tuple(acc_outs), grid=(t // br,), in_specs=in_specs,
                         out_specs=tuple(out_specs), name=name, compiler_params=_params(1))(*row_ins, *full_ins)
    return res


def _norm_fwd(x, g, name):
    def fn(rv, fv):
        (xv,), (gv,) = rv, fv
        r = lax.rsqrt(jnp.mean(xv * xv, axis=1, keepdims=True) + EPS)
        return [xv * r * gv], []
    return _rowwise(fn, [x], [g], [SDS(x.shape, bf16)], [], name=name)[0]


def _norm_bwd_gain_only(dh, x, name):
    d = x.shape[1]

    def fn(rv, fv):
        dhv, xv = rv
        r = lax.rsqrt(jnp.mean(xv * xv, axis=1, keepdims=True) + EPS)
        return [], [_sum8(dhv * xv * r)]
    return _rowwise(fn, [dh, x], [], [], [SDS((SUBLANES, d), f32)], name=name)[0]


def _loss_head(x, target, g, name):
    d = x.shape[1]

    def fn(rv, fv):
        (xv, tv), (gv,) = rv, fv
        r = lax.rsqrt(jnp.mean(xv * xv, axis=1, keepdims=True) + EPS)
        xh = xv * r
        err = xh * gv - tv
        dy = err * (1.0 / d)
        dxh = dy * gv
        dx = r * (dxh - xh * jnp.mean(dxh * xh, axis=1, keepdims=True))
        return [dx], [_sum8(dy * xh), _sum8(err * err)]
    return _rowwise(fn, [x, target], [g], [SDS(x.shape, f32)], [SDS((SUBLANES, d), f32), SDS((SUBLANES, d), f32)], name=name)


_GELU_C = math.sqrt(2.0 / math.pi)


def _gelu_fwd(y, name):
    def fn(rv, fv):
        (v,) = rv
        t = jnp.tanh(_GELU_C * (v + 0.044715 * v * v * v))
        return [0.5 * v * (1.0 + t)], []
    return _rowwise(fn, [y], [], [SDS(y.shape, bf16)], [], name=name)[0]


def _gelu_bwd(dg, y, name):
    def fn(rv, fv):
        dgv, v = rv
        t = jnp.tanh(_GELU_C * (v + 0.044715 * v * v * v))
        dt = (1.0 - t * t) * _GELU_C * (1.0 + 3.0 * 0.044715 * v * v)
        return [dgv * (0.5 * (1.0 + t) + 0.5 * v * dt)], []
    return _rowwise(fn, [dg, y], [], [SDS(y.shape, f32)], [], name=name)[0]


def _glu_fwd(glu, x, g, name):
    d = x.shape[1]

    def fn(rv, fv):
        (gl, xv), (gv,) = rv, fv
        y = xv + gl[:, :d] * _sigmoid(gl[:, d:])
        r = lax.rsqrt(jnp.mean(y * y, axis=1, keepdims=True) + EPS)
        return [y, y * r * gv], []
    return _rowwise(fn, [glu, x], [g], [SDS(x.shape, f32), SDS(x.shape, bf16)], [], name=name)


def _glu_bwd(dx, glu, name):
    d = dx.shape[1]

    def fn(rv, fv):
        dxv, gl = rv
        sg = _sigmoid(gl[:, d:])
        return [jnp.concatenate([dxv * sg, dxv * gl[:, :d] * sg * (1.0 - sg)], axis=1)], []
    return _rowwise(fn, [dx, glu], [], [SDS(glu.shape, bf16)], [], name=name)[0]


def _adamw(w, g, m, v, name):
    c1 = 1.0 - ADAM_B1 ** ADAM_STEP
    c2 = 1.0 - ADAM_B2 ** ADAM_STEP

    def fn(rv, fv):
        wv, gv, mv, vv = rv
        m2 = ADAM_B1 * mv + (1.0 - ADAM_B1) * gv
        v2 = ADAM_B2 * vv + (1.0 - ADAM_B2) * (gv * gv)
        delta = -ADAM_LR * ((m2 / c1) / (jnp.sqrt(v2 / c2) + ADAM_EPS) + ADAM_WD * wv)
        return [delta, m2, v2], []
    s = SDS(w.shape, f32)
    return _rowwise(fn, [w, g, m, v], [], [s, s, s], [], name=name, br=256)


SB_TQ = 128
SB_KB = 3
SB_DEAD = -110.0


def _sb_heads(q, t):
    lane = lax.broadcasted_iota(jnp.int32, (t, LANES), 1)
    masks = [(lane >= hh * SB_HEAD_DIM) & (lane < (hh + 1) * SB_HEAD_DIM) for hh in range(LANES // SB_HEAD_DIM)]
    return [(m, q * jnp.where(m, 1.0, 0.0).astype(bf16)) for m in masks]


def _sb_key_minus_query(t):
    return lax.broadcasted_iota(jnp.int32, (t, t), 1) - lax.broadcasted_iota(jnp.int32, (t, t), 0)


def _tri(t, op):
    row = lax.broadcasted_iota(jnp.int32, (t, t), 0)
    col = lax.broadcasted_iota(jnp.int32, (t, t), 1)
    return jnp.where(op(row, col), 1.0, 0.0).astype(bf16)


def _sb_block(i, g, kk, kbn, t, kmq, k_ref, v_ref):
    j = i - g * kbn - kk
    off = pl.multiple_of(jnp.maximum(j, 0) * t, t)
    limit = jnp.where(j >= 0, (i - j) * t, -2 * t)
    return off, k_ref[pl.ds(off, t), :], v_ref[pl.ds(off, t), :], kmq < limit


def _sb_fwd(qkv, seq, name, comm=None):
    t_all, w3 = qkv.shape
    w = w3 // 3
    hp, tq = w // LANES, SB_TQ
    nb, nq = t_all // seq, seq // tq
    kbn = min(SB_KB, nq)

    def body(q_ref, k_ref, v_ref, o_ref, lt_ref, first_ref):
        i = pl.program_id(2)
        heads = _sb_heads(q_ref[...], tq)
        kmq = _sb_key_minus_query(tq)
        u_after = _tri(tq, lambda r, c: r > c)
        n_it = (i + kbn) // kbn

        def alive(state):
            return (state[0] < n_it) & (state[1] > SB_DEAD)

        def step(state):
            it, carry = state[0], list(state[2:])
            blocks = [_sb_block(i, it, kk, kbn, tq, kmq, k_ref, v_ref)[1:] for kk in range(kbn)]
            chains = [(hh, qh, kb, vb, valid) for kb, vb, valid in blocks for hh, (_, qh) in enumerate(heads)]
            zs = [_dot(qh, kb, NT) for _, qh, kb, _, _ in chains]
            lbs, his, los, sums = [], [], [], []
            for z, (_, _, _, _, valid) in zip(zs, chains):
                z = z * (SB_HEAD_DIM ** -0.5)
                sp = jnp.log(1.0 + jnp.exp(-jnp.abs(z)))
                lb = jnp.minimum(z, 0.0) - sp
                lk = jnp.where(valid, lb - z, 0.0)
                hi, lo = _split_bf16(lk)
                lbs.append(lb), his.append(hi), los.append(lo), sums.append(jnp.sum(lk, axis=1, keepdims=True))
            afts = [_dot(hi, u_after, NN) + _dot(lo, u_after, NN) for hi, lo in zip(his, los)]
            wgts = []
            for (hh, _, _, _, valid), lb, aft, sm in zip(chains, lbs, afts, sums):
                wgts.append(jnp.where(valid, jnp.exp(lb + (carry[2 * hh] + aft)), 0.0).astype(bf16))
                carry[2 * hh] = carry[2 * hh] + sm
            for (hh, _, _, vb, _), wgt in zip(chains, wgts):
                carry[2 * hh + 1] = carry[2 * hh + 1] + _dot(wgt, vb, NN)
            top = jnp.max(carry[0])
            for hh in range(1, len(heads)):
                top = jnp.maximum(top, jnp.max(carry[2 * hh]))
            return (it + 1, top, *carry)

        init = (jnp.int32(0), jnp.float32(0.0)) + (jnp.zeros((tq, 1), f32), jnp.zeros((tq, LANES), f32)) * len(heads)
        fin = lax.while_loop(alive, step, init)
        out = jnp.zeros((tq, LANES), f32)
        ltot = jnp.zeros((tq, LANES), f32)
        for hh, (m, _) in enumerate(heads):
            out = out + jnp.where(m, fin[2 * hh + 3], 0.0)
            ltot = ltot + jnp.where(m, fin[2 * hh + 2], 0.0)
        o_ref[...] = out
        lt_ref[...] = ltot
        first_ref[...] = jnp.zeros((SUBLANES, LANES), f32) + fin[0].astype(f32)

    row_blk = pl.BlockSpec((tq, LANES), lambda b, p, i: (b * nq + i, p))
    (mix, ltot, first), extra = _call(
        body, ins=[qkv, qkv, qkv], out_shape=[SDS((t_all, 2 * w), f32), SDS((t_all, w), f32), SDS((nb * nq * SUBLANES, w), f32)],
        grid=(nb, hp, nq),
        in_specs=[row_blk, pl.BlockSpec((seq, LANES), lambda b, p, i: (b, hp + p)),
                  pl.BlockSpec((seq, LANES), lambda b, p, i: (b, 2 * hp + p))],
        out_specs=[row_blk, row_blk, pl.BlockSpec((SUBLANES, LANES), lambda b, p, i: (b * nq + i, p))],
        scratch_shapes=[], name=name, comm=comm)
    return mix, ltot, first, extra


def _sb_bwd(qkv, ltot, first, dmix, seq, name, comm=None):
    t_all, w3 = qkv.shape
    w = w3 // 3
    hp, tq = w // LANES, SB_TQ
    nb, nq = t_all // seq, seq // tq
    kbn = min(SB_KB, nq)

    def body(q_ref, k_ref, v_ref, lt_ref, first_ref, do_ref, dq_ref, dk_ref, dv_ref, dk_acc, dv_acc):
        i = pl.program_id(2)

        @pl.when(i == 0)
        def _():
            dk_acc[...] = jnp.zeros_like(dk_acc)
            dv_acc[...] = jnp.zeros_like(dv_acc)

        heads = _sb_heads(q_ref[...], tq)
        do = do_ref[...]
        ltv = lt_ref[...]
        dos = [jnp.where(m, do, 0.0).astype(bf16) for m, _ in heads]
        lts = [jnp.sum(jnp.where(m, ltv, 0.0), axis=1, keepdims=True) * (1.0 / SB_HEAD_DIM) for m, _ in heads]
        kmq = _sb_key_minus_query(tq)
        u_incl = _tri(tq, lambda r, c: r <= c)
        u_excl = _tri(tq, lambda r, c: r < c)
        n_it = (i + kbn) // kbn

        walked = jnp.clip(jnp.max(first_ref[...]).astype(jnp.int32), 1, n_it)

        def step(s, carry):
            carry = list(carry)
            blocks = [_sb_block(i, walked - 1 - s, kk, kbn, tq, kmq, k_ref, v_ref) for kk in reversed(range(kbn))]
            chains = [(hh, qh, kb, vb, valid) for _, kb, vb, valid in blocks for hh, (_, qh) in enumerate(heads)]
            zs = [_dot(qh, kb, NT) for _, qh, kb, _, _ in chains]
            dws = [_dot(dos[hh], vb, NT) for hh, _, _, vb, _ in chains]
            lbs, lkrs, his, los, sums = [], [], [], [], []
            for z, (_, _, _, _, valid) in zip(zs, chains):
                z = z * (SB_HEAD_DIM ** -0.5)
                sp = jnp.log(1.0 + jnp.exp(-jnp.abs(z)))
                lb = jnp.minimum(z, 0.0) - sp
                lk_raw = lb - z
                lk = jnp.where(valid, lk_raw, 0.0)
                hi, lo = _split_bf16(lk)
                lbs.append(lb), lkrs.append(lk_raw), his.append(hi), los.append(lo)
                sums.append(jnp.sum(lk, axis=1, keepdims=True))
            pins = [_dot(hi, u_incl, NN) + _dot(lo, u_incl, NN) for hi, lo in zip(his, los)]
            wbs, gs, ghis, glos, gpres = [], [], [], [], []
            for (hh, _, _, _, valid), lb, pin, sm, dw in zip(chains, lbs, pins, sums, dws):
                wgt = jnp.where(valid, jnp.exp(lb + (lts[hh] - (carry[3 * hh] + pin))), 0.0)
                carry[3 * hh] = carry[3 * hh] + sm
                g = dw * wgt
                hi, lo = _split_bf16(g)
                wbs.append(wgt.astype(bf16)), gs.append(g), ghis.append(hi), glos.append(lo)
                gpres.append(carry[3 * hh + 1])
                carry[3 * hh + 1] = carry[3 * hh + 1] + jnp.sum(g, axis=1, keepdims=True)
            gins = [_dot(hi, u_excl, NN) + _dot(lo, u_excl, NN) for hi, lo in zip(ghis, glos)]
            dzbs = []
            for (_, _, _, _, valid), lb, lk_raw, g, gpre, gin in zip(chains, lbs, lkrs, gs, gpres, gins):
                dz = jnp.where(valid, g * jnp.exp(lk_raw) - (gpre + gin) * jnp.exp(lb), 0.0) * (SB_HEAD_DIM ** -0.5)
                dzbs.append(dz.astype(bf16))
            for (hh, _, kb, _, _), dzb in zip(chains, dzbs):
                carry[3 * hh + 2] = carry[3 * hh + 2] + _dot(dzb, kb, NN)
            nh = len(heads)
            for bi, (off, _, _, _) in enumerate(blocks):
                dk_j = jnp.zeros((tq, LANES), f32)
                dv_j = jnp.zeros((tq, LANES), f32)
                for hh, (_, qh) in enumerate(heads):
                    dk_j = dk_j + _dot(dzbs[bi * nh + hh], qh, TN)
                    dv_j = dv_j + _dot(wbs[bi * nh + hh], dos[hh], TN)
                dk_acc[pl.ds(off, tq), :] += dk_j
                dv_acc[pl.ds(off, tq), :] += dv_j
            return tuple(carry)

        zero1 = jnp.zeros((tq, 1), f32)
        fin = lax.fori_loop(0, walked, step, (zero1, zero1, jnp.zeros((tq, LANES), f32)) * len(heads))
        dq_all = jnp.zeros((tq, LANES), f32)
        for hh, (m, _) in enumerate(heads):
            dq_all = dq_all + jnp.where(m, fin[3 * hh + 2], 0.0)
        dq_ref[...] = dq_all.astype(bf16)

        @pl.when(i == nq - 1)
        def _():
            dk_ref[...] = dk_acc[...].astype(bf16)
            dv_ref[...] = dv_acc[...].astype(bf16)

    row_blk = pl.BlockSpec((tq, LANES), lambda b, p, i: (b * nq + i, p))
    seq_blk = pl.BlockSpec((seq, LANES), lambda b, p, i: (b, p))
    out = SDS((t_all, w), bf16)
    (dq, dk, dv), extra = _call(
        body, ins=[qkv, qkv, qkv, ltot, first, dmix], out_shape=[out, out, out], grid=(nb, hp, nq),
        in_specs=[row_blk,
                  pl.BlockSpec((seq, LANES), lambda b, p, i: (b, hp + p)),
                  pl.BlockSpec((seq, LANES), lambda b, p, i: (b, 2 * hp + p)),
                  row_blk, pl.BlockSpec((SUBLANES, LANES), lambda b, p, i: (b * nq + i, p)), row_blk],
        out_specs=[row_blk, seq_blk, seq_blk],
        scratch_shapes=[pltpu.VMEM((seq, LANES), f32), pltpu.VMEM((seq, LANES), f32)], name=name, comm=comm)
    return dq, dk, dv, extra


POOL_CHUNK = 256
POOL_HALO = 16


def _band(rows, cols, lo, hi):
    r = lax.broadcasted_iota(jnp.int32, (rows, cols), 0)
    c = lax.broadcasted_iota(jnp.int32, (rows, cols), 1)
    d = c - r
    return jnp.where((d >= lo) & (d < hi), 1.0, 0.0).astype(bf16)


def _pool_counts(r0, rows, win):
    t = lax.broadcasted_iota(jnp.int32, (rows, 1), 0) + r0
    return jnp.minimum(t + 1, win).astype(f32)


def _pool_fwd(u, mix, pool_w, scale, seq, name):
    t_all, w = u.shape
    ng, rc = w // POOL_GROUP, min(POOL_CHUNK, seq)

    def body(u_ref, w_ref, s_ref, mix_in, p_ref, o_ref, pad):
        del mix_in
        pad[0:POOL_HALO, :] = jnp.zeros((POOL_HALO, POOL_GROUP), f32)
        for g in range(ng):
            cols = slice(g * POOL_GROUP, (g + 1) * POOL_GROUP)
            win = POOL_WINDOWS[g]
            pad[POOL_HALO:POOL_HALO + seq, :] = u_ref[:, cols]
            band = _band(rc, rc + POOL_HALO, POOL_HALO - win + 1, POOL_HALO + 1)
            wg = w_ref[g].astype(bf16)
            for r0 in range(0, seq, rc):
                ue = pad[r0:r0 + rc + POOL_HALO, :]
                hi, lo = _split_bf16(ue)
                sm = _dot(band, hi, NN) + _dot(band, lo, NN)
                pch = sm / _pool_counts(r0, rc, win) - ue[POOL_HALO:, :]
                pb = pch.astype(bf16)
                p_ref[r0:r0 + rc, cols] = pb
                o_ref[r0:r0 + rc, cols] = _dot(pb, wg, NN) * s_ref[:, cols]

    return pl.pallas_call(
        body, out_shape=(SDS((t_all, w), bf16), SDS(mix.shape, f32)), grid=(t_all // seq,),
        in_specs=[pl.BlockSpec((seq, w), lambda b: (b, 0)), pl.BlockSpec(pool_w.shape, lambda b: (0, 0, 0)),
                  pl.BlockSpec(scale.shape, lambda b: (0, 0)), ANY],
        out_specs=(pl.BlockSpec((seq, w), lambda b: (b, 0)), pl.BlockSpec((seq, w), lambda b: (b, 1))),
        scratch_shapes=[pltpu.VMEM((seq + POOL_HALO, POOL_GROUP), f32)],
        input_output_aliases={3: 1}, name=name, compiler_params=_params(1))(u, pool_w, scale, mix)


def _pool_bwd(dmix, p, pool_w, scale, seq, name):
    t_all, w = p.shape
    ng, rc = w // POOL_GROUP, min(POOL_CHUNK, seq)

    def body(dy_ref, p_ref, w_ref, s_ref, du_ref, dw_ref, ds_ref, dpn, dpr):
        b = pl.program_id(0)

        @pl.when(b == 0)
        def _():
            dw_ref[...] = jnp.zeros_like(dw_ref)
            ds_ref[...] = jnp.zeros_like(ds_ref)

        dpn[seq:seq + POOL_HALO, :] = jnp.zeros((POOL_HALO, POOL_GROUP), f32)
        for g in range(ng):
            cols = slice(g * POOL_GROUP, (g + 1) * POOL_GROUP)
            win = POOL_WINDOWS[g]
            wg = w_ref[g].astype(bf16)
            sg = s_ref[:, cols]
            dwg = jnp.zeros((POOL_GROUP, POOL_GROUP), f32)
            dsg = jnp.zeros((SUBLANES, POOL_GROUP), f32)
            for r0 in range(0, seq, rc):
                dy = dy_ref[r0:r0 + rc, cols]
                pb = p_ref[r0:r0 + rc, cols]
                dsg = dsg + _sum8(dy * _dot(pb, wg, NN))
                dyw = (dy * sg).astype(bf16)
                dwg = dwg + _dot(pb, dyw, TN)
                dp = _dot(dyw, wg, NT)
                dpr[r0:r0 + rc, :] = dp
                dpn[r0:r0 + rc, :] = dp / _pool_counts(r0, rc, win)
            dw_ref[g] += dwg
            ds_ref[:, cols] += dsg
            band = _band(rc, rc + POOL_HALO, 0, win)
            for r0 in range(0, seq, rc):
                hi, lo = _split_bf16(dpn[r0:r0 + rc + POOL_HALO, :])
                du = _dot(band, hi, NN) + _dot(band, lo, NN) - dpr[r0:r0 + rc, :]
                du_ref[r0:r0 + rc, cols] = du.astype(bf16)

    return pl.pallas_call(
        body, out_shape=(SDS((t_all, w), bf16), SDS(pool_w.shape, f32), SDS((SUBLANES, w), f32)), grid=(t_all // seq,),
        in_specs=[pl.BlockSpec((seq, w), lambda b: (b, 1)), pl.BlockSpec((seq, w), lambda b: (b, 0)),
                  pl.BlockSpec(pool_w.shape, lambda b: (0, 0, 0)), pl.BlockSpec(scale.shape, lambda b: (0, 0))],
        out_specs=(pl.BlockSpec((seq, w), lambda b: (b, 0)), pl.BlockSpec(pool_w.shape, lambda b: (0, 0, 0)),
                   pl.BlockSpec((SUBLANES, w), lambda b: (0, 0))),
        scratch_shapes=[pltpu.VMEM((seq + POOL_HALO, POOL_GROUP), f32), pltpu.VMEM((seq, POOL_GROUP), f32)],
        name=name, compiler_params=_params(1))(dmix, p, pool_w, scale)


XA_TQ = 1024


def _xa_probs(qh, kh, dh):
    s = _dot(qh, kh, NT) * (dh ** -0.5)
    e = jnp.exp(s - jnp.max(s, axis=1, keepdims=True))
    return e / jnp.sum(e, axis=1, keepdims=True)


def _xa_fwd(q, kv, seq, name):
    t_all, d = q.shape
    nb = t_all // seq
    mem, dh, tq = kv.shape[0] // nb, d // XA_HEADS, min(XA_TQ, seq)
    nq = seq // tq

    def body(q_ref, kv_ref, o_ref):
        for h in range(XA_HEADS):
            cols = slice(h * dh, (h + 1) * dh)
            p = _xa_probs(q_ref[:, cols], kv_ref[:, cols], dh)
            o_ref[:, cols] = _dot(p.astype(bf16), kv_ref[:, d + h * dh:d + (h + 1) * dh], NN).astype(bf16)

    return pl.pallas_call(
        body, out_shape=SDS((t_all, d), bf16), grid=(nb, nq),
        in_specs=[pl.BlockSpec((tq, d), lambda b, i: (b * nq + i, 0)), pl.BlockSpec((mem, 2 * d), lambda b, i: (b, 0))],
        out_specs=pl.BlockSpec((tq, d), lambda b, i: (b * nq + i, 0)), name=name, compiler_params=_params(2))(q, kv)


def _xa_bwd(q, kv, do, seq, name):
    t_all, d = q.shape
    nb = t_all // seq
    mem, dh, tq = kv.shape[0] // nb, d // XA_HEADS, min(XA_TQ, seq)
    nq = seq // tq

    def body(q_ref, kv_ref, do_ref, dq_ref, dkv_ref):
        i = pl.program_id(1)

        @pl.when(i == 0)
        def _():
            dkv_ref[...] = jnp.zeros_like(dkv_ref)

        for h in range(XA_HEADS):
            cols = slice(h * dh, (h + 1) * dh)
            vcols = slice(d + h * dh, d + (h + 1) * dh)
            qh, kh, doh = q_ref[:, cols], kv_ref[:, cols], do_ref[:, cols]
            p = _xa_probs(qh, kh, dh)
            dkv_ref[:, vcols] += _dot(p.astype(bf16), doh, TN)
            dp = _dot(doh, kv_ref[:, vcols], NT)
            ds = (p * (dp - jnp.sum(dp * p, axis=1, keepdims=True)) * (dh ** -0.5)).astype(bf16)
            dq_ref[:, cols] = _dot(ds, kh, NN).astype(bf16)
            dkv_ref[:, cols] += _dot(ds, qh, TN)

    row = pl.BlockSpec((tq, d), lambda b, i: (b * nq + i, 0))
    kvs = pl.BlockSpec((mem, 2 * d), lambda b, i: (b, 0))
    return pl.pallas_call(body, out_shape=(SDS((t_all, d), bf16), SDS(kv.shape, f32)), grid=(nb, nq),
                          in_specs=[row, kvs, row], out_specs=(row, kvs), name=name, compiler_params=_params(2))(q, kv, do)


FFN_BR = 256
FFN_CHUNK = 256


def _conv3(ext, w_ref, b, cols, lo, rows):
    return (b + w_ref[2:3, cols] * ext[lo:lo + rows, :] + w_ref[1:2, cols] * ext[lo - 1:lo - 1 + rows, :]
            + w_ref[0:1, cols] * ext[lo - 2:lo - 2 + rows, :])


FFN_HALO = 16


def _ffn_gate_fwd(up, cw, cb, seq, name):
    t_all, f2 = up.shape
    ff, br, ch, hl = f2 // 2, min(FFN_BR, seq), FFN_CHUNK, FFN_HALO
    per_seq, hb = seq // br, br // hl

    def body(up_ref, halo_ref, cw_ref, cb_ref, o_ref, cv_ref, ev, eg):
        i = pl.program_id(0)
        keep = jnp.where(i % per_seq == 0, 0.0, 1.0)
        for c0 in range(0, ff, ch):
            convs = []
            for ext, off in ((ev, c0), (eg, ff + c0)):
                cols = slice(off, off + ch)
                ext[0:hl, :] = halo_ref[:, cols].astype(f32) * keep
                ext[hl:hl + br, :] = up_ref[:, cols].astype(f32)
                conv = _conv3(ext, cw_ref, cb_ref[:, cols], cols, hl, br)
                cv_ref[:, cols] = conv.astype(bf16)
                convs.append(conv)
            val, gate = convs
            o_ref[:, c0:c0 + ch] = (gate * _sigmoid(gate) * val).astype(bf16)

    return pl.pallas_call(
        body, out_shape=(SDS((t_all, ff), bf16), SDS((t_all, f2), bf16)), grid=(t_all // br,),
        in_specs=[pl.BlockSpec((br, f2), lambda i: (i, 0)),
                  pl.BlockSpec((hl, f2), lambda i: (jnp.maximum(i * hb - 1, 0), 0)),
                  pl.BlockSpec(cw.shape, lambda i: (0, 0)), pl.BlockSpec(cb.shape, lambda i: (0, 0))],
        out_specs=(pl.BlockSpec((br, ff), lambda i: (i, 0)), pl.BlockSpec((br, f2), lambda i: (i, 0))),
        scratch_shapes=[pltpu.VMEM((br + hl, ch), f32), pltpu.VMEM((br + hl, ch), f32)],
        name=name, compiler_params=_params(1))(up, up, cw, cb)


def _ffn_gate_bwd(dact, up, cv, cw, seq, name):
    t_all, f2 = up.shape
    ff, br, ch, hl = f2 // 2, min(FFN_BR, seq), FFN_CHUNK, FFN_HALO
    per_seq, hb, last = seq // br, br // hl, t_all // hl - 1
    ext_rows = br + SUBLANES

    def body(da_ref, dan_ref, cv_ref, cvn_ref, up_ref, upp_ref, cw_ref, du_ref, dcw_ref, dcb_ref, ext, e1, e2, e3, dcv, dcg):
        i = pl.program_id(0)

        @pl.when(i == 0)
        def _():
            dcw_ref[...] = jnp.zeros_like(dcw_ref)
            dcb_ref[...] = jnp.zeros_like(dcb_ref)

        keep_prev = jnp.where(i % per_seq == 0, 0.0, 1.0)
        keep_next = jnp.where((i + 1) % per_seq == 0, 0.0, 1.0)

        def with_next(scr, blk_ref, nxt_ref, cols, scale):
            scr[0:br, :] = blk_ref[:, cols].astype(f32)
            scr[br:br + hl, :] = nxt_ref[:, cols].astype(f32) * scale
            return scr[0:ext_rows, :]

        for c0 in range(0, ff, ch):
            da = with_next(e1, da_ref, dan_ref, slice(c0, c0 + ch), keep_next)
            val = with_next(e2, cv_ref, cvn_ref, slice(c0, c0 + ch), 1.0)
            gate = with_next(e3, cv_ref, cvn_ref, slice(ff + c0, ff + c0 + ch), 1.0)
            sg = _sigmoid(gate)
            dcv[...] = da * gate * sg
            dcg[...] = da * val * sg * (1.0 + gate * (1.0 - sg))
            for dc, off in ((dcv, c0), (dcg, ff + c0)):
                cols = slice(off, off + ch)
                du = (cw_ref[2:3, cols] * dc[0:br, :] + cw_ref[1:2, cols] * dc[1:br + 1, :]
                      + cw_ref[0:1, cols] * dc[2:br + 2, :])
                du_ref[:, cols] = du.astype(bf16)
                d0 = dc[0:br, :]
                dcb_ref[:, cols] += _sum8(d0)
                ext[0:hl, :] = upp_ref[:, cols].astype(f32) * keep_prev
                ext[hl:hl + br, :] = up_ref[:, cols].astype(f32)
                for tap in range(3):
                    lo = hl - (2 - tap)
                    dcw_ref[tap, :, cols] += _sum8(d0 * ext[lo:lo + br, :])

    blk = lambda n: pl.BlockSpec((br, n), lambda i: (i, 0))
    prev = lambda n: pl.BlockSpec((hl, n), lambda i: (jnp.maximum(i * hb - 1, 0), 0))
    nxt = lambda n: pl.BlockSpec((hl, n), lambda i: (jnp.minimum((i + 1) * hb, last), 0))
    return pl.pallas_call(
        body, out_shape=(SDS((t_all, f2), bf16), SDS((3, SUBLANES, f2), f32), SDS((SUBLANES, f2), f32)), grid=(t_all // br,),
        in_specs=[blk(ff), nxt(ff), blk(f2), nxt(f2), blk(f2), prev(f2), pl.BlockSpec(cw.shape, lambda i: (0, 0))],
        out_specs=(blk(f2), pl.BlockSpec((3, SUBLANES, f2), lambda i: (0, 0, 0)), pl.BlockSpec((SUBLANES, f2), lambda i: (0, 0))),
        scratch_shapes=[pltpu.VMEM((br + hl, ch), f32)] * 4 + [pltpu.VMEM((ext_rows, ch), f32)] * 2,
        name=name, compiler_params=_params(1))(dact, dact, cv, cv, up, up, cw)


SSM_GB = 8
SSM_PLANES = 8
SSM_ROWS = 256
SSM_UNROLL = 8


def _ssm_pitch(seq):
    p = seq + SUBLANES
    assert (p // SUBLANES) % 2 == 1
    return p


def _rows(base, rc):
    return pl.ds(pl.multiple_of(base + rc * SSM_ROWS, SUBLANES), SSM_ROWS)


def _ssm_project_in(u_ref, b_ref, planes, e, seq, pitch):
    def chunk(rc, _):
        uc = u_ref[_rows(e * seq, rc), :].astype(bf16)
        for j in range(SSM_PLANES):
            planes[_rows(j * pitch, rc), :] = _dot(uc, b_ref[:, j * LANES:(j + 1) * LANES], NN)
        return 0
    lax.fori_loop(0, seq // SSM_ROWS, chunk, 0)


def _ssm_rows(planes, rc, pitch):
    return jnp.concatenate([planes[_rows(j * pitch, rc), :].astype(bf16) for j in range(SSM_PLANES)], axis=1)


def _ssm_scan(planes_list, l1, l2, seq, pitch, reverse=False):
    def step(s, hs):
        hs = list(hs)
        for k in range(SSM_UNROLL):
            t = s * SSM_UNROLL + k
            t = seq - 1 - t if reverse else t
            for e, planes in enumerate(planes_list):
                hs[e] = hs[e] * l1 + pltpu.roll(hs[e], 4, 0) * l2 + planes[pl.ds(t, SUBLANES, stride=pitch), :]
                planes[pl.ds(t, SUBLANES, stride=pitch), :] = hs[e]
        return tuple(hs)
    zero = jnp.zeros((SUBLANES, LANES), f32)
    lax.fori_loop(0, seq // SSM_UNROLL, step, tuple(zero for _ in planes_list))


def _ssm_fwd(u, b_big, c_big, lslab, dskip, seq, name, comm=None):
    t_all, w = u.shape
    nb, gw, pitch = t_all // seq, SSM_GB * SSM_GROUP, _ssm_pitch(seq)
    assert gw == LANES

    def body(u_ref, b_ref, c_ref, l_ref, d_ref, y_ref, *planes):
        l1, l2 = l_ref[0:SUBLANES, :], l_ref[SUBLANES:2 * SUBLANES, :]
        for e in range(nb):
            _ssm_project_in(u_ref, b_ref, planes[e], e, seq, pitch)
        _ssm_scan(planes, l1, l2, seq, pitch)
        for e in range(nb):
            def chunk(rc, _, e=e):
                rows = _rows(e * seq, rc)
                y_ref[rows, :] = _dot(_ssm_rows(planes[e], rc, pitch), c_ref[...], NN) + d_ref[...] * u_ref[rows, :]
                return 0
            lax.fori_loop(0, seq // SSM_ROWS, chunk, 0)

    (y,), extra = _call(
        body, ins=[u, b_big, c_big, lslab, dskip], out_shape=[SDS((t_all, w), f32)], grid=(w // gw,),
        in_specs=[pl.BlockSpec((t_all, gw), lambda k: (0, k)), pl.BlockSpec((None,) + b_big.shape[1:], lambda k: (k, 0, 0)),
                  pl.BlockSpec((None,) + c_big.shape[1:], lambda k: (k, 0, 0)),
                  pl.BlockSpec((None,) + lslab.shape[1:], lambda k: (k, 0, 0)), pl.BlockSpec((1, gw), lambda k: (0, k))],
        out_specs=[pl.BlockSpec((t_all, gw), lambda k: (0, k))],
        scratch_shapes=[pltpu.VMEM((SSM_PLANES * pitch, LANES), f32) for _ in range(nb)], name=name, comm=comm)
    return y, extra


def _ssm_bwd(u, dy, b_big, c_big, lslab, dskip, seq, name, comm=None):
    t_all, w = u.shape
    nb, gw, pitch = t_all // seq, SSM_GB * SSM_GROUP, _ssm_pitch(seq)

    def body(u_ref, dy_ref, b_ref, c_ref, l_ref, d_ref, du_ref, db_ref, dc_ref, dl_ref, dd_ref, *planes):
        hp, ap = planes[:nb], planes[nb:]
        l1, l2 = l_ref[0:SUBLANES, :], l_ref[SUBLANES:2 * SUBLANES, :]
        for e in range(nb):
            _ssm_project_in(u_ref, b_ref, hp[e], e, seq, pitch)
        _ssm_scan(hp, l1, l2, seq, pitch)
        dd_ref[...] = jnp.zeros_like(dd_ref)
        dc_ref[...] = jnp.zeros_like(dc_ref)
        db_ref[...] = jnp.zeros_like(db_ref)
        for e in range(nb):
            def chunk(rc, _, e=e):
                rows = _rows(e * seq, rc)
                dyc = dy_ref[rows, :]
                dyb = dyc.astype(bf16)
                for j in range(SSM_PLANES):
                    ap[e][_rows(j * pitch, rc), :] = _dot(dyb, c_ref[j * LANES:(j + 1) * LANES, :], NT)
                dd_ref[...] += _sum8(dyc * u_ref[rows, :])
                dc_ref[...] += _dot(_ssm_rows(hp[e], rc, pitch), dyb, TN)
                return 0
            lax.fori_loop(0, seq // SSM_ROWS, chunk, 0)

        def step(s, carry):
            carry = [list(c) for c in carry]
            for k in range(SSM_UNROLL):
                t = seq - 1 - (s * SSM_UNROLL + k)
                for e in range(nb):
                    a, s1, s2 = carry[e]
                    a = a * l1 - pltpu.roll(a, 4, 0) * l2 + ap[e][pl.ds(t, SUBLANES, stride=pitch), :]
                    ap[e][pl.ds(t, SUBLANES, stride=pitch), :] = a
                    hprev = hp[e][pl.ds(jnp.maximum(t - 1, 0), SUBLANES, stride=pitch), :] * jnp.where(t > 0, 1.0, 0.0)
                    carry[e] = [a, s1 + a * hprev, s2 + a * pltpu.roll(hprev, 4, 0)]
            return tuple(tuple(c) for c in carry)
        zero = jnp.zeros((SUBLANES, LANES), f32)
        fin = lax.fori_loop(0, seq // SSM_UNROLL, step, tuple((zero, zero, zero) for _ in range(nb)))
        dl_ref[0:SUBLANES, :] = sum(f[1] for f in fin)
        dl_ref[SUBLANES:2 * SUBLANES, :] = sum(f[2] for f in fin)

        for e in range(nb):
            def chunk2(rc, _, e=e):
                rows = _rows(e * seq, rc)
                ar = _ssm_rows(ap[e], rc, pitch)
                du_ref[rows, :] = (_dot(ar, b_ref[...], NT) + d_ref[...] * dy_ref[rows, :]).astype(bf16)
                db_ref[...] += _dot(u_ref[rows, :].astype(bf16), ar, TN)
                return 0
            lax.fori_loop(0, seq // SSM_ROWS, chunk2, 0)

    col = pl.BlockSpec((t_all, gw), lambda k: (0, k))
    per = lambda s: pl.BlockSpec((None,) + s[1:], lambda k: (k, 0, 0))
    ng = w // gw
    res, extra = _call(
        body, ins=[u, dy, b_big, c_big, lslab, dskip],
        out_shape=[SDS((t_all, w), bf16), SDS(b_big.shape, f32), SDS(c_big.shape, f32), SDS((ng, 2 * SUBLANES, LANES), f32),
                   SDS((SUBLANES, w), f32)],
        grid=(ng,),
        in_specs=[col, col, per(b_big.shape), per(c_big.shape), per(lslab.shape), pl.BlockSpec((1, gw), lambda k: (0, k))],
        out_specs=[col, per(b_big.shape), per(c_big.shape), per((ng, 2 * SUBLANES, LANES)), pl.BlockSpec((SUBLANES, gw), lambda k: (0, k))],
        scratch_shapes=[pltpu.VMEM((SSM_PLANES * pitch, LANES), f32) for _ in range(2 * nb)], name=name, comm=comm)
    return (*res, extra)


def _dense(col):
    return col.reshape(-1, LANES)


def _zoh(a, b, dtv):
    mag, ang = jnp.exp(a * dtv), b * dtv
    cs, sn = jnp.cos(ang), jnp.sin(ang)
    lr, li = mag * cs, mag * sn
    nr, den = lr - 1.0, a * a + b * b
    return lr, li, (nr * a + li * b) / den, (li * a - nr * b) / den, mag, cs, sn, nr, den


def _ssm_disc_fwd(lam_re, lam_im, dt, b_re, b_im, name):
    def states(a_ref, b_ref, dt_ref, lr_ref, li_ref, cr_ref, ci_ref):
        lr_ref[...], li_ref[...], cr_ref[...], ci_ref[...] = _zoh(a_ref[...], b_ref[...], dt_ref[...])[:4]

    def maps(cr_ref, ci_ref, br_ref, bi_ref, bbr_ref, bbi_ref):
        cr, ci = cr_ref[...], ci_ref[...]
        bbr_ref[...] = cr * br_ref[...] - ci * bi_ref[...]
        bbi_ref[...] = cr * bi_ref[...] + ci * br_ref[...]

    c, m = SDS(_dense(lam_re).shape, f32), SDS(b_re.shape, f32)
    lr, li, cr, ci = pl.pallas_call(states, out_shape=(c, c, c, c), name=name)(_dense(lam_re), _dense(lam_im), _dense(dt))
    cr, ci = cr.reshape(lam_re.shape), ci.reshape(lam_re.shape)
    bbr, bbi = pl.pallas_call(maps, out_shape=(m, m), name=name + "_maps")(cr, ci, b_re, b_im)
    return lr.reshape(lam_re.shape), li.reshape(lam_re.shape), cr, ci, bbr, bbi


def _ssm_disc_bwd(lam_re, lam_im, dt, coef_re, coef_im, b_re, b_im, g_lr, g_li, g_bbr, g_bbi, name):
    def maps(cr_ref, ci_ref, br_ref, bi_ref, gbr_ref, gbi_ref, dbr_ref, dbi_ref, dcr_ref, dci_ref):
        cr, ci = cr_ref[...], ci_ref[...]
        gbr, gbi, brv, biv = gbr_ref[...], gbi_ref[...], br_ref[...], bi_ref[...]
        dbr_ref[...] = cr * gbr + ci * gbi
        dbi_ref[...] = cr * gbi - ci * gbr
        dcr_ref[...] = jnp.sum(brv * gbr + biv * gbi, axis=1, keepdims=True)
        dci_ref[...] = jnp.sum(brv * gbi - biv * gbr, axis=1, keepdims=True)

    def states(a_ref, b_ref, dt_ref, glr_ref, gli_ref, dcr_ref, dci_ref, da_ref, db_ref, ddt_ref):
        a, b, dtv = a_ref[...], b_ref[...], dt_ref[...]
        lr, li, cr, ci, mag, cs, sn, nr, den = _zoh(a, b, dtv)
        dcr, dci = dcr_ref[...], dci_ref[...]
        dnum_r, dnum_i = dcr / den, dci / den
        dden = -(dcr * cr + dci * ci) / den
        dnr = dnum_r * a - dnum_i * b
        dli = gli_ref[...] + dnum_r * b + dnum_i * a
        dlr = glr_ref[...] + dnr
        dmag, dang = dlr * cs + dli * sn, dli * lr - dlr * li
        dadt = dmag * mag
        da_ref[...] = dnum_r * nr + dnum_i * li + dden * 2.0 * a + dadt * dtv
        db_ref[...] = dnum_r * li - dnum_i * nr + dden * 2.0 * b + dang * dtv
        ddt_ref[...] = dadt * a + dang * b

    col, m = SDS(lam_re.shape, f32), SDS(b_re.shape, f32)
    d_br, d_bi, dcr, dci = pl.pallas_call(maps, out_shape=(m, m, col, col), name=name + "_maps")(coef_re, coef_im, b_re, b_im, g_bbr, g_bbi)
    c = SDS(_dense(lam_re).shape, f32)
    d_a, d_b, d_dt = pl.pallas_call(states, out_shape=(c, c, c), name=name)(
        _dense(lam_re), _dense(lam_im), _dense(dt), _dense(g_lr), _dense(g_li), _dense(dcr), _dense(dci))
    return d_a.reshape(lam_re.shape), d_b.reshape(lam_re.shape), d_dt.reshape(lam_re.shape), d_br, d_bi


def _place():
    x, y, c = lax.axis_index("x"), lax.axis_index("y"), lax.axis_index("c")
    return x, y, c, 2 * x + y


def _half_axis(shape, ax):
    return 0 if shape[0] == 2 else (3 - ax)


def _sub(ref, axis, start, size):
    idx = [slice(None)] * len(ref.shape)
    idx[axis] = pl.ds(start, size)
    return ref.at[tuple(idx)]


def _region(ref, full_shape, ax, slot=None, half=None):
    if slot is not None:
        n = full_shape[ax] // N_CHIPS
        ref = _sub(ref, ax, slot * n, n)
    if half is not None:
        ha = _half_axis(full_shape, ax)
        n = full_shape[ha] // 2
        ref = _sub(ref, ha, half * n, n)
    return ref


def _halved(shape, axis):
    return tuple(s // 2 if a == axis else s for a, s in enumerate(shape))


class _Comm:
    def __init__(self, ins, out_shapes, aliases, scratch, start, finish):
        self.ins, self.out_shapes, self.aliases, self.scratch, self.start, self.finish = ins, out_shapes, aliases, scratch, start, finish


def _call(body, *, ins, in_specs, out_shape, out_specs, grid, scratch_shapes, name, comm=None):
    if comm is None:
        res = pl.pallas_call(body, out_shape=tuple(out_shape), grid=grid, in_specs=list(in_specs), out_specs=tuple(out_specs),
                             scratch_shapes=list(scratch_shapes), name=name, compiler_params=_params(len(grid)))(*ins)
        return list(res), []
    n_in, n_out, n_scr, c_in, c_out = len(ins), len(out_shape), len(scratch_shapes), len(comm.ins), len(comm.out_shapes)

    def fused(*refs):
        pos = [n_in, n_in + c_in, n_in + c_in + n_out, n_in + c_in + n_out + c_out, n_in + c_in + n_out + c_out + n_scr]
        in_refs, cin, out_refs, cout, scr, cscr = (refs[:pos[0]], refs[pos[0]:pos[1]], refs[pos[1]:pos[2]], refs[pos[2]:pos[3]],
                                                   refs[pos[3]:pos[4]], refs[pos[4]:])
        ids = [pl.program_id(a) for a in range(len(grid))]
        first, last = ids[0] == 0, ids[0] == grid[0] - 1
        for a in range(1, len(grid)):
            first, last = first & (ids[a] == 0), last & (ids[a] == grid[a] - 1)

        @pl.when(first)
        def _():
            comm.start(cin, cout, cscr)

        body(*in_refs, *out_refs, *scr)

        @pl.when(last)
        def _():
            comm.finish(cin, cout, cscr)

    res = pl.pallas_call(
        fused, out_shape=tuple(out_shape) + tuple(comm.out_shapes), grid=grid, in_specs=list(in_specs) + [ANY] * c_in,
        out_specs=tuple(out_specs) + tuple([ANY] * c_out), scratch_shapes=list(scratch_shapes) + list(comm.scratch),
        input_output_aliases={n_in + i: n_out + o for i, o in comm.aliases}, name=name, compiler_params=_params(len(grid)))(*ins, *comm.ins)
    return list(res[:n_out]), list(res[n_out:])


def _comm_only(comm, name):
    c_in, c_out = len(comm.ins), len(comm.out_shapes)

    def body(*refs):
        cin, cout, cscr = refs[:c_in], refs[c_in:c_in + c_out], refs[c_in + c_out:]
        comm.start(cin, cout, cscr)
        comm.finish(cin, cout, cscr)

    return pl.pallas_call(body, out_shape=tuple(comm.out_shapes), in_specs=[ANY] * c_in, out_specs=tuple([ANY] * c_out),
                          scratch_shapes=list(comm.scratch), input_output_aliases=dict(comm.aliases), name=name)(*comm.ins)


def _gather_plan(shards, axes):
    n = len(shards)
    fulls = [tuple(s * N_CHIPS if a == ax else s for a, s in enumerate(sh.shape)) for sh, ax in zip(shards, axes)]
    own = 6

    def copies(src, dst, scr, sends_only=False):
        send_sems, recv_sems = scr
        x, y, c, p = _place()
        chips = [(1 - x, y), (x, 1 - y), (1 - x, 1 - y)]
        slots = [2 * cx + cy for cx, cy in chips]

        def copy(a, k, slot, half, to, from_shard=False):
            where = _region(dst[a], fulls[a], axes[a], slot, half)
            source = where
            if from_shard:
                ha = _half_axis(fulls[a], axes[a])
                hn = fulls[a][ha] // 2
                source = _sub(src[a], ha, half * hn, hn)
            return pltpu.make_async_remote_copy(src_ref=source, dst_ref=where, send_sem=send_sems.at[a, k],
                                                recv_sem=recv_sems.at[a, k], device_id=to, device_id_type=MESH)

        parts = range(n)
        mine = [pltpu.make_async_remote_copy(src_ref=src[a], dst_ref=_region(dst[a], fulls[a], axes[a], p),
                                             send_sem=send_sems.at[a, own], recv_sem=recv_sems.at[a, own],
                                             device_id=(x, y, 1 - c), device_id_type=MESH) for a in parts]
        first = [copy(a, j, p, c, (*chips[j], c), True) for a in parts for j in range(3)]
        if sends_only:
            return mine, first
        landed = [copy(a, j, slots[j], c, (x, y, c)) for a in parts for j in range(3)]
        passed = [copy(a, 3 + j, slots[j], c, (x, y, 1 - c)) for a in parts for j in range(3)]
        handed = [copy(a, 3 + j, slots[j], 1 - c, (x, y, c)) for a in parts for j in range(3)]
        return mine, first, landed, passed, handed

    def start(src, dst, scr):
        mine, first = copies(src, dst, scr, sends_only=True)
        for cp in first + mine:
            cp.start()

    def finish(src, dst, scr):
        mine, first, landed, passed, handed = copies(src, dst, scr)
        for arrived, fwd in zip(landed, passed):
            arrived.wait_recv()
            fwd.start()
        for cp in handed + mine:
            cp.wait_recv()
        for cp in first + passed + mine:
            cp.wait_send()

    return _Comm(list(shards), [SDS(f, s.dtype) for f, s in zip(fulls, shards)], [],
                 [pltpu.SemaphoreType.DMA((n, 7)), pltpu.SemaphoreType.DMA((n, 7))], start, finish)


def _all_gather(shards, axes, name):
    return _comm_only(_gather_plan(shards, axes), name)


def _swap_halves(grads, axes, name):
    n = len(grads)
    shapes = [g.shape for g in grads]

    def body(*refs):
        src, dst = refs[:n], refs[n:2 * n]
        send_sems, recv_sems = refs[2 * n:]
        x, y, c, _ = _place()
        cps = [pltpu.make_async_remote_copy(src_ref=_region(src[a], shapes[a], axes[a], None, 1 - c), dst_ref=dst[a],
                                            send_sem=send_sems.at[a], recv_sem=recv_sems.at[a],
                                            device_id=(x, y, 1 - c), device_id_type=MESH) for a in range(n)]
        for cp in cps:
            cp.start()
        for cp in cps:
            cp.wait()

    outs = tuple(SDS(_halved(s, _half_axis(s, ax)), g.dtype) for s, ax, g in zip(shapes, axes, grads))
    return pl.pallas_call(body, out_shape=outs, in_specs=[ANY] * n, out_specs=tuple([ANY] * n),
                          scratch_shapes=[pltpu.SemaphoreType.DMA((n,)), pltpu.SemaphoreType.DMA((n,))], name=name)(*grads)


def _row_block(rows, row_bytes, limit=3 << 20):
    for b in (1024, 512, 256, 128, 64, 32, 16, 8):
        if rows % b == 0 and b * row_bytes <= limit:
            return b
    return rows


def _add_own_half(g, other, ax, cidx, name):
    _, kp, np_ = other.shape
    ha = _half_axis(g.shape, ax)
    ks, ns = (kp // N_CHIPS, np_) if ax == 1 else (kp, np_ // N_CHIPS)
    bk = _row_block(ks, ns * 4)
    nkb = ks // bk

    def g_map(q, i, cref):
        c = cref[0]
        if ax == 1:
            return (c, q * nkb + i, 0) if ha == 0 else (0, q * nkb + i, c)
        return (c, i, q) if ha == 0 else (0, c * nkb + i, q)

    def o_map(q, i, cref):
        return (0, q * nkb + i, 0) if ax == 1 else (0, i, q)

    def body(c_ref, g_ref, o_ref, send_ref, land_ref):
        del c_ref
        s = (g_ref[...].astype(f32) + o_ref[...].astype(f32)).astype(send_ref.dtype)
        send_ref[...] = s
        land_ref[...] = s

    out = pl.BlockSpec((None, bk, ns), lambda q, i, cref: (q, i, 0))
    grid_spec = pltpu.PrefetchScalarGridSpec(
        num_scalar_prefetch=1, grid=(N_CHIPS, nkb),
        in_specs=[pl.BlockSpec((None, bk, ns), g_map), pl.BlockSpec((None, bk, ns), o_map)], out_specs=(out, out))
    shape = SDS((N_CHIPS, ks, ns), g.dtype)
    return pl.pallas_call(body, out_shape=(shape, shape), grid_spec=grid_spec, name=name, compiler_params=_params(2))(cidx, g, other)


def _owner_plan(sends, lands):
    n = len(sends)

    def copies(cin, dst, scr, arrivals):
        src = cin[:n]
        send_sems, recv_sems = scr
        x, y, c, p = _place()
        chips = [(1 - x, y), (x, 1 - y), (1 - x, 1 - y)]
        slots = [2 * cx + cy for cx, cy in chips]
        if arrivals:
            return [pltpu.make_async_remote_copy(src_ref=src[a].at[p], dst_ref=dst[a].at[slots[j]], send_sem=send_sems.at[a, j],
                                                 recv_sem=recv_sems.at[a, j], device_id=(x, y, c), device_id_type=MESH)
                    for a in range(n) for j in range(3)]
        return [pltpu.make_async_remote_copy(src_ref=src[a].at[slots[j]], dst_ref=dst[a].at[p], send_sem=send_sems.at[a, j],
                                             recv_sem=recv_sems.at[a, j], device_id=(*chips[j], c), device_id_type=MESH)
                for a in range(n) for j in range(3)]

    def start(cin, dst, scr):
        for cp in copies(cin, dst, scr, False):
            cp.start()

    def finish(cin, dst, scr):
        for cp in copies(cin, dst, scr, True):
            cp.wait_recv()
        for cp in copies(cin, dst, scr, False):
            cp.wait_send()

    return _Comm(list(sends) + list(lands), [SDS(l.shape, l.dtype) for l in lands], [(n + a, a) for a in range(n)],
                 [pltpu.SemaphoreType.DMA((n, 3)), pltpu.SemaphoreType.DMA((n, 3))], start, finish)


def _sum_chips(stack, shard_shape, ax, cidx, name):
    _, ks, ns = stack.shape
    ha = _half_axis(shard_shape, ax)
    bk = _row_block(ks, ns * 4 * N_CHIPS)
    nkb = ks // bk

    def o_map(i, cref):
        c = cref[0]
        return (c, i, 0) if ha == 0 else ((0, c * nkb + i, 0) if ha == 1 else (0, i, c))

    def body(c_ref, s_ref, o_ref):
        del c_ref
        acc = s_ref[0].astype(f32)
        for q in range(1, N_CHIPS):
            acc = acc + s_ref[q].astype(f32)
        o_ref[...] = acc

    grid_spec = pltpu.PrefetchScalarGridSpec(
        num_scalar_prefetch=1, grid=(nkb,), in_specs=[pl.BlockSpec((N_CHIPS, bk, ns), lambda i, cref: (0, i, 0))],
        out_specs=pl.BlockSpec((None, bk, ns), o_map))
    return pl.pallas_call(body, out_shape=SDS(shard_shape, f32), grid_spec=grid_spec, name=name, compiler_params=_params(1))(cidx, stack)


def _join_halves(slices, axes, name):
    n = len(slices)

    def body(*refs):
        dst = refs[n:2 * n]
        send_sems, recv_sems = refs[2 * n:]
        x, y, c, _ = _place()

        def half(a, h):
            ha = _half_axis(slices[a].shape, axes[a])
            hn = slices[a].shape[ha] // 2
            return _sub(dst[a], ha, h * hn, hn)

        cps = [pltpu.make_async_remote_copy(src_ref=half(a, c), dst_ref=half(a, c), send_sem=send_sems.at[a], recv_sem=recv_sems.at[a],
                                            device_id=(x, y, 1 - c), device_id_type=MESH) for a in range(n)]
        for cp in cps:
            cp.start()
        for a in range(n):
            pltpu.make_async_remote_copy(src_ref=half(a, c), dst_ref=half(a, 1 - c), send_sem=send_sems.at[a], recv_sem=recv_sems.at[a],
                                         device_id=(x, y, c), device_id_type=MESH).wait_recv()
        for cp in cps:
            cp.wait_send()

    return pl.pallas_call(
        body, out_shape=tuple(SDS(s.shape, s.dtype) for s in slices), in_specs=[ANY] * n, out_specs=tuple([ANY] * n),
        scratch_shapes=[pltpu.SemaphoreType.DMA((n,)), pltpu.SemaphoreType.DMA((n,))],
        input_output_aliases={a: a for a in range(n)}, name=name)(*slices)


def _core_index():
    return jnp.reshape(lax.axis_index("c"), (1,)).astype(jnp.int32)


def _reduce_begin(grads, axes, tag):
    cidx = _core_index()
    others = _swap_halves(grads, axes, f"rs_swap_{tag}")
    pairs = [_add_own_half(g, o, ax, cidx, f"rs_add_{tag}_{a}") for a, (g, o, ax) in enumerate(zip(grads, others, axes))]
    return _owner_plan([s for s, _ in pairs], [l for _, l in pairs])


def _reduce_end(stacks, shapes, axes, tag):
    cidx = _core_index()
    shard_shapes = [tuple(s // N_CHIPS if i == ax else s for i, s in enumerate(sh)) for sh, ax in zip(shapes, axes)]
    slices = [_sum_chips(s, sh, ax, cidx, f"rs_sum_{tag}_{a}") for a, (s, sh, ax) in enumerate(zip(stacks, shard_shapes, axes))]
    return _join_halves(slices, axes, f"rs_join_{tag}")


SMALL_COLS = 256


def _pack(arrays, rows_multiple):
    flat = jnp.concatenate([a.reshape(-1).astype(f32) for a in arrays])
    rows = -(-flat.shape[0] // SMALL_COLS)
    rows = -(-rows // rows_multiple) * rows_multiple
    flat = jnp.pad(flat, (0, rows * SMALL_COLS - flat.shape[0]))
    return flat.reshape(1, rows, SMALL_COLS)


def _unpack(buf, shapes):
    flat, out, off = buf.reshape(-1), [], 0
    for s in shapes:
        n = math.prod(s)
        out.append(flat[off:off + n].reshape(s))
        off += n
    return out


def _block_diag_in(bb):
    g, p, c = bb.shape
    k = g // SSM_GB
    eye = jnp.eye(SSM_GB, dtype=bb.dtype)
    return jnp.einsum("kgpc,gh->kgchp", bb.reshape(k, SSM_GB, p, c), eye).reshape(k, SSM_GB * c, SSM_GB * p)


def _block_diag_out(cc):
    g, c, p = cc.shape
    k = g // SSM_GB
    eye = jnp.eye(SSM_GB, dtype=cc.dtype)
    return jnp.einsum("kgcp,gh->kgphc", cc.reshape(k, SSM_GB, c, p), eye).reshape(k, SSM_GB * p, SSM_GB * c)


def _diag_in(db, p, c):
    k = db.shape[0]
    return jnp.einsum("kgcgp->kgpc", db.reshape(k, SSM_GB, c, SSM_GB, p)).reshape(k * SSM_GB, p, c)


def _diag_out(dc, p, c):
    k = dc.shape[0]
    return jnp.einsum("kgpgc->kgcp", dc.reshape(k, SSM_GB, p, SSM_GB, c)).reshape(k * SSM_GB, c, p)


def _state_slab(v):
    g, p = v.shape
    return v.reshape(g // SSM_GB, SSM_GB * p // LANES, LANES)


BIG = ("ab_w_in", "ab_w_out", "ssm_w_in", "ssm_w_glu", "xa_w_q", "xa_w_kv", "xa_w_o", "ffn_w_up", "ffn_w_down")
BIG_AXIS = dict(ab_w_in=2, ab_w_out=1, ssm_w_in=1, ssm_w_glu=2, xa_w_q=1, xa_w_kv=2, xa_w_o=1, ffn_w_up=2, ffn_w_down=1)
SMALL_REPL = ("norm_mix", "norm_xattn", "norm_ffn", "norm_mem", "norm_final", "pool_w", "pool_scale", "ssm_lam_re", "ssm_lam_im",
              "ssm_log_dt", "ssm_b_re", "ssm_b_im", "ssm_c_re", "ssm_c_im", "ffn_conv_b")
SMALL_SHARDED = ("ssm_d", "ffn_conv_w")
FIRST_MIXER = ("ab_w_in", "ab_w_out")
WEIGHTS = ("norm_mix", "norm_xattn", "norm_ffn", "norm_mem", "norm_final", "ab_w_in", "pool_w", "pool_scale", "ab_w_out", "ssm_w_in",
           "ssm_lam_re", "ssm_lam_im", "ssm_log_dt", "ssm_b_re", "ssm_b_im", "ssm_c_re", "ssm_c_im", "ssm_d", "ssm_w_glu", "xa_w_q",
           "xa_w_kv", "xa_w_o", "ffn_w_up", "ffn_conv_w", "ffn_conv_b", "ffn_w_down")


class _Reducer:
    def __init__(self):
        self.done, self.groups = {}, 0

    def begin(self, keys, gw):
        self.groups += 1
        return _reduce_begin([gw[k] for k in keys], [BIG_AXIS.get(k[0], 1) for k in keys], f"g{self.groups}")

    def end(self, keys, gw, stacks):
        slices = _reduce_end(stacks, [gw[k].shape for k in keys], [BIG_AXIS.get(k[0], 1) for k in keys], f"g{self.groups}")
        self.done.update(zip(keys, slices))


def _local_step(xf, memf, tgt, w, wf, conv_w, ssm_d, seq, late_weights=None, reducer=None):
    d = xf.shape[1]
    depth = w["norm_mix"].shape[0]
    wf = dict(wf)
    late_weights = late_weights or {}
    sbw = wf["ab_w_in", 0].shape[2] // 4
    row = lambda a: a.reshape(1, -1)

    gs, ps = w["ssm_lam_re"].shape[1:]
    col = lambda a: a.reshape(gs * ps, 1)
    lam_re, lam_im = col(w["ssm_lam_re"][0]), col(w["ssm_lam_im"][0])
    dt = col(jnp.broadcast_to(jnp.exp(w["ssm_log_dt"][0])[:, None], (gs, ps)))
    b_re, b_im = w["ssm_b_re"][0].reshape(gs * ps, -1), w["ssm_b_im"][0].reshape(gs * ps, -1)
    lb_re, lb_im, coef_re, coef_im, bb_re, bb_im = _ssm_disc_fwd(lam_re, lam_im, dt, b_re, b_im, "ssm_disc")
    cgrp = b_re.shape[1]
    b_big = jnp.concatenate([_block_diag_in(bb_re.reshape(gs, ps, cgrp)), _block_diag_in(bb_im.reshape(gs, ps, cgrp))], axis=2).astype(bf16)
    c_big = jnp.concatenate([_block_diag_out(w["ssm_c_re"][0]), -_block_diag_out(w["ssm_c_im"][0])], axis=1).astype(bf16)
    lr_s, li_s = _state_slab(lb_re.reshape(gs, ps)), _state_slab(lb_im.reshape(gs, ps))
    lslab = jnp.concatenate([lr_s, lr_s, -li_s, li_s], axis=1)

    mem_n = _norm_fwd(memf, row(w["norm_mem"]), "norm_mem")
    kv = [None] * depth
    xs, saved = [xf], []
    cur = xf
    h_next = _norm_fwd(cur, row(w["norm_mix"][0]), "norm_mix0")
    for l in range(depth):
        sv = {}
        h = h_next
        sv["h"] = h
        if l % 2 == 0:
            qkv = _mm(h, wf["ab_w_in", 0], mode="nn", b_l=0, n=3 * sbw, out_dtype=bf16, name=f"qkv{l}")
            u = _mm(h, wf["ab_w_in", 0], mode="nn", b_l=0, b_n0=3 * sbw, n=sbw, out_dtype=f32, name=f"poolin{l}")
            plan, names = late_weights.get(f"sb_fwd{l}", (None, ()))
            mix, ltot, first, late = _sb_fwd(qkv, seq, f"sb_fwd{l}", comm=plan)
            wf.update(zip(names, late))
            pooled, mix = _pool_fwd(u, mix, w["pool_w"][0], w["pool_scale"], seq, f"pool_fwd{l}")
            sv.update(qkv=qkv, mix=mix, ltot=ltot, first=first, pooled=pooled)
            cur, hx = _mm(mix, wf["ab_w_out", 0], mode="nn", b_l=0, res=cur, out_dtype=f32, name=f"mixout{l}",
                          norm=("fwd", row(w["norm_xattn"][l])))
        else:
            us = _mm(h, wf["ssm_w_in", 0], mode="nn", b_l=0, out_dtype=f32, name=f"ssmin{l}")
            plan, names = late_weights.get(f"ssm_fwd{l}", (None, ()))
            ys, late = _ssm_fwd(us, b_big, c_big, lslab, ssm_d, seq, f"ssm_fwd{l}", comm=plan)
            wf.update(zip(names, late))
            gl = _gelu_fwd(ys, f"gelu{l}")
            glu = _mm(gl, wf["ssm_w_glu", 0], mode="nn", b_l=0, out_dtype=f32, name=f"glu{l}")
            sv.update(us=us, ys=ys, gl=gl, glu=glu)
            cur, hx = _glu_fwd(glu, cur, row(w["norm_xattn"][l]), f"glugate{l}")
        sv["x1"] = cur
        kv[l] = _mm(mem_n, wf["xa_w_kv", l], mode="nn", b_l=0, out_dtype=bf16, name=f"kv{l}")
        qx = _mm(hx, wf["xa_w_q", l], mode="nn", b_l=0, out_dtype=bf16, name=f"xaq{l}")
        ox = _xa_fwd(qx, kv[l], seq, f"xa_fwd{l}")
        cur, hf = _mm(ox, wf["xa_w_o", l], mode="nn", b_l=0, res=cur, out_dtype=f32, name=f"xao{l}",
                      norm=("fwd", row(w["norm_ffn"][l])))
        sv.update(hx=hx, qx=qx, ox=ox, x2=cur)
        up = _mm(hf, wf["ffn_w_up", l], mode="nn", b_l=0, out_dtype=bf16, name=f"ffnup{l}")
        act, cv = _ffn_gate_fwd(up, conv_w[l], row(w["ffn_conv_b"][l]), seq, f"ffn_gate{l}")
        if l + 1 < depth:
            cur, h_next = _mm(act, wf["ffn_w_down", l], mode="nn", b_l=0, res=cur, out_dtype=f32, name=f"ffndown{l}",
                              norm=("fwd", row(w["norm_mix"][l + 1])))
        else:
            cur = _mm(act, wf["ffn_w_down", l], mode="nn", b_l=0, res=cur, out_dtype=f32, name=f"ffndown{l}")
        sv.update(hf=hf, up=up, cv=cv, act=act)
        saved.append(sv)
        xs.append(cur)

    dx, g_final8, loss8 = _loss_head(cur, tgt, row(w["norm_final"]), "loss_head")

    gw = {}
    small = {"norm_final": jnp.sum(g_final8, axis=0)}
    g_mix, g_xa, g_ffn, g_cw, g_cb = [None] * depth, [None] * depth, [None] * depth, [None] * depth, [None] * depth
    dmem_n = None

    pending = []

    def wgrad(key, a, b, l, **kw):
        kw.setdefault("bk", 1024)
        gw[key, l] = _mm(a, b, mode="tn", out_dtype=bf16, out_l=0, out_layers=1, name=f"dw_{key}{l}", **kw)
        pending.append((key, l))

    def reduce_beside():
        if reducer is None or not pending:
            return None, []
        keys = list(pending)
        pending.clear()
        return reducer.begin(keys, gw), keys

    for l in reversed(range(depth)):
        sv = saved[l]
        dact = _mm(dx, wf["ffn_w_down", l], mode="nt", b_l=0, out_dtype=bf16, name=f"d_act{l}")
        wgrad("ffn_w_down", sv["act"], dx, l)
        dup, dcw8, dcb8 = _ffn_gate_bwd(dact, sv["up"], sv["cv"], conv_w[l], seq, f"ffn_gate_bwd{l}")
        g_cw[l], g_cb[l] = jnp.sum(dcw8, axis=1), jnp.sum(dcb8, axis=0)
        wgrad("ffn_w_up", sv["hf"], dup, l)
        dx, g8 = _mm(dup, wf["ffn_w_up", l], mode="nt", b_l=0, out_dtype=f32, name=f"d_hf{l}",
                     norm=("bwd", sv["x2"], dx, row(w["norm_ffn"][l])))
        g_ffn[l] = jnp.sum(g8, axis=0)
        dox = _mm(dx, wf["xa_w_o", l], mode="nt", b_l=0, out_dtype=bf16, name=f"d_ox{l}")
        wgrad("xa_w_o", sv["ox"], dx, l)
        dqx, dkv = _xa_bwd(sv["qx"], kv[l], dox, seq, f"xa_bwd{l}")
        wgrad("xa_w_kv", mem_n, dkv, l, bk=mem_n.shape[0])
        dmem_n = _mm(dkv, wf["xa_w_kv", l], mode="nt", b_l=0, res=dmem_n, out_dtype=f32, name=f"d_memn{l}")
        wgrad("xa_w_q", sv["hx"], dqx, l)
        dx, g8 = _mm(dqx, wf["xa_w_q", l], mode="nt", b_l=0, out_dtype=f32, name=f"d_hx{l}",
                     norm=("bwd", sv["x1"], dx, row(w["norm_xattn"][l])))
        g_xa[l] = jnp.sum(g8, axis=0)
        if l % 2 == 0:
            dmix = _mm(dx, wf["ab_w_out", 0], mode="nt", b_l=0, out_dtype=f32, name=f"d_mix{l}")
            comm, keys = reduce_beside()
            dq, dk, dv, stacks = _sb_bwd(sv["qkv"], sv["ltot"], sv["first"], dmix, seq, f"sb_bwd{l}", comm=comm)
            if comm is not None:
                reducer.end(keys, gw, stacks)
            wgrad("ab_w_out", sv["mix"], dx, 0)
            du, dpw, dps8 = _pool_bwd(dmix, sv["pooled"], w["pool_w"][0], w["pool_scale"], seq, f"pool_bwd{l}")
            small["pool_w"], small["pool_scale"] = dpw[None], jnp.sum(dps8, axis=0)[None]
            dproj = jnp.concatenate([dq, dk, dv, du], axis=1)
            wgrad("ab_w_in", sv["h"], dproj, 0)
            dx, g8 = _mm(dproj, wf["ab_w_in", 0], mode="nt", b_l=0, out_dtype=f32, name=f"d_h{l}",
                         norm=("bwd", xs[l], dx, row(w["norm_mix"][l])))
        else:
            dglu = _glu_bwd(dx, sv["glu"], f"glugate_bwd{l}")
            dgl = _mm(dglu, wf["ssm_w_glu", 0], mode="nt", b_l=0, out_dtype=f32, name=f"d_gelu{l}")
            dys = _gelu_bwd(dgl, sv["ys"], f"gelu_bwd{l}")
            comm, keys = reduce_beside()
            dus, db_big, dc_big, dl, dd8, stacks = _ssm_bwd(sv["us"], dys, b_big, c_big, lslab, ssm_d, seq, f"ssm_bwd{l}", comm=comm)
            if comm is not None:
                reducer.end(keys, gw, stacks)
            wgrad("ssm_w_glu", sv["gl"], dglu, 0)
            small["ssm_d"] = jnp.sum(dd8, axis=0)[None]
            half = SSM_PLANES // 2
            g_lr = (dl[:, 0:half] + dl[:, half:SUBLANES]).reshape(gs * ps, 1)
            g_li = (dl[:, SUBLANES + half:] - dl[:, SUBLANES:SUBLANES + half]).reshape(gs * ps, 1)
            g_bbr = _diag_in(db_big[:, :, :SSM_GB * ps], ps, cgrp).reshape(gs * ps, cgrp)
            g_bbi = _diag_in(db_big[:, :, SSM_GB * ps:], ps, cgrp).reshape(gs * ps, cgrp)
            d_a, d_b, d_dt, d_br, d_bi = _ssm_disc_bwd(lam_re, lam_im, dt, coef_re, coef_im, b_re, b_im, g_lr, g_li, g_bbr, g_bbi,
                                                       "ssm_disc_bwd")
            small["ssm_lam_re"], small["ssm_lam_im"] = d_a.reshape(1, gs, ps), d_b.reshape(1, gs, ps)
            small["ssm_log_dt"] = (jnp.sum(d_dt.reshape(gs, ps), axis=1) * dt.reshape(gs, ps)[:, 0])[None]
            small["ssm_b_re"], small["ssm_b_im"] = d_br.reshape(1, gs, ps, cgrp), d_bi.reshape(1, gs, ps, cgrp)
            small["ssm_c_re"] = _diag_out(dc_big[:, :SSM_GB * ps], ps, cgrp)[None]
            small["ssm_c_im"] = -_diag_out(dc_big[:, SSM_GB * ps:], ps, cgrp)[None]
            wgrad("ssm_w_in", sv["h"], dus, 0)
            dx, g8 = _mm(dus, wf["ssm_w_in", 0], mode="nt", b_l=0, out_dtype=f32, name=f"d_h{l}",
                         norm=("bwd", xs[l], dx, row(w["norm_mix"][l])))
        g_mix[l] = jnp.sum(g8, axis=0)

    small["norm_mem"] = jnp.sum(_norm_bwd_gain_only(dmem_n, memf, "norm_mem_bwd"), axis=0)
    small["norm_mix"], small["norm_xattn"], small["norm_ffn"] = jnp.stack(g_mix), jnp.stack(g_xa), jnp.stack(g_ffn)
    small["ffn_conv_w"], small["ffn_conv_b"] = jnp.stack(g_cw), jnp.stack(g_cb)
    return loss8, dx, gw, small, pending


def _step(x, mem, loss_target, w, m, v):
    nb, seq, d = x.shape
    t_all = nb * seq
    depth = w["norm_mix"].shape[0]
    chip = 2 * lax.axis_index("x") + lax.axis_index("y")

    small_mine = _pack([w[k] for k in SMALL_SHARDED], SUBLANES)
    gathered = _all_gather([w[k].astype(bf16) for k in FIRST_MIXER] + [small_mine], [BIG_AXIS[k] for k in FIRST_MIXER] + [1],
                           "gather_first")
    wf = {(k, 0): g for k, g in zip(FIRST_MIXER, gathered[:-1])}
    per_chip = gathered[-1].reshape(N_CHIPS, -1)
    pieces = [_unpack(per_chip[q], [w[k].shape for k in SMALL_SHARDED]) for q in range(N_CHIPS)]
    ssm_d = jnp.concatenate([pc[0] for pc in pieces], axis=-1)
    conv_w = jnp.concatenate([pc[1] for pc in pieces], axis=-1)
    ff2 = conv_w.shape[-1]
    late = [(k, l) for k in BIG if k not in FIRST_MIXER for l in range(w[k].shape[0])]
    groups = {"sb_fwd0": [kl for kl in late if kl[1] == 0], "ssm_fwd1": [kl for kl in late if kl[1] > 0]}
    late_weights = {hook: (_gather_plan([w[k][l:l + 1].astype(bf16) for k, l in keys], [BIG_AXIS[k] for k, _ in keys]), keys)
                    for hook, keys in groups.items()}

    reducer = _Reducer()
    loss8, dx, gw, small, pending = _local_step(x.reshape(t_all, d), mem.reshape(-1, d), loss_target.reshape(t_all, d), w, wf,
                                                conv_w, ssm_d, seq, late_weights=late_weights, reducer=reducer)
    loss = lax.psum(0.5 * jnp.sum(loss8) / d, ("x", "y", "c"))

    small_names = SMALL_REPL + SMALL_SHARDED
    small_full_shapes = [w[k].shape for k in SMALL_REPL] + [(1, d), (depth, 3, ff2)]
    gw["small", 0] = _pack([small[k] for k in small_names], 2 * N_CHIPS * SUBLANES)
    keys = pending + [("small", 0)]
    reducer.end(keys, gw, _comm_only(reducer.begin(keys, gw), "rs_owner_last"))
    g_big = {k: jnp.concatenate([reducer.done[k, l] for l in range(w[k].shape[0])], axis=0) for k in BIG}
    small_all = _all_gather([reducer.done["small", 0]], [1], "gather_small_grads")[0]
    g_small = dict(zip(small_names, _unpack(small_all, small_full_shapes)))
    g_small["ssm_d"] = lax.dynamic_slice_in_dim(g_small["ssm_d"], chip * (d // N_CHIPS), d // N_CHIPS, axis=1)
    g_small["ffn_conv_w"] = lax.dynamic_slice_in_dim(g_small["ffn_conv_w"], chip * (ff2 // N_CHIPS), ff2 // N_CHIPS, axis=2)
    grads = {**g_big, **g_small}

    delta, new_m, new_v = {}, {}, {}
    for k in BIG:
        n_cols = w[k].shape[-1]
        two = lambda a: a.reshape(-1, n_cols)
        dl_, m_, v_ = _adamw(two(w[k]), two(grads[k]), two(m[k]), two(v[k]), f"adamw_{k}")
        delta[k], new_m[k], new_v[k] = dl_.reshape(w[k].shape), m_.reshape(w[k].shape), v_.reshape(w[k].shape)
    pk = lambda tree: _pack([tree[k] for k in small_names], 256)[0]
    small_shapes = [w[k].shape for k in small_names]
    outs = _adamw(pk(w), pk(grads), pk(m), pk(v), "adamw_small")
    for tree, buf in zip((delta, new_m, new_v), outs):
        tree.update(zip(small_names, _unpack(buf, small_shapes)))

    grad_x = dx.reshape(nb, seq, d)
    return (loss, grad_x, *[grads[k] for k in WEIGHTS], *[delta[k] for k in WEIGHTS], *[new_m[k] for k in WEIGHTS],
            *[new_v[k] for k in WEIGHTS])


def kernel(x, mem, norm_mix, norm_xattn, norm_ffn, norm_mem, norm_final, ab_w_in, pool_w, pool_scale, ab_w_out, ssm_w_in, ssm_lam_re, ssm_lam_im, ssm_log_dt, ssm_b_re, ssm_b_im, ssm_c_re, ssm_c_im, ssm_d, ssm_w_glu, xa_w_q, xa_w_kv, xa_w_o, ffn_w_up, ffn_conv_w, ffn_conv_b, ffn_w_down, loss_target, m_norm_mix, m_norm_xattn, m_norm_ffn, m_norm_mem, m_norm_final, m_ab_w_in, m_pool_w, m_pool_scale, m_ab_w_out, m_ssm_w_in, m_ssm_lam_re, m_ssm_lam_im, m_ssm_log_dt, m_ssm_b_re, m_ssm_b_im, m_ssm_c_re, m_ssm_c_im, m_ssm_d, m_ssm_w_glu, m_xa_w_q, m_xa_w_kv, m_xa_w_o, m_ffn_w_up, m_ffn_conv_w, m_ffn_conv_b, m_ffn_w_down, v_norm_mix, v_norm_xattn, v_norm_ffn, v_norm_mem, v_norm_final, v_ab_w_in, v_pool_w, v_pool_scale, v_ab_w_out, v_ssm_w_in, v_ssm_lam_re, v_ssm_lam_im, v_ssm_log_dt, v_ssm_b_re, v_ssm_b_im, v_ssm_c_re, v_ssm_c_im, v_ssm_d, v_ssm_w_glu, v_xa_w_q, v_xa_w_kv, v_xa_w_o, v_ffn_w_up, v_ffn_conv_w, v_ffn_conv_b, v_ffn_w_down):
    args = dict(locals())
    w = {k: args[k] for k in WEIGHTS}
    m = {k: args["m_" + k] for k in WEIGHTS}
    v = {k: args["v_" + k] for k in WEIGHTS}
    return _step(x, mem, loss_target, w, m, v)
```
